```python
import jax, jax.numpy as jnp
from jax import lax
import numpy as np

D_MODEL = 2048
BATCH = 8
SEQ = 2048
DEPTH = 1

CHUNK = 64
N_META = 16
D_CONF = D_MODEL // 2
D_SHORT = D_MODEL // 2
CONF_KERNEL = 31
SHORT_KERNEL = 3
D_FF = 4 * D_MODEL
IN_COLS = 2 * D_CONF + 3 * D_SHORT + 2 * D_MODEL
RMS_EPS = 1e-6
LN_EPS = 1e-5

kernel_name = "hybrid_gated_conformer_shortconv_block"


def rms_norm(x, g):
    xf = x.astype(jnp.float32)
    y = xf * lax.rsqrt(jnp.mean(xf * xf, axis=-1, keepdims=True) + RMS_EPS)
    return (y * g.astype(jnp.float32)).astype(x.dtype)


def layer_norm(x, g, b):
    xf = x.astype(jnp.float32)
    mu = jnp.mean(xf, axis=-1, keepdims=True)
    var = jnp.mean(jnp.square(xf - mu), axis=-1, keepdims=True)
    y = (xf - mu) * lax.rsqrt(var + LN_EPS)
    return (y * g.astype(jnp.float32) + b.astype(jnp.float32)).astype(x.dtype)


def causal_depthwise_conv(x, w, b=None):
    k = w.shape[0]
    y = lax.conv_general_dilated(
        x, w[:, None, :].astype(x.dtype),
        window_strides=(1,),
        padding=[(k - 1, 0)],
        dimension_numbers=("NWC", "WIO", "NWC"),
        feature_group_count=x.shape[-1])
    if b is not None:
        y = y + b.astype(x.dtype)
    return y


def _fwd_setup_inputs(seed: int = 0) -> dict:
    key = jax.random.key(seed)
    ks = jax.random.split(key, 24)
    f32 = jnp.float32

    def nrm(k, shape, scale):
        return jax.random.normal(k, shape, f32) * scale

    def gain(k, shape):
        return 1.0 + 0.05 * jax.random.normal(k, shape, f32)

    return {
        "x": jax.random.normal(ks[0], (BATCH, SEQ, D_MODEL), f32),
        "meta": nrm(ks[1], (N_META, D_MODEL), 1.0),
        "g_pre_mix": gain(ks[2], (DEPTH, D_MODEL)),
        "w_in": nrm(ks[3], (DEPTH, D_MODEL, IN_COLS), D_MODEL ** -0.5),
        "b_gates": nrm(ks[4], (DEPTH, 2 * D_MODEL), 0.1),
        "conf_dw_w": nrm(ks[5], (DEPTH, CONF_KERNEL, D_CONF), CONF_KERNEL ** -0.5),
        "conf_dw_b": nrm(ks[6], (DEPTH, D_CONF), 0.02),
        "conf_ln_g": gain(ks[7], (DEPTH, D_CONF)),
        "conf_ln_b": nrm(ks[8], (DEPTH, D_CONF), 0.02),
        "conf_w_pw": nrm(ks[9], (DEPTH, D_CONF, D_MODEL), D_CONF ** -0.5),
        "short_dw_w": nrm(ks[10], (DEPTH, SHORT_KERNEL, D_SHORT), SHORT_KERNEL ** -0.5),
        "short_w_out": nrm(ks[11], (DEPTH, D_SHORT, D_MODEL), D_SHORT ** -0.5),
        "w_o": nrm(ks[12], (DEPTH, D_MODEL, D_MODEL), D_MODEL ** -0.5),
        "g_post_mix": gain(ks[13], (DEPTH, D_MODEL)),
        "g_pre_mlp": gain(ks[14], (DEPTH, D_MODEL)),
        "w_up": nrm(ks[15], (DEPTH, D_MODEL, D_FF), D_MODEL ** -0.5),
        "w_down": nrm(ks[16], (DEPTH, D_FF, D_MODEL), D_FF ** -0.5),
        "g_post_mlp": gain(ks[17], (DEPTH, D_MODEL)),
    }


def _fwd_reference(x, meta, g_pre_mix, w_in, b_gates, conf_dw_w, conf_dw_b, conf_ln_g,
              conf_ln_b, conf_w_pw, short_dw_w, short_w_out, w_o, g_post_mix,
              g_pre_mlp, w_up, w_down, g_post_mlp):
    bsz = x.shape[0]
    meta_b = jnp.broadcast_to(meta.astype(x.dtype)[None], (bsz, N_META, D_MODEL))
    h = jnp.concatenate([meta_b, x], axis=1)

    for l in range(DEPTH):
        n = rms_norm(h, g_pre_mix[l])
        proj = jnp.einsum("btd,dc->btc", n, w_in[l])
        o1 = 2 * D_CONF
        o2 = o1 + 3 * D_SHORT
        u_a = proj[..., :o1]
        u_b = proj[..., o1:o2]
        gates = jax.nn.sigmoid(proj[..., o2:] + b_gates[l])
        gate_a = gates[..., :D_MODEL]
        gate_b = gates[..., D_MODEL:]

        a_val, a_gate = jnp.split(u_a, 2, axis=-1)
        a = a_val * jax.nn.sigmoid(a_gate)
        a = causal_depthwise_conv(a, conf_dw_w[l], conf_dw_b[l])
        a = jax.nn.silu(layer_norm(a, conf_ln_g[l], conf_ln_b[l]))
        y_a = jnp.einsum("btc,cd->btd", a, conf_w_pw[l])

        b_g, c_g, v = jnp.split(u_b, 3, axis=-1)
        s = b_g * causal_depthwise_conv(c_g * v, short_dw_w[l])
        y_b = jnp.einsum("btc,cd->btd", s, short_w_out[l])

        m = gate_a * y_a + gate_b * y_b
        mix = jnp.einsum("btd,de->bte", m, w_o[l])
        h = h + rms_norm(mix, g_post_mix[l])

        n2 = rms_norm(h, g_pre_mlp[l])
        f = jnp.square(jax.nn.relu(jnp.einsum("btd,df->btf", n2, w_up[l])))
        f = jnp.einsum("btf,fd->btd", f, w_down[l])
        h = h + rms_norm(f, g_post_mlp[l])

    return h[:, N_META:, :]


import jax as _jax
import jax.numpy as _jnp

TWIN_FORMAT = 'train_step'
FWD_PARAMS = ['x', 'meta', 'g_pre_mix', 'w_in', 'b_gates', 'conf_dw_w', 'conf_dw_b', 'conf_ln_g', 'conf_ln_b', 'conf_w_pw', 'short_dw_w', 'short_w_out', 'w_o', 'g_post_mix', 'g_pre_mlp', 'w_up', 'w_down', 'g_post_mlp']
TWIN_WEIGHTS = ['meta', 'g_pre_mix', 'w_in', 'b_gates', 'conf_dw_w', 'conf_dw_b', 'conf_ln_g', 'conf_ln_b', 'conf_w_pw', 'short_dw_w', 'short_w_out', 'w_o', 'g_post_mix', 'g_pre_mlp', 'w_up', 'w_down', 'g_post_mlp']
TWIN_DIFF_INPUT = 'x'
TWIN_INPUTS = ['x', 'meta', 'g_pre_mix', 'w_in', 'b_gates', 'conf_dw_w', 'conf_dw_b', 'conf_ln_g', 'conf_ln_b', 'conf_w_pw', 'short_dw_w', 'short_w_out', 'w_o', 'g_post_mix', 'g_pre_mlp', 'w_up', 'w_down', 'g_post_mlp', 'loss_target', 'm_meta', 'm_g_pre_mix', 'm_w_in', 'm_b_gates', 'm_conf_dw_w', 'm_conf_dw_b', 'm_conf_ln_g', 'm_conf_ln_b', 'm_conf_w_pw', 'm_short_dw_w', 'm_short_w_out', 'm_w_o', 'm_g_post_mix', 'm_g_pre_mlp', 'm_w_up', 'm_w_down', 'm_g_post_mlp', 'v_meta', 'v_g_pre_mix', 'v_w_in', 'v_b_gates', 'v_conf_dw_w', 'v_conf_dw_b', 'v_conf_ln_g', 'v_conf_ln_b', 'v_conf_w_pw', 'v_short_dw_w', 'v_short_w_out', 'v_w_o', 'v_g_post_mix', 'v_g_pre_mlp', 'v_w_up', 'v_w_down', 'v_g_post_mlp']
TWIN_OUTPUTS = ['loss', 'grad_x', 'grad_meta', 'grad_g_pre_mix', 'grad_w_in', 'grad_b_gates', 'grad_conf_dw_w', 'grad_conf_dw_b', 'grad_conf_ln_g', 'grad_conf_ln_b', 'grad_conf_w_pw', 'grad_short_dw_w', 'grad_short_w_out', 'grad_w_o', 'grad_g_post_mix', 'grad_g_pre_mlp', 'grad_w_up', 'grad_w_down', 'grad_g_post_mlp', 'delta_meta', 'delta_g_pre_mix', 'delta_w_in', 'delta_b_gates', 'delta_conf_dw_w', 'delta_conf_dw_b', 'delta_conf_ln_g', 'delta_conf_ln_b', 'delta_conf_w_pw', 'delta_short_dw_w', 'delta_short_w_out', 'delta_w_o', 'delta_g_post_mix', 'delta_g_pre_mlp', 'delta_w_up', 'delta_w_down', 'delta_g_post_mlp', 'new_m_meta', 'new_m_g_pre_mix', 'new_m_w_in', 'new_m_b_gates', 'new_m_conf_dw_w', 'new_m_conf_dw_b', 'new_m_conf_ln_g', 'new_m_conf_ln_b', 'new_m_conf_w_pw', 'new_m_short_dw_w', 'new_m_short_w_out', 'new_m_w_o', 'new_m_g_post_mix', 'new_m_g_pre_mlp', 'new_m_w_up', 'new_m_w_down', 'new_m_g_post_mlp', 'new_v_meta', 'new_v_g_pre_mix', 'new_v_w_in', 'new_v_b_gates', 'new_v_conf_dw_w', 'new_v_conf_dw_b', 'new_v_conf_ln_g', 'new_v_conf_ln_b', 'new_v_conf_w_pw', 'new_v_short_dw_w', 'new_v_short_w_out', 'new_v_w_o', 'new_v_g_post_mix', 'new_v_g_pre_mlp', 'new_v_w_up', 'new_v_w_down', 'new_v_g_post_mlp']
TWIN_LEAF_KINDS = {'loss': 'loss', 'grad_x': 'grad_x', 'grad_meta': 'grad_w', 'grad_g_pre_mix': 'grad_w', 'grad_w_in': 'grad_w', 'grad_b_gates': 'grad_w', 'grad_conf_dw_w': 'grad_w', 'grad_conf_dw_b': 'grad_w', 'grad_conf_ln_g': 'grad_w', 'grad_conf_ln_b': 'grad_w', 'grad_conf_w_pw': 'grad_w', 'grad_short_dw_w': 'grad_w', 'grad_short_w_out': 'grad_w', 'grad_w_o': 'grad_w', 'grad_g_post_mix': 'grad_w', 'grad_g_pre_mlp': 'grad_w', 'grad_w_up': 'grad_w', 'grad_w_down': 'grad_w', 'grad_g_post_mlp': 'grad_w', 'delta_meta': 'delta_w', 'delta_g_pre_mix': 'delta_w', 'delta_w_in': 'delta_w', 'delta_b_gates': 'delta_w', 'delta_conf_dw_w': 'delta_w', 'delta_conf_dw_b': 'delta_w', 'delta_conf_ln_g': 'delta_w', 'delta_conf_ln_b': 'delta_w', 'delta_conf_w_pw': 'delta_w', 'delta_short_dw_w': 'delta_w', 'delta_short_w_out': 'delta_w', 'delta_w_o': 'delta_w', 'delta_g_post_mix': 'delta_w', 'delta_g_pre_mlp': 'delta_w', 'delta_w_up': 'delta_w', 'delta_w_down': 'delta_w', 'delta_g_post_mlp': 'delta_w', 'new_m_meta': 'new_m', 'new_m_g_pre_mix': 'new_m', 'new_m_w_in': 'new_m', 'new_m_b_gates': 'new_m', 'new_m_conf_dw_w': 'new_m', 'new_m_conf_dw_b': 'new_m', 'new_m_conf_ln_g': 'new_m', 'new_m_conf_ln_b': 'new_m', 'new_m_conf_w_pw': 'new_m', 'new_m_short_dw_w': 'new_m', 'new_m_short_w_out': 'new_m', 'new_m_w_o': 'new_m', 'new_m_g_post_mix': 'new_m', 'new_m_g_pre_mlp': 'new_m', 'new_m_w_up': 'new_m', 'new_m_w_down': 'new_m', 'new_m_g_post_mlp': 'new_m', 'new_v_meta': 'new_v', 'new_v_g_pre_mix': 'new_v', 'new_v_w_in': 'new_v', 'new_v_b_gates': 'new_v', 'new_v_conf_dw_w': 'new_v', 'new_v_conf_dw_b': 'new_v', 'new_v_conf_ln_g': 'new_v', 'new_v_conf_ln_b': 'new_v', 'new_v_conf_w_pw': 'new_v', 'new_v_short_dw_w': 'new_v', 'new_v_short_w_out': 'new_v', 'new_v_w_o': 'new_v', 'new_v_g_post_mix': 'new_v', 'new_v_g_pre_mlp': 'new_v', 'new_v_w_up': 'new_v', 'new_v_w_down': 'new_v', 'new_v_g_post_mlp': 'new_v'}


def _forward(args):
    return _fwd_reference(*[args[k] for k in FWD_PARAMS])


def _output_shape():
    out = _jax.eval_shape(lambda: _forward(_fwd_setup_inputs(0)))
    return out.shape, out.dtype

N_MICROBATCH = 1
ADAM_LR = 0.001
ADAM_B1 = 0.9
ADAM_B2 = 0.999
ADAM_EPS = 1e-08
ADAM_WD = 0.01
ADAM_STEP = 10
PER_EXAMPLE_BATCH_AXIS = {'x': 0, 'loss_target': 0}
SHARED_INPUTS = []
_WEIGHT_DTYPES = {'meta': _jnp.float32, 'g_pre_mix': _jnp.float32, 'w_in': _jnp.float32, 'b_gates': _jnp.float32, 'conf_dw_w': _jnp.float32, 'conf_dw_b': _jnp.float32, 'conf_ln_g': _jnp.float32, 'conf_ln_b': _jnp.float32, 'conf_w_pw': _jnp.float32, 'short_dw_w': _jnp.float32, 'short_w_out': _jnp.float32, 'w_o': _jnp.float32, 'g_post_mix': _jnp.float32, 'g_pre_mlp': _jnp.float32, 'w_up': _jnp.float32, 'w_down': _jnp.float32, 'g_post_mlp': _jnp.float32}
MOMENT_SCALE = {'meta': 5.653309e-03, 'g_pre_mix': 2.263104e-01, 'w_in': 1.094097e-01, 'b_gates': 1.610179e-01, 'conf_dw_w': 2.113892e-01, 'conf_dw_b': 3.580702e+00, 'conf_ln_g': 1.303354e+00, 'conf_ln_b': 1.945057e+00, 'conf_w_pw': 5.652397e-01, 'short_dw_w': 1.733086e-01, 'short_w_out': 1.528146e-01, 'w_o': 5.363187e-01, 'g_post_mix': 8.077800e+00, 'g_pre_mlp': 2.149171e-01, 'w_up': 1.090265e-01, 'w_down': 5.709094e-01, 'g_post_mlp': 8.262614e+00}


def _to_microbatches(a, axis):
    t = _jnp.moveaxis(a, axis, 0)
    t = t.reshape((N_MICROBATCH, t.shape[0] // N_MICROBATCH) + t.shape[1:])
    return _jnp.moveaxis(t, 1, axis + 1)


def setup_inputs(seed: int = 0) -> dict:
    inp = _fwd_setup_inputs(seed)
    key = _jax.random.fold_in(_jax.random.key(seed), 7919)
    shape, _ = _output_shape()
    out = dict(inp)
    out["loss_target"] = _jax.random.normal(_jax.random.fold_in(key, 0), shape, _jnp.float32)
    for i, name in enumerate(TWIN_WEIGHTS):
        w = inp[name].astype(_jnp.float32)
        if MOMENT_SCALE is None:
            s = _jnp.sqrt(_jnp.mean(_jnp.square(w)) + 1e-30)
        else:
            s = MOMENT_SCALE[name]
        km, kv = _jax.random.split(_jax.random.fold_in(key, i + 1))
        out[name] = w
        out["m_" + name] = s * _jax.random.normal(km, w.shape, _jnp.float32)
        out["v_" + name] = (s * s) * _jax.random.uniform(kv, w.shape, _jnp.float32, 0.5, 1.5)
    if N_MICROBATCH > 1:
        for name, axis in PER_EXAMPLE_BATCH_AXIS.items():
            out[name] = _to_microbatches(out[name], axis)
    return {'x': out['x'], 'meta': out['meta'], 'g_pre_mix': out['g_pre_mix'], 'w_in': out['w_in'], 'b_gates': out['b_gates'], 'conf_dw_w': out['conf_dw_w'], 'conf_dw_b': out['conf_dw_b'], 'conf_ln_g': out['conf_ln_g'], 'conf_ln_b': out['conf_ln_b'], 'conf_w_pw': out['conf_w_pw'], 'short_dw_w': out['short_dw_w'], 'short_w_out': out['short_w_out'], 'w_o': out['w_o'], 'g_post_mix': out['g_post_mix'], 'g_pre_mlp': out['g_pre_mlp'], 'w_up': out['w_up'], 'w_down': out['w_down'], 'g_post_mlp': out['g_post_mlp'], 'loss_target': out['loss_target'], 'm_meta': out['m_meta'], 'm_g_pre_mix': out['m_g_pre_mix'], 'm_w_in': out['m_w_in'], 'm_b_gates': out['m_b_gates'], 'm_conf_dw_w': out['m_conf_dw_w'], 'm_conf_dw_b': out['m_conf_dw_b'], 'm_conf_ln_g': out['m_conf_ln_g'], 'm_conf_ln_b': out['m_conf_ln_b'], 'm_conf_w_pw': out['m_conf_w_pw'], 'm_short_dw_w': out['m_short_dw_w'], 'm_short_w_out': out['m_short_w_out'], 'm_w_o': out['m_w_o'], 'm_g_post_mix': out['m_g_post_mix'], 'm_g_pre_mlp': out['m_g_pre_mlp'], 'm_w_up': out['m_w_up'], 'm_w_down': out['m_w_down'], 'm_g_post_mlp': out['m_g_post_mlp'], 'v_meta': out['v_meta'], 'v_g_pre_mix': out['v_g_pre_mix'], 'v_w_in': out['v_w_in'], 'v_b_gates': out['v_b_gates'], 'v_conf_dw_w': out['v_conf_dw_w'], 'v_conf_dw_b': out['v_conf_dw_b'], 'v_conf_ln_g': out['v_conf_ln_g'], 'v_conf_ln_b': out['v_conf_ln_b'], 'v_conf_w_pw': out['v_conf_w_pw'], 'v_short_dw_w': out['v_short_dw_w'], 'v_short_w_out': out['v_short_w_out'], 'v_w_o': out['v_w_o'], 'v_g_post_mix': out['v_g_post_mix'], 'v_g_pre_mlp': out['v_g_pre_mlp'], 'v_w_up': out['v_w_up'], 'v_w_down': out['v_w_down'], 'v_g_post_mlp': out['v_g_post_mlp']}


def _loss(weights, diff, rest, loss_target):
    with _jax.named_scope("forward"):
        args = {**rest, TWIN_DIFF_INPUT: diff, **{k: w.astype(_WEIGHT_DTYPES[k]) for k, w in weights.items()}}
        y = _forward(args)
    with _jax.named_scope("loss_head"):
        err = _jnp.square(y.astype(_jnp.float32) - loss_target)
        return 0.5 * _jnp.sum(_jnp.mean(err, axis=-1)) if err.ndim else 0.5 * err


def _adamw(w, g, m, v):
    m = ADAM_B1 * m + (1.0 - ADAM_B1) * g
    v = ADAM_B2 * v + (1.0 - ADAM_B2) * _jnp.square(g)
    m_hat = m / (1.0 - ADAM_B1 ** ADAM_STEP)
    v_hat = v / (1.0 - ADAM_B2 ** ADAM_STEP)
    delta = -ADAM_LR * (m_hat / (_jnp.sqrt(v_hat) + ADAM_EPS) + ADAM_WD * w)
    return delta, m, v


def reference(x, meta, g_pre_mix, w_in, b_gates, conf_dw_w, conf_dw_b, conf_ln_g, conf_ln_b, conf_w_pw, short_dw_w, short_w_out, w_o, g_post_mix, g_pre_mlp, w_up, w_down, g_post_mlp, loss_target, m_meta, m_g_pre_mix, m_w_in, m_b_gates, m_conf_dw_w, m_conf_dw_b, m_conf_ln_g, m_conf_ln_b, m_conf_w_pw, m_short_dw_w, m_short_w_out, m_w_o, m_g_post_mix, m_g_pre_mlp, m_w_up, m_w_down, m_g_post_mlp, v_meta, v_g_pre_mix, v_w_in, v_b_gates, v_conf_dw_w, v_conf_dw_b, v_conf_ln_g, v_conf_ln_b, v_conf_w_pw, v_short_dw_w, v_short_w_out, v_w_o, v_g_post_mix, v_g_pre_mlp, v_w_up, v_w_down, v_g_post_mlp):
    given = dict(x=x, meta=meta, g_pre_mix=g_pre_mix, w_in=w_in, b_gates=b_gates, conf_dw_w=conf_dw_w, conf_dw_b=conf_dw_b, conf_ln_g=conf_ln_g, conf_ln_b=conf_ln_b, conf_w_pw=conf_w_pw, short_dw_w=short_dw_w, short_w_out=short_w_out, w_o=w_o, g_post_mix=g_post_mix, g_pre_mlp=g_pre_mlp, w_up=w_up, w_down=w_down, g_post_mlp=g_post_mlp, loss_target=loss_target, m_meta=m_meta, m_g_pre_mix=m_g_pre_mix, m_w_in=m_w_in, m_b_gates=m_b_gates, m_conf_dw_w=m_conf_dw_w, m_conf_dw_b=m_conf_dw_b, m_conf_ln_g=m_conf_ln_g, m_conf_ln_b=m_conf_ln_b, m_conf_w_pw=m_conf_w_pw, m_short_dw_w=m_short_dw_w, m_short_w_out=m_short_w_out, m_w_o=m_w_o, m_g_post_mix=m_g_post_mix, m_g_pre_mlp=m_g_pre_mlp, m_w_up=m_w_up, m_w_down=m_w_down, m_g_post_mlp=m_g_post_mlp, v_meta=v_meta, v_g_pre_mix=v_g_pre_mix, v_w_in=v_w_in, v_b_gates=v_b_gates, v_conf_dw_w=v_conf_dw_w, v_conf_dw_b=v_conf_dw_b, v_conf_ln_g=v_conf_ln_g, v_conf_ln_b=v_conf_ln_b, v_conf_w_pw=v_conf_w_pw, v_short_dw_w=v_short_dw_w, v_short_w_out=v_short_w_out, v_w_o=v_w_o, v_g_post_mix=v_g_post_mix, v_g_pre_mlp=v_g_pre_mlp, v_w_up=v_w_up, v_w_down=v_w_down, v_g_post_mlp=v_g_post_mlp)
    weights = {n: given[n] for n in TWIN_WEIGHTS}
    shared = {n: given[n] for n in SHARED_INPUTS}
    per_example = {n: given[n] for n in ['x']}
    grad_fn = _jax.value_and_grad(_loss, argnums=(0, 1))

    def one_microbatch(ex, loss_target):
        ex = dict(ex)
        diff = ex.pop(TWIN_DIFF_INPUT)
        return grad_fn(weights, diff, {**shared, **ex}, loss_target)

    if N_MICROBATCH == 1:
        loss, (grad_w, grad_x) = one_microbatch(per_example, given["loss_target"])
    else:
        def body(carry, xs):
            loss_sum, grad_sum = carry
            l_k, (gw_k, gx_k) = one_microbatch(xs[0], xs[1])
            with _jax.named_scope("update"):
                return (loss_sum + l_k, _jax.tree.map(_jnp.add, grad_sum, gw_k)), gx_k

        init = (_jnp.zeros((), _jnp.float32), _jax.tree.map(_jnp.zeros_like, weights))
        (loss, grad_w), grad_x = _jax.lax.scan(body, init, (per_example, given["loss_target"]))
    with _jax.named_scope("update"):
        delta_w, new_m, new_v = {}, {}, {}
        for n in TWIN_WEIGHTS:
            delta_w[n], new_m[n], new_v[n] = _adamw(weights[n], grad_w[n], given["m_" + n], given["v_" + n])
    return (loss, grad_x, *[grad_w[n] for n in TWIN_WEIGHTS], *[delta_w[n] for n in TWIN_WEIGHTS],
            *[new_m[n] for n in TWIN_WEIGHTS], *[new_v[n] for n in TWIN_WEIGHTS])
```

```python
import functools

import jax
import jax.numpy as jnp
from jax import lax
from jax.experimental import pallas as pl
from jax.experimental.pallas import tpu as pltpu

F32 = jnp.float32
BF16 = jnp.bfloat16
MXU_DTYPE = BF16
WIRE_DTYPE = BF16

N_META = 16
CONF_KERNEL = 31
SHORT_KERNEL = 3
CONV_PAD = 32
RMS_EPS = 1e-6
LN_EPS = 1e-5
ADAM_LR = 0.001
ADAM_B1 = 0.9
ADAM_B2 = 0.999
ADAM_EPS = 1e-08
ADAM_WD = 0.01
ADAM_STEP = 10

N_CHIPS = 4
MESH = pl.DeviceIdType.MESH
LANES = 128


def _sigmoid(z):
    return 1.0 / (1.0 + jnp.exp(-z))


def _matmul(name, a, b, *, kind, tm, tn, tk, out_dtypes, out_pieces=1, epilogue=None, extras=()):
    pieces = b.shape[0] if b.ndim == 3 else 1
    if kind == "nn":
        m, kdim = a.shape
        n = b.shape[-1] * pieces
        dims = (((1,), (0,)), ((), ()))
        a_spec = pl.BlockSpec((tm, tk), lambda i, j, k: (i, k))
        if b.ndim == 2:
            b_spec = pl.BlockSpec((tk, tn), lambda i, j, k: (k, j))
        else:
            npp = b.shape[-1] // tn
            b_spec = pl.BlockSpec((None, tk, tn), lambda i, j, k: (j // npp, k, j % npp))
    elif kind == "nt":
        m, kdim = a.shape
        n = b.shape[-2]
        dims = (((1,), (1,)), ((), ()))
        a_spec = pl.BlockSpec((tm, tk), lambda i, j, k: (i, k))
        if b.ndim == 2:
            b_spec = pl.BlockSpec((tn, tk), lambda i, j, k: (j, k))
        else:
            kpp = b.shape[-1] // tk
            b_spec = pl.BlockSpec((None, tn, tk), lambda i, j, k: (k // kpp, j, k % kpp))
    else:
        kdim, m = a.shape
        n = b.shape[-1]
        dims = (((0,), (0,)), ((), ()))
        a_spec = pl.BlockSpec((tk, tm), lambda i, j, k: (k, i))
        b_spec = pl.BlockSpec((tk, tn), lambda i, j, k: (k, j))
    assert m % tm == 0 and n % tn == 0 and kdim % tk == 0, (name, m, n, kdim, tm, tn, tk)
    nk = kdim // tk
    if out_pieces == 1:
        out_shape = (m, n)
        out_spec = pl.BlockSpec((tm, tn), lambda i, j, k: (i, j))
    else:
        onpp = n // out_pieces // tn
        out_shape = (out_pieces, m, n // out_pieces)
        out_spec = pl.BlockSpec((None, tm, tn), lambda i, j, k: (j // onpp, i, j % onpp))
    n_ex, n_out = len(extras), len(out_dtypes)
    if epilogue is None:
        epilogue = lambda acc: (acc,)

    def body(a_ref, b_ref, *rest):
        ex_refs, o_refs = rest[:n_ex], rest[n_ex:n_ex + n_out]
        prod = lax.dot_general(a_ref[...], b_ref[...], dims, preferred_element_type=F32)

        def finish(acc):
            tiles = epilogue(acc, *[r[...] for r in ex_refs])
            for o_ref, t in zip(o_refs, tiles):
                o_ref[...] = t.astype(o_ref.dtype)

        if nk == 1:
            finish(prod)
        else:
            acc_ref = rest[n_ex + n_out]
            k = pl.program_id(2)

            @pl.when(k == 0)
            def _():
                acc_ref[...] = prod

            @pl.when(jnp.logical_and(k > 0, k < nk - 1))
            def _():
                acc_ref[...] += prod

            @pl.when(k == nk - 1)
            def _():
                finish(acc_ref[...] + prod)

    ex_specs = [pl.BlockSpec((tm, tn), lambda i, j, k: (i, j)) for _ in extras]
    res = pl.pallas_call(
        body,
        name=name,
        grid=(m // tm, n // tn, nk),
        in_specs=[a_spec, b_spec, *ex_specs],
        out_specs=[out_spec] * n_out,
        out_shape=[jax.ShapeDtypeStruct(out_shape, d) for d in out_dtypes],
        scratch_shapes=[pltpu.VMEM((tm, tn), F32)] if nk > 1 else [],
        compiler_params=pltpu.CompilerParams(dimension_semantics=("parallel", "parallel", "arbitrary")),
    )(a, b, *extras)
    return res[0] if n_out == 1 else res


def _rowwise(name, fn, rows, vecs, outs, sums, tr):
    t = rows[0][0].shape[0]
    assert t % tr == 0
    n_r, n_v, n_o, n_s = len(rows), len(vecs), len(outs), len(sums)

    def body(*refs):
        r_in, v_in = refs[:n_r], refs[n_r:n_r + n_v]
        o_refs = refs[n_r + n_v:n_r + n_v + n_o]
        s_refs = refs[n_r + n_v + n_o:]
        i = pl.program_id(0)
        o_tiles, s_tiles = fn(i, [r[...] for r in r_in], [v[...] for v in v_in])
        for o_ref, tile in zip(o_refs, o_tiles):
            o_ref[...] = tile.astype(o_ref.dtype)
        for s_ref, tile in zip(s_refs, s_tiles):
            part = jnp.sum(tile, axis=0, keepdims=True)

            @pl.when(i == 0)
            def _(s_ref=s_ref, part=part):
                s_ref[...] = part

            @pl.when(i > 0)
            def _(s_ref=s_ref, part=part):
                s_ref[...] += part

    def row_spec(width, blk):
        return pl.BlockSpec((tr, width), lambda i: (i, blk))

    res = pl.pallas_call(
        body,
        name=name,
        grid=(t // tr,),
        in_specs=[row_spec(w, blk) for _, w, blk in rows]
        + [pl.BlockSpec(v.shape, lambda i: (0, 0)) for v in vecs],
        out_specs=[pl.BlockSpec((tr, c), lambda i: (i, 0)) for c, _ in outs]
        + [pl.BlockSpec((1, c), lambda i: (0, 0)) for c in sums],
        out_shape=[jax.ShapeDtypeStruct((t, c), d) for c, d in outs]
        + [jax.ShapeDtypeStruct((1, c), F32) for c in sums],
        compiler_params=pltpu.CompilerParams(dimension_semantics=("arbitrary",)),
    )(*[r[0] for r in rows], *vecs)
    return res


def _rms_fwd(x, g):
    r = lax.rsqrt(jnp.mean(x * x, axis=-1, keepdims=True) + RMS_EPS)
    return x * r * g


def _rms_bwd(x, g, dy):
    r = lax.rsqrt(jnp.mean(x * x, axis=-1, keepdims=True) + RMS_EPS)
    xn = x * r
    dxn = dy * g
    dx = r * (dxn - xn * jnp.mean(dxn * xn, axis=-1, keepdims=True))
    return dx, dy * xn


CONV_ROWS = 48
CONV_LANES = 128


def _conv_fwd(proj, wdw, bdw, w3, d_conf):
    t = proj.shape[0]
    cl = CONV_LANES
    nb = d_conf // cl
    nchunk = t // CONV_ROWS
    assert t % CONV_ROWS == 0

    def body(av_ref, ag_ref, cg_ref, v_ref, wdw_ref, bdw_ref, w3_ref, ac_ref, c3_ref, apad, cpad):
        zeros = jnp.zeros((CONV_PAD, cl), F32)
        apad[0:CONV_PAD, :] = zeros
        cpad[0:CONV_PAD, :] = zeros
        apad[CONV_PAD:, :] = av_ref[...] * _sigmoid(ag_ref[...])
        cpad[CONV_PAD:, :] = cg_ref[...] * v_ref[...]

        def chunk(ci, carry):
            base = pl.multiple_of(ci * CONV_ROWS, 8)
            acc = jnp.zeros((CONV_ROWS, cl), F32) + bdw_ref[...]
            for k in range(CONF_KERNEL):
                off = CONV_PAD - (CONF_KERNEL - 1) + k
                acc = acc + apad[pl.ds(base + off, CONV_ROWS), :] * wdw_ref[k:k + 1, :]
            ac_ref[pl.ds(base, CONV_ROWS), :] = acc
            acc3 = jnp.zeros((CONV_ROWS, cl), F32)
            for k in range(SHORT_KERNEL):
                off = CONV_PAD - (SHORT_KERNEL - 1) + k
                acc3 = acc3 + cpad[pl.ds(base + off, CONV_ROWS), :] * w3_ref[k:k + 1, :]
            c3_ref[pl.ds(base, CONV_ROWS), :] = acc3
            return carry

        lax.fori_loop(0, nchunk, chunk, 0)

    def col(blk0):
        return pl.BlockSpec((t, cl), lambda j: (0, blk0 + j))

    return pl.pallas_call(
        body,
        name="conv_fwd",
        grid=(nb,),
        in_specs=[col(0), col(nb), col(3 * nb), col(4 * nb),
                  pl.BlockSpec((CONF_KERNEL, cl), lambda j: (0, j)),
                  pl.BlockSpec((1, cl), lambda j: (0, j)),
                  pl.BlockSpec((SHORT_KERNEL, cl), lambda j: (0, j))],
        out_specs=[pl.BlockSpec((t, cl), lambda j: (0, j))] * 2,
        out_shape=[jax.ShapeDtypeStruct((t, d_conf), F32)] * 2,
        scratch_shapes=[pltpu.VMEM((t + CONV_PAD, cl), F32)] * 2,
        compiler_params=pltpu.CompilerParams(dimension_semantics=("parallel",)),
    )(proj, proj, proj, proj, wdw, bdw, w3)


def _conv_bwd(proj, d_ac, d_c3, wdw, w3, d_conf):
    t = proj.shape[0]
    cl = CONV_LANES
    nb = d_conf // cl
    nchunk = t // CONV_ROWS
    nsub = CONV_ROWS // 8

    def fold(p):
        r = p[0:8]
        for s in range(1, nsub):
            r = r + p[8 * s:8 * s + 8]
        return r

    def body(av_ref, ag_ref, cg_ref, v_ref, dac_ref, dc3_ref, wdw_ref, w3_ref,
             dav_ref, dag_ref, dcg_ref, dv_ref, dwdw_ref, dw3_ref, apad, cpad, dapad, dcpad):
        zeros = jnp.zeros((CONV_PAD, cl), F32)
        apad[0:CONV_PAD, :] = zeros
        cpad[0:CONV_PAD, :] = zeros
        apad[CONV_PAD:, :] = av_ref[...] * _sigmoid(ag_ref[...])
        cpad[CONV_PAD:, :] = cg_ref[...] * v_ref[...]
        dapad[0:t, :] = dac_ref[...]
        dcpad[0:t, :] = dc3_ref[...]
        dapad[t:, :] = zeros
        dcpad[t:, :] = zeros

        def chunk(ci, accs):
            base = pl.multiple_of(ci * CONV_ROWS, 8)
            rows = pl.ds(base, CONV_ROWS)
            da = jnp.zeros((CONV_ROWS, cl), F32)
            for k in range(CONF_KERNEL):
                da = da + dapad[pl.ds(base + (CONF_KERNEL - 1 - k), CONV_ROWS), :] * wdw_ref[k:k + 1, :]
            dcv = jnp.zeros((CONV_ROWS, cl), F32)
            for k in range(SHORT_KERNEL):
                dcv = dcv + dcpad[pl.ds(base + (SHORT_KERNEL - 1 - k), CONV_ROWS), :] * w3_ref[k:k + 1, :]
            av, sg = av_ref[rows, :], _sigmoid(ag_ref[rows, :])
            dav_ref[rows, :] = (da * sg).astype(dav_ref.dtype)
            dag_ref[rows, :] = (da * av * sg * (1.0 - sg)).astype(dag_ref.dtype)
            dcg_ref[rows, :] = (dcv * v_ref[rows, :]).astype(dcg_ref.dtype)
            dv_ref[rows, :] = (dcv * cg_ref[rows, :]).astype(dv_ref.dtype)
            d_out, d_out3 = dac_ref[rows, :], dc3_ref[rows, :]
            new = []
            for k in range(CONF_KERNEL):
                off = CONV_PAD - (CONF_KERNEL - 1) + k
                new.append(accs[k] + fold(d_out * apad[pl.ds(base + off, CONV_ROWS), :]))
            for k in range(SHORT_KERNEL):
                off = CONV_PAD - (SHORT_KERNEL - 1) + k
                new.append(accs[CONF_KERNEL + k] + fold(d_out3 * cpad[pl.ds(base + off, CONV_ROWS), :]))
            return tuple(new)

        init = tuple(jnp.zeros((8, cl), F32) for _ in range(CONF_KERNEL + SHORT_KERNEL))
        accs = lax.fori_loop(0, nchunk, chunk, init)
        for k in range(CONF_KERNEL):
            dwdw_ref[k:k + 1, :] = jnp.sum(accs[k], axis=0, keepdims=True)
        for k in range(SHORT_KERNEL):
            dw3_ref[k:k + 1, :] = jnp.sum(accs[CONF_KERNEL + k], axis=0, keepdims=True)

    def col(blk0):
        return pl.BlockSpec((t, cl), lambda j: (0, blk0 + j))

    own = pl.BlockSpec((t, cl), lambda j: (0, j))
    return pl.pallas_call(
        body,
        name="conv_bwd",
        grid=(nb,),
        in_specs=[col(0), col(nb), col(3 * nb), col(4 * nb), own, own,
                  pl.BlockSpec((CONF_KERNEL, cl), lambda j: (0, j)),
                  pl.BlockSpec((SHORT_KERNEL, cl), lambda j: (0, j))],
        out_specs=[own] * 4 + [pl.BlockSpec((CONF_KERNEL, cl), lambda j: (0, j)),
                               pl.BlockSpec((SHORT_KERNEL, cl), lambda j: (0, j))],
        out_shape=[jax.ShapeDtypeStruct((t, d_conf), MXU_DTYPE)] * 4
        + [jax.ShapeDtypeStruct((CONF_KERNEL, d_conf), F32), jax.ShapeDtypeStruct((SHORT_KERNEL, d_conf), F32)],
        scratch_shapes=[pltpu.VMEM((t + CONV_PAD, cl), F32)] * 4,
        compiler_params=pltpu.CompilerParams(dimension_semantics=("parallel",)),
    )(proj, proj, proj, proj, d_ac, d_c3, wdw, w3)


def _elementwise(name, fn, ins, out_dtypes, tr):
    ins = [(a, ()) if not isinstance(a, tuple) else a for a in ins]
    r, c = ins[0][0].shape[-2:]
    assert r % tr == 0, (name, r, tr)
    n_in = len(ins)

    def body(*refs):
        tiles = fn(*[x[...] for x in refs[:n_in]])
        for o_ref, tile in zip(refs[n_in:], tiles):
            o_ref[...] = tile.astype(o_ref.dtype)

    def spec(lead):
        return pl.BlockSpec((None,) * len(lead) + (tr, c), lambda i: (*lead, i, 0))

    res = pl.pallas_call(
        body,
        name=name,
        grid=(r // tr,),
        in_specs=[spec(lead) for _, lead in ins],
        out_specs=[pl.BlockSpec((tr, c), lambda i: (i, 0))] * len(out_dtypes),
        out_shape=[jax.ShapeDtypeStruct((r, c), d) for d in out_dtypes],
        compiler_params=pltpu.CompilerParams(dimension_semantics=("parallel",)),
    )(*[a for a, _ in ins])
    return res


def _adamw_tiles(w, g, m, v):
    m = ADAM_B1 * m + (1.0 - ADAM_B1) * g
    v = ADAM_B2 * v + (1.0 - ADAM_B2) * jnp.square(g)
    m_hat = m / (1.0 - ADAM_B1 ** ADAM_STEP)
    v_hat = v / (1.0 - ADAM_B2 ** ADAM_STEP)
    delta = -ADAM_LR * (m_hat / (jnp.sqrt(v_hat) + ADAM_EPS) + ADAM_WD * w)
    return g, delta, m, v


def _adamw(name, w, g, m, v, tr):
    shape = w.shape
    flat = [a.reshape(shape[-2:]) if a.ndim > 2 else a for a in (w, g, m, v)]
    res = _elementwise(name, _adamw_tiles, flat, [F32] * 4, tr)
    return [a.reshape(shape) for a in res]


def _pair_sum(name, p, q, core, tr):
    n_p, _, hr, c = p.shape
    assert hr % tr == 0

    def body(core_ref, p_ref, q_ref, o_ref):
        o_ref[...] = (p_ref[...] + q_ref[...]).astype(o_ref.dtype)

    return pl.pallas_call(
        body,
        name=name,
        grid_spec=pltpu.PrefetchScalarGridSpec(
            num_scalar_prefetch=1,
            grid=(n_p, hr // tr),
            in_specs=[pl.BlockSpec((None, None, tr, c), lambda a, i, core_ref: (a, core_ref[0], i, 0)),
                      pl.BlockSpec((None, tr, c), lambda a, i, core_ref: (a, i, 0))],
            out_specs=pl.BlockSpec((None, tr, c), lambda a, i, core_ref: (a, i, 0)),
        ),
        out_shape=jax.ShapeDtypeStruct((n_p, hr, c), WIRE_DTYPE),
        compiler_params=pltpu.CompilerParams(dimension_semantics=("parallel", "parallel")),
    )(core, p, q)


def _sum_pieces(name, rb, tr):
    n_p = rb.shape[0]

    def fn(*tiles):
        acc = tiles[0].astype(F32)
        for tile in tiles[1:]:
            acc = acc + tile.astype(F32)
        return (acc,)

    return _elementwise(name, fn, [(rb, (a,)) for a in range(n_p)], [F32], tr)[0]


HBM_SPEC = pl.BlockSpec(memory_space=pl.ANY)


def _place():
    x, y, c = lax.axis_index("x"), lax.axis_index("y"), lax.axis_index("c")
    chips = [(1 - x, y), (x, 1 - y), (1 - x, 1 - y)]
    return x, y, c, chips


def _gather_shards(shards, split):
    n = len(shards)
    n_split = sum(split)
    fwd_slot = {a: s for s, a in enumerate([a for a in range(n) if split[a]])}

    def body(*refs):
        ins, outs = refs[:n], refs[n:2 * n]
        send_sems, recv_sems, fsend_sems, frecv_sems, local_sems = refs[2 * n:]
        x, y, c, chips = _place()
        me = 2 * x + y
        sibling = (x, y, 1 - c)

        def part(ref, a, h):
            if not split[a]:
                return ref
            hr = shards[a].shape[0] // 2
            return ref.at[pl.ds(h * hr, hr), :]

        local = [pltpu.make_async_copy(ins[a], outs[a].at[me], local_sems.at[a]) for a in range(n)]
        for cp in local:
            cp.start()
        sends = []
        for a in range(n):
            for j, chip in enumerate(chips):
                sends.append(pltpu.make_async_remote_copy(
                    src_ref=part(ins[a], a, c), dst_ref=part(outs[a].at[me], a, c),
                    send_sem=send_sems.at[3 * a + j], recv_sem=recv_sems.at[3 * a + j],
                    device_id=(*chip, c), device_id_type=MESH))
        for cp in sends:
            cp.start()
        passed = []
        for a in range(n):
            for j, chip in enumerate(chips):
                landed = part(outs[a].at[2 * chip[0] + chip[1]], a, c)
                pltpu.make_async_remote_copy(
                    src_ref=landed, dst_ref=landed, send_sem=send_sems.at[3 * a + j], recv_sem=recv_sems.at[3 * a + j],
                    device_id=(*chip, c), device_id_type=MESH).wait_recv()
                if split[a]:
                    s = 3 * fwd_slot[a] + j
                    fwd = pltpu.make_async_remote_copy(
                        src_ref=landed, dst_ref=landed, send_sem=fsend_sems.at[s], recv_sem=frecv_sems.at[s],
                        device_id=sibling, device_id_type=MESH)
                    fwd.start()
                    passed.append(fwd)
        for a in range(n):
            if split[a]:
                for j, chip in enumerate(chips):
                    s = 3 * fwd_slot[a] + j
                    other = part(outs[a].at[2 * chip[0] + chip[1]], a, 1 - c)
                    pltpu.make_async_remote_copy(
                        src_ref=other, dst_ref=other, send_sem=fsend_sems.at[s], recv_sem=frecv_sems.at[s],
                        device_id=sibling, device_id_type=MESH).wait_recv()
        for cp in sends + passed:
            cp.wait_send()
        for cp in local:
            cp.wait()

    return pl.pallas_call(
        body,
        name="gather_shards",
        in_specs=[HBM_SPEC] * n,
        out_specs=[HBM_SPEC] * n,
        out_shape=[jax.ShapeDtypeStruct((N_CHIPS, *s.shape), s.dtype) for s in shards],
        scratch_shapes=[pltpu.SemaphoreType.DMA((3 * n,)), pltpu.SemaphoreType.DMA((3 * n,)),
                        pltpu.SemaphoreType.DMA((3 * n_split,)), pltpu.SemaphoreType.DMA((3 * n_split,)),
                        pltpu.SemaphoreType.DMA((n,))],
    )(*shards)


def _swap_halves(parts):
    n = len(parts)
    n_p = parts[0].shape[0]

    def body(*refs):
        ins, outs = refs[:n], refs[n:2 * n]
        send_sems, recv_sems = refs[2 * n:]
        x, y, c, _ = _place()
        copies = []
        for a in range(n):
            for p in range(n_p):
                copies.append(pltpu.make_async_remote_copy(
                    src_ref=ins[a].at[p, 1 - c], dst_ref=outs[a].at[p],
                    send_sem=send_sems.at[n_p * a + p], recv_sem=recv_sems.at[n_p * a + p],
                    device_id=(x, y, 1 - c), device_id_type=MESH))
        for cp in copies:
            cp.start()
        for cp in copies:
            cp.wait()

    return pl.pallas_call(
        body,
        name="swap_halves",
        in_specs=[HBM_SPEC] * n,
        out_specs=[HBM_SPEC] * n,
        out_shape=[jax.ShapeDtypeStruct((n_p, *p.shape[2:]), p.dtype) for p in parts],
        scratch_shapes=[pltpu.SemaphoreType.DMA((n_p * n,)), pltpu.SemaphoreType.DMA((n_p * n,))],
    )(*parts)


def _exchange_pieces(sums):
    n = len(sums)

    def body(*refs):
        ins, outs = refs[:n], refs[n:2 * n]
        send_sems, recv_sems, local_sems = refs[2 * n:]
        x, y, c, chips = _place()
        me = 2 * x + y
        local = [pltpu.make_async_copy(ins[a].at[me], outs[a].at[me], local_sems.at[a]) for a in range(n)]
        for cp in local:
            cp.start()
        sends = []
        for a in range(n):
            for j, chip in enumerate(chips):
                sends.append(pltpu.make_async_remote_copy(
                    src_ref=ins[a].at[2 * chip[0] + chip[1]], dst_ref=outs[a].at[me],
                    send_sem=send_sems.at[3 * a + j], recv_sem=recv_sems.at[3 * a + j],
                    device_id=(*chip, c), device_id_type=MESH))
        for cp in sends:
            cp.start()
        for a in range(n):
            for j, chip in enumerate(chips):
                slot = outs[a].at[2 * chip[0] + chip[1]]
                pltpu.make_async_remote_copy(
                    src_ref=slot, dst_ref=slot, send_sem=send_sems.at[3 * a + j], recv_sem=recv_sems.at[3 * a + j],
                    device_id=(*chip, c), device_id_type=MESH).wait_recv()
        for cp in sends:
            cp.wait_send()
        for cp in local:
            cp.wait()

    return pl.pallas_call(
        body,
        name="exchange_pieces",
        in_specs=[HBM_SPEC] * n,
        out_specs=[HBM_SPEC] * n,
        out_shape=[jax.ShapeDtypeStruct(s.shape, s.dtype) for s in sums],
        scratch_shapes=[pltpu.SemaphoreType.DMA((3 * n,)), pltpu.SemaphoreType.DMA((3 * n,)),
                        pltpu.SemaphoreType.DMA((n,))],
    )(*sums)


def _join_halves(halves):
    n = len(halves)

    def body(*refs):
        ins, outs = refs[:n], refs[n:2 * n]
        send_sems, recv_sems, local_sems = refs[2 * n:]
        x, y, c, _ = _place()
        local = [pltpu.make_async_copy(ins[a], outs[a].at[c], local_sems.at[a]) for a in range(n)]
        copies = [pltpu.make_async_remote_copy(
            src_ref=ins[a], dst_ref=outs[a].at[c], send_sem=send_sems.at[a], recv_sem=recv_sems.at[a],
            device_id=(x, y, 1 - c), device_id_type=MESH) for a in range(n)]
        for cp in local + copies:
            cp.start()
        for a in range(n):
            other = outs[a].at[1 - c]
            pltpu.make_async_remote_copy(
                src_ref=other, dst_ref=other, send_sem=send_sems.at[a], recv_sem=recv_sems.at[a],
                device_id=(x, y, 1 - c), device_id_type=MESH).wait_recv()
        for cp in copies:
            cp.wait_send()
        for cp in local:
            cp.wait()

    return pl.pallas_call(
        body,
        name="join_halves",
        in_specs=[HBM_SPEC] * n,
        out_specs=[HBM_SPEC] * n,
        out_shape=[jax.ShapeDtypeStruct((2, *h.shape), h.dtype) for h in halves],
        scratch_shapes=[pltpu.SemaphoreType.DMA((n,)), pltpu.SemaphoreType.DMA((n,)), pltpu.SemaphoreType.DMA((n,))],
    )(*halves)


def _allgather_small(block):
    m_per, n = block.shape

    def body(x_ref, out_ref, send_sems, recv_sems, local_sem):
        x, y, c, chips = _place()
        me, sibling = (x, y, c), (x, y, 1 - c)

        def rows(px, py, pc):
            return out_ref.at[pl.ds((4 * px + 2 * py + pc) * m_per, m_per), :]

        def copy(k, blk, to, src=None):
            return pltpu.make_async_remote_copy(
                src_ref=rows(*blk) if src is None else src, dst_ref=rows(*blk),
                send_sem=send_sems.at[k], recv_sem=recv_sems.at[k], device_id=to, device_id_type=MESH)

        mine = pltpu.make_async_copy(x_ref, rows(*me), local_sem)
        mine.start()
        first = [copy(0, me, sibling, src=x_ref)]
        first += [copy(1 + j, me, (*chip, c), src=x_ref) for j, chip in enumerate(chips)]
        for cp in first:
            cp.start()
        passed = [copy(4 + j, (*chip, c), sibling) for j, chip in enumerate(chips)]
        for j, chip in enumerate(chips):
            copy(1 + j, (*chip, c), me).wait_recv()
            passed[j].start()
        copy(0, sibling, me).wait_recv()
        for j, chip in enumerate(chips):
            copy(4 + j, (*chip, 1 - c), me).wait_recv()
        for cp in first + passed:
            cp.wait_send()
        mine.wait()

    return pl.pallas_call(
        body,
        name="allgather_small",
        out_shape=jax.ShapeDtypeStruct((8 * m_per, n), block.dtype),
        in_specs=[pl.BlockSpec(memory_space=pltpu.VMEM)],
        out_specs=pl.BlockSpec(memory_space=pltpu.VMEM),
        scratch_shapes=[pltpu.SemaphoreType.DMA((7,)), pltpu.SemaphoreType.DMA((7,)), pltpu.SemaphoreType.DMA],
    )(block)


def _sum_blocks(name, gathered, n_blocks):
    r = gathered.shape[0] // n_blocks
    c = gathered.shape[1]

    def body(g_ref, o_ref):
        acc = g_ref[0:r, :]
        for b in range(1, n_blocks):
            acc = acc + g_ref[b * r:(b + 1) * r, :]
        o_ref[...] = acc

    return pl.pallas_call(body, name=name, out_shape=jax.ShapeDtypeStruct((r, c), F32))(gathered)


def _largest_tile(n, cap, mult):
    best = None
    for d in range(mult, min(n, cap) + 1, mult):
        if n % d == 0:
            best = d
    assert best is not None, (n, cap, mult)
    return best


def kernel(x, meta, g_pre_mix, w_in, b_gates, conf_dw_w, conf_dw_b, conf_ln_g, conf_ln_b, conf_w_pw, short_dw_w, short_w_out, w_o, g_post_mix, g_pre_mlp, w_up, w_down, g_post_mlp, loss_target, m_meta, m_g_pre_mix, m_w_in, m_b_gates, m_conf_dw_w, m_conf_dw_b, m_conf_ln_g, m_conf_ln_b, m_conf_w_pw, m_short_dw_w, m_short_w_out, m_w_o, m_g_post_mix, m_g_pre_mlp, m_w_up, m_w_down, m_g_post_mlp, v_meta, v_g_pre_mix, v_w_in, v_b_gates, v_conf_dw_w, v_conf_dw_b, v_conf_ln_g, v_conf_ln_b, v_conf_w_pw, v_short_dw_w, v_short_w_out, v_w_o, v_g_post_mix, v_g_pre_mlp, v_w_up, v_w_down, v_g_post_mlp):
    seq, d = x.shape[1], x.shape[2]
    t = seq + N_META
    d_conf = conf_dw_b.shape[1]
    d_ff = w_up.shape[2] * N_CHIPS
    in_cols = w_in.shape[2] * N_CHIPS
    assert in_cols == 5 * d_conf + 2 * d and d == 2 * d_conf
    cw = d_conf
    core = lax.axis_index("c")
    chip = 2 * lax.axis_index("x") + lax.axis_index("y")

    tr = _largest_tile(t, 64, 16)
    tm = _largest_tile(t, 1024, 16)
    tn_in = _largest_tile(in_cols // N_CHIPS, 1152, LANES)
    tn_d = _largest_tile(d, 1024, LANES)
    tn_ff = _largest_tile(d_ff // N_CHIPS, 1024, LANES)
    tn_pw = _largest_tile(d // N_CHIPS, 512, LANES)

    big = [w_in[0], conf_w_pw[0], short_w_out[0], w_o[0], w_up[0], w_down[0]]
    big_lp = [_elementwise(f"cast_w{a}", lambda w: (w,), [w], [MXU_DTYPE], _largest_tile(w.shape[0], 256, 16))[0]
              for a, w in enumerate(big)]
    small = [meta, conf_dw_w[0], short_dw_w[0]]
    gathered = _gather_shards(big_lp + small, [True] * len(big_lp) + [False] * len(small))
    wg_in, wg_pw, wg_sout, wg_o, wg_up, wg_down, meta_g, wdw_g, w3_g = gathered
    wg_o = wg_o.reshape(d, d)
    wg_down = wg_down.reshape(d_ff, d)
    meta_full = jnp.transpose(meta_g, (1, 0, 2)).reshape(N_META, d)
    wdw = jnp.transpose(wdw_g, (1, 0, 2)).reshape(CONF_KERNEL, d_conf)
    w3 = jnp.transpose(w3_g, (1, 0, 2)).reshape(SHORT_KERNEL, d_conf)

    h0 = jnp.concatenate([meta_full, x[0]], axis=0)
    target = jnp.concatenate([jnp.zeros((N_META, d), F32), loss_target[0]], axis=0)

    def norm_in(i, rows, vecs):
        return [_rms_fwd(rows[0], vecs[0])], []

    (n_lp,) = _rowwise("norm_in", norm_in, [(h0, d, 0)], [g_pre_mix], [(d, MXU_DTYPE)], [], tr)
    proj = _matmul("proj", n_lp, wg_in, kind="nn", tm=tm, tn=tn_in, tk=d, out_dtypes=[F32])
    ac, c3 = _conv_fwd(proj, wdw, conf_dw_b, w3, d_conf)

    def ln_parts(ac_t, ln_g, ln_b):
        mu = jnp.mean(ac_t, axis=-1, keepdims=True)
        xc = ac_t - mu
        rstd = lax.rsqrt(jnp.mean(xc * xc, axis=-1, keepdims=True) + LN_EPS)
        xh = xc * rstd
        return xh, rstd, xh * ln_g + ln_b

    def branch_act(i, rows, vecs):
        ac_t, c3_t, bg_t = rows
        _, _, al = ln_parts(ac_t, vecs[0], vecs[1])
        return [al * _sigmoid(al), bg_t * c3_t], []

    a_act, s_lp = _rowwise("branch_act", branch_act, [(ac, cw, 0), (c3, cw, 0), (proj, cw, 2)],
                           [conf_ln_g, conf_ln_b], [(d_conf, MXU_DTYPE), (d_conf, MXU_DTYPE)], [], tr)
    y_a = _matmul("y_a", a_act, wg_pw, kind="nn", tm=tm, tn=tn_pw, tk=d_conf, out_dtypes=[F32])
    y_b = _matmul("y_b", s_lp, wg_sout, kind="nn", tm=tm, tn=tn_pw, tk=d_conf, out_dtypes=[F32])

    gate_rows = [(proj, cw, 5), (proj, cw, 6), (proj, cw, 7), (proj, cw, 8)]

    def gates_of(rows, b):
        ga = _sigmoid(jnp.concatenate([rows[0], rows[1]], axis=1) + b[:, :d])
        gb = _sigmoid(jnp.concatenate([rows[2], rows[3]], axis=1) + b[:, d:])
        return ga, gb

    def gate(i, rows, vecs):
        ga, gb = gates_of(rows[2:], vecs[0])
        return [ga * rows[0] + gb * rows[1]], []

    (m_lp,) = _rowwise("gate", gate, [(y_a, d, 0), (y_b, d, 0)] + gate_rows, [b_gates], [(d, MXU_DTYPE)], [], tr)
    mix = _matmul("mix", m_lp, wg_o, kind="nn", tm=tm, tn=tn_d, tk=d, out_dtypes=[F32])

    def post_mix(i, rows, vecs):
        h1_t = rows[0] + _rms_fwd(rows[1], vecs[0])
        return [h1_t, _rms_fwd(h1_t, vecs[1])], []

    h1, n2_lp = _rowwise("post_mix", post_mix, [(h0, d, 0), (mix, d, 0)], [g_post_mix, g_pre_mlp],
                         [(d, F32), (d, MXU_DTYPE)], [], tr)
    up, f_lp = _matmul("up", n2_lp, wg_up, kind="nn", tm=tm, tn=tn_ff, tk=d, out_dtypes=[F32, MXU_DTYPE],
                       epilogue=lambda acc: (acc, jnp.square(jnp.maximum(acc, 0.0))))
    dn = _matmul("down", f_lp, wg_down, kind="nn", tm=tm, tn=tn_d, tk=_largest_tile(d_ff, 2048, LANES),
                 out_dtypes=[F32])

    def head(i, rows, vecs):
        h1_t, dn_t, tgt = rows
        y = h1_t + _rms_fwd(dn_t, vecs[0])
        row = i * tr + lax.broadcasted_iota(jnp.int32, (tr, 1), 0)
        err = jnp.where(row >= N_META, y - tgt, 0.0)
        dy = err / d
        d_dn, dg = _rms_bwd(dn_t, vecs[0], dy)
        loss_rows = 0.5 * jnp.mean(err * err, axis=-1, keepdims=True)
        return [dy, d_dn], [dg, jnp.broadcast_to(loss_rows, (tr, LANES))]

    dy, d_dn, dg_post_mlp, loss_vec = _rowwise(
        "head", head, [(h1, d, 0), (dn, d, 0), (target, d, 0)], [g_post_mlp], [(d, F32), (d, MXU_DTYPE)], [d, LANES], tr)
    loss = lax.psum(loss_vec[0, 0], ("x", "y", "c"))

    d_up = _matmul("d_up", d_dn, wg_down, kind="nt", tm=tm, tn=tn_ff, tk=d, out_dtypes=[MXU_DTYPE], extras=[up],
                   epilogue=lambda acc, up_t: (acc * (2.0 * jnp.maximum(up_t, 0.0)),))
    tk_t = t
    gw_down = _matmul("gw_down", f_lp, d_dn, kind="tn", tm=_largest_tile(d_ff, 512, LANES), tn=tn_d, tk=tk_t,
                      out_dtypes=[F32])
    d_n2 = _matmul("d_n2", d_up, wg_up, kind="nt", tm=tm, tn=tn_d, tk=tn_ff, out_dtypes=[F32])
    gw_up = _matmul("gw_up", n2_lp, d_up, kind="tn", tm=_largest_tile(d, 512, LANES), tn=tn_ff, tk=tk_t,
                    out_dtypes=[F32], out_pieces=N_CHIPS)

    def bwd_mid(i, rows, vecs):
        dy_t, dn2_t, h1_t, mix_t = rows
        d_h1a, dg_pre_mlp = _rms_bwd(h1_t, vecs[1], dn2_t)
        d_h1 = dy_t + d_h1a
        d_mix, dg_post_mix = _rms_bwd(mix_t, vecs[0], d_h1)
        return [d_h1, d_mix], [dg_pre_mlp, dg_post_mix]

    d_h1, d_mix, dg_pre_mlp, dg_post_mix = _rowwise(
        "bwd_mid", bwd_mid, [(dy, d, 0), (d_n2, d, 0), (h1, d, 0), (mix, d, 0)], [g_post_mix, g_pre_mlp],
        [(d, F32), (d, MXU_DTYPE)], [d, d], tr)
    d_m = _matmul("d_m", d_mix, wg_o, kind="nt", tm=tm, tn=tn_d, tk=d, out_dtypes=[F32])
    gw_o = _matmul("gw_o", m_lp, d_mix, kind="tn", tm=_largest_tile(d, 512, LANES), tn=tn_d, tk=tk_t, out_dtypes=[F32])

    def gate_bwd(i, rows, vecs):
        dm_t, ya_t, yb_t = rows[:3]
        ga, gb = gates_of(rows[3:], vecs[0])
        d_gpre = jnp.concatenate([dm_t * ya_t * ga * (1.0 - ga), dm_t * yb_t * gb * (1.0 - gb)], axis=1)
        return [dm_t * ga, dm_t * gb, d_gpre], [d_gpre]

    d_ya, d_yb, d_gpre, dg_b_gates = _rowwise(
        "gate_bwd", gate_bwd, [(d_m, d, 0), (y_a, d, 0), (y_b, d, 0)] + gate_rows, [b_gates],
        [(d, MXU_DTYPE), (d, MXU_DTYPE), (2 * d, MXU_DTYPE)], [2 * d], tr)
    d_aact = _matmul("d_aact", d_ya, wg_pw, kind="nt", tm=tm, tn=d_conf, tk=tn_pw, out_dtypes=[F32])
    gw_pw = _matmul("gw_pw", a_act, d_ya, kind="tn", tm=_largest_tile(d_conf, 512, LANES), tn=tn_pw, tk=tk_t,
                    out_dtypes=[F32], out_pieces=N_CHIPS)
    d_s = _matmul("d_s", d_yb, wg_sout, kind="nt", tm=tm, tn=d_conf, tk=tn_pw, out_dtypes=[F32])
    gw_sout = _matmul("gw_sout", s_lp, d_yb, kind="tn", tm=_largest_tile(d_conf, 512, LANES), tn=tn_pw, tk=tk_t,
                      out_dtypes=[F32], out_pieces=N_CHIPS)

    def branch_bwd(i, rows, vecs):
        daact_t, ds_t, ac_t, c3_t, bg_t = rows
        xh, rstd, al = ln_parts(ac_t, vecs[0], vecs[1])
        sg = _sigmoid(al)
        d_al = daact_t * (sg * (1.0 + al * (1.0 - sg)))
        dxh = d_al * vecs[0]
        d_ac = rstd * (dxh - jnp.mean(dxh, axis=-1, keepdims=True) - xh * jnp.mean(dxh * xh, axis=-1, keepdims=True))
        return [d_ac, ds_t * bg_t, ds_t * c3_t], [d_al * xh, d_al, d_ac]

    d_ac, d_c3, d_bg, dg_ln_g, dg_ln_b, dg_dw_b = _rowwise(
        "branch_bwd", branch_bwd, [(d_aact, cw, 0), (d_s, cw, 0), (ac, cw, 0), (c3, cw, 0), (proj, cw, 2)],
        [conf_ln_g, conf_ln_b], [(d_conf, F32), (d_conf, F32), (d_conf, MXU_DTYPE)], [d_conf] * 3, tr)
    d_av, d_ag, d_cg, d_v, dg_wdw, dg_w3 = _conv_bwd(proj, d_ac, d_c3, wdw, w3, d_conf)
    d_proj = jnp.concatenate([d_av, d_ag, d_bg, d_cg, d_v, d_gpre], axis=1)
    d_n = _matmul("d_n", d_proj, wg_in, kind="nt", tm=tm, tn=tn_d, tk=tn_in, out_dtypes=[F32])
    gw_in = _matmul("gw_in", n_lp, d_proj, kind="tn", tm=_largest_tile(d, 512, LANES), tn=tn_in, tk=tk_t,
                    out_dtypes=[F32], out_pieces=N_CHIPS)

    def bwd_in(i, rows, vecs):
        d_h0a, dg = _rms_bwd(rows[2], vecs[0], rows[1])
        return [rows[0] + d_h0a], [dg]

    d_h0, dg_pre_mix = _rowwise("bwd_in", bwd_in, [(d_h1, d, 0), (d_n, d, 0), (h0, d, 0)], [g_pre_mix],
                                [(d, F32)], [d], tr)
    grad_x = d_h0[N_META:][None]

    partial = [gw_in, gw_pw, gw_sout, gw_o.reshape(N_CHIPS, d // N_CHIPS, d), gw_up,
               gw_down.reshape(N_CHIPS, d_ff // N_CHIPS, d)]
    halves = [p.reshape(N_CHIPS, 2, p.shape[1] // 2, p.shape[2]) for p in partial]
    from_sibling = _swap_halves(halves)
    core_arr = core.astype(jnp.int32).reshape(1)
    pair = [_pair_sum(f"pair_sum{a}", p, q, core_arr, _largest_tile(q.shape[1], 128, 16))
            for a, (p, q) in enumerate(zip(halves, from_sibling))]
    pieces = _exchange_pieces(pair)
    reduced = [_sum_pieces(f"sum_pieces{a}", rb, _largest_tile(rb.shape[1], 128, 16)) for a, rb in enumerate(pieces)]
    joined = _join_halves(reduced)
    g_big = [j.reshape(w.shape) for j, w in zip(joined, big)]

    small_w = d_conf
    rep = [dg_pre_mix, dg_b_gates, dg_dw_b, dg_ln_g, dg_ln_b, dg_post_mix, dg_pre_mlp, dg_post_mlp]
    rep_w = [g_pre_mix, b_gates, conf_dw_b, conf_ln_g, conf_ln_b, g_post_mix, g_pre_mlp, g_post_mlp]
    rep_m = [m_g_pre_mix, m_b_gates, m_conf_dw_b, m_conf_ln_g, m_conf_ln_b, m_g_post_mix, m_g_pre_mlp, m_g_post_mlp]
    rep_v = [v_g_pre_mix, v_b_gates, v_conf_dw_b, v_conf_ln_g, v_conf_ln_b, v_g_post_mix, v_g_pre_mlp, v_g_post_mlp]
    col = [dg_wdw, dg_w3, d_h0[:N_META]]
    assert all(a.size % small_w == 0 for a in rep + col)

    def pack(arrs):
        flat = jnp.concatenate([a.reshape(-1, small_w) for a in arrs], axis=0)
        return jnp.pad(flat, ((0, -flat.shape[0] % 8), (0, 0)))

    def unpack(buf, like):
        out, r0 = [], 0
        for a in like:
            nr = a.size // small_w
            out.append(buf[r0:r0 + nr].reshape(a.shape))
            r0 += nr
        return out

    packed = pack(rep + col)
    total = _sum_blocks("sum_small", _allgather_small(packed), 8)
    small_g = unpack(total, rep + col)
    g_rep = small_g[:len(rep)]
    g_wdw_full, g_w3_full, g_meta_full = small_g[len(rep):]
    sc = d_conf // N_CHIPS
    g_wdw = lax.dynamic_slice_in_dim(g_wdw_full, chip * sc, sc, axis=1)
    g_w3 = lax.dynamic_slice_in_dim(g_w3_full, chip * sc, sc, axis=1)
    g_meta = lax.dynamic_slice_in_dim(g_meta_full, chip * (d // N_CHIPS), d // N_CHIPS, axis=1)

    big_m = [m_w_in, m_conf_w_pw, m_short_w_out, m_w_o, m_w_up, m_w_down]
    big_v = [v_w_in, v_conf_w_pw, v_short_w_out, v_w_o, v_w_up, v_w_down]
    big_res = [_adamw(f"adamw_big{a}", w, g, m[0], v[0], _largest_tile(w.shape[0], 128, 8))
               for a, (w, g, m, v) in enumerate(zip(big, g_big, big_m, big_v))]
    rep_pack = [pack(arrs) for arrs in (rep_w, g_rep, rep_m, rep_v)]
    rep_res = [unpack(buf, rep_w) for buf in _adamw("adamw_rep", *rep_pack, rep_pack[0].shape[0])]
    col_res = [_adamw(f"adamw_col{a}", w, g, m, v, w.shape[0]) for a, (w, g, m, v) in enumerate(
        [(meta, g_meta, m_meta, v_meta), (conf_dw_w[0], g_wdw, m_conf_dw_w[0], v_conf_dw_w[0]),
         (short_dw_w[0], g_w3, m_short_dw_w[0], v_short_dw_w[0])])]

    def leaf(q):
        r = lambda a: rep_res[q][a]
        b = lambda a: big_res[a][q][None]
        return [col_res[0][q], r(0), b(0), r(1), col_res[1][q][None], r(2), r(3), r(4), b(1), col_res[2][q][None], b(2),
                b(3), r(5), r(6), b(4), b(5), r(7)]

    return (loss, grad_x, *leaf(0), *leaf(1), *leaf(2), *leaf(3))
```

```python
import functools

import jax
import jax.numpy as jnp
from jax import lax
from jax.experimental import pallas as pl
from jax.experimental.pallas import tpu as pltpu

F32 = jnp.float32
BF16 = jnp.bfloat16
MXU_DTYPE = BF16
WIRE_DTYPE = BF16

N_META = 16
CONF_KERNEL = 31
SHORT_KERNEL = 3
CONV_PAD = 32
RMS_EPS = 1e-6
LN_EPS = 1e-5
ADAM_LR = 0.001
ADAM_B1 = 0.9
ADAM_B2 = 0.999
ADAM_EPS = 1e-08
ADAM_WD = 0.01
ADAM_STEP = 10

N_CHIPS = 4
MESH = pl.DeviceIdType.MESH
LANES = 128


def _sigmoid(z):
    return 1.0 / (1.0 + jnp.exp(-z))


def _matmul(name, a, b, *, kind, tm, tn, tk, out_dtypes, out_pieces=1, epilogue=None, extras=()):
    pieces = b.shape[0] if b.ndim == 3 else 1
    if kind == "nn":
        m, kdim = a.shape
        n = b.shape[-1] * pieces
        dims = (((1,), (0,)), ((), ()))
        a_spec = pl.BlockSpec((tm, tk), lambda i, j, k: (i, k))
        if b.ndim == 2:
            b_spec = pl.BlockSpec((tk, tn), lambda i, j, k: (k, j))
        else:
            npp = b.shape[-1] // tn
            b_spec = pl.BlockSpec((None, tk, tn), lambda i, j, k: (j // npp, k, j % npp))
    elif kind == "nt":
        m, kdim = a.shape
        n = b.shape[-2]
        dims = (((1,), (1,)), ((), ()))
        a_spec = pl.BlockSpec((tm, tk), lambda i, j, k: (i, k))
        if b.ndim == 2:
            b_spec = pl.BlockSpec((tn, tk), lambda i, j, k: (j, k))
        else:
            kpp = b.shape[-1] // tk
            b_spec = pl.BlockSpec((None, tn, tk), lambda i, j, k: (k // kpp, j, k % kpp))
    else:
        kdim, m = a.shape
        n = b.shape[-1]
        dims = (((0,), (0,)), ((), ()))
        a_spec = pl.BlockSpec((tk, tm), lambda i, j, k: (k, i))
        b_spec = pl.BlockSpec((tk, tn), lambda i, j, k: (k, j))
    assert m % tm == 0 and n % tn == 0 and kdim % tk == 0, (name, m, n, kdim, tm, tn, tk)
    nk = kdim // tk
    if out_pieces == 1:
        out_shape = (m, n)
        out_spec = pl.BlockSpec((tm, tn), lambda i, j, k: (i, j))
    else:
        onpp = n // out_pieces // tn
        out_shape = (out_pieces, m, n // out_pieces)
        out_spec = pl.BlockSpec((None, tm, tn), lambda i, j, k: (j // onpp, i, j % onpp))
    n_ex, n_out = len(extras), len(out_dtypes)
    if epilogue is None:
        epilogue = lambda acc: (acc,)

    def body(a_ref, b_ref, *rest):
        ex_refs, o_refs = rest[:n_ex], rest[n_ex:n_ex + n_out]
        prod = lax.dot_general(a_ref[...], b_ref[...], dims, preferred_element_type=F32)

        def finish(acc):
            tiles = epilogue(acc, *[r[...] for r in ex_refs])
            for o_ref, t in zip(o_refs, tiles):
                o_ref[...] = t.astype(o_ref.dtype)

        if nk == 1:
            finish(prod)
        else:
            acc_ref = rest[n_ex + n_out]
            k = pl.program_id(2)

            @pl.when(k == 0)
            def _():
                acc_ref[...] = prod

            @pl.when(jnp.logical_and(k > 0, k < nk - 1))
            def _():
                acc_ref[...] += prod

            @pl.when(k == nk - 1)
            def _():
                finish(acc_ref[...] + prod)

    ex_specs = [pl.BlockSpec((tm, tn), lambda i, j, k: (i, j)) for _ in extras]
    res = pl.pallas_call(
        body,
        name=name,
        grid=(m // tm, n // tn, nk),
        in_specs=[a_spec, b_spec, *ex_specs],
        out_specs=[out_spec] * n_out,
        out_shape=[jax.ShapeDtypeStruct(out_shape, d) for d in out_dtypes],
        scratch_shapes=[pltpu.VMEM((tm, tn), F32)] if nk > 1 else [],
        compiler_params=pltpu.CompilerParams(dimension_semantics=("parallel", "parallel", "arbitrary")),
    )(a, b, *extras)
    return res[0] if n_out == 1 else res


def _rowwise(name, fn, rows, vecs, outs, sums, tr):
    t = rows[0][0].shape[0]
    assert t % tr == 0
    n_r, n_v, n_o, n_s = len(rows), len(vecs), len(outs), len(sums)

    def body(*refs):
        r_in, v_in = refs[:n_r], refs[n_r:n_r + n_v]
        o_refs = refs[n_r + n_v:n_r + n_v + n_o]
        s_refs = refs[n_r + n_v + n_o:]
        i = pl.program_id(0)
        o_tiles, s_tiles = fn(i, [r[...] for r in r_in], [v[...] for v in v_in])
        for o_ref, tile in zip(o_refs, o_tiles):
            o_ref[...] = tile.astype(o_ref.dtype)
        for s_ref, tile in zip(s_refs, s_tiles):
            part = jnp.sum(tile, axis=0, keepdims=True)

            @pl.when(i == 0)
            def _(s_ref=s_ref, part=part):
                s_ref[...] = part

            @pl.when(i > 0)
            def _(s_ref=s_ref, part=part):
                s_ref[...] += part

    def row_spec(width, blk):
        return pl.BlockSpec((tr, width), lambda i: (i, blk))

    res = pl.pallas_call(
        body,
        name=name,
        grid=(t // tr,),
        in_specs=[row_spec(w, blk) for _, w, blk in rows]
        + [pl.BlockSpec(v.shape, lambda i: (0, 0)) for v in vecs],
        out_specs=[pl.BlockSpec((tr, c), lambda i: (i, 0)) for c, _ in outs]
        + [pl.BlockSpec((1, c), lambda i: (0, 0)) for c in sums],
        out_shape=[jax.ShapeDtypeStruct((t, c), d) for c, d in outs]
        + [jax.ShapeDtypeStruct((1, c), F32) for c in sums],
        compiler_params=pltpu.CompilerParams(dimension_semantics=("arbitrary",)),
    )(*[r[0] for r in rows], *vecs)
    return res


def _rms_fwd(x, g):
    r = lax.rsqrt(jnp.mean(x * x, axis=-1, keepdims=True) + RMS_EPS)
    return x * r * g


def _rms_bwd(x, g, dy):
    r = lax.rsqrt(jnp.mean(x * x, axis=-1, keepdims=True) + RMS_EPS)
    xn = x * r
    dxn = dy * g
    dx = r * (dxn - xn * jnp.mean(dxn * xn, axis=-1, keepdims=True))
    return dx, dy * xn


CONV_ROWS = 48
CONV_LANES = 128


def _conv_fwd(proj, wdw, bdw, w3, d_conf):
    t = proj.shape[0]
    cl = CONV_LANES
    nb = d_conf // cl
    nchunk = t // CONV_ROWS
    assert t % CONV_ROWS == 0

    def body(av_ref, ag_ref, cg_ref, v_ref, wdw_ref, bdw_ref, w3_ref, ac_ref, c3_ref, apad, cpad):
        zeros = jnp.zeros((CONV_PAD, cl), F32)
        apad[0:CONV_PAD, :] = zeros
        cpad[0:CONV_PAD, :] = zeros
        apad[CONV_PAD:, :] = av_ref[...] * _sigmoid(ag_ref[...])
        cpad[CONV_PAD:, :] = cg_ref[...] * v_ref[...]

        def chunk(ci, carry):
            base = pl.multiple_of(ci * CONV_ROWS, 8)
            acc = jnp.zeros((CONV_ROWS, cl), F32) + bdw_ref[...]
            for k in range(CONF_KERNEL):
                off = CONV_PAD - (CONF_KERNEL - 1) + k
                acc = acc + apad[pl.ds(base + off, CONV_ROWS), :] * wdw_ref[k:k + 1, :]
            ac_ref[pl.ds(base, CONV_ROWS), :] = acc
            acc3 = jnp.zeros((CONV_ROWS, cl), F32)
            for k in range(SHORT_KERNEL):
                off = CONV_PAD - (SHORT_KERNEL - 1) + k
                acc3 = acc3 + cpad[pl.ds(base + off, CONV_ROWS), :] * w3_ref[k:k + 1, :]
            c3_ref[pl.ds(base, CONV_ROWS), :] = acc3
            return carry

        lax.fori_loop(0, nchunk, chunk, 0)

    def col(blk0):
        return pl.BlockSpec((t, cl), lambda j: (0, blk0 + j))

    return pl.pallas_call(
        body,
        name="conv_fwd",
        grid=(nb,),
        in_specs=[col(0), col(nb), col(3 * nb), col(4 * nb),
                  pl.BlockSpec((CONF_KERNEL, cl), lambda j: (0, j)),
                  pl.BlockSpec((1, cl), lambda j: (0, j)),
                  pl.BlockSpec((SHORT_KERNEL, cl), lambda j: (0, j))],
        out_specs=[pl.BlockSpec((t, cl), lambda j: (0, j))] * 2,
        out_shape=[jax.ShapeDtypeStruct((t, d_conf), F32)] * 2,
        scratch_shapes=[pltpu.VMEM((t + CONV_PAD, cl), F32)] * 2,
        compiler_params=pltpu.CompilerParams(dimension_semantics=("parallel",)),
    )(proj, proj, proj, proj, wdw, bdw, w3)


def _conv_bwd(proj, d_ac, d_c3, wdw, w3, d_conf):
    t = proj.shape[0]
    cl = CONV_LANES
    nb = d_conf // cl
    nchunk = t // CONV_ROWS
    nsub = CONV_ROWS // 8

    def fold(p):
        r = p[0:8]
        for s in range(1, nsub):
            r = r + p[8 * s:8 * s + 8]
        return r

    def body(av_ref, ag_ref, cg_ref, v_ref, dac_ref, dc3_ref, wdw_ref, w3_ref,
             dav_ref, dag_ref, dcg_ref, dv_ref, dwdw_ref, dw3_ref, apad, cpad, dapad, dcpad):
        zeros = jnp.zeros((CONV_PAD, cl), F32)
        apad[0:CONV_PAD, :] = zeros
        cpad[0:CONV_PAD, :] = zeros
        apad[CONV_PAD:, :] = av_ref[...] * _sigmoid(ag_ref[...])
        cpad[CONV_PAD:, :] = cg_ref[...] * v_ref[...]
        dapad[0:t, :] = dac_ref[...]
        dcpad[0:t, :] = dc3_ref[...]
        dapad[t:, :] = zeros
        dcpad[t:, :] = zeros

        def chunk(ci, accs):
            base = pl.multiple_of(ci * CONV_ROWS, 8)
            rows = pl.ds(base, CONV_ROWS)
            da = jnp.zeros((CONV_ROWS, cl), F32)
            for k in range(CONF_KERNEL):
                da = da + dapad[pl.ds(base + (CONF_KERNEL - 1 - k), CONV_ROWS), :] * wdw_ref[k:k + 1, :]
            dcv = jnp.zeros((CONV_ROWS, cl), F32)
            for k in range(SHORT_KERNEL):
                dcv = dcv + dcpad[pl.ds(base + (SHORT_KERNEL - 1 - k), CONV_ROWS), :] * w3_ref[k:k + 1, :]
            av, sg = av_ref[rows, :], _sigmoid(ag_ref[rows, :])
            dav_ref[rows, :] = (da * sg).astype(dav_ref.dtype)
            dag_ref[rows, :] = (da * av * sg * (1.0 - sg)).astype(dag_ref.dtype)
            dcg_ref[rows, :] = (dcv * v_ref[rows, :]).astype(dcg_ref.dtype)
            dv_ref[rows, :] = (dcv * cg_ref[rows, :]).astype(dv_ref.dtype)
            d_out, d_out3 = dac_ref[rows, :], dc3_ref[rows, :]
            new = []
            for k in range(CONF_KERNEL):
                off = CONV_PAD - (CONF_KERNEL - 1) + k
                new.append(accs[k] + fold(d_out * apad[pl.ds(base + off, CONV_ROWS), :]))
            for k in range(SHORT_KERNEL):
                off = CONV_PAD - (SHORT_KERNEL - 1) + k
                new.append(accs[CONF_KERNEL + k] + fold(d_out3 * cpad[pl.ds(base + off, CONV_ROWS), :]))
            return tuple(new)

        init = tuple(jnp.zeros((8, cl), F32) for _ in range(CONF_KERNEL + SHORT_KERNEL))
        accs = lax.fori_loop(0, nchunk, chunk, init)
        for k in range(CONF_KERNEL):
            dwdw_ref[k:k + 1, :] = jnp.sum(accs[k], axis=0, keepdims=True)
        for k in range(SHORT_KERNEL):
            dw3_ref[k:k + 1, :] = jnp.sum(accs[CONF_KERNEL + k], axis=0, keepdims=True)

    def col(blk0):
        return pl.BlockSpec((t, cl), lambda j: (0, blk0 + j))

    own = pl.BlockSpec((t, cl), lambda j: (0, j))
    return pl.pallas_call(
        body,
        name="conv_bwd",
        grid=(nb,),
        in_specs=[col(0), col(nb), col(3 * nb), col(4 * nb), own, own,
                  pl.BlockSpec((CONF_KERNEL, cl), lambda j: (0, j)),
                  pl.BlockSpec((SHORT_KERNEL, cl), lambda j: (0, j))],
        out_specs=[own] * 4 + [pl.BlockSpec((CONF_KERNEL, cl), lambda j: (0, j)),
                               pl.BlockSpec((SHORT_KERNEL, cl), lambda j: (0, j))],
        out_shape=[jax.ShapeDtypeStruct((t, d_conf), MXU_DTYPE)] * 4
        + [jax.ShapeDtypeStruct((CONF_KERNEL, d_conf), F32), jax.ShapeDtypeStruct((SHORT_KERNEL, d_conf), F32)],
        scratch_shapes=[pltpu.VMEM((t + CONV_PAD, cl), F32)] * 4,
        compiler_params=pltpu.CompilerParams(dimension_semantics=("parallel",)),
    )(proj, proj, proj, proj, d_ac, d_c3, wdw, w3)


def _elementwise(name, fn, ins, out_dtypes, tr):
    ins = [(a, ()) if not isinstance(a, tuple) else a for a in ins]
    r, c = ins[0][0].shape[-2:]
    assert r % tr == 0, (name, r, tr)
    n_in = len(ins)

    def body(*refs):
        tiles = fn(*[x[...] for x in refs[:n_in]])
        for o_ref, tile in zip(refs[n_in:], tiles):
            o_ref[...] = tile.astype(o_ref.dtype)

    def spec(lead):
        return pl.BlockSpec((None,) * len(lead) + (tr, c), lambda i: (*lead, i, 0))

    res = pl.pallas_call(
        body,
        name=name,
        grid=(r // tr,),
        in_specs=[spec(lead) for _, lead in ins],
        out_specs=[pl.BlockSpec((tr, c), lambda i: (i, 0))] * len(out_dtypes),
        out_shape=[jax.ShapeDtypeStruct((r, c), d) for d in out_dtypes],
        compiler_params=pltpu.CompilerParams(dimension_semantics=("parallel",)),
    )(*[a for a, _ in ins])
    return res


def _adamw_tiles(w, g, m, v):
    m = ADAM_B1 * m + (1.0 - ADAM_B1) * g
    v = ADAM_B2 * v + (1.0 - ADAM_B2) * jnp.square(g)
    m_hat = m / (1.0 - ADAM_B1 ** ADAM_STEP)
    v_hat = v / (1.0 - ADAM_B2 ** ADAM_STEP)
    delta = -ADAM_LR * (m_hat / (jnp.sqrt(v_hat) + ADAM_EPS) + ADAM_WD * w)
    return g, delta, m, v


def _adamw(name, w, g, m, v, tr):
    shape = w.shape
    flat = [a.reshape(shape[-2:]) if a.ndim > 2 else a for a in (w, g, m, v)]
    res = _elementwise(name, _adamw_tiles, flat, [F32] * 4, tr)
    return [a.reshape(shape) for a in res]


def _pair_sum(name, p, q, core, tr):
    n_p, _, hr, c = p.shape
    assert hr % tr == 0

    def body(core_ref, p_ref, q_ref, o_ref):
        o_ref[...] = (p_ref[...] + q_ref[...]).astype(o_ref.dtype)

    return pl.pallas_call(
        body,
        name=name,
        grid_spec=pltpu.PrefetchScalarGridSpec(
            num_scalar_prefetch=1,
            grid=(n_p, hr // tr),
            in_specs=[pl.BlockSpec((None, None, tr, c), lambda a, i, core_ref: (a, core_ref[0], i, 0)),
                      pl.BlockSpec((None, tr, c), lambda a, i, core_ref: (a, i, 0))],
            out_specs=pl.BlockSpec((None, tr, c), lambda a, i, core_ref: (a, i, 0)),
        ),
        out_shape=jax.ShapeDtypeStruct((n_p, hr, c), WIRE_DTYPE),
        compiler_params=pltpu.CompilerParams(dimension_semantics=("parallel", "parallel")),
    )(core, p, q)


def _into_slot(name, w, slots, slot, dtype, tr):
    r, c = w.shape
    assert r % tr == 0

    def body(slot_ref, w_ref, o_ref):
        o_ref[...] = w_ref[...].astype(o_ref.dtype)

    return pl.pallas_call(
        body,
        name=name,
        grid_spec=pltpu.PrefetchScalarGridSpec(
            num_scalar_prefetch=1,
            grid=(r // tr,),
            in_specs=[pl.BlockSpec((tr, c), lambda i, slot_ref: (i, 0))],
            out_specs=pl.BlockSpec((None, tr, c), lambda i, slot_ref: (slot_ref[0], i, 0)),
        ),
        out_shape=jax.ShapeDtypeStruct((slots, r, c), dtype),
        compiler_params=pltpu.CompilerParams(dimension_semantics=("parallel",)),
    )(slot, w)


def _sum_pieces(name, own, rb, place, tr):
    n_p, hr, c = rb.shape
    assert hr % tr == 0

    def body(place_ref, own_ref, *refs):
        chip = place_ref[0]
        acc = None
        for k in range(n_p):
            tile = jnp.where(chip == k, own_ref[...], refs[k][...]).astype(F32)
            acc = tile if acc is None else acc + tile
        refs[n_p][...] = acc

    def landed(k):
        return pl.BlockSpec((None, tr, c), lambda i, place_ref: (jnp.where(place_ref[0] == k, (k + 1) % n_p, k), i, 0))

    return pl.pallas_call(
        body,
        name=name,
        grid_spec=pltpu.PrefetchScalarGridSpec(
            num_scalar_prefetch=1,
            grid=(hr // tr,),
            in_specs=[pl.BlockSpec((None, tr, c), lambda i, place_ref: (place_ref[0], i, 0))]
            + [landed(k) for k in range(n_p)],
            out_specs=pl.BlockSpec((None, tr, c), lambda i, place_ref: (place_ref[1], i, 0)),
        ),
        out_shape=jax.ShapeDtypeStruct((2, hr, c), F32),
        compiler_params=pltpu.CompilerParams(dimension_semantics=("parallel",)),
    )(place, own, *([rb] * n_p))


HBM_SPEC = pl.BlockSpec(memory_space=pl.ANY)


def _place():
    x, y, c = lax.axis_index("x"), lax.axis_index("y"), lax.axis_index("c")
    chips = [(1 - x, y), (x, 1 - y), (1 - x, 1 - y)]
    return x, y, c, chips


def _gather_shards(bufs, split):
    n = len(bufs)
    n_split = sum(split)
    fwd_slot = {a: s for s, a in enumerate([a for a in range(n) if split[a]])}

    def body(*refs):
        outs = refs[n:2 * n]
        send_sems, recv_sems, fsend_sems, frecv_sems = refs[2 * n:]
        x, y, c, chips = _place()
        me = 2 * x + y
        sibling = (x, y, 1 - c)

        def part(a, slot, h):
            if not split[a]:
                return outs[a].at[slot]
            hr = bufs[a].shape[1] // 2
            return outs[a].at[slot, pl.ds(h * hr, hr), :]

        sends = []
        for a in range(n):
            for j, chip in enumerate(chips):
                sends.append(pltpu.make_async_remote_copy(
                    src_ref=part(a, me, c), dst_ref=part(a, me, c),
                    send_sem=send_sems.at[3 * a + j], recv_sem=recv_sems.at[3 * a + j],
                    device_id=(*chip, c), device_id_type=MESH))
        for cp in sends:
            cp.start()
        passed = []
        for a in range(n):
            for j, chip in enumerate(chips):
                landed = part(a, 2 * chip[0] + chip[1], c)
                pltpu.make_async_remote_copy(
                    src_ref=landed, dst_ref=landed, send_sem=send_sems.at[3 * a + j], recv_sem=recv_sems.at[3 * a + j],
                    device_id=(*chip, c), device_id_type=MESH).wait_recv()
                if split[a]:
                    s = 3 * fwd_slot[a] + j
                    fwd = pltpu.make_async_remote_copy(
                        src_ref=landed, dst_ref=landed, send_sem=fsend_sems.at[s], recv_sem=frecv_sems.at[s],
                        device_id=sibling, device_id_type=MESH)
                    fwd.start()
                    passed.append(fwd)
        for a in range(n):
            if split[a]:
                for j, chip in enumerate(chips):
                    s = 3 * fwd_slot[a] + j
                    other = part(a, 2 * chip[0] + chip[1], 1 - c)
                    pltpu.make_async_remote_copy(
                        src_ref=other, dst_ref=other, send_sem=fsend_sems.at[s], recv_sem=frecv_sems.at[s],
                        device_id=sibling, device_id_type=MESH).wait_recv()
        for cp in sends + passed:
            cp.wait_send()

    return pl.pallas_call(
        body,
        name="gather_shards",
        in_specs=[HBM_SPEC] * n,
        out_specs=[HBM_SPEC] * n,
        out_shape=[jax.ShapeDtypeStruct(b.shape, b.dtype) for b in bufs],
        input_output_aliases={a: a for a in range(n)},
        scratch_shapes=[pltpu.SemaphoreType.DMA((3 * n,)), pltpu.SemaphoreType.DMA((3 * n,)),
                        pltpu.SemaphoreType.DMA((3 * n_split,)), pltpu.SemaphoreType.DMA((3 * n_split,))],
    )(*bufs)


def _swap_halves(parts):
    n = len(parts)
    n_p = parts[0].shape[0]

    def body(*refs):
        ins, outs = refs[:n], refs[n:2 * n]
        send_sems, recv_sems = refs[2 * n:]
        x, y, c, _ = _place()
        copies = []
        for a in range(n):
            for p in range(n_p):
                copies.append(pltpu.make_async_remote_copy(
                    src_ref=ins[a].at[p, 1 - c], dst_ref=outs[a].at[p],
                    send_sem=send_sems.at[n_p * a + p], recv_sem=recv_sems.at[n_p * a + p],
                    device_id=(x, y, 1 - c), device_id_type=MESH))
        for cp in copies:
            cp.start()
        for cp in copies:
            cp.wait()

    return pl.pallas_call(
        body,
        name="swap_halves",
        in_specs=[HBM_SPEC] * n,
        out_specs=[HBM_SPEC] * n,
        out_shape=[jax.ShapeDtypeStruct((n_p, *p.shape[2:]), p.dtype) for p in parts],
        scratch_shapes=[pltpu.SemaphoreType.DMA((n_p * n,)), pltpu.SemaphoreType.DMA((n_p * n,))],
    )(*parts)


def _exchange_pieces(sums):
    n = len(sums)

    def body(*refs):
        ins, outs = refs[:n], refs[n:2 * n]
        send_sems, recv_sems = refs[2 * n:]
        x, y, c, chips = _place()
        me = 2 * x + y
        sends = []
        for a in range(n):
            for j, chip in enumerate(chips):
                sends.append(pltpu.make_async_remote_copy(
                    src_ref=ins[a].at[2 * chip[0] + chip[1]], dst_ref=outs[a].at[me],
                    send_sem=send_sems.at[3 * a + j], recv_sem=recv_sems.at[3 * a + j],
                    device_id=(*chip, c), device_id_type=MESH))
        for cp in sends:
            cp.start()
        for a in range(n):
            for j, chip in enumerate(chips):
                slot = outs[a].at[2 * chip[0] + chip[1]]
                pltpu.make_async_remote_copy(
                    src_ref=slot, dst_ref=slot, send_sem=send_sems.at[3 * a + j], recv_sem=recv_sems.at[3 * a + j],
                    device_id=(*chip, c), device_id_type=MESH).wait_recv()
        for cp in sends:
            cp.wait_send()

    return pl.pallas_call(
        body,
        name="exchange_pieces",
        in_specs=[HBM_SPEC] * n,
        out_specs=[HBM_SPEC] * n,
        out_shape=[jax.ShapeDtypeStruct(s.shape, s.dtype) for s in sums],
        scratch_shapes=[pltpu.SemaphoreType.DMA((3 * n,)), pltpu.SemaphoreType.DMA((3 * n,))],
    )(*sums)


def _join_halves(bufs):
    n = len(bufs)

    def body(*refs):
        outs = refs[n:2 * n]
        send_sems, recv_sems = refs[2 * n:]
        x, y, c, _ = _place()
        copies = [pltpu.make_async_remote_copy(
            src_ref=outs[a].at[c], dst_ref=outs[a].at[c], send_sem=send_sems.at[a], recv_sem=recv_sems.at[a],
            device_id=(x, y, 1 - c), device_id_type=MESH) for a in range(n)]
        for cp in copies:
            cp.start()
        for a in range(n):
            other = outs[a].at[1 - c]
            pltpu.make_async_remote_copy(
                src_ref=other, dst_ref=other, send_sem=send_sems.at[a], recv_sem=recv_sems.at[a],
                device_id=(x, y, 1 - c), device_id_type=MESH).wait_recv()
        for cp in copies:
            cp.wait_send()

    return pl.pallas_call(
        body,
        name="join_halves",
        in_specs=[HBM_SPEC] * n,
        out_specs=[HBM_SPEC] * n,
        out_shape=[jax.ShapeDtypeStruct(b.shape, b.dtype) for b in bufs],
        input_output_aliases={a: a for a in range(n)},
        scratch_shapes=[pltpu.SemaphoreType.DMA((n,)), pltpu.SemaphoreType.DMA((n,))],
    )(*bufs)


def _allgather_small(block):
    m_per, n = block.shape

    def body(x_ref, out_ref, send_sems, recv_sems, local_sem):
        x, y, c, chips = _place()
        me, sibling = (x, y, c), (x, y, 1 - c)

        def rows(px, py, pc):
            return out_ref.at[pl.ds((4 * px + 2 * py + pc) * m_per, m_per), :]

        def copy(k, blk, to, src=None):
            return pltpu.make_async_remote_copy(
                src_ref=rows(*blk) if src is None else src, dst_ref=rows(*blk),
                send_sem=send_sems.at[k], recv_sem=recv_sems.at[k], device_id=to, device_id_type=MESH)

        mine = pltpu.make_async_copy(x_ref, rows(*me), local_sem)
        mine.start()
        first = [copy(0, me, sibling, src=x_ref)]
        first += [copy(1 + j, me, (*chip, c), src=x_ref) for j, chip in enumerate(chips)]
        for cp in first:
            cp.start()
        passed = [copy(4 + j, (*chip, c), sibling) for j, chip in enumerate(chips)]
        for j, chip in enumerate(chips):
            copy(1 + j, (*chip, c), me).wait_recv()
            passed[j].start()
        copy(0, sibling, me).wait_recv()
        for j, chip in enumerate(chips):
            copy(4 + j, (*chip, 1 - c), me).wait_recv()
        for cp in first + passed:
            cp.wait_send()
        mine.wait()

    return pl.pallas_call(
        body,
        name="allgather_small",
        out_shape=jax.ShapeDtypeStruct((8 * m_per, n), block.dtype),
        in_specs=[pl.BlockSpec(memory_space=pltpu.VMEM)],
        out_specs=pl.BlockSpec(memory_space=pltpu.VMEM),
        scratch_shapes=[pltpu.SemaphoreType.DMA((7,)), pltpu.SemaphoreType.DMA((7,)), pltpu.SemaphoreType.DMA],
    )(block)


def _sum_blocks(name, gathered, n_blocks):
    r = gathered.shape[0] // n_blocks
    c = gathered.shape[1]

    def body(g_ref, o_ref):
        acc = g_ref[0:r, :]
        for b in range(1, n_blocks):
            acc = acc + g_ref[b * r:(b + 1) * r, :]
        o_ref[...] = acc

    return pl.pallas_call(body, name=name, out_shape=jax.ShapeDtypeStruct((r, c), F32))(gathered)


def _largest_tile(n, cap, mult):
    best = None
    for d in range(mult, min(n, cap) + 1, mult):
        if n % d == 0:
            best = d
    assert best is not None, (n, cap, mult)
    return best


def kernel(x, meta, g_pre_mix, w_in, b_gates, conf_dw_w, conf_dw_b, conf_ln_g, conf_ln_b, conf_w_pw, short_dw_w, short_w_out, w_o, g_post_mix, g_pre_mlp, w_up, w_down, g_post_mlp, loss_target, m_meta, m_g_pre_mix, m_w_in, m_b_gates, m_conf_dw_w, m_conf_dw_b, m_conf_ln_g, m_conf_ln_b, m_conf_w_pw, m_short_dw_w, m_short_w_out, m_w_o, m_g_post_mix, m_g_pre_mlp, m_w_up, m_w_down, m_g_post_mlp, v_meta, v_g_pre_mix, v_w_in, v_b_gates, v_conf_dw_w, v_conf_dw_b, v_conf_ln_g, v_conf_ln_b, v_conf_w_pw, v_short_dw_w, v_short_w_out, v_w_o, v_g_post_mix, v_g_pre_mlp, v_w_up, v_w_down, v_g_post_mlp):
    seq, d = x.shape[1], x.shape[2]
    t = seq + N_META
    d_conf = conf_dw_b.shape[1]
    d_ff = w_up.shape[2] * N_CHIPS
    in_cols = w_in.shape[2] * N_CHIPS
    assert in_cols == 5 * d_conf + 2 * d and d == 2 * d_conf
    cw = d_conf
    core = lax.axis_index("c")
    chip = 2 * lax.axis_index("x") + lax.axis_index("y")

    tr = _largest_tile(t, 64, 16)
    tm = _largest_tile(t, 1024, 16)
    tn_in = _largest_tile(in_cols // N_CHIPS, 1152, LANES)
    tn_d = _largest_tile(d, 1024, LANES)
    tn_ff = _largest_tile(d_ff // N_CHIPS, 1024, LANES)
    tn_pw = _largest_tile(d // N_CHIPS, 512, LANES)

    big = [w_in[0], conf_w_pw[0], short_w_out[0], w_o[0], w_up[0], w_down[0]]
    chip_arr = chip.astype(jnp.int32).reshape(1)
    big_lp = [_into_slot(f"cast_w{a}", w, N_CHIPS, chip_arr, MXU_DTYPE, _largest_tile(w.shape[0], 256, 16))
              for a, w in enumerate(big)]
    small = [_into_slot(f"place_w{a}", w, N_CHIPS, chip_arr, F32, w.shape[0])
             for a, w in enumerate([meta, conf_dw_w[0], short_dw_w[0]])]
    gathered = _gather_shards(small + big_lp, [False] * len(small) + [True] * len(big_lp))
    meta_g, wdw_g, w3_g, wg_in, wg_pw, wg_sout, wg_o, wg_up, wg_down = gathered
    wg_o = wg_o.reshape(d, d)
    wg_down = wg_down.reshape(d_ff, d)
    meta_full = jnp.transpose(meta_g, (1, 0, 2)).reshape(N_META, d)
    wdw = jnp.transpose(wdw_g, (1, 0, 2)).reshape(CONF_KERNEL, d_conf)
    w3 = jnp.transpose(w3_g, (1, 0, 2)).reshape(SHORT_KERNEL, d_conf)

    h0 = jnp.concatenate([meta_full, x[0]], axis=0)
    target = jnp.concatenate([jnp.zeros((N_META, d), F32), loss_target[0]], axis=0)

    def norm_in(i, rows, vecs):
        return [_rms_fwd(rows[0], vecs[0])], []

    (n_lp,) = _rowwise("norm_in", norm_in, [(h0, d, 0)], [g_pre_mix], [(d, MXU_DTYPE)], [], tr)
    proj = _matmul("proj", n_lp, wg_in, kind="nn", tm=tm, tn=tn_in, tk=d, out_dtypes=[F32])
    ac, c3 = _conv_fwd(proj, wdw, conf_dw_b, w3, d_conf)

    def ln_parts(ac_t, ln_g, ln_b):
        mu = jnp.mean(ac_t, axis=-1, keepdims=True)
        xc = ac_t - mu
        rstd = lax.rsqrt(jnp.mean(xc * xc, axis=-1, keepdims=True) + LN_EPS)
        xh = xc * rstd
        return xh, rstd, xh * ln_g + ln_b

    def branch_act(i, rows, vecs):
        ac_t, c3_t, bg_t = rows
        _, _, al = ln_parts(ac_t, vecs[0], vecs[1])
        return [al * _sigmoid(al), bg_t * c3_t], []

    a_act, s_lp = _rowwise("branch_act", branch_act, [(ac, cw, 0), (c3, cw, 0), (proj, cw, 2)],
                           [conf_ln_g, conf_ln_b], [(d_conf, MXU_DTYPE), (d_conf, MXU_DTYPE)], [], tr)
    y_a = _matmul("y_a", a_act, wg_pw, kind="nn", tm=tm, tn=tn_pw, tk=d_conf, out_dtypes=[F32])
    y_b = _matmul("y_b", s_lp, wg_sout, kind="nn", tm=tm, tn=tn_pw, tk=d_conf, out_dtypes=[F32])

    gate_rows = [(proj, cw, 5), (proj, cw, 6), (proj, cw, 7), (proj, cw, 8)]

    def gates_of(rows, b):
        ga = _sigmoid(jnp.concatenate([rows[0], rows[1]], axis=1) + b[:, :d])
        gb = _sigmoid(jnp.concatenate([rows[2], rows[3]], axis=1) + b[:, d:])
        return ga, gb

    def gate(i, rows, vecs):
        ga, gb = gates_of(rows[2:], vecs[0])
        return [ga * rows[0] + gb * rows[1]], []

    (m_lp,) = _rowwise("gate", gate, [(y_a, d, 0), (y_b, d, 0)] + gate_rows, [b_gates], [(d, MXU_DTYPE)], [], tr)
    mix = _matmul("mix", m_lp, wg_o, kind="nn", tm=tm, tn=tn_d, tk=d, out_dtypes=[F32])

    def post_mix(i, rows, vecs):
        h1_t = rows[0] + _rms_fwd(rows[1], vecs[0])
        return [h1_t, _rms_fwd(h1_t, vecs[1])], []

    h1, n2_lp = _rowwise("post_mix", post_mix, [(h0, d, 0), (mix, d, 0)], [g_post_mix, g_pre_mlp],
                         [(d, F32), (d, MXU_DTYPE)], [], tr)
    up, f_lp = _matmul("up", n2_lp, wg_up, kind="nn", tm=tm, tn=tn_ff, tk=d, out_dtypes=[F32, MXU_DTYPE],
                       epilogue=lambda acc: (acc, jnp.square(jnp.maximum(acc, 0.0))))
    dn = _matmul("down", f_lp, wg_down, kind="nn", tm=tm, tn=tn_d, tk=_largest_tile(d_ff, 2048, LANES),
                 out_dtypes=[F32])

    def head(i, rows, vecs):
        h1_t, dn_t, tgt = rows
        y = h1_t + _rms_fwd(dn_t, vecs[0])
        row = i * tr + lax.broadcasted_iota(jnp.int32, (tr, 1), 0)
        err = jnp.where(row >= N_META, y - tgt, 0.0)
        dy = err / d
        d_dn, dg = _rms_bwd(dn_t, vecs[0], dy)
        loss_rows = 0.5 * jnp.mean(err * err, axis=-1, keepdims=True)
        return [dy, d_dn], [dg, jnp.broadcast_to(loss_rows, (tr, LANES))]

    dy, d_dn, dg_post_mlp, loss_vec = _rowwise(
        "head", head, [(h1, d, 0), (dn, d, 0), (target, d, 0)], [g_post_mlp], [(d, F32), (d, MXU_DTYPE)], [d, LANES], tr)
    loss = lax.psum(loss_vec[0, 0], ("x", "y", "c"))

    d_up = _matmul("d_up", d_dn, wg_down, kind="nt", tm=tm, tn=tn_ff, tk=d, out_dtypes=[MXU_DTYPE], extras=[up],
                   epilogue=lambda acc, up_t: (acc * (2.0 * jnp.maximum(up_t, 0.0)),))
    tk_t = t
    gw_down = _matmul("gw_down", f_lp, d_dn, kind="tn", tm=_largest_tile(d_ff, 512, LANES), tn=tn_d, tk=tk_t,
                      out_dtypes=[F32])
    d_n2 = _matmul("d_n2", d_up, wg_up, kind="nt", tm=tm, tn=tn_d, tk=tn_ff, out_dtypes=[F32])
    gw_up = _matmul("gw_up", n2_lp, d_up, kind="tn", tm=_largest_tile(d, 512, LANES), tn=tn_ff, tk=tk_t,
                    out_dtypes=[F32], out_pieces=N_CHIPS)

    def bwd_mid(i, rows, vecs):
        dy_t, dn2_t, h1_t, mix_t = rows
        d_h1a, dg_pre_mlp = _rms_bwd(h1_t, vecs[1], dn2_t)
        d_h1 = dy_t + d_h1a
        d_mix, dg_post_mix = _rms_bwd(mix_t, vecs[0], d_h1)
        return [d_h1, d_mix], [dg_pre_mlp, dg_post_mix]

    d_h1, d_mix, dg_pre_mlp, dg_post_mix = _rowwise(
        "bwd_mid", bwd_mid, [(dy, d, 0), (d_n2, d, 0), (h1, d, 0), (mix, d, 0)], [g_post_mix, g_pre_mlp],
        [(d, F32), (d, MXU_DTYPE)], [d, d], tr)
    d_m = _matmul("d_m", d_mix, wg_o, kind="nt", tm=tm, tn=tn_d, tk=d, out_dtypes=[F32])
    gw_o = _matmul("gw_o", m_lp, d_mix, kind="tn", tm=_largest_tile(d, 512, LANES), tn=tn_d, tk=tk_t, out_dtypes=[F32])

    def gate_bwd(i, rows, vecs):
        dm_t, ya_t, yb_t = rows[:3]
        ga, gb = gates_of(rows[3:], vecs[0])
        d_gpre = jnp.concatenate([dm_t * ya_t * ga * (1.0 - ga), dm_t * yb_t * gb * (1.0 - gb)], axis=1)
        return [dm_t * ga, dm_t * gb, d_gpre], [d_gpre]

    d_ya, d_yb, d_gpre, dg_b_gates = _rowwise(
        "gate_bwd", gate_bwd, [(d_m, d, 0), (y_a, d, 0), (y_b, d, 0)] + gate_rows, [b_gates],
        [(d, MXU_DTYPE), (d, MXU_DTYPE), (2 * d, MXU_DTYPE)], [2 * d], tr)
    d_aact = _matmul("d_aact", d_ya, wg_pw, kind="nt", tm=tm, tn=d_conf, tk=tn_pw, out_dtypes=[F32])
    gw_pw = _matmul("gw_pw", a_act, d_ya, kind="tn", tm=_largest_tile(d_conf, 512, LANES), tn=tn_pw, tk=tk_t,
                    out_dtypes=[F32], out_pieces=N_CHIPS)
    d_s = _matmul("d_s", d_yb, wg_sout, kind="nt", tm=tm, tn=d_conf, tk=tn_pw, out_dtypes=[F32])
    gw_sout = _matmul("gw_sout", s_lp, d_yb, kind="tn", tm=_largest_tile(d_conf, 512, LANES), tn=tn_pw, tk=tk_t,
                      out_dtypes=[F32], out_pieces=N_CHIPS)

    def branch_bwd(i, rows, vecs):
        daact_t, ds_t, ac_t, c3_t, bg_t = rows
        xh, rstd, al = ln_parts(ac_t, vecs[0], vecs[1])
        sg = _sigmoid(al)
        d_al = daact_t * (sg * (1.0 + al * (1.0 - sg)))
        dxh = d_al * vecs[0]
        d_ac = rstd * (dxh - jnp.mean(dxh, axis=-1, keepdims=True) - xh * jnp.mean(dxh * xh, axis=-1, keepdims=True))
        return [d_ac, ds_t * bg_t, ds_t * c3_t], [d_al * xh, d_al, d_ac]

    d_ac, d_c3, d_bg, dg_ln_g, dg_ln_b, dg_dw_b = _rowwise(
        "branch_bwd", branch_bwd, [(d_aact, cw, 0), (d_s, cw, 0), (ac, cw, 0), (c3, cw, 0), (proj, cw, 2)],
        [conf_ln_g, conf_ln_b], [(d_conf, F32), (d_conf, F32), (d_conf, MXU_DTYPE)], [d_conf] * 3, tr)
    d_av, d_ag, d_cg, d_v, dg_wdw, dg_w3 = _conv_bwd(proj, d_ac, d_c3, wdw, w3, d_conf)
    d_proj = jnp.concatenate([d_av, d_ag, d_bg, d_cg, d_v, d_gpre], axis=1)
    d_n = _matmul("d_n", d_proj, wg_in, kind="nt", tm=tm, tn=tn_d, tk=tn_in, out_dtypes=[F32])
    gw_in = _matmul("gw_in", n_lp, d_proj, kind="tn", tm=_largest_tile(d, 512, LANES), tn=tn_in, tk=tk_t,
                    out_dtypes=[F32], out_pieces=N_CHIPS)

    def bwd_in(i, rows, vecs):
        d_h0a, dg = _rms_bwd(rows[2], vecs[0], rows[1])
        return [rows[0] + d_h0a], [dg]

    d_h0, dg_pre_mix = _rowwise("bwd_in", bwd_in, [(d_h1, d, 0), (d_n, d, 0), (h0, d, 0)], [g_pre_mix],
                                [(d, F32)], [d], tr)
    grad_x = d_h0[N_META:][None]

    partial = [gw_in, gw_pw, gw_sout, gw_o.reshape(N_CHIPS, d // N_CHIPS, d), gw_up,
               gw_down.reshape(N_CHIPS, d_ff // N_CHIPS, d)]
    halves = [p.reshape(N_CHIPS, 2, p.shape[1] // 2, p.shape[2]) for p in partial]
    from_sibling = _swap_halves(halves)
    core_arr = core.astype(jnp.int32).reshape(1)
    pair = [_pair_sum(f"pair_sum{a}", p, q, core_arr, _largest_tile(q.shape[1], 128, 16))
            for a, (p, q) in enumerate(zip(halves, from_sibling))]
    pieces = _exchange_pieces(pair)
    place = jnp.stack([chip, core]).astype(jnp.int32)
    reduced = [_sum_pieces(f"sum_pieces{a}", own, rb, place, _largest_tile(rb.shape[1], 128, 16))
               for a, (own, rb) in enumerate(zip(pair, pieces))]
    joined = _join_halves(reduced)
    g_big = [j.reshape(w.shape) for j, w in zip(joined, big)]

    small_w = d_conf
    rep = [dg_pre_mix, dg_b_gates, dg_dw_b, dg_ln_g, dg_ln_b, dg_post_mix, dg_pre_mlp, dg_post_mlp]
    rep_w = [g_pre_mix, b_gates, conf_dw_b, conf_ln_g, conf_ln_b, g_post_mix, g_pre_mlp, g_post_mlp]
    rep_m = [m_g_pre_mix, m_b_gates, m_conf_dw_b, m_conf_ln_g, m_conf_ln_b, m_g_post_mix, m_g_pre_mlp, m_g_post_mlp]
    rep_v = [v_g_pre_mix, v_b_gates, v_conf_dw_b, v_conf_ln_g, v_conf_ln_b, v_g_post_mix, v_g_pre_mlp, v_g_post_mlp]
    col = [dg_wdw, dg_w3, d_h0[:N_META]]
    assert all(a.size % small_w == 0 for a in rep + col)

    def pack(arrs):
        flat = jnp.concatenate([a.reshape(-1, small_w) for a in arrs], axis=0)
        return jnp.pad(flat, ((0, -flat.shape[0] % 8), (0, 0)))

    def unpack(buf, like):
        out, r0 = [], 0
        for a in like:
            nr = a.size // small_w
            out.append(buf[r0:r0 + nr].reshape(a.shape))
            r0 += nr
        return out

    packed = pack(rep + col)
    total = _sum_blocks("sum_small", _allgather_small(packed), 8)
    small_g = unpack(total, rep + col)
    g_rep = small_g[:len(rep)]
    g_wdw_full, g_w3_full, g_meta_full = small_g[len(rep):]
    sc = d_conf // N_CHIPS
    g_wdw = lax.dynamic_slice_in_dim(g_wdw_full, chip * sc, sc, axis=1)
    g_w3 = lax.dynamic_slice_in_dim(g_w3_full, chip * sc, sc, axis=1)
    g_meta = lax.dynamic_slice_in_dim(g_meta_full, chip * (d // N_CHIPS), d // N_CHIPS, axis=1)

    big_m = [m_w_in, m_conf_w_pw, m_short_w_out, m_w_o, m_w_up, m_w_down]
    big_v = [v_w_in, v_conf_w_pw, v_short_w_out, v_w_o, v_w_up, v_w_down]
    big_res = [_adamw(f"adamw_big{a}", w, g, m[0], v[0], _largest_tile(w.shape[0], 128, 8))
               for a, (w, g, m, v) in enumerate(zip(big, g_big, big_m, big_v))]
    rep_pack = [pack(arrs) for arrs in (rep_w, g_rep, rep_m, rep_v)]
    rep_res = [unpack(buf, rep_w) for buf in _adamw("adamw_rep", *rep_pack, rep_pack[0].shape[0])]
    col_res = [_adamw(f"adamw_col{a}", w, g, m, v, w.shape[0]) for a, (w, g, m, v) in enumerate(
        [(meta, g_meta, m_meta, v_meta), (conf_dw_w[0], g_wdw, m_conf_dw_w[0], v_conf_dw_w[0]),
         (short_dw_w[0], g_w3, m_short_dw_w[0], v_short_dw_w[0])])]

    def leaf(q):
        r = lambda a: rep_res[q][a]
        b = lambda a: big_res[a][q][None]
        return [col_res[0][q], r(0), b(0), r(1), col_res[1][q][None], r(2), r(3), r(4), b(1), col_res[2][q][None], b(2),
                b(3), r(5), r(6), b(4), b(5), r(7)]

    return (loss, grad_x, *leaf(0), *leaf(1), *leaf(2), *leaf(3))
```

```python
import functools

import jax
import jax.numpy as jnp
from jax import lax
from jax.experimental import pallas as pl
from jax.experimental.pallas import tpu as pltpu

F32 = jnp.float32
BF16 = jnp.bfloat16
MXU_DTYPE = BF16
WIRE_DTYPE = BF16

N_META = 16
CONF_KERNEL = 31
SHORT_KERNEL = 3
CONV_PAD = 32
RMS_EPS = 1e-6
LN_EPS = 1e-5
ADAM_LR = 0.001
ADAM_B1 = 0.9
ADAM_B2 = 0.999
ADAM_EPS = 1e-08
ADAM_WD = 0.01
ADAM_STEP = 10

N_CHIPS = 4
MESH = pl.DeviceIdType.MESH
LANES = 128


def _sigmoid(z):
    return 1.0 / (1.0 + jnp.exp(-z))


def _matmul(name, a, b, *, kind, tm, tn, tk, out_dtypes, out_pieces=1, epilogue=None, extras=()):
    pieces = b.shape[0] if b.ndim == 3 else 1
    if kind == "nn":
        m, kdim = a.shape
        n = b.shape[-1] * pieces
        dims = (((1,), (0,)), ((), ()))
        a_spec = pl.BlockSpec((tm, tk), lambda i, j, k: (i, k))
        if b.ndim == 2:
            b_spec = pl.BlockSpec((tk, tn), lambda i, j, k: (k, j))
        else:
            npp = b.shape[-1] // tn
            b_spec = pl.BlockSpec((None, tk, tn), lambda i, j, k: (j // npp, k, j % npp))
    elif kind == "nt":
        m, kdim = a.shape
        n = b.shape[-2]
        dims = (((1,), (1,)), ((), ()))
        a_spec = pl.BlockSpec((tm, tk), lambda i, j, k: (i, k))
        if b.ndim == 2:
            b_spec = pl.BlockSpec((tn, tk), lambda i, j, k: (j, k))
        else:
            kpp = b.shape[-1] // tk
            b_spec = pl.BlockSpec((None, tn, tk), lambda i, j, k: (k // kpp, j, k % kpp))
    else:
        kdim, m = a.shape
        n = b.shape[-1]
        dims = (((0,), (0,)), ((), ()))
        a_spec = pl.BlockSpec((tk, tm), lambda i, j, k: (k, i))
        b_spec = pl.BlockSpec((tk, tn), lambda i, j, k: (k, j))
    assert m % tm == 0 and n % tn == 0 and kdim % tk == 0, (name, m, n, kdim, tm, tn, tk)
    nk = kdim // tk
    if out_pieces == 1:
        out_shape = (m, n)
        out_spec = pl.BlockSpec((tm, tn), lambda i, j, k: (i, j))
    else:
        onpp = n // out_pieces // tn
        out_shape = (out_pieces, m, n // out_pieces)
        out_spec = pl.BlockSpec((None, tm, tn), lambda i, j, k: (j // onpp, i, j % onpp))
    n_ex, n_out = len(extras), len(out_dtypes)
    if epilogue is None:
        epilogue = lambda acc: (acc,)

    def body(a_ref, b_ref, *rest):
        ex_refs, o_refs = rest[:n_ex], rest[n_ex:n_ex + n_out]
        prod = lax.dot_general(a_ref[...], b_ref[...], dims, preferred_element_type=F32)

        def finish(acc):
            tiles = epilogue(acc, *[r[...] for r in ex_refs])
            for o_ref, t in zip(o_refs, tiles):
                o_ref[...] = t.astype(o_ref.dtype)

        if nk == 1:
            finish(prod)
        else:
            acc_ref = rest[n_ex + n_out]
            k = pl.program_id(2)

            @pl.when(k == 0)
            def _():
                acc_ref[...] = prod

            @pl.when(jnp.logical_and(k > 0, k < nk - 1))
            def _():
                acc_ref[...] += prod

            @pl.when(k == nk - 1)
            def _():
                finish(acc_ref[...] + prod)

    ex_specs = [pl.BlockSpec((tm, tn), lambda i, j, k: (i, j)) for _ in extras]
    res = pl.pallas_call(
        body,
        name=name,
        grid=(m // tm, n // tn, nk),
        in_specs=[a_spec, b_spec, *ex_specs],
        out_specs=[out_spec] * n_out,
        out_shape=[jax.ShapeDtypeStruct(out_shape, d) for d in out_dtypes],
        scratch_shapes=[pltpu.VMEM((tm, tn), F32)] if nk > 1 else [],
        compiler_params=pltpu.CompilerParams(dimension_semantics=("parallel", "parallel", "arbitrary")),
    )(a, b, *extras)
    return res[0] if n_out == 1 else res


def _rowwise(name, fn, rows, vecs, outs, sums, tr):
    t = rows[0][0].shape[0]
    assert t % tr == 0
    n_r, n_v, n_o, n_s = len(rows), len(vecs), len(outs), len(sums)

    def body(*refs):
        r_in, v_in = refs[:n_r], refs[n_r:n_r + n_v]
        o_refs = refs[n_r + n_v:n_r + n_v + n_o]
        s_refs = refs[n_r + n_v + n_o:]
        i = pl.program_id(0)
        o_tiles, s_tiles = fn(i, [r[...] for r in r_in], [v[...] for v in v_in])
        for o_ref, tile in zip(o_refs, o_tiles):
            o_ref[...] = tile.astype(o_ref.dtype)
        for s_ref, tile in zip(s_refs, s_tiles):
            part = jnp.sum(tile, axis=0, keepdims=True)

            @pl.when(i == 0)
            def _(s_ref=s_ref, part=part):
                s_ref[...] = part

            @pl.when(i > 0)
            def _(s_ref=s_ref, part=part):
                s_ref[...] += part

    def row_spec(width, blk):
        return pl.BlockSpec((tr, width), lambda i: (i, blk))

    res = pl.pallas_call(
        body,
        name=name,
        grid=(t // tr,),
        in_specs=[row_spec(w, blk) for _, w, blk in rows]
        + [pl.BlockSpec(v.shape, lambda i: (0, 0)) for v in vecs],
        out_specs=[pl.BlockSpec((tr, c), lambda i: (i, 0)) for c, _ in outs]
        + [pl.BlockSpec((1, c), lambda i: (0, 0)) for c in sums],
        out_shape=[jax.ShapeDtypeStruct((t, c), d) for c, d in outs]
        + [jax.ShapeDtypeStruct((1, c), F32) for c in sums],
        compiler_params=pltpu.CompilerParams(dimension_semantics=("arbitrary",)),
    )(*[r[0] for r in rows], *vecs)
    return res


def _rms_fwd(x, g):
    r = lax.rsqrt(jnp.mean(x * x, axis=-1, keepdims=True) + RMS_EPS)
    return x * r * g


def _rms_bwd(x, g, dy):
    r = lax.rsqrt(jnp.mean(x * x, axis=-1, keepdims=True) + RMS_EPS)
    xn = x * r
    dxn = dy * g
    dx = r * (dxn - xn * jnp.mean(dxn * xn, axis=-1, keepdims=True))
    return dx, dy * xn


CONV_ROWS = 48
CONV_LANES = 128


def _conv_fwd(proj, wdw, bdw, w3, d_conf):
    t = proj.shape[0]
    cl = CONV_LANES
    nb = d_conf // cl
    nchunk = t // CONV_ROWS
    assert t % CONV_ROWS == 0

    def body(av_ref, ag_ref, cg_ref, v_ref, wdw_ref, bdw_ref, w3_ref, ac_ref, c3_ref, apad, cpad):
        zeros = jnp.zeros((CONV_PAD, cl), F32)
        apad[0:CONV_PAD, :] = zeros
        cpad[0:CONV_PAD, :] = zeros
        apad[CONV_PAD:, :] = av_ref[...] * _sigmoid(ag_ref[...])
        cpad[CONV_PAD:, :] = cg_ref[...] * v_ref[...]

        def chunk(ci, carry):
            base = pl.multiple_of(ci * CONV_ROWS, 8)
            acc = jnp.zeros((CONV_ROWS, cl), F32) + bdw_ref[...]
            for k in range(CONF_KERNEL):
                off = CONV_PAD - (CONF_KERNEL - 1) + k
                acc = acc + apad[pl.ds(base + off, CONV_ROWS), :] * wdw_ref[k:k + 1, :]
            ac_ref[pl.ds(base, CONV_ROWS), :] = acc
            acc3 = jnp.zeros((CONV_ROWS, cl), F32)
            for k in range(SHORT_KERNEL):
                off = CONV_PAD - (SHORT_KERNEL - 1) + k
                acc3 = acc3 + cpad[pl.ds(base + off, CONV_ROWS), :] * w3_ref[k:k + 1, :]
            c3_ref[pl.ds(base, CONV_ROWS), :] = acc3
            return carry

        lax.fori_loop(0, nchunk, chunk, 0)

    def col(blk0):
        return pl.BlockSpec((t, cl), lambda j: (0, blk0 + j))

    return pl.pallas_call(
        body,
        name="conv_fwd",
        grid=(nb,),
        in_specs=[col(0), col(nb), col(3 * nb), col(4 * nb),
                  pl.BlockSpec((CONF_KERNEL, cl), lambda j: (0, j)),
                  pl.BlockSpec((1, cl), lambda j: (0, j)),
                  pl.BlockSpec((SHORT_KERNEL, cl), lambda j: (0, j))],
        out_specs=[pl.BlockSpec((t, cl), lambda j: (0, j))] * 2,
        out_shape=[jax.ShapeDtypeStruct((t, d_conf), F32)] * 2,
        scratch_shapes=[pltpu.VMEM((t + CONV_PAD, cl), F32)] * 2,
        compiler_params=pltpu.CompilerParams(dimension_semantics=("parallel",)),
    )(proj, proj, proj, proj, wdw, bdw, w3)


def _conv_bwd(proj, d_ac, d_c3, wdw, w3, d_conf):
    t = proj.shape[0]
    cl = CONV_LANES
    nb = d_conf // cl
    nchunk = t // CONV_ROWS
    nsub = CONV_ROWS // 8

    def fold(p):
        r = p[0:8]
        for s in range(1, nsub):
            r = r + p[8 * s:8 * s + 8]
        return r

    def body(av_ref, ag_ref, cg_ref, v_ref, dac_ref, dc3_ref, wdw_ref, w3_ref,
             dav_ref, dag_ref, dcg_ref, dv_ref, dwdw_ref, dw3_ref, apad, cpad, dapad, dcpad):
        zeros = jnp.zeros((CONV_PAD, cl), F32)
        apad[0:CONV_PAD, :] = zeros
        cpad[0:CONV_PAD, :] = zeros
        apad[CONV_PAD:, :] = av_ref[...] * _sigmoid(ag_ref[...])
        cpad[CONV_PAD:, :] = cg_ref[...] * v_ref[...]
        dapad[0:t, :] = dac_ref[...]
        dcpad[0:t, :] = dc3_ref[...]
        dapad[t:, :] = zeros
        dcpad[t:, :] = zeros

        def chunk(ci, accs):
            base = pl.multiple_of(ci * CONV_ROWS, 8)
            rows = pl.ds(base, CONV_ROWS)
            da = jnp.zeros((CONV_ROWS, cl), F32)
            for k in range(CONF_KERNEL):
                da = da + dapad[pl.ds(base + (CONF_KERNEL - 1 - k), CONV_ROWS), :] * wdw_ref[k:k + 1, :]
            dcv = jnp.zeros((CONV_ROWS, cl), F32)
            for k in range(SHORT_KERNEL):
                dcv = dcv + dcpad[pl.ds(base + (SHORT_KERNEL - 1 - k), CONV_ROWS), :] * w3_ref[k:k + 1, :]
            av, sg = av_ref[rows, :], _sigmoid(ag_ref[rows, :])
            dav_ref[rows, :] = (da * sg).astype(dav_ref.dtype)
            dag_ref[rows, :] = (da * av * sg * (1.0 - sg)).astype(dag_ref.dtype)
            dcg_ref[rows, :] = (dcv * v_ref[rows, :]).astype(dcg_ref.dtype)
            dv_ref[rows, :] = (dcv * cg_ref[rows, :]).astype(dv_ref.dtype)
            d_out, d_out3 = dac_ref[rows, :], dc3_ref[rows, :]
            new = []
            for k in range(CONF_KERNEL):
                off = CONV_PAD - (CONF_KERNEL - 1) + k
                new.append(accs[k] + fold(d_out * apad[pl.ds(base + off, CONV_ROWS), :]))
            for k in range(SHORT_KERNEL):
                off = CONV_PAD - (SHORT_KERNEL - 1) + k
                new.append(accs[CONF_KERNEL + k] + fold(d_out3 * cpad[pl.ds(base + off, CONV_ROWS), :]))
            return tuple(new)

        init = tuple(jnp.zeros((8, cl), F32) for _ in range(CONF_KERNEL + SHORT_KERNEL))
        accs = lax.fori_loop(0, nchunk, chunk, init)
        for k in range(CONF_KERNEL):
            dwdw_ref[k:k + 1, :] = jnp.sum(accs[k], axis=0, keepdims=True)
        for k in range(SHORT_KERNEL):
            dw3_ref[k:k + 1, :] = jnp.sum(accs[CONF_KERNEL + k], axis=0, keepdims=True)

    def col(blk0):
        return pl.BlockSpec((t, cl), lambda j: (0, blk0 + j))

    own = pl.BlockSpec((t, cl), lambda j: (0, j))
    return pl.pallas_call(
        body,
        name="conv_bwd",
        grid=(nb,),
        in_specs=[col(0), col(nb), col(3 * nb), col(4 * nb), own, own,
                  pl.BlockSpec((CONF_KERNEL, cl), lambda j: (0, j)),
                  pl.BlockSpec((SHORT_KERNEL, cl), lambda j: (0, j))],
        out_specs=[own] * 4 + [pl.BlockSpec((CONF_KERNEL, cl), lambda j: (0, j)),
                               pl.BlockSpec((SHORT_KERNEL, cl), lambda j: (0, j))],
        out_shape=[jax.ShapeDtypeStruct((t, d_conf), MXU_DTYPE)] * 4
        + [jax.ShapeDtypeStruct((CONF_KERNEL, d_conf), F32), jax.ShapeDtypeStruct((SHORT_KERNEL, d_conf), F32)],
        scratch_shapes=[pltpu.VMEM((t + CONV_PAD, cl), F32)] * 4,
        compiler_params=pltpu.CompilerParams(dimension_semantics=("parallel",)),
    )(proj, proj, proj, proj, d_ac, d_c3, wdw, w3)


def _elementwise(name, fn, ins, out_dtypes, tr):
    ins = [(a, ()) if not isinstance(a, tuple) else a for a in ins]
    r, c = ins[0][0].shape[-2:]
    assert r % tr == 0, (name, r, tr)
    n_in = len(ins)

    def body(*refs):
        tiles = fn(*[x[...] for x in refs[:n_in]])
        for o_ref, tile in zip(refs[n_in:], tiles):
            o_ref[...] = tile.astype(o_ref.dtype)

    def spec(lead):
        return pl.BlockSpec((None,) * len(lead) + (tr, c), lambda i: (*lead, i, 0))

    res = pl.pallas_call(
        body,
        name=name,
        grid=(r // tr,),
        in_specs=[spec(lead) for _, lead in ins],
        out_specs=[pl.BlockSpec((tr, c), lambda i: (i, 0))] * len(out_dtypes),
        out_shape=[jax.ShapeDtypeStruct((r, c), d) for d in out_dtypes],
        compiler_params=pltpu.CompilerParams(dimension_semantics=("parallel",)),
    )(*[a for a, _ in ins])
    return res


def _adamw_tiles(w, g, m, v):
    m = ADAM_B1 * m + (1.0 - ADAM_B1) * g
    v = ADAM_B2 * v + (1.0 - ADAM_B2) * jnp.square(g)
    m_hat = m / (1.0 - ADAM_B1 ** ADAM_STEP)
    v_hat = v / (1.0 - ADAM_B2 ** ADAM_STEP)
    delta = -ADAM_LR * (m_hat / (jnp.sqrt(v_hat) + ADAM_EPS) + ADAM_WD * w)
    return g, delta, m, v


def _adamw(name, w, g, m, v, tr):
    shape = w.shape
    flat = [a.reshape(shape[-2:]) if a.ndim > 2 else a for a in (w, g, m, v)]
    res = _elementwise(name, _adamw_tiles, flat, [F32] * 4, tr)
    return [a.reshape(shape) for a in res]


def _pair_sum(name, p, q, core, tr):
    n_p, _, hr, c = p.shape
    assert hr % tr == 0

    def body(core_ref, p_ref, q_ref, o_ref):
        o_ref[...] = (p_ref[...] + q_ref[...]).astype(o_ref.dtype)

    return pl.pallas_call(
        body,
        name=name,
        grid_spec=pltpu.PrefetchScalarGridSpec(
            num_scalar_prefetch=1,
            grid=(n_p, hr // tr),
            in_specs=[pl.BlockSpec((None, None, tr, c), lambda a, i, core_ref: (a, core_ref[0], i, 0)),
                      pl.BlockSpec((None, tr, c), lambda a, i, core_ref: (a, i, 0))],
            out_specs=pl.BlockSpec((None, tr, c), lambda a, i, core_ref: (a, i, 0)),
        ),
        out_shape=jax.ShapeDtypeStruct((n_p, hr, c), WIRE_DTYPE),
        compiler_params=pltpu.CompilerParams(dimension_semantics=("parallel", "parallel")),
    )(core, p, q)


def _into_slot(name, w, slots, slot, dtype, tr):
    r, c = w.shape
    assert r % tr == 0

    def body(slot_ref, w_ref, o_ref):
        o_ref[...] = w_ref[...].astype(o_ref.dtype)

    return pl.pallas_call(
        body,
        name=name,
        grid_spec=pltpu.PrefetchScalarGridSpec(
            num_scalar_prefetch=1,
            grid=(r // tr,),
            in_specs=[pl.BlockSpec((tr, c), lambda i, slot_ref: (i, 0))],
            out_specs=pl.BlockSpec((None, tr, c), lambda i, slot_ref: (slot_ref[0], i, 0)),
        ),
        out_shape=jax.ShapeDtypeStruct((slots, r, c), dtype),
        compiler_params=pltpu.CompilerParams(dimension_semantics=("parallel",)),
    )(slot, w)


def _sum_pieces(name, own, rb, place, tr):
    n_p, hr, c = rb.shape
    assert hr % tr == 0

    def body(place_ref, own_ref, *refs):
        chip = place_ref[0]
        acc = None
        for k in range(n_p):
            tile = jnp.where(chip == k, own_ref[...], refs[k][...]).astype(F32)
            acc = tile if acc is None else acc + tile
        refs[n_p][...] = acc

    def landed(k):
        return pl.BlockSpec((None, tr, c), lambda i, place_ref: (jnp.where(place_ref[0] == k, (k + 1) % n_p, k), i, 0))

    return pl.pallas_call(
        body,
        name=name,
        grid_spec=pltpu.PrefetchScalarGridSpec(
            num_scalar_prefetch=1,
            grid=(hr // tr,),
            in_specs=[pl.BlockSpec((None, tr, c), lambda i, place_ref: (place_ref[0], i, 0))]
            + [landed(k) for k in range(n_p)],
            out_specs=pl.BlockSpec((None, tr, c), lambda i, place_ref: (place_ref[1], i, 0)),
        ),
        out_shape=jax.ShapeDtypeStruct((2, hr, c), F32),
        compiler_params=pltpu.CompilerParams(dimension_semantics=("parallel",)),
    )(place, own, *([rb] * n_p))


HBM_SPEC = pl.BlockSpec(memory_space=pl.ANY)


def _place():
    x, y, c = lax.axis_index("x"), lax.axis_index("y"), lax.axis_index("c")
    chips = [(1 - x, y), (x, 1 - y), (1 - x, 1 - y)]
    return x, y, c, chips


def _gather_shards(bufs, split):
    n = len(bufs)
    n_split = sum(split)
    fwd_slot = {a: s for s, a in enumerate([a for a in range(n) if split[a]])}

    def body(*refs):
        outs = refs[n:2 * n]
        send_sems, recv_sems, fsend_sems, frecv_sems = refs[2 * n:]
        x, y, c, chips = _place()
        me = 2 * x + y
        sibling = (x, y, 1 - c)

        def part(a, slot, h):
            if not split[a]:
                return outs[a].at[slot]
            hr = bufs[a].shape[1] // 2
            return outs[a].at[slot, pl.ds(h * hr, hr), :]

        sends = []
        for a in range(n):
            for j, chip in enumerate(chips):
                sends.append(pltpu.make_async_remote_copy(
                    src_ref=part(a, me, c), dst_ref=part(a, me, c),
                    send_sem=send_sems.at[3 * a + j], recv_sem=recv_sems.at[3 * a + j],
                    device_id=(*chip, c), device_id_type=MESH))
        for cp in sends:
            cp.start()
        passed = []
        for a in range(n):
            for j, chip in enumerate(chips):
                landed = part(a, 2 * chip[0] + chip[1], c)
                pltpu.make_async_remote_copy(
                    src_ref=landed, dst_ref=landed, send_sem=send_sems.at[3 * a + j], recv_sem=recv_sems.at[3 * a + j],
                    device_id=(*chip, c), device_id_type=MESH).wait_recv()
                if split[a]:
                    s = 3 * fwd_slot[a] + j
                    fwd = pltpu.make_async_remote_copy(
                        src_ref=landed, dst_ref=landed, send_sem=fsend_sems.at[s], recv_sem=frecv_sems.at[s],
                        device_id=sibling, device_id_type=MESH)
                    fwd.start()
                    passed.append(fwd)
        for a in range(n):
            if split[a]:
                for j, chip in enumerate(chips):
                    s = 3 * fwd_slot[a] + j
                    other = part(a, 2 * chip[0] + chip[1], 1 - c)
                    pltpu.make_async_remote_copy(
                        src_ref=other, dst_ref=other, send_sem=fsend_sems.at[s], recv_sem=frecv_sems.at[s],
                        device_id=sibling, device_id_type=MESH).wait_recv()
        for cp in sends + passed:
            cp.wait_send()

    return pl.pallas_call(
        body,
        name="gather_shards",
        in_specs=[HBM_SPEC] * n,
        out_specs=[HBM_SPEC] * n,
        out_shape=[jax.ShapeDtypeStruct(b.shape, b.dtype) for b in bufs],
        input_output_aliases={a: a for a in range(n)},
        scratch_shapes=[pltpu.SemaphoreType.DMA((3 * n,)), pltpu.SemaphoreType.DMA((3 * n,)),
                        pltpu.SemaphoreType.DMA((3 * n_split,)), pltpu.SemaphoreType.DMA((3 * n_split,))],
    )(*bufs)


def _swap_halves(name, parts):
    n = len(parts)
    n_p = parts[0].shape[0]

    def body(*refs):
        ins, outs = refs[:n], refs[n:2 * n]
        send_sems, recv_sems = refs[2 * n:]
        x, y, c, _ = _place()
        copies = []
        for a in range(n):
            for p in range(n_p):
                copies.append(pltpu.make_async_remote_copy(
                    src_ref=ins[a].at[p, 1 - c], dst_ref=outs[a].at[p],
                    send_sem=send_sems.at[n_p * a + p], recv_sem=recv_sems.at[n_p * a + p],
                    device_id=(x, y, 1 - c), device_id_type=MESH))
        for cp in copies:
            cp.start()
        for cp in copies:
            cp.wait()

    return pl.pallas_call(
        body,
        name=name,
        in_specs=[HBM_SPEC] * n,
        out_specs=[HBM_SPEC] * n,
        out_shape=[jax.ShapeDtypeStruct((n_p, *p.shape[2:]), p.dtype) for p in parts],
        scratch_shapes=[pltpu.SemaphoreType.DMA((n_p * n,)), pltpu.SemaphoreType.DMA((n_p * n,))],
    )(*parts)


def _exchange_pieces(sums):
    n = len(sums)

    def body(*refs):
        ins, outs = refs[:n], refs[n:2 * n]
        send_sems, recv_sems = refs[2 * n:]
        x, y, c, chips = _place()
        me = 2 * x + y
        sends = []
        for a in range(n):
            for j, chip in enumerate(chips):
                sends.append(pltpu.make_async_remote_copy(
                    src_ref=ins[a].at[2 * chip[0] + chip[1]], dst_ref=outs[a].at[me],
                    send_sem=send_sems.at[3 * a + j], recv_sem=recv_sems.at[3 * a + j],
                    device_id=(*chip, c), device_id_type=MESH))
        for cp in sends:
            cp.start()
        for a in range(n):
            for j, chip in enumerate(chips):
                slot = outs[a].at[2 * chip[0] + chip[1]]
                pltpu.make_async_remote_copy(
                    src_ref=slot, dst_ref=slot, send_sem=send_sems.at[3 * a + j], recv_sem=recv_sems.at[3 * a + j],
                    device_id=(*chip, c), device_id_type=MESH).wait_recv()
        for cp in sends:
            cp.wait_send()

    return pl.pallas_call(
        body,
        name="exchange_pieces",
        in_specs=[HBM_SPEC] * n,
        out_specs=[HBM_SPEC] * n,
        out_shape=[jax.ShapeDtypeStruct(s.shape, s.dtype) for s in sums],
        scratch_shapes=[pltpu.SemaphoreType.DMA((3 * n,)), pltpu.SemaphoreType.DMA((3 * n,))],
    )(*sums)


def _join_halves(name, bufs):
    n = len(bufs)

    def body(*refs):
        outs = refs[n:2 * n]
        send_sems, recv_sems = refs[2 * n:]
        x, y, c, _ = _place()
        copies = [pltpu.make_async_remote_copy(
            src_ref=outs[a].at[c], dst_ref=outs[a].at[c], send_sem=send_sems.at[a], recv_sem=recv_sems.at[a],
            device_id=(x, y, 1 - c), device_id_type=MESH) for a in range(n)]
        for cp in copies:
            cp.start()
        for a in range(n):
            other = outs[a].at[1 - c]
            pltpu.make_async_remote_copy(
                src_ref=other, dst_ref=other, send_sem=send_sems.at[a], recv_sem=recv_sems.at[a],
                device_id=(x, y, 1 - c), device_id_type=MESH).wait_recv()
        for cp in copies:
            cp.wait_send()

    return pl.pallas_call(
        body,
        name=name,
        in_specs=[HBM_SPEC] * n,
        out_specs=[HBM_SPEC] * n,
        out_shape=[jax.ShapeDtypeStruct(b.shape, b.dtype) for b in bufs],
        input_output_aliases={a: a for a in range(n)},
        scratch_shapes=[pltpu.SemaphoreType.DMA((n,)), pltpu.SemaphoreType.DMA((n,))],
    )(*bufs)


HBM_ONLY = pl.BlockSpec(memory_space=pltpu.HBM)
SEM_SPEC = pl.BlockSpec(memory_space=pltpu.SEMAPHORE)
DATAFLOW = pltpu.SideEffectType.DATAFLOW_SIDE_EFFECTING


def _in_hbm(a):
    return pltpu.with_memory_space_constraint(a, pltpu.HBM)


def _shard_part(ref, is_split, slot, h):
    if not is_split:
        return ref.at[slot]
    hr = ref.shape[1] // 2
    return ref.at[slot, pl.ds(h * hr, hr), :]


def _gather_start(bufs, split, groups):
    n, ng = len(bufs), len(groups)

    def body(*refs):
        ins, sems = refs[:n], refs[n:n + 2 * ng]
        x, y, c, chips = _place()
        me = 2 * x + y
        for g, members in enumerate(groups):
            for s, a in enumerate(members):
                mine = _shard_part(ins[a], split[a], me, c)
                for j, chip in enumerate(chips):
                    pltpu.make_async_remote_copy(
                        src_ref=mine, dst_ref=mine, send_sem=sems[2 * g].at[3 * s + j], recv_sem=sems[2 * g + 1].at[3 * s + j],
                        device_id=(*chip, c), device_id_type=MESH).start()

    res = pl.pallas_call(
        body,
        name="gather_start",
        in_specs=[HBM_ONLY] * n,
        out_specs=[SEM_SPEC] * (2 * ng) + [HBM_ONLY] * n,
        out_shape=[pltpu.SemaphoreType.DMA((3 * len(members),)) for members in groups for _ in range(2)]
        + [pltpu.HBM(b.shape, b.dtype) for b in bufs],
        input_output_aliases={a: 2 * ng + a for a in range(n)},
        compiler_params=pltpu.CompilerParams(has_side_effects=DATAFLOW),
    )(*[_in_hbm(b) for b in bufs])
    return [(res[2 * g], res[2 * g + 1]) for g in range(ng)], list(res[2 * ng:])


def _gather_wait(name, bufs, split, sems, after):
    n = len(bufs)

    def body(*refs):
        ins, send_sems, recv_sems = refs[:n], refs[n], refs[n + 1]
        x, y, c, chips = _place()
        me = 2 * x + y
        for s in range(n):
            for j, chip in enumerate(chips):
                copy = pltpu.make_async_remote_copy(
                    src_ref=_shard_part(ins[s], split[s], me, c),
                    dst_ref=_shard_part(ins[s], split[s], 2 * chip[0] + chip[1], c),
                    send_sem=send_sems.at[3 * s + j], recv_sem=recv_sems.at[3 * s + j],
                    device_id=(*chip, c), device_id_type=MESH)
                copy.wait_send()
                copy.wait_recv()

    res = pl.pallas_call(
        body,
        name=name,
        in_specs=[HBM_ONLY] * n + [SEM_SPEC, SEM_SPEC, HBM_SPEC],
        out_specs=[HBM_ONLY] * n,
        out_shape=[pltpu.HBM(b.shape, b.dtype) for b in bufs],
        input_output_aliases={a: a for a in range(n)},
        compiler_params=pltpu.CompilerParams(has_side_effects=DATAFLOW),
    )(*bufs, *sems, after)
    return list(res)


def _pass_halves(name, bufs):
    n = len(bufs)

    def body(*refs):
        outs = refs[n:2 * n]
        send_sems, recv_sems = refs[2 * n:]
        x, y, c, chips = _place()
        sibling = (x, y, 1 - c)

        def copy(a, j, h):
            blk = _shard_part(outs[a], True, 2 * chips[j][0] + chips[j][1], h)
            return pltpu.make_async_remote_copy(
                src_ref=blk, dst_ref=blk, send_sem=send_sems.at[3 * a + j], recv_sem=recv_sems.at[3 * a + j],
                device_id=sibling, device_id_type=MESH)

        sends = [copy(a, j, c) for a in range(n) for j in range(3)]
        for cp in sends:
            cp.start()
        for a in range(n):
            for j in range(3):
                copy(a, j, 1 - c).wait_recv()
        for cp in sends:
            cp.wait_send()

    res = pl.pallas_call(
        body,
        name=name,
        in_specs=[HBM_SPEC] * n,
        out_specs=[HBM_SPEC] * n,
        out_shape=[jax.ShapeDtypeStruct(b.shape, b.dtype) for b in bufs],
        input_output_aliases={a: a for a in range(n)},
        scratch_shapes=[pltpu.SemaphoreType.DMA((3 * n,)), pltpu.SemaphoreType.DMA((3 * n,))],
    )(*bufs)
    return list(res)


def _exchange_start(name, pair):
    def body(pair_ref, land_ref, send_sems, recv_sems, pair_thru, land_thru):
        x, y, c, chips = _place()
        me = 2 * x + y
        for j, chip in enumerate(chips):
            pltpu.make_async_remote_copy(
                src_ref=pair_ref.at[2 * chip[0] + chip[1]], dst_ref=land_ref.at[me],
                send_sem=send_sems.at[j], recv_sem=recv_sems.at[j], device_id=(*chip, c), device_id_type=MESH).start()

    send_sems, recv_sems, pair_thru, land_thru = pl.pallas_call(
        body,
        name=name,
        in_specs=[HBM_ONLY, HBM_ONLY],
        out_specs=[SEM_SPEC, SEM_SPEC, HBM_ONLY, HBM_ONLY],
        out_shape=[pltpu.SemaphoreType.DMA((3,)), pltpu.SemaphoreType.DMA((3,)),
                   pltpu.HBM(pair.shape, pair.dtype), pltpu.HBM(pair.shape, pair.dtype)],
        input_output_aliases={0: 2, 1: 3},
        compiler_params=pltpu.CompilerParams(has_side_effects=DATAFLOW),
    )(_in_hbm(pair), _in_hbm(lax.empty(pair.shape, pair.dtype)))
    return (send_sems, recv_sems), pair_thru, land_thru


def _exchange_wait(name, pair, land, sems, after):
    def body(pair_ref, land_ref, send_sems, recv_sems, *rest):
        x, y, c, chips = _place()
        for j, chip in enumerate(chips):
            k = 2 * chip[0] + chip[1]
            copy = pltpu.make_async_remote_copy(
                src_ref=pair_ref.at[k], dst_ref=land_ref.at[k], send_sem=send_sems.at[j], recv_sem=recv_sems.at[j],
                device_id=(*chip, c), device_id_type=MESH)
            copy.wait_send()
            copy.wait_recv()

    return pl.pallas_call(
        body,
        name=name,
        in_specs=[HBM_ONLY, HBM_ONLY, SEM_SPEC, SEM_SPEC] + [HBM_SPEC] * len(after),
        out_specs=[HBM_ONLY, HBM_ONLY],
        out_shape=[pltpu.HBM(pair.shape, pair.dtype), pltpu.HBM(land.shape, land.dtype)],
        input_output_aliases={0: 0, 1: 1},
        compiler_params=pltpu.CompilerParams(has_side_effects=DATAFLOW),
    )(pair, land, *sems, *after)


def _allgather_small(block):
    m_per, n = block.shape

    def body(x_ref, out_ref, send_sems, recv_sems, local_sem):
        x, y, c, chips = _place()
        me, sibling = (x, y, c), (x, y, 1 - c)

        def rows(px, py, pc):
            return out_ref.at[pl.ds((4 * px + 2 * py + pc) * m_per, m_per), :]

        def copy(k, blk, to, src=None):
            return pltpu.make_async_remote_copy(
                src_ref=rows(*blk) if src is None else src, dst_ref=rows(*blk),
                send_sem=send_sems.at[k], recv_sem=recv_sems.at[k], device_id=to, device_id_type=MESH)

        mine = pltpu.make_async_copy(x_ref, rows(*me), local_sem)
        mine.start()
        first = [copy(0, me, sibling, src=x_ref)]
        first += [copy(1 + j, me, (*chip, c), src=x_ref) for j, chip in enumerate(chips)]
        for cp in first:
            cp.start()
        passed = [copy(4 + j, (*chip, c), sibling) for j, chip in enumerate(chips)]
        for j, chip in enumerate(chips):
            copy(1 + j, (*chip, c), me).wait_recv()
            passed[j].start()
        copy(0, sibling, me).wait_recv()
        for j, chip in enumerate(chips):
            copy(4 + j, (*chip, 1 - c), me).wait_recv()
        for cp in first + passed:
            cp.wait_send()
        mine.wait()

    return pl.pallas_call(
        body,
        name="allgather_small",
        out_shape=jax.ShapeDtypeStruct((8 * m_per, n), block.dtype),
        in_specs=[pl.BlockSpec(memory_space=pltpu.VMEM)],
        out_specs=pl.BlockSpec(memory_space=pltpu.VMEM),
        scratch_shapes=[pltpu.SemaphoreType.DMA((7,)), pltpu.SemaphoreType.DMA((7,)), pltpu.SemaphoreType.DMA],
    )(block)


def _sum_blocks(name, gathered, n_blocks):
    r = gathered.shape[0] // n_blocks
    c = gathered.shape[1]

    def body(g_ref, o_ref):
        acc = g_ref[0:r, :]
        for b in range(1, n_blocks):
            acc = acc + g_ref[b * r:(b + 1) * r, :]
        o_ref[...] = acc

    return pl.pallas_call(body, name=name, out_shape=jax.ShapeDtypeStruct((r, c), F32))(gathered)


def _largest_tile(n, cap, mult):
    best = None
    for d in range(mult, min(n, cap) + 1, mult):
        if n % d == 0:
            best = d
    assert best is not None, (n, cap, mult)
    return best


def kernel(x, meta, g_pre_mix, w_in, b_gates, conf_dw_w, conf_dw_b, conf_ln_g, conf_ln_b, conf_w_pw, short_dw_w, short_w_out, w_o, g_post_mix, g_pre_mlp, w_up, w_down, g_post_mlp, loss_target, m_meta, m_g_pre_mix, m_w_in, m_b_gates, m_conf_dw_w, m_conf_dw_b, m_conf_ln_g, m_conf_ln_b, m_conf_w_pw, m_short_dw_w, m_short_w_out, m_w_o, m_g_post_mix, m_g_pre_mlp, m_w_up, m_w_down, m_g_post_mlp, v_meta, v_g_pre_mix, v_w_in, v_b_gates, v_conf_dw_w, v_conf_dw_b, v_conf_ln_g, v_conf_ln_b, v_conf_w_pw, v_short_dw_w, v_short_w_out, v_w_o, v_g_post_mix, v_g_pre_mlp, v_w_up, v_w_down, v_g_post_mlp):
    seq, d = x.shape[1], x.shape[2]
    t = seq + N_META
    d_conf = conf_dw_b.shape[1]
    d_ff = w_up.shape[2] * N_CHIPS
    in_cols = w_in.shape[2] * N_CHIPS
    assert in_cols == 5 * d_conf + 2 * d and d == 2 * d_conf
    cw = d_conf
    core = lax.axis_index("c")
    chip = 2 * lax.axis_index("x") + lax.axis_index("y")

    tr = _largest_tile(t, 64, 16)
    tm = _largest_tile(t, 1024, 16)
    tn_in = _largest_tile(in_cols // N_CHIPS, 1152, LANES)
    tn_d = _largest_tile(d, 1024, LANES)
    tn_ff = _largest_tile(d_ff // N_CHIPS, 1024, LANES)
    tn_pw = _largest_tile(d // N_CHIPS, 512, LANES)

    big = [w_in[0], conf_w_pw[0], short_w_out[0], w_o[0], w_up[0], w_down[0]]
    chip_arr = chip.astype(jnp.int32).reshape(1)
    big_lp = [_into_slot(f"cast_w{a}", w, N_CHIPS, chip_arr, MXU_DTYPE, _largest_tile(w.shape[0], 256, 16))
              for a, w in enumerate(big)]
    small = [_into_slot(f"place_w{a}", w, N_CHIPS, chip_arr, F32, w.shape[0])
             for a, w in enumerate([meta, conf_dw_w[0], short_dw_w[0]])]
    bufs = small + big_lp
    split = [False] * len(small) + [True] * len(big_lp)
    groups = [[0, 1, 2, 3], [4, 5], [6], [7], [8]]
    gather_sems, flying = _gather_start(bufs, split, groups)

    def arrive(g, after):
        members = groups[g]
        got = _gather_wait(f"gather_wait{g}", [flying[a] for a in members], [split[a] for a in members],
                           gather_sems[g], after)
        whole = [b for b, a in zip(got, members) if not split[a]]
        return whole + _pass_halves(f"gather_pass{g}", [b for b, a in zip(got, members) if split[a]])

    meta_g, wdw_g, w3_g, wg_in = arrive(0, chip_arr)
    meta_full = jnp.transpose(meta_g, (1, 0, 2)).reshape(N_META, d)
    wdw = jnp.transpose(wdw_g, (1, 0, 2)).reshape(CONF_KERNEL, d_conf)
    w3 = jnp.transpose(w3_g, (1, 0, 2)).reshape(SHORT_KERNEL, d_conf)

    h0 = jnp.concatenate([meta_full, x[0]], axis=0)
    target = jnp.concatenate([jnp.zeros((N_META, d), F32), loss_target[0]], axis=0)

    def norm_in(i, rows, vecs):
        return [_rms_fwd(rows[0], vecs[0])], []

    (n_lp,) = _rowwise("norm_in", norm_in, [(h0, d, 0)], [g_pre_mix], [(d, MXU_DTYPE)], [], tr)
    proj = _matmul("proj", n_lp, wg_in, kind="nn", tm=tm, tn=tn_in, tk=d, out_dtypes=[F32])
    ac, c3 = _conv_fwd(proj, wdw, conf_dw_b, w3, d_conf)

    def ln_parts(ac_t, ln_g, ln_b):
        mu = jnp.mean(ac_t, axis=-1, keepdims=True)
        xc = ac_t - mu
        rstd = lax.rsqrt(jnp.mean(xc * xc, axis=-1, keepdims=True) + LN_EPS)
        xh = xc * rstd
        return xh, rstd, xh * ln_g + ln_b

    def branch_act(i, rows, vecs):
        ac_t, c3_t, bg_t = rows
        _, _, al = ln_parts(ac_t, vecs[0], vecs[1])
        return [al * _sigmoid(al), bg_t * c3_t], []

    a_act, s_lp = _rowwise("branch_act", branch_act, [(ac, cw, 0), (c3, cw, 0), (proj, cw, 2)],
                           [conf_ln_g, conf_ln_b], [(d_conf, MXU_DTYPE), (d_conf, MXU_DTYPE)], [], tr)
    wg_pw, wg_sout = arrive(1, a_act)
    y_a = _matmul("y_a", a_act, wg_pw, kind="nn", tm=tm, tn=tn_pw, tk=d_conf, out_dtypes=[F32])
    y_b = _matmul("y_b", s_lp, wg_sout, kind="nn", tm=tm, tn=tn_pw, tk=d_conf, out_dtypes=[F32])

    gate_rows = [(proj, cw, 5), (proj, cw, 6), (proj, cw, 7), (proj, cw, 8)]

    def gates_of(rows, b):
        ga = _sigmoid(jnp.concatenate([rows[0], rows[1]], axis=1) + b[:, :d])
        gb = _sigmoid(jnp.concatenate([rows[2], rows[3]], axis=1) + b[:, d:])
        return ga, gb

    def gate(i, rows, vecs):
        ga, gb = gates_of(rows[2:], vecs[0])
        return [ga * rows[0] + gb * rows[1]], []

    (m_lp,) = _rowwise("gate", gate, [(y_a, d, 0), (y_b, d, 0)] + gate_rows, [b_gates], [(d, MXU_DTYPE)], [], tr)
    wg_o = arrive(2, m_lp)[0].reshape(d, d)
    mix = _matmul("mix", m_lp, wg_o, kind="nn", tm=tm, tn=tn_d, tk=d, out_dtypes=[F32])

    def post_mix(i, rows, vecs):
        h1_t = rows[0] + _rms_fwd(rows[1], vecs[0])
        return [h1_t, _rms_fwd(h1_t, vecs[1])], []

    h1, n2_lp = _rowwise("post_mix", post_mix, [(h0, d, 0), (mix, d, 0)], [g_post_mix, g_pre_mlp],
                         [(d, F32), (d, MXU_DTYPE)], [], tr)
    (wg_up,) = arrive(3, n2_lp)
    up, f_lp = _matmul("up", n2_lp, wg_up, kind="nn", tm=tm, tn=tn_ff, tk=d, out_dtypes=[F32, MXU_DTYPE],
                       epilogue=lambda acc: (acc, jnp.square(jnp.maximum(acc, 0.0))))
    wg_down = arrive(4, f_lp)[0].reshape(d_ff, d)
    dn = _matmul("down", f_lp, wg_down, kind="nn", tm=tm, tn=tn_d, tk=_largest_tile(d_ff, 2048, LANES),
                 out_dtypes=[F32])

    def head(i, rows, vecs):
        h1_t, dn_t, tgt = rows
        y = h1_t + _rms_fwd(dn_t, vecs[0])
        row = i * tr + lax.broadcasted_iota(jnp.int32, (tr, 1), 0)
        err = jnp.where(row >= N_META, y - tgt, 0.0)
        dy = err / d
        d_dn, dg = _rms_bwd(dn_t, vecs[0], dy)
        loss_rows = 0.5 * jnp.mean(err * err, axis=-1, keepdims=True)
        return [dy, d_dn], [dg, jnp.broadcast_to(loss_rows, (tr, LANES))]

    dy, d_dn, dg_post_mlp, loss_vec = _rowwise(
        "head", head, [(h1, d, 0), (dn, d, 0), (target, d, 0)], [g_post_mlp], [(d, F32), (d, MXU_DTYPE)], [d, LANES], tr)
    loss = lax.psum(loss_vec[0, 0], ("x", "y", "c"))

    core_arr = core.astype(jnp.int32).reshape(1)
    place = jnp.stack([chip, core]).astype(jnp.int32)
    in_flight = {}

    def reduce_start(a, gw):
        halves = gw.reshape(N_CHIPS, 2, gw.shape[1] // 2, gw.shape[2])
        (from_sibling,) = _swap_halves(f"swap_halves{a}", [halves])
        pair = _pair_sum(f"pair_sum{a}", halves, from_sibling, core_arr, _largest_tile(from_sibling.shape[1], 128, 16))
        in_flight[a] = _exchange_start(f"exchange_start{a}", pair)

    def reduce_finish(a, after):
        sems, pair, land = in_flight[a]
        pair, land = _exchange_wait(f"exchange_wait{a}", pair, land, sems, after)
        return _sum_pieces(f"sum_pieces{a}", pair, land, place, _largest_tile(land.shape[1], 128, 16))

    d_up = _matmul("d_up", d_dn, wg_down, kind="nt", tm=tm, tn=tn_ff, tk=d, out_dtypes=[MXU_DTYPE], extras=[up],
                   epilogue=lambda acc, up_t: (acc * (2.0 * jnp.maximum(up_t, 0.0)),))
    tk_t = t
    gw_down = _matmul("gw_down", f_lp, d_dn, kind="tn", tm=_largest_tile(d_ff, 512, LANES), tn=tn_d, tk=tk_t,
                      out_dtypes=[F32])
    reduce_start(5, gw_down.reshape(N_CHIPS, d_ff // N_CHIPS, d))
    d_n2 = _matmul("d_n2", d_up, wg_up, kind="nt", tm=tm, tn=tn_d, tk=tn_ff, out_dtypes=[F32])
    gw_up = _matmul("gw_up", n2_lp, d_up, kind="tn", tm=_largest_tile(d, 512, LANES), tn=tn_ff, tk=tk_t,
                    out_dtypes=[F32], out_pieces=N_CHIPS)

    reduce_start(4, gw_up)

    def bwd_mid(i, rows, vecs):
        dy_t, dn2_t, h1_t, mix_t = rows
        d_h1a, dg_pre_mlp = _rms_bwd(h1_t, vecs[1], dn2_t)
        d_h1 = dy_t + d_h1a
        d_mix, dg_post_mix = _rms_bwd(mix_t, vecs[0], d_h1)
        return [d_h1, d_mix], [dg_pre_mlp, dg_post_mix]

    d_h1, d_mix, dg_pre_mlp, dg_post_mix = _rowwise(
        "bwd_mid", bwd_mid, [(dy, d, 0), (d_n2, d, 0), (h1, d, 0), (mix, d, 0)], [g_post_mix, g_pre_mlp],
        [(d, F32), (d, MXU_DTYPE)], [d, d], tr)
    d_m = _matmul("d_m", d_mix, wg_o, kind="nt", tm=tm, tn=tn_d, tk=d, out_dtypes=[F32])
    gw_o = _matmul("gw_o", m_lp, d_mix, kind="tn", tm=_largest_tile(d, 512, LANES), tn=tn_d, tk=tk_t, out_dtypes=[F32])

    reduce_start(3, gw_o.reshape(N_CHIPS, d // N_CHIPS, d))

    def gate_bwd(i, rows, vecs):
        dm_t, ya_t, yb_t = rows[:3]
        ga, gb = gates_of(rows[3:], vecs[0])
        d_gpre = jnp.concatenate([dm_t * ya_t * ga * (1.0 - ga), dm_t * yb_t * gb * (1.0 - gb)], axis=1)
        return [dm_t * ga, dm_t * gb, d_gpre], [d_gpre]

    d_ya, d_yb, d_gpre, dg_b_gates = _rowwise(
        "gate_bwd", gate_bwd, [(d_m, d, 0), (y_a, d, 0), (y_b, d, 0)] + gate_rows, [b_gates],
        [(d, MXU_DTYPE), (d, MXU_DTYPE), (2 * d, MXU_DTYPE)], [2 * d], tr)
    d_aact = _matmul("d_aact", d_ya, wg_pw, kind="nt", tm=tm, tn=d_conf, tk=tn_pw, out_dtypes=[F32])
    gw_pw = _matmul("gw_pw", a_act, d_ya, kind="tn", tm=_largest_tile(d_conf, 512, LANES), tn=tn_pw, tk=tk_t,
                    out_dtypes=[F32], out_pieces=N_CHIPS)
    reduce_start(1, gw_pw)
    d_s = _matmul("d_s", d_yb, wg_sout, kind="nt", tm=tm, tn=d_conf, tk=tn_pw, out_dtypes=[F32])
    gw_sout = _matmul("gw_sout", s_lp, d_yb, kind="tn", tm=_largest_tile(d_conf, 512, LANES), tn=tn_pw, tk=tk_t,
                      out_dtypes=[F32], out_pieces=N_CHIPS)

    reduce_start(2, gw_sout)

    def branch_bwd(i, rows, vecs):
        daact_t, ds_t, ac_t, c3_t, bg_t = rows
        xh, rstd, al = ln_parts(ac_t, vecs[0], vecs[1])
        sg = _sigmoid(al)
        d_al = daact_t * (sg * (1.0 + al * (1.0 - sg)))
        dxh = d_al * vecs[0]
        d_ac = rstd * (dxh - jnp.mean(dxh, axis=-1, keepdims=True) - xh * jnp.mean(dxh * xh, axis=-1, keepdims=True))
        return [d_ac, ds_t * bg_t, ds_t * c3_t], [d_al * xh, d_al, d_ac]

    d_ac, d_c3, d_bg, dg_ln_g, dg_ln_b, dg_dw_b = _rowwise(
        "branch_bwd", branch_bwd, [(d_aact, cw, 0), (d_s, cw, 0), (ac, cw, 0), (c3, cw, 0), (proj, cw, 2)],
        [conf_ln_g, conf_ln_b], [(d_conf, F32), (d_conf, F32), (d_conf, MXU_DTYPE)], [d_conf] * 3, tr)
    d_av, d_ag, d_cg, d_v, dg_wdw, dg_w3 = _conv_bwd(proj, d_ac, d_c3, wdw, w3, d_conf)
    d_proj = jnp.concatenate([d_av, d_ag, d_bg, d_cg, d_v, d_gpre], axis=1)
    d_n = _matmul("d_n", d_proj, wg_in, kind="nt", tm=tm, tn=tn_d, tk=tn_in, out_dtypes=[F32])
    gw_in = _matmul("gw_in", n_lp, d_proj, kind="tn", tm=_largest_tile(d, 512, LANES), tn=tn_in, tk=tk_t,
                    out_dtypes=[F32], out_pieces=N_CHIPS)

    reduce_start(0, gw_in)

    def bwd_in(i, rows, vecs):
        d_h0a, dg = _rms_bwd(rows[2], vecs[0], rows[1])
        return [rows[0] + d_h0a], [dg]

    d_h0, dg_pre_mix = _rowwise("bwd_in", bwd_in, [(d_h1, d, 0), (d_n, d, 0), (h0, d, 0)], [g_pre_mix],
                                [(d, F32)], [d], tr)
    grad_x = d_h0[N_META:][None]

    big_m = [m_w_in, m_conf_w_pw, m_short_w_out, m_w_o, m_w_up, m_w_down]
    big_v = [v_w_in, v_conf_w_pw, v_short_w_out, v_w_o, v_w_up, v_w_down]
    big_res = {}

    def update_big(tag, members, after):
        reduced = []
        for a in members:
            reduced.append(reduce_finish(a, after))
            after = [reduced[-1]]
        joined = _join_halves(f"join_halves_{tag}", reduced)
        for a, j in zip(members, joined):
            big_res[a] = _adamw(f"adamw_big{a}", big[a], j.reshape(big[a].shape), big_m[a][0], big_v[a][0],
                                _largest_tile(big[a].shape[0], 128, 8))

    update_big("early", [5, 4, 3, 1, 2], [d_h0])

    small_w = d_conf
    rep = [dg_pre_mix, dg_b_gates, dg_dw_b, dg_ln_g, dg_ln_b, dg_post_mix, dg_pre_mlp, dg_post_mlp]
    rep_w = [g_pre_mix, b_gates, conf_dw_b, conf_ln_g, conf_ln_b, g_post_mix, g_pre_mlp, g_post_mlp]
    rep_m = [m_g_pre_mix, m_b_gates, m_conf_dw_b, m_conf_ln_g, m_conf_ln_b, m_g_post_mix, m_g_pre_mlp, m_g_post_mlp]
    rep_v = [v_g_pre_mix, v_b_gates, v_conf_dw_b, v_conf_ln_g, v_conf_ln_b, v_g_post_mix, v_g_pre_mlp, v_g_post_mlp]
    col = [dg_wdw, dg_w3, d_h0[:N_META]]
    assert all(a.size % small_w == 0 for a in rep + col)

    def pack(arrs):
        flat = jnp.concatenate([a.reshape(-1, small_w) for a in arrs], axis=0)
        return jnp.pad(flat, ((0, -flat.shape[0] % 8), (0, 0)))

    def unpack(buf, like):
        out, r0 = [], 0
        for a in like:
            nr = a.size // small_w
            out.append(buf[r0:r0 + nr].reshape(a.shape))
            r0 += nr
        return out

    packed = pack(rep + col)
    total = _sum_blocks("sum_small", _allgather_small(packed), 8)
    small_g = unpack(total, rep + col)
    g_rep = small_g[:len(rep)]
    g_wdw_full, g_w3_full, g_meta_full = small_g[len(rep):]
    sc = d_conf // N_CHIPS
    g_wdw = lax.dynamic_slice_in_dim(g_wdw_full, chip * sc, sc, axis=1)
    g_w3 = lax.dynamic_slice_in_dim(g_w3_full, chip * sc, sc, axis=1)
    g_meta = lax.dynamic_slice_in_dim(g_meta_full, chip * (d // N_CHIPS), d // N_CHIPS, axis=1)

    rep_pack = [pack(arrs) for arrs in (rep_w, g_rep, rep_m, rep_v)]
    rep_res = [unpack(buf, rep_w) for buf in _adamw("adamw_rep", *rep_pack, rep_pack[0].shape[0])]
    col_res = [_adamw(f"adamw_col{a}", w, g, m, v, w.shape[0]) for a, (w, g, m, v) in enumerate(
        [(meta, g_meta, m_meta, v_meta), (conf_dw_w[0], g_wdw, m_conf_dw_w[0], v_conf_dw_w[0]),
         (short_dw_w[0], g_w3, m_short_dw_w[0], v_short_dw_w[0])])]

    update_big("late", [0], [col_res[0][1], rep_res[1][0]] + [big_res[a][1] for a in (1, 2, 3, 4, 5)])

    def leaf(q):
        r = lambda a: rep_res[q][a]
        b = lambda a: big_res[a][q][None]
        return [col_res[0][q], r(0), b(0), r(1), col_res[1][q][None], r(2), r(3), r(4), b(1), col_res[2][q][None], b(2),
                b(3), r(5), r(6), b(4), b(5), r(7)]

    return (loss, grad_x, *leaf(0), *leaf(1), *leaf(2), *leaf(3))
```

```python
import functools

import jax
import jax.numpy as jnp
from jax import lax
from jax.experimental import pallas as pl
from jax.experimental.pallas import tpu as pltpu

F32 = jnp.float32
BF16 = jnp.bfloat16
MXU_DTYPE = BF16
WIRE_DTYPE = BF16

N_META = 16
CONF_KERNEL = 31
SHORT_KERNEL = 3
CONV_PAD = 32
RMS_EPS = 1e-6
LN_EPS = 1e-5
ADAM_LR = 0.001
ADAM_B1 = 0.9
ADAM_B2 = 0.999
ADAM_EPS = 1e-08
ADAM_WD = 0.01
ADAM_STEP = 10

N_CHIPS = 4
MESH = pl.DeviceIdType.MESH
LANES = 128


def _sigmoid(z):
    return 1.0 / (1.0 + jnp.exp(-z))


ANY_SPEC = pl.BlockSpec(memory_space=pl.ANY)


def _matmul(name, a, b, *, kind, tm, tn, tk, out_dtypes, out_pieces=1, epilogue=None, extras=(), deps=()):
    pieces = b.shape[0] if b.ndim == 3 else 1
    if kind == "nn":
        m, kdim = a.shape
        n = b.shape[-1] * pieces
        dims = (((1,), (0,)), ((), ()))
        a_spec = pl.BlockSpec((tm, tk), lambda i, j, k: (i, k))
        if b.ndim == 2:
            b_spec = pl.BlockSpec((tk, tn), lambda i, j, k: (k, j))
        else:
            npp = b.shape[-1] // tn
            b_spec = pl.BlockSpec((None, tk, tn), lambda i, j, k: (j // npp, k, j % npp))
    elif kind == "nt":
        m, kdim = a.shape
        n = b.shape[-2]
        dims = (((1,), (1,)), ((), ()))
        a_spec = pl.BlockSpec((tm, tk), lambda i, j, k: (i, k))
        if b.ndim == 2:
            b_spec = pl.BlockSpec((tn, tk), lambda i, j, k: (j, k))
        else:
            kpp = b.shape[-1] // tk
            b_spec = pl.BlockSpec((None, tn, tk), lambda i, j, k: (k // kpp, j, k % kpp))
    else:
        kdim, m = a.shape
        n = b.shape[-1]
        dims = (((0,), (0,)), ((), ()))
        a_spec = pl.BlockSpec((tk, tm), lambda i, j, k: (k, i))
        b_spec = pl.BlockSpec((tk, tn), lambda i, j, k: (k, j))
    assert m % tm == 0 and n % tn == 0 and kdim % tk == 0, (name, m, n, kdim, tm, tn, tk)
    nk = kdim // tk
    if out_pieces == 1:
        out_shape = (m, n)
        out_spec = pl.BlockSpec((tm, tn), lambda i, j, k: (i, j))
    else:
        onpp = n // out_pieces // tn
        out_shape = (out_pieces, m, n // out_pieces)
        out_spec = pl.BlockSpec((None, tm, tn), lambda i, j, k: (j // onpp, i, j % onpp))
    n_ex, n_out, n_in = len(extras), len(out_dtypes), len(extras) + len(deps)
    if epilogue is None:
        epilogue = lambda acc: (acc,)

    def body(a_ref, b_ref, *rest):
        ex_refs, o_refs = rest[:n_ex], rest[n_in:n_in + n_out]
        prod = lax.dot_general(a_ref[...], b_ref[...], dims, preferred_element_type=F32)

        def finish(acc):
            tiles = epilogue(acc, *[r[...] for r in ex_refs])
            for o_ref, t in zip(o_refs, tiles):
                o_ref[...] = t.astype(o_ref.dtype)

        if nk == 1:
            finish(prod)
        else:
            acc_ref = rest[n_in + n_out]
            k = pl.program_id(2)

            @pl.when(k == 0)
            def _():
                acc_ref[...] = prod

            @pl.when(jnp.logical_and(k > 0, k < nk - 1))
            def _():
                acc_ref[...] += prod

            @pl.when(k == nk - 1)
            def _():
                finish(acc_ref[...] + prod)

    ex_specs = [pl.BlockSpec((tm, tn), lambda i, j, k: (i, j)) for _ in extras]
    res = pl.pallas_call(
        body,
        name=name,
        grid=(m // tm, n // tn, nk),
        in_specs=[a_spec, b_spec, *ex_specs] + [ANY_SPEC] * len(deps),
        out_specs=[out_spec] * n_out,
        out_shape=[jax.ShapeDtypeStruct(out_shape, d) for d in out_dtypes],
        scratch_shapes=[pltpu.VMEM((tm, tn), F32)] if nk > 1 else [],
        compiler_params=pltpu.CompilerParams(dimension_semantics=("parallel", "parallel", "arbitrary")),
    )(a, b, *extras, *deps)
    return res[0] if n_out == 1 else res


def _rowwise(name, fn, rows, vecs, outs, sums, tr, deps=()):
    t = rows[0][0].shape[0]
    assert t % tr == 0
    n_r, n_v, n_o, n_s = len(rows), len(vecs), len(outs), len(sums)
    n_in = n_r + n_v + len(deps)

    def body(*refs):
        r_in, v_in = refs[:n_r], refs[n_r:n_r + n_v]
        o_refs = refs[n_in:n_in + n_o]
        s_refs = refs[n_in + n_o:]
        i = pl.program_id(0)
        o_tiles, s_tiles = fn(i, [r[...] for r in r_in], [v[...] for v in v_in])
        for o_ref, tile in zip(o_refs, o_tiles):
            o_ref[...] = tile.astype(o_ref.dtype)
        for s_ref, tile in zip(s_refs, s_tiles):
            part = jnp.sum(tile, axis=0, keepdims=True)

            @pl.when(i == 0)
            def _(s_ref=s_ref, part=part):
                s_ref[...] = part

            @pl.when(i > 0)
            def _(s_ref=s_ref, part=part):
                s_ref[...] += part

    def row_spec(width, blk):
        return pl.BlockSpec((tr, width), lambda i: (i, blk))

    res = pl.pallas_call(
        body,
        name=name,
        grid=(t // tr,),
        in_specs=[row_spec(w, blk) for _, w, blk in rows]
        + [pl.BlockSpec(v.shape, lambda i: (0, 0)) for v in vecs] + [ANY_SPEC] * len(deps),
        out_specs=[pl.BlockSpec((tr, c), lambda i: (i, 0)) for c, _ in outs]
        + [pl.BlockSpec((1, c), lambda i: (0, 0)) for c in sums],
        out_shape=[jax.ShapeDtypeStruct((t, c), d) for c, d in outs]
        + [jax.ShapeDtypeStruct((1, c), F32) for c in sums],
        compiler_params=pltpu.CompilerParams(dimension_semantics=("arbitrary",)),
    )(*[r[0] for r in rows], *vecs, *deps)
    return res


def _rms_fwd(x, g):
    r = lax.rsqrt(jnp.mean(x * x, axis=-1, keepdims=True) + RMS_EPS)
    return x * r * g


def _rms_bwd(x, g, dy):
    r = lax.rsqrt(jnp.mean(x * x, axis=-1, keepdims=True) + RMS_EPS)
    xn = x * r
    dxn = dy * g
    dx = r * (dxn - xn * jnp.mean(dxn * xn, axis=-1, keepdims=True))
    return dx, dy * xn


CONV_ROWS = 48
CONV_LANES = 128


def _conv_fwd(proj, wdw, bdw, w3, d_conf):
    t = proj.shape[0]
    cl = CONV_LANES
    nb = d_conf // cl
    nchunk = t // CONV_ROWS
    assert t % CONV_ROWS == 0

    def body(av_ref, ag_ref, cg_ref, v_ref, wdw_ref, bdw_ref, w3_ref, ac_ref, c3_ref, apad, cpad):
        zeros = jnp.zeros((CONV_PAD, cl), F32)
        apad[0:CONV_PAD, :] = zeros
        cpad[0:CONV_PAD, :] = zeros
        apad[CONV_PAD:, :] = av_ref[...] * _sigmoid(ag_ref[...])
        cpad[CONV_PAD:, :] = cg_ref[...] * v_ref[...]

        def chunk(ci, carry):
            base = pl.multiple_of(ci * CONV_ROWS, 8)
            acc = jnp.zeros((CONV_ROWS, cl), F32) + bdw_ref[...]
            for k in range(CONF_KERNEL):
                off = CONV_PAD - (CONF_KERNEL - 1) + k
                acc = acc + apad[pl.ds(base + off, CONV_ROWS), :] * wdw_ref[k:k + 1, :]
            ac_ref[pl.ds(base, CONV_ROWS), :] = acc
            acc3 = jnp.zeros((CONV_ROWS, cl), F32)
            for k in range(SHORT_KERNEL):
                off = CONV_PAD - (SHORT_KERNEL - 1) + k
                acc3 = acc3 + cpad[pl.ds(base + off, CONV_ROWS), :] * w3_ref[k:k + 1, :]
            c3_ref[pl.ds(base, CONV_ROWS), :] = acc3
            return carry

        lax.fori_loop(0, nchunk, chunk, 0)

    def col(blk0):
        return pl.BlockSpec((t, cl), lambda j: (0, blk0 + j))

    return pl.pallas_call(
        body,
        name="conv_fwd",
        grid=(nb,),
        in_specs=[col(0), col(nb), col(3 * nb), col(4 * nb),
                  pl.BlockSpec((CONF_KERNEL, cl), lambda j: (0, j)),
                  pl.BlockSpec((1, cl), lambda j: (0, j)),
                  pl.BlockSpec((SHORT_KERNEL, cl), lambda j: (0, j))],
        out_specs=[pl.BlockSpec((t, cl), lambda j: (0, j))] * 2,
        out_shape=[jax.ShapeDtypeStruct((t, d_conf), F32)] * 2,
        scratch_shapes=[pltpu.VMEM((t + CONV_PAD, cl), F32)] * 2,
        compiler_params=pltpu.CompilerParams(dimension_semantics=("parallel",)),
    )(proj, proj, proj, proj, wdw, bdw, w3)


def _conv_bwd(proj, d_ac, d_c3, wdw, w3, d_conf):
    t = proj.shape[0]
    cl = CONV_LANES
    nb = d_conf // cl
    nchunk = t // CONV_ROWS
    nsub = CONV_ROWS // 8

    def fold(p):
        r = p[0:8]
        for s in range(1, nsub):
            r = r + p[8 * s:8 * s + 8]
        return r

    def body(av_ref, ag_ref, cg_ref, v_ref, dac_ref, dc3_ref, wdw_ref, w3_ref,
             dav_ref, dag_ref, dcg_ref, dv_ref, dwdw_ref, dw3_ref, apad, cpad, dapad, dcpad):
        zeros = jnp.zeros((CONV_PAD, cl), F32)
        apad[0:CONV_PAD, :] = zeros
        cpad[0:CONV_PAD, :] = zeros
        apad[CONV_PAD:, :] = av_ref[...] * _sigmoid(ag_ref[...])
        cpad[CONV_PAD:, :] = cg_ref[...] * v_ref[...]
        dapad[0:t, :] = dac_ref[...]
        dcpad[0:t, :] = dc3_ref[...]
        dapad[t:, :] = zeros
        dcpad[t:, :] = zeros

        def chunk(ci, accs):
            base = pl.multiple_of(ci * CONV_ROWS, 8)
            rows = pl.ds(base, CONV_ROWS)
            da = jnp.zeros((CONV_ROWS, cl), F32)
            for k in range(CONF_KERNEL):
                da = da + dapad[pl.ds(base + (CONF_KERNEL - 1 - k), CONV_ROWS), :] * wdw_ref[k:k + 1, :]
            dcv = jnp.zeros((CONV_ROWS, cl), F32)
            for k in range(SHORT_KERNEL):
                dcv = dcv + dcpad[pl.ds(base + (SHORT_KERNEL - 1 - k), CONV_ROWS), :] * w3_ref[k:k + 1, :]
            av, sg = av_ref[rows, :], _sigmoid(ag_ref[rows, :])
            dav_ref[rows, :] = (da * sg).astype(dav_ref.dtype)
            dag_ref[rows, :] = (da * av * sg * (1.0 - sg)).astype(dag_ref.dtype)
            dcg_ref[rows, :] = (dcv * v_ref[rows, :]).astype(dcg_ref.dtype)
            dv_ref[rows, :] = (dcv * cg_ref[rows, :]).astype(dv_ref.dtype)
            d_out, d_out3 = dac_ref[rows, :], dc3_ref[rows, :]
            new = []
            for k in range(CONF_KERNEL):
                off = CONV_PAD - (CONF_KERNEL - 1) + k
                new.append(accs[k] + fold(d_out * apad[pl.ds(base + off, CONV_ROWS), :]))
            for k in range(SHORT_KERNEL):
                off = CONV_PAD - (SHORT_KERNEL - 1) + k
                new.append(accs[CONF_KERNEL + k] + fold(d_out3 * cpad[pl.ds(base + off, CONV_ROWS), :]))
            return tuple(new)

        init = tuple(jnp.zeros((8, cl), F32) for _ in range(CONF_KERNEL + SHORT_KERNEL))
        accs = lax.fori_loop(0, nchunk, chunk, init)
        for k in range(CONF_KERNEL):
            dwdw_ref[k:k + 1, :] = jnp.sum(accs[k], axis=0, keepdims=True)
        for k in range(SHORT_KERNEL):
            dw3_ref[k:k + 1, :] = jnp.sum(accs[CONF_KERNEL + k], axis=0, keepdims=True)

    def col(blk0):
        return pl.BlockSpec((t, cl), lambda j: (0, blk0 + j))

    own = pl.BlockSpec((t, cl), lambda j: (0, j))
    return pl.pallas_call(
        body,
        name="conv_bwd",
        grid=(nb,),
        in_specs=[col(0), col(nb), col(3 * nb), col(4 * nb), own, own,
                  pl.BlockSpec((CONF_KERNEL, cl), lambda j: (0, j)),
                  pl.BlockSpec((SHORT_KERNEL, cl), lambda j: (0, j))],
        out_specs=[own] * 4 + [pl.BlockSpec((CONF_KERNEL, cl), lambda j: (0, j)),
                               pl.BlockSpec((SHORT_KERNEL, cl), lambda j: (0, j))],
        out_shape=[jax.ShapeDtypeStruct((t, d_conf), MXU_DTYPE)] * 4
        + [jax.ShapeDtypeStruct((CONF_KERNEL, d_conf), F32), jax.ShapeDtypeStruct((SHORT_KERNEL, d_conf), F32)],
        scratch_shapes=[pltpu.VMEM((t + CONV_PAD, cl), F32)] * 4,
        compiler_params=pltpu.CompilerParams(dimension_semantics=("parallel",)),
    )(proj, proj, proj, proj, d_ac, d_c3, wdw, w3)


def _elementwise(name, fn, ins, out_dtypes, tr):
    ins = [(a, ()) if not isinstance(a, tuple) else a for a in ins]
    r, c = ins[0][0].shape[-2:]
    assert r % tr == 0, (name, r, tr)
    n_in = len(ins)

    def body(*refs):
        tiles = fn(*[x[...] for x in refs[:n_in]])
        for o_ref, tile in zip(refs[n_in:], tiles):
            o_ref[...] = tile.astype(o_ref.dtype)

    def spec(lead):
        return pl.BlockSpec((None,) * len(lead) + (tr, c), lambda i: (*lead, i, 0))

    res = pl.pallas_call(
        body,
        name=name,
        grid=(r // tr,),
        in_specs=[spec(lead) for _, lead in ins],
        out_specs=[pl.BlockSpec((tr, c), lambda i: (i, 0))] * len(out_dtypes),
        out_shape=[jax.ShapeDtypeStruct((r, c), d) for d in out_dtypes],
        compiler_params=pltpu.CompilerParams(dimension_semantics=("parallel",)),
    )(*[a for a, _ in ins])
    return res


def _adamw_tiles(w, g, m, v):
    m = ADAM_B1 * m + (1.0 - ADAM_B1) * g
    v = ADAM_B2 * v + (1.0 - ADAM_B2) * jnp.square(g)
    m_hat = m / (1.0 - ADAM_B1 ** ADAM_STEP)
    v_hat = v / (1.0 - ADAM_B2 ** ADAM_STEP)
    delta = -ADAM_LR * (m_hat / (jnp.sqrt(v_hat) + ADAM_EPS) + ADAM_WD * w)
    return g, delta, m, v


def _adamw(name, w, g, m, v, tr):
    shape = w.shape
    flat = [a.reshape(shape[-2:]) if a.ndim > 2 else a for a in (w, g, m, v)]
    res = _elementwise(name, _adamw_tiles, flat, [F32] * 4, tr)
    return [a.reshape(shape) for a in res]


def _pair_sum(name, p, q, core, tr):
    n_p, _, hr, c = p.shape
    assert hr % tr == 0

    def body(core_ref, p_ref, q_ref, o_ref):
        o_ref[...] = (p_ref[...] + q_ref[...]).astype(o_ref.dtype)

    return pl.pallas_call(
        body,
        name=name,
        grid_spec=pltpu.PrefetchScalarGridSpec(
            num_scalar_prefetch=1,
            grid=(n_p, hr // tr),
            in_specs=[pl.BlockSpec((None, None, tr, c), lambda a, i, core_ref: (a, core_ref[0], i, 0)),
                      pl.BlockSpec((None, tr, c), lambda a, i, core_ref: (a, i, 0))],
            out_specs=pl.BlockSpec((None, tr, c), lambda a, i, core_ref: (a, i, 0)),
        ),
        out_shape=jax.ShapeDtypeStruct((n_p, hr, c), WIRE_DTYPE),
        compiler_params=pltpu.CompilerParams(dimension_semantics=("parallel", "parallel")),
    )(core, p, q)


def _into_slot(name, w, slots, slot, dtype, tr, deps=()):
    r, c = w.shape
    assert r % tr == 0

    def body(slot_ref, w_ref, *rest):
        o_ref = rest[len(deps)]
        o_ref[...] = w_ref[...].astype(o_ref.dtype)

    return pl.pallas_call(
        body,
        name=name,
        grid_spec=pltpu.PrefetchScalarGridSpec(
            num_scalar_prefetch=1,
            grid=(r // tr,),
            in_specs=[pl.BlockSpec((tr, c), lambda i, slot_ref: (i, 0))] + [ANY_SPEC] * len(deps),
            out_specs=pl.BlockSpec((None, tr, c), lambda i, slot_ref: (slot_ref[0], i, 0)),
        ),
        out_shape=jax.ShapeDtypeStruct((slots, r, c), dtype),
        compiler_params=pltpu.CompilerParams(dimension_semantics=("parallel",)),
    )(slot, w, *deps)


def _sum_pieces(name, own, rb, place, tr):
    n_p, hr, c = rb.shape
    assert hr % tr == 0

    def body(place_ref, own_ref, *refs):
        chip = place_ref[0]
        acc = None
        for k in range(n_p):
            tile = jnp.where(chip == k, own_ref[...], refs[k][...]).astype(F32)
            acc = tile if acc is None else acc + tile
        refs[n_p][...] = acc

    def landed(k):
        return pl.BlockSpec((None, tr, c), lambda i, place_ref: (jnp.where(place_ref[0] == k, (k + 1) % n_p, k), i, 0))

    return pl.pallas_call(
        body,
        name=name,
        grid_spec=pltpu.PrefetchScalarGridSpec(
            num_scalar_prefetch=1,
            grid=(hr // tr,),
            in_specs=[pl.BlockSpec((None, tr, c), lambda i, place_ref: (place_ref[0], i, 0))]
            + [landed(k) for k in range(n_p)],
            out_specs=pl.BlockSpec((None, tr, c), lambda i, place_ref: (place_ref[1], i, 0)),
        ),
        out_shape=jax.ShapeDtypeStruct((2, hr, c), F32),
        compiler_params=pltpu.CompilerParams(dimension_semantics=("parallel",)),
    )(place, own, *([rb] * n_p))


HBM_SPEC = pl.BlockSpec(memory_space=pl.ANY)


def _place():
    x, y, c = lax.axis_index("x"), lax.axis_index("y"), lax.axis_index("c")
    chips = [(1 - x, y), (x, 1 - y), (1 - x, 1 - y)]
    return x, y, c, chips


def _gather_shards(bufs, split):
    n = len(bufs)
    n_split = sum(split)
    fwd_slot = {a: s for s, a in enumerate([a for a in range(n) if split[a]])}

    def body(*refs):
        outs = refs[n:2 * n]
        send_sems, recv_sems, fsend_sems, frecv_sems = refs[2 * n:]
        x, y, c, chips = _place()
        me = 2 * x + y
        sibling = (x, y, 1 - c)

        def part(a, slot, h):
            if not split[a]:
                return outs[a].at[slot]
            hr = bufs[a].shape[1] // 2
            return outs[a].at[slot, pl.ds(h * hr, hr), :]

        sends = []
        for a in range(n):
            for j, chip in enumerate(chips):
                sends.append(pltpu.make_async_remote_copy(
                    src_ref=part(a, me, c), dst_ref=part(a, me, c),
                    send_sem=send_sems.at[3 * a + j], recv_sem=recv_sems.at[3 * a + j],
                    device_id=(*chip, c), device_id_type=MESH))
        for cp in sends:
            cp.start()
        passed = []
        for a in range(n):
            for j, chip in enumerate(chips):
                landed = part(a, 2 * chip[0] + chip[1], c)
                pltpu.make_async_remote_copy(
                    src_ref=landed, dst_ref=landed, send_sem=send_sems.at[3 * a + j], recv_sem=recv_sems.at[3 * a + j],
                    device_id=(*chip, c), device_id_type=MESH).wait_recv()
                if split[a]:
                    s = 3 * fwd_slot[a] + j
                    fwd = pltpu.make_async_remote_copy(
                        src_ref=landed, dst_ref=landed, send_sem=fsend_sems.at[s], recv_sem=frecv_sems.at[s],
                        device_id=sibling, device_id_type=MESH)
                    fwd.start()
                    passed.append(fwd)
        for a in range(n):
            if split[a]:
                for j, chip in enumerate(chips):
                    s = 3 * fwd_slot[a] + j
                    other = part(a, 2 * chip[0] + chip[1], 1 - c)
                    pltpu.make_async_remote_copy(
                        src_ref=other, dst_ref=other, send_sem=fsend_sems.at[s], recv_sem=frecv_sems.at[s],
                        device_id=sibling, device_id_type=MESH).wait_recv()
        for cp in sends + passed:
            cp.wait_send()

    return pl.pallas_call(
        body,
        name="gather_shards",
        in_specs=[HBM_SPEC] * n,
        out_specs=[HBM_SPEC] * n,
        out_shape=[jax.ShapeDtypeStruct(b.shape, b.dtype) for b in bufs],
        input_output_aliases={a: a for a in range(n)},
        scratch_shapes=[pltpu.SemaphoreType.DMA((3 * n,)), pltpu.SemaphoreType.DMA((3 * n,)),
                        pltpu.SemaphoreType.DMA((3 * n_split,)), pltpu.SemaphoreType.DMA((3 * n_split,))],
    )(*bufs)


def _swap_halves(name, parts):
    n = len(parts)
    n_p = parts[0].shape[0]

    def body(*refs):
        ins, outs = refs[:n], refs[n:2 * n]
        send_sems, recv_sems = refs[2 * n:]
        x, y, c, _ = _place()
        copies = []
        for a in range(n):
            for p in range(n_p):
                copies.append(pltpu.make_async_remote_copy(
                    src_ref=ins[a].at[p, 1 - c], dst_ref=outs[a].at[p],
                    send_sem=send_sems.at[n_p * a + p], recv_sem=recv_sems.at[n_p * a + p],
                    device_id=(x, y, 1 - c), device_id_type=MESH))
        for cp in copies:
            cp.start()
        for cp in copies:
            cp.wait()

    return pl.pallas_call(
        body,
        name=name,
        in_specs=[HBM_SPEC] * n,
        out_specs=[HBM_SPEC] * n,
        out_shape=[jax.ShapeDtypeStruct((n_p, *p.shape[2:]), p.dtype) for p in parts],
        scratch_shapes=[pltpu.SemaphoreType.DMA((n_p * n,)), pltpu.SemaphoreType.DMA((n_p * n,))],
    )(*parts)


def _exchange_pieces(sums):
    n = len(sums)

    def body(*refs):
        ins, outs = refs[:n], refs[n:2 * n]
        send_sems, recv_sems = refs[2 * n:]
        x, y, c, chips = _place()
        me = 2 * x + y
        sends = []
        for a in range(n):
            for j, chip in enumerate(chips):
                sends.append(pltpu.make_async_remote_copy(
                    src_ref=ins[a].at[2 * chip[0] + chip[1]], dst_ref=outs[a].at[me],
                    send_sem=send_sems.at[3 * a + j], recv_sem=recv_sems.at[3 * a + j],
                    device_id=(*chip, c), device_id_type=MESH))
        for cp in sends:
            cp.start()
        for a in range(n):
            for j, chip in enumerate(chips):
                slot = outs[a].at[2 * chip[0] + chip[1]]
                pltpu.make_async_remote_copy(
                    src_ref=slot, dst_ref=slot, send_sem=send_sems.at[3 * a + j], recv_sem=recv_sems.at[3 * a + j],
                    device_id=(*chip, c), device_id_type=MESH).wait_recv()
        for cp in sends:
            cp.wait_send()

    return pl.pallas_call(
        body,
        name="exchange_pieces",
        in_specs=[HBM_SPEC] * n,
        out_specs=[HBM_SPEC] * n,
        out_shape=[jax.ShapeDtypeStruct(s.shape, s.dtype) for s in sums],
        scratch_shapes=[pltpu.SemaphoreType.DMA((3 * n,)), pltpu.SemaphoreType.DMA((3 * n,))],
    )(*sums)


def _join_halves(name, bufs):
    n = len(bufs)

    def body(*refs):
        outs = refs[n:2 * n]
        send_sems, recv_sems = refs[2 * n:]
        x, y, c, _ = _place()
        copies = [pltpu.make_async_remote_copy(
            src_ref=outs[a].at[c], dst_ref=outs[a].at[c], send_sem=send_sems.at[a], recv_sem=recv_sems.at[a],
            device_id=(x, y, 1 - c), device_id_type=MESH) for a in range(n)]
        for cp in copies:
            cp.start()
        for a in range(n):
            other = outs[a].at[1 - c]
            pltpu.make_async_remote_copy(
                src_ref=other, dst_ref=other, send_sem=send_sems.at[a], recv_sem=recv_sems.at[a],
                device_id=(x, y, 1 - c), device_id_type=MESH).wait_recv()
        for cp in copies:
            cp.wait_send()

    return pl.pallas_call(
        body,
        name=name,
        in_specs=[HBM_SPEC] * n,
        out_specs=[HBM_SPEC] * n,
        out_shape=[jax.ShapeDtypeStruct(b.shape, b.dtype) for b in bufs],
        input_output_aliases={a: a for a in range(n)},
        scratch_shapes=[pltpu.SemaphoreType.DMA((n,)), pltpu.SemaphoreType.DMA((n,))],
    )(*bufs)


HBM_ONLY = pl.BlockSpec(memory_space=pltpu.HBM)
SEM_SPEC = pl.BlockSpec(memory_space=pltpu.SEMAPHORE)
DATAFLOW = pltpu.SideEffectType.DATAFLOW_SIDE_EFFECTING


def _in_hbm(a):
    return pltpu.with_memory_space_constraint(a, pltpu.HBM)


def _shard_part(ref, is_split, slot, h):
    if not is_split:
        return ref.at[slot]
    hr = ref.shape[1] // 2
    return ref.at[slot, pl.ds(h * hr, hr), :]


TOKEN = jax.ShapeDtypeStruct((8, LANES), F32)
VMEM_SPEC = pl.BlockSpec(memory_space=pltpu.VMEM)


def _gather_start(name, bufs, split, groups, deps=()):
    n, ng = len(bufs), len(groups)

    def body(*refs):
        ins, sems = refs[:n], refs[n + len(deps):n + len(deps) + 2 * ng]
        refs[-1][...] = jnp.zeros(TOKEN.shape, TOKEN.dtype)
        x, y, c, chips = _place()
        me = 2 * x + y
        for g, members in enumerate(groups):
            for s, a in enumerate(members):
                mine = _shard_part(ins[a], split[a], me, c)
                for j, chip in enumerate(chips):
                    pltpu.make_async_remote_copy(
                        src_ref=mine, dst_ref=mine, send_sem=sems[2 * g].at[3 * s + j], recv_sem=sems[2 * g + 1].at[3 * s + j],
                        device_id=(*chip, c), device_id_type=MESH).start()

    res = pl.pallas_call(
        body,
        name=name,
        in_specs=[HBM_ONLY] * n + [ANY_SPEC] * len(deps),
        out_specs=[SEM_SPEC] * (2 * ng) + [HBM_ONLY] * n + [VMEM_SPEC],
        out_shape=[pltpu.SemaphoreType.DMA((3 * len(members),)) for members in groups for _ in range(2)]
        + [pltpu.HBM(b.shape, b.dtype) for b in bufs] + [TOKEN],
        input_output_aliases={a: 2 * ng + a for a in range(n)},
        compiler_params=pltpu.CompilerParams(has_side_effects=DATAFLOW),
    )(*[_in_hbm(b) for b in bufs], *deps)
    return [(res[2 * g], res[2 * g + 1]) for g in range(ng)], list(res[2 * ng:2 * ng + n]), res[-1]


def _gather_wait(name, bufs, split, sems, after):
    n = len(bufs)

    def body(*refs):
        ins, send_sems, recv_sems = refs[:n], refs[n], refs[n + 1]
        x, y, c, chips = _place()
        me = 2 * x + y
        for s in range(n):
            for j, chip in enumerate(chips):
                copy = pltpu.make_async_remote_copy(
                    src_ref=_shard_part(ins[s], split[s], me, c),
                    dst_ref=_shard_part(ins[s], split[s], 2 * chip[0] + chip[1], c),
                    send_sem=send_sems.at[3 * s + j], recv_sem=recv_sems.at[3 * s + j],
                    device_id=(*chip, c), device_id_type=MESH)
                copy.wait_send()
                copy.wait_recv()

    res = pl.pallas_call(
        body,
        name=name,
        in_specs=[HBM_ONLY] * n + [SEM_SPEC, SEM_SPEC] + [ANY_SPEC] * len(after),
        out_specs=[HBM_ONLY] * n,
        out_shape=[pltpu.HBM(b.shape, b.dtype) for b in bufs],
        input_output_aliases={a: a for a in range(n)},
        compiler_params=pltpu.CompilerParams(has_side_effects=DATAFLOW),
    )(*bufs, *sems, *after)
    return list(res)


def _pass_halves(name, bufs):
    n = len(bufs)

    def body(*refs):
        outs = refs[n:2 * n]
        send_sems, recv_sems = refs[2 * n:]
        x, y, c, chips = _place()
        sibling = (x, y, 1 - c)

        def copy(a, j, h):
            blk = _shard_part(outs[a], True, 2 * chips[j][0] + chips[j][1], h)
            return pltpu.make_async_remote_copy(
                src_ref=blk, dst_ref=blk, send_sem=send_sems.at[3 * a + j], recv_sem=recv_sems.at[3 * a + j],
                device_id=sibling, device_id_type=MESH)

        sends = [copy(a, j, c) for a in range(n) for j in range(3)]
        for cp in sends:
            cp.start()
        for a in range(n):
            for j in range(3):
                copy(a, j, 1 - c).wait_recv()
        for cp in sends:
            cp.wait_send()

    res = pl.pallas_call(
        body,
        name=name,
        in_specs=[HBM_SPEC] * n,
        out_specs=[HBM_SPEC] * n,
        out_shape=[jax.ShapeDtypeStruct(b.shape, b.dtype) for b in bufs],
        input_output_aliases={a: a for a in range(n)},
        scratch_shapes=[pltpu.SemaphoreType.DMA((3 * n,)), pltpu.SemaphoreType.DMA((3 * n,))],
    )(*bufs)
    return list(res)


def _exchange_start(name, pair):
    def body(pair_ref, land_ref, send_sems, recv_sems, pair_thru, land_thru, token):
        x, y, c, chips = _place()
        me = 2 * x + y
        for j, chip in enumerate(chips):
            pltpu.make_async_remote_copy(
                src_ref=pair_ref.at[2 * chip[0] + chip[1]], dst_ref=land_ref.at[me],
                send_sem=send_sems.at[j], recv_sem=recv_sems.at[j], device_id=(*chip, c), device_id_type=MESH).start()
        token[...] = jnp.zeros(TOKEN.shape, TOKEN.dtype)

    send_sems, recv_sems, pair_thru, land_thru, token = pl.pallas_call(
        body,
        name=name,
        in_specs=[HBM_ONLY, HBM_ONLY],
        out_specs=[SEM_SPEC, SEM_SPEC, HBM_ONLY, HBM_ONLY, VMEM_SPEC],
        out_shape=[pltpu.SemaphoreType.DMA((3,)), pltpu.SemaphoreType.DMA((3,)),
                   pltpu.HBM(pair.shape, pair.dtype), pltpu.HBM(pair.shape, pair.dtype), TOKEN],
        input_output_aliases={0: 2, 1: 3},
        compiler_params=pltpu.CompilerParams(has_side_effects=DATAFLOW),
    )(_in_hbm(pair), _in_hbm(lax.empty(pair.shape, pair.dtype)))
    return (send_sems, recv_sems), pair_thru, land_thru, token


def _swap_start(name, halves):
    n_p, _, hr, cols = halves.shape
    land_shape = (n_p, hr, cols)

    def body(halves_ref, land_ref, send_sems, recv_sems, halves_thru, land_thru, token):
        x, y, c, _ = _place()
        for p in range(n_p):
            pltpu.make_async_remote_copy(
                src_ref=halves_ref.at[p, 1 - c], dst_ref=land_ref.at[p], send_sem=send_sems.at[p], recv_sem=recv_sems.at[p],
                device_id=(x, y, 1 - c), device_id_type=MESH).start()
        token[...] = jnp.zeros(TOKEN.shape, TOKEN.dtype)

    send_sems, recv_sems, halves_thru, land_thru, token = pl.pallas_call(
        body,
        name=name,
        in_specs=[HBM_ONLY, HBM_ONLY],
        out_specs=[SEM_SPEC, SEM_SPEC, HBM_ONLY, HBM_ONLY, VMEM_SPEC],
        out_shape=[pltpu.SemaphoreType.DMA((n_p,)), pltpu.SemaphoreType.DMA((n_p,)),
                   pltpu.HBM(halves.shape, halves.dtype), pltpu.HBM(land_shape, halves.dtype), TOKEN],
        input_output_aliases={0: 2, 1: 3},
        compiler_params=pltpu.CompilerParams(has_side_effects=DATAFLOW),
    )(_in_hbm(halves), _in_hbm(lax.empty(land_shape, halves.dtype)))
    return (send_sems, recv_sems), halves_thru, land_thru, token


def _swap_wait(name, halves, land, sems, after):
    n_p = halves.shape[0]

    def body(halves_ref, land_ref, send_sems, recv_sems, *rest):
        x, y, c, _ = _place()
        for p in range(n_p):
            copy = pltpu.make_async_remote_copy(
                src_ref=halves_ref.at[p, 1 - c], dst_ref=land_ref.at[p], send_sem=send_sems.at[p], recv_sem=recv_sems.at[p],
                device_id=(x, y, 1 - c), device_id_type=MESH)
            copy.wait_send()
            copy.wait_recv()

    return pl.pallas_call(
        body,
        name=name,
        in_specs=[HBM_ONLY, HBM_ONLY, SEM_SPEC, SEM_SPEC] + [ANY_SPEC] * len(after),
        out_specs=[HBM_ONLY, HBM_ONLY],
        out_shape=[pltpu.HBM(halves.shape, halves.dtype), pltpu.HBM(land.shape, land.dtype)],
        input_output_aliases={0: 0, 1: 1},
        compiler_params=pltpu.CompilerParams(has_side_effects=DATAFLOW),
    )(halves, land, *sems, *after)


def _exchange_wait(name, pair, land, sems, after):
    def body(pair_ref, land_ref, send_sems, recv_sems, *rest):
        x, y, c, chips = _place()
        for j, chip in enumerate(chips):
            k = 2 * chip[0] + chip[1]
            copy = pltpu.make_async_remote_copy(
                src_ref=pair_ref.at[k], dst_ref=land_ref.at[k], send_sem=send_sems.at[j], recv_sem=recv_sems.at[j],
                device_id=(*chip, c), device_id_type=MESH)
            copy.wait_send()
            copy.wait_recv()

    return pl.pallas_call(
        body,
        name=name,
        in_specs=[HBM_ONLY, HBM_ONLY, SEM_SPEC, SEM_SPEC] + [HBM_SPEC] * len(after),
        out_specs=[HBM_ONLY, HBM_ONLY],
        out_shape=[pltpu.HBM(pair.shape, pair.dtype), pltpu.HBM(land.shape, land.dtype)],
        input_output_aliases={0: 0, 1: 1},
        compiler_params=pltpu.CompilerParams(has_side_effects=DATAFLOW),
    )(pair, land, *sems, *after)


def _allgather_small(block):
    m_per, n = block.shape

    def body(x_ref, out_ref, send_sems, recv_sems, local_sem):
        x, y, c, chips = _place()
        me, sibling = (x, y, c), (x, y, 1 - c)

        def rows(px, py, pc):
            return out_ref.at[pl.ds((4 * px + 2 * py + pc) * m_per, m_per), :]

        def copy(k, blk, to, src=None):
            return pltpu.make_async_remote_copy(
                src_ref=rows(*blk) if src is None else src, dst_ref=rows(*blk),
                send_sem=send_sems.at[k], recv_sem=recv_sems.at[k], device_id=to, device_id_type=MESH)

        mine = pltpu.make_async_copy(x_ref, rows(*me), local_sem)
        mine.start()
        first = [copy(0, me, sibling, src=x_ref)]
        first += [copy(1 + j, me, (*chip, c), src=x_ref) for j, chip in enumerate(chips)]
        for cp in first:
            cp.start()
        passed = [copy(4 + j, (*chip, c), sibling) for j, chip in enumerate(chips)]
        for j, chip in enumerate(chips):
            copy(1 + j, (*chip, c), me).wait_recv()
            passed[j].start()
        copy(0, sibling, me).wait_recv()
        for j, chip in enumerate(chips):
            copy(4 + j, (*chip, 1 - c), me).wait_recv()
        for cp in first + passed:
            cp.wait_send()
        mine.wait()

    return pl.pallas_call(
        body,
        name="allgather_small",
        out_shape=jax.ShapeDtypeStruct((8 * m_per, n), block.dtype),
        in_specs=[pl.BlockSpec(memory_space=pltpu.VMEM)],
        out_specs=pl.BlockSpec(memory_space=pltpu.VMEM),
        scratch_shapes=[pltpu.SemaphoreType.DMA((7,)), pltpu.SemaphoreType.DMA((7,)), pltpu.SemaphoreType.DMA],
    )(block)


def _sum_blocks(name, gathered, n_blocks):
    r = gathered.shape[0] // n_blocks
    c = gathered.shape[1]

    def body(g_ref, o_ref):
        acc = g_ref[0:r, :]
        for b in range(1, n_blocks):
            acc = acc + g_ref[b * r:(b + 1) * r, :]
        o_ref[...] = acc

    return pl.pallas_call(body, name=name, out_shape=jax.ShapeDtypeStruct((r, c), F32))(gathered)


def _largest_tile(n, cap, mult):
    best = None
    for d in range(mult, min(n, cap) + 1, mult):
        if n % d == 0:
            best = d
    assert best is not None, (n, cap, mult)
    return best


def kernel(x, meta, g_pre_mix, w_in, b_gates, conf_dw_w, conf_dw_b, conf_ln_g, conf_ln_b, conf_w_pw, short_dw_w, short_w_out, w_o, g_post_mix, g_pre_mlp, w_up, w_down, g_post_mlp, loss_target, m_meta, m_g_pre_mix, m_w_in, m_b_gates, m_conf_dw_w, m_conf_dw_b, m_conf_ln_g, m_conf_ln_b, m_conf_w_pw, m_short_dw_w, m_short_w_out, m_w_o, m_g_post_mix, m_g_pre_mlp, m_w_up, m_w_down, m_g_post_mlp, v_meta, v_g_pre_mix, v_w_in, v_b_gates, v_conf_dw_w, v_conf_dw_b, v_conf_ln_g, v_conf_ln_b, v_conf_w_pw, v_short_dw_w, v_short_w_out, v_w_o, v_g_post_mix, v_g_pre_mlp, v_w_up, v_w_down, v_g_post_mlp):
    seq, d = x.shape[1], x.shape[2]
    t = seq + N_META
    d_conf = conf_dw_b.shape[1]
    d_ff = w_up.shape[2] * N_CHIPS
    in_cols = w_in.shape[2] * N_CHIPS
    assert in_cols == 5 * d_conf + 2 * d and d == 2 * d_conf
    cw = d_conf
    core = lax.axis_index("c")
    chip = 2 * lax.axis_index("x") + lax.axis_index("y")

    tr = _largest_tile(t, 64, 16)
    tm = _largest_tile(t, 1024, 16)
    tn_in = _largest_tile(in_cols // N_CHIPS, 1152, LANES)
    tn_d = _largest_tile(d, 1024, LANES)
    tn_ff = _largest_tile(d_ff // N_CHIPS, 1024, LANES)
    tn_pw = _largest_tile(d // N_CHIPS, 512, LANES)

    big = [w_in[0], conf_w_pw[0], short_w_out[0], w_o[0], w_up[0], w_down[0]]
    chip_arr = chip.astype(jnp.int32).reshape(1)

    def cast(a, deps):
        return _into_slot(f"cast_w{a}", big[a], N_CHIPS, chip_arr, MXU_DTYPE, _largest_tile(big[a].shape[0], 256, 16), deps)

    small = [_into_slot(f"place_w{a}", w, N_CHIPS, chip_arr, F32, w.shape[0])
             for a, w in enumerate([meta, conf_dw_w[0], short_dw_w[0]])]
    sems_small, fly_small, tok_small = _gather_start("gather_start_small", small, [False] * 3, [[0, 1, 2]])
    sems_in, fly_in, tok_in = _gather_start("gather_start_in", [cast(0, [tok_small])], [True], [[0]])
    rest_groups = [[0, 1], [2], [3], [4]]
    sems_rest, fly_rest, tok_rest = _gather_start(
        "gather_start_rest", [cast(a, [tok_in]) for a in range(1, 6)], [True] * 5, rest_groups)

    def arrive(g, after):
        members = rest_groups[g]
        got = _gather_wait(f"gather_wait_rest{g}", [fly_rest[a] for a in members], [True] * len(members),
                           sems_rest[g], after)
        return _pass_halves(f"gather_pass_rest{g}", got)

    meta_g, wdw_g, w3_g = _gather_wait("gather_wait_small", fly_small, [False] * 3, sems_small[0], [tok_in])
    meta_full = jnp.transpose(meta_g, (1, 0, 2)).reshape(N_META, d)
    wdw = jnp.transpose(wdw_g, (1, 0, 2)).reshape(CONF_KERNEL, d_conf)
    w3 = jnp.transpose(w3_g, (1, 0, 2)).reshape(SHORT_KERNEL, d_conf)

    h0 = jnp.concatenate([meta_full, x[0]], axis=0)
    target = jnp.concatenate([jnp.zeros((N_META, d), F32), loss_target[0]], axis=0)

    def norm_in(i, rows, vecs):
        return [_rms_fwd(rows[0], vecs[0])], []

    (n_lp,) = _rowwise("norm_in", norm_in, [(h0, d, 0)], [g_pre_mix], [(d, MXU_DTYPE)], [], tr, deps=[tok_rest])
    (wg_in,) = _pass_halves("gather_pass_in", _gather_wait("gather_wait_in", fly_in, [True], sems_in[0], [n_lp]))
    proj =_matmul("proj", n_lp, wg_in, kind="nn", tm=tm, tn=tn_in, tk=d, out_dtypes=[F32])
    ac, c3 = _conv_fwd(proj, wdw, conf_dw_b, w3, d_conf)

    def ln_parts(ac_t, ln_g, ln_b):
        mu = jnp.mean(ac_t, axis=-1, keepdims=True)
        xc = ac_t - mu
        rstd = lax.rsqrt(jnp.mean(xc * xc, axis=-1, keepdims=True) + LN_EPS)
        xh = xc * rstd
        return xh, rstd, xh * ln_g + ln_b

    def branch_act(i, rows, vecs):
        ac_t, c3_t, bg_t = rows
        _, _, al = ln_parts(ac_t, vecs[0], vecs[1])
        return [al * _sigmoid(al), bg_t * c3_t], []

    a_act, s_lp = _rowwise("branch_act", branch_act, [(ac, cw, 0), (c3, cw, 0), (proj, cw, 2)],
                           [conf_ln_g, conf_ln_b], [(d_conf, MXU_DTYPE), (d_conf, MXU_DTYPE)], [], tr)
    wg_pw, wg_sout = arrive(0, [a_act])
    y_a = _matmul("y_a", a_act, wg_pw, kind="nn", tm=tm, tn=tn_pw, tk=d_conf, out_dtypes=[F32])
    y_b = _matmul("y_b", s_lp, wg_sout, kind="nn", tm=tm, tn=tn_pw, tk=d_conf, out_dtypes=[F32])

    gate_rows = [(proj, cw, 5), (proj, cw, 6), (proj, cw, 7), (proj, cw, 8)]

    def gates_of(rows, b):
        ga = _sigmoid(jnp.concatenate([rows[0], rows[1]], axis=1) + b[:, :d])
        gb = _sigmoid(jnp.concatenate([rows[2], rows[3]], axis=1) + b[:, d:])
        return ga, gb

    def gate(i, rows, vecs):
        ga, gb = gates_of(rows[2:], vecs[0])
        return [ga * rows[0] + gb * rows[1]], []

    (m_lp,) = _rowwise("gate", gate, [(y_a, d, 0), (y_b, d, 0)] + gate_rows, [b_gates], [(d, MXU_DTYPE)], [], tr)
    wg_o = arrive(1, [m_lp])[0].reshape(d, d)
    mix = _matmul("mix", m_lp, wg_o, kind="nn", tm=tm, tn=tn_d, tk=d, out_dtypes=[F32])

    def post_mix(i, rows, vecs):
        h1_t = rows[0] + _rms_fwd(rows[1], vecs[0])
        return [h1_t, _rms_fwd(h1_t, vecs[1])], []

    h1, n2_lp = _rowwise("post_mix", post_mix, [(h0, d, 0), (mix, d, 0)], [g_post_mix, g_pre_mlp],
                         [(d, F32), (d, MXU_DTYPE)], [], tr)
    (wg_up,) = arrive(2, [n2_lp])
    up, f_lp = _matmul("up", n2_lp, wg_up, kind="nn", tm=tm, tn=tn_ff, tk=d, out_dtypes=[F32, MXU_DTYPE],
                       epilogue=lambda acc: (acc, jnp.square(jnp.maximum(acc, 0.0))))
    wg_down = arrive(3, [f_lp])[0].reshape(d_ff, d)
    dn = _matmul("down", f_lp, wg_down, kind="nn", tm=tm, tn=tn_d, tk=_largest_tile(d_ff, 2048, LANES),
                 out_dtypes=[F32])

    def head(i, rows, vecs):
        h1_t, dn_t, tgt = rows
        y = h1_t + _rms_fwd(dn_t, vecs[0])
        row = i * tr + lax.broadcasted_iota(jnp.int32, (tr, 1), 0)
        err = jnp.where(row >= N_META, y - tgt, 0.0)
        dy = err / d
        d_dn, dg = _rms_bwd(dn_t, vecs[0], dy)
        loss_rows = 0.5 * jnp.mean(err * err, axis=-1, keepdims=True)
        return [dy, d_dn], [dg, jnp.broadcast_to(loss_rows, (tr, LANES))]

    dy, d_dn, dg_post_mlp, loss_vec = _rowwise(
        "head", head, [(h1, d, 0), (dn, d, 0), (target, d, 0)], [g_post_mlp], [(d, F32), (d, MXU_DTYPE)], [d, LANES], tr)
    loss = lax.psum(loss_vec[0, 0], ("x", "y", "c"))

    core_arr = core.astype(jnp.int32).reshape(1)
    place = jnp.stack([chip, core]).astype(jnp.int32)
    in_flight = {}


    def swap_start(a, gw):
        halves = gw.reshape(N_CHIPS, 2, gw.shape[1] // 2, gw.shape[2])
        *in_flight[a], token = _swap_start(f"swap_start{a}", halves)
        return token

    def exchange_start(a, after):
        sems, halves, land = in_flight[a]
        halves, land = _swap_wait(f"swap_wait{a}", halves, land, sems, after)
        pair = _pair_sum(f"pair_sum{a}", halves, land, core_arr, _largest_tile(land.shape[1], 512, 16))
        *in_flight[a], token = _exchange_start(f"exchange_start{a}", pair)
        return token

    def reduce_finish(a, after):
        sems, pair, land = in_flight[a]
        pair, land = _exchange_wait(f"exchange_wait{a}", pair, land, sems, after)
        return _sum_pieces(f"sum_pieces{a}", pair, land, place, _largest_tile(land.shape[1], 256, 16))

    d_up = _matmul("d_up", d_dn, wg_down, kind="nt", tm=tm, tn=tn_ff, tk=d, out_dtypes=[MXU_DTYPE], extras=[up],
                   epilogue=lambda acc, up_t: (acc * (2.0 * jnp.maximum(up_t, 0.0)),))
    tk_t = t
    gw_down = _matmul("gw_down", f_lp, d_dn, kind="tn", tm=_largest_tile(d_ff, 512, LANES), tn=tn_d, tk=tk_t,
                      out_dtypes=[F32])
    tok = swap_start(5, gw_down.reshape(N_CHIPS, d_ff // N_CHIPS, d))
    d_n2 = _matmul("d_n2", d_up, wg_up, kind="nt", tm=tm, tn=tn_d, tk=tn_ff, out_dtypes=[F32], deps=[tok])
    tok = exchange_start(5, [d_n2])
    gw_up = _matmul("gw_up", n2_lp, d_up, kind="tn", tm=_largest_tile(d, 512, LANES), tn=tn_ff, tk=tk_t,
                    out_dtypes=[F32], out_pieces=N_CHIPS, deps=[tok])
    tok = swap_start(4, gw_up)

    def bwd_mid(i, rows, vecs):
        dy_t, dn2_t, h1_t, mix_t = rows
        d_h1a, dg_pre_mlp = _rms_bwd(h1_t, vecs[1], dn2_t)
        d_h1 = dy_t + d_h1a
        d_mix, dg_post_mix = _rms_bwd(mix_t, vecs[0], d_h1)
        return [d_h1, d_mix], [dg_pre_mlp, dg_post_mix]

    d_h1, d_mix, dg_pre_mlp, dg_post_mix = _rowwise(
        "bwd_mid", bwd_mid, [(dy, d, 0), (d_n2, d, 0), (h1, d, 0), (mix, d, 0)], [g_post_mix, g_pre_mlp],
        [(d, F32), (d, MXU_DTYPE)], [d, d], tr, deps=[tok])
    tok = exchange_start(4, [d_mix])
    d_m = _matmul("d_m", d_mix, wg_o, kind="nt", tm=tm, tn=tn_d, tk=d, out_dtypes=[F32], deps=[tok])
    gw_o = _matmul("gw_o", m_lp, d_mix, kind="tn", tm=_largest_tile(d, 512, LANES), tn=tn_d, tk=tk_t, out_dtypes=[F32])
    tok = swap_start(3, gw_o.reshape(N_CHIPS, d // N_CHIPS, d))

    def gate_bwd(i, rows, vecs):
        dm_t, ya_t, yb_t = rows[:3]
        ga, gb = gates_of(rows[3:], vecs[0])
        d_gpre = jnp.concatenate([dm_t * ya_t * ga * (1.0 - ga), dm_t * yb_t * gb * (1.0 - gb)], axis=1)
        return [dm_t * ga, dm_t * gb, d_gpre], [d_gpre]

    d_ya, d_yb, d_gpre, dg_b_gates = _rowwise(
        "gate_bwd", gate_bwd, [(d_m, d, 0), (y_a, d, 0), (y_b, d, 0)] + gate_rows, [b_gates],
        [(d, MXU_DTYPE), (d, MXU_DTYPE), (2 * d, MXU_DTYPE)], [2 * d], tr, deps=[tok])
    tok = exchange_start(3, [d_ya])
    d_aact = _matmul("d_aact", d_ya, wg_pw, kind="nt", tm=tm, tn=d_conf, tk=tn_pw, out_dtypes=[F32], deps=[tok])
    gw_pw = _matmul("gw_pw", a_act, d_ya, kind="tn", tm=_largest_tile(d_conf, 512, LANES), tn=tn_pw, tk=tk_t,
                    out_dtypes=[F32], out_pieces=N_CHIPS)
    tok = swap_start(1, gw_pw)
    d_s = _matmul("d_s", d_yb, wg_sout, kind="nt", tm=tm, tn=d_conf, tk=tn_pw, out_dtypes=[F32], deps=[tok])
    tok = exchange_start(1, [d_s])
    gw_sout = _matmul("gw_sout", s_lp, d_yb, kind="tn", tm=_largest_tile(d_conf, 512, LANES), tn=tn_pw, tk=tk_t,
                      out_dtypes=[F32], out_pieces=N_CHIPS, deps=[tok])
    tok = swap_start(2, gw_sout)

    def branch_bwd(i, rows, vecs):
        daact_t, ds_t, ac_t, c3_t, bg_t = rows
        xh, rstd, al = ln_parts(ac_t, vecs[0], vecs[1])
        sg = _sigmoid(al)
        d_al = daact_t * (sg * (1.0 + al * (1.0 - sg)))
        dxh = d_al * vecs[0]
        d_ac = rstd * (dxh - jnp.mean(dxh, axis=-1, keepdims=True) - xh * jnp.mean(dxh * xh, axis=-1, keepdims=True))
        return [d_ac, ds_t * bg_t, ds_t * c3_t], [d_al * xh, d_al, d_ac]

    d_ac, d_c3, d_bg, dg_ln_g, dg_ln_b, dg_dw_b = _rowwise(
        "branch_bwd", branch_bwd, [(d_aact, cw, 0), (d_s, cw, 0), (ac, cw, 0), (c3, cw, 0), (proj, cw, 2)],
        [conf_ln_g, conf_ln_b], [(d_conf, F32), (d_conf, F32), (d_conf, MXU_DTYPE)], [d_conf] * 3, tr, deps=[tok])
    tok = exchange_start(2, [d_ac])
    d_av, d_ag, d_cg, d_v, dg_wdw, dg_w3 = _conv_bwd(proj, d_ac, d_c3, wdw, w3, d_conf)
    d_proj = jnp.concatenate([d_av, d_ag, d_bg, d_cg, d_v, d_gpre], axis=1)
    d_n = _matmul("d_n", d_proj, wg_in, kind="nt", tm=tm, tn=tn_d, tk=tn_in, out_dtypes=[F32], deps=[tok])
    gw_in = _matmul("gw_in", n_lp, d_proj, kind="tn", tm=_largest_tile(d, 512, LANES), tn=tn_in, tk=tk_t,
                    out_dtypes=[F32], out_pieces=N_CHIPS)
    tok = swap_start(0, gw_in)

    def bwd_in(i, rows, vecs):
        d_h0a, dg = _rms_bwd(rows[2], vecs[0], rows[1])
        return [rows[0] + d_h0a], [dg]

    d_h0, dg_pre_mix = _rowwise("bwd_in", bwd_in, [(d_h1, d, 0), (d_n, d, 0), (h0, d, 0)], [g_pre_mix],
                                [(d, F32)], [d], tr, deps=[tok])
    tok = exchange_start(0, [d_h0])
    grad_x = d_h0[N_META:][None]

    big_m = [m_w_in, m_conf_w_pw, m_short_w_out, m_w_o, m_w_up, m_w_down]
    big_v = [v_w_in, v_conf_w_pw, v_short_w_out, v_w_o, v_w_up, v_w_down]
    big_res = {}

    def update_big(tag, members, after):
        reduced = []
        for a in members:
            reduced.append(reduce_finish(a, after))
            after = [reduced[-1]]
        joined = _join_halves(f"join_halves_{tag}", reduced)
        for a, j in zip(members, joined):
            big_res[a] = _adamw(f"adamw_big{a}", big[a], j.reshape(big[a].shape), big_m[a][0], big_v[a][0],
                                _largest_tile(big[a].shape[0], 128, 8))

    update_big("early", [5, 4, 3, 1, 2], [tok])

    small_w = d_conf
    rep = [dg_pre_mix, dg_b_gates, dg_dw_b, dg_ln_g, dg_ln_b, dg_post_mix, dg_pre_mlp, dg_post_mlp]
    rep_w = [g_pre_mix, b_gates, conf_dw_b, conf_ln_g, conf_ln_b, g_post_mix, g_pre_mlp, g_post_mlp]
    rep_m = [m_g_pre_mix, m_b_gates, m_conf_dw_b, m_conf_ln_g, m_conf_ln_b, m_g_post_mix, m_g_pre_mlp, m_g_post_mlp]
    rep_v = [v_g_pre_mix, v_b_gates, v_conf_dw_b, v_conf_ln_g, v_conf_ln_b, v_g_post_mix, v_g_pre_mlp, v_g_post_mlp]
    col = [dg_wdw, dg_w3, d_h0[:N_META]]
    assert all(a.size % small_w == 0 for a in rep + col)

    def pack(arrs):
        flat = jnp.concatenate([a.reshape(-1, small_w) for a in arrs], axis=0)
        return jnp.pad(flat, ((0, -flat.shape[0] % 8), (0, 0)))

    def unpack(buf, like):
        out, r0 = [], 0
        for a in like:
            nr = a.size // small_w
            out.append(buf[r0:r0 + nr].reshape(a.shape))
            r0 += nr
        return out

    packed = pack(rep + col)
    total = _sum_blocks("sum_small", _allgather_small(packed), 8)
    small_g = unpack(total, rep + col)
    g_rep = small_g[:len(rep)]
    g_wdw_full, g_w3_full, g_meta_full = small_g[len(rep):]
    sc = d_conf // N_CHIPS
    g_wdw = lax.dynamic_slice_in_dim(g_wdw_full, chip * sc, sc, axis=1)
    g_w3 = lax.dynamic_slice_in_dim(g_w3_full, chip * sc, sc, axis=1)
    g_meta = lax.dynamic_slice_in_dim(g_meta_full, chip * (d // N_CHIPS), d // N_CHIPS, axis=1)

    rep_pack = [pack(arrs) for arrs in (rep_w, g_rep, rep_m, rep_v)]
    rep_res = [unpack(buf, rep_w) for buf in _adamw("adamw_rep", *rep_pack, rep_pack[0].shape[0])]
    col_res = [_adamw(f"adamw_col{a}", w, g, m, v, w.shape[0]) for a, (w, g, m, v) in enumerate(
        [(meta, g_meta, m_meta, v_meta), (conf_dw_w[0], g_wdw, m_conf_dw_w[0], v_conf_dw_w[0]),
         (short_dw_w[0], g_w3, m_short_dw_w[0], v_short_dw_w[0])])]

    update_big("late", [0], [col_res[0][1], rep_res[1][0]] + [big_res[a][1] for a in (1, 2, 3, 4, 5)])

    def leaf(q):
        r = lambda a: rep_res[q][a]
        b = lambda a: big_res[a][q][None]
        return [col_res[0][q], r(0), b(0), r(1), col_res[1][q][None], r(2), r(3), r(4), b(1), col_res[2][q][None], b(2),
                b(3), r(5), r(6), b(4), b(5), r(7)]

    return (loss, grad_x, *leaf(0), *leaf(1), *leaf(2), *leaf(3))
```

```python
import functools

import jax
import jax.numpy as jnp
from jax import lax
from jax.experimental import pallas as pl
from jax.experimental.pallas import tpu as pltpu

F32 = jnp.float32
BF16 = jnp.bfloat16
MXU_DTYPE = BF16
WIRE_DTYPE = BF16

N_META = 16
CONF_KERNEL = 31
SHORT_KERNEL = 3
CONV_PAD = 32
RMS_EPS = 1e-6
LN_EPS = 1e-5
ADAM_LR = 0.001
ADAM_B1 = 0.9
ADAM_B2 = 0.999
ADAM_EPS = 1e-08
ADAM_WD = 0.01
ADAM_STEP = 10

N_CHIPS = 4
MESH = pl.DeviceIdType.MESH
LANES = 128


def _sigmoid(z):
    return 1.0 / (1.0 + jnp.exp(-z))


ANY_SPEC = pl.BlockSpec(memory_space=pl.ANY)


def _matmul(name, a, b, *, kind, tm, tn, tk, out_dtypes, out_pieces=1, epilogue=None, extras=(), deps=()):
    pieces = b.shape[0] if b.ndim == 3 else 1
    if kind == "nn":
        m, kdim = a.shape
        n = b.shape[-1] * pieces
        dims = (((1,), (0,)), ((), ()))
        a_spec = pl.BlockSpec((tm, tk), lambda i, j, k: (i, k))
        if b.ndim == 2:
            b_spec = pl.BlockSpec((tk, tn), lambda i, j, k: (k, j))
        else:
            npp = b.shape[-1] // tn
            b_spec = pl.BlockSpec((None, tk, tn), lambda i, j, k: (j // npp, k, j % npp))
    elif kind == "nt":
        m, kdim = a.shape
        n = b.shape[-2]
        dims = (((1,), (1,)), ((), ()))
        a_spec = pl.BlockSpec((tm, tk), lambda i, j, k: (i, k))
        if b.ndim == 2:
            b_spec = pl.BlockSpec((tn, tk), lambda i, j, k: (j, k))
        else:
            kpp = b.shape[-1] // tk
            b_spec = pl.BlockSpec((None, tn, tk), lambda i, j, k: (k // kpp, j, k % kpp))
    else:
        kdim, m = a.shape
        n = b.shape[-1]
        dims = (((0,), (0,)), ((), ()))
        a_spec = pl.BlockSpec((tk, tm), lambda i, j, k: (k, i))
        b_spec = pl.BlockSpec((tk, tn), lambda i, j, k: (k, j))
    assert m % tm == 0 and n % tn == 0 and kdim % tk == 0, (name, m, n, kdim, tm, tn, tk)
    nk = kdim // tk
    if out_pieces == 1:
        out_shape = (m, n)
        out_spec = pl.BlockSpec((tm, tn), lambda i, j, k: (i, j))
    else:
        onpp = n // out_pieces // tn
        out_shape = (out_pieces, m, n // out_pieces)
        out_spec = pl.BlockSpec((None, tm, tn), lambda i, j, k: (j // onpp, i, j % onpp))
    n_ex, n_out, n_in = len(extras), len(out_dtypes), len(extras) + len(deps)
    if epilogue is None:
        epilogue = lambda acc: (acc,)

    def body(a_ref, b_ref, *rest):
        ex_refs, o_refs = rest[:n_ex], rest[n_in:n_in + n_out]
        prod = lax.dot_general(a_ref[...], b_ref[...], dims, preferred_element_type=F32)

        def finish(acc):
            tiles = epilogue(acc, *[r[...] for r in ex_refs])
            for o_ref, t in zip(o_refs, tiles):
                o_ref[...] = t.astype(o_ref.dtype)

        if nk == 1:
            finish(prod)
        else:
            acc_ref = rest[n_in + n_out]
            k = pl.program_id(2)

            @pl.when(k == 0)
            def _():
                acc_ref[...] = prod

            @pl.when(jnp.logical_and(k > 0, k < nk - 1))
            def _():
                acc_ref[...] += prod

            @pl.when(k == nk - 1)
            def _():
                finish(acc_ref[...] + prod)

    ex_specs = [pl.BlockSpec((tm, tn), lambda i, j, k: (i, j)) for _ in extras]
    res = pl.pallas_call(
        body,
        name=name,
        grid=(m // tm, n // tn, nk),
        in_specs=[a_spec, b_spec, *ex_specs] + [ANY_SPEC] * len(deps),
        out_specs=[out_spec] * n_out,
        out_shape=[jax.ShapeDtypeStruct(out_shape, d) for d in out_dtypes],
        scratch_shapes=[pltpu.VMEM((tm, tn), F32)] if nk > 1 else [],
        compiler_params=pltpu.CompilerParams(dimension_semantics=("parallel", "parallel", "arbitrary")),
    )(a, b, *extras, *deps)
    return res[0] if n_out == 1 else res


ROW_CHUNK = 16
SUBLANES = 8


def _rowwise(name, fn, rows, vecs, outs, sums, tr, deps=()):
    t = rows[0][0].shape[0]
    assert t % tr == 0 and tr % ROW_CHUNK == 0
    n_r, n_v, n_o, n_s = len(rows), len(vecs), len(outs), len(sums)
    n_in = n_r + n_v + len(deps)
    n_steps = t // tr

    def body(*refs):
        r_in, v_in = refs[:n_r], refs[n_r:n_r + n_v]
        o_refs = refs[n_in:n_in + n_o]
        s_refs = refs[n_in + n_o:n_in + n_o + n_s]
        acc_refs = refs[n_in + n_o + n_s:]
        i = pl.program_id(0)

        @pl.when(i == 0)
        def _():
            for acc_ref in acc_refs:
                acc_ref[...] = jnp.zeros(acc_ref.shape, F32)

        def chunk(ci, carry):
            r0 = pl.multiple_of(ci * ROW_CHUNK, ROW_CHUNK)
            sl = pl.ds(r0, ROW_CHUNK)
            o_tiles, s_tiles = fn(i * tr + r0, [r[sl, :] for r in r_in], [v[...] for v in v_in])
            for o_ref, tile in zip(o_refs, o_tiles):
                o_ref[sl, :] = tile.astype(o_ref.dtype)
            for acc_ref, tile in zip(acc_refs, s_tiles):
                part = tile[0:SUBLANES]
                for s in range(1, ROW_CHUNK // SUBLANES):
                    part = part + tile[s * SUBLANES:(s + 1) * SUBLANES]
                acc_ref[...] += part
            return carry

        lax.fori_loop(0, tr // ROW_CHUNK, chunk, 0)

        @pl.when(i == n_steps - 1)
        def _():
            for s_ref, acc_ref in zip(s_refs, acc_refs):
                s_ref[...] = jnp.sum(acc_ref[...], axis=0, keepdims=True)

    def row_spec(width, blk):
        return pl.BlockSpec((tr, width), lambda i: (i, blk))

    res = pl.pallas_call(
        body,
        name=name,
        grid=(t // tr,),
        in_specs=[row_spec(w, blk) for _, w, blk in rows]
        + [pl.BlockSpec(v.shape, lambda i: (0, 0)) for v in vecs] + [ANY_SPEC] * len(deps),
        out_specs=[pl.BlockSpec((tr, c), lambda i: (i, 0)) for c, _ in outs]
        + [pl.BlockSpec((1, c), lambda i: (0, 0)) for c in sums],
        out_shape=[jax.ShapeDtypeStruct((t, c), d) for c, d in outs]
        + [jax.ShapeDtypeStruct((1, c), F32) for c in sums],
        scratch_shapes=[pltpu.VMEM((SUBLANES, c), F32) for c in sums],
        compiler_params=pltpu.CompilerParams(dimension_semantics=("arbitrary",)),
    )(*[r[0] for r in rows], *vecs, *deps)
    return res


def _rms_fwd(x, g):
    r = lax.rsqrt(jnp.mean(x * x, axis=-1, keepdims=True) + RMS_EPS)
    return x * r * g


def _rms_bwd(x, g, dy):
    r = lax.rsqrt(jnp.mean(x * x, axis=-1, keepdims=True) + RMS_EPS)
    xn = x * r
    dxn = dy * g
    dx = r * (dxn - xn * jnp.mean(dxn * xn, axis=-1, keepdims=True))
    return dx, dy * xn


CONV_ROWS = 64
CONV_LANES = 128


def _conv_fwd(proj, wdw, bdw, w3, d_conf):
    t = proj.shape[0]
    cl = CONV_LANES
    nb = d_conf // cl
    nchunk = t // CONV_ROWS
    assert t % CONV_ROWS == 0

    def body(av_ref, ag_ref, cg_ref, v_ref, wdw_ref, bdw_ref, w3_ref, ac_ref, c3_ref, apad, cpad):
        zeros = jnp.zeros((CONV_PAD, cl), F32)
        apad[0:CONV_PAD, :] = zeros
        cpad[0:CONV_PAD, :] = zeros
        apad[CONV_PAD:, :] = av_ref[...] * _sigmoid(ag_ref[...])
        cpad[CONV_PAD:, :] = cg_ref[...] * v_ref[...]

        def chunk(ci, carry):
            base = pl.multiple_of(ci * CONV_ROWS, 8)
            acc = jnp.zeros((CONV_ROWS, cl), F32) + bdw_ref[...]
            for k in range(CONF_KERNEL):
                off = CONV_PAD - (CONF_KERNEL - 1) + k
                acc = acc + apad[pl.ds(base + off, CONV_ROWS), :] * wdw_ref[k:k + 1, :]
            ac_ref[pl.ds(base, CONV_ROWS), :] = acc
            acc3 = jnp.zeros((CONV_ROWS, cl), F32)
            for k in range(SHORT_KERNEL):
                off = CONV_PAD - (SHORT_KERNEL - 1) + k
                acc3 = acc3 + cpad[pl.ds(base + off, CONV_ROWS), :] * w3_ref[k:k + 1, :]
            c3_ref[pl.ds(base, CONV_ROWS), :] = acc3
            return carry

        lax.fori_loop(0, nchunk, chunk, 0)

    def col(blk0):
        return pl.BlockSpec((t, cl), lambda j: (0, blk0 + j))

    return pl.pallas_call(
        body,
        name="conv_fwd",
        grid=(nb,),
        in_specs=[col(0), col(nb), col(3 * nb), col(4 * nb),
                  pl.BlockSpec((CONF_KERNEL, cl), lambda j: (0, j)),
                  pl.BlockSpec((1, cl), lambda j: (0, j)),
                  pl.BlockSpec((SHORT_KERNEL, cl), lambda j: (0, j))],
        out_specs=[pl.BlockSpec((t, cl), lambda j: (0, j))] * 2,
        out_shape=[jax.ShapeDtypeStruct((t, d_conf), F32)] * 2,
        scratch_shapes=[pltpu.VMEM((t + CONV_PAD, cl), F32)] * 2,
        compiler_params=pltpu.CompilerParams(dimension_semantics=("parallel",)),
    )(proj, proj, proj, proj, wdw, bdw, w3)


def _conv_bwd(proj, d_ac, d_c3, wdw, w3, d_conf):
    t = proj.shape[0]
    cl = CONV_LANES
    nb = d_conf // cl
    nchunk = t // CONV_ROWS
    nsub = CONV_ROWS // 8

    def fold(p):
        r = p[0:8]
        for s in range(1, nsub):
            r = r + p[8 * s:8 * s + 8]
        return r

    def body(av_ref, ag_ref, cg_ref, v_ref, dac_ref, dc3_ref, wdw_ref, w3_ref,
             dav_ref, dag_ref, dcg_ref, dv_ref, dwdw_ref, dw3_ref, apad, cpad, dapad, dcpad):
        zeros = jnp.zeros((CONV_PAD, cl), F32)
        apad[0:CONV_PAD, :] = zeros
        cpad[0:CONV_PAD, :] = zeros
        apad[CONV_PAD:, :] = av_ref[...] * _sigmoid(ag_ref[...])
        cpad[CONV_PAD:, :] = cg_ref[...] * v_ref[...]
        dapad[0:t, :] = dac_ref[...]
        dcpad[0:t, :] = dc3_ref[...]
        dapad[t:, :] = zeros
        dcpad[t:, :] = zeros

        def chunk(ci, accs):
            base = pl.multiple_of(ci * CONV_ROWS, 8)
            rows = pl.ds(base, CONV_ROWS)
            da = jnp.zeros((CONV_ROWS, cl), F32)
            for k in range(CONF_KERNEL):
                da = da + dapad[pl.ds(base + (CONF_KERNEL - 1 - k), CONV_ROWS), :] * wdw_ref[k:k + 1, :]
            dcv = jnp.zeros((CONV_ROWS, cl), F32)
            for k in range(SHORT_KERNEL):
                dcv = dcv + dcpad[pl.ds(base + (SHORT_KERNEL - 1 - k), CONV_ROWS), :] * w3_ref[k:k + 1, :]
            av, sg = av_ref[rows, :], _sigmoid(ag_ref[rows, :])
            dav_ref[rows, :] = (da * sg).astype(dav_ref.dtype)
            dag_ref[rows, :] = (da * av * sg * (1.0 - sg)).astype(dag_ref.dtype)
            dcg_ref[rows, :] = (dcv * v_ref[rows, :]).astype(dcg_ref.dtype)
            dv_ref[rows, :] = (dcv * cg_ref[rows, :]).astype(dv_ref.dtype)
            d_out, d_out3 = dac_ref[rows, :], dc3_ref[rows, :]
            new = []
            for k in range(CONF_KERNEL):
                off = CONV_PAD - (CONF_KERNEL - 1) + k
                new.append(accs[k] + fold(d_out * apad[pl.ds(base + off, CONV_ROWS), :]))
            for k in range(SHORT_KERNEL):
                off = CONV_PAD - (SHORT_KERNEL - 1) + k
                new.append(accs[CONF_KERNEL + k] + fold(d_out3 * cpad[pl.ds(base + off, CONV_ROWS), :]))
            return tuple(new)

        init = tuple(jnp.zeros((8, cl), F32) for _ in range(CONF_KERNEL + SHORT_KERNEL))
        accs = lax.fori_loop(0, nchunk, chunk, init)
        for k in range(CONF_KERNEL):
            dwdw_ref[k:k + 1, :] = jnp.sum(accs[k], axis=0, keepdims=True)
        for k in range(SHORT_KERNEL):
            dw3_ref[k:k + 1, :] = jnp.sum(accs[CONF_KERNEL + k], axis=0, keepdims=True)

    def col(blk0):
        return pl.BlockSpec((t, cl), lambda j: (0, blk0 + j))

    own = pl.BlockSpec((t, cl), lambda j: (0, j))
    return pl.pallas_call(
        body,
        name="conv_bwd",
        grid=(nb,),
        in_specs=[col(0), col(nb), col(3 * nb), col(4 * nb), own, own,
                  pl.BlockSpec((CONF_KERNEL, cl), lambda j: (0, j)),
                  pl.BlockSpec((SHORT_KERNEL, cl), lambda j: (0, j))],
        out_specs=[own] * 4 + [pl.BlockSpec((CONF_KERNEL, cl), lambda j: (0, j)),
                               pl.BlockSpec((SHORT_KERNEL, cl), lambda j: (0, j))],
        out_shape=[jax.ShapeDtypeStruct((t, d_conf), MXU_DTYPE)] * 4
        + [jax.ShapeDtypeStruct((CONF_KERNEL, d_conf), F32), jax.ShapeDtypeStruct((SHORT_KERNEL, d_conf), F32)],
        scratch_shapes=[pltpu.VMEM((t + CONV_PAD, cl), F32)] * 4,
        compiler_params=pltpu.CompilerParams(dimension_semantics=("parallel",)),
    )(proj, proj, proj, proj, d_ac, d_c3, wdw, w3)


def _elementwise(name, fn, ins, out_dtypes, tr):
    ins = [(a, ()) if not isinstance(a, tuple) else a for a in ins]
    r, c = ins[0][0].shape[-2:]
    assert r % tr == 0, (name, r, tr)
    n_in = len(ins)

    def body(*refs):
        tiles = fn(*[x[...] for x in refs[:n_in]])
        for o_ref, tile in zip(refs[n_in:], tiles):
            o_ref[...] = tile.astype(o_ref.dtype)

    def spec(lead):
        return pl.BlockSpec((None,) * len(lead) + (tr, c), lambda i: (*lead, i, 0))

    res = pl.pallas_call(
        body,
        name=name,
        grid=(r // tr,),
        in_specs=[spec(lead) for _, lead in ins],
        out_specs=[pl.BlockSpec((tr, c), lambda i: (i, 0))] * len(out_dtypes),
        out_shape=[jax.ShapeDtypeStruct((r, c), d) for d in out_dtypes],
        compiler_params=pltpu.CompilerParams(dimension_semantics=("parallel",)),
    )(*[a for a, _ in ins])
    return res


def _adamw_tiles(w, g, m, v):
    m = ADAM_B1 * m + (1.0 - ADAM_B1) * g
    v = ADAM_B2 * v + (1.0 - ADAM_B2) * jnp.square(g)
    m_hat = m / (1.0 - ADAM_B1 ** ADAM_STEP)
    v_hat = v / (1.0 - ADAM_B2 ** ADAM_STEP)
    delta = -ADAM_LR * (m_hat / (jnp.sqrt(v_hat) + ADAM_EPS) + ADAM_WD * w)
    return g, delta, m, v


def _adamw(name, w, g, m, v, tr):
    shape = w.shape
    flat = [a.reshape(shape[-2:]) if a.ndim > 2 else a for a in (w, g, m, v)]
    res = _elementwise(name, _adamw_tiles, flat, [F32] * 4, tr)
    return [a.reshape(shape) for a in res]


def _pair_sum(name, p, q, core, tr):
    n_p, _, hr, c = p.shape
    assert hr % tr == 0

    def body(core_ref, p_ref, q_ref, o_ref):
        o_ref[...] = (p_ref[...] + q_ref[...]).astype(o_ref.dtype)

    return pl.pallas_call(
        body,
        name=name,
        grid_spec=pltpu.PrefetchScalarGridSpec(
            num_scalar_prefetch=1,
            grid=(n_p, hr // tr),
            in_specs=[pl.BlockSpec((None, None, tr, c), lambda a, i, core_ref: (a, core_ref[0], i, 0)),
                      pl.BlockSpec((None, tr, c), lambda a, i, core_ref: (a, i, 0))],
            out_specs=pl.BlockSpec((None, tr, c), lambda a, i, core_ref: (a, i, 0)),
        ),
        out_shape=jax.ShapeDtypeStruct((n_p, hr, c), WIRE_DTYPE),
        compiler_params=pltpu.CompilerParams(dimension_semantics=("parallel", "parallel")),
    )(core, p, q)


def _into_slot(name, w, slots, slot, dtype, tr, deps=()):
    r, c = w.shape
    assert r % tr == 0

    def body(slot_ref, w_ref, *rest):
        o_ref = rest[len(deps)]
        o_ref[...] = w_ref[...].astype(o_ref.dtype)

    return pl.pallas_call(
        body,
        name=name,
        grid_spec=pltpu.PrefetchScalarGridSpec(
            num_scalar_prefetch=1,
            grid=(r // tr,),
            in_specs=[pl.BlockSpec((tr, c), lambda i, slot_ref: (i, 0))] + [ANY_SPEC] * len(deps),
            out_specs=pl.BlockSpec((None, tr, c), lambda i, slot_ref: (slot_ref[0], i, 0)),
        ),
        out_shape=jax.ShapeDtypeStruct((slots, r, c), dtype),
        compiler_params=pltpu.CompilerParams(dimension_semantics=("parallel",)),
    )(slot, w, *deps)


def _sum_pieces(name, own, rb, place, tr):
    n_p, hr, c = rb.shape
    assert hr % tr == 0

    def body(place_ref, own_ref, *refs):
        chip = place_ref[0]
        acc = None
        for k in range(n_p):
            tile = jnp.where(chip == k, own_ref[...], refs[k][...]).astype(F32)
            acc = tile if acc is None else acc + tile
        refs[n_p][...] = acc

    def landed(k):
        return pl.BlockSpec((None, tr, c), lambda i, place_ref: (jnp.where(place_ref[0] == k, (k + 1) % n_p, k), i, 0))

    return pl.pallas_call(
        body,
        name=name,
        grid_spec=pltpu.PrefetchScalarGridSpec(
            num_scalar_prefetch=1,
            grid=(hr // tr,),
            in_specs=[pl.BlockSpec((None, tr, c), lambda i, place_ref: (place_ref[0], i, 0))]
            + [landed(k) for k in range(n_p)],
            out_specs=pl.BlockSpec((None, tr, c), lambda i, place_ref: (place_ref[1], i, 0)),
        ),
        out_shape=jax.ShapeDtypeStruct((2, hr, c), F32),
        compiler_params=pltpu.CompilerParams(dimension_semantics=("parallel",)),
    )(place, own, *([rb] * n_p))


HBM_SPEC = pl.BlockSpec(memory_space=pl.ANY)


def _place():
    x, y, c = lax.axis_index("x"), lax.axis_index("y"), lax.axis_index("c")
    chips = [(1 - x, y), (x, 1 - y), (1 - x, 1 - y)]
    return x, y, c, chips


def _gather_shards(bufs, split):
    n = len(bufs)
    n_split = sum(split)
    fwd_slot = {a: s for s, a in enumerate([a for a in range(n) if split[a]])}

    def body(*refs):
        outs = refs[n:2 * n]
        send_sems, recv_sems, fsend_sems, frecv_sems = refs[2 * n:]
        x, y, c, chips = _place()
        me = 2 * x + y
        sibling = (x, y, 1 - c)

        def part(a, slot, h):
            if not split[a]:
                return outs[a].at[slot]
            hr = bufs[a].shape[1] // 2
            return outs[a].at[slot, pl.ds(h * hr, hr), :]

        sends = []
        for a in range(n):
            for j, chip in enumerate(chips):
                sends.append(pltpu.make_async_remote_copy(
                    src_ref=part(a, me, c), dst_ref=part(a, me, c),
                    send_sem=send_sems.at[3 * a + j], recv_sem=recv_sems.at[3 * a + j],
                    device_id=(*chip, c), device_id_type=MESH))
        for cp in sends:
            cp.start()
        passed = []
        for a in range(n):
            for j, chip in enumerate(chips):
                landed = part(a, 2 * chip[0] + chip[1], c)
                pltpu.make_async_remote_copy(
                    src_ref=landed, dst_ref=landed, send_sem=send_sems.at[3 * a + j], recv_sem=recv_sems.at[3 * a + j],
                    device_id=(*chip, c), device_id_type=MESH).wait_recv()
                if split[a]:
                    s = 3 * fwd_slot[a] + j
                    fwd = pltpu.make_async_remote_copy(
                        src_ref=landed, dst_ref=landed, send_sem=fsend_sems.at[s], recv_sem=frecv_sems.at[s],
                        device_id=sibling, device_id_type=MESH)
                    fwd.start()
                    passed.append(fwd)
        for a in range(n):
            if split[a]:
                for j, chip in enumerate(chips):
                    s = 3 * fwd_slot[a] + j
                    other = part(a, 2 * chip[0] + chip[1], 1 - c)
                    pltpu.make_async_remote_copy(
                        src_ref=other, dst_ref=other, send_sem=fsend_sems.at[s], recv_sem=frecv_sems.at[s],
                        device_id=sibling, device_id_type=MESH).wait_recv()
        for cp in sends + passed:
            cp.wait_send()

    return pl.pallas_call(
        body,
        name="gather_shards",
        in_specs=[HBM_SPEC] * n,
        out_specs=[HBM_SPEC] * n,
        out_shape=[jax.ShapeDtypeStruct(b.shape, b.dtype) for b in bufs],
        input_output_aliases={a: a for a in range(n)},
        scratch_shapes=[pltpu.SemaphoreType.DMA((3 * n,)), pltpu.SemaphoreType.DMA((3 * n,)),
                        pltpu.SemaphoreType.DMA((3 * n_split,)), pltpu.SemaphoreType.DMA((3 * n_split,))],
    )(*bufs)


def _swap_halves(name, parts):
    n = len(parts)
    n_p = parts[0].shape[0]

    def body(*refs):
        ins, outs = refs[:n], refs[n:2 * n]
        send_sems, recv_sems = refs[2 * n:]
        x, y, c, _ = _place()
        copies = []
        for a in range(n):
            for p in range(n_p):
                copies.append(pltpu.make_async_remote_copy(
                    src_ref=ins[a].at[p, 1 - c], dst_ref=outs[a].at[p],
                    send_sem=send_sems.at[n_p * a + p], recv_sem=recv_sems.at[n_p * a + p],
                    device_id=(x, y, 1 - c), device_id_type=MESH))
        for cp in copies:
            cp.start()
        for cp in copies:
            cp.wait()

    return pl.pallas_call(
        body,
        name=name,
        in_specs=[HBM_SPEC] * n,
        out_specs=[HBM_SPEC] * n,
        out_shape=[jax.ShapeDtypeStruct((n_p, *p.shape[2:]), p.dtype) for p in parts],
        scratch_shapes=[pltpu.SemaphoreType.DMA((n_p * n,)), pltpu.SemaphoreType.DMA((n_p * n,))],
    )(*parts)


def _exchange_pieces(sums):
    n = len(sums)

    def body(*refs):
        ins, outs = refs[:n], refs[n:2 * n]
        send_sems, recv_sems = refs[2 * n:]
        x, y, c, chips = _place()
        me = 2 * x + y
        sends = []
        for a in range(n):
            for j, chip in enumerate(chips):
                sends.append(pltpu.make_async_remote_copy(
                    src_ref=ins[a].at[2 * chip[0] + chip[1]], dst_ref=outs[a].at[me],
                    send_sem=send_sems.at[3 * a + j], recv_sem=recv_sems.at[3 * a + j],
                    device_id=(*chip, c), device_id_type=MESH))
        for cp in sends:
            cp.start()
        for a in range(n):
            for j, chip in enumerate(chips):
                slot = outs[a].at[2 * chip[0] + chip[1]]
                pltpu.make_async_remote_copy(
                    src_ref=slot, dst_ref=slot, send_sem=send_sems.at[3 * a + j], recv_sem=recv_sems.at[3 * a + j],
                    device_id=(*chip, c), device_id_type=MESH).wait_recv()
        for cp in sends:
            cp.wait_send()

    return pl.pallas_call(
        body,
        name="exchange_pieces",
        in_specs=[HBM_SPEC] * n,
        out_specs=[HBM_SPEC] * n,
        out_shape=[jax.ShapeDtypeStruct(s.shape, s.dtype) for s in sums],
        scratch_shapes=[pltpu.SemaphoreType.DMA((3 * n,)), pltpu.SemaphoreType.DMA((3 * n,))],
    )(*sums)


def _join_halves(name, bufs):
    n = len(bufs)

    def body(*refs):
        outs = refs[n:2 * n]
        send_sems, recv_sems = refs[2 * n:]
        x, y, c, _ = _place()
        copies = [pltpu.make_async_remote_copy(
            src_ref=outs[a].at[c], dst_ref=outs[a].at[c], send_sem=send_sems.at[a], recv_sem=recv_sems.at[a],
            device_id=(x, y, 1 - c), device_id_type=MESH) for a in range(n)]
        for cp in copies:
            cp.start()
        for a in range(n):
            other = outs[a].at[1 - c]
            pltpu.make_async_remote_copy(
                src_ref=other, dst_ref=other, send_sem=send_sems.at[a], recv_sem=recv_sems.at[a],
                device_id=(x, y, 1 - c), device_id_type=MESH).wait_recv()
        for cp in copies:
            cp.wait_send()

    return pl.pallas_call(
        body,
        name=name,
        in_specs=[HBM_SPEC] * n,
        out_specs=[HBM_SPEC] * n,
        out_shape=[jax.ShapeDtypeStruct(b.shape, b.dtype) for b in bufs],
        input_output_aliases={a: a for a in range(n)},
        scratch_shapes=[pltpu.SemaphoreType.DMA((n,)), pltpu.SemaphoreType.DMA((n,))],
    )(*bufs)


HBM_ONLY = pl.BlockSpec(memory_space=pltpu.HBM)
SEM_SPEC = pl.BlockSpec(memory_space=pltpu.SEMAPHORE)
DATAFLOW = pltpu.SideEffectType.DATAFLOW_SIDE_EFFECTING


def _in_hbm(a):
    return pltpu.with_memory_space_constraint(a, pltpu.HBM)


def _shard_part(ref, is_split, slot, h):
    if not is_split:
        return ref.at[slot]
    hr = ref.shape[1] // 2
    return ref.at[slot, pl.ds(h * hr, hr), :]


TOKEN = jax.ShapeDtypeStruct((8, LANES), F32)
VMEM_SPEC = pl.BlockSpec(memory_space=pltpu.VMEM)


def _gather_start(name, bufs, split, groups, deps=()):
    n, ng = len(bufs), len(groups)

    def body(*refs):
        ins, sems = refs[:n], refs[n + len(deps):n + len(deps) + 2 * ng]
        refs[-1][...] = jnp.zeros(TOKEN.shape, TOKEN.dtype)
        x, y, c, chips = _place()
        me = 2 * x + y
        for g, members in enumerate(groups):
            for s, a in enumerate(members):
                mine = _shard_part(ins[a], split[a], me, c)
                for j, chip in enumerate(chips):
                    pltpu.make_async_remote_copy(
                        src_ref=mine, dst_ref=mine, send_sem=sems[2 * g].at[3 * s + j], recv_sem=sems[2 * g + 1].at[3 * s + j],
                        device_id=(*chip, c), device_id_type=MESH).start()

    res = pl.pallas_call(
        body,
        name=name,
        in_specs=[HBM_ONLY] * n + [ANY_SPEC] * len(deps),
        out_specs=[SEM_SPEC] * (2 * ng) + [HBM_ONLY] * n + [VMEM_SPEC],
        out_shape=[pltpu.SemaphoreType.DMA((3 * len(members),)) for members in groups for _ in range(2)]
        + [pltpu.HBM(b.shape, b.dtype) for b in bufs] + [TOKEN],
        input_output_aliases={a: 2 * ng + a for a in range(n)},
        compiler_params=pltpu.CompilerParams(has_side_effects=DATAFLOW),
    )(*[_in_hbm(b) for b in bufs], *deps)
    return [(res[2 * g], res[2 * g + 1]) for g in range(ng)], list(res[2 * ng:2 * ng + n]), res[-1]


def _gather_wait(name, bufs, split, sems, after):
    n = len(bufs)

    def body(*refs):
        ins, send_sems, recv_sems = refs[:n], refs[n], refs[n + 1]
        x, y, c, chips = _place()
        me = 2 * x + y
        for s in range(n):
            for j, chip in enumerate(chips):
                copy = pltpu.make_async_remote_copy(
                    src_ref=_shard_part(ins[s], split[s], me, c),
                    dst_ref=_shard_part(ins[s], split[s], 2 * chip[0] + chip[1], c),
                    send_sem=send_sems.at[3 * s + j], recv_sem=recv_sems.at[3 * s + j],
                    device_id=(*chip, c), device_id_type=MESH)
                copy.wait_send()
                copy.wait_recv()

    res = pl.pallas_call(
        body,
        name=name,
        in_specs=[HBM_ONLY] * n + [SEM_SPEC, SEM_SPEC] + [ANY_SPEC] * len(after),
        out_specs=[HBM_ONLY] * n,
        out_shape=[pltpu.HBM(b.shape, b.dtype) for b in bufs],
        input_output_aliases={a: a for a in range(n)},
        compiler_params=pltpu.CompilerParams(has_side_effects=DATAFLOW),
    )(*bufs, *sems, *after)
    return list(res)


def _pass_halves(name, bufs):
    n = len(bufs)

    def body(*refs):
        outs = refs[n:2 * n]
        send_sems, recv_sems = refs[2 * n:]
        x, y, c, chips = _place()
        sibling = (x, y, 1 - c)

        def copy(a, j, h):
            blk = _shard_part(outs[a], True, 2 * chips[j][0] + chips[j][1], h)
            return pltpu.make_async_remote_copy(
                src_ref=blk, dst_ref=blk, send_sem=send_sems.at[3 * a + j], recv_sem=recv_sems.at[3 * a + j],
                device_id=sibling, device_id_type=MESH)

        sends = [copy(a, j, c) for a in range(n) for j in range(3)]
        for cp in sends:
            cp.start()
        for a in range(n):
            for j in range(3):
                copy(a, j, 1 - c).wait_recv()
        for cp in sends:
            cp.wait_send()

    res = pl.pallas_call(
        body,
        name=name,
        in_specs=[HBM_SPEC] * n,
        out_specs=[HBM_SPEC] * n,
        out_shape=[jax.ShapeDtypeStruct(b.shape, b.dtype) for b in bufs],
        input_output_aliases={a: a for a in range(n)},
        scratch_shapes=[pltpu.SemaphoreType.DMA((3 * n,)), pltpu.SemaphoreType.DMA((3 * n,))],
    )(*bufs)
    return list(res)


def _exchange_start(name, pair):
    def body(pair_ref, land_ref, send_sems, recv_sems, pair_thru, land_thru, token):
        x, y, c, chips = _place()
        me = 2 * x + y
        for j, chip in enumerate(chips):
            pltpu.make_async_remote_copy(
                src_ref=pair_ref.at[2 * chip[0] + chip[1]], dst_ref=land_ref.at[me],
                send_sem=send_sems.at[j], recv_sem=recv_sems.at[j], device_id=(*chip, c), device_id_type=MESH).start()
        token[...] = jnp.zeros(TOKEN.shape, TOKEN.dtype)

    send_sems, recv_sems, pair_thru, land_thru, token = pl.pallas_call(
        body,
        name=name,
        in_specs=[HBM_ONLY, HBM_ONLY],
        out_specs=[SEM_SPEC, SEM_SPEC, HBM_ONLY, HBM_ONLY, VMEM_SPEC],
        out_shape=[pltpu.SemaphoreType.DMA((3,)), pltpu.SemaphoreType.DMA((3,)),
                   pltpu.HBM(pair.shape, pair.dtype), pltpu.HBM(pair.shape, pair.dtype), TOKEN],
        input_output_aliases={0: 2, 1: 3},
        compiler_params=pltpu.CompilerParams(has_side_effects=DATAFLOW),
    )(_in_hbm(pair), _in_hbm(lax.empty(pair.shape, pair.dtype)))
    return (send_sems, recv_sems), pair_thru, land_thru, token


def _swap_start(name, halves):
    n_p, _, hr, cols = halves.shape
    land_shape = (n_p, hr, cols)

    def body(halves_ref, land_ref, send_sems, recv_sems, halves_thru, land_thru, token):
        x, y, c, _ = _place()
        for p in range(n_p):
            pltpu.make_async_remote_copy(
                src_ref=halves_ref.at[p, 1 - c], dst_ref=land_ref.at[p], send_sem=send_sems.at[p], recv_sem=recv_sems.at[p],
                device_id=(x, y, 1 - c), device_id_type=MESH).start()
        token[...] = jnp.zeros(TOKEN.shape, TOKEN.dtype)

    send_sems, recv_sems, halves_thru, land_thru, token = pl.pallas_call(
        body,
        name=name,
        in_specs=[HBM_ONLY, HBM_ONLY],
        out_specs=[SEM_SPEC, SEM_SPEC, HBM_ONLY, HBM_ONLY, VMEM_SPEC],
        out_shape=[pltpu.SemaphoreType.DMA((n_p,)), pltpu.SemaphoreType.DMA((n_p,)),
                   pltpu.HBM(halves.shape, halves.dtype), pltpu.HBM(land_shape, halves.dtype), TOKEN],
        input_output_aliases={0: 2, 1: 3},
        compiler_params=pltpu.CompilerParams(has_side_effects=DATAFLOW),
    )(_in_hbm(halves), _in_hbm(lax.empty(land_shape, halves.dtype)))
    return (send_sems, recv_sems), halves_thru, land_thru, token


def _swap_wait(name, halves, land, sems, after):
    n_p = halves.shape[0]

    def body(halves_ref, land_ref, send_sems, recv_sems, *rest):
        x, y, c, _ = _place()
        for p in range(n_p):
            copy = pltpu.make_async_remote_copy(
                src_ref=halves_ref.at[p, 1 - c], dst_ref=land_ref.at[p], send_sem=send_sems.at[p], recv_sem=recv_sems.at[p],
                device_id=(x, y, 1 - c), device_id_type=MESH)
            copy.wait_send()
            copy.wait_recv()

    return pl.pallas_call(
        body,
        name=name,
        in_specs=[HBM_ONLY, HBM_ONLY, SEM_SPEC, SEM_SPEC] + [ANY_SPEC] * len(after),
        out_specs=[HBM_ONLY, HBM_ONLY],
        out_shape=[pltpu.HBM(halves.shape, halves.dtype), pltpu.HBM(land.shape, land.dtype)],
        input_output_aliases={0: 0, 1: 1},
        compiler_params=pltpu.CompilerParams(has_side_effects=DATAFLOW),
    )(halves, land, *sems, *after)


def _exchange_wait(name, pair, land, sems, after):
    def body(pair_ref, land_ref, send_sems, recv_sems, *rest):
        x, y, c, chips = _place()
        for j, chip in enumerate(chips):
            k = 2 * chip[0] + chip[1]
            copy = pltpu.make_async_remote_copy(
                src_ref=pair_ref.at[k], dst_ref=land_ref.at[k], send_sem=send_sems.at[j], recv_sem=recv_sems.at[j],
                device_id=(*chip, c), device_id_type=MESH)
            copy.wait_send()
            copy.wait_recv()

    return pl.pallas_call(
        body,
        name=name,
        in_specs=[HBM_ONLY, HBM_ONLY, SEM_SPEC, SEM_SPEC] + [HBM_SPEC] * len(after),
        out_specs=[HBM_ONLY, HBM_ONLY],
        out_shape=[pltpu.HBM(pair.shape, pair.dtype), pltpu.HBM(land.shape, land.dtype)],
        input_output_aliases={0: 0, 1: 1},
        compiler_params=pltpu.CompilerParams(has_side_effects=DATAFLOW),
    )(pair, land, *sems, *after)


def _allgather_small(block):
    m_per, n = block.shape

    def body(x_ref, out_ref, send_sems, recv_sems, local_sem):
        x, y, c, chips = _place()
        me, sibling = (x, y, c), (x, y, 1 - c)

        def rows(px, py, pc):
            return out_ref.at[pl.ds((4 * px + 2 * py + pc) * m_per, m_per), :]

        def copy(k, blk, to, src=None):
            return pltpu.make_async_remote_copy(
                src_ref=rows(*blk) if src is None else src, dst_ref=rows(*blk),
                send_sem=send_sems.at[k], recv_sem=recv_sems.at[k], device_id=to, device_id_type=MESH)

        mine = pltpu.make_async_copy(x_ref, rows(*me), local_sem)
        mine.start()
        first = [copy(0, me, sibling, src=x_ref)]
        first += [copy(1 + j, me, (*chip, c), src=x_ref) for j, chip in enumerate(chips)]
        for cp in first:
            cp.start()
        passed = [copy(4 + j, (*chip, c), sibling) for j, chip in enumerate(chips)]
        for j, chip in enumerate(chips):
            copy(1 + j, (*chip, c), me).wait_recv()
            passed[j].start()
        copy(0, sibling, me).wait_recv()
        for j, chip in enumerate(chips):
            copy(4 + j, (*chip, 1 - c), me).wait_recv()
        for cp in first + passed:
            cp.wait_send()
        mine.wait()

    return pl.pallas_call(
        body,
        name="allgather_small",
        out_shape=jax.ShapeDtypeStruct((8 * m_per, n), block.dtype),
        in_specs=[pl.BlockSpec(memory_space=pltpu.VMEM)],
        out_specs=pl.BlockSpec(memory_space=pltpu.VMEM),
        scratch_shapes=[pltpu.SemaphoreType.DMA((7,)), pltpu.SemaphoreType.DMA((7,)), pltpu.SemaphoreType.DMA],
    )(block)


def _sum_blocks(name, gathered, n_blocks):
    r = gathered.shape[0] // n_blocks
    c = gathered.shape[1]

    def body(g_ref, o_ref):
        acc = g_ref[0:r, :]
        for b in range(1, n_blocks):
            acc = acc + g_ref[b * r:(b + 1) * r, :]
        o_ref[...] = acc

    return pl.pallas_call(body, name=name, out_shape=jax.ShapeDtypeStruct((r, c), F32))(gathered)


def _largest_tile(n, cap, mult):
    best = None
    for d in range(mult, min(n, cap) + 1, mult):
        if n % d == 0:
            best = d
    assert best is not None, (n, cap, mult)
    return best


def kernel(x, meta, g_pre_mix, w_in, b_gates, conf_dw_w, conf_dw_b, conf_ln_g, conf_ln_b, conf_w_pw, short_dw_w, short_w_out, w_o, g_post_mix, g_pre_mlp, w_up, w_down, g_post_mlp, loss_target, m_meta, m_g_pre_mix, m_w_in, m_b_gates, m_conf_dw_w, m_conf_dw_b, m_conf_ln_g, m_conf_ln_b, m_conf_w_pw, m_short_dw_w, m_short_w_out, m_w_o, m_g_post_mix, m_g_pre_mlp, m_w_up, m_w_down, m_g_post_mlp, v_meta, v_g_pre_mix, v_w_in, v_b_gates, v_conf_dw_w, v_conf_dw_b, v_conf_ln_g, v_conf_ln_b, v_conf_w_pw, v_short_dw_w, v_short_w_out, v_w_o, v_g_post_mix, v_g_pre_mlp, v_w_up, v_w_down, v_g_post_mlp):
    seq, d = x.shape[1], x.shape[2]
    t_real = seq + N_META
    t = -(-t_real // LANES) * LANES
    d_conf = conf_dw_b.shape[1]
    d_ff = w_up.shape[2] * N_CHIPS
    in_cols = w_in.shape[2] * N_CHIPS
    assert in_cols == 5 * d_conf + 2 * d and d == 2 * d_conf
    cw = d_conf
    core = lax.axis_index("c")
    chip = 2 * lax.axis_index("x") + lax.axis_index("y")

    tr = _largest_tile(t, 272, ROW_CHUNK)
    tm = _largest_tile(t, 1088, 16)
    tn_in = _largest_tile(in_cols // N_CHIPS, 1152, LANES)
    tn_d = _largest_tile(d, 1024, LANES)
    tn_ff = _largest_tile(d_ff // N_CHIPS, 1024, LANES)
    tn_pw = _largest_tile(d // N_CHIPS, 512, LANES)

    big = [w_in[0], conf_w_pw[0], short_w_out[0], w_o[0], w_up[0], w_down[0]]
    chip_arr = chip.astype(jnp.int32).reshape(1)

    def cast(a, deps):
        return _into_slot(f"cast_w{a}", big[a], N_CHIPS, chip_arr, MXU_DTYPE, _largest_tile(big[a].shape[0], 256, 16), deps)

    small = [_into_slot(f"place_w{a}", w, N_CHIPS, chip_arr, F32, w.shape[0])
             for a, w in enumerate([meta, conf_dw_w[0], short_dw_w[0]])]
    sems_small, fly_small, tok_small = _gather_start("gather_start_small", small, [False] * 3, [[0, 1, 2]])
    sems_in, fly_in, tok_in = _gather_start("gather_start_in", [cast(0, [tok_small])], [True], [[0]])
    rest_groups = [[0, 1], [2], [3], [4]]
    sems_rest, fly_rest, tok_rest = _gather_start(
        "gather_start_rest", [cast(a, [tok_in]) for a in range(1, 6)], [True] * 5, rest_groups)

    def arrive(g, after):
        members = rest_groups[g]
        got = _gather_wait(f"gather_wait_rest{g}", [fly_rest[a] for a in members], [True] * len(members),
                           sems_rest[g], after)
        return _pass_halves(f"gather_pass_rest{g}", got)

    meta_g, wdw_g, w3_g = _gather_wait("gather_wait_small", fly_small, [False] * 3, sems_small[0], [tok_in])
    meta_full = jnp.transpose(meta_g, (1, 0, 2)).reshape(N_META, d)
    wdw = jnp.transpose(wdw_g, (1, 0, 2)).reshape(CONF_KERNEL, d_conf)
    w3 = jnp.transpose(w3_g, (1, 0, 2)).reshape(SHORT_KERNEL, d_conf)

    tail = jnp.zeros((t - t_real, d), F32)
    h0 = jnp.concatenate([meta_full, x[0], tail], axis=0)
    target = jnp.concatenate([jnp.zeros((N_META, d), F32), loss_target[0], tail], axis=0)

    def norm_in(i, rows, vecs):
        return [_rms_fwd(rows[0], vecs[0])], []

    (n_lp,) = _rowwise("norm_in", norm_in, [(h0, d, 0)], [g_pre_mix], [(d, MXU_DTYPE)], [], tr, deps=[tok_rest])
    (wg_in,) = _pass_halves("gather_pass_in", _gather_wait("gather_wait_in", fly_in, [True], sems_in[0], [n_lp]))
    proj =_matmul("proj", n_lp, wg_in, kind="nn", tm=tm, tn=tn_in, tk=d, out_dtypes=[F32])
    ac, c3 = _conv_fwd(proj, wdw, conf_dw_b, w3, d_conf)

    def ln_parts(ac_t, ln_g, ln_b):
        mu = jnp.mean(ac_t, axis=-1, keepdims=True)
        xc = ac_t - mu
        rstd = lax.rsqrt(jnp.mean(xc * xc, axis=-1, keepdims=True) + LN_EPS)
        xh = xc * rstd
        return xh, rstd, xh * ln_g + ln_b

    def branch_act(i, rows, vecs):
        ac_t, c3_t, bg_t = rows
        _, _, al = ln_parts(ac_t, vecs[0], vecs[1])
        return [al * _sigmoid(al), bg_t * c3_t], []

    a_act, s_lp = _rowwise("branch_act", branch_act, [(ac, cw, 0), (c3, cw, 0), (proj, cw, 2)],
                           [conf_ln_g, conf_ln_b], [(d_conf, MXU_DTYPE), (d_conf, MXU_DTYPE)], [], tr)
    wg_pw, wg_sout = arrive(0, [a_act])
    y_a = _matmul("y_a", a_act, wg_pw, kind="nn", tm=tm, tn=tn_pw, tk=d_conf, out_dtypes=[F32])
    y_b = _matmul("y_b", s_lp, wg_sout, kind="nn", tm=tm, tn=tn_pw, tk=d_conf, out_dtypes=[F32])

    gate_rows = [(proj, cw, 5), (proj, cw, 6), (proj, cw, 7), (proj, cw, 8)]

    def gates_of(rows, b):
        ga = _sigmoid(jnp.concatenate([rows[0], rows[1]], axis=1) + b[:, :d])
        gb = _sigmoid(jnp.concatenate([rows[2], rows[3]], axis=1) + b[:, d:])
        return ga, gb

    def gate(i, rows, vecs):
        ga, gb = gates_of(rows[2:], vecs[0])
        return [ga * rows[0] + gb * rows[1]], []

    (m_lp,) = _rowwise("gate", gate, [(y_a, d, 0), (y_b, d, 0)] + gate_rows, [b_gates], [(d, MXU_DTYPE)], [], tr)
    wg_o = arrive(1, [m_lp])[0].reshape(d, d)
    mix = _matmul("mix", m_lp, wg_o, kind="nn", tm=tm, tn=tn_d, tk=d, out_dtypes=[F32])

    def post_mix(i, rows, vecs):
        h1_t = rows[0] + _rms_fwd(rows[1], vecs[0])
        return [h1_t, _rms_fwd(h1_t, vecs[1])], []

    h1, n2_lp = _rowwise("post_mix", post_mix, [(h0, d, 0), (mix, d, 0)], [g_post_mix, g_pre_mlp],
                         [(d, F32), (d, MXU_DTYPE)], [], tr)
    (wg_up,) = arrive(2, [n2_lp])
    up, f_lp = _matmul("up", n2_lp, wg_up, kind="nn", tm=tm, tn=tn_ff, tk=d, out_dtypes=[F32, MXU_DTYPE],
                       epilogue=lambda acc: (acc, jnp.square(jnp.maximum(acc, 0.0))))
    wg_down = arrive(3, [f_lp])[0].reshape(d_ff, d)
    dn = _matmul("down", f_lp, wg_down, kind="nn", tm=tm, tn=tn_d, tk=_largest_tile(d_ff, 2048, LANES),
                 out_dtypes=[F32])

    def head(i, rows, vecs):
        h1_t, dn_t, tgt = rows
        y = h1_t + _rms_fwd(dn_t, vecs[0])
        row = i + lax.broadcasted_iota(jnp.int32, (ROW_CHUNK, 1), 0)
        err = jnp.where(jnp.logical_and(row >= N_META, row < t_real), y - tgt, 0.0)
        dy = err / d
        d_dn, dg = _rms_bwd(dn_t, vecs[0], dy)
        loss_rows = 0.5 * jnp.mean(err * err, axis=-1, keepdims=True)
        return [dy, d_dn], [dg, jnp.broadcast_to(loss_rows, (ROW_CHUNK, LANES))]

    dy, d_dn, dg_post_mlp, loss_vec = _rowwise(
        "head", head, [(h1, d, 0), (dn, d, 0), (target, d, 0)], [g_post_mlp], [(d, F32), (d, MXU_DTYPE)], [d, LANES], tr)
    loss = lax.psum(loss_vec[0, 0], ("x", "y", "c"))

    core_arr = core.astype(jnp.int32).reshape(1)
    place = jnp.stack([chip, core]).astype(jnp.int32)
    in_flight = {}


    def swap_start(a, gw):
        halves = gw.reshape(N_CHIPS, 2, gw.shape[1] // 2, gw.shape[2])
        *in_flight[a], token = _swap_start(f"swap_start{a}", halves)
        return token

    def exchange_start(a, after):
        sems, halves, land = in_flight[a]
        halves, land = _swap_wait(f"swap_wait{a}", halves, land, sems, after)
        pair = _pair_sum(f"pair_sum{a}", halves, land, core_arr, _largest_tile(land.shape[1], 512, 16))
        *in_flight[a], token = _exchange_start(f"exchange_start{a}", pair)
        return token

    def reduce_finish(a, after):
        sems, pair, land = in_flight[a]
        pair, land = _exchange_wait(f"exchange_wait{a}", pair, land, sems, after)
        return _sum_pieces(f"sum_pieces{a}", pair, land, place, _largest_tile(land.shape[1], 256, 16))

    d_up = _matmul("d_up", d_dn, wg_down, kind="nt", tm=tm, tn=tn_ff, tk=d, out_dtypes=[MXU_DTYPE], extras=[up],
                   epilogue=lambda acc, up_t: (acc * (2.0 * jnp.maximum(up_t, 0.0)),))
    tk_t = t
    gw_down = _matmul("gw_down", f_lp, d_dn, kind="tn", tm=_largest_tile(d_ff, 512, LANES), tn=tn_d, tk=tk_t,
                      out_dtypes=[F32])
    tok = swap_start(5, gw_down.reshape(N_CHIPS, d_ff // N_CHIPS, d))
    d_n2 = _matmul("d_n2", d_up, wg_up, kind="nt", tm=tm, tn=tn_d, tk=d_ff // N_CHIPS, out_dtypes=[F32], deps=[tok])
    tok = exchange_start(5, [d_n2])
    gw_up = _matmul("gw_up", n2_lp, d_up, kind="tn", tm=_largest_tile(d, 512, LANES), tn=tn_ff, tk=tk_t,
                    out_dtypes=[F32], out_pieces=N_CHIPS, deps=[tok])
    tok = swap_start(4, gw_up)

    def bwd_mid(i, rows, vecs):
        dy_t, dn2_t, h1_t, mix_t = rows
        d_h1a, dg_pre_mlp = _rms_bwd(h1_t, vecs[1], dn2_t)
        d_h1 = dy_t + d_h1a
        d_mix, dg_post_mix = _rms_bwd(mix_t, vecs[0], d_h1)
        return [d_h1, d_mix], [dg_pre_mlp, dg_post_mix]

    d_h1, d_mix, dg_pre_mlp, dg_post_mix = _rowwise(
        "bwd_mid", bwd_mid, [(dy, d, 0), (d_n2, d, 0), (h1, d, 0), (mix, d, 0)], [g_post_mix, g_pre_mlp],
        [(d, F32), (d, MXU_DTYPE)], [d, d], tr, deps=[tok])
    tok = exchange_start(4, [d_mix])
    d_m = _matmul("d_m", d_mix, wg_o, kind="nt", tm=tm, tn=tn_d, tk=d, out_dtypes=[F32], deps=[tok])
    gw_o = _matmul("gw_o", m_lp, d_mix, kind="tn", tm=_largest_tile(d, 512, LANES), tn=tn_d, tk=tk_t, out_dtypes=[F32])
    tok = swap_start(3, gw_o.reshape(N_CHIPS, d // N_CHIPS, d))

    def gate_bwd(i, rows, vecs):
        dm_t, ya_t, yb_t = rows[:3]
        ga, gb = gates_of(rows[3:], vecs[0])
        d_gpre = jnp.concatenate([dm_t * ya_t * ga * (1.0 - ga), dm_t * yb_t * gb * (1.0 - gb)], axis=1)
        return [dm_t * ga, dm_t * gb, d_gpre], [d_gpre]

    d_ya, d_yb, d_gpre, dg_b_gates = _rowwise(
        "gate_bwd", gate_bwd, [(d_m, d, 0), (y_a, d, 0), (y_b, d, 0)] + gate_rows, [b_gates],
        [(d, MXU_DTYPE), (d, MXU_DTYPE), (2 * d, MXU_DTYPE)], [2 * d], tr, deps=[tok])
    tok = exchange_start(3, [d_ya])
    d_aact = _matmul("d_aact", d_ya, wg_pw, kind="nt", tm=tm, tn=d_conf, tk=tn_pw, out_dtypes=[F32], deps=[tok])
    gw_pw = _matmul("gw_pw", a_act, d_ya, kind="tn", tm=_largest_tile(d_conf, 512, LANES), tn=tn_pw, tk=tk_t,
                    out_dtypes=[F32], out_pieces=N_CHIPS)
    tok = swap_start(1, gw_pw)
    d_s = _matmul("d_s", d_yb, wg_sout, kind="nt", tm=tm, tn=d_conf, tk=tn_pw, out_dtypes=[F32], deps=[tok])
    tok = exchange_start(1, [d_s])
    gw_sout = _matmul("gw_sout", s_lp, d_yb, kind="tn", tm=_largest_tile(d_conf, 512, LANES), tn=tn_pw, tk=tk_t,
                      out_dtypes=[F32], out_pieces=N_CHIPS, deps=[tok])
    tok = swap_start(2, gw_sout)

    def branch_bwd(i, rows, vecs):
        daact_t, ds_t, ac_t, c3_t, bg_t = rows
        xh, rstd, al = ln_parts(ac_t, vecs[0], vecs[1])
        sg = _sigmoid(al)
        d_al = daact_t * (sg * (1.0 + al * (1.0 - sg)))
        dxh = d_al * vecs[0]
        d_ac = rstd * (dxh - jnp.mean(dxh, axis=-1, keepdims=True) - xh * jnp.mean(dxh * xh, axis=-1, keepdims=True))
        return [d_ac, ds_t * bg_t, ds_t * c3_t], [d_al * xh, d_al, d_ac]

    d_ac, d_c3, d_bg, dg_ln_g, dg_ln_b, dg_dw_b = _rowwise(
        "branch_bwd", branch_bwd, [(d_aact, cw, 0), (d_s, cw, 0), (ac, cw, 0), (c3, cw, 0), (proj, cw, 2)],
        [conf_ln_g, conf_ln_b], [(d_conf, F32), (d_conf, F32), (d_conf, MXU_DTYPE)], [d_conf] * 3, tr, deps=[tok])
    tok = exchange_start(2, [d_ac])
    d_av, d_ag, d_cg, d_v, dg_wdw, dg_w3 = _conv_bwd(proj, d_ac, d_c3, wdw, w3, d_conf)
    d_proj = jnp.concatenate([d_av, d_ag, d_bg, d_cg, d_v, d_gpre], axis=1)
    d_n = _matmul("d_n", d_proj, wg_in, kind="nt", tm=tm, tn=tn_d, tk=in_cols // N_CHIPS, out_dtypes=[F32], deps=[tok])
    gw_in = _matmul("gw_in", n_lp, d_proj, kind="tn", tm=_largest_tile(d, 512, LANES), tn=tn_in, tk=tk_t,
                    out_dtypes=[F32], out_pieces=N_CHIPS)
    tok = swap_start(0, gw_in)

    def bwd_in(i, rows, vecs):
        d_h0a, dg = _rms_bwd(rows[2], vecs[0], rows[1])
        return [rows[0] + d_h0a], [dg]

    d_h0, dg_pre_mix = _rowwise("bwd_in", bwd_in, [(d_h1, d, 0), (d_n, d, 0), (h0, d, 0)], [g_pre_mix],
                                [(d, F32)], [d], tr, deps=[tok])
    tok = exchange_start(0, [d_h0])
    grad_x = d_h0[N_META:t_real][None]

    big_m = [m_w_in, m_conf_w_pw, m_short_w_out, m_w_o, m_w_up, m_w_down]
    big_v = [v_w_in, v_conf_w_pw, v_short_w_out, v_w_o, v_w_up, v_w_down]
    big_res = {}

    def update_big(tag, members, after):
        reduced = []
        for a in members:
            reduced.append(reduce_finish(a, after))
            after = [reduced[-1]]
        joined = _join_halves(f"join_halves_{tag}", reduced)
        for a, j in zip(members, joined):
            big_res[a] = _adamw(f"adamw_big{a}", big[a], j.reshape(big[a].shape), big_m[a][0], big_v[a][0],
                                _largest_tile(big[a].shape[0], 128, 8))

    update_big("early", [5, 4, 3, 1, 2], [tok])

    small_w = d_conf
    rep = [dg_pre_mix, dg_b_gates, dg_dw_b, dg_ln_g, dg_ln_b, dg_post_mix, dg_pre_mlp, dg_post_mlp]
    rep_w = [g_pre_mix, b_gates, conf_dw_b, conf_ln_g, conf_ln_b, g_post_mix, g_pre_mlp, g_post_mlp]
    rep_m = [m_g_pre_mix, m_b_gates, m_conf_dw_b, m_conf_ln_g, m_conf_ln_b, m_g_post_mix, m_g_pre_mlp, m_g_post_mlp]
    rep_v = [v_g_pre_mix, v_b_gates, v_conf_dw_b, v_conf_ln_g, v_conf_ln_b, v_g_post_mix, v_g_pre_mlp, v_g_post_mlp]
    col = [dg_wdw, dg_w3, d_h0[:N_META]]
    assert all(a.size % small_w == 0 for a in rep + col)

    def pack(arrs):
        flat = jnp.concatenate([a.reshape(-1, small_w) for a in arrs], axis=0)
        return jnp.pad(flat, ((0, -flat.shape[0] % 8), (0, 0)))

    def unpack(buf, like):
        out, r0 = [], 0
        for a in like:
            nr = a.size // small_w
            out.append(buf[r0:r0 + nr].reshape(a.shape))
            r0 += nr
        return out

    packed = pack(rep + col)
    total = _sum_blocks("sum_small", _allgather_small(packed), 8)
    small_g = unpack(total, rep + col)
    g_rep = small_g[:len(rep)]
    g_wdw_full, g_w3_full, g_meta_full = small_g[len(rep):]
    sc = d_conf // N_CHIPS
    g_wdw = lax.dynamic_slice_in_dim(g_wdw_full, chip * sc, sc, axis=1)
    g_w3 = lax.dynamic_slice_in_dim(g_w3_full, chip * sc, sc, axis=1)
    g_meta = lax.dynamic_slice_in_dim(g_meta_full, chip * (d // N_CHIPS), d // N_CHIPS, axis=1)

    rep_pack = [pack(arrs) for arrs in (rep_w, g_rep, rep_m, rep_v)]
    rep_res = [unpack(buf, rep_w) for buf in _adamw("adamw_rep", *rep_pack, rep_pack[0].shape[0])]
    col_res = [_adamw(f"adamw_col{a}", w, g, m, v, w.shape[0]) for a, (w, g, m, v) in enumerate(
        [(meta, g_meta, m_meta, v_meta), (conf_dw_w[0], g_wdw, m_conf_dw_w[0], v_conf_dw_w[0]),
         (short_dw_w[0], g_w3, m_short_dw_w[0], v_short_dw_w[0])])]

    update_big("late", [0], [col_res[0][1], rep_res[1][0]] + [big_res[a][1] for a in (1, 2, 3, 4, 5)])

    def leaf(q):
        r = lambda a: rep_res[q][a]
        b = lambda a: big_res[a][q][None]
        return [col_res[0][q], r(0), b(0), r(1), col_res[1][q][None], r(2), r(3), r(4), b(1), col_res[2][q][None], b(2),
                b(3), r(5), r(6), b(4), b(5), r(7)]

    return (loss, grad_x, *leaf(0), *leaf(1), *leaf(2), *leaf(3))
```

```python
import functools

import jax
import jax.numpy as jnp
from jax import lax
from jax.experimental import pallas as pl
from jax.experimental.pallas import tpu as pltpu

F32 = jnp.float32
BF16 = jnp.bfloat16
MXU_DTYPE = BF16
WIRE_DTYPE = BF16

N_META = 16
CONF_KERNEL = 31
SHORT_KERNEL = 3
CONV_PAD = 32
RMS_EPS = 1e-6
LN_EPS = 1e-5
ADAM_LR = 0.001
ADAM_B1 = 0.9
ADAM_B2 = 0.999
ADAM_EPS = 1e-08
ADAM_WD = 0.01
ADAM_STEP = 10

N_CHIPS = 4
MESH = pl.DeviceIdType.MESH
LANES = 128


def _sigmoid(z):
    return 1.0 / (1.0 + jnp.exp(-z))


ANY_SPEC = pl.BlockSpec(memory_space=pl.ANY)


def _matmul(name, a, b, *, kind, tm, tn, tk, out_dtypes, out_pieces=1, epilogue=None, extras=(), deps=()):
    pieces = b.shape[0] if b.ndim == 3 else 1
    if kind == "nn":
        m, kdim = a.shape
        n = b.shape[-1] * pieces
        dims = (((1,), (0,)), ((), ()))
        a_spec = pl.BlockSpec((tm, tk), lambda i, j, k: (i, k))
        if b.ndim == 2:
            b_spec = pl.BlockSpec((tk, tn), lambda i, j, k: (k, j))
        else:
            npp = b.shape[-1] // tn
            b_spec = pl.BlockSpec((None, tk, tn), lambda i, j, k: (j // npp, k, j % npp))
    elif kind == "nt":
        m, kdim = a.shape
        n = b.shape[-2]
        dims = (((1,), (1,)), ((), ()))
        a_spec = pl.BlockSpec((tm, tk), lambda i, j, k: (i, k))
        if b.ndim == 2:
            b_spec = pl.BlockSpec((tn, tk), lambda i, j, k: (j, k))
        else:
            kpp = b.shape[-1] // tk
            b_spec = pl.BlockSpec((None, tn, tk), lambda i, j, k: (k // kpp, j, k % kpp))
    else:
        kdim, m = a.shape
        n = b.shape[-1]
        dims = (((0,), (0,)), ((), ()))
        a_spec = pl.BlockSpec((tk, tm), lambda i, j, k: (k, i))
        b_spec = pl.BlockSpec((tk, tn), lambda i, j, k: (k, j))
    assert m % tm == 0 and n % tn == 0 and kdim % tk == 0, (name, m, n, kdim, tm, tn, tk)
    nk = kdim // tk
    if out_pieces == 1:
        out_shape = (m, n)
        out_spec = pl.BlockSpec((tm, tn), lambda i, j, k: (i, j))
    else:
        onpp = n // out_pieces // tn
        out_shape = (out_pieces, m, n // out_pieces)
        out_spec = pl.BlockSpec((None, tm, tn), lambda i, j, k: (j // onpp, i, j % onpp))
    n_ex, n_out, n_in = len(extras), len(out_dtypes), len(extras) + len(deps)
    if epilogue is None:
        epilogue = lambda acc: (acc,)

    def body(a_ref, b_ref, *rest):
        ex_refs, o_refs = rest[:n_ex], rest[n_in:n_in + n_out]
        prod = lax.dot_general(a_ref[...], b_ref[...], dims, preferred_element_type=F32)

        def finish(acc):
            tiles = epilogue(acc, *[r[...] for r in ex_refs])
            for o_ref, t in zip(o_refs, tiles):
                o_ref[...] = t.astype(o_ref.dtype)

        if nk == 1:
            finish(prod)
        else:
            acc_ref = rest[n_in + n_out]
            k = pl.program_id(2)

            @pl.when(k == 0)
            def _():
                acc_ref[...] = prod

            @pl.when(jnp.logical_and(k > 0, k < nk - 1))
            def _():
                acc_ref[...] += prod

            @pl.when(k == nk - 1)
            def _():
                finish(acc_ref[...] + prod)

    ex_specs = [pl.BlockSpec((tm, tn), lambda i, j, k: (i, j)) for _ in extras]
    res = pl.pallas_call(
        body,
        name=name,
        grid=(m // tm, n // tn, nk),
        in_specs=[a_spec, b_spec, *ex_specs] + [ANY_SPEC] * len(deps),
        out_specs=[out_spec] * n_out,
        out_shape=[jax.ShapeDtypeStruct(out_shape, d) for d in out_dtypes],
        scratch_shapes=[pltpu.VMEM((tm, tn), F32)] if nk > 1 else [],
        compiler_params=pltpu.CompilerParams(dimension_semantics=("parallel", "parallel", "arbitrary")),
    )(a, b, *extras, *deps)
    return res[0] if n_out == 1 else res


ROW_CHUNK = 16
SUBLANES = 8


def _rowwise(name, fn, rows, vecs, outs, sums, tr, deps=()):
    t = rows[0][0].shape[0]
    assert t % tr == 0 and tr % ROW_CHUNK == 0
    n_r, n_v, n_o, n_s = len(rows), len(vecs), len(outs), len(sums)
    n_in = n_r + n_v + len(deps)
    n_steps = t // tr

    def body(*refs):
        r_in, v_in = refs[:n_r], refs[n_r:n_r + n_v]
        o_refs = refs[n_in:n_in + n_o]
        s_refs = refs[n_in + n_o:n_in + n_o + n_s]
        acc_refs = refs[n_in + n_o + n_s:]
        i = pl.program_id(0)

        @pl.when(i == 0)
        def _():
            for acc_ref in acc_refs:
                acc_ref[...] = jnp.zeros(acc_ref.shape, F32)

        def chunk(ci, carry):
            r0 = pl.multiple_of(ci * ROW_CHUNK, ROW_CHUNK)
            sl = pl.ds(r0, ROW_CHUNK)
            o_tiles, s_tiles = fn(i * tr + r0, [r[sl, :] for r in r_in], [v[...] for v in v_in])
            for o_ref, tile in zip(o_refs, o_tiles):
                o_ref[sl, :] = tile.astype(o_ref.dtype)
            for acc_ref, tile in zip(acc_refs, s_tiles):
                part = tile[0:SUBLANES]
                for s in range(1, ROW_CHUNK // SUBLANES):
                    part = part + tile[s * SUBLANES:(s + 1) * SUBLANES]
                acc_ref[...] += part
            return carry

        lax.fori_loop(0, tr // ROW_CHUNK, chunk, 0)

        @pl.when(i == n_steps - 1)
        def _():
            for s_ref, acc_ref in zip(s_refs, acc_refs):
                s_ref[...] = jnp.sum(acc_ref[...], axis=0, keepdims=True)

    def row_spec(width, blk):
        return pl.BlockSpec((tr, width), lambda i: (i, blk))

    res = pl.pallas_call(
        body,
        name=name,
        grid=(t // tr,),
        in_specs=[row_spec(w, blk) for _, w, blk in rows]
        + [pl.BlockSpec(v.shape, lambda i: (0, 0)) for v in vecs] + [ANY_SPEC] * len(deps),
        out_specs=[pl.BlockSpec((tr, c), lambda i: (i, 0)) for c, _ in outs]
        + [pl.BlockSpec((1, c), lambda i: (0, 0)) for c in sums],
        out_shape=[jax.ShapeDtypeStruct((t, c), d) for c, d in outs]
        + [jax.ShapeDtypeStruct((1, c), F32) for c in sums],
        scratch_shapes=[pltpu.VMEM((SUBLANES, c), F32) for c in sums],
        compiler_params=pltpu.CompilerParams(dimension_semantics=("arbitrary",)),
    )(*[r[0] for r in rows], *vecs, *deps)
    return res


def _rms_fwd(x, g):
    r = lax.rsqrt(jnp.mean(x * x, axis=-1, keepdims=True) + RMS_EPS)
    return x * r * g


def _rms_bwd(x, g, dy):
    r = lax.rsqrt(jnp.mean(x * x, axis=-1, keepdims=True) + RMS_EPS)
    xn = x * r
    dxn = dy * g
    dx = r * (dxn - xn * jnp.mean(dxn * xn, axis=-1, keepdims=True))
    return dx, dy * xn


CONV_ROWS = 64
CONV_LANES = 128


def _conv_fwd(proj, wdw, bdw, w3, d_conf):
    t = proj.shape[0]
    cl = CONV_LANES
    nb = d_conf // cl
    nchunk = t // CONV_ROWS
    assert t % CONV_ROWS == 0

    def body(av_ref, ag_ref, cg_ref, v_ref, wdw_ref, bdw_ref, w3_ref, ac_ref, c3_ref, apad, cpad):
        zeros = jnp.zeros((CONV_PAD, cl), F32)
        apad[0:CONV_PAD, :] = zeros
        cpad[0:CONV_PAD, :] = zeros
        apad[CONV_PAD:, :] = av_ref[...] * _sigmoid(ag_ref[...])
        cpad[CONV_PAD:, :] = cg_ref[...] * v_ref[...]

        def chunk(ci, carry):
            base = pl.multiple_of(ci * CONV_ROWS, 8)
            acc = jnp.zeros((CONV_ROWS, cl), F32) + bdw_ref[...]
            for k in range(CONF_KERNEL):
                off = CONV_PAD - (CONF_KERNEL - 1) + k
                acc = acc + apad[pl.ds(base + off, CONV_ROWS), :] * wdw_ref[k:k + 1, :]
            ac_ref[pl.ds(base, CONV_ROWS), :] = acc
            acc3 = jnp.zeros((CONV_ROWS, cl), F32)
            for k in range(SHORT_KERNEL):
                off = CONV_PAD - (SHORT_KERNEL - 1) + k
                acc3 = acc3 + cpad[pl.ds(base + off, CONV_ROWS), :] * w3_ref[k:k + 1, :]
            c3_ref[pl.ds(base, CONV_ROWS), :] = acc3
            return carry

        lax.fori_loop(0, nchunk, chunk, 0)

    def col(blk0):
        return pl.BlockSpec((t, cl), lambda j: (0, blk0 + j))

    return pl.pallas_call(
        body,
        name="conv_fwd",
        grid=(nb,),
        in_specs=[col(0), col(nb), col(3 * nb), col(4 * nb),
                  pl.BlockSpec((CONF_KERNEL, cl), lambda j: (0, j)),
                  pl.BlockSpec((1, cl), lambda j: (0, j)),
                  pl.BlockSpec((SHORT_KERNEL, cl), lambda j: (0, j))],
        out_specs=[pl.BlockSpec((t, cl), lambda j: (0, j))] * 2,
        out_shape=[jax.ShapeDtypeStruct((t, d_conf), F32)] * 2,
        scratch_shapes=[pltpu.VMEM((t + CONV_PAD, cl), F32)] * 2,
        compiler_params=pltpu.CompilerParams(dimension_semantics=("parallel",)),
    )(proj, proj, proj, proj, wdw, bdw, w3)


def _conv_bwd(proj, d_ac, d_c3, wdw, w3, d_conf):
    t = proj.shape[0]
    cl = CONV_LANES
    nb = d_conf // cl
    nchunk = t // CONV_ROWS
    nsub = CONV_ROWS // 8

    def fold(p):
        r = p[0:8]
        for s in range(1, nsub):
            r = r + p[8 * s:8 * s + 8]
        return r

    def body(av_ref, ag_ref, cg_ref, v_ref, dac_ref, dc3_ref, wdw_ref, w3_ref,
             dav_ref, dag_ref, dcg_ref, dv_ref, dwdw_ref, dw3_ref, apad, cpad, dapad, dcpad):
        zeros = jnp.zeros((CONV_PAD, cl), F32)
        apad[0:CONV_PAD, :] = zeros
        cpad[0:CONV_PAD, :] = zeros
        apad[CONV_PAD:, :] = av_ref[...] * _sigmoid(ag_ref[...])
        cpad[CONV_PAD:, :] = cg_ref[...] * v_ref[...]
        dapad[0:t, :] = dac_ref[...]
        dcpad[0:t, :] = dc3_ref[...]
        dapad[t:, :] = zeros
        dcpad[t:, :] = zeros

        def chunk(ci, accs):
            base = pl.multiple_of(ci * CONV_ROWS, 8)
            rows = pl.ds(base, CONV_ROWS)
            da = jnp.zeros((CONV_ROWS, cl), F32)
            for k in range(CONF_KERNEL):
                da = da + dapad[pl.ds(base + (CONF_KERNEL - 1 - k), CONV_ROWS), :] * wdw_ref[k:k + 1, :]
            dcv = jnp.zeros((CONV_ROWS, cl), F32)
            for k in range(SHORT_KERNEL):
                dcv = dcv + dcpad[pl.ds(base + (SHORT_KERNEL - 1 - k), CONV_ROWS), :] * w3_ref[k:k + 1, :]
            av, sg = av_ref[rows, :], _sigmoid(ag_ref[rows, :])
            dav_ref[rows, :] = (da * sg).astype(dav_ref.dtype)
            dag_ref[rows, :] = (da * av * sg * (1.0 - sg)).astype(dag_ref.dtype)
            dcg_ref[rows, :] = (dcv * v_ref[rows, :]).astype(dcg_ref.dtype)
            dv_ref[rows, :] = (dcv * cg_ref[rows, :]).astype(dv_ref.dtype)
            d_out, d_out3 = dac_ref[rows, :], dc3_ref[rows, :]
            new = []
            for k in range(CONF_KERNEL):
                off = CONV_PAD - (CONF_KERNEL - 1) + k
                new.append(accs[k] + fold(d_out * apad[pl.ds(base + off, CONV_ROWS), :]))
            for k in range(SHORT_KERNEL):
                off = CONV_PAD - (SHORT_KERNEL - 1) + k
                new.append(accs[CONF_KERNEL + k] + fold(d_out3 * cpad[pl.ds(base + off, CONV_ROWS), :]))
            return tuple(new)

        init = tuple(jnp.zeros((8, cl), F32) for _ in range(CONF_KERNEL + SHORT_KERNEL))
        accs = lax.fori_loop(0, nchunk, chunk, init)
        for k in range(CONF_KERNEL):
            dwdw_ref[k:k + 1, :] = jnp.sum(accs[k], axis=0, keepdims=True)
        for k in range(SHORT_KERNEL):
            dw3_ref[k:k + 1, :] = jnp.sum(accs[CONF_KERNEL + k], axis=0, keepdims=True)

    def col(blk0):
        return pl.BlockSpec((t, cl), lambda j: (0, blk0 + j))

    own = pl.BlockSpec((t, cl), lambda j: (0, j))
    return pl.pallas_call(
        body,
        name="conv_bwd",
        grid=(nb,),
        in_specs=[col(0), col(nb), col(3 * nb), col(4 * nb), own, own,
                  pl.BlockSpec((CONF_KERNEL, cl), lambda j: (0, j)),
                  pl.BlockSpec((SHORT_KERNEL, cl), lambda j: (0, j))],
        out_specs=[own] * 4 + [pl.BlockSpec((CONF_KERNEL, cl), lambda j: (0, j)),
                               pl.BlockSpec((SHORT_KERNEL, cl), lambda j: (0, j))],
        out_shape=[jax.ShapeDtypeStruct((t, d_conf), MXU_DTYPE)] * 4
        + [jax.ShapeDtypeStruct((CONF_KERNEL, d_conf), F32), jax.ShapeDtypeStruct((SHORT_KERNEL, d_conf), F32)],
        scratch_shapes=[pltpu.VMEM((t + CONV_PAD, cl), F32)] * 4,
        compiler_params=pltpu.CompilerParams(dimension_semantics=("parallel",)),
    )(proj, proj, proj, proj, d_ac, d_c3, wdw, w3)


def _elementwise(name, fn, ins, out_dtypes, tr):
    ins = [(a, ()) if not isinstance(a, tuple) else a for a in ins]
    r, c = ins[0][0].shape[-2:]
    assert r % tr == 0, (name, r, tr)
    n_in = len(ins)

    def body(*refs):
        tiles = fn(*[x[...] for x in refs[:n_in]])
        for o_ref, tile in zip(refs[n_in:], tiles):
            o_ref[...] = tile.astype(o_ref.dtype)

    def spec(lead):
        return pl.BlockSpec((None,) * len(lead) + (tr, c), lambda i: (*lead, i, 0))

    res = pl.pallas_call(
        body,
        name=name,
        grid=(r // tr,),
        in_specs=[spec(lead) for _, lead in ins],
        out_specs=[pl.BlockSpec((tr, c), lambda i: (i, 0))] * len(out_dtypes),
        out_shape=[jax.ShapeDtypeStruct((r, c), d) for d in out_dtypes],
        compiler_params=pltpu.CompilerParams(dimension_semantics=("parallel",)),
    )(*[a for a, _ in ins])
    return res


def _adamw_tiles(w, g, m, v):
    m = ADAM_B1 * m + (1.0 - ADAM_B1) * g
    v = ADAM_B2 * v + (1.0 - ADAM_B2) * jnp.square(g)
    m_hat = m / (1.0 - ADAM_B1 ** ADAM_STEP)
    v_hat = v / (1.0 - ADAM_B2 ** ADAM_STEP)
    delta = -ADAM_LR * (m_hat / (jnp.sqrt(v_hat) + ADAM_EPS) + ADAM_WD * w)
    return g, delta, m, v


def _adamw(name, w, g, m, v, tr):
    shape = w.shape
    flat = [a.reshape(shape[-2:]) if a.ndim > 2 else a for a in (w, g, m, v)]
    res = _elementwise(name, _adamw_tiles, flat, [F32] * 4, tr)
    return [a.reshape(shape) for a in res]


def _pair_sum(name, p, q, core, tr):
    n_p, _, hr, c = p.shape
    assert hr % tr == 0

    def body(core_ref, p_ref, q_ref, o_ref):
        o_ref[...] = (p_ref[...] + q_ref[...]).astype(o_ref.dtype)

    return pl.pallas_call(
        body,
        name=name,
        grid_spec=pltpu.PrefetchScalarGridSpec(
            num_scalar_prefetch=1,
            grid=(n_p, hr // tr),
            in_specs=[pl.BlockSpec((None, None, tr, c), lambda a, i, core_ref: (a, core_ref[0], i, 0)),
                      pl.BlockSpec((None, tr, c), lambda a, i, core_ref: (a, i, 0))],
            out_specs=pl.BlockSpec((None, tr, c), lambda a, i, core_ref: (a, i, 0)),
        ),
        out_shape=jax.ShapeDtypeStruct((n_p, hr, c), WIRE_DTYPE),
        compiler_params=pltpu.CompilerParams(dimension_semantics=("parallel", "parallel")),
    )(core, p, q)


def _into_slot(name, w, slots, slot, dtype, tr, deps=()):
    r, c = w.shape
    assert r % tr == 0

    def body(slot_ref, w_ref, *rest):
        o_ref = rest[len(deps)]
        o_ref[...] = w_ref[...].astype(o_ref.dtype)

    return pl.pallas_call(
        body,
        name=name,
        grid_spec=pltpu.PrefetchScalarGridSpec(
            num_scalar_prefetch=1,
            grid=(r // tr,),
            in_specs=[pl.BlockSpec((tr, c), lambda i, slot_ref: (i, 0))] + [ANY_SPEC] * len(deps),
            out_specs=pl.BlockSpec((None, tr, c), lambda i, slot_ref: (slot_ref[0], i, 0)),
        ),
        out_shape=jax.ShapeDtypeStruct((slots, r, c), dtype),
        compiler_params=pltpu.CompilerParams(dimension_semantics=("parallel",)),
    )(slot, w, *deps)


def _sum_pieces(name, own, rb, place, tr):
    n_p, hr, c = rb.shape
    assert hr % tr == 0

    def body(place_ref, own_ref, *refs):
        chip = place_ref[0]
        acc = None
        for k in range(n_p):
            tile = jnp.where(chip == k, own_ref[...], refs[k][...]).astype(F32)
            acc = tile if acc is None else acc + tile
        refs[n_p][...] = acc

    def landed(k):
        return pl.BlockSpec((None, tr, c), lambda i, place_ref: (jnp.where(place_ref[0] == k, (k + 1) % n_p, k), i, 0))

    return pl.pallas_call(
        body,
        name=name,
        grid_spec=pltpu.PrefetchScalarGridSpec(
            num_scalar_prefetch=1,
            grid=(hr // tr,),
            in_specs=[pl.BlockSpec((None, tr, c), lambda i, place_ref: (place_ref[0], i, 0))]
            + [landed(k) for k in range(n_p)],
            out_specs=pl.BlockSpec((None, tr, c), lambda i, place_ref: (place_ref[1], i, 0)),
        ),
        out_shape=jax.ShapeDtypeStruct((2, hr, c), F32),
        compiler_params=pltpu.CompilerParams(dimension_semantics=("parallel",)),
    )(place, own, *([rb] * n_p))


HBM_SPEC = pl.BlockSpec(memory_space=pl.ANY)


def _place():
    x, y, c = lax.axis_index("x"), lax.axis_index("y"), lax.axis_index("c")
    chips = [(1 - x, y), (x, 1 - y), (1 - x, 1 - y)]
    return x, y, c, chips


def _gather_shards(bufs, split):
    n = len(bufs)
    n_split = sum(split)
    fwd_slot = {a: s for s, a in enumerate([a for a in range(n) if split[a]])}

    def body(*refs):
        outs = refs[n:2 * n]
        send_sems, recv_sems, fsend_sems, frecv_sems = refs[2 * n:]
        x, y, c, chips = _place()
        me = 2 * x + y
        sibling = (x, y, 1 - c)

        def part(a, slot, h):
            if not split[a]:
                return outs[a].at[slot]
            hr = bufs[a].shape[1] // 2
            return outs[a].at[slot, pl.ds(h * hr, hr), :]

        sends = []
        for a in range(n):
            for j, chip in enumerate(chips):
                sends.append(pltpu.make_async_remote_copy(
                    src_ref=part(a, me, c), dst_ref=part(a, me, c),
                    send_sem=send_sems.at[3 * a + j], recv_sem=recv_sems.at[3 * a + j],
                    device_id=(*chip, c), device_id_type=MESH))
        for cp in sends:
            cp.start()
        passed = []
        for a in range(n):
            for j, chip in enumerate(chips):
                landed = part(a, 2 * chip[0] + chip[1], c)
                pltpu.make_async_remote_copy(
                    src_ref=landed, dst_ref=landed, send_sem=send_sems.at[3 * a + j], recv_sem=recv_sems.at[3 * a + j],
                    device_id=(*chip, c), device_id_type=MESH).wait_recv()
                if split[a]:
                    s = 3 * fwd_slot[a] + j
                    fwd = pltpu.make_async_remote_copy(
                        src_ref=landed, dst_ref=landed, send_sem=fsend_sems.at[s], recv_sem=frecv_sems.at[s],
                        device_id=sibling, device_id_type=MESH)
                    fwd.start()
                    passed.append(fwd)
        for a in range(n):
            if split[a]:
                for j, chip in enumerate(chips):
                    s = 3 * fwd_slot[a] + j
                    other = part(a, 2 * chip[0] + chip[1], 1 - c)
                    pltpu.make_async_remote_copy(
                        src_ref=other, dst_ref=other, send_sem=fsend_sems.at[s], recv_sem=frecv_sems.at[s],
                        device_id=sibling, device_id_type=MESH).wait_recv()
        for cp in sends + passed:
            cp.wait_send()

    return pl.pallas_call(
        body,
        name="gather_shards",
        in_specs=[HBM_SPEC] * n,
        out_specs=[HBM_SPEC] * n,
        out_shape=[jax.ShapeDtypeStruct(b.shape, b.dtype) for b in bufs],
        input_output_aliases={a: a for a in range(n)},
        scratch_shapes=[pltpu.SemaphoreType.DMA((3 * n,)), pltpu.SemaphoreType.DMA((3 * n,)),
                        pltpu.SemaphoreType.DMA((3 * n_split,)), pltpu.SemaphoreType.DMA((3 * n_split,))],
    )(*bufs)


def _swap_halves(name, parts):
    n = len(parts)
    n_p = parts[0].shape[0]

    def body(*refs):
        ins, outs = refs[:n], refs[n:2 * n]
        send_sems, recv_sems = refs[2 * n:]
        x, y, c, _ = _place()
        copies = []
        for a in range(n):
            for p in range(n_p):
                copies.append(pltpu.make_async_remote_copy(
                    src_ref=ins[a].at[p, 1 - c], dst_ref=outs[a].at[p],
                    send_sem=send_sems.at[n_p * a + p], recv_sem=recv_sems.at[n_p * a + p],
                    device_id=(x, y, 1 - c), device_id_type=MESH))
        for cp in copies:
            cp.start()
        for cp in copies:
            cp.wait()

    return pl.pallas_call(
        body,
        name=name,
        in_specs=[HBM_SPEC] * n,
        out_specs=[HBM_SPEC] * n,
        out_shape=[jax.ShapeDtypeStruct((n_p, *p.shape[2:]), p.dtype) for p in parts],
        scratch_shapes=[pltpu.SemaphoreType.DMA((n_p * n,)), pltpu.SemaphoreType.DMA((n_p * n,))],
    )(*parts)


def _exchange_pieces(sums):
    n = len(sums)

    def body(*refs):
        ins, outs = refs[:n], refs[n:2 * n]
        send_sems, recv_sems = refs[2 * n:]
        x, y, c, chips = _place()
        me = 2 * x + y
        sends = []
        for a in range(n):
            for j, chip in enumerate(chips):
                sends.append(pltpu.make_async_remote_copy(
                    src_ref=ins[a].at[2 * chip[0] + chip[1]], dst_ref=outs[a].at[me],
                    send_sem=send_sems.at[3 * a + j], recv_sem=recv_sems.at[3 * a + j],
                    device_id=(*chip, c), device_id_type=MESH))
        for cp in sends:
            cp.start()
        for a in range(n):
            for j, chip in enumerate(chips):
                slot = outs[a].at[2 * chip[0] + chip[1]]
                pltpu.make_async_remote_copy(
                    src_ref=slot, dst_ref=slot, send_sem=send_sems.at[3 * a + j], recv_sem=recv_sems.at[3 * a + j],
                    device_id=(*chip, c), device_id_type=MESH).wait_recv()
        for cp in sends:
            cp.wait_send()

    return pl.pallas_call(
        body,
        name="exchange_pieces",
        in_specs=[HBM_SPEC] * n,
        out_specs=[HBM_SPEC] * n,
        out_shape=[jax.ShapeDtypeStruct(s.shape, s.dtype) for s in sums],
        scratch_shapes=[pltpu.SemaphoreType.DMA((3 * n,)), pltpu.SemaphoreType.DMA((3 * n,))],
    )(*sums)


def _join_halves(name, bufs):
    n = len(bufs)

    def body(*refs):
        outs = refs[n:2 * n]
        send_sems, recv_sems = refs[2 * n:]
        x, y, c, _ = _place()
        copies = [pltpu.make_async_remote_copy(
            src_ref=outs[a].at[c], dst_ref=outs[a].at[c], send_sem=send_sems.at[a], recv_sem=recv_sems.at[a],
            device_id=(x, y, 1 - c), device_id_type=MESH) for a in range(n)]
        for cp in copies:
            cp.start()
        for a in range(n):
            other = outs[a].at[1 - c]
            pltpu.make_async_remote_copy(
                src_ref=other, dst_ref=other, send_sem=send_sems.at[a], recv_sem=recv_sems.at[a],
                device_id=(x, y, 1 - c), device_id_type=MESH).wait_recv()
        for cp in copies:
            cp.wait_send()

    return pl.pallas_call(
        body,
        name=name,
        in_specs=[HBM_SPEC] * n,
        out_specs=[HBM_SPEC] * n,
        out_shape=[jax.ShapeDtypeStruct(b.shape, b.dtype) for b in bufs],
        input_output_aliases={a: a for a in range(n)},
        scratch_shapes=[pltpu.SemaphoreType.DMA((n,)), pltpu.SemaphoreType.DMA((n,))],
    )(*bufs)


HBM_ONLY = pl.BlockSpec(memory_space=pltpu.HBM)
SEM_SPEC = pl.BlockSpec(memory_space=pltpu.SEMAPHORE)
DATAFLOW = pltpu.SideEffectType.DATAFLOW_SIDE_EFFECTING


def _in_hbm(a):
    return pltpu.with_memory_space_constraint(a, pltpu.HBM)


def _shard_part(ref, is_split, slot, h):
    if not is_split:
        return ref.at[slot]
    hr = ref.shape[1] // 2
    return ref.at[slot, pl.ds(h * hr, hr), :]


TOKEN = jax.ShapeDtypeStruct((8, LANES), F32)
VMEM_SPEC = pl.BlockSpec(memory_space=pltpu.VMEM)


def _gather_start(name, bufs, split, groups, deps=()):
    n, ng = len(bufs), len(groups)

    def body(*refs):
        ins, sems = refs[:n], refs[n + len(deps):n + len(deps) + 2 * ng]
        refs[-1][...] = jnp.zeros(TOKEN.shape, TOKEN.dtype)
        x, y, c, chips = _place()
        me = 2 * x + y
        for g, members in enumerate(groups):
            for s, a in enumerate(members):
                mine = _shard_part(ins[a], split[a], me, c)
                for j, chip in enumerate(chips):
                    pltpu.make_async_remote_copy(
                        src_ref=mine, dst_ref=mine, send_sem=sems[2 * g].at[3 * s + j], recv_sem=sems[2 * g + 1].at[3 * s + j],
                        device_id=(*chip, c), device_id_type=MESH).start()

    res = pl.pallas_call(
        body,
        name=name,
        in_specs=[HBM_ONLY] * n + [ANY_SPEC] * len(deps),
        out_specs=[SEM_SPEC] * (2 * ng) + [HBM_ONLY] * n + [VMEM_SPEC],
        out_shape=[pltpu.SemaphoreType.DMA((3 * len(members),)) for members in groups for _ in range(2)]
        + [pltpu.HBM(b.shape, b.dtype) for b in bufs] + [TOKEN],
        input_output_aliases={a: 2 * ng + a for a in range(n)},
        compiler_params=pltpu.CompilerParams(has_side_effects=DATAFLOW),
    )(*[_in_hbm(b) for b in bufs], *deps)
    return [(res[2 * g], res[2 * g + 1]) for g in range(ng)], list(res[2 * ng:2 * ng + n]), res[-1]


def _gather_wait(name, bufs, split, sems, after):
    n = len(bufs)

    def body(*refs):
        ins, send_sems, recv_sems = refs[:n], refs[n], refs[n + 1]
        x, y, c, chips = _place()
        me = 2 * x + y
        for s in range(n):
            for j, chip in enumerate(chips):
                copy = pltpu.make_async_remote_copy(
                    src_ref=_shard_part(ins[s], split[s], me, c),
                    dst_ref=_shard_part(ins[s], split[s], 2 * chip[0] + chip[1], c),
                    send_sem=send_sems.at[3 * s + j], recv_sem=recv_sems.at[3 * s + j],
                    device_id=(*chip, c), device_id_type=MESH)
                copy.wait_send()
                copy.wait_recv()

    res = pl.pallas_call(
        body,
        name=name,
        in_specs=[HBM_ONLY] * n + [SEM_SPEC, SEM_SPEC] + [ANY_SPEC] * len(after),
        out_specs=[HBM_ONLY] * n,
        out_shape=[pltpu.HBM(b.shape, b.dtype) for b in bufs],
        input_output_aliases={a: a for a in range(n)},
        compiler_params=pltpu.CompilerParams(has_side_effects=DATAFLOW),
    )(*bufs, *sems, *after)
    return list(res)


def _pass_halves(name, bufs):
    n = len(bufs)

    def body(*refs):
        outs = refs[n:2 * n]
        send_sems, recv_sems = refs[2 * n:]
        x, y, c, chips = _place()
        sibling = (x, y, 1 - c)

        def copy(a, j, h):
            blk = _shard_part(outs[a], True, 2 * chips[j][0] + chips[j][1], h)
            return pltpu.make_async_remote_copy(
                src_ref=blk, dst_ref=blk, send_sem=send_sems.at[3 * a + j], recv_sem=recv_sems.at[3 * a + j],
                device_id=sibling, device_id_type=MESH)

        sends = [copy(a, j, c) for a in range(n) for j in range(3)]
        for cp in sends:
            cp.start()
        for a in range(n):
            for j in range(3):
                copy(a, j, 1 - c).wait_recv()
        for cp in sends:
            cp.wait_send()

    res = pl.pallas_call(
        body,
        name=name,
        in_specs=[HBM_SPEC] * n,
        out_specs=[HBM_SPEC] * n,
        out_shape=[jax.ShapeDtypeStruct(b.shape, b.dtype) for b in bufs],
        input_output_aliases={a: a for a in range(n)},
        scratch_shapes=[pltpu.SemaphoreType.DMA((3 * n,)), pltpu.SemaphoreType.DMA((3 * n,))],
    )(*bufs)
    return list(res)


def _exchange_start(name, pair):
    def body(pair_ref, land_ref, send_sems, recv_sems, pair_thru, land_thru, token):
        x, y, c, chips = _place()
        me = 2 * x + y
        for j, chip in enumerate(chips):
            pltpu.make_async_remote_copy(
                src_ref=pair_ref.at[2 * chip[0] + chip[1]], dst_ref=land_ref.at[me],
                send_sem=send_sems.at[j], recv_sem=recv_sems.at[j], device_id=(*chip, c), device_id_type=MESH).start()
        token[...] = jnp.zeros(TOKEN.shape, TOKEN.dtype)

    send_sems, recv_sems, pair_thru, land_thru, token = pl.pallas_call(
        body,
        name=name,
        in_specs=[HBM_ONLY, HBM_ONLY],
        out_specs=[SEM_SPEC, SEM_SPEC, HBM_ONLY, HBM_ONLY, VMEM_SPEC],
        out_shape=[pltpu.SemaphoreType.DMA((3,)), pltpu.SemaphoreType.DMA((3,)),
                   pltpu.HBM(pair.shape, pair.dtype), pltpu.HBM(pair.shape, pair.dtype), TOKEN],
        input_output_aliases={0: 2, 1: 3},
        compiler_params=pltpu.CompilerParams(has_side_effects=DATAFLOW),
    )(_in_hbm(pair), _in_hbm(lax.empty(pair.shape, pair.dtype)))
    return (send_sems, recv_sems), pair_thru, land_thru, token


def _swap_start(name, halves):
    n_p, _, hr, cols = halves.shape
    land_shape = (n_p, hr, cols)

    def body(halves_ref, land_ref, send_sems, recv_sems, halves_thru, land_thru, token):
        x, y, c, _ = _place()
        for p in range(n_p):
            pltpu.make_async_remote_copy(
                src_ref=halves_ref.at[p, 1 - c], dst_ref=land_ref.at[p], send_sem=send_sems.at[p], recv_sem=recv_sems.at[p],
                device_id=(x, y, 1 - c), device_id_type=MESH).start()
        token[...] = jnp.zeros(TOKEN.shape, TOKEN.dtype)

    send_sems, recv_sems, halves_thru, land_thru, token = pl.pallas_call(
        body,
        name=name,
        in_specs=[HBM_ONLY, HBM_ONLY],
        out_specs=[SEM_SPEC, SEM_SPEC, HBM_ONLY, HBM_ONLY, VMEM_SPEC],
        out_shape=[pltpu.SemaphoreType.DMA((n_p,)), pltpu.SemaphoreType.DMA((n_p,)),
                   pltpu.HBM(halves.shape, halves.dtype), pltpu.HBM(land_shape, halves.dtype), TOKEN],
        input_output_aliases={0: 2, 1: 3},
        compiler_params=pltpu.CompilerParams(has_side_effects=DATAFLOW),
    )(_in_hbm(halves), _in_hbm(lax.empty(land_shape, halves.dtype)))
    return (send_sems, recv_sems), halves_thru, land_thru, token


def _swap_wait(name, halves, land, sems, after):
    n_p = halves.shape[0]

    def body(halves_ref, land_ref, send_sems, recv_sems, *rest):
        x, y, c, _ = _place()
        for p in range(n_p):
            copy = pltpu.make_async_remote_copy(
                src_ref=halves_ref.at[p, 1 - c], dst_ref=land_ref.at[p], send_sem=send_sems.at[p], recv_sem=recv_sems.at[p],
                device_id=(x, y, 1 - c), device_id_type=MESH)
            copy.wait_send()
            copy.wait_recv()

    return pl.pallas_call(
        body,
        name=name,
        in_specs=[HBM_ONLY, HBM_ONLY, SEM_SPEC, SEM_SPEC] + [ANY_SPEC] * len(after),
        out_specs=[HBM_ONLY, HBM_ONLY],
        out_shape=[pltpu.HBM(halves.shape, halves.dtype), pltpu.HBM(land.shape, land.dtype)],
        input_output_aliases={0: 0, 1: 1},
        compiler_params=pltpu.CompilerParams(has_side_effects=DATAFLOW),
    )(halves, land, *sems, *after)


def _exchange_wait(name, pair, land, sems, after):
    def body(pair_ref, land_ref, send_sems, recv_sems, *rest):
        x, y, c, chips = _place()
        for j, chip in enumerate(chips):
            k = 2 * chip[0] + chip[1]
            copy = pltpu.make_async_remote_copy(
                src_ref=pair_ref.at[k], dst_ref=land_ref.at[k], send_sem=send_sems.at[j], recv_sem=recv_sems.at[j],
                device_id=(*chip, c), device_id_type=MESH)
            copy.wait_send()
            copy.wait_recv()

    return pl.pallas_call(
        body,
        name=name,
        in_specs=[HBM_ONLY, HBM_ONLY, SEM_SPEC, SEM_SPEC] + [HBM_SPEC] * len(after),
        out_specs=[HBM_ONLY, HBM_ONLY],
        out_shape=[pltpu.HBM(pair.shape, pair.dtype), pltpu.HBM(land.shape, land.dtype)],
        input_output_aliases={0: 0, 1: 1},
        compiler_params=pltpu.CompilerParams(has_side_effects=DATAFLOW),
    )(pair, land, *sems, *after)


def _allgather_small(block):
    m_per, n = block.shape

    def body(x_ref, out_ref, send_sems, recv_sems, local_sem):
        x, y, c, chips = _place()
        me, sibling = (x, y, c), (x, y, 1 - c)

        def rows(px, py, pc):
            return out_ref.at[pl.ds((4 * px + 2 * py + pc) * m_per, m_per), :]

        def copy(k, blk, to, src=None):
            return pltpu.make_async_remote_copy(
                src_ref=rows(*blk) if src is None else src, dst_ref=rows(*blk),
                send_sem=send_sems.at[k], recv_sem=recv_sems.at[k], device_id=to, device_id_type=MESH)

        mine = pltpu.make_async_copy(x_ref, rows(*me), local_sem)
        mine.start()
        first = [copy(0, me, sibling, src=x_ref)]
        first += [copy(1 + j, me, (*chip, c), src=x_ref) for j, chip in enumerate(chips)]
        for cp in first:
            cp.start()
        passed = [copy(4 + j, (*chip, c), sibling) for j, chip in enumerate(chips)]
        for j, chip in enumerate(chips):
            copy(1 + j, (*chip, c), me).wait_recv()
            passed[j].start()
        copy(0, sibling, me).wait_recv()
        for j, chip in enumerate(chips):
            copy(4 + j, (*chip, 1 - c), me).wait_recv()
        for cp in first + passed:
            cp.wait_send()
        mine.wait()

    return pl.pallas_call(
        body,
        name="allgather_small",
        out_shape=jax.ShapeDtypeStruct((8 * m_per, n), block.dtype),
        in_specs=[pl.BlockSpec(memory_space=pltpu.VMEM)],
        out_specs=pl.BlockSpec(memory_space=pltpu.VMEM),
        scratch_shapes=[pltpu.SemaphoreType.DMA((7,)), pltpu.SemaphoreType.DMA((7,)), pltpu.SemaphoreType.DMA],
    )(block)


def _sum_blocks(name, gathered, n_blocks):
    r = gathered.shape[0] // n_blocks
    c = gathered.shape[1]

    def body(g_ref, o_ref):
        acc = g_ref[0:r, :]
        for b in range(1, n_blocks):
            acc = acc + g_ref[b * r:(b + 1) * r, :]
        o_ref[...] = acc

    return pl.pallas_call(body, name=name, out_shape=jax.ShapeDtypeStruct((r, c), F32))(gathered)


def _largest_tile(n, cap, mult):
    best = None
    for d in range(mult, min(n, cap) + 1, mult):
        if n % d == 0:
            best = d
    assert best is not None, (n, cap, mult)
    return best


def kernel(x, meta, g_pre_mix, w_in, b_gates, conf_dw_w, conf_dw_b, conf_ln_g, conf_ln_b, conf_w_pw, short_dw_w, short_w_out, w_o, g_post_mix, g_pre_mlp, w_up, w_down, g_post_mlp, loss_target, m_meta, m_g_pre_mix, m_w_in, m_b_gates, m_conf_dw_w, m_conf_dw_b, m_conf_ln_g, m_conf_ln_b, m_conf_w_pw, m_short_dw_w, m_short_w_out, m_w_o, m_g_post_mix, m_g_pre_mlp, m_w_up, m_w_down, m_g_post_mlp, v_meta, v_g_pre_mix, v_w_in, v_b_gates, v_conf_dw_w, v_conf_dw_b, v_conf_ln_g, v_conf_ln_b, v_conf_w_pw, v_short_dw_w, v_short_w_out, v_w_o, v_g_post_mix, v_g_pre_mlp, v_w_up, v_w_down, v_g_post_mlp):
    seq, d = x.shape[1], x.shape[2]
    t_real = seq + N_META
    t = -(-t_real // LANES) * LANES
    d_conf = conf_dw_b.shape[1]
    d_ff = w_up.shape[2] * N_CHIPS
    in_cols = w_in.shape[2] * N_CHIPS
    assert in_cols == 5 * d_conf + 2 * d and d == 2 * d_conf
    cw = d_conf
    core = lax.axis_index("c")
    chip = 2 * lax.axis_index("x") + lax.axis_index("y")

    tr = _largest_tile(t, 272, ROW_CHUNK)
    tm = t
    tn_in = _largest_tile(in_cols // N_CHIPS, 768, LANES)
    tn_d = _largest_tile(d, 1024, LANES)
    tn_h = _largest_tile(d, 512, LANES)
    tn_ff = _largest_tile(d_ff // N_CHIPS, 1024, LANES)
    tn_pw = _largest_tile(d // N_CHIPS, 512, LANES)

    big = [w_in[0], conf_w_pw[0], short_w_out[0], w_o[0], w_up[0], w_down[0]]
    chip_arr = chip.astype(jnp.int32).reshape(1)

    def cast(a, deps):
        return _into_slot(f"cast_w{a}", big[a], N_CHIPS, chip_arr, MXU_DTYPE, _largest_tile(big[a].shape[0], 256, 16), deps)

    small = [_into_slot(f"place_w{a}", w, N_CHIPS, chip_arr, F32, w.shape[0])
             for a, w in enumerate([meta, conf_dw_w[0], short_dw_w[0]])]
    sems_small, fly_small, tok_small = _gather_start("gather_start_small", small, [False] * 3, [[0, 1, 2]])
    sems_in, fly_in, tok_in = _gather_start("gather_start_in", [cast(0, [tok_small])], [True], [[0]])
    rest_groups = [[0, 1], [2], [3], [4]]
    sems_rest, fly_rest, tok_rest = _gather_start(
        "gather_start_rest", [cast(a, [tok_in]) for a in range(1, 6)], [True] * 5, rest_groups)

    def arrive(g, after):
        members = rest_groups[g]
        got = _gather_wait(f"gather_wait_rest{g}", [fly_rest[a] for a in members], [True] * len(members),
                           sems_rest[g], after)
        return _pass_halves(f"gather_pass_rest{g}", got)

    meta_g, wdw_g, w3_g = _gather_wait("gather_wait_small", fly_small, [False] * 3, sems_small[0], [tok_in])
    meta_full = jnp.transpose(meta_g, (1, 0, 2)).reshape(N_META, d)
    wdw = jnp.transpose(wdw_g, (1, 0, 2)).reshape(CONF_KERNEL, d_conf)
    w3 = jnp.transpose(w3_g, (1, 0, 2)).reshape(SHORT_KERNEL, d_conf)

    tail = jnp.zeros((t - t_real, d), F32)
    h0 = jnp.concatenate([meta_full, x[0], tail], axis=0)
    target = jnp.concatenate([jnp.zeros((N_META, d), F32), loss_target[0], tail], axis=0)

    def norm_in(i, rows, vecs):
        return [_rms_fwd(rows[0], vecs[0])], []

    (n_lp,) = _rowwise("norm_in", norm_in, [(h0, d, 0)], [g_pre_mix], [(d, MXU_DTYPE)], [], tr, deps=[tok_rest])
    (wg_in,) = _pass_halves("gather_pass_in", _gather_wait("gather_wait_in", fly_in, [True], sems_in[0], [n_lp]))
    proj =_matmul("proj", n_lp, wg_in, kind="nn", tm=tm, tn=tn_in, tk=d, out_dtypes=[F32])
    ac, c3 = _conv_fwd(proj, wdw, conf_dw_b, w3, d_conf)

    def ln_parts(ac_t, ln_g, ln_b):
        mu = jnp.mean(ac_t, axis=-1, keepdims=True)
        xc = ac_t - mu
        rstd = lax.rsqrt(jnp.mean(xc * xc, axis=-1, keepdims=True) + LN_EPS)
        xh = xc * rstd
        return xh, rstd, xh * ln_g + ln_b

    def branch_act(i, rows, vecs):
        ac_t, c3_t, bg_t = rows
        _, _, al = ln_parts(ac_t, vecs[0], vecs[1])
        return [al * _sigmoid(al), bg_t * c3_t], []

    a_act, s_lp = _rowwise("branch_act", branch_act, [(ac, cw, 0), (c3, cw, 0), (proj, cw, 2)],
                           [conf_ln_g, conf_ln_b], [(d_conf, MXU_DTYPE), (d_conf, MXU_DTYPE)], [], tr)
    wg_pw, wg_sout = arrive(0, [a_act])
    y_a = _matmul("y_a", a_act, wg_pw, kind="nn", tm=tm, tn=tn_pw, tk=d_conf, out_dtypes=[F32])
    y_b = _matmul("y_b", s_lp, wg_sout, kind="nn", tm=tm, tn=tn_pw, tk=d_conf, out_dtypes=[F32])

    gate_rows = [(proj, cw, 5), (proj, cw, 6), (proj, cw, 7), (proj, cw, 8)]

    def gates_of(rows, b):
        ga = _sigmoid(jnp.concatenate([rows[0], rows[1]], axis=1) + b[:, :d])
        gb = _sigmoid(jnp.concatenate([rows[2], rows[3]], axis=1) + b[:, d:])
        return ga, gb

    def gate(i, rows, vecs):
        ga, gb = gates_of(rows[2:], vecs[0])
        return [ga * rows[0] + gb * rows[1]], []

    (m_lp,) = _rowwise("gate", gate, [(y_a, d, 0), (y_b, d, 0)] + gate_rows, [b_gates], [(d, MXU_DTYPE)], [], tr)
    wg_o = arrive(1, [m_lp])[0].reshape(d, d)
    mix = _matmul("mix", m_lp, wg_o, kind="nn", tm=tm, tn=tn_d, tk=d, out_dtypes=[F32])

    def post_mix(i, rows, vecs):
        h1_t = rows[0] + _rms_fwd(rows[1], vecs[0])
        return [h1_t, _rms_fwd(h1_t, vecs[1])], []

    h1, n2_lp = _rowwise("post_mix", post_mix, [(h0, d, 0), (mix, d, 0)], [g_post_mix, g_pre_mlp],
                         [(d, F32), (d, MXU_DTYPE)], [], tr)
    (wg_up,) = arrive(2, [n2_lp])
    up, f_lp = _matmul("up", n2_lp, wg_up, kind="nn", tm=tm, tn=tn_h, tk=d, out_dtypes=[F32, MXU_DTYPE],
                       epilogue=lambda acc: (acc, jnp.square(jnp.maximum(acc, 0.0))))
    wg_down = arrive(3, [f_lp])[0].reshape(d_ff, d)
    dn = _matmul("down", f_lp, wg_down, kind="nn", tm=tm, tn=tn_h, tk=_largest_tile(d_ff, 2048, LANES),
                 out_dtypes=[F32])

    def head(i, rows, vecs):
        h1_t, dn_t, tgt = rows
        y = h1_t + _rms_fwd(dn_t, vecs[0])
        row = i + lax.broadcasted_iota(jnp.int32, (ROW_CHUNK, 1), 0)
        err = jnp.where(jnp.logical_and(row >= N_META, row < t_real), y - tgt, 0.0)
        dy = err / d
        d_dn, dg = _rms_bwd(dn_t, vecs[0], dy)
        loss_rows = 0.5 * jnp.mean(err * err, axis=-1, keepdims=True)
        return [dy, d_dn], [dg, jnp.broadcast_to(loss_rows, (ROW_CHUNK, LANES))]

    dy, d_dn, dg_post_mlp, loss_vec = _rowwise(
        "head", head, [(h1, d, 0), (dn, d, 0), (target, d, 0)], [g_post_mlp], [(d, F32), (d, MXU_DTYPE)], [d, LANES], tr)
    loss = lax.psum(loss_vec[0, 0], ("x", "y", "c"))

    core_arr = core.astype(jnp.int32).reshape(1)
    place = jnp.stack([chip, core]).astype(jnp.int32)
    in_flight = {}


    def swap_start(a, gw):
        halves = gw.reshape(N_CHIPS, 2, gw.shape[1] // 2, gw.shape[2])
        *in_flight[a], token = _swap_start(f"swap_start{a}", halves)
        return token

    def exchange_start(a, after):
        sems, halves, land = in_flight[a]
        halves, land = _swap_wait(f"swap_wait{a}", halves, land, sems, after)
        pair = _pair_sum(f"pair_sum{a}", halves, land, core_arr, _largest_tile(land.shape[1], 512, 16))
        *in_flight[a], token = _exchange_start(f"exchange_start{a}", pair)
        return token

    def reduce_finish(a, after):
        sems, pair, land = in_flight[a]
        pair, land = _exchange_wait(f"exchange_wait{a}", pair, land, sems, after)
        return _sum_pieces(f"sum_pieces{a}", pair, land, place, _largest_tile(land.shape[1], 256, 16))

    d_up = _matmul("d_up", d_dn, wg_down, kind="nt", tm=tm, tn=tn_h, tk=d, out_dtypes=[MXU_DTYPE], extras=[up],
                   epilogue=lambda acc, up_t: (acc * (2.0 * jnp.maximum(up_t, 0.0)),))
    tk_t = t
    gw_down = _matmul("gw_down", f_lp, d_dn, kind="tn", tm=_largest_tile(d_ff, 2048, LANES), tn=tn_d, tk=tk_t,
                      out_dtypes=[F32])
    tok = swap_start(5, gw_down.reshape(N_CHIPS, d_ff // N_CHIPS, d))
    d_n2 = _matmul("d_n2", d_up, wg_up, kind="nt", tm=tm, tn=tn_h, tk=d_ff // N_CHIPS, out_dtypes=[F32], deps=[tok])
    tok = exchange_start(5, [d_n2])
    gw_up = _matmul("gw_up", n2_lp, d_up, kind="tn", tm=_largest_tile(d, 2048, LANES), tn=tn_ff, tk=tk_t,
                    out_dtypes=[F32], out_pieces=N_CHIPS, deps=[tok])
    tok = swap_start(4, gw_up)

    def bwd_mid(i, rows, vecs):
        dy_t, dn2_t, h1_t, mix_t = rows
        d_h1a, dg_pre_mlp = _rms_bwd(h1_t, vecs[1], dn2_t)
        d_h1 = dy_t + d_h1a
        d_mix, dg_post_mix = _rms_bwd(mix_t, vecs[0], d_h1)
        return [d_h1, d_mix], [dg_pre_mlp, dg_post_mix]

    d_h1, d_mix, dg_pre_mlp, dg_post_mix = _rowwise(
        "bwd_mid", bwd_mid, [(dy, d, 0), (d_n2, d, 0), (h1, d, 0), (mix, d, 0)], [g_post_mix, g_pre_mlp],
        [(d, F32), (d, MXU_DTYPE)], [d, d], tr, deps=[tok])
    tok = exchange_start(4, [d_mix])
    d_m = _matmul("d_m", d_mix, wg_o, kind="nt", tm=tm, tn=tn_d, tk=d, out_dtypes=[F32], deps=[tok])
    gw_o = _matmul("gw_o", m_lp, d_mix, kind="tn", tm=_largest_tile(d, 2048, LANES), tn=tn_d, tk=tk_t, out_dtypes=[F32])
    tok = swap_start(3, gw_o.reshape(N_CHIPS, d // N_CHIPS, d))

    def gate_bwd(i, rows, vecs):
        dm_t, ya_t, yb_t = rows[:3]
        ga, gb = gates_of(rows[3:], vecs[0])
        d_gpre = jnp.concatenate([dm_t * ya_t * ga * (1.0 - ga), dm_t * yb_t * gb * (1.0 - gb)], axis=1)
        return [dm_t * ga, dm_t * gb, d_gpre], [d_gpre]

    d_ya, d_yb, d_gpre, dg_b_gates = _rowwise(
        "gate_bwd", gate_bwd, [(d_m, d, 0), (y_a, d, 0), (y_b, d, 0)] + gate_rows, [b_gates],
        [(d, MXU_DTYPE), (d, MXU_DTYPE), (2 * d, MXU_DTYPE)], [2 * d], tr, deps=[tok])
    tok = exchange_start(3, [d_ya])
    d_aact = _matmul("d_aact", d_ya, wg_pw, kind="nt", tm=tm, tn=d_conf, tk=tn_pw, out_dtypes=[F32], deps=[tok])
    gw_pw = _matmul("gw_pw", a_act, d_ya, kind="tn", tm=_largest_tile(d_conf, 2048, LANES), tn=tn_pw, tk=tk_t,
                    out_dtypes=[F32], out_pieces=N_CHIPS)
    tok = swap_start(1, gw_pw)
    d_s = _matmul("d_s", d_yb, wg_sout, kind="nt", tm=tm, tn=d_conf, tk=tn_pw, out_dtypes=[F32], deps=[tok])
    tok = exchange_start(1, [d_s])
    gw_sout = _matmul("gw_sout", s_lp, d_yb, kind="tn", tm=_largest_tile(d_conf, 2048, LANES), tn=tn_pw, tk=tk_t,
                      out_dtypes=[F32], out_pieces=N_CHIPS, deps=[tok])
    tok = swap_start(2, gw_sout)

    def branch_bwd(i, rows, vecs):
        daact_t, ds_t, ac_t, c3_t, bg_t = rows
        xh, rstd, al = ln_parts(ac_t, vecs[0], vecs[1])
        sg = _sigmoid(al)
        d_al = daact_t * (sg * (1.0 + al * (1.0 - sg)))
        dxh = d_al * vecs[0]
        d_ac = rstd * (dxh - jnp.mean(dxh, axis=-1, keepdims=True) - xh * jnp.mean(dxh * xh, axis=-1, keepdims=True))
        return [d_ac, ds_t * bg_t, ds_t * c3_t], [d_al * xh, d_al, d_ac]

    d_ac, d_c3, d_bg, dg_ln_g, dg_ln_b, dg_dw_b = _rowwise(
        "branch_bwd", branch_bwd, [(d_aact, cw, 0), (d_s, cw, 0), (ac, cw, 0), (c3, cw, 0), (proj, cw, 2)],
        [conf_ln_g, conf_ln_b], [(d_conf, F32), (d_conf, F32), (d_conf, MXU_DTYPE)], [d_conf] * 3, tr, deps=[tok])
    tok = exchange_start(2, [d_ac])
    d_av, d_ag, d_cg, d_v, dg_wdw, dg_w3 = _conv_bwd(proj, d_ac, d_c3, wdw, w3, d_conf)
    d_proj = jnp.concatenate([d_av, d_ag, d_bg, d_cg, d_v, d_gpre], axis=1)
    d_n = _matmul("d_n", d_proj, wg_in, kind="nt", tm=tm, tn=tn_h, tk=in_cols // N_CHIPS, out_dtypes=[F32], deps=[tok])
    gw_in = _matmul("gw_in", n_lp, d_proj, kind="tn", tm=_largest_tile(d, 2048, LANES), tn=tn_in, tk=tk_t,
                    out_dtypes=[F32], out_pieces=N_CHIPS)
    tok = swap_start(0, gw_in)

    def bwd_in(i, rows, vecs):
        d_h0a, dg = _rms_bwd(rows[2], vecs[0], rows[1])
        return [rows[0] + d_h0a], [dg]

    d_h0, dg_pre_mix = _rowwise("bwd_in", bwd_in, [(d_h1, d, 0), (d_n, d, 0), (h0, d, 0)], [g_pre_mix],
                                [(d, F32)], [d], tr, deps=[tok])
    tok = exchange_start(0, [d_h0])
    grad_x = d_h0[N_META:t_real][None]

    big_m = [m_w_in, m_conf_w_pw, m_short_w_out, m_w_o, m_w_up, m_w_down]
    big_v = [v_w_in, v_conf_w_pw, v_short_w_out, v_w_o, v_w_up, v_w_down]
    big_res = {}

    def update_big(tag, members, after):
        reduced = []
        for a in members:
            reduced.append(reduce_finish(a, after))
            after = [reduced[-1]]
        joined = _join_halves(f"join_halves_{tag}", reduced)
        for a, j in zip(members, joined):
            big_res[a] = _adamw(f"adamw_big{a}", big[a], j.reshape(big[a].shape), big_m[a][0], big_v[a][0],
                                _largest_tile(big[a].shape[0], 256, 8))

    update_big("early", [5, 4, 3, 1, 2], [tok])

    small_w = d_conf
    rep = [dg_pre_mix, dg_b_gates, dg_dw_b, dg_ln_g, dg_ln_b, dg_post_mix, dg_pre_mlp, dg_post_mlp]
    rep_w = [g_pre_mix, b_gates, conf_dw_b, conf_ln_g, conf_ln_b, g_post_mix, g_pre_mlp, g_post_mlp]
    rep_m = [m_g_pre_mix, m_b_gates, m_conf_dw_b, m_conf_ln_g, m_conf_ln_b, m_g_post_mix, m_g_pre_mlp, m_g_post_mlp]
    rep_v = [v_g_pre_mix, v_b_gates, v_conf_dw_b, v_conf_ln_g, v_conf_ln_b, v_g_post_mix, v_g_pre_mlp, v_g_post_mlp]
    col = [dg_wdw, dg_w3, d_h0[:N_META]]
    assert all(a.size % small_w == 0 for a in rep + col)

    def pack(arrs):
        flat = jnp.concatenate([a.reshape(-1, small_w) for a in arrs], axis=0)
        return jnp.pad(flat, ((0, -flat.shape[0] % 8), (0, 0)))

    def unpack(buf, like):
        out, r0 = [], 0
        for a in like:
            nr = a.size // small_w
            out.append(buf[r0:r0 + nr].reshape(a.shape))
            r0 += nr
        return out

    packed = pack(rep + col)
    total = _sum_blocks("sum_small", _allgather_small(packed), 8)
    small_g = unpack(total, rep + col)
    g_rep = small_g[:len(rep)]
    g_wdw_full, g_w3_full, g_meta_full = small_g[len(rep):]
    sc = d_conf // N_CHIPS
    g_wdw = lax.dynamic_slice_in_dim(g_wdw_full, chip * sc, sc, axis=1)
    g_w3 = lax.dynamic_slice_in_dim(g_w3_full, chip * sc, sc, axis=1)
    g_meta = lax.dynamic_slice_in_dim(g_meta_full, chip * (d // N_CHIPS), d // N_CHIPS, axis=1)

    rep_pack = [pack(arrs) for arrs in (rep_w, g_rep, rep_m, rep_v)]
    rep_res = [unpack(buf, rep_w) for buf in _adamw("adamw_rep", *rep_pack, rep_pack[0].shape[0])]
    col_res = [_adamw(f"adamw_col{a}", w, g, m, v, w.shape[0]) for a, (w, g, m, v) in enumerate(
        [(meta, g_meta, m_meta, v_meta), (conf_dw_w[0], g_wdw, m_conf_dw_w[0], v_conf_dw_w[0]),
         (short_dw_w[0], g_w3, m_short_dw_w[0], v_short_dw_w[0])])]

    update_big("late", [0], [col_res[0][1], rep_res[1][0]] + [big_res[a][1] for a in (1, 2, 3, 4, 5)])

    def leaf(q):
        r = lambda a: rep_res[q][a]
        b = lambda a: big_res[a][q][None]
        return [col_res[0][q], r(0), b(0), r(1), col_res[1][q][None], r(2), r(3), r(4), b(1), col_res[2][q][None], b(2),
                b(3), r(5), r(6), b(4), b(5), r(7)]

    return (loss, grad_x, *leaf(0), *leaf(1), *leaf(2), *leaf(3))
```

```python
import functools

import jax
import jax.numpy as jnp
from jax import lax
from jax.experimental import pallas as pl
from jax.experimental.pallas import tpu as pltpu

F32 = jnp.float32
BF16 = jnp.bfloat16
MXU_DTYPE = BF16
WIRE_DTYPE = BF16

N_META = 16
CONF_KERNEL = 31
SHORT_KERNEL = 3
CONV_PAD = 32
RMS_EPS = 1e-6
LN_EPS = 1e-5
ADAM_LR = 0.001
ADAM_B1 = 0.9
ADAM_B2 = 0.999
ADAM_EPS = 1e-08
ADAM_WD = 0.01
ADAM_STEP = 10

N_CHIPS = 4
MESH = pl.DeviceIdType.MESH
LANES = 128


def _sigmoid(z):
    return 1.0 / (1.0 + jnp.exp(-z))


ANY_SPEC = pl.BlockSpec(memory_space=pl.ANY)


def _matmul(name, a, b, *, kind, tm, tn, tk, out_dtypes, out_pieces=1, epilogue=None, extras=(), deps=()):
    pieces = b.shape[0] if b.ndim == 3 else 1
    if kind == "nn":
        m, kdim = a.shape
        n = b.shape[-1] * pieces
        dims = (((1,), (0,)), ((), ()))
        a_spec = pl.BlockSpec((tm, tk), lambda i, j, k: (i, k))
        if b.ndim == 2:
            b_spec = pl.BlockSpec((tk, tn), lambda i, j, k: (k, j))
        else:
            npp = b.shape[-1] // tn
            b_spec = pl.BlockSpec((None, tk, tn), lambda i, j, k: (j // npp, k, j % npp))
    elif kind == "nt":
        m, kdim = a.shape
        n = b.shape[-2]
        dims = (((1,), (1,)), ((), ()))
        a_spec = pl.BlockSpec((tm, tk), lambda i, j, k: (i, k))
        if b.ndim == 2:
            b_spec = pl.BlockSpec((tn, tk), lambda i, j, k: (j, k))
        else:
            kpp = b.shape[-1] // tk
            b_spec = pl.BlockSpec((None, tn, tk), lambda i, j, k: (k // kpp, j, k % kpp))
    else:
        kdim, m = a.shape
        n = b.shape[-1]
        dims = (((0,), (0,)), ((), ()))
        a_spec = pl.BlockSpec((tk, tm), lambda i, j, k: (k, i))
        b_spec = pl.BlockSpec((tk, tn), lambda i, j, k: (k, j))
    assert m % tm == 0 and n % tn == 0 and kdim % tk == 0, (name, m, n, kdim, tm, tn, tk)
    nk = kdim // tk
    if out_pieces == 1:
        out_shape = (m, n)
        out_spec = pl.BlockSpec((tm, tn), lambda i, j, k: (i, j))
    else:
        onpp = n // out_pieces // tn
        out_shape = (out_pieces, m, n // out_pieces)
        out_spec = pl.BlockSpec((None, tm, tn), lambda i, j, k: (j // onpp, i, j % onpp))
    n_ex, n_out, n_in = len(extras), len(out_dtypes), len(extras) + len(deps)
    if epilogue is None:
        epilogue = lambda acc: (acc,)

    def body(a_ref, b_ref, *rest):
        ex_refs, o_refs = rest[:n_ex], rest[n_in:n_in + n_out]
        prod = lax.dot_general(a_ref[...], b_ref[...], dims, preferred_element_type=F32)

        def finish(acc):
            tiles = epilogue(acc, *[r[...] for r in ex_refs])
            for o_ref, t in zip(o_refs, tiles):
                o_ref[...] = t.astype(o_ref.dtype)

        if nk == 1:
            finish(prod)
        else:
            acc_ref = rest[n_in + n_out]
            k = pl.program_id(2)

            @pl.when(k == 0)
            def _():
                acc_ref[...] = prod

            @pl.when(jnp.logical_and(k > 0, k < nk - 1))
            def _():
                acc_ref[...] += prod

            @pl.when(k == nk - 1)
            def _():
                finish(acc_ref[...] + prod)

    ex_specs = [pl.BlockSpec((tm, tn), lambda i, j, k: (i, j)) for _ in extras]
    res = pl.pallas_call(
        body,
        name=name,
        grid=(m // tm, n // tn, nk),
        in_specs=[a_spec, b_spec, *ex_specs] + [ANY_SPEC] * len(deps),
        out_specs=[out_spec] * n_out,
        out_shape=[jax.ShapeDtypeStruct(out_shape, d) for d in out_dtypes],
        scratch_shapes=[pltpu.VMEM((tm, tn), F32)] if nk > 1 else [],
        compiler_params=pltpu.CompilerParams(dimension_semantics=("parallel", "parallel", "arbitrary")),
    )(a, b, *extras, *deps)
    return res[0] if n_out == 1 else res


ROW_CHUNK = 16
SUBLANES = 8


def _rowwise(name, fn, rows, vecs, outs, sums, tr, deps=()):
    t = rows[0][0].shape[0]
    assert t % tr == 0 and tr % ROW_CHUNK == 0
    n_r, n_v, n_o, n_s = len(rows), len(vecs), len(outs), len(sums)
    n_in = n_r + n_v + len(deps)
    n_steps = t // tr

    def body(*refs):
        r_in, v_in = refs[:n_r], refs[n_r:n_r + n_v]
        o_refs = refs[n_in:n_in + n_o]
        s_refs = refs[n_in + n_o:n_in + n_o + n_s]
        acc_refs = refs[n_in + n_o + n_s:]
        i = pl.program_id(0)

        @pl.when(i == 0)
        def _():
            for acc_ref in acc_refs:
                acc_ref[...] = jnp.zeros(acc_ref.shape, F32)

        def chunk(ci):
            r0 = ci * ROW_CHUNK
            sl = pl.ds(r0, ROW_CHUNK)
            o_tiles, s_tiles = fn(i * tr + r0, [r[sl, :] for r in r_in], [v[...] for v in v_in])
            for o_ref, tile in zip(o_refs, o_tiles):
                o_ref[sl, :] = tile.astype(o_ref.dtype)
            for acc_ref, tile in zip(acc_refs, s_tiles):
                part = tile[0:SUBLANES]
                for s in range(1, ROW_CHUNK // SUBLANES):
                    part = part + tile[s * SUBLANES:(s + 1) * SUBLANES]
                acc_ref[...] += part

        for ci in range(tr // ROW_CHUNK):
            chunk(ci)

        @pl.when(i == n_steps - 1)
        def _():
            for s_ref, acc_ref in zip(s_refs, acc_refs):
                s_ref[...] = jnp.sum(acc_ref[...], axis=0, keepdims=True)

    def row_spec(width, blk):
        return pl.BlockSpec((tr, width), lambda i: (i, blk))

    res = pl.pallas_call(
        body,
        name=name,
        grid=(t // tr,),
        in_specs=[row_spec(w, blk) for _, w, blk in rows]
        + [pl.BlockSpec(v.shape, lambda i: (0, 0)) for v in vecs] + [ANY_SPEC] * len(deps),
        out_specs=[pl.BlockSpec((tr, c), lambda i: (i, 0)) for c, _ in outs]
        + [pl.BlockSpec((1, c), lambda i: (0, 0)) for c in sums],
        out_shape=[jax.ShapeDtypeStruct((t, c), d) for c, d in outs]
        + [jax.ShapeDtypeStruct((1, c), F32) for c in sums],
        scratch_shapes=[pltpu.VMEM((SUBLANES, c), F32) for c in sums],
        compiler_params=pltpu.CompilerParams(dimension_semantics=("arbitrary",)),
    )(*[r[0] for r in rows], *vecs, *deps)
    return res


def _rms_fwd(x, g):
    r = lax.rsqrt(jnp.mean(x * x, axis=-1, keepdims=True) + RMS_EPS)
    return x * r * g


def _rms_bwd(x, g, dy):
    r = lax.rsqrt(jnp.mean(x * x, axis=-1, keepdims=True) + RMS_EPS)
    xn = x * r
    dxn = dy * g
    dx = r * (dxn - xn * jnp.mean(dxn * xn, axis=-1, keepdims=True))
    return dx, dy * xn


CONV_ROWS = 64
CONV_LANES = 128


def _conv_fwd(proj, wdw, bdw, w3, d_conf):
    t = proj.shape[0]
    cl = CONV_LANES
    nb = d_conf // cl
    nchunk = t // CONV_ROWS
    assert t % CONV_ROWS == 0

    def body(av_ref, ag_ref, cg_ref, v_ref, wdw_ref, bdw_ref, w3_ref, ac_ref, c3_ref, apad, cpad):
        zeros = jnp.zeros((CONV_PAD, cl), F32)
        apad[0:CONV_PAD, :] = zeros
        cpad[0:CONV_PAD, :] = zeros
        apad[CONV_PAD:, :] = av_ref[...] * _sigmoid(ag_ref[...])
        cpad[CONV_PAD:, :] = cg_ref[...] * v_ref[...]

        def chunk(ci, carry):
            base = pl.multiple_of(ci * CONV_ROWS, 8)
            acc = jnp.zeros((CONV_ROWS, cl), F32) + bdw_ref[...]
            for k in range(CONF_KERNEL):
                off = CONV_PAD - (CONF_KERNEL - 1) + k
                acc = acc + apad[pl.ds(base + off, CONV_ROWS), :] * wdw_ref[k:k + 1, :]
            ac_ref[pl.ds(base, CONV_ROWS), :] = acc
            acc3 = jnp.zeros((CONV_ROWS, cl), F32)
            for k in range(SHORT_KERNEL):
                off = CONV_PAD - (SHORT_KERNEL - 1) + k
                acc3 = acc3 + cpad[pl.ds(base + off, CONV_ROWS), :] * w3_ref[k:k + 1, :]
            c3_ref[pl.ds(base, CONV_ROWS), :] = acc3
            return carry

        lax.fori_loop(0, nchunk, chunk, 0)

    def col(blk0):
        return pl.BlockSpec((t, cl), lambda j: (0, blk0 + j))

    return pl.pallas_call(
        body,
        name="conv_fwd",
        grid=(nb,),
        in_specs=[col(0), col(nb), col(3 * nb), col(4 * nb),
                  pl.BlockSpec((CONF_KERNEL, cl), lambda j: (0, j)),
                  pl.BlockSpec((1, cl), lambda j: (0, j)),
                  pl.BlockSpec((SHORT_KERNEL, cl), lambda j: (0, j))],
        out_specs=[pl.BlockSpec((t, cl), lambda j: (0, j))] * 2,
        out_shape=[jax.ShapeDtypeStruct((t, d_conf), F32)] * 2,
        scratch_shapes=[pltpu.VMEM((t + CONV_PAD, cl), F32)] * 2,
        compiler_params=pltpu.CompilerParams(dimension_semantics=("parallel",)),
    )(proj, proj, proj, proj, wdw, bdw, w3)


def _conv_bwd(proj, d_ac, d_c3, wdw, w3, d_conf):
    t = proj.shape[0]
    cl = CONV_LANES
    nb = d_conf // cl
    nchunk = t // CONV_ROWS
    nsub = CONV_ROWS // 8

    def fold(p):
        r = p[0:8]
        for s in range(1, nsub):
            r = r + p[8 * s:8 * s + 8]
        return r

    def body(av_ref, ag_ref, cg_ref, v_ref, dac_ref, dc3_ref, wdw_ref, w3_ref,
             dav_ref, dag_ref, dcg_ref, dv_ref, dwdw_ref, dw3_ref, apad, cpad, dapad, dcpad):
        zeros = jnp.zeros((CONV_PAD, cl), F32)
        apad[0:CONV_PAD, :] = zeros
        cpad[0:CONV_PAD, :] = zeros
        apad[CONV_PAD:, :] = av_ref[...] * _sigmoid(ag_ref[...])
        cpad[CONV_PAD:, :] = cg_ref[...] * v_ref[...]
        dapad[0:t, :] = dac_ref[...]
        dcpad[0:t, :] = dc3_ref[...]
        dapad[t:, :] = zeros
        dcpad[t:, :] = zeros

        def chunk(ci, accs):
            base = pl.multiple_of(ci * CONV_ROWS, 8)
            rows = pl.ds(base, CONV_ROWS)
            da = jnp.zeros((CONV_ROWS, cl), F32)
            for k in range(CONF_KERNEL):
                da = da + dapad[pl.ds(base + (CONF_KERNEL - 1 - k), CONV_ROWS), :] * wdw_ref[k:k + 1, :]
            dcv = jnp.zeros((CONV_ROWS, cl), F32)
            for k in range(SHORT_KERNEL):
                dcv = dcv + dcpad[pl.ds(base + (SHORT_KERNEL - 1 - k), CONV_ROWS), :] * w3_ref[k:k + 1, :]
            av, sg = av_ref[rows, :], _sigmoid(ag_ref[rows, :])
            dav_ref[rows, :] = (da * sg).astype(dav_ref.dtype)
            dag_ref[rows, :] = (da * av * sg * (1.0 - sg)).astype(dag_ref.dtype)
            dcg_ref[rows, :] = (dcv * v_ref[rows, :]).astype(dcg_ref.dtype)
            dv_ref[rows, :] = (dcv * cg_ref[rows, :]).astype(dv_ref.dtype)
            d_out, d_out3 = dac_ref[rows, :], dc3_ref[rows, :]
            new = []
            for k in range(CONF_KERNEL):
                off = CONV_PAD - (CONF_KERNEL - 1) + k
                new.append(accs[k] + fold(d_out * apad[pl.ds(base + off, CONV_ROWS), :]))
            for k in range(SHORT_KERNEL):
                off = CONV_PAD - (SHORT_KERNEL - 1) + k
                new.append(accs[CONF_KERNEL + k] + fold(d_out3 * cpad[pl.ds(base + off, CONV_ROWS), :]))
            return tuple(new)

        init = tuple(jnp.zeros((8, cl), F32) for _ in range(CONF_KERNEL + SHORT_KERNEL))
        accs = lax.fori_loop(0, nchunk, chunk, init)
        for k in range(CONF_KERNEL):
            dwdw_ref[k:k + 1, :] = jnp.sum(accs[k], axis=0, keepdims=True)
        for k in range(SHORT_KERNEL):
            dw3_ref[k:k + 1, :] = jnp.sum(accs[CONF_KERNEL + k], axis=0, keepdims=True)

    def col(blk0):
        return pl.BlockSpec((t, cl), lambda j: (0, blk0 + j))

    own = pl.BlockSpec((t, cl), lambda j: (0, j))
    return pl.pallas_call(
        body,
        name="conv_bwd",
        grid=(nb,),
        in_specs=[col(0), col(nb), col(3 * nb), col(4 * nb), own, own,
                  pl.BlockSpec((CONF_KERNEL, cl), lambda j: (0, j)),
                  pl.BlockSpec((SHORT_KERNEL, cl), lambda j: (0, j))],
        out_specs=[own] * 4 + [pl.BlockSpec((CONF_KERNEL, cl), lambda j: (0, j)),
                               pl.BlockSpec((SHORT_KERNEL, cl), lambda j: (0, j))],
        out_shape=[jax.ShapeDtypeStruct((t, d_conf), MXU_DTYPE)] * 4
        + [jax.ShapeDtypeStruct((CONF_KERNEL, d_conf), F32), jax.ShapeDtypeStruct((SHORT_KERNEL, d_conf), F32)],
        scratch_shapes=[pltpu.VMEM((t + CONV_PAD, cl), F32)] * 4,
        compiler_params=pltpu.CompilerParams(dimension_semantics=("parallel",)),
    )(proj, proj, proj, proj, d_ac, d_c3, wdw, w3)


def _elementwise(name, fn, ins, out_dtypes, tr):
    ins = [(a, ()) if not isinstance(a, tuple) else a for a in ins]
    r, c = ins[0][0].shape[-2:]
    assert r % tr == 0, (name, r, tr)
    n_in = len(ins)

    def body(*refs):
        tiles = fn(*[x[...] for x in refs[:n_in]])
        for o_ref, tile in zip(refs[n_in:], tiles):
            o_ref[...] = tile.astype(o_ref.dtype)

    def spec(lead):
        return pl.BlockSpec((None,) * len(lead) + (tr, c), lambda i: (*lead, i, 0))

    res = pl.pallas_call(
        body,
        name=name,
        grid=(r // tr,),
        in_specs=[spec(lead) for _, lead in ins],
        out_specs=[pl.BlockSpec((tr, c), lambda i: (i, 0))] * len(out_dtypes),
        out_shape=[jax.ShapeDtypeStruct((r, c), d) for d in out_dtypes],
        compiler_params=pltpu.CompilerParams(dimension_semantics=("parallel",)),
    )(*[a for a, _ in ins])
    return res


def _adamw_tiles(w, g, m, v):
    m = ADAM_B1 * m + (1.0 - ADAM_B1) * g
    v = ADAM_B2 * v + (1.0 - ADAM_B2) * jnp.square(g)
    m_hat = m / (1.0 - ADAM_B1 ** ADAM_STEP)
    v_hat = v / (1.0 - ADAM_B2 ** ADAM_STEP)
    delta = -ADAM_LR * (m_hat / (jnp.sqrt(v_hat) + ADAM_EPS) + ADAM_WD * w)
    return g, delta, m, v


def _adamw(name, w, g, m, v, tr):
    shape = w.shape
    flat = [a.reshape(shape[-2:]) if a.ndim > 2 else a for a in (w, g, m, v)]
    res = _elementwise(name, _adamw_tiles, flat, [F32] * 4, tr)
    return [a.reshape(shape) for a in res]


def _pair_sum(name, p, q, core, tr):
    n_p, _, hr, c = p.shape
    assert hr % tr == 0

    def body(core_ref, p_ref, q_ref, o_ref):
        o_ref[...] = (p_ref[...] + q_ref[...]).astype(o_ref.dtype)

    return pl.pallas_call(
        body,
        name=name,
        grid_spec=pltpu.PrefetchScalarGridSpec(
            num_scalar_prefetch=1,
            grid=(n_p, hr // tr),
            in_specs=[pl.BlockSpec((None, None, tr, c), lambda a, i, core_ref: (a, core_ref[0], i, 0)),
                      pl.BlockSpec((None, tr, c), lambda a, i, core_ref: (a, i, 0))],
            out_specs=pl.BlockSpec((None, tr, c), lambda a, i, core_ref: (a, i, 0)),
        ),
        out_shape=jax.ShapeDtypeStruct((n_p, hr, c), WIRE_DTYPE),
        compiler_params=pltpu.CompilerParams(dimension_semantics=("parallel", "parallel")),
    )(core, p, q)


def _into_slot(name, w, slots, slot, dtype, tr, deps=()):
    r, c = w.shape
    assert r % tr == 0

    def body(slot_ref, w_ref, *rest):
        o_ref = rest[len(deps)]
        o_ref[...] = w_ref[...].astype(o_ref.dtype)

    return pl.pallas_call(
        body,
        name=name,
        grid_spec=pltpu.PrefetchScalarGridSpec(
            num_scalar_prefetch=1,
            grid=(r // tr,),
            in_specs=[pl.BlockSpec((tr, c), lambda i, slot_ref: (i, 0))] + [ANY_SPEC] * len(deps),
            out_specs=pl.BlockSpec((None, tr, c), lambda i, slot_ref: (slot_ref[0], i, 0)),
        ),
        out_shape=jax.ShapeDtypeStruct((slots, r, c), dtype),
        compiler_params=pltpu.CompilerParams(dimension_semantics=("parallel",)),
    )(slot, w, *deps)


def _sum_pieces(name, own, rb, place, tr):
    n_p, hr, c = rb.shape
    assert hr % tr == 0

    def body(place_ref, own_ref, *refs):
        chip = place_ref[0]
        acc = None
        for k in range(n_p):
            tile = jnp.where(chip == k, own_ref[...], refs[k][...]).astype(F32)
            acc = tile if acc is None else acc + tile
        refs[n_p][...] = acc

    def landed(k):
        return pl.BlockSpec((None, tr, c), lambda i, place_ref: (jnp.where(place_ref[0] == k, (k + 1) % n_p, k), i, 0))

    return pl.pallas_call(
        body,
        name=name,
        grid_spec=pltpu.PrefetchScalarGridSpec(
            num_scalar_prefetch=1,
            grid=(hr // tr,),
            in_specs=[pl.BlockSpec((None, tr, c), lambda i, place_ref: (place_ref[0], i, 0))]
            + [landed(k) for k in range(n_p)],
            out_specs=pl.BlockSpec((None, tr, c), lambda i, place_ref: (place_ref[1], i, 0)),
        ),
        out_shape=jax.ShapeDtypeStruct((2, hr, c), F32),
        compiler_params=pltpu.CompilerParams(dimension_semantics=("parallel",)),
    )(place, own, *([rb] * n_p))


HBM_SPEC = pl.BlockSpec(memory_space=pl.ANY)


def _place():
    x, y, c = lax.axis_index("x"), lax.axis_index("y"), lax.axis_index("c")
    chips = [(1 - x, y), (x, 1 - y), (1 - x, 1 - y)]
    return x, y, c, chips


def _gather_shards(bufs, split):
    n = len(bufs)
    n_split = sum(split)
    fwd_slot = {a: s for s, a in enumerate([a for a in range(n) if split[a]])}

    def body(*refs):
        outs = refs[n:2 * n]
        send_sems, recv_sems, fsend_sems, frecv_sems = refs[2 * n:]
        x, y, c, chips = _place()
        me = 2 * x + y
        sibling = (x, y, 1 - c)

        def part(a, slot, h):
            if not split[a]:
                return outs[a].at[slot]
            hr = bufs[a].shape[1] // 2
            return outs[a].at[slot, pl.ds(h * hr, hr), :]

        sends = []
        for a in range(n):
            for j, chip in enumerate(chips):
                sends.append(pltpu.make_async_remote_copy(
                    src_ref=part(a, me, c), dst_ref=part(a, me, c),
                    send_sem=send_sems.at[3 * a + j], recv_sem=recv_sems.at[3 * a + j],
                    device_id=(*chip, c), device_id_type=MESH))
        for cp in sends:
            cp.start()
        passed = []
        for a in range(n):
            for j, chip in enumerate(chips):
                landed = part(a, 2 * chip[0] + chip[1], c)
                pltpu.make_async_remote_copy(
                    src_ref=landed, dst_ref=landed, send_sem=send_sems.at[3 * a + j], recv_sem=recv_sems.at[3 * a + j],
                    device_id=(*chip, c), device_id_type=MESH).wait_recv()
                if split[a]:
                    s = 3 * fwd_slot[a] + j
                    fwd = pltpu.make_async_remote_copy(
                        src_ref=landed, dst_ref=landed, send_sem=fsend_sems.at[s], recv_sem=frecv_sems.at[s],
                        device_id=sibling, device_id_type=MESH)
                    fwd.start()
                    passed.append(fwd)
        for a in range(n):
            if split[a]:
                for j, chip in enumerate(chips):
                    s = 3 * fwd_slot[a] + j
                    other = part(a, 2 * chip[0] + chip[1], 1 - c)
                    pltpu.make_async_remote_copy(
                        src_ref=other, dst_ref=other, send_sem=fsend_sems.at[s], recv_sem=frecv_sems.at[s],
                        device_id=sibling, device_id_type=MESH).wait_recv()
        for cp in sends + passed:
            cp.wait_send()

    return pl.pallas_call(
        body,
        name="gather_shards",
        in_specs=[HBM_SPEC] * n,
        out_specs=[HBM_SPEC] * n,
        out_shape=[jax.ShapeDtypeStruct(b.shape, b.dtype) for b in bufs],
        input_output_aliases={a: a for a in range(n)},
        scratch_shapes=[pltpu.SemaphoreType.DMA((3 * n,)), pltpu.SemaphoreType.DMA((3 * n,)),
                        pltpu.SemaphoreType.DMA((3 * n_split,)), pltpu.SemaphoreType.DMA((3 * n_split,))],
    )(*bufs)


def _swap_halves(name, parts):
    n = len(parts)
    n_p = parts[0].shape[0]

    def body(*refs):
        ins, outs = refs[:n], refs[n:2 * n]
        send_sems, recv_sems = refs[2 * n:]
        x, y, c, _ = _place()
        copies = []
        for a in range(n):
            for p in range(n_p):
                copies.append(pltpu.make_async_remote_copy(
                    src_ref=ins[a].at[p, 1 - c], dst_ref=outs[a].at[p],
                    send_sem=send_sems.at[n_p * a + p], recv_sem=recv_sems.at[n_p * a + p],
                    device_id=(x, y, 1 - c), device_id_type=MESH))
        for cp in copies:
            cp.start()
        for cp in copies:
            cp.wait()

    return pl.pallas_call(
        body,
        name=name,
        in_specs=[HBM_SPEC] * n,
        out_specs=[HBM_SPEC] * n,
        out_shape=[jax.ShapeDtypeStruct((n_p, *p.shape[2:]), p.dtype) for p in parts],
        scratch_shapes=[pltpu.SemaphoreType.DMA((n_p * n,)), pltpu.SemaphoreType.DMA((n_p * n,))],
    )(*parts)


def _exchange_pieces(sums):
    n = len(sums)

    def body(*refs):
        ins, outs = refs[:n], refs[n:2 * n]
        send_sems, recv_sems = refs[2 * n:]
        x, y, c, chips = _place()
        me = 2 * x + y
        sends = []
        for a in range(n):
            for j, chip in enumerate(chips):
                sends.append(pltpu.make_async_remote_copy(
                    src_ref=ins[a].at[2 * chip[0] + chip[1]], dst_ref=outs[a].at[me],
                    send_sem=send_sems.at[3 * a + j], recv_sem=recv_sems.at[3 * a + j],
                    device_id=(*chip, c), device_id_type=MESH))
        for cp in sends:
            cp.start()
        for a in range(n):
            for j, chip in enumerate(chips):
                slot = outs[a].at[2 * chip[0] + chip[1]]
                pltpu.make_async_remote_copy(
                    src_ref=slot, dst_ref=slot, send_sem=send_sems.at[3 * a + j], recv_sem=recv_sems.at[3 * a + j],
                    device_id=(*chip, c), device_id_type=MESH).wait_recv()
        for cp in sends:
            cp.wait_send()

    return pl.pallas_call(
        body,
        name="exchange_pieces",
        in_specs=[HBM_SPEC] * n,
        out_specs=[HBM_SPEC] * n,
        out_shape=[jax.ShapeDtypeStruct(s.shape, s.dtype) for s in sums],
        scratch_shapes=[pltpu.SemaphoreType.DMA((3 * n,)), pltpu.SemaphoreType.DMA((3 * n,))],
    )(*sums)


def _join_halves(name, bufs):
    n = len(bufs)

    def body(*refs):
        outs = refs[n:2 * n]
        send_sems, recv_sems = refs[2 * n:]
        x, y, c, _ = _place()
        copies = [pltpu.make_async_remote_copy(
            src_ref=outs[a].at[c], dst_ref=outs[a].at[c], send_sem=send_sems.at[a], recv_sem=recv_sems.at[a],
            device_id=(x, y, 1 - c), device_id_type=MESH) for a in range(n)]
        for cp in copies:
            cp.start()
        for a in range(n):
            other = outs[a].at[1 - c]
            pltpu.make_async_remote_copy(
                src_ref=other, dst_ref=other, send_sem=send_sems.at[a], recv_sem=recv_sems.at[a],
                device_id=(x, y, 1 - c), device_id_type=MESH).wait_recv()
        for cp in copies:
            cp.wait_send()

    return pl.pallas_call(
        body,
        name=name,
        in_specs=[HBM_SPEC] * n,
        out_specs=[HBM_SPEC] * n,
        out_shape=[jax.ShapeDtypeStruct(b.shape, b.dtype) for b in bufs],
        input_output_aliases={a: a for a in range(n)},
        scratch_shapes=[pltpu.SemaphoreType.DMA((n,)), pltpu.SemaphoreType.DMA((n,))],
    )(*bufs)


HBM_ONLY = pl.BlockSpec(memory_space=pltpu.HBM)
SEM_SPEC = pl.BlockSpec(memory_space=pltpu.SEMAPHORE)
DATAFLOW = pltpu.SideEffectType.DATAFLOW_SIDE_EFFECTING


def _in_hbm(a):
    return pltpu.with_memory_space_constraint(a, pltpu.HBM)


def _shard_part(ref, is_split, slot, h):
    if not is_split:
        return ref.at[slot]
    hr = ref.shape[1] // 2
    return ref.at[slot, pl.ds(h * hr, hr), :]


TOKEN = jax.ShapeDtypeStruct((8, LANES), F32)
VMEM_SPEC = pl.BlockSpec(memory_space=pltpu.VMEM)


def _gather_start(name, bufs, split, groups, deps=()):
    n, ng = len(bufs), len(groups)

    def body(*refs):
        ins, sems = refs[:n], refs[n + len(deps):n + len(deps) + 2 * ng]
        refs[-1][...] = jnp.zeros(TOKEN.shape, TOKEN.dtype)
        x, y, c, chips = _place()
        me = 2 * x + y
        for g, members in enumerate(groups):
            for s, a in enumerate(members):
                mine = _shard_part(ins[a], split[a], me, c)
                for j, chip in enumerate(chips):
                    pltpu.make_async_remote_copy(
                        src_ref=mine, dst_ref=mine, send_sem=sems[2 * g].at[3 * s + j], recv_sem=sems[2 * g + 1].at[3 * s + j],
                        device_id=(*chip, c), device_id_type=MESH).start()

    res = pl.pallas_call(
        body,
        name=name,
        in_specs=[HBM_ONLY] * n + [ANY_SPEC] * len(deps),
        out_specs=[SEM_SPEC] * (2 * ng) + [HBM_ONLY] * n + [VMEM_SPEC],
        out_shape=[pltpu.SemaphoreType.DMA((3 * len(members),)) for members in groups for _ in range(2)]
        + [pltpu.HBM(b.shape, b.dtype) for b in bufs] + [TOKEN],
        input_output_aliases={a: 2 * ng + a for a in range(n)},
        compiler_params=pltpu.CompilerParams(has_side_effects=DATAFLOW),
    )(*[_in_hbm(b) for b in bufs], *deps)
    return [(res[2 * g], res[2 * g + 1]) for g in range(ng)], list(res[2 * ng:2 * ng + n]), res[-1]


def _gather_wait(name, bufs, split, sems, after):
    n = len(bufs)

    def body(*refs):
        ins, send_sems, recv_sems = refs[:n], refs[n], refs[n + 1]
        x, y, c, chips = _place()
        me = 2 * x + y
        for s in range(n):
            for j, chip in enumerate(chips):
                copy = pltpu.make_async_remote_copy(
                    src_ref=_shard_part(ins[s], split[s], me, c),
                    dst_ref=_shard_part(ins[s], split[s], 2 * chip[0] + chip[1], c),
                    send_sem=send_sems.at[3 * s + j], recv_sem=recv_sems.at[3 * s + j],
                    device_id=(*chip, c), device_id_type=MESH)
                copy.wait_send()
                copy.wait_recv()

    res = pl.pallas_call(
        body,
        name=name,
        in_specs=[HBM_ONLY] * n + [SEM_SPEC, SEM_SPEC] + [ANY_SPEC] * len(after),
        out_specs=[HBM_ONLY] * n,
        out_shape=[pltpu.HBM(b.shape, b.dtype) for b in bufs],
        input_output_aliases={a: a for a in range(n)},
        compiler_params=pltpu.CompilerParams(has_side_effects=DATAFLOW),
    )(*bufs, *sems, *after)
    return list(res)


def _pass_halves(name, bufs):
    n = len(bufs)

    def body(*refs):
        outs = refs[n:2 * n]
        send_sems, recv_sems = refs[2 * n:]
        x, y, c, chips = _place()
        sibling = (x, y, 1 - c)

        def copy(a, j, h):
            blk = _shard_part(outs[a], True, 2 * chips[j][0] + chips[j][1], h)
            return pltpu.make_async_remote_copy(
                src_ref=blk, dst_ref=blk, send_sem=send_sems.at[3 * a + j], recv_sem=recv_sems.at[3 * a + j],
                device_id=sibling, device_id_type=MESH)

        sends = [copy(a, j, c) for a in range(n) for j in range(3)]
        for cp in sends:
            cp.start()
        for a in range(n):
            for j in range(3):
                copy(a, j, 1 - c).wait_recv()
        for cp in sends:
            cp.wait_send()

    res = pl.pallas_call(
        body,
        name=name,
        in_specs=[HBM_SPEC] * n,
        out_specs=[HBM_SPEC] * n,
        out_shape=[jax.ShapeDtypeStruct(b.shape, b.dtype) for b in bufs],
        input_output_aliases={a: a for a in range(n)},
        scratch_shapes=[pltpu.SemaphoreType.DMA((3 * n,)), pltpu.SemaphoreType.DMA((3 * n,))],
    )(*bufs)
    return list(res)


def _exchange_start(name, pair):
    def body(pair_ref, land_ref, send_sems, recv_sems, pair_thru, land_thru, token):
        x, y, c, chips = _place()
        me = 2 * x + y
        for j, chip in enumerate(chips):
            pltpu.make_async_remote_copy(
                src_ref=pair_ref.at[2 * chip[0] + chip[1]], dst_ref=land_ref.at[me],
                send_sem=send_sems.at[j], recv_sem=recv_sems.at[j], device_id=(*chip, c), device_id_type=MESH).start()
        token[...] = jnp.zeros(TOKEN.shape, TOKEN.dtype)

    send_sems, recv_sems, pair_thru, land_thru, token = pl.pallas_call(
        body,
        name=name,
        in_specs=[HBM_ONLY, HBM_ONLY],
        out_specs=[SEM_SPEC, SEM_SPEC, HBM_ONLY, HBM_ONLY, VMEM_SPEC],
        out_shape=[pltpu.SemaphoreType.DMA((3,)), pltpu.SemaphoreType.DMA((3,)),
                   pltpu.HBM(pair.shape, pair.dtype), pltpu.HBM(pair.shape, pair.dtype), TOKEN],
        input_output_aliases={0: 2, 1: 3},
        compiler_params=pltpu.CompilerParams(has_side_effects=DATAFLOW),
    )(_in_hbm(pair), _in_hbm(lax.empty(pair.shape, pair.dtype)))
    return (send_sems, recv_sems), pair_thru, land_thru, token


def _swap_start(name, halves):
    n_p, _, hr, cols = halves.shape
    land_shape = (n_p, hr, cols)

    def body(halves_ref, land_ref, send_sems, recv_sems, halves_thru, land_thru, token):
        x, y, c, _ = _place()
        for p in range(n_p):
            pltpu.make_async_remote_copy(
                src_ref=halves_ref.at[p, 1 - c], dst_ref=land_ref.at[p], send_sem=send_sems.at[p], recv_sem=recv_sems.at[p],
                device_id=(x, y, 1 - c), device_id_type=MESH).start()
        token[...] = jnp.zeros(TOKEN.shape, TOKEN.dtype)

    send_sems, recv_sems, halves_thru, land_thru, token = pl.pallas_call(
        body,
        name=name,
        in_specs=[HBM_ONLY, HBM_ONLY],
        out_specs=[SEM_SPEC, SEM_SPEC, HBM_ONLY, HBM_ONLY, VMEM_SPEC],
        out_shape=[pltpu.SemaphoreType.DMA((n_p,)), pltpu.SemaphoreType.DMA((n_p,)),
                   pltpu.HBM(halves.shape, halves.dtype), pltpu.HBM(land_shape, halves.dtype), TOKEN],
        input_output_aliases={0: 2, 1: 3},
        compiler_params=pltpu.CompilerParams(has_side_effects=DATAFLOW),
    )(_in_hbm(halves), _in_hbm(lax.empty(land_shape, halves.dtype)))
    return (send_sems, recv_sems), halves_thru, land_thru, token


def _swap_wait(name, halves, land, sems, after):
    n_p = halves.shape[0]

    def body(halves_ref, land_ref, send_sems, recv_sems, *rest):
        x, y, c, _ = _place()
        for p in range(n_p):
            copy = pltpu.make_async_remote_copy(
                src_ref=halves_ref.at[p, 1 - c], dst_ref=land_ref.at[p], send_sem=send_sems.at[p], recv_sem=recv_sems.at[p],
                device_id=(x, y, 1 - c), device_id_type=MESH)
            copy.wait_send()
            copy.wait_recv()

    return pl.pallas_call(
        body,
        name=name,
        in_specs=[HBM_ONLY, HBM_ONLY, SEM_SPEC, SEM_SPEC] + [ANY_SPEC] * len(after),
        out_specs=[HBM_ONLY, HBM_ONLY],
        out_shape=[pltpu.HBM(halves.shape, halves.dtype), pltpu.HBM(land.shape, land.dtype)],
        input_output_aliases={0: 0, 1: 1},
        compiler_params=pltpu.CompilerParams(has_side_effects=DATAFLOW),
    )(halves, land, *sems, *after)


def _exchange_wait(name, pair, land, sems, after):
    def body(pair_ref, land_ref, send_sems, recv_sems, *rest):
        x, y, c, chips = _place()
        for j, chip in enumerate(chips):
            k = 2 * chip[0] + chip[1]
            copy = pltpu.make_async_remote_copy(
                src_ref=pair_ref.at[k], dst_ref=land_ref.at[k], send_sem=send_sems.at[j], recv_sem=recv_sems.at[j],
                device_id=(*chip, c), device_id_type=MESH)
            copy.wait_send()
            copy.wait_recv()

    return pl.pallas_call(
        body,
        name=name,
        in_specs=[HBM_ONLY, HBM_ONLY, SEM_SPEC, SEM_SPEC] + [HBM_SPEC] * len(after),
        out_specs=[HBM_ONLY, HBM_ONLY],
        out_shape=[pltpu.HBM(pair.shape, pair.dtype), pltpu.HBM(land.shape, land.dtype)],
        input_output_aliases={0: 0, 1: 1},
        compiler_params=pltpu.CompilerParams(has_side_effects=DATAFLOW),
    )(pair, land, *sems, *after)


def _allgather_small(block, deps=()):
    m_per, n = block.shape

    def body(x_ref, *rest):
        out_ref, send_sems, recv_sems, local_sem = rest[len(deps):]
        x, y, c, chips = _place()
        me, sibling = (x, y, c), (x, y, 1 - c)

        def rows(px, py, pc):
            return out_ref.at[pl.ds((4 * px + 2 * py + pc) * m_per, m_per), :]

        def copy(k, blk, to, src=None):
            return pltpu.make_async_remote_copy(
                src_ref=rows(*blk) if src is None else src, dst_ref=rows(*blk),
                send_sem=send_sems.at[k], recv_sem=recv_sems.at[k], device_id=to, device_id_type=MESH)

        mine = pltpu.make_async_copy(x_ref, rows(*me), local_sem)
        mine.start()
        first = [copy(0, me, sibling, src=x_ref)]
        first += [copy(1 + j, me, (*chip, c), src=x_ref) for j, chip in enumerate(chips)]
        for cp in first:
            cp.start()
        passed = [copy(4 + j, (*chip, c), sibling) for j, chip in enumerate(chips)]
        for j, chip in enumerate(chips):
            copy(1 + j, (*chip, c), me).wait_recv()
            passed[j].start()
        copy(0, sibling, me).wait_recv()
        for j, chip in enumerate(chips):
            copy(4 + j, (*chip, 1 - c), me).wait_recv()
        for cp in first + passed:
            cp.wait_send()
        mine.wait()

    return pl.pallas_call(
        body,
        name="allgather_small",
        out_shape=jax.ShapeDtypeStruct((8 * m_per, n), block.dtype),
        in_specs=[pl.BlockSpec(memory_space=pltpu.VMEM)] + [ANY_SPEC] * len(deps),
        out_specs=pl.BlockSpec(memory_space=pltpu.VMEM),
        scratch_shapes=[pltpu.SemaphoreType.DMA((7,)), pltpu.SemaphoreType.DMA((7,)), pltpu.SemaphoreType.DMA],
    )(block, *deps)


def _sum_blocks(name, gathered, n_blocks):
    r = gathered.shape[0] // n_blocks
    c = gathered.shape[1]

    def body(g_ref, o_ref):
        acc = g_ref[0:r, :]
        for b in range(1, n_blocks):
            acc = acc + g_ref[b * r:(b + 1) * r, :]
        o_ref[...] = acc

    return pl.pallas_call(body, name=name, out_shape=jax.ShapeDtypeStruct((r, c), F32))(gathered)


def _largest_tile(n, cap, mult):
    best = None
    for d in range(mult, min(n, cap) + 1, mult):
        if n % d == 0:
            best = d
    assert best is not None, (n, cap, mult)
    return best


def kernel(x, meta, g_pre_mix, w_in, b_gates, conf_dw_w, conf_dw_b, conf_ln_g, conf_ln_b, conf_w_pw, short_dw_w, short_w_out, w_o, g_post_mix, g_pre_mlp, w_up, w_down, g_post_mlp, loss_target, m_meta, m_g_pre_mix, m_w_in, m_b_gates, m_conf_dw_w, m_conf_dw_b, m_conf_ln_g, m_conf_ln_b, m_conf_w_pw, m_short_dw_w, m_short_w_out, m_w_o, m_g_post_mix, m_g_pre_mlp, m_w_up, m_w_down, m_g_post_mlp, v_meta, v_g_pre_mix, v_w_in, v_b_gates, v_conf_dw_w, v_conf_dw_b, v_conf_ln_g, v_conf_ln_b, v_conf_w_pw, v_short_dw_w, v_short_w_out, v_w_o, v_g_post_mix, v_g_pre_mlp, v_w_up, v_w_down, v_g_post_mlp):
    seq, d = x.shape[1], x.shape[2]
    t_real = seq + N_META
    t = -(-t_real // LANES) * LANES
    d_conf = conf_dw_b.shape[1]
    d_ff = w_up.shape[2] * N_CHIPS
    in_cols = w_in.shape[2] * N_CHIPS
    assert in_cols == 5 * d_conf + 2 * d and d == 2 * d_conf
    cw = d_conf
    core = lax.axis_index("c")
    chip = 2 * lax.axis_index("x") + lax.axis_index("y")

    tr = _largest_tile(t, 272, ROW_CHUNK)
    tm = t
    tn_in = _largest_tile(in_cols // N_CHIPS, 768, LANES)
    tn_d = _largest_tile(d, 1024, LANES)
    tn_h = _largest_tile(d, 512, LANES)
    tn_ff = _largest_tile(d_ff // N_CHIPS, 1024, LANES)
    tn_pw = _largest_tile(d // N_CHIPS, 512, LANES)

    big = [w_in[0], conf_w_pw[0], short_w_out[0], w_o[0], w_up[0], w_down[0]]
    chip_arr = chip.astype(jnp.int32).reshape(1)

    def cast(a, deps):
        return _into_slot(f"cast_w{a}", big[a], N_CHIPS, chip_arr, MXU_DTYPE, _largest_tile(big[a].shape[0], 256, 16), deps)

    small = [_into_slot(f"place_w{a}", w, N_CHIPS, chip_arr, F32, w.shape[0])
             for a, w in enumerate([meta, conf_dw_w[0], short_dw_w[0]])]
    sems_small, fly_small, tok_small = _gather_start("gather_start_small", small, [False] * 3, [[0, 1, 2]])
    sems_in, fly_in, tok_in = _gather_start("gather_start_in", [cast(0, [tok_small])], [True], [[0]])
    rest_groups = [[0, 1], [2], [3], [4]]
    sems_rest, fly_rest, tok_rest = _gather_start(
        "gather_start_rest", [cast(a, [tok_in]) for a in range(1, 6)], [True] * 5, rest_groups)

    def arrive(g, after):
        members = rest_groups[g]
        got = _gather_wait(f"gather_wait_rest{g}", [fly_rest[a] for a in members], [True] * len(members),
                           sems_rest[g], after)
        return _pass_halves(f"gather_pass_rest{g}", got)

    meta_g, wdw_g, w3_g = _gather_wait("gather_wait_small", fly_small, [False] * 3, sems_small[0], [tok_in])
    meta_full = jnp.transpose(meta_g, (1, 0, 2)).reshape(N_META, d)
    wdw = jnp.transpose(wdw_g, (1, 0, 2)).reshape(CONF_KERNEL, d_conf)
    w3 = jnp.transpose(w3_g, (1, 0, 2)).reshape(SHORT_KERNEL, d_conf)

    tail = jnp.zeros((t - t_real, d), F32)
    h0 = jnp.concatenate([meta_full, x[0], tail], axis=0)
    target = jnp.concatenate([jnp.zeros((N_META, d), F32), loss_target[0], tail], axis=0)

    def norm_in(i, rows, vecs):
        return [_rms_fwd(rows[0], vecs[0])], []

    (n_lp,) = _rowwise("norm_in", norm_in, [(h0, d, 0)], [g_pre_mix], [(d, MXU_DTYPE)], [], tr, deps=[tok_rest])
    (wg_in,) = _pass_halves("gather_pass_in", _gather_wait("gather_wait_in", fly_in, [True], sems_in[0], [n_lp]))
    proj =_matmul("proj", n_lp, wg_in, kind="nn", tm=tm, tn=tn_in, tk=d, out_dtypes=[F32])
    ac, c3 = _conv_fwd(proj, wdw, conf_dw_b, w3, d_conf)

    def ln_parts(ac_t, ln_g, ln_b):
        mu = jnp.mean(ac_t, axis=-1, keepdims=True)
        xc = ac_t - mu
        rstd = lax.rsqrt(jnp.mean(xc * xc, axis=-1, keepdims=True) + LN_EPS)
        xh = xc * rstd
        return xh, rstd, xh * ln_g + ln_b

    def branch_act(i, rows, vecs):
        ac_t, c3_t, bg_t = rows
        _, _, al = ln_parts(ac_t, vecs[0], vecs[1])
        return [al * _sigmoid(al), bg_t * c3_t], []

    a_act, s_lp = _rowwise("branch_act", branch_act, [(ac, cw, 0), (c3, cw, 0), (proj, cw, 2)],
                           [conf_ln_g, conf_ln_b], [(d_conf, MXU_DTYPE), (d_conf, MXU_DTYPE)], [], tr)
    wg_pw, wg_sout = arrive(0, [a_act])
    y_a = _matmul("y_a", a_act, wg_pw, kind="nn", tm=tm, tn=tn_pw, tk=d_conf, out_dtypes=[F32])
    y_b = _matmul("y_b", s_lp, wg_sout, kind="nn", tm=tm, tn=tn_pw, tk=d_conf, out_dtypes=[F32])

    gate_rows = [(proj, cw, 5), (proj, cw, 6), (proj, cw, 7), (proj, cw, 8)]

    def gates_of(rows, b):
        ga = _sigmoid(jnp.concatenate([rows[0], rows[1]], axis=1) + b[:, :d])
        gb = _sigmoid(jnp.concatenate([rows[2], rows[3]], axis=1) + b[:, d:])
        return ga, gb

    def gate(i, rows, vecs):
        ga, gb = gates_of(rows[2:], vecs[0])
        return [ga * rows[0] + gb * rows[1]], []

    (m_lp,) = _rowwise("gate", gate, [(y_a, d, 0), (y_b, d, 0)] + gate_rows, [b_gates], [(d, MXU_DTYPE)], [], tr)
    wg_o = arrive(1, [m_lp])[0].reshape(d, d)
    mix = _matmul("mix", m_lp, wg_o, kind="nn", tm=tm, tn=tn_d, tk=d, out_dtypes=[F32])

    def post_mix(i, rows, vecs):
        h1_t = rows[0] + _rms_fwd(rows[1], vecs[0])
        return [h1_t, _rms_fwd(h1_t, vecs[1])], []

    h1, n2_lp = _rowwise("post_mix", post_mix, [(h0, d, 0), (mix, d, 0)], [g_post_mix, g_pre_mlp],
                         [(d, F32), (d, MXU_DTYPE)], [], tr)
    (wg_up,) = arrive(2, [n2_lp])
    up, f_lp = _matmul("up", n2_lp, wg_up, kind="nn", tm=tm, tn=tn_h, tk=d, out_dtypes=[F32, MXU_DTYPE],
                       epilogue=lambda acc: (acc, jnp.square(jnp.maximum(acc, 0.0))))
    wg_down = arrive(3, [f_lp])[0].reshape(d_ff, d)
    dn = _matmul("down", f_lp, wg_down, kind="nn", tm=tm, tn=tn_h, tk=_largest_tile(d_ff, 2048, LANES),
                 out_dtypes=[F32])

    def head(i, rows, vecs):
        h1_t, dn_t, tgt = rows
        y = h1_t + _rms_fwd(dn_t, vecs[0])
        row = i + lax.broadcasted_iota(jnp.int32, (ROW_CHUNK, 1), 0)
        err = jnp.where(jnp.logical_and(row >= N_META, row < t_real), y - tgt, 0.0)
        dy = err / d
        d_dn, dg = _rms_bwd(dn_t, vecs[0], dy)
        loss_rows = 0.5 * jnp.mean(err * err, axis=-1, keepdims=True)
        return [dy, d_dn], [dg, jnp.broadcast_to(loss_rows, (ROW_CHUNK, LANES))]

    dy, d_dn, dg_post_mlp, loss_vec = _rowwise(
        "head", head, [(h1, d, 0), (dn, d, 0), (target, d, 0)], [g_post_mlp], [(d, F32), (d, MXU_DTYPE)], [d, LANES], tr)
    loss = lax.psum(loss_vec[0, 0], ("x", "y", "c"))

    core_arr = core.astype(jnp.int32).reshape(1)
    place = jnp.stack([chip, core]).astype(jnp.int32)
    in_flight = {}


    def swap_start(a, gw):
        halves = gw.reshape(N_CHIPS, 2, gw.shape[1] // 2, gw.shape[2])
        *in_flight[a], token = _swap_start(f"swap_start{a}", halves)
        return token

    def exchange_start(a, after):
        sems, halves, land = in_flight[a]
        halves, land = _swap_wait(f"swap_wait{a}", halves, land, sems, after)
        pair = _pair_sum(f"pair_sum{a}", halves, land, core_arr, _largest_tile(land.shape[1], 256, 16))
        *in_flight[a], token = _exchange_start(f"exchange_start{a}", pair)
        return token

    def reduce_finish(a, after):
        sems, pair, land = in_flight[a]
        pair, land = _exchange_wait(f"exchange_wait{a}", pair, land, sems, after)
        return _sum_pieces(f"sum_pieces{a}", pair, land, place, _largest_tile(land.shape[1], 256, 16))

    d_up = _matmul("d_up", d_dn, wg_down, kind="nt", tm=tm, tn=tn_h, tk=d, out_dtypes=[MXU_DTYPE], extras=[up],
                   epilogue=lambda acc, up_t: (acc * (2.0 * jnp.maximum(up_t, 0.0)),))
    tk_t = t
    gw_down = _matmul("gw_down", f_lp, d_dn, kind="tn", tm=_largest_tile(d_ff, 2048, LANES), tn=tn_d, tk=tk_t,
                      out_dtypes=[F32])
    tok = swap_start(5, gw_down.reshape(N_CHIPS, d_ff // N_CHIPS, d))
    d_n2 = _matmul("d_n2", d_up, wg_up, kind="nt", tm=tm, tn=tn_h, tk=d_ff // N_CHIPS, out_dtypes=[F32], deps=[tok])
    tok = exchange_start(5, [d_n2])
    gw_up = _matmul("gw_up", n2_lp, d_up, kind="tn", tm=_largest_tile(d, 2048, LANES), tn=tn_ff, tk=tk_t,
                    out_dtypes=[F32], out_pieces=N_CHIPS, deps=[tok])
    tok = swap_start(4, gw_up)

    def bwd_mid(i, rows, vecs):
        dy_t, dn2_t, h1_t, mix_t = rows
        d_h1a, dg_pre_mlp = _rms_bwd(h1_t, vecs[1], dn2_t)
        d_h1 = dy_t + d_h1a
        d_mix, dg_post_mix = _rms_bwd(mix_t, vecs[0], d_h1)
        return [d_h1, d_mix], [dg_pre_mlp, dg_post_mix]

    d_h1, d_mix, dg_pre_mlp, dg_post_mix = _rowwise(
        "bwd_mid", bwd_mid, [(dy, d, 0), (d_n2, d, 0), (h1, d, 0), (mix, d, 0)], [g_post_mix, g_pre_mlp],
        [(d, F32), (d, MXU_DTYPE)], [d, d], tr, deps=[tok])
    tok = exchange_start(4, [d_mix])
    d_m = _matmul("d_m", d_mix, wg_o, kind="nt", tm=tm, tn=tn_d, tk=d, out_dtypes=[F32], deps=[tok])
    gw_o = _matmul("gw_o", m_lp, d_mix, kind="tn", tm=_largest_tile(d, 2048, LANES), tn=tn_d, tk=tk_t, out_dtypes=[F32])
    tok = swap_start(3, gw_o.reshape(N_CHIPS, d // N_CHIPS, d))

    def gate_bwd(i, rows, vecs):
        dm_t, ya_t, yb_t = rows[:3]
        ga, gb = gates_of(rows[3:], vecs[0])
        d_gpre = jnp.concatenate([dm_t * ya_t * ga * (1.0 - ga), dm_t * yb_t * gb * (1.0 - gb)], axis=1)
        return [dm_t * ga, dm_t * gb, d_gpre], [d_gpre]

    d_ya, d_yb, d_gpre, dg_b_gates = _rowwise(
        "gate_bwd", gate_bwd, [(d_m, d, 0), (y_a, d, 0), (y_b, d, 0)] + gate_rows, [b_gates],
        [(d, MXU_DTYPE), (d, MXU_DTYPE), (2 * d, MXU_DTYPE)], [2 * d], tr, deps=[tok])
    tok = exchange_start(3, [d_ya])
    d_aact = _matmul("d_aact", d_ya, wg_pw, kind="nt", tm=tm, tn=d_conf, tk=tn_pw, out_dtypes=[F32], deps=[tok])
    gw_pw = _matmul("gw_pw", a_act, d_ya, kind="tn", tm=_largest_tile(d_conf, 2048, LANES), tn=tn_pw, tk=tk_t,
                    out_dtypes=[F32], out_pieces=N_CHIPS)
    tok = swap_start(1, gw_pw)
    d_s = _matmul("d_s", d_yb, wg_sout, kind="nt", tm=tm, tn=d_conf, tk=tn_pw, out_dtypes=[F32], deps=[tok])
    tok = exchange_start(1, [d_s])
    gw_sout = _matmul("gw_sout", s_lp, d_yb, kind="tn", tm=_largest_tile(d_conf, 2048, LANES), tn=tn_pw, tk=tk_t,
                      out_dtypes=[F32], out_pieces=N_CHIPS, deps=[tok])
    tok = swap_start(2, gw_sout)

    def branch_bwd(i, rows, vecs):
        daact_t, ds_t, ac_t, c3_t, bg_t = rows
        xh, rstd, al = ln_parts(ac_t, vecs[0], vecs[1])
        sg = _sigmoid(al)
        d_al = daact_t * (sg * (1.0 + al * (1.0 - sg)))
        dxh = d_al * vecs[0]
        d_ac = rstd * (dxh - jnp.mean(dxh, axis=-1, keepdims=True) - xh * jnp.mean(dxh * xh, axis=-1, keepdims=True))
        return [d_ac, ds_t * bg_t, ds_t * c3_t], [d_al * xh, d_al, d_ac]

    d_ac, d_c3, d_bg, dg_ln_g, dg_ln_b, dg_dw_b = _rowwise(
        "branch_bwd", branch_bwd, [(d_aact, cw, 0), (d_s, cw, 0), (ac, cw, 0), (c3, cw, 0), (proj, cw, 2)],
        [conf_ln_g, conf_ln_b], [(d_conf, F32), (d_conf, F32), (d_conf, MXU_DTYPE)], [d_conf] * 3, tr, deps=[tok])
    tok = exchange_start(2, [d_ac])
    d_av, d_ag, d_cg, d_v, dg_wdw, dg_w3 = _conv_bwd(proj, d_ac, d_c3, wdw, w3, d_conf)
    d_proj = jnp.concatenate([d_av, d_ag, d_bg, d_cg, d_v, d_gpre], axis=1)
    d_n = _matmul("d_n", d_proj, wg_in, kind="nt", tm=tm, tn=tn_h, tk=in_cols // N_CHIPS, out_dtypes=[F32], deps=[tok])
    gw_in = _matmul("gw_in", n_lp, d_proj, kind="tn", tm=_largest_tile(d, 2048, LANES), tn=tn_in, tk=tk_t,
                    out_dtypes=[F32], out_pieces=N_CHIPS)
    tok = swap_start(0, gw_in)

    def bwd_in(i, rows, vecs):
        d_h0a, dg = _rms_bwd(rows[2], vecs[0], rows[1])
        return [rows[0] + d_h0a], [dg]

    d_h0, dg_pre_mix = _rowwise("bwd_in", bwd_in, [(d_h1, d, 0), (d_n, d, 0), (h0, d, 0)], [g_pre_mix],
                                [(d, F32)], [d], tr, deps=[tok])
    tok = exchange_start(0, [d_h0])
    grad_x = d_h0[N_META:t_real][None]

    big_m = [m_w_in, m_conf_w_pw, m_short_w_out, m_w_o, m_w_up, m_w_down]
    big_v = [v_w_in, v_conf_w_pw, v_short_w_out, v_w_o, v_w_up, v_w_down]
    big_res = {}

    def update_big(tag, members, after):
        reduced = []
        for a in members:
            reduced.append(reduce_finish(a, after))
            after = [reduced[-1]]
        joined = _join_halves(f"join_halves_{tag}", reduced)
        for a, j in zip(members, joined):
            big_res[a] = _adamw(f"adamw_big{a}", big[a], j.reshape(big[a].shape), big_m[a][0], big_v[a][0],
                                _largest_tile(big[a].shape[0], 256, 8))

    update_big("early", [5, 4, 3, 1, 2], [tok])

    small_w = d_conf
    rep = [dg_pre_mix, dg_b_gates, dg_dw_b, dg_ln_g, dg_ln_b, dg_post_mix, dg_pre_mlp, dg_post_mlp]
    rep_w = [g_pre_mix, b_gates, conf_dw_b, conf_ln_g, conf_ln_b, g_post_mix, g_pre_mlp, g_post_mlp]
    rep_m = [m_g_pre_mix, m_b_gates, m_conf_dw_b, m_conf_ln_g, m_conf_ln_b, m_g_post_mix, m_g_pre_mlp, m_g_post_mlp]
    rep_v = [v_g_pre_mix, v_b_gates, v_conf_dw_b, v_conf_ln_g, v_conf_ln_b, v_g_post_mix, v_g_pre_mlp, v_g_post_mlp]
    col = [dg_wdw, dg_w3, d_h0[:N_META]]
    assert all(a.size % small_w == 0 for a in rep + col)

    def pack(arrs):
        flat = jnp.concatenate([a.reshape(-1, small_w) for a in arrs], axis=0)
        return jnp.pad(flat, ((0, -flat.shape[0] % 8), (0, 0)))

    def unpack(buf, like):
        out, r0 = [], 0
        for a in like:
            nr = a.size // small_w
            out.append(buf[r0:r0 + nr].reshape(a.shape))
            r0 += nr
        return out

    packed = pack(rep + col)
    total = _sum_blocks("sum_small", _allgather_small(packed, deps=[tok]), 8)
    small_g = unpack(total, rep + col)
    g_rep = small_g[:len(rep)]
    g_wdw_full, g_w3_full, g_meta_full = small_g[len(rep):]
    sc = d_conf // N_CHIPS
    g_wdw = lax.dynamic_slice_in_dim(g_wdw_full, chip * sc, sc, axis=1)
    g_w3 = lax.dynamic_slice_in_dim(g_w3_full, chip * sc, sc, axis=1)
    g_meta = lax.dynamic_slice_in_dim(g_meta_full, chip * (d // N_CHIPS), d // N_CHIPS, axis=1)

    rep_pack = [pack(arrs) for arrs in (rep_w, g_rep, rep_m, rep_v)]
    rep_res = [unpack(buf, rep_w) for buf in _adamw("adamw_rep", *rep_pack, rep_pack[0].shape[0])]
    col_res = [_adamw(f"adamw_col{a}", w, g, m, v, w.shape[0]) for a, (w, g, m, v) in enumerate(
        [(meta, g_meta, m_meta, v_meta), (conf_dw_w[0], g_wdw, m_conf_dw_w[0], v_conf_dw_w[0]),
         (short_dw_w[0], g_w3, m_short_dw_w[0], v_short_dw_w[0])])]

    update_big("late", [0], [col_res[0][1], rep_res[1][0]] + [big_res[a][1] for a in (1, 2, 3, 4, 5)])

    def leaf(q):
        r = lambda a: rep_res[q][a]
        b = lambda a: big_res[a][q][None]
        return [col_res[0][q], r(0), b(0), r(1), col_res[1][q][None], r(2), r(3), r(4), b(1), col_res[2][q][None], b(2),
                b(3), r(5), r(6), b(4), b(5), r(7)]

    return (loss, grad_x, *leaf(0), *leaf(1), *leaf(2), *leaf(3))
```

```python
import functools

import jax
import jax.numpy as jnp
from jax import lax
from jax.experimental import pallas as pl
from jax.experimental.pallas import tpu as pltpu

F32 = jnp.float32
BF16 = jnp.bfloat16
MXU_DTYPE = BF16
WIRE_DTYPE = BF16

N_META = 16
CONF_KERNEL = 31
SHORT_KERNEL = 3
CONV_PAD = 32
RMS_EPS = 1e-6
LN_EPS = 1e-5
ADAM_LR = 0.001
ADAM_B1 = 0.9
ADAM_B2 = 0.999
ADAM_EPS = 1e-08
ADAM_WD = 0.01
ADAM_STEP = 10

N_CHIPS = 4
MESH = pl.DeviceIdType.MESH
LANES = 128


def _sigmoid(z):
    return 1.0 / (1.0 + jnp.exp(-z))


ANY_SPEC = pl.BlockSpec(memory_space=pl.ANY)


def _matmul(name, a, b, *, kind, tm, tn, tk, out_dtypes, out_pieces=1, epilogue=None, extras=(), deps=()):
    pieces = b.shape[0] if b.ndim == 3 else 1
    if kind == "nn":
        m, kdim = a.shape
        n = b.shape[-1] * pieces
        dims = (((1,), (0,)), ((), ()))
        a_spec = pl.BlockSpec((tm, tk), lambda i, j, k: (i, k))
        if b.ndim == 2:
            b_spec = pl.BlockSpec((tk, tn), lambda i, j, k: (k, j))
        else:
            npp = b.shape[-1] // tn
            b_spec = pl.BlockSpec((None, tk, tn), lambda i, j, k: (j // npp, k, j % npp))
    elif kind == "nt":
        m, kdim = a.shape
        n = b.shape[-2]
        dims = (((1,), (1,)), ((), ()))
        a_spec = pl.BlockSpec((tm, tk), lambda i, j, k: (i, k))
        if b.ndim == 2:
            b_spec = pl.BlockSpec((tn, tk), lambda i, j, k: (j, k))
        else:
            kpp = b.shape[-1] // tk
            b_spec = pl.BlockSpec((None, tn, tk), lambda i, j, k: (k // kpp, j, k % kpp))
    else:
        kdim, m = a.shape
        n = b.shape[-1]
        dims = (((0,), (0,)), ((), ()))
        a_spec = pl.BlockSpec((tk, tm), lambda i, j, k: (k, i))
        b_spec = pl.BlockSpec((tk, tn), lambda i, j, k: (k, j))
    assert m % tm == 0 and n % tn == 0 and kdim % tk == 0, (name, m, n, kdim, tm, tn, tk)
    nk = kdim // tk
    if out_pieces == 1:
        out_shape = (m, n)
        out_spec = pl.BlockSpec((tm, tn), lambda i, j, k: (i, j))
    else:
        onpp = n // out_pieces // tn
        out_shape = (out_pieces, m, n // out_pieces)
        out_spec = pl.BlockSpec((None, tm, tn), lambda i, j, k: (j // onpp, i, j % onpp))
    n_ex, n_out, n_in = len(extras), len(out_dtypes), len(extras) + len(deps)
    if epilogue is None:
        epilogue = lambda acc: (acc,)

    def body(a_ref, b_ref, *rest):
        ex_refs, o_refs = rest[:n_ex], rest[n_in:n_in + n_out]
        prod = lax.dot_general(a_ref[...], b_ref[...], dims, preferred_element_type=F32)

        def finish(acc):
            tiles = epilogue(acc, *[r[...] for r in ex_refs])
            for o_ref, t in zip(o_refs, tiles):
                o_ref[...] = t.astype(o_ref.dtype)

        if nk == 1:
            finish(prod)
        else:
            acc_ref = rest[n_in + n_out]
            k = pl.program_id(2)

            @pl.when(k == 0)
            def _():
                acc_ref[...] = prod

            @pl.when(jnp.logical_and(k > 0, k < nk - 1))
            def _():
                acc_ref[...] += prod

            @pl.when(k == nk - 1)
            def _():
                finish(acc_ref[...] + prod)

    ex_specs = [pl.BlockSpec((tm, tn), lambda i, j, k: (i, j)) for _ in extras]
    res = pl.pallas_call(
        body,
        name=name,
        grid=(m // tm, n // tn, nk),
        in_specs=[a_spec, b_spec, *ex_specs] + [ANY_SPEC] * len(deps),
        out_specs=[out_spec] * n_out,
        out_shape=[jax.ShapeDtypeStruct(out_shape, d) for d in out_dtypes],
        scratch_shapes=[pltpu.VMEM((tm, tn), F32)] if nk > 1 else [],
        compiler_params=pltpu.CompilerParams(dimension_semantics=("parallel", "parallel", "arbitrary")),
    )(a, b, *extras, *deps)
    return res[0] if n_out == 1 else res


ROW_CHUNK = 16
SUBLANES = 8


def _rowwise(name, fn, rows, vecs, outs, sums, tr, deps=()):
    t = rows[0][0].shape[0]
    assert t % tr == 0 and tr % ROW_CHUNK == 0
    n_r, n_v, n_o, n_s = len(rows), len(vecs), len(outs), len(sums)
    n_in = n_r + n_v + len(deps)
    n_steps = t // tr

    def body(*refs):
        r_in, v_in = refs[:n_r], refs[n_r:n_r + n_v]
        o_refs = refs[n_in:n_in + n_o]
        s_refs = refs[n_in + n_o:n_in + n_o + n_s]
        acc_refs = refs[n_in + n_o + n_s:]
        i = pl.program_id(0)

        @pl.when(i == 0)
        def _():
            for acc_ref in acc_refs:
                acc_ref[...] = jnp.zeros(acc_ref.shape, F32)

        def chunk(ci):
            r0 = ci * ROW_CHUNK
            sl = pl.ds(r0, ROW_CHUNK)
            o_tiles, s_tiles = fn(i * tr + r0, [r[sl, :] for r in r_in], [v[...] for v in v_in])
            for o_ref, tile in zip(o_refs, o_tiles):
                o_ref[sl, :] = tile.astype(o_ref.dtype)
            for acc_ref, tile in zip(acc_refs, s_tiles):
                part = tile[0:SUBLANES]
                for s in range(1, ROW_CHUNK // SUBLANES):
                    part = part + tile[s * SUBLANES:(s + 1) * SUBLANES]
                acc_ref[...] += part

        for ci in range(tr // ROW_CHUNK):
            chunk(ci)

        @pl.when(i == n_steps - 1)
        def _():
            for s_ref, acc_ref in zip(s_refs, acc_refs):
                s_ref[...] = jnp.sum(acc_ref[...], axis=0, keepdims=True)

    def row_spec(width, blk):
        return pl.BlockSpec((tr, width), lambda i: (i, blk))

    res = pl.pallas_call(
        body,
        name=name,
        grid=(t // tr,),
        in_specs=[row_spec(w, blk) for _, w, blk in rows]
        + [pl.BlockSpec(v.shape, lambda i: (0, 0)) for v in vecs] + [ANY_SPEC] * len(deps),
        out_specs=[pl.BlockSpec((tr, c), lambda i: (i, 0)) for c, _ in outs]
        + [pl.BlockSpec((1, c), lambda i: (0, 0)) for c in sums],
        out_shape=[jax.ShapeDtypeStruct((t, c), d) for c, d in outs]
        + [jax.ShapeDtypeStruct((1, c), F32) for c in sums],
        scratch_shapes=[pltpu.VMEM((SUBLANES, c), F32) for c in sums],
        compiler_params=pltpu.CompilerParams(dimension_semantics=("arbitrary",)),
    )(*[r[0] for r in rows], *vecs, *deps)
    return res


def _rms_fwd(x, g):
    r = lax.rsqrt(jnp.mean(x * x, axis=-1, keepdims=True) + RMS_EPS)
    return x * r * g


def _rms_bwd(x, g, dy):
    r = lax.rsqrt(jnp.mean(x * x, axis=-1, keepdims=True) + RMS_EPS)
    xn = x * r
    dxn = dy * g
    dx = r * (dxn - xn * jnp.mean(dxn * xn, axis=-1, keepdims=True))
    return dx, dy * xn


CONV_ROWS = 64
CONV_LANES = 128


def _conv_fwd(proj, wdw, bdw, w3, d_conf):
    t = proj.shape[0]
    cl = CONV_LANES
    nb = d_conf // cl
    nchunk = t // CONV_ROWS
    assert t % CONV_ROWS == 0

    def body(av_ref, ag_ref, cg_ref, v_ref, wdw_ref, bdw_ref, w3_ref, ac_ref, c3_ref, apad, cpad):
        zeros = jnp.zeros((CONV_PAD, cl), F32)
        apad[0:CONV_PAD, :] = zeros
        cpad[0:CONV_PAD, :] = zeros
        apad[CONV_PAD:, :] = av_ref[...] * _sigmoid(ag_ref[...])
        cpad[CONV_PAD:, :] = cg_ref[...] * v_ref[...]

        def chunk(ci, carry):
            base = pl.multiple_of(ci * CONV_ROWS, 8)
            acc = jnp.zeros((CONV_ROWS, cl), F32) + bdw_ref[...]
            for k in range(CONF_KERNEL):
                off = CONV_PAD - (CONF_KERNEL - 1) + k
                acc = acc + apad[pl.ds(base + off, CONV_ROWS), :] * wdw_ref[k:k + 1, :]
            ac_ref[pl.ds(base, CONV_ROWS), :] = acc
            acc3 = jnp.zeros((CONV_ROWS, cl), F32)
            for k in range(SHORT_KERNEL):
                off = CONV_PAD - (SHORT_KERNEL - 1) + k
                acc3 = acc3 + cpad[pl.ds(base + off, CONV_ROWS), :] * w3_ref[k:k + 1, :]
            c3_ref[pl.ds(base, CONV_ROWS), :] = acc3
            return carry

        lax.fori_loop(0, nchunk, chunk, 0)

    def col(blk0):
        return pl.BlockSpec((t, cl), lambda j: (0, blk0 + j))

    return pl.pallas_call(
        body,
        name="conv_fwd",
        grid=(nb,),
        in_specs=[col(0), col(nb), col(3 * nb), col(4 * nb),
                  pl.BlockSpec((CONF_KERNEL, cl), lambda j: (0, j)),
                  pl.BlockSpec((1, cl), lambda j: (0, j)),
                  pl.BlockSpec((SHORT_KERNEL, cl), lambda j: (0, j))],
        out_specs=[pl.BlockSpec((t, cl), lambda j: (0, j))] * 2,
        out_shape=[jax.ShapeDtypeStruct((t, d_conf), F32)] * 2,
        scratch_shapes=[pltpu.VMEM((t + CONV_PAD, cl), F32)] * 2,
        compiler_params=pltpu.CompilerParams(dimension_semantics=("parallel",)),
    )(proj, proj, proj, proj, wdw, bdw, w3)


def _conv_bwd(proj, d_ac, d_c3, wdw, w3, d_conf):
    t = proj.shape[0]
    cl = CONV_LANES
    nb = d_conf // cl
    nchunk = t // CONV_ROWS
    nsub = CONV_ROWS // 8

    def fold(p):
        r = p[0:8]
        for s in range(1, nsub):
            r = r + p[8 * s:8 * s + 8]
        return r

    def body(av_ref, ag_ref, cg_ref, v_ref, dac_ref, dc3_ref, wdw_ref, w3_ref,
             dav_ref, dag_ref, dcg_ref, dv_ref, dwdw_ref, dw3_ref, apad, cpad, dapad, dcpad):
        zeros = jnp.zeros((CONV_PAD, cl), F32)
        apad[0:CONV_PAD, :] = zeros
        cpad[0:CONV_PAD, :] = zeros
        apad[CONV_PAD:, :] = av_ref[...] * _sigmoid(ag_ref[...])
        cpad[CONV_PAD:, :] = cg_ref[...] * v_ref[...]
        dapad[0:t, :] = dac_ref[...]
        dcpad[0:t, :] = dc3_ref[...]
        dapad[t:, :] = zeros
        dcpad[t:, :] = zeros

        def chunk(ci, accs):
            base = pl.multiple_of(ci * CONV_ROWS, 8)
            rows = pl.ds(base, CONV_ROWS)
            da = jnp.zeros((CONV_ROWS, cl), F32)
            for k in range(CONF_KERNEL):
                da = da + dapad[pl.ds(base + (CONF_KERNEL - 1 - k), CONV_ROWS), :] * wdw_ref[k:k + 1, :]
            dcv = jnp.zeros((CONV_ROWS, cl), F32)
            for k in range(SHORT_KERNEL):
                dcv = dcv + dcpad[pl.ds(base + (SHORT_KERNEL - 1 - k), CONV_ROWS), :] * w3_ref[k:k + 1, :]
            av, sg = av_ref[rows, :], _sigmoid(ag_ref[rows, :])
            dav_ref[rows, :] = (da * sg).astype(dav_ref.dtype)
            dag_ref[rows, :] = (da * av * sg * (1.0 - sg)).astype(dag_ref.dtype)
            dcg_ref[rows, :] = (dcv * v_ref[rows, :]).astype(dcg_ref.dtype)
            dv_ref[rows, :] = (dcv * cg_ref[rows, :]).astype(dv_ref.dtype)
            d_out, d_out3 = dac_ref[rows, :], dc3_ref[rows, :]
            new = []
            for k in range(CONF_KERNEL):
                off = CONV_PAD - (CONF_KERNEL - 1) + k
                new.append(accs[k] + fold(d_out * apad[pl.ds(base + off, CONV_ROWS), :]))
            for k in range(SHORT_KERNEL):
                off = CONV_PAD - (SHORT_KERNEL - 1) + k
                new.append(accs[CONF_KERNEL + k] + fold(d_out3 * cpad[pl.ds(base + off, CONV_ROWS), :]))
            return tuple(new)

        init = tuple(jnp.zeros((8, cl), F32) for _ in range(CONF_KERNEL + SHORT_KERNEL))
        accs = lax.fori_loop(0, nchunk, chunk, init)
        for k in range(CONF_KERNEL):
            dwdw_ref[k:k + 1, :] = jnp.sum(accs[k], axis=0, keepdims=True)
        for k in range(SHORT_KERNEL):
            dw3_ref[k:k + 1, :] = jnp.sum(accs[CONF_KERNEL + k], axis=0, keepdims=True)

    def col(blk0):
        return pl.BlockSpec((t, cl), lambda j: (0, blk0 + j))

    own = pl.BlockSpec((t, cl), lambda j: (0, j))
    return pl.pallas_call(
        body,
        name="conv_bwd",
        grid=(nb,),
        in_specs=[col(0), col(nb), col(3 * nb), col(4 * nb), own, own,
                  pl.BlockSpec((CONF_KERNEL, cl), lambda j: (0, j)),
                  pl.BlockSpec((SHORT_KERNEL, cl), lambda j: (0, j))],
        out_specs=[own] * 4 + [pl.BlockSpec((CONF_KERNEL, cl), lambda j: (0, j)),
                               pl.BlockSpec((SHORT_KERNEL, cl), lambda j: (0, j))],
        out_shape=[jax.ShapeDtypeStruct((t, d_conf), MXU_DTYPE)] * 4
        + [jax.ShapeDtypeStruct((CONF_KERNEL, d_conf), F32), jax.ShapeDtypeStruct((SHORT_KERNEL, d_conf), F32)],
        scratch_shapes=[pltpu.VMEM((t + CONV_PAD, cl), F32)] * 4,
        compiler_params=pltpu.CompilerParams(dimension_semantics=("parallel",)),
    )(proj, proj, proj, proj, d_ac, d_c3, wdw, w3)


def _elementwise(name, fn, ins, out_dtypes, tr, deps=()):
    ins = [(a, ()) if not isinstance(a, tuple) else a for a in ins]
    r, c = ins[0][0].shape[-2:]
    assert r % tr == 0, (name, r, tr)
    n_in = len(ins)

    def body(*refs):
        tiles = fn(*[x[...] for x in refs[:n_in]])
        for o_ref, tile in zip(refs[n_in + len(deps):], tiles):
            o_ref[...] = tile.astype(o_ref.dtype)

    def spec(lead):
        return pl.BlockSpec((None,) * len(lead) + (tr, c), lambda i: (*lead, i, 0))

    res = pl.pallas_call(
        body,
        name=name,
        grid=(r // tr,),
        in_specs=[spec(lead) for _, lead in ins] + [ANY_SPEC] * len(deps),
        out_specs=[pl.BlockSpec((tr, c), lambda i: (i, 0))] * len(out_dtypes),
        out_shape=[jax.ShapeDtypeStruct((r, c), d) for d in out_dtypes],
        compiler_params=pltpu.CompilerParams(dimension_semantics=("parallel",)),
    )(*[a for a, _ in ins], *deps)
    return res


def _adamw_tiles(w, g, m, v):
    m = ADAM_B1 * m + (1.0 - ADAM_B1) * g
    v = ADAM_B2 * v + (1.0 - ADAM_B2) * jnp.square(g)
    m_hat = m / (1.0 - ADAM_B1 ** ADAM_STEP)
    v_hat = v / (1.0 - ADAM_B2 ** ADAM_STEP)
    delta = -ADAM_LR * (m_hat / (jnp.sqrt(v_hat) + ADAM_EPS) + ADAM_WD * w)
    return g, delta, m, v


def _adamw(name, w, g, m, v, tr, deps=()):
    shape = w.shape
    flat = [a.reshape(shape[-2:]) if a.ndim > 2 else a for a in (w, g, m, v)]
    res = _elementwise(name, _adamw_tiles, flat, [F32] * 4, tr, deps)
    return [a.reshape(shape) for a in res]


def _pair_sum(name, p, q, core, tr):
    n_p, _, hr, c = p.shape
    assert hr % tr == 0

    def body(core_ref, p_ref, q_ref, o_ref):
        o_ref[...] = (p_ref[...] + q_ref[...]).astype(o_ref.dtype)

    return pl.pallas_call(
        body,
        name=name,
        grid_spec=pltpu.PrefetchScalarGridSpec(
            num_scalar_prefetch=1,
            grid=(n_p, hr // tr),
            in_specs=[pl.BlockSpec((None, None, tr, c), lambda a, i, core_ref: (a, core_ref[0], i, 0)),
                      pl.BlockSpec((None, tr, c), lambda a, i, core_ref: (a, i, 0))],
            out_specs=pl.BlockSpec((None, tr, c), lambda a, i, core_ref: (a, i, 0)),
        ),
        out_shape=jax.ShapeDtypeStruct((n_p, hr, c), WIRE_DTYPE),
        compiler_params=pltpu.CompilerParams(dimension_semantics=("parallel", "parallel")),
    )(core, p, q)


def _into_slot(name, w, slots, slot, dtype, tr, deps=()):
    r, c = w.shape
    assert r % tr == 0

    def body(slot_ref, w_ref, *rest):
        o_ref = rest[len(deps)]
        o_ref[...] = w_ref[...].astype(o_ref.dtype)

    return pl.pallas_call(
        body,
        name=name,
        grid_spec=pltpu.PrefetchScalarGridSpec(
            num_scalar_prefetch=1,
            grid=(r // tr,),
            in_specs=[pl.BlockSpec((tr, c), lambda i, slot_ref: (i, 0))] + [ANY_SPEC] * len(deps),
            out_specs=pl.BlockSpec((None, tr, c), lambda i, slot_ref: (slot_ref[0], i, 0)),
        ),
        out_shape=jax.ShapeDtypeStruct((slots, r, c), dtype),
        compiler_params=pltpu.CompilerParams(dimension_semantics=("parallel",)),
    )(slot, w, *deps)


def _sum_pieces(name, own, rb, place, tr):
    n_p, hr, c = rb.shape
    assert hr % tr == 0

    def body(place_ref, own_ref, *refs):
        chip = place_ref[0]
        acc = None
        for k in range(n_p):
            tile = jnp.where(chip == k, own_ref[...], refs[k][...]).astype(F32)
            acc = tile if acc is None else acc + tile
        refs[n_p][...] = acc

    def landed(k):
        return pl.BlockSpec((None, tr, c), lambda i, place_ref: (jnp.where(place_ref[0] == k, (k + 1) % n_p, k), i, 0))

    return pl.pallas_call(
        body,
        name=name,
        grid_spec=pltpu.PrefetchScalarGridSpec(
            num_scalar_prefetch=1,
            grid=(hr // tr,),
            in_specs=[pl.BlockSpec((None, tr, c), lambda i, place_ref: (place_ref[0], i, 0))]
            + [landed(k) for k in range(n_p)],
            out_specs=pl.BlockSpec((None, tr, c), lambda i, place_ref: (place_ref[1], i, 0)),
        ),
        out_shape=jax.ShapeDtypeStruct((2, hr, c), F32),
        compiler_params=pltpu.CompilerParams(dimension_semantics=("parallel",)),
    )(place, own, *([rb] * n_p))


HBM_SPEC = pl.BlockSpec(memory_space=pl.ANY)


def _place():
    x, y, c = lax.axis_index("x"), lax.axis_index("y"), lax.axis_index("c")
    chips = [(1 - x, y), (x, 1 - y), (1 - x, 1 - y)]
    return x, y, c, chips


def _gather_shards(bufs, split):
    n = len(bufs)
    n_split = sum(split)
    fwd_slot = {a: s for s, a in enumerate([a for a in range(n) if split[a]])}

    def body(*refs):
        outs = refs[n:2 * n]
        send_sems, recv_sems, fsend_sems, frecv_sems = refs[2 * n:]
        x, y, c, chips = _place()
        me = 2 * x + y
        sibling = (x, y, 1 - c)

        def part(a, slot, h):
            if not split[a]:
                return outs[a].at[slot]
            hr = bufs[a].shape[1] // 2
            return outs[a].at[slot, pl.ds(h * hr, hr), :]

        sends = []
        for a in range(n):
            for j, chip in enumerate(chips):
                sends.append(pltpu.make_async_remote_copy(
                    src_ref=part(a, me, c), dst_ref=part(a, me, c),
                    send_sem=send_sems.at[3 * a + j], recv_sem=recv_sems.at[3 * a + j],
                    device_id=(*chip, c), device_id_type=MESH))
        for cp in sends:
            cp.start()
        passed = []
        for a in range(n):
            for j, chip in enumerate(chips):
                landed = part(a, 2 * chip[0] + chip[1], c)
                pltpu.make_async_remote_copy(
                    src_ref=landed, dst_ref=landed, send_sem=send_sems.at[3 * a + j], recv_sem=recv_sems.at[3 * a + j],
                    device_id=(*chip, c), device_id_type=MESH).wait_recv()
                if split[a]:
                    s = 3 * fwd_slot[a] + j
                    fwd = pltpu.make_async_remote_copy(
                        src_ref=landed, dst_ref=landed, send_sem=fsend_sems.at[s], recv_sem=frecv_sems.at[s],
                        device_id=sibling, device_id_type=MESH)
                    fwd.start()
                    passed.append(fwd)
        for a in range(n):
            if split[a]:
                for j, chip in enumerate(chips):
                    s = 3 * fwd_slot[a] + j
                    other = part(a, 2 * chip[0] + chip[1], 1 - c)
                    pltpu.make_async_remote_copy(
                        src_ref=other, dst_ref=other, send_sem=fsend_sems.at[s], recv_sem=frecv_sems.at[s],
                        device_id=sibling, device_id_type=MESH).wait_recv()
        for cp in sends + passed:
            cp.wait_send()

    return pl.pallas_call(
        body,
        name="gather_shards",
        in_specs=[HBM_SPEC] * n,
        out_specs=[HBM_SPEC] * n,
        out_shape=[jax.ShapeDtypeStruct(b.shape, b.dtype) for b in bufs],
        input_output_aliases={a: a for a in range(n)},
        scratch_shapes=[pltpu.SemaphoreType.DMA((3 * n,)), pltpu.SemaphoreType.DMA((3 * n,)),
                        pltpu.SemaphoreType.DMA((3 * n_split,)), pltpu.SemaphoreType.DMA((3 * n_split,))],
    )(*bufs)


def _swap_halves(name, parts):
    n = len(parts)
    n_p = parts[0].shape[0]

    def body(*refs):
        ins, outs = refs[:n], refs[n:2 * n]
        send_sems, recv_sems = refs[2 * n:]
        x, y, c, _ = _place()
        copies = []
        for a in range(n):
            for p in range(n_p):
                copies.append(pltpu.make_async_remote_copy(
                    src_ref=ins[a].at[p, 1 - c], dst_ref=outs[a].at[p],
                    send_sem=send_sems.at[n_p * a + p], recv_sem=recv_sems.at[n_p * a + p],
                    device_id=(x, y, 1 - c), device_id_type=MESH))
        for cp in copies:
            cp.start()
        for cp in copies:
            cp.wait()

    return pl.pallas_call(
        body,
        name=name,
        in_specs=[HBM_SPEC] * n,
        out_specs=[HBM_SPEC] * n,
        out_shape=[jax.ShapeDtypeStruct((n_p, *p.shape[2:]), p.dtype) for p in parts],
        scratch_shapes=[pltpu.SemaphoreType.DMA((n_p * n,)), pltpu.SemaphoreType.DMA((n_p * n,))],
    )(*parts)


def _exchange_pieces(sums):
    n = len(sums)

    def body(*refs):
        ins, outs = refs[:n], refs[n:2 * n]
        send_sems, recv_sems = refs[2 * n:]
        x, y, c, chips = _place()
        me = 2 * x + y
        sends = []
        for a in range(n):
            for j, chip in enumerate(chips):
                sends.append(pltpu.make_async_remote_copy(
                    src_ref=ins[a].at[2 * chip[0] + chip[1]], dst_ref=outs[a].at[me],
                    send_sem=send_sems.at[3 * a + j], recv_sem=recv_sems.at[3 * a + j],
                    device_id=(*chip, c), device_id_type=MESH))
        for cp in sends:
            cp.start()
        for a in range(n):
            for j, chip in enumerate(chips):
                slot = outs[a].at[2 * chip[0] + chip[1]]
                pltpu.make_async_remote_copy(
                    src_ref=slot, dst_ref=slot, send_sem=send_sems.at[3 * a + j], recv_sem=recv_sems.at[3 * a + j],
                    device_id=(*chip, c), device_id_type=MESH).wait_recv()
        for cp in sends:
            cp.wait_send()

    return pl.pallas_call(
        body,
        name="exchange_pieces",
        in_specs=[HBM_SPEC] * n,
        out_specs=[HBM_SPEC] * n,
        out_shape=[jax.ShapeDtypeStruct(s.shape, s.dtype) for s in sums],
        scratch_shapes=[pltpu.SemaphoreType.DMA((3 * n,)), pltpu.SemaphoreType.DMA((3 * n,))],
    )(*sums)


def _join_halves(name, bufs):
    n = len(bufs)

    def body(*refs):
        outs = refs[n:2 * n]
        send_sems, recv_sems = refs[2 * n:]
        x, y, c, _ = _place()
        copies = [pltpu.make_async_remote_copy(
            src_ref=outs[a].at[c], dst_ref=outs[a].at[c], send_sem=send_sems.at[a], recv_sem=recv_sems.at[a],
            device_id=(x, y, 1 - c), device_id_type=MESH) for a in range(n)]
        for cp in copies:
            cp.start()
        for a in range(n):
            other = outs[a].at[1 - c]
            pltpu.make_async_remote_copy(
                src_ref=other, dst_ref=other, send_sem=send_sems.at[a], recv_sem=recv_sems.at[a],
                device_id=(x, y, 1 - c), device_id_type=MESH).wait_recv()
        for cp in copies:
            cp.wait_send()

    return pl.pallas_call(
        body,
        name=name,
        in_specs=[HBM_SPEC] * n,
        out_specs=[HBM_SPEC] * n,
        out_shape=[jax.ShapeDtypeStruct(b.shape, b.dtype) for b in bufs],
        input_output_aliases={a: a for a in range(n)},
        scratch_shapes=[pltpu.SemaphoreType.DMA((n,)), pltpu.SemaphoreType.DMA((n,))],
    )(*bufs)


HBM_ONLY = pl.BlockSpec(memory_space=pltpu.HBM)
SEM_SPEC = pl.BlockSpec(memory_space=pltpu.SEMAPHORE)
DATAFLOW = pltpu.SideEffectType.DATAFLOW_SIDE_EFFECTING


def _in_hbm(a):
    return pltpu.with_memory_space_constraint(a, pltpu.HBM)


def _shard_part(ref, is_split, slot, h):
    if not is_split:
        return ref.at[slot]
    hr = ref.shape[1] // 2
    return ref.at[slot, pl.ds(h * hr, hr), :]


TOKEN = jax.ShapeDtypeStruct((8, LANES), F32)
VMEM_SPEC = pl.BlockSpec(memory_space=pltpu.VMEM)


def _gather_start(name, bufs, split, groups, deps=()):
    n, ng = len(bufs), len(groups)

    def body(*refs):
        ins, sems = refs[:n], refs[n + len(deps):n + len(deps) + 2 * ng]
        refs[-1][...] = jnp.zeros(TOKEN.shape, TOKEN.dtype)
        x, y, c, chips = _place()
        me = 2 * x + y
        for g, members in enumerate(groups):
            for s, a in enumerate(members):
                mine = _shard_part(ins[a], split[a], me, c)
                for j, chip in enumerate(chips):
                    pltpu.make_async_remote_copy(
                        src_ref=mine, dst_ref=mine, send_sem=sems[2 * g].at[3 * s + j], recv_sem=sems[2 * g + 1].at[3 * s + j],
                        device_id=(*chip, c), device_id_type=MESH).start()

    res = pl.pallas_call(
        body,
        name=name,
        in_specs=[HBM_ONLY] * n + [ANY_SPEC] * len(deps),
        out_specs=[SEM_SPEC] * (2 * ng) + [HBM_ONLY] * n + [VMEM_SPEC],
        out_shape=[pltpu.SemaphoreType.DMA((3 * len(members),)) for members in groups for _ in range(2)]
        + [pltpu.HBM(b.shape, b.dtype) for b in bufs] + [TOKEN],
        input_output_aliases={a: 2 * ng + a for a in range(n)},
        compiler_params=pltpu.CompilerParams(has_side_effects=DATAFLOW),
    )(*[_in_hbm(b) for b in bufs], *deps)
    return [(res[2 * g], res[2 * g + 1]) for g in range(ng)], list(res[2 * ng:2 * ng + n]), res[-1]


def _gather_wait(name, bufs, split, sems, after):
    n = len(bufs)

    def body(*refs):
        ins, send_sems, recv_sems = refs[:n], refs[n], refs[n + 1]
        x, y, c, chips = _place()
        me = 2 * x + y
        for s in range(n):
            for j, chip in enumerate(chips):
                copy = pltpu.make_async_remote_copy(
                    src_ref=_shard_part(ins[s], split[s], me, c),
                    dst_ref=_shard_part(ins[s], split[s], 2 * chip[0] + chip[1], c),
                    send_sem=send_sems.at[3 * s + j], recv_sem=recv_sems.at[3 * s + j],
                    device_id=(*chip, c), device_id_type=MESH)
                copy.wait_send()
                copy.wait_recv()

    res = pl.pallas_call(
        body,
        name=name,
        in_specs=[HBM_ONLY] * n + [SEM_SPEC, SEM_SPEC] + [ANY_SPEC] * len(after),
        out_specs=[HBM_ONLY] * n,
        out_shape=[pltpu.HBM(b.shape, b.dtype) for b in bufs],
        input_output_aliases={a: a for a in range(n)},
        compiler_params=pltpu.CompilerParams(has_side_effects=DATAFLOW),
    )(*bufs, *sems, *after)
    return list(res)


def _pass_halves(name, bufs):
    n = len(bufs)

    def body(*refs):
        outs = refs[n:2 * n]
        send_sems, recv_sems = refs[2 * n:]
        x, y, c, chips = _place()
        sibling = (x, y, 1 - c)

        def copy(a, j, h):
            blk = _shard_part(outs[a], True, 2 * chips[j][0] + chips[j][1], h)
            return pltpu.make_async_remote_copy(
                src_ref=blk, dst_ref=blk, send_sem=send_sems.at[3 * a + j], recv_sem=recv_sems.at[3 * a + j],
                device_id=sibling, device_id_type=MESH)

        sends = [copy(a, j, c) for a in range(n) for j in range(3)]
        for cp in sends:
            cp.start()
        for a in range(n):
            for j in range(3):
                copy(a, j, 1 - c).wait_recv()
        for cp in sends:
            cp.wait_send()

    res = pl.pallas_call(
        body,
        name=name,
        in_specs=[HBM_SPEC] * n,
        out_specs=[HBM_SPEC] * n,
        out_shape=[jax.ShapeDtypeStruct(b.shape, b.dtype) for b in bufs],
        input_output_aliases={a: a for a in range(n)},
        scratch_shapes=[pltpu.SemaphoreType.DMA((3 * n,)), pltpu.SemaphoreType.DMA((3 * n,))],
    )(*bufs)
    return list(res)


def _exchange_start(name, pair):
    def body(pair_ref, land_ref, send_sems, recv_sems, pair_thru, land_thru, token):
        x, y, c, chips = _place()
        me = 2 * x + y
        for j, chip in enumerate(chips):
            pltpu.make_async_remote_copy(
                src_ref=pair_ref.at[2 * chip[0] + chip[1]], dst_ref=land_ref.at[me],
                send_sem=send_sems.at[j], recv_sem=recv_sems.at[j], device_id=(*chip, c), device_id_type=MESH).start()
        token[...] = jnp.zeros(TOKEN.shape, TOKEN.dtype)

    send_sems, recv_sems, pair_thru, land_thru, token = pl.pallas_call(
        body,
        name=name,
        in_specs=[HBM_ONLY, HBM_ONLY],
        out_specs=[SEM_SPEC, SEM_SPEC, HBM_ONLY, HBM_ONLY, VMEM_SPEC],
        out_shape=[pltpu.SemaphoreType.DMA((3,)), pltpu.SemaphoreType.DMA((3,)),
                   pltpu.HBM(pair.shape, pair.dtype), pltpu.HBM(pair.shape, pair.dtype), TOKEN],
        input_output_aliases={0: 2, 1: 3},
        compiler_params=pltpu.CompilerParams(has_side_effects=DATAFLOW),
    )(_in_hbm(pair), _in_hbm(lax.empty(pair.shape, pair.dtype)))
    return (send_sems, recv_sems), pair_thru, land_thru, token


def _swap_start(name, halves):
    n_p, _, hr, cols = halves.shape
    land_shape = (n_p, hr, cols)

    def body(halves_ref, land_ref, send_sems, recv_sems, halves_thru, land_thru, token):
        x, y, c, _ = _place()
        for p in range(n_p):
            pltpu.make_async_remote_copy(
                src_ref=halves_ref.at[p, 1 - c], dst_ref=land_ref.at[p], send_sem=send_sems.at[p], recv_sem=recv_sems.at[p],
                device_id=(x, y, 1 - c), device_id_type=MESH).start()
        token[...] = jnp.zeros(TOKEN.shape, TOKEN.dtype)

    send_sems, recv_sems, halves_thru, land_thru, token = pl.pallas_call(
        body,
        name=name,
        in_specs=[HBM_ONLY, HBM_ONLY],
        out_specs=[SEM_SPEC, SEM_SPEC, HBM_ONLY, HBM_ONLY, VMEM_SPEC],
        out_shape=[pltpu.SemaphoreType.DMA((n_p,)), pltpu.SemaphoreType.DMA((n_p,)),
                   pltpu.HBM(halves.shape, halves.dtype), pltpu.HBM(land_shape, halves.dtype), TOKEN],
        input_output_aliases={0: 2, 1: 3},
        compiler_params=pltpu.CompilerParams(has_side_effects=DATAFLOW),
    )(_in_hbm(halves), _in_hbm(lax.empty(land_shape, halves.dtype)))
    return (send_sems, recv_sems), halves_thru, land_thru, token


def _swap_wait(name, halves, land, sems, after):
    n_p = halves.shape[0]

    def body(halves_ref, land_ref, send_sems, recv_sems, *rest):
        x, y, c, _ = _place()
        for p in range(n_p):
            copy = pltpu.make_async_remote_copy(
                src_ref=halves_ref.at[p, 1 - c], dst_ref=land_ref.at[p], send_sem=send_sems.at[p], recv_sem=recv_sems.at[p],
                device_id=(x, y, 1 - c), device_id_type=MESH)
            copy.wait_send()
            copy.wait_recv()

    return pl.pallas_call(
        body,
        name=name,
        in_specs=[HBM_ONLY, HBM_ONLY, SEM_SPEC, SEM_SPEC] + [ANY_SPEC] * len(after),
        out_specs=[HBM_ONLY, HBM_ONLY],
        out_shape=[pltpu.HBM(halves.shape, halves.dtype), pltpu.HBM(land.shape, land.dtype)],
        input_output_aliases={0: 0, 1: 1},
        compiler_params=pltpu.CompilerParams(has_side_effects=DATAFLOW),
    )(halves, land, *sems, *after)


def _exchange_wait(name, pair, land, sems, after):
    def body(pair_ref, land_ref, send_sems, recv_sems, *rest):
        x, y, c, chips = _place()
        for j, chip in enumerate(chips):
            k = 2 * chip[0] + chip[1]
            copy = pltpu.make_async_remote_copy(
                src_ref=pair_ref.at[k], dst_ref=land_ref.at[k], send_sem=send_sems.at[j], recv_sem=recv_sems.at[j],
                device_id=(*chip, c), device_id_type=MESH)
            copy.wait_send()
            copy.wait_recv()

    return pl.pallas_call(
        body,
        name=name,
        in_specs=[HBM_ONLY, HBM_ONLY, SEM_SPEC, SEM_SPEC] + [HBM_SPEC] * len(after),
        out_specs=[HBM_ONLY, HBM_ONLY],
        out_shape=[pltpu.HBM(pair.shape, pair.dtype), pltpu.HBM(land.shape, land.dtype)],
        input_output_aliases={0: 0, 1: 1},
        compiler_params=pltpu.CompilerParams(has_side_effects=DATAFLOW),
    )(pair, land, *sems, *after)


def _allgather_small(block, deps=()):
    m_per, n = block.shape

    def body(x_ref, *rest):
        out_ref, send_sems, recv_sems, local_sem = rest[len(deps):]
        x, y, c, chips = _place()
        me, sibling = (x, y, c), (x, y, 1 - c)

        def rows(px, py, pc):
            return out_ref.at[pl.ds((4 * px + 2 * py + pc) * m_per, m_per), :]

        def copy(k, blk, to, src=None):
            return pltpu.make_async_remote_copy(
                src_ref=rows(*blk) if src is None else src, dst_ref=rows(*blk),
                send_sem=send_sems.at[k], recv_sem=recv_sems.at[k], device_id=to, device_id_type=MESH)

        mine = pltpu.make_async_copy(x_ref, rows(*me), local_sem)
        mine.start()
        first = [copy(0, me, sibling, src=x_ref)]
        first += [copy(1 + j, me, (*chip, c), src=x_ref) for j, chip in enumerate(chips)]
        for cp in first:
            cp.start()
        passed = [copy(4 + j, (*chip, c), sibling) for j, chip in enumerate(chips)]
        for j, chip in enumerate(chips):
            copy(1 + j, (*chip, c), me).wait_recv()
            passed[j].start()
        copy(0, sibling, me).wait_recv()
        for j, chip in enumerate(chips):
            copy(4 + j, (*chip, 1 - c), me).wait_recv()
        for cp in first + passed:
            cp.wait_send()
        mine.wait()

    return pl.pallas_call(
        body,
        name="allgather_small",
        out_shape=jax.ShapeDtypeStruct((8 * m_per, n), block.dtype),
        in_specs=[pl.BlockSpec(memory_space=pltpu.VMEM)] + [ANY_SPEC] * len(deps),
        out_specs=pl.BlockSpec(memory_space=pltpu.VMEM),
        scratch_shapes=[pltpu.SemaphoreType.DMA((7,)), pltpu.SemaphoreType.DMA((7,)), pltpu.SemaphoreType.DMA],
    )(block, *deps)


def _sum_blocks(name, gathered, n_blocks):
    r = gathered.shape[0] // n_blocks
    c = gathered.shape[1]

    def body(g_ref, o_ref):
        acc = g_ref[0:r, :]
        for b in range(1, n_blocks):
            acc = acc + g_ref[b * r:(b + 1) * r, :]
        o_ref[...] = acc

    return pl.pallas_call(body, name=name, out_shape=jax.ShapeDtypeStruct((r, c), F32))(gathered)


def _largest_tile(n, cap, mult):
    best = None
    for d in range(mult, min(n, cap) + 1, mult):
        if n % d == 0:
            best = d
    assert best is not None, (n, cap, mult)
    return best


def kernel(x, meta, g_pre_mix, w_in, b_gates, conf_dw_w, conf_dw_b, conf_ln_g, conf_ln_b, conf_w_pw, short_dw_w, short_w_out, w_o, g_post_mix, g_pre_mlp, w_up, w_down, g_post_mlp, loss_target, m_meta, m_g_pre_mix, m_w_in, m_b_gates, m_conf_dw_w, m_conf_dw_b, m_conf_ln_g, m_conf_ln_b, m_conf_w_pw, m_short_dw_w, m_short_w_out, m_w_o, m_g_post_mix, m_g_pre_mlp, m_w_up, m_w_down, m_g_post_mlp, v_meta, v_g_pre_mix, v_w_in, v_b_gates, v_conf_dw_w, v_conf_dw_b, v_conf_ln_g, v_conf_ln_b, v_conf_w_pw, v_short_dw_w, v_short_w_out, v_w_o, v_g_post_mix, v_g_pre_mlp, v_w_up, v_w_down, v_g_post_mlp):
    seq, d = x.shape[1], x.shape[2]
    t_real = seq + N_META
    t = -(-t_real // LANES) * LANES
    d_conf = conf_dw_b.shape[1]
    d_ff = w_up.shape[2] * N_CHIPS
    in_cols = w_in.shape[2] * N_CHIPS
    assert in_cols == 5 * d_conf + 2 * d and d == 2 * d_conf
    cw = d_conf
    core = lax.axis_index("c")
    chip = 2 * lax.axis_index("x") + lax.axis_index("y")

    tr = _largest_tile(t, 272, ROW_CHUNK)
    tm = t
    tn_in = _largest_tile(in_cols // N_CHIPS, 768, LANES)
    tn_d = _largest_tile(d, 1024, LANES)
    tn_h = _largest_tile(d, 512, LANES)
    tn_ff = _largest_tile(d_ff // N_CHIPS, 1024, LANES)
    tn_pw = _largest_tile(d // N_CHIPS, 512, LANES)

    big = [w_in[0], conf_w_pw[0], short_w_out[0], w_o[0], w_up[0], w_down[0]]
    chip_arr = chip.astype(jnp.int32).reshape(1)

    def cast(a, deps):
        return _into_slot(f"cast_w{a}", big[a], N_CHIPS, chip_arr, MXU_DTYPE, _largest_tile(big[a].shape[0], 256, 16), deps)

    small = [_into_slot(f"place_w{a}", w, N_CHIPS, chip_arr, F32, w.shape[0])
             for a, w in enumerate([meta, conf_dw_w[0], short_dw_w[0]])]
    sems_small, fly_small, tok_small = _gather_start("gather_start_small", small, [False] * 3, [[0, 1, 2]])
    sems_in, fly_in, tok_in = _gather_start("gather_start_in", [cast(0, [tok_small])], [True], [[0]])
    rest_groups = [[0, 1], [2], [3], [4]]
    sems_rest, fly_rest, tok_rest = _gather_start(
        "gather_start_rest", [cast(a, [tok_in]) for a in range(1, 6)], [True] * 5, rest_groups)

    def arrive(g, after):
        members = rest_groups[g]
        got = _gather_wait(f"gather_wait_rest{g}", [fly_rest[a] for a in members], [True] * len(members),
                           sems_rest[g], after)
        return _pass_halves(f"gather_pass_rest{g}", got)

    meta_g, wdw_g, w3_g = _gather_wait("gather_wait_small", fly_small, [False] * 3, sems_small[0], [tok_in])
    meta_full = jnp.transpose(meta_g, (1, 0, 2)).reshape(N_META, d)
    wdw = jnp.transpose(wdw_g, (1, 0, 2)).reshape(CONF_KERNEL, d_conf)
    w3 = jnp.transpose(w3_g, (1, 0, 2)).reshape(SHORT_KERNEL, d_conf)

    tail = jnp.zeros((t - t_real, d), F32)
    h0 = jnp.concatenate([meta_full, x[0], tail], axis=0)
    target = jnp.concatenate([jnp.zeros((N_META, d), F32), loss_target[0], tail], axis=0)

    def norm_in(i, rows, vecs):
        return [_rms_fwd(rows[0], vecs[0])], []

    (n_lp,) = _rowwise("norm_in", norm_in, [(h0, d, 0)], [g_pre_mix], [(d, MXU_DTYPE)], [], tr, deps=[tok_rest])
    (wg_in,) = _pass_halves("gather_pass_in", _gather_wait("gather_wait_in", fly_in, [True], sems_in[0], [n_lp]))
    proj =_matmul("proj", n_lp, wg_in, kind="nn", tm=tm, tn=tn_in, tk=d, out_dtypes=[F32])
    ac, c3 = _conv_fwd(proj, wdw, conf_dw_b, w3, d_conf)

    def ln_parts(ac_t, ln_g, ln_b):
        mu = jnp.mean(ac_t, axis=-1, keepdims=True)
        xc = ac_t - mu
        rstd = lax.rsqrt(jnp.mean(xc * xc, axis=-1, keepdims=True) + LN_EPS)
        xh = xc * rstd
        return xh, rstd, xh * ln_g + ln_b

    def branch_act(i, rows, vecs):
        ac_t, c3_t, bg_t = rows
        _, _, al = ln_parts(ac_t, vecs[0], vecs[1])
        return [al * _sigmoid(al), bg_t * c3_t], []

    a_act, s_lp = _rowwise("branch_act", branch_act, [(ac, cw, 0), (c3, cw, 0), (proj, cw, 2)],
                           [conf_ln_g, conf_ln_b], [(d_conf, MXU_DTYPE), (d_conf, MXU_DTYPE)], [], tr)
    wg_pw, wg_sout = arrive(0, [a_act])
    y_a = _matmul("y_a", a_act, wg_pw, kind="nn", tm=tm, tn=tn_pw, tk=d_conf, out_dtypes=[F32])
    y_b = _matmul("y_b", s_lp, wg_sout, kind="nn", tm=tm, tn=tn_pw, tk=d_conf, out_dtypes=[F32])

    gate_rows = [(proj, cw, 5), (proj, cw, 6), (proj, cw, 7), (proj, cw, 8)]

    def gates_of(rows, b):
        ga = _sigmoid(jnp.concatenate([rows[0], rows[1]], axis=1) + b[:, :d])
        gb = _sigmoid(jnp.concatenate([rows[2], rows[3]], axis=1) + b[:, d:])
        return ga, gb

    def gate(i, rows, vecs):
        ga, gb = gates_of(rows[2:], vecs[0])
        return [ga * rows[0] + gb * rows[1]], []

    (m_lp,) = _rowwise("gate", gate, [(y_a, d, 0), (y_b, d, 0)] + gate_rows, [b_gates], [(d, MXU_DTYPE)], [], tr)
    wg_o = arrive(1, [m_lp])[0].reshape(d, d)
    mix = _matmul("mix", m_lp, wg_o, kind="nn", tm=tm, tn=tn_d, tk=d, out_dtypes=[F32])

    def post_mix(i, rows, vecs):
        h1_t = rows[0] + _rms_fwd(rows[1], vecs[0])
        return [h1_t, _rms_fwd(h1_t, vecs[1])], []

    h1, n2_lp = _rowwise("post_mix", post_mix, [(h0, d, 0), (mix, d, 0)], [g_post_mix, g_pre_mlp],
                         [(d, F32), (d, MXU_DTYPE)], [], tr)
    (wg_up,) = arrive(2, [n2_lp])
    up, f_lp = _matmul("up", n2_lp, wg_up, kind="nn", tm=tm, tn=tn_h, tk=d, out_dtypes=[F32, MXU_DTYPE],
                       epilogue=lambda acc: (acc, jnp.square(jnp.maximum(acc, 0.0))))
    wg_down = arrive(3, [f_lp])[0].reshape(d_ff, d)
    dn = _matmul("down", f_lp, wg_down, kind="nn", tm=tm, tn=tn_h, tk=_largest_tile(d_ff, 2048, LANES),
                 out_dtypes=[F32])

    def head(i, rows, vecs):
        h1_t, dn_t, tgt = rows
        y = h1_t + _rms_fwd(dn_t, vecs[0])
        row = i + lax.broadcasted_iota(jnp.int32, (ROW_CHUNK, 1), 0)
        err = jnp.where(jnp.logical_and(row >= N_META, row < t_real), y - tgt, 0.0)
        dy = err / d
        d_dn, dg = _rms_bwd(dn_t, vecs[0], dy)
        loss_rows = 0.5 * jnp.mean(err * err, axis=-1, keepdims=True)
        return [dy, d_dn], [dg, jnp.broadcast_to(loss_rows, (ROW_CHUNK, LANES))]

    dy, d_dn, dg_post_mlp, loss_vec = _rowwise(
        "head", head, [(h1, d, 0), (dn, d, 0), (target, d, 0)], [g_post_mlp], [(d, F32), (d, MXU_DTYPE)], [d, LANES], tr)
    loss = lax.psum(loss_vec[0, 0], ("x", "y", "c"))

    core_arr = core.astype(jnp.int32).reshape(1)
    place = jnp.stack([chip, core]).astype(jnp.int32)
    in_flight = {}


    def swap_start(a, gw):
        halves = gw.reshape(N_CHIPS, 2, gw.shape[1] // 2, gw.shape[2])
        *in_flight[a], token = _swap_start(f"swap_start{a}", halves)
        return token

    def exchange_start(a, after):
        sems, halves, land = in_flight[a]
        halves, land = _swap_wait(f"swap_wait{a}", halves, land, sems, after)
        pair = _pair_sum(f"pair_sum{a}", halves, land, core_arr, _largest_tile(land.shape[1], 256, 16))
        *in_flight[a], token = _exchange_start(f"exchange_start{a}", pair)
        return token

    def reduce_finish(a, after):
        sems, pair, land = in_flight[a]
        pair, land = _exchange_wait(f"exchange_wait{a}", pair, land, sems, after)
        return _sum_pieces(f"sum_pieces{a}", pair, land, place, _largest_tile(land.shape[1], 256, 16))

    d_up = _matmul("d_up", d_dn, wg_down, kind="nt", tm=tm, tn=tn_h, tk=d, out_dtypes=[MXU_DTYPE], extras=[up],
                   epilogue=lambda acc, up_t: (acc * (2.0 * jnp.maximum(up_t, 0.0)),))
    tk_t = t
    gw_down = _matmul("gw_down", f_lp, d_dn, kind="tn", tm=_largest_tile(d_ff, 2048, LANES), tn=tn_d, tk=tk_t,
                      out_dtypes=[F32])
    tok = swap_start(5, gw_down.reshape(N_CHIPS, d_ff // N_CHIPS, d))
    d_n2 = _matmul("d_n2", d_up, wg_up, kind="nt", tm=tm, tn=tn_h, tk=d_ff // N_CHIPS, out_dtypes=[F32], deps=[tok])
    tok = exchange_start(5, [d_n2])
    gw_up = _matmul("gw_up", n2_lp, d_up, kind="tn", tm=_largest_tile(d, 2048, LANES), tn=tn_ff, tk=tk_t,
                    out_dtypes=[F32], out_pieces=N_CHIPS, deps=[tok])
    tok = swap_start(4, gw_up)

    def bwd_mid(i, rows, vecs):
        dy_t, dn2_t, h1_t, mix_t = rows
        d_h1a, dg_pre_mlp = _rms_bwd(h1_t, vecs[1], dn2_t)
        d_h1 = dy_t + d_h1a
        d_mix, dg_post_mix = _rms_bwd(mix_t, vecs[0], d_h1)
        return [d_h1, d_mix], [dg_pre_mlp, dg_post_mix]

    d_h1, d_mix, dg_pre_mlp, dg_post_mix = _rowwise(
        "bwd_mid", bwd_mid, [(dy, d, 0), (d_n2, d, 0), (h1, d, 0), (mix, d, 0)], [g_post_mix, g_pre_mlp],
        [(d, F32), (d, MXU_DTYPE)], [d, d], tr, deps=[tok])
    d_m = _matmul("d_m", d_mix, wg_o, kind="nt", tm=tm, tn=tn_d, tk=d, out_dtypes=[F32])
    tok = exchange_start(4, [d_m])
    gw_o = _matmul("gw_o", m_lp, d_mix, kind="tn", tm=_largest_tile(d, 2048, LANES), tn=tn_d, tk=tk_t, out_dtypes=[F32],
                   deps=[tok])
    tok = swap_start(3, gw_o.reshape(N_CHIPS, d // N_CHIPS, d))

    def gate_bwd(i, rows, vecs):
        dm_t, ya_t, yb_t = rows[:3]
        ga, gb = gates_of(rows[3:], vecs[0])
        d_gpre = jnp.concatenate([dm_t * ya_t * ga * (1.0 - ga), dm_t * yb_t * gb * (1.0 - gb)], axis=1)
        return [dm_t * ga, dm_t * gb, d_gpre], [d_gpre]

    d_ya, d_yb, d_gpre, dg_b_gates = _rowwise(
        "gate_bwd", gate_bwd, [(d_m, d, 0), (y_a, d, 0), (y_b, d, 0)] + gate_rows, [b_gates],
        [(d, MXU_DTYPE), (d, MXU_DTYPE), (2 * d, MXU_DTYPE)], [2 * d], tr, deps=[tok])
    tok = exchange_start(3, [d_ya])
    d_aact = _matmul("d_aact", d_ya, wg_pw, kind="nt", tm=tm, tn=d_conf, tk=tn_pw, out_dtypes=[F32], deps=[tok])
    gw_pw = _matmul("gw_pw", a_act, d_ya, kind="tn", tm=_largest_tile(d_conf, 2048, LANES), tn=tn_pw, tk=tk_t,
                    out_dtypes=[F32], out_pieces=N_CHIPS)
    tok = swap_start(1, gw_pw)
    d_s = _matmul("d_s", d_yb, wg_sout, kind="nt", tm=tm, tn=d_conf, tk=tn_pw, out_dtypes=[F32], deps=[tok])
    tok = exchange_start(1, [d_s])
    gw_sout = _matmul("gw_sout", s_lp, d_yb, kind="tn", tm=_largest_tile(d_conf, 2048, LANES), tn=tn_pw, tk=tk_t,
                      out_dtypes=[F32], out_pieces=N_CHIPS, deps=[tok])
    tok = swap_start(2, gw_sout)

    def branch_bwd(i, rows, vecs):
        daact_t, ds_t, ac_t, c3_t, bg_t = rows
        xh, rstd, al = ln_parts(ac_t, vecs[0], vecs[1])
        sg = _sigmoid(al)
        d_al = daact_t * (sg * (1.0 + al * (1.0 - sg)))
        dxh = d_al * vecs[0]
        d_ac = rstd * (dxh - jnp.mean(dxh, axis=-1, keepdims=True) - xh * jnp.mean(dxh * xh, axis=-1, keepdims=True))
        return [d_ac, ds_t * bg_t, ds_t * c3_t], [d_al * xh, d_al, d_ac]

    d_ac, d_c3, d_bg, dg_ln_g, dg_ln_b, dg_dw_b = _rowwise(
        "branch_bwd", branch_bwd, [(d_aact, cw, 0), (d_s, cw, 0), (ac, cw, 0), (c3, cw, 0), (proj, cw, 2)],
        [conf_ln_g, conf_ln_b], [(d_conf, F32), (d_conf, F32), (d_conf, MXU_DTYPE)], [d_conf] * 3, tr, deps=[tok])
    tok = exchange_start(2, [d_ac])
    d_av, d_ag, d_cg, d_v, dg_wdw, dg_w3 = _conv_bwd(proj, d_ac, d_c3, wdw, w3, d_conf)
    d_proj = jnp.concatenate([d_av, d_ag, d_bg, d_cg, d_v, d_gpre], axis=1)
    d_n = _matmul("d_n", d_proj, wg_in, kind="nt", tm=tm, tn=tn_h, tk=in_cols // N_CHIPS, out_dtypes=[F32], deps=[tok])

    def bwd_in(i, rows, vecs):
        d_h0a, dg = _rms_bwd(rows[2], vecs[0], rows[1])
        return [rows[0] + d_h0a], [dg]

    d_h0, dg_pre_mix = _rowwise("bwd_in", bwd_in, [(d_h1, d, 0), (d_n, d, 0), (h0, d, 0)], [g_pre_mix],
                                [(d, F32)], [d], tr)
    grad_x = d_h0[N_META:t_real][None]

    small_w = d_conf
    rep = [dg_pre_mix, dg_b_gates, dg_dw_b, dg_ln_g, dg_ln_b, dg_post_mix, dg_pre_mlp, dg_post_mlp]
    rep_w = [g_pre_mix, b_gates, conf_dw_b, conf_ln_g, conf_ln_b, g_post_mix, g_pre_mlp, g_post_mlp]
    rep_m = [m_g_pre_mix, m_b_gates, m_conf_dw_b, m_conf_ln_g, m_conf_ln_b, m_g_post_mix, m_g_pre_mlp, m_g_post_mlp]
    rep_v = [v_g_pre_mix, v_b_gates, v_conf_dw_b, v_conf_ln_g, v_conf_ln_b, v_g_post_mix, v_g_pre_mlp, v_g_post_mlp]
    col = [dg_wdw, dg_w3, d_h0[:N_META]]
    assert all(a.size % small_w == 0 for a in rep + col)

    def pack(arrs):
        flat = jnp.concatenate([a.reshape(-1, small_w) for a in arrs], axis=0)
        return jnp.pad(flat, ((0, -flat.shape[0] % 8), (0, 0)))

    def unpack(buf, like):
        out, r0 = [], 0
        for a in like:
            nr = a.size // small_w
            out.append(buf[r0:r0 + nr].reshape(a.shape))
            r0 += nr
        return out

    packed = pack(rep + col)
    total = _sum_blocks("sum_small", _allgather_small(packed), 8)
    small_g = unpack(total, rep + col)
    g_rep = small_g[:len(rep)]
    g_wdw_full, g_w3_full, g_meta_full = small_g[len(rep):]
    sc = d_conf // N_CHIPS
    g_wdw = lax.dynamic_slice_in_dim(g_wdw_full, chip * sc, sc, axis=1)
    g_w3 = lax.dynamic_slice_in_dim(g_w3_full, chip * sc, sc, axis=1)
    g_meta = lax.dynamic_slice_in_dim(g_meta_full, chip * (d // N_CHIPS), d // N_CHIPS, axis=1)

    gw_in = _matmul("gw_in", n_lp, d_proj, kind="tn", tm=_largest_tile(d, 2048, LANES), tn=tn_in, tk=tk_t,
                    out_dtypes=[F32], out_pieces=N_CHIPS, deps=[total])
    tok = swap_start(0, gw_in)

    big_m = [m_w_in, m_conf_w_pw, m_short_w_out, m_w_o, m_w_up, m_w_down]
    big_v = [v_w_in, v_conf_w_pw, v_short_w_out, v_w_o, v_w_up, v_w_down]
    big_res = {}

    def reduce_group(tag, members, after):
        reduced = []
        for a in members:
            reduced.append(reduce_finish(a, after))
            after = [reduced[-1]]
        return _join_halves(f"join_halves_{tag}", reduced)

    def adam_group(members, joined, deps):
        for a, j in zip(members, joined):
            big_res[a] = _adamw(f"adamw_big{a}", big[a], j.reshape(big[a].shape), big_m[a][0], big_v[a][0],
                                _largest_tile(big[a].shape[0], 256, 8), deps)

    early = [5, 4, 3, 1, 2]
    joined_early = reduce_group("early", early, [tok])
    tok = exchange_start(0, joined_early)
    adam_group(early, joined_early, [tok])
    rep_pack = [pack(arrs) for arrs in (rep_w, g_rep, rep_m, rep_v)]
    rep_res = [unpack(buf, rep_w) for buf in _adamw("adamw_rep", *rep_pack, rep_pack[0].shape[0])]
    col_res = [_adamw(f"adamw_col{a}", w, g, m, v, w.shape[0]) for a, (w, g, m, v) in enumerate(
        [(meta, g_meta, m_meta, v_meta), (conf_dw_w[0], g_wdw, m_conf_dw_w[0], v_conf_dw_w[0]),
         (short_dw_w[0], g_w3, m_short_dw_w[0], v_short_dw_w[0])])]
    joined_late = reduce_group("late", [0], [col_res[0][1], rep_res[1][0]] + [big_res[a][1] for a in early])
    adam_group([0], joined_late, [])

    def leaf(q):
        r = lambda a: rep_res[q][a]
        b = lambda a: big_res[a][q][None]
        return [col_res[0][q], r(0), b(0), r(1), col_res[1][q][None], r(2), r(3), r(4), b(1), col_res[2][q][None], b(2),
                b(3), r(5), r(6), b(4), b(5), r(7)]

    return (loss, grad_x, *leaf(0), *leaf(1), *leaf(2), *leaf(3))
```

```python
import functools

import jax
import jax.numpy as jnp
from jax import lax
from jax.experimental import pallas as pl
from jax.experimental.pallas import tpu as pltpu

F32 = jnp.float32
BF16 = jnp.bfloat16
MXU_DTYPE = BF16
WIRE_DTYPE = BF16

N_META = 16
CONF_KERNEL = 31
SHORT_KERNEL = 3
CONV_PAD = 32
RMS_EPS = 1e-6
LN_EPS = 1e-5
ADAM_LR = 0.001
ADAM_B1 = 0.9
ADAM_B2 = 0.999
ADAM_EPS = 1e-08
ADAM_WD = 0.01
ADAM_STEP = 10

N_CHIPS = 4
MESH = pl.DeviceIdType.MESH
LANES = 128


def _sigmoid(z):
    return 1.0 / (1.0 + jnp.exp(-z))


ANY_SPEC = pl.BlockSpec(memory_space=pl.ANY)


def _matmul(name, a, b, *, kind, tm, tn, tk, out_dtypes, out_pieces=1, epilogue=None, extras=(), deps=()):
    pieces = b.shape[0] if b.ndim == 3 else 1
    if kind == "nn":
        m, kdim = a.shape
        n = b.shape[-1] * pieces
        dims = (((1,), (0,)), ((), ()))
        a_spec = pl.BlockSpec((tm, tk), lambda i, j, k: (i, k))
        if b.ndim == 2:
            b_spec = pl.BlockSpec((tk, tn), lambda i, j, k: (k, j))
        else:
            npp = b.shape[-1] // tn
            b_spec = pl.BlockSpec((None, tk, tn), lambda i, j, k: (j // npp, k, j % npp))
    elif kind == "nt":
        m, kdim = a.shape
        n = b.shape[-2]
        dims = (((1,), (1,)), ((), ()))
        a_spec = pl.BlockSpec((tm, tk), lambda i, j, k: (i, k))
        if b.ndim == 2:
            b_spec = pl.BlockSpec((tn, tk), lambda i, j, k: (j, k))
        else:
            kpp = b.shape[-1] // tk
            b_spec = pl.BlockSpec((None, tn, tk), lambda i, j, k: (k // kpp, j, k % kpp))
    else:
        kdim, m = a.shape
        n = b.shape[-1]
        dims = (((0,), (0,)), ((), ()))
        a_spec = pl.BlockSpec((tk, tm), lambda i, j, k: (k, i))
        b_spec = pl.BlockSpec((tk, tn), lambda i, j, k: (k, j))
    assert m % tm == 0 and n % tn == 0 and kdim % tk == 0, (name, m, n, kdim, tm, tn, tk)
    nk = kdim // tk
    if out_pieces == 1:
        out_shape = (m, n)
        out_spec = pl.BlockSpec((tm, tn), lambda i, j, k: (i, j))
    else:
        onpp = n // out_pieces // tn
        out_shape = (out_pieces, m, n // out_pieces)
        out_spec = pl.BlockSpec((None, tm, tn), lambda i, j, k: (j // onpp, i, j % onpp))
    n_ex, n_out, n_in = len(extras), len(out_dtypes), len(extras) + len(deps)
    if epilogue is None:
        epilogue = lambda acc: (acc,)

    def body(a_ref, b_ref, *rest):
        ex_refs, o_refs = rest[:n_ex], rest[n_in:n_in + n_out]
        prod = lax.dot_general(a_ref[...], b_ref[...], dims, preferred_element_type=F32)

        def finish(acc):
            tiles = epilogue(acc, *[r[...] for r in ex_refs])
            for o_ref, t in zip(o_refs, tiles):
                o_ref[...] = t.astype(o_ref.dtype)

        if nk == 1:
            finish(prod)
        else:
            acc_ref = rest[n_in + n_out]
            k = pl.program_id(2)

            @pl.when(k == 0)
            def _():
                acc_ref[...] = prod

            @pl.when(jnp.logical_and(k > 0, k < nk - 1))
            def _():
                acc_ref[...] += prod

            @pl.when(k == nk - 1)
            def _():
                finish(acc_ref[...] + prod)

    ex_specs = [pl.BlockSpec((tm, tn), lambda i, j, k: (i, j)) for _ in extras]
    res = pl.pallas_call(
        body,
        name=name,
        grid=(m // tm, n // tn, nk),
        in_specs=[a_spec, b_spec, *ex_specs] + [ANY_SPEC] * len(deps),
        out_specs=[out_spec] * n_out,
        out_shape=[jax.ShapeDtypeStruct(out_shape, d) for d in out_dtypes],
        scratch_shapes=[pltpu.VMEM((tm, tn), F32)] if nk > 1 else [],
        compiler_params=pltpu.CompilerParams(dimension_semantics=("parallel", "parallel", "arbitrary")),
    )(a, b, *extras, *deps)
    return res[0] if n_out == 1 else res


ROW_CHUNK = 16
SUBLANES = 8


def _rowwise(name, fn, rows, vecs, outs, sums, tr, deps=()):
    t = rows[0][0].shape[0]
    assert t % tr == 0 and tr % ROW_CHUNK == 0
    n_r, n_v, n_o, n_s = len(rows), len(vecs), len(outs), len(sums)
    n_in = n_r + n_v + len(deps)
    n_steps = t // tr

    def body(*refs):
        r_in, v_in = refs[:n_r], refs[n_r:n_r + n_v]
        o_refs = refs[n_in:n_in + n_o]
        s_refs = refs[n_in + n_o:n_in + n_o + n_s]
        acc_refs = refs[n_in + n_o + n_s:]
        i = pl.program_id(0)

        @pl.when(i == 0)
        def _():
            for acc_ref in acc_refs:
                acc_ref[...] = jnp.zeros(acc_ref.shape, F32)

        def chunk(ci):
            r0 = ci * ROW_CHUNK
            sl = pl.ds(r0, ROW_CHUNK)
            o_tiles, s_tiles = fn(i * tr + r0, [r[sl, :] for r in r_in], [v[...] for v in v_in])
            for o_ref, tile in zip(o_refs, o_tiles):
                o_ref[sl, :] = tile.astype(o_ref.dtype)
            for acc_ref, tile in zip(acc_refs, s_tiles):
                part = tile[0:SUBLANES]
                for s in range(1, ROW_CHUNK // SUBLANES):
                    part = part + tile[s * SUBLANES:(s + 1) * SUBLANES]
                acc_ref[...] += part

        for ci in range(tr // ROW_CHUNK):
            chunk(ci)

        @pl.when(i == n_steps - 1)
        def _():
            for s_ref, acc_ref in zip(s_refs, acc_refs):
                s_ref[...] = jnp.sum(acc_ref[...], axis=0, keepdims=True)

    def row_spec(width, blk):
        return pl.BlockSpec((tr, width), lambda i: (i, blk))

    res = pl.pallas_call(
        body,
        name=name,
        grid=(t // tr,),
        in_specs=[row_spec(w, blk) for _, w, blk in rows]
        + [pl.BlockSpec(v.shape, lambda i: (0, 0)) for v in vecs] + [ANY_SPEC] * len(deps),
        out_specs=[pl.BlockSpec((tr, c), lambda i: (i, 0)) for c, _ in outs]
        + [pl.BlockSpec((1, c), lambda i: (0, 0)) for c in sums],
        out_shape=[jax.ShapeDtypeStruct((t, c), d) for c, d in outs]
        + [jax.ShapeDtypeStruct((1, c), F32) for c in sums],
        scratch_shapes=[pltpu.VMEM((SUBLANES, c), F32) for c in sums],
        compiler_params=pltpu.CompilerParams(dimension_semantics=("arbitrary",)),
    )(*[r[0] for r in rows], *vecs, *deps)
    return res


def _rms_fwd(x, g):
    r = lax.rsqrt(jnp.mean(x * x, axis=-1, keepdims=True) + RMS_EPS)
    return x * r * g


def _rms_bwd(x, g, dy):
    r = lax.rsqrt(jnp.mean(x * x, axis=-1, keepdims=True) + RMS_EPS)
    xn = x * r
    dxn = dy * g
    dx = r * (dxn - xn * jnp.mean(dxn * xn, axis=-1, keepdims=True))
    return dx, dy * xn


CONV_ROWS = 64
CONV_LANES = 128


def _conv_fwd(proj, wdw, bdw, w3, d_conf):
    t = proj.shape[0]
    cl = CONV_LANES
    nb = d_conf // cl
    nchunk = t // CONV_ROWS
    assert t % CONV_ROWS == 0

    def body(av_ref, ag_ref, cg_ref, v_ref, wdw_ref, bdw_ref, w3_ref, ac_ref, c3_ref, apad, cpad):
        zeros = jnp.zeros((CONV_PAD, cl), F32)
        apad[0:CONV_PAD, :] = zeros
        cpad[0:CONV_PAD, :] = zeros
        apad[CONV_PAD:, :] = av_ref[...] * _sigmoid(ag_ref[...])
        cpad[CONV_PAD:, :] = cg_ref[...] * v_ref[...]

        def chunk(ci, carry):
            base = pl.multiple_of(ci * CONV_ROWS, 8)
            acc = jnp.zeros((CONV_ROWS, cl), F32) + bdw_ref[...]
            for k in range(CONF_KERNEL):
                off = CONV_PAD - (CONF_KERNEL - 1) + k
                acc = acc + apad[pl.ds(base + off, CONV_ROWS), :] * wdw_ref[k:k + 1, :]
            ac_ref[pl.ds(base, CONV_ROWS), :] = acc
            acc3 = jnp.zeros((CONV_ROWS, cl), F32)
            for k in range(SHORT_KERNEL):
                off = CONV_PAD - (SHORT_KERNEL - 1) + k
                acc3 = acc3 + cpad[pl.ds(base + off, CONV_ROWS), :] * w3_ref[k:k + 1, :]
            c3_ref[pl.ds(base, CONV_ROWS), :] = acc3
            return carry

        lax.fori_loop(0, nchunk, chunk, 0)

    def col(blk0):
        return pl.BlockSpec((t, cl), lambda j: (0, blk0 + j))

    return pl.pallas_call(
        body,
        name="conv_fwd",
        grid=(nb,),
        in_specs=[col(0), col(nb), col(3 * nb), col(4 * nb),
                  pl.BlockSpec((CONF_KERNEL, cl), lambda j: (0, j)),
                  pl.BlockSpec((1, cl), lambda j: (0, j)),
                  pl.BlockSpec((SHORT_KERNEL, cl), lambda j: (0, j))],
        out_specs=[pl.BlockSpec((t, cl), lambda j: (0, j))] * 2,
        out_shape=[jax.ShapeDtypeStruct((t, d_conf), F32)] * 2,
        scratch_shapes=[pltpu.VMEM((t + CONV_PAD, cl), F32)] * 2,
        compiler_params=pltpu.CompilerParams(dimension_semantics=("parallel",)),
    )(proj, proj, proj, proj, wdw, bdw, w3)


def _conv_bwd(proj, d_ac, d_c3, wdw, w3, d_conf):
    t = proj.shape[0]
    cl = CONV_LANES
    nb = d_conf // cl
    nchunk = t // CONV_ROWS
    nsub = CONV_ROWS // 8

    def fold(p):
        r = p[0:8]
        for s in range(1, nsub):
            r = r + p[8 * s:8 * s + 8]
        return r

    def body(av_ref, ag_ref, cg_ref, v_ref, dac_ref, dc3_ref, wdw_ref, w3_ref,
             dav_ref, dag_ref, dcg_ref, dv_ref, dwdw_ref, dw3_ref, apad, cpad, dapad, dcpad):
        zeros = jnp.zeros((CONV_PAD, cl), F32)
        apad[0:CONV_PAD, :] = zeros
        cpad[0:CONV_PAD, :] = zeros
        apad[CONV_PAD:, :] = av_ref[...] * _sigmoid(ag_ref[...])
        cpad[CONV_PAD:, :] = cg_ref[...] * v_ref[...]
        dapad[0:t, :] = dac_ref[...]
        dcpad[0:t, :] = dc3_ref[...]
        dapad[t:, :] = zeros
        dcpad[t:, :] = zeros

        def chunk(ci, accs):
            base = pl.multiple_of(ci * CONV_ROWS, 8)
            rows = pl.ds(base, CONV_ROWS)
            da = jnp.zeros((CONV_ROWS, cl), F32)
            for k in range(CONF_KERNEL):
                da = da + dapad[pl.ds(base + (CONF_KERNEL - 1 - k), CONV_ROWS), :] * wdw_ref[k:k + 1, :]
            dcv = jnp.zeros((CONV_ROWS, cl), F32)
            for k in range(SHORT_KERNEL):
                dcv = dcv + dcpad[pl.ds(base + (SHORT_KERNEL - 1 - k), CONV_ROWS), :] * w3_ref[k:k + 1, :]
            av, sg = av_ref[rows, :], _sigmoid(ag_ref[rows, :])
            dav_ref[rows, :] = (da * sg).astype(dav_ref.dtype)
            dag_ref[rows, :] = (da * av * sg * (1.0 - sg)).astype(dag_ref.dtype)
            dcg_ref[rows, :] = (dcv * v_ref[rows, :]).astype(dcg_ref.dtype)
            dv_ref[rows, :] = (dcv * cg_ref[rows, :]).astype(dv_ref.dtype)
            d_out, d_out3 = dac_ref[rows, :], dc3_ref[rows, :]
            new = []
            for k in range(CONF_KERNEL):
                off = CONV_PAD - (CONF_KERNEL - 1) + k
                new.append(accs[k] + fold(d_out * apad[pl.ds(base + off, CONV_ROWS), :]))
            for k in range(SHORT_KERNEL):
                off = CONV_PAD - (SHORT_KERNEL - 1) + k
                new.append(accs[CONF_KERNEL + k] + fold(d_out3 * cpad[pl.ds(base + off, CONV_ROWS), :]))
            return tuple(new)

        init = tuple(jnp.zeros((8, cl), F32) for _ in range(CONF_KERNEL + SHORT_KERNEL))
        accs = lax.fori_loop(0, nchunk, chunk, init)
        for k in range(CONF_KERNEL):
            dwdw_ref[k:k + 1, :] = jnp.sum(accs[k], axis=0, keepdims=True)
        for k in range(SHORT_KERNEL):
            dw3_ref[k:k + 1, :] = jnp.sum(accs[CONF_KERNEL + k], axis=0, keepdims=True)

    def col(blk0):
        return pl.BlockSpec((t, cl), lambda j: (0, blk0 + j))

    own = pl.BlockSpec((t, cl), lambda j: (0, j))
    return pl.pallas_call(
        body,
        name="conv_bwd",
        grid=(nb,),
        in_specs=[col(0), col(nb), col(3 * nb), col(4 * nb), own, own,
                  pl.BlockSpec((CONF_KERNEL, cl), lambda j: (0, j)),
                  pl.BlockSpec((SHORT_KERNEL, cl), lambda j: (0, j))],
        out_specs=[own] * 4 + [pl.BlockSpec((CONF_KERNEL, cl), lambda j: (0, j)),
                               pl.BlockSpec((SHORT_KERNEL, cl), lambda j: (0, j))],
        out_shape=[jax.ShapeDtypeStruct((t, d_conf), MXU_DTYPE)] * 4
        + [jax.ShapeDtypeStruct((CONF_KERNEL, d_conf), F32), jax.ShapeDtypeStruct((SHORT_KERNEL, d_conf), F32)],
        scratch_shapes=[pltpu.VMEM((t + CONV_PAD, cl), F32)] * 4,
        compiler_params=pltpu.CompilerParams(dimension_semantics=("parallel",)),
    )(proj, proj, proj, proj, d_ac, d_c3, wdw, w3)


def _elementwise(name, fn, ins, out_dtypes, tr, deps=()):
    ins = [(a, ()) if not isinstance(a, tuple) else a for a in ins]
    r, c = ins[0][0].shape[-2:]
    assert r % tr == 0, (name, r, tr)
    n_in = len(ins)

    def body(*refs):
        tiles = fn(*[x[...] for x in refs[:n_in]])
        for o_ref, tile in zip(refs[n_in + len(deps):], tiles):
            o_ref[...] = tile.astype(o_ref.dtype)

    def spec(lead):
        return pl.BlockSpec((None,) * len(lead) + (tr, c), lambda i: (*lead, i, 0))

    res = pl.pallas_call(
        body,
        name=name,
        grid=(r // tr,),
        in_specs=[spec(lead) for _, lead in ins] + [ANY_SPEC] * len(deps),
        out_specs=[pl.BlockSpec((tr, c), lambda i: (i, 0))] * len(out_dtypes),
        out_shape=[jax.ShapeDtypeStruct((r, c), d) for d in out_dtypes],
        compiler_params=pltpu.CompilerParams(dimension_semantics=("parallel",)),
    )(*[a for a, _ in ins], *deps)
    return res


def _adamw_tiles(w, g, m, v):
    m = ADAM_B1 * m + (1.0 - ADAM_B1) * g
    v = ADAM_B2 * v + (1.0 - ADAM_B2) * jnp.square(g)
    m_hat = m / (1.0 - ADAM_B1 ** ADAM_STEP)
    v_hat = v / (1.0 - ADAM_B2 ** ADAM_STEP)
    delta = -ADAM_LR * (m_hat / (jnp.sqrt(v_hat) + ADAM_EPS) + ADAM_WD * w)
    return g, delta, m, v


def _adamw(name, w, g, m, v, tr, deps=()):
    shape = w.shape
    flat = [a.reshape(shape[-2:]) if a.ndim > 2 else a for a in (w, g, m, v)]
    res = _elementwise(name, _adamw_tiles, flat, [F32] * 4, tr, deps)
    return [a.reshape(shape) for a in res]


def _pair_sum(name, p, q, core, tr):
    n_p, _, hr, c = p.shape
    assert hr % tr == 0

    def body(core_ref, p_ref, q_ref, o_ref):
        o_ref[...] = (p_ref[...] + q_ref[...]).astype(o_ref.dtype)

    return pl.pallas_call(
        body,
        name=name,
        grid_spec=pltpu.PrefetchScalarGridSpec(
            num_scalar_prefetch=1,
            grid=(n_p, hr // tr),
            in_specs=[pl.BlockSpec((None, None, tr, c), lambda a, i, core_ref: (a, core_ref[0], i, 0)),
                      pl.BlockSpec((None, tr, c), lambda a, i, core_ref: (a, i, 0))],
            out_specs=pl.BlockSpec((None, tr, c), lambda a, i, core_ref: (a, i, 0)),
        ),
        out_shape=jax.ShapeDtypeStruct((n_p, hr, c), WIRE_DTYPE),
        compiler_params=pltpu.CompilerParams(dimension_semantics=("parallel", "parallel")),
    )(core, p, q)


def _into_slot(name, w, slots, slot, dtype, tr, deps=()):
    r, c = w.shape
    assert r % tr == 0

    def body(slot_ref, w_ref, *rest):
        o_ref = rest[len(deps)]
        o_ref[...] = w_ref[...].astype(o_ref.dtype)

    return pl.pallas_call(
        body,
        name=name,
        grid_spec=pltpu.PrefetchScalarGridSpec(
            num_scalar_prefetch=1,
            grid=(r // tr,),
            in_specs=[pl.BlockSpec((tr, c), lambda i, slot_ref: (i, 0))] + [ANY_SPEC] * len(deps),
            out_specs=pl.BlockSpec((None, tr, c), lambda i, slot_ref: (slot_ref[0], i, 0)),
        ),
        out_shape=jax.ShapeDtypeStruct((slots, r, c), dtype),
        compiler_params=pltpu.CompilerParams(dimension_semantics=("parallel",)),
    )(slot, w, *deps)


def _sum_pieces(name, own, rb, place, tr):
    n_p, hr, c = rb.shape
    assert hr % tr == 0

    def body(place_ref, own_ref, *refs):
        chip = place_ref[0]
        acc = None
        for k in range(n_p):
            tile = jnp.where(chip == k, own_ref[...], refs[k][...]).astype(F32)
            acc = tile if acc is None else acc + tile
        refs[n_p][...] = acc

    def landed(k):
        return pl.BlockSpec((None, tr, c), lambda i, place_ref: (jnp.where(place_ref[0] == k, (k + 1) % n_p, k), i, 0))

    return pl.pallas_call(
        body,
        name=name,
        grid_spec=pltpu.PrefetchScalarGridSpec(
            num_scalar_prefetch=1,
            grid=(hr // tr,),
            in_specs=[pl.BlockSpec((None, tr, c), lambda i, place_ref: (place_ref[0], i, 0))]
            + [landed(k) for k in range(n_p)],
            out_specs=pl.BlockSpec((None, tr, c), lambda i, place_ref: (place_ref[1], i, 0)),
        ),
        out_shape=jax.ShapeDtypeStruct((2, hr, c), F32),
        compiler_params=pltpu.CompilerParams(dimension_semantics=("parallel",)),
    )(place, own, *([rb] * n_p))


HBM_SPEC = pl.BlockSpec(memory_space=pl.ANY)


def _place():
    x, y, c = lax.axis_index("x"), lax.axis_index("y"), lax.axis_index("c")
    chips = [(1 - x, y), (x, 1 - y), (1 - x, 1 - y)]
    return x, y, c, chips


def _gather_shards(bufs, split):
    n = len(bufs)
    n_split = sum(split)
    fwd_slot = {a: s for s, a in enumerate([a for a in range(n) if split[a]])}

    def body(*refs):
        outs = refs[n:2 * n]
        send_sems, recv_sems, fsend_sems, frecv_sems = refs[2 * n:]
        x, y, c, chips = _place()
        me = 2 * x + y
        sibling = (x, y, 1 - c)

        def part(a, slot, h):
            if not split[a]:
                return outs[a].at[slot]
            hr = bufs[a].shape[1] // 2
            return outs[a].at[slot, pl.ds(h * hr, hr), :]

        sends = []
        for a in range(n):
            for j, chip in enumerate(chips):
                sends.append(pltpu.make_async_remote_copy(
                    src_ref=part(a, me, c), dst_ref=part(a, me, c),
                    send_sem=send_sems.at[3 * a + j], recv_sem=recv_sems.at[3 * a + j],
                    device_id=(*chip, c), device_id_type=MESH))
        for cp in sends:
            cp.start()
        passed = []
        for a in range(n):
            for j, chip in enumerate(chips):
                landed = part(a, 2 * chip[0] + chip[1], c)
                pltpu.make_async_remote_copy(
                    src_ref=landed, dst_ref=landed, send_sem=send_sems.at[3 * a + j], recv_sem=recv_sems.at[3 * a + j],
                    device_id=(*chip, c), device_id_type=MESH).wait_recv()
                if split[a]:
                    s = 3 * fwd_slot[a] + j
                    fwd = pltpu.make_async_remote_copy(
                        src_ref=landed, dst_ref=landed, send_sem=fsend_sems.at[s], recv_sem=frecv_sems.at[s],
                        device_id=sibling, device_id_type=MESH)
                    fwd.start()
                    passed.append(fwd)
        for a in range(n):
            if split[a]:
                for j, chip in enumerate(chips):
                    s = 3 * fwd_slot[a] + j
                    other = part(a, 2 * chip[0] + chip[1], 1 - c)
                    pltpu.make_async_remote_copy(
                        src_ref=other, dst_ref=other, send_sem=fsend_sems.at[s], recv_sem=frecv_sems.at[s],
                        device_id=sibling, device_id_type=MESH).wait_recv()
        for cp in sends + passed:
            cp.wait_send()

    return pl.pallas_call(
        body,
        name="gather_shards",
        in_specs=[HBM_SPEC] * n,
        out_specs=[HBM_SPEC] * n,
        out_shape=[jax.ShapeDtypeStruct(b.shape, b.dtype) for b in bufs],
        input_output_aliases={a: a for a in range(n)},
        scratch_shapes=[pltpu.SemaphoreType.DMA((3 * n,)), pltpu.SemaphoreType.DMA((3 * n,)),
                        pltpu.SemaphoreType.DMA((3 * n_split,)), pltpu.SemaphoreType.DMA((3 * n_split,))],
    )(*bufs)


def _swap_halves(name, parts):
    n = len(parts)
    n_p = parts[0].shape[0]

    def body(*refs):
        ins, outs = refs[:n], refs[n:2 * n]
        send_sems, recv_sems = refs[2 * n:]
        x, y, c, _ = _place()
        copies = []
        for a in range(n):
            for p in range(n_p):
                copies.append(pltpu.make_async_remote_copy(
                    src_ref=ins[a].at[p, 1 - c], dst_ref=outs[a].at[p],
                    send_sem=send_sems.at[n_p * a + p], recv_sem=recv_sems.at[n_p * a + p],
                    device_id=(x, y, 1 - c), device_id_type=MESH))
        for cp in copies:
            cp.start()
        for cp in copies:
            cp.wait()

    return pl.pallas_call(
        body,
        name=name,
        in_specs=[HBM_SPEC] * n,
        out_specs=[HBM_SPEC] * n,
        out_shape=[jax.ShapeDtypeStruct((n_p, *p.shape[2:]), p.dtype) for p in parts],
        scratch_shapes=[pltpu.SemaphoreType.DMA((n_p * n,)), pltpu.SemaphoreType.DMA((n_p * n,))],
    )(*parts)


def _exchange_pieces(sums):
    n = len(sums)

    def body(*refs):
        ins, outs = refs[:n], refs[n:2 * n]
        send_sems, recv_sems = refs[2 * n:]
        x, y, c, chips = _place()
        me = 2 * x + y
        sends = []
        for a in range(n):
            for j, chip in enumerate(chips):
                sends.append(pltpu.make_async_remote_copy(
                    src_ref=ins[a].at[2 * chip[0] + chip[1]], dst_ref=outs[a].at[me],
                    send_sem=send_sems.at[3 * a + j], recv_sem=recv_sems.at[3 * a + j],
                    device_id=(*chip, c), device_id_type=MESH))
        for cp in sends:
            cp.start()
        for a in range(n):
            for j, chip in enumerate(chips):
                slot = outs[a].at[2 * chip[0] + chip[1]]
                pltpu.make_async_remote_copy(
                    src_ref=slot, dst_ref=slot, send_sem=send_sems.at[3 * a + j], recv_sem=recv_sems.at[3 * a + j],
                    device_id=(*chip, c), device_id_type=MESH).wait_recv()
        for cp in sends:
            cp.wait_send()

    return pl.pallas_call(
        body,
        name="exchange_pieces",
        in_specs=[HBM_SPEC] * n,
        out_specs=[HBM_SPEC] * n,
        out_shape=[jax.ShapeDtypeStruct(s.shape, s.dtype) for s in sums],
        scratch_shapes=[pltpu.SemaphoreType.DMA((3 * n,)), pltpu.SemaphoreType.DMA((3 * n,))],
    )(*sums)


def _join_halves(name, bufs, deps=()):
    n = len(bufs)

    def body(*refs):
        outs = refs[n + len(deps):2 * n + len(deps)]
        send_sems, recv_sems = refs[2 * n + len(deps):]
        x, y, c, _ = _place()
        copies = [pltpu.make_async_remote_copy(
            src_ref=outs[a].at[c], dst_ref=outs[a].at[c], send_sem=send_sems.at[a], recv_sem=recv_sems.at[a],
            device_id=(x, y, 1 - c), device_id_type=MESH) for a in range(n)]
        for cp in copies:
            cp.start()
        for a in range(n):
            other = outs[a].at[1 - c]
            pltpu.make_async_remote_copy(
                src_ref=other, dst_ref=other, send_sem=send_sems.at[a], recv_sem=recv_sems.at[a],
                device_id=(x, y, 1 - c), device_id_type=MESH).wait_recv()
        for cp in copies:
            cp.wait_send()

    return pl.pallas_call(
        body,
        name=name,
        in_specs=[HBM_SPEC] * n + [ANY_SPEC] * len(deps),
        out_specs=[HBM_SPEC] * n,
        out_shape=[jax.ShapeDtypeStruct(b.shape, b.dtype) for b in bufs],
        input_output_aliases={a: a for a in range(n)},
        scratch_shapes=[pltpu.SemaphoreType.DMA((n,)), pltpu.SemaphoreType.DMA((n,))],
    )(*bufs, *deps)


HBM_ONLY = pl.BlockSpec(memory_space=pltpu.HBM)
SEM_SPEC = pl.BlockSpec(memory_space=pltpu.SEMAPHORE)
DATAFLOW = pltpu.SideEffectType.DATAFLOW_SIDE_EFFECTING


def _in_hbm(a):
    return pltpu.with_memory_space_constraint(a, pltpu.HBM)


def _shard_part(ref, is_split, slot, h):
    if not is_split:
        return ref.at[slot]
    hr = ref.shape[1] // 2
    return ref.at[slot, pl.ds(h * hr, hr), :]


TOKEN = jax.ShapeDtypeStruct((8, LANES), F32)
VMEM_SPEC = pl.BlockSpec(memory_space=pltpu.VMEM)


def _gather_start(name, bufs, split, groups, deps=()):
    n, ng = len(bufs), len(groups)

    def body(*refs):
        ins, sems = refs[:n], refs[n + len(deps):n + len(deps) + 2 * ng]
        refs[-1][...] = jnp.zeros(TOKEN.shape, TOKEN.dtype)
        x, y, c, chips = _place()
        me = 2 * x + y
        for g, members in enumerate(groups):
            for s, a in enumerate(members):
                mine = _shard_part(ins[a], split[a], me, c)
                for j, chip in enumerate(chips):
                    pltpu.make_async_remote_copy(
                        src_ref=mine, dst_ref=mine, send_sem=sems[2 * g].at[3 * s + j], recv_sem=sems[2 * g + 1].at[3 * s + j],
                        device_id=(*chip, c), device_id_type=MESH).start()

    res = pl.pallas_call(
        body,
        name=name,
        in_specs=[HBM_ONLY] * n + [ANY_SPEC] * len(deps),
        out_specs=[SEM_SPEC] * (2 * ng) + [HBM_ONLY] * n + [VMEM_SPEC],
        out_shape=[pltpu.SemaphoreType.DMA((3 * len(members),)) for members in groups for _ in range(2)]
        + [pltpu.HBM(b.shape, b.dtype) for b in bufs] + [TOKEN],
        input_output_aliases={a: 2 * ng + a for a in range(n)},
        compiler_params=pltpu.CompilerParams(has_side_effects=DATAFLOW),
    )(*[_in_hbm(b) for b in bufs], *deps)
    return [(res[2 * g], res[2 * g + 1]) for g in range(ng)], list(res[2 * ng:2 * ng + n]), res[-1]


def _gather_wait(name, bufs, split, sems, after):
    n = len(bufs)

    def body(*refs):
        ins, send_sems, recv_sems = refs[:n], refs[n], refs[n + 1]
        x, y, c, chips = _place()
        me = 2 * x + y
        for s in range(n):
            for j, chip in enumerate(chips):
                copy = pltpu.make_async_remote_copy(
                    src_ref=_shard_part(ins[s], split[s], me, c),
                    dst_ref=_shard_part(ins[s], split[s], 2 * chip[0] + chip[1], c),
                    send_sem=send_sems.at[3 * s + j], recv_sem=recv_sems.at[3 * s + j],
                    device_id=(*chip, c), device_id_type=MESH)
                copy.wait_send()
                copy.wait_recv()

    res = pl.pallas_call(
        body,
        name=name,
        in_specs=[HBM_ONLY] * n + [SEM_SPEC, SEM_SPEC] + [ANY_SPEC] * len(after),
        out_specs=[HBM_ONLY] * n,
        out_shape=[pltpu.HBM(b.shape, b.dtype) for b in bufs],
        input_output_aliases={a: a for a in range(n)},
        compiler_params=pltpu.CompilerParams(has_side_effects=DATAFLOW),
    )(*bufs, *sems, *after)
    return list(res)


def _pass_halves(name, bufs):
    n = len(bufs)

    def body(*refs):
        outs = refs[n:2 * n]
        send_sems, recv_sems = refs[2 * n:]
        x, y, c, chips = _place()
        sibling = (x, y, 1 - c)

        def copy(a, j, h):
            blk = _shard_part(outs[a], True, 2 * chips[j][0] + chips[j][1], h)
            return pltpu.make_async_remote_copy(
                src_ref=blk, dst_ref=blk, send_sem=send_sems.at[3 * a + j], recv_sem=recv_sems.at[3 * a + j],
                device_id=sibling, device_id_type=MESH)

        sends = [copy(a, j, c) for a in range(n) for j in range(3)]
        for cp in sends:
            cp.start()
        for a in range(n):
            for j in range(3):
                copy(a, j, 1 - c).wait_recv()
        for cp in sends:
            cp.wait_send()

    res = pl.pallas_call(
        body,
        name=name,
        in_specs=[HBM_SPEC] * n,
        out_specs=[HBM_SPEC] * n,
        out_shape=[jax.ShapeDtypeStruct(b.shape, b.dtype) for b in bufs],
        input_output_aliases={a: a for a in range(n)},
        scratch_shapes=[pltpu.SemaphoreType.DMA((3 * n,)), pltpu.SemaphoreType.DMA((3 * n,))],
    )(*bufs)
    return list(res)


def _exchange_start(name, pair):
    def body(pair_ref, land_ref, send_sems, recv_sems, pair_thru, land_thru, token):
        x, y, c, chips = _place()
        me = 2 * x + y
        for j, chip in enumerate(chips):
            pltpu.make_async_remote_copy(
                src_ref=pair_ref.at[2 * chip[0] + chip[1]], dst_ref=land_ref.at[me],
                send_sem=send_sems.at[j], recv_sem=recv_sems.at[j], device_id=(*chip, c), device_id_type=MESH).start()
        token[...] = jnp.zeros(TOKEN.shape, TOKEN.dtype)

    send_sems, recv_sems, pair_thru, land_thru, token = pl.pallas_call(
        body,
        name=name,
        in_specs=[HBM_ONLY, HBM_ONLY],
        out_specs=[SEM_SPEC, SEM_SPEC, HBM_ONLY, HBM_ONLY, VMEM_SPEC],
        out_shape=[pltpu.SemaphoreType.DMA((3,)), pltpu.SemaphoreType.DMA((3,)),
                   pltpu.HBM(pair.shape, pair.dtype), pltpu.HBM(pair.shape, pair.dtype), TOKEN],
        input_output_aliases={0: 2, 1: 3},
        compiler_params=pltpu.CompilerParams(has_side_effects=DATAFLOW),
    )(_in_hbm(pair), _in_hbm(lax.empty(pair.shape, pair.dtype)))
    return (send_sems, recv_sems), pair_thru, land_thru, token


def _swap_start(name, halves):
    n_p, _, hr, cols = halves.shape
    land_shape = (n_p, hr, cols)

    def body(halves_ref, land_ref, send_sems, recv_sems, halves_thru, land_thru, token):
        x, y, c, _ = _place()
        for p in range(n_p):
            pltpu.make_async_remote_copy(
                src_ref=halves_ref.at[p, 1 - c], dst_ref=land_ref.at[p], send_sem=send_sems.at[p], recv_sem=recv_sems.at[p],
                device_id=(x, y, 1 - c), device_id_type=MESH).start()
        token[...] = jnp.zeros(TOKEN.shape, TOKEN.dtype)

    send_sems, recv_sems, halves_thru, land_thru, token = pl.pallas_call(
        body,
        name=name,
        in_specs=[HBM_ONLY, HBM_ONLY],
        out_specs=[SEM_SPEC, SEM_SPEC, HBM_ONLY, HBM_ONLY, VMEM_SPEC],
        out_shape=[pltpu.SemaphoreType.DMA((n_p,)), pltpu.SemaphoreType.DMA((n_p,)),
                   pltpu.HBM(halves.shape, halves.dtype), pltpu.HBM(land_shape, halves.dtype), TOKEN],
        input_output_aliases={0: 2, 1: 3},
        compiler_params=pltpu.CompilerParams(has_side_effects=DATAFLOW),
    )(_in_hbm(halves), _in_hbm(lax.empty(land_shape, halves.dtype)))
    return (send_sems, recv_sems), halves_thru, land_thru, token


def _swap_wait(name, halves, land, sems, after):
    n_p = halves.shape[0]

    def body(halves_ref, land_ref, send_sems, recv_sems, *rest):
        x, y, c, _ = _place()
        for p in range(n_p):
            copy = pltpu.make_async_remote_copy(
                src_ref=halves_ref.at[p, 1 - c], dst_ref=land_ref.at[p], send_sem=send_sems.at[p], recv_sem=recv_sems.at[p],
                device_id=(x, y, 1 - c), device_id_type=MESH)
            copy.wait_send()
            copy.wait_recv()

    return pl.pallas_call(
        body,
        name=name,
        in_specs=[HBM_ONLY, HBM_ONLY, SEM_SPEC, SEM_SPEC] + [ANY_SPEC] * len(after),
        out_specs=[HBM_ONLY, HBM_ONLY],
        out_shape=[pltpu.HBM(halves.shape, halves.dtype), pltpu.HBM(land.shape, land.dtype)],
        input_output_aliases={0: 0, 1: 1},
        compiler_params=pltpu.CompilerParams(has_side_effects=DATAFLOW),
    )(halves, land, *sems, *after)


def _exchange_wait(name, pair, land, sems, after):
    def body(pair_ref, land_ref, send_sems, recv_sems, *rest):
        x, y, c, chips = _place()
        for j, chip in enumerate(chips):
            k = 2 * chip[0] + chip[1]
            copy = pltpu.make_async_remote_copy(
                src_ref=pair_ref.at[k], dst_ref=land_ref.at[k], send_sem=send_sems.at[j], recv_sem=recv_sems.at[j],
                device_id=(*chip, c), device_id_type=MESH)
            copy.wait_send()
            copy.wait_recv()

    return pl.pallas_call(
        body,
        name=name,
        in_specs=[HBM_ONLY, HBM_ONLY, SEM_SPEC, SEM_SPEC] + [HBM_SPEC] * len(after),
        out_specs=[HBM_ONLY, HBM_ONLY],
        out_shape=[pltpu.HBM(pair.shape, pair.dtype), pltpu.HBM(land.shape, land.dtype)],
        input_output_aliases={0: 0, 1: 1},
        compiler_params=pltpu.CompilerParams(has_side_effects=DATAFLOW),
    )(pair, land, *sems, *after)


def _allgather_small(block, deps=()):
    m_per, n = block.shape

    def body(x_ref, *rest):
        out_ref, send_sems, recv_sems, local_sem = rest[len(deps):]
        x, y, c, chips = _place()
        me, sibling = (x, y, c), (x, y, 1 - c)

        def rows(px, py, pc):
            return out_ref.at[pl.ds((4 * px + 2 * py + pc) * m_per, m_per), :]

        def copy(k, blk, to, src=None):
            return pltpu.make_async_remote_copy(
                src_ref=rows(*blk) if src is None else src, dst_ref=rows(*blk),
                send_sem=send_sems.at[k], recv_sem=recv_sems.at[k], device_id=to, device_id_type=MESH)

        mine = pltpu.make_async_copy(x_ref, rows(*me), local_sem)
        mine.start()
        first = [copy(0, me, sibling, src=x_ref)]
        first += [copy(1 + j, me, (*chip, c), src=x_ref) for j, chip in enumerate(chips)]
        for cp in first:
            cp.start()
        passed = [copy(4 + j, (*chip, c), sibling) for j, chip in enumerate(chips)]
        for j, chip in enumerate(chips):
            copy(1 + j, (*chip, c), me).wait_recv()
            passed[j].start()
        copy(0, sibling, me).wait_recv()
        for j, chip in enumerate(chips):
            copy(4 + j, (*chip, 1 - c), me).wait_recv()
        for cp in first + passed:
            cp.wait_send()
        mine.wait()

    return pl.pallas_call(
        body,
        name="allgather_small",
        out_shape=jax.ShapeDtypeStruct((8 * m_per, n), block.dtype),
        in_specs=[pl.BlockSpec(memory_space=pltpu.VMEM)] + [ANY_SPEC] * len(deps),
        out_specs=pl.BlockSpec(memory_space=pltpu.VMEM),
        scratch_shapes=[pltpu.SemaphoreType.DMA((7,)), pltpu.SemaphoreType.DMA((7,)), pltpu.SemaphoreType.DMA],
    )(block, *deps)


def _sum_blocks(name, gathered, n_blocks):
    r = gathered.shape[0] // n_blocks
    c = gathered.shape[1]

    def body(g_ref, o_ref):
        acc = g_ref[0:r, :]
        for b in range(1, n_blocks):
            acc = acc + g_ref[b * r:(b + 1) * r, :]
        o_ref[...] = acc

    return pl.pallas_call(body, name=name, out_shape=jax.ShapeDtypeStruct((r, c), F32))(gathered)


def _largest_tile(n, cap, mult):
    best = None
    for d in range(mult, min(n, cap) + 1, mult):
        if n % d == 0:
            best = d
    assert best is not None, (n, cap, mult)
    return best


def kernel(x, meta, g_pre_mix, w_in, b_gates, conf_dw_w, conf_dw_b, conf_ln_g, conf_ln_b, conf_w_pw, short_dw_w, short_w_out, w_o, g_post_mix, g_pre_mlp, w_up, w_down, g_post_mlp, loss_target, m_meta, m_g_pre_mix, m_w_in, m_b_gates, m_conf_dw_w, m_conf_dw_b, m_conf_ln_g, m_conf_ln_b, m_conf_w_pw, m_short_dw_w, m_short_w_out, m_w_o, m_g_post_mix, m_g_pre_mlp, m_w_up, m_w_down, m_g_post_mlp, v_meta, v_g_pre_mix, v_w_in, v_b_gates, v_conf_dw_w, v_conf_dw_b, v_conf_ln_g, v_conf_ln_b, v_conf_w_pw, v_short_dw_w, v_short_w_out, v_w_o, v_g_post_mix, v_g_pre_mlp, v_w_up, v_w_down, v_g_post_mlp):
    seq, d = x.shape[1], x.shape[2]
    t_real = seq + N_META
    t = -(-t_real // LANES) * LANES
    d_conf = conf_dw_b.shape[1]
    d_ff = w_up.shape[2] * N_CHIPS
    in_cols = w_in.shape[2] * N_CHIPS
    assert in_cols == 5 * d_conf + 2 * d and d == 2 * d_conf
    cw = d_conf
    core = lax.axis_index("c")
    chip = 2 * lax.axis_index("x") + lax.axis_index("y")

    tr = _largest_tile(t, 272, ROW_CHUNK)
    tm = t
    tn_in = _largest_tile(in_cols // N_CHIPS, 768, LANES)
    tn_d = _largest_tile(d, 1024, LANES)
    tn_h = _largest_tile(d, 512, LANES)
    tn_ff = _largest_tile(d_ff // N_CHIPS, 1024, LANES)
    tn_pw = _largest_tile(d // N_CHIPS, 512, LANES)

    big = [w_in[0], conf_w_pw[0], short_w_out[0], w_o[0], w_up[0], w_down[0]]
    chip_arr = chip.astype(jnp.int32).reshape(1)

    def cast(a, deps):
        return _into_slot(f"cast_w{a}", big[a], N_CHIPS, chip_arr, MXU_DTYPE, _largest_tile(big[a].shape[0], 256, 16), deps)

    small = [_into_slot(f"place_w{a}", w, N_CHIPS, chip_arr, F32, w.shape[0])
             for a, w in enumerate([meta, conf_dw_w[0], short_dw_w[0]])]
    sems_small, fly_small, tok_small = _gather_start("gather_start_small", small, [False] * 3, [[0, 1, 2]])
    sems_in, fly_in, tok_in = _gather_start("gather_start_in", [cast(0, [tok_small])], [True], [[0]])
    rest_groups = [[0, 1], [2], [3], [4]]
    sems_rest, fly_rest, tok_rest = _gather_start(
        "gather_start_rest", [cast(a, [tok_in]) for a in range(1, 6)], [True] * 5, rest_groups)

    def arrive(g, after):
        members = rest_groups[g]
        got = _gather_wait(f"gather_wait_rest{g}", [fly_rest[a] for a in members], [True] * len(members),
                           sems_rest[g], after)
        return _pass_halves(f"gather_pass_rest{g}", got)

    meta_g, wdw_g, w3_g = _gather_wait("gather_wait_small", fly_small, [False] * 3, sems_small[0], [tok_in])
    meta_full = jnp.transpose(meta_g, (1, 0, 2)).reshape(N_META, d)
    wdw = jnp.transpose(wdw_g, (1, 0, 2)).reshape(CONF_KERNEL, d_conf)
    w3 = jnp.transpose(w3_g, (1, 0, 2)).reshape(SHORT_KERNEL, d_conf)

    tail = jnp.zeros((t - t_real, d), F32)
    h0 = jnp.concatenate([meta_full, x[0], tail], axis=0)
    target = jnp.concatenate([jnp.zeros((N_META, d), F32), loss_target[0], tail], axis=0)

    def norm_in(i, rows, vecs):
        return [_rms_fwd(rows[0], vecs[0])], []

    (n_lp,) = _rowwise("norm_in", norm_in, [(h0, d, 0)], [g_pre_mix], [(d, MXU_DTYPE)], [], tr, deps=[tok_rest])
    (wg_in,) = _pass_halves("gather_pass_in", _gather_wait("gather_wait_in", fly_in, [True], sems_in[0], [n_lp]))
    proj =_matmul("proj", n_lp, wg_in, kind="nn", tm=tm, tn=tn_in, tk=d, out_dtypes=[F32])
    ac, c3 = _conv_fwd(proj, wdw, conf_dw_b, w3, d_conf)

    def ln_parts(ac_t, ln_g, ln_b):
        mu = jnp.mean(ac_t, axis=-1, keepdims=True)
        xc = ac_t - mu
        rstd = lax.rsqrt(jnp.mean(xc * xc, axis=-1, keepdims=True) + LN_EPS)
        xh = xc * rstd
        return xh, rstd, xh * ln_g + ln_b

    def branch_act(i, rows, vecs):
        ac_t, c3_t, bg_t = rows
        _, _, al = ln_parts(ac_t, vecs[0], vecs[1])
        return [al * _sigmoid(al), bg_t * c3_t], []

    a_act, s_lp = _rowwise("branch_act", branch_act, [(ac, cw, 0), (c3, cw, 0), (proj, cw, 2)],
                           [conf_ln_g, conf_ln_b], [(d_conf, MXU_DTYPE), (d_conf, MXU_DTYPE)], [], tr)
    wg_pw, wg_sout = arrive(0, [a_act])
    y_a = _matmul("y_a", a_act, wg_pw, kind="nn", tm=tm, tn=tn_pw, tk=d_conf, out_dtypes=[F32])
    y_b = _matmul("y_b", s_lp, wg_sout, kind="nn", tm=tm, tn=tn_pw, tk=d_conf, out_dtypes=[F32])

    gate_rows = [(proj, cw, 5), (proj, cw, 6), (proj, cw, 7), (proj, cw, 8)]

    def gates_of(rows, b):
        ga = _sigmoid(jnp.concatenate([rows[0], rows[1]], axis=1) + b[:, :d])
        gb = _sigmoid(jnp.concatenate([rows[2], rows[3]], axis=1) + b[:, d:])
        return ga, gb

    def gate(i, rows, vecs):
        ga, gb = gates_of(rows[2:], vecs[0])
        return [ga * rows[0] + gb * rows[1]], []

    (m_lp,) = _rowwise("gate", gate, [(y_a, d, 0), (y_b, d, 0)] + gate_rows, [b_gates], [(d, MXU_DTYPE)], [], tr)
    wg_o = arrive(1, [m_lp])[0].reshape(d, d)
    mix = _matmul("mix", m_lp, wg_o, kind="nn", tm=tm, tn=tn_d, tk=d, out_dtypes=[F32])

    def post_mix(i, rows, vecs):
        h1_t = rows[0] + _rms_fwd(rows[1], vecs[0])
        return [h1_t, _rms_fwd(h1_t, vecs[1])], []

    h1, n2_lp = _rowwise("post_mix", post_mix, [(h0, d, 0), (mix, d, 0)], [g_post_mix, g_pre_mlp],
                         [(d, F32), (d, MXU_DTYPE)], [], tr)
    (wg_up,) = arrive(2, [n2_lp])
    up, f_lp = _matmul("up", n2_lp, wg_up, kind="nn", tm=tm, tn=tn_h, tk=d, out_dtypes=[F32, MXU_DTYPE],
                       epilogue=lambda acc: (acc, jnp.square(jnp.maximum(acc, 0.0))))
    wg_down = arrive(3, [f_lp])[0].reshape(d_ff, d)
    dn = _matmul("down", f_lp, wg_down, kind="nn", tm=tm, tn=tn_h, tk=_largest_tile(d_ff, 2048, LANES),
                 out_dtypes=[F32])

    def head(i, rows, vecs):
        h1_t, dn_t, tgt = rows
        y = h1_t + _rms_fwd(dn_t, vecs[0])
        row = i + lax.broadcasted_iota(jnp.int32, (ROW_CHUNK, 1), 0)
        err = jnp.where(jnp.logical_and(row >= N_META, row < t_real), y - tgt, 0.0)
        dy = err / d
        d_dn, dg = _rms_bwd(dn_t, vecs[0], dy)
        loss_rows = 0.5 * jnp.mean(err * err, axis=-1, keepdims=True)
        return [dy, d_dn], [dg, jnp.broadcast_to(loss_rows, (ROW_CHUNK, LANES))]

    dy, d_dn, dg_post_mlp, loss_vec = _rowwise(
        "head", head, [(h1, d, 0), (dn, d, 0), (target, d, 0)], [g_post_mlp], [(d, F32), (d, MXU_DTYPE)], [d, LANES], tr)
    loss = lax.psum(loss_vec[0, 0], ("x", "y", "c"))

    core_arr = core.astype(jnp.int32).reshape(1)
    place = jnp.stack([chip, core]).astype(jnp.int32)
    in_flight = {}


    def swap_start(a, gw):
        halves = gw.reshape(N_CHIPS, 2, gw.shape[1] // 2, gw.shape[2])
        *in_flight[a], token = _swap_start(f"swap_start{a}", halves)
        return token

    def exchange_start(a, after):
        sems, halves, land = in_flight[a]
        halves, land = _swap_wait(f"swap_wait{a}", halves, land, sems, after)
        pair = _pair_sum(f"pair_sum{a}", halves, land, core_arr, _largest_tile(land.shape[1], 256, 16))
        *in_flight[a], token = _exchange_start(f"exchange_start{a}", pair)
        return token

    def reduce_finish(a, after):
        sems, pair, land = in_flight[a]
        pair, land = _exchange_wait(f"exchange_wait{a}", pair, land, sems, after)
        return _sum_pieces(f"sum_pieces{a}", pair, land, place, _largest_tile(land.shape[1], 256, 16))

    d_up = _matmul("d_up", d_dn, wg_down, kind="nt", tm=tm, tn=tn_h, tk=d, out_dtypes=[MXU_DTYPE], extras=[up],
                   epilogue=lambda acc, up_t: (acc * (2.0 * jnp.maximum(up_t, 0.0)),))
    tk_t = t
    gw_down = _matmul("gw_down", f_lp, d_dn, kind="tn", tm=_largest_tile(d_ff, 2048, LANES), tn=tn_d, tk=tk_t,
                      out_dtypes=[F32])
    tok = swap_start(5, gw_down.reshape(N_CHIPS, d_ff // N_CHIPS, d))
    d_n2 = _matmul("d_n2", d_up, wg_up, kind="nt", tm=tm, tn=tn_h, tk=d_ff // N_CHIPS, out_dtypes=[F32], deps=[tok])
    tok = exchange_start(5, [d_n2])
    gw_up = _matmul("gw_up", n2_lp, d_up, kind="tn", tm=_largest_tile(d, 2048, LANES), tn=tn_ff, tk=tk_t,
                    out_dtypes=[F32], out_pieces=N_CHIPS, deps=[tok])
    tok = swap_start(4, gw_up)

    def bwd_mid(i, rows, vecs):
        dy_t, dn2_t, h1_t, mix_t = rows
        d_h1a, dg_pre_mlp = _rms_bwd(h1_t, vecs[1], dn2_t)
        d_h1 = dy_t + d_h1a
        d_mix, dg_post_mix = _rms_bwd(mix_t, vecs[0], d_h1)
        return [d_h1, d_mix], [dg_pre_mlp, dg_post_mix]

    d_h1, d_mix, dg_pre_mlp, dg_post_mix = _rowwise(
        "bwd_mid", bwd_mid, [(dy, d, 0), (d_n2, d, 0), (h1, d, 0), (mix, d, 0)], [g_post_mix, g_pre_mlp],
        [(d, F32), (d, MXU_DTYPE)], [d, d], tr, deps=[tok])
    d_m = _matmul("d_m", d_mix, wg_o, kind="nt", tm=tm, tn=tn_d, tk=d, out_dtypes=[F32])
    tok = exchange_start(4, [d_m])
    gw_o = _matmul("gw_o", m_lp, d_mix, kind="tn", tm=_largest_tile(d, 2048, LANES), tn=tn_d, tk=tk_t, out_dtypes=[F32],
                   deps=[tok])
    tok = swap_start(3, gw_o.reshape(N_CHIPS, d // N_CHIPS, d))

    def gate_bwd(i, rows, vecs):
        dm_t, ya_t, yb_t = rows[:3]
        ga, gb = gates_of(rows[3:], vecs[0])
        d_gpre = jnp.concatenate([dm_t * ya_t * ga * (1.0 - ga), dm_t * yb_t * gb * (1.0 - gb)], axis=1)
        return [dm_t * ga, dm_t * gb, d_gpre], [d_gpre]

    d_ya, d_yb, d_gpre, dg_b_gates = _rowwise(
        "gate_bwd", gate_bwd, [(d_m, d, 0), (y_a, d, 0), (y_b, d, 0)] + gate_rows, [b_gates],
        [(d, MXU_DTYPE), (d, MXU_DTYPE), (2 * d, MXU_DTYPE)], [2 * d], tr, deps=[tok])
    tok = exchange_start(3, [d_ya])
    d_aact = _matmul("d_aact", d_ya, wg_pw, kind="nt", tm=tm, tn=d_conf, tk=tn_pw, out_dtypes=[F32], deps=[tok])
    gw_pw = _matmul("gw_pw", a_act, d_ya, kind="tn", tm=_largest_tile(d_conf, 2048, LANES), tn=tn_pw, tk=tk_t,
                    out_dtypes=[F32], out_pieces=N_CHIPS)
    tok = swap_start(1, gw_pw)
    d_s = _matmul("d_s", d_yb, wg_sout, kind="nt", tm=tm, tn=d_conf, tk=tn_pw, out_dtypes=[F32], deps=[tok])
    tok = exchange_start(1, [d_s])
    gw_sout = _matmul("gw_sout", s_lp, d_yb, kind="tn", tm=_largest_tile(d_conf, 2048, LANES), tn=tn_pw, tk=tk_t,
                      out_dtypes=[F32], out_pieces=N_CHIPS, deps=[tok])
    tok = swap_start(2, gw_sout)

    def branch_bwd(i, rows, vecs):
        daact_t, ds_t, ac_t, c3_t, bg_t = rows
        xh, rstd, al = ln_parts(ac_t, vecs[0], vecs[1])
        sg = _sigmoid(al)
        d_al = daact_t * (sg * (1.0 + al * (1.0 - sg)))
        dxh = d_al * vecs[0]
        d_ac = rstd * (dxh - jnp.mean(dxh, axis=-1, keepdims=True) - xh * jnp.mean(dxh * xh, axis=-1, keepdims=True))
        return [d_ac, ds_t * bg_t, ds_t * c3_t], [d_al * xh, d_al, d_ac]

    d_ac, d_c3, d_bg, dg_ln_g, dg_ln_b, dg_dw_b = _rowwise(
        "branch_bwd", branch_bwd, [(d_aact, cw, 0), (d_s, cw, 0), (ac, cw, 0), (c3, cw, 0), (proj, cw, 2)],
        [conf_ln_g, conf_ln_b], [(d_conf, F32), (d_conf, F32), (d_conf, MXU_DTYPE)], [d_conf] * 3, tr, deps=[tok])
    tok = exchange_start(2, [d_ac])
    d_av, d_ag, d_cg, d_v, dg_wdw, dg_w3 = _conv_bwd(proj, d_ac, d_c3, wdw, w3, d_conf)
    d_proj = jnp.concatenate([d_av, d_ag, d_bg, d_cg, d_v, d_gpre], axis=1)
    d_n = _matmul("d_n", d_proj, wg_in, kind="nt", tm=tm, tn=tn_h, tk=in_cols // N_CHIPS, out_dtypes=[F32], deps=[tok])

    def bwd_in(i, rows, vecs):
        d_h0a, dg = _rms_bwd(rows[2], vecs[0], rows[1])
        return [rows[0] + d_h0a], [dg]

    d_h0, dg_pre_mix = _rowwise("bwd_in", bwd_in, [(d_h1, d, 0), (d_n, d, 0), (h0, d, 0)], [g_pre_mix],
                                [(d, F32)], [d], tr)
    grad_x = d_h0[N_META:t_real][None]

    small_w = d_conf
    rep = [dg_pre_mix, dg_b_gates, dg_dw_b, dg_ln_g, dg_ln_b, dg_post_mix, dg_pre_mlp, dg_post_mlp]
    rep_w = [g_pre_mix, b_gates, conf_dw_b, conf_ln_g, conf_ln_b, g_post_mix, g_pre_mlp, g_post_mlp]
    rep_m = [m_g_pre_mix, m_b_gates, m_conf_dw_b, m_conf_ln_g, m_conf_ln_b, m_g_post_mix, m_g_pre_mlp, m_g_post_mlp]
    rep_v = [v_g_pre_mix, v_b_gates, v_conf_dw_b, v_conf_ln_g, v_conf_ln_b, v_g_post_mix, v_g_pre_mlp, v_g_post_mlp]
    col = [dg_wdw, dg_w3, d_h0[:N_META]]
    assert all(a.size % small_w == 0 for a in rep + col)

    def pack(arrs):
        flat = jnp.concatenate([a.reshape(-1, small_w) for a in arrs], axis=0)
        return jnp.pad(flat, ((0, -flat.shape[0] % 8), (0, 0)))

    def unpack(buf, like):
        out, r0 = [], 0
        for a in like:
            nr = a.size // small_w
            out.append(buf[r0:r0 + nr].reshape(a.shape))
            r0 += nr
        return out

    packed = pack(rep + col)
    total = _sum_blocks("sum_small", _allgather_small(packed), 8)
    small_g = unpack(total, rep + col)
    g_rep = small_g[:len(rep)]
    g_wdw_full, g_w3_full, g_meta_full = small_g[len(rep):]
    sc = d_conf // N_CHIPS
    g_wdw = lax.dynamic_slice_in_dim(g_wdw_full, chip * sc, sc, axis=1)
    g_w3 = lax.dynamic_slice_in_dim(g_w3_full, chip * sc, sc, axis=1)
    g_meta = lax.dynamic_slice_in_dim(g_meta_full, chip * (d // N_CHIPS), d // N_CHIPS, axis=1)

    gw_in = _matmul("gw_in", n_lp, d_proj, kind="tn", tm=_largest_tile(d, 2048, LANES), tn=tn_in, tk=tk_t,
                    out_dtypes=[F32], out_pieces=N_CHIPS, deps=[total])
    tok = swap_start(0, gw_in)

    big_m = [m_w_in, m_conf_w_pw, m_short_w_out, m_w_o, m_w_up, m_w_down]
    big_v = [v_w_in, v_conf_w_pw, v_short_w_out, v_w_o, v_w_up, v_w_down]
    big_res = {}

    def reduce_group(members, after):
        reduced = []
        for a in members:
            reduced.append(reduce_finish(a, after))
            after = [reduced[-1]]
        return reduced

    def adam_group(members, joined, deps):
        for a, j in zip(members, joined):
            big_res[a] = _adamw(f"adamw_big{a}", big[a], j.reshape(big[a].shape), big_m[a][0], big_v[a][0],
                                _largest_tile(big[a].shape[0], 256, 8), deps)

    early = [5, 4, 3, 1, 2]
    reduced_early = reduce_group(early, [tok])
    tok = exchange_start(0, reduced_early)
    adam_group(early, _join_halves("join_halves_early", reduced_early, deps=[tok]), [])
    rep_pack = [pack(arrs) for arrs in (rep_w, g_rep, rep_m, rep_v)]
    rep_res = [unpack(buf, rep_w) for buf in _adamw("adamw_rep", *rep_pack, rep_pack[0].shape[0])]
    col_res = [_adamw(f"adamw_col{a}", w, g, m, v, w.shape[0]) for a, (w, g, m, v) in enumerate(
        [(meta, g_meta, m_meta, v_meta), (conf_dw_w[0], g_wdw, m_conf_dw_w[0], v_conf_dw_w[0]),
         (short_dw_w[0], g_w3, m_short_dw_w[0], v_short_dw_w[0])])]
    reduced_late = reduce_group([0], [col_res[0][1], rep_res[1][0]] + [big_res[a][1] for a in early])
    adam_group([0], _join_halves("join_halves_late", reduced_late), [])

    def leaf(q):
        r = lambda a: rep_res[q][a]
        b = lambda a: big_res[a][q][None]
        return [col_res[0][q], r(0), b(0), r(1), col_res[1][q][None], r(2), r(3), r(4), b(1), col_res[2][q][None], b(2),
                b(3), r(5), r(6), b(4), b(5), r(7)]

    return (loss, grad_x, *leaf(0), *leaf(1), *leaf(2), *leaf(3))
```

```python
import functools

import jax
import jax.numpy as jnp
from jax import lax
from jax.experimental import pallas as pl
from jax.experimental.pallas import tpu as pltpu

F32 = jnp.float32
BF16 = jnp.bfloat16
MXU_DTYPE = BF16
WIRE_DTYPE = BF16

N_META = 16
CONF_KERNEL = 31
SHORT_KERNEL = 3
CONV_PAD = 32
RMS_EPS = 1e-6
LN_EPS = 1e-5
ADAM_LR = 0.001
ADAM_B1 = 0.9
ADAM_B2 = 0.999
ADAM_EPS = 1e-08
ADAM_WD = 0.01
ADAM_STEP = 10

N_CHIPS = 4
MESH = pl.DeviceIdType.MESH
LANES = 128


def _sigmoid(z):
    return 1.0 / (1.0 + jnp.exp(-z))


ANY_SPEC = pl.BlockSpec(memory_space=pl.ANY)


def _matmul(name, a, b, *, kind, tm, tn, tk, out_dtypes, out_pieces=1, epilogue=None, extras=(), deps=()):
    pieces = b.shape[0] if b.ndim == 3 else 1
    if kind == "nn":
        m, kdim = a.shape
        n = b.shape[-1] * pieces
        dims = (((1,), (0,)), ((), ()))
        a_spec = pl.BlockSpec((tm, tk), lambda i, j, k: (i, k))
        if b.ndim == 2:
            b_spec = pl.BlockSpec((tk, tn), lambda i, j, k: (k, j))
        else:
            npp = b.shape[-1] // tn
            b_spec = pl.BlockSpec((None, tk, tn), lambda i, j, k: (j // npp, k, j % npp))
    elif kind == "nt":
        m, kdim = a.shape
        n = b.shape[-2]
        dims = (((1,), (1,)), ((), ()))
        a_spec = pl.BlockSpec((tm, tk), lambda i, j, k: (i, k))
        if b.ndim == 2:
            b_spec = pl.BlockSpec((tn, tk), lambda i, j, k: (j, k))
        else:
            kpp = b.shape[-1] // tk
            b_spec = pl.BlockSpec((None, tn, tk), lambda i, j, k: (k // kpp, j, k % kpp))
    else:
        kdim, m = a.shape
        n = b.shape[-1]
        dims = (((0,), (0,)), ((), ()))
        a_spec = pl.BlockSpec((tk, tm), lambda i, j, k: (k, i))
        b_spec = pl.BlockSpec((tk, tn), lambda i, j, k: (k, j))
    assert m % tm == 0 and n % tn == 0 and kdim % tk == 0, (name, m, n, kdim, tm, tn, tk)
    nk = kdim // tk
    if out_pieces == 1:
        out_shape = (m, n)
        out_spec = pl.BlockSpec((tm, tn), lambda i, j, k: (i, j))
    else:
        onpp = n // out_pieces // tn
        out_shape = (out_pieces, m, n // out_pieces)
        out_spec = pl.BlockSpec((None, tm, tn), lambda i, j, k: (j // onpp, i, j % onpp))
    n_ex, n_out, n_in = len(extras), len(out_dtypes), len(extras) + len(deps)
    if epilogue is None:
        epilogue = lambda acc: (acc,)

    def body(a_ref, b_ref, *rest):
        ex_refs, o_refs = rest[:n_ex], rest[n_in:n_in + n_out]
        prod = lax.dot_general(a_ref[...], b_ref[...], dims, preferred_element_type=F32)

        def finish(acc):
            tiles = epilogue(acc, *[r[...] for r in ex_refs])
            for o_ref, t in zip(o_refs, tiles):
                o_ref[...] = t.astype(o_ref.dtype)

        if nk == 1:
            finish(prod)
        else:
            acc_ref = rest[n_in + n_out]
            k = pl.program_id(2)

            @pl.when(k == 0)
            def _():
                acc_ref[...] = prod

            @pl.when(jnp.logical_and(k > 0, k < nk - 1))
            def _():
                acc_ref[...] += prod

            @pl.when(k == nk - 1)
            def _():
                finish(acc_ref[...] + prod)

    ex_specs = [pl.BlockSpec((tm, tn), lambda i, j, k: (i, j)) for _ in extras]
    res = pl.pallas_call(
        body,
        name=name,
        grid=(m // tm, n // tn, nk),
        in_specs=[a_spec, b_spec, *ex_specs] + [ANY_SPEC] * len(deps),
        out_specs=[out_spec] * n_out,
        out_shape=[jax.ShapeDtypeStruct(out_shape, d) for d in out_dtypes],
        scratch_shapes=[pltpu.VMEM((tm, tn), F32)] if nk > 1 else [],
        compiler_params=pltpu.CompilerParams(dimension_semantics=("parallel", "parallel", "arbitrary")),
    )(a, b, *extras, *deps)
    return res[0] if n_out == 1 else res


ROW_CHUNK = 16
SUBLANES = 8


def _rowwise(name, fn, rows, vecs, outs, sums, tr, deps=()):
    t = rows[0][0].shape[0]
    assert t % tr == 0 and tr % ROW_CHUNK == 0
    n_r, n_v, n_o, n_s = len(rows), len(vecs), len(outs), len(sums)
    n_in = n_r + n_v + len(deps)
    n_steps = t // tr

    def body(*refs):
        r_in, v_in = refs[:n_r], refs[n_r:n_r + n_v]
        o_refs = refs[n_in:n_in + n_o]
        s_refs = refs[n_in + n_o:n_in + n_o + n_s]
        acc_refs = refs[n_in + n_o + n_s:]
        i = pl.program_id(0)

        @pl.when(i == 0)
        def _():
            for acc_ref in acc_refs:
                acc_ref[...] = jnp.zeros(acc_ref.shape, F32)

        def chunk(ci):
            r0 = ci * ROW_CHUNK
            sl = pl.ds(r0, ROW_CHUNK)
            o_tiles, s_tiles = fn(i * tr + r0, [r[sl, :] for r in r_in], [v[...] for v in v_in])
            for o_ref, tile in zip(o_refs, o_tiles):
                o_ref[sl, :] = tile.astype(o_ref.dtype)
            for acc_ref, tile in zip(acc_refs, s_tiles):
                part = tile[0:SUBLANES]
                for s in range(1, ROW_CHUNK // SUBLANES):
                    part = part + tile[s * SUBLANES:(s + 1) * SUBLANES]
                acc_ref[...] += part

        for ci in range(tr // ROW_CHUNK):
            chunk(ci)

        @pl.when(i == n_steps - 1)
        def _():
            for s_ref, acc_ref in zip(s_refs, acc_refs):
                s_ref[...] = jnp.sum(acc_ref[...], axis=0, keepdims=True)

    def row_spec(width, blk):
        return pl.BlockSpec((tr, width), lambda i: (i, blk))

    res = pl.pallas_call(
        body,
        name=name,
        grid=(t // tr,),
        in_specs=[row_spec(w, blk) for _, w, blk in rows]
        + [pl.BlockSpec(v.shape, lambda i: (0, 0)) for v in vecs] + [ANY_SPEC] * len(deps),
        out_specs=[pl.BlockSpec((tr, c), lambda i: (i, 0)) for c, _ in outs]
        + [pl.BlockSpec((1, c), lambda i: (0, 0)) for c in sums],
        out_shape=[jax.ShapeDtypeStruct((t, c), d) for c, d in outs]
        + [jax.ShapeDtypeStruct((1, c), F32) for c in sums],
        scratch_shapes=[pltpu.VMEM((SUBLANES, c), F32) for c in sums],
        compiler_params=pltpu.CompilerParams(dimension_semantics=("arbitrary",)),
    )(*[r[0] for r in rows], *vecs, *deps)
    return res


def _rms_fwd(x, g):
    r = lax.rsqrt(jnp.mean(x * x, axis=-1, keepdims=True) + RMS_EPS)
    return x * r * g


def _rms_bwd(x, g, dy):
    r = lax.rsqrt(jnp.mean(x * x, axis=-1, keepdims=True) + RMS_EPS)
    xn = x * r
    dxn = dy * g
    dx = r * (dxn - xn * jnp.mean(dxn * xn, axis=-1, keepdims=True))
    return dx, dy * xn


CONV_ROWS = 64
CONV_LANES = 128


def _conv_fwd(proj, wdw, bdw, w3, d_conf):
    t = proj.shape[0]
    cl = CONV_LANES
    nb = d_conf // cl
    nchunk = t // CONV_ROWS
    assert t % CONV_ROWS == 0

    def body(av_ref, ag_ref, cg_ref, v_ref, wdw_ref, bdw_ref, w3_ref, ac_ref, c3_ref, apad, cpad):
        zeros = jnp.zeros((CONV_PAD, cl), F32)
        apad[0:CONV_PAD, :] = zeros
        cpad[0:CONV_PAD, :] = zeros
        apad[CONV_PAD:, :] = av_ref[...] * _sigmoid(ag_ref[...])
        cpad[CONV_PAD:, :] = cg_ref[...] * v_ref[...]

        def chunk(ci, carry):
            base = pl.multiple_of(ci * CONV_ROWS, 8)
            acc = jnp.zeros((CONV_ROWS, cl), F32) + bdw_ref[...]
            for k in range(CONF_KERNEL):
                off = CONV_PAD - (CONF_KERNEL - 1) + k
                acc = acc + apad[pl.ds(base + off, CONV_ROWS), :] * wdw_ref[k:k + 1, :]
            ac_ref[pl.ds(base, CONV_ROWS), :] = acc
            acc3 = jnp.zeros((CONV_ROWS, cl), F32)
            for k in range(SHORT_KERNEL):
                off = CONV_PAD - (SHORT_KERNEL - 1) + k
                acc3 = acc3 + cpad[pl.ds(base + off, CONV_ROWS), :] * w3_ref[k:k + 1, :]
            c3_ref[pl.ds(base, CONV_ROWS), :] = acc3
            return carry

        lax.fori_loop(0, nchunk, chunk, 0)

    def col(blk0):
        return pl.BlockSpec((t, cl), lambda j: (0, blk0 + j))

    return pl.pallas_call(
        body,
        name="conv_fwd",
        grid=(nb,),
        in_specs=[col(0), col(nb), col(3 * nb), col(4 * nb),
                  pl.BlockSpec((CONF_KERNEL, cl), lambda j: (0, j)),
                  pl.BlockSpec((1, cl), lambda j: (0, j)),
                  pl.BlockSpec((SHORT_KERNEL, cl), lambda j: (0, j))],
        out_specs=[pl.BlockSpec((t, cl), lambda j: (0, j))] * 2,
        out_shape=[jax.ShapeDtypeStruct((t, d_conf), F32)] * 2,
        scratch_shapes=[pltpu.VMEM((t + CONV_PAD, cl), F32)] * 2,
        compiler_params=pltpu.CompilerParams(dimension_semantics=("parallel",)),
    )(proj, proj, proj, proj, wdw, bdw, w3)


def _conv_bwd(proj, d_ac, d_c3, wdw, w3, d_conf):
    t = proj.shape[0]
    cl = CONV_LANES
    nb = d_conf // cl
    nchunk = t // CONV_ROWS
    nsub = CONV_ROWS // 8

    def fold(p):
        r = p[0:8]
        for s in range(1, nsub):
            r = r + p[8 * s:8 * s + 8]
        return r

    def body(av_ref, ag_ref, cg_ref, v_ref, dac_ref, dc3_ref, wdw_ref, w3_ref,
             dav_ref, dag_ref, dcg_ref, dv_ref, dwdw_ref, dw3_ref, apad, cpad, dapad, dcpad):
        zeros = jnp.zeros((CONV_PAD, cl), F32)
        apad[0:CONV_PAD, :] = zeros
        cpad[0:CONV_PAD, :] = zeros
        apad[CONV_PAD:, :] = av_ref[...] * _sigmoid(ag_ref[...])
        cpad[CONV_PAD:, :] = cg_ref[...] * v_ref[...]
        dapad[0:t, :] = dac_ref[...]
        dcpad[0:t, :] = dc3_ref[...]
        dapad[t:, :] = zeros
        dcpad[t:, :] = zeros

        def chunk(ci, accs):
            base = pl.multiple_of(ci * CONV_ROWS, 8)
            rows = pl.ds(base, CONV_ROWS)
            da = jnp.zeros((CONV_ROWS, cl), F32)
            for k in range(CONF_KERNEL):
                da = da + dapad[pl.ds(base + (CONF_KERNEL - 1 - k), CONV_ROWS), :] * wdw_ref[k:k + 1, :]
            dcv = jnp.zeros((CONV_ROWS, cl), F32)
            for k in range(SHORT_KERNEL):
                dcv = dcv + dcpad[pl.ds(base + (SHORT_KERNEL - 1 - k), CONV_ROWS), :] * w3_ref[k:k + 1, :]
            av, sg = av_ref[rows, :], _sigmoid(ag_ref[rows, :])
            dav_ref[rows, :] = (da * sg).astype(dav_ref.dtype)
            dag_ref[rows, :] = (da * av * sg * (1.0 - sg)).astype(dag_ref.dtype)
            dcg_ref[rows, :] = (dcv * v_ref[rows, :]).astype(dcg_ref.dtype)
            dv_ref[rows, :] = (dcv * cg_ref[rows, :]).astype(dv_ref.dtype)
            d_out, d_out3 = dac_ref[rows, :], dc3_ref[rows, :]
            new = []
            for k in range(CONF_KERNEL):
                off = CONV_PAD - (CONF_KERNEL - 1) + k
                new.append(accs[k] + fold(d_out * apad[pl.ds(base + off, CONV_ROWS), :]))
            for k in range(SHORT_KERNEL):
                off = CONV_PAD - (SHORT_KERNEL - 1) + k
                new.append(accs[CONF_KERNEL + k] + fold(d_out3 * cpad[pl.ds(base + off, CONV_ROWS), :]))
            return tuple(new)

        init = tuple(jnp.zeros((8, cl), F32) for _ in range(CONF_KERNEL + SHORT_KERNEL))
        accs = lax.fori_loop(0, nchunk, chunk, init)
        for k in range(CONF_KERNEL):
            dwdw_ref[k:k + 1, :] = jnp.sum(accs[k], axis=0, keepdims=True)
        for k in range(SHORT_KERNEL):
            dw3_ref[k:k + 1, :] = jnp.sum(accs[CONF_KERNEL + k], axis=0, keepdims=True)

    def col(blk0):
        return pl.BlockSpec((t, cl), lambda j: (0, blk0 + j))

    own = pl.BlockSpec((t, cl), lambda j: (0, j))
    return pl.pallas_call(
        body,
        name="conv_bwd",
        grid=(nb,),
        in_specs=[col(0), col(nb), col(3 * nb), col(4 * nb), own, own,
                  pl.BlockSpec((CONF_KERNEL, cl), lambda j: (0, j)),
                  pl.BlockSpec((SHORT_KERNEL, cl), lambda j: (0, j))],
        out_specs=[own] * 4 + [pl.BlockSpec((CONF_KERNEL, cl), lambda j: (0, j)),
                               pl.BlockSpec((SHORT_KERNEL, cl), lambda j: (0, j))],
        out_shape=[jax.ShapeDtypeStruct((t, d_conf), MXU_DTYPE)] * 4
        + [jax.ShapeDtypeStruct((CONF_KERNEL, d_conf), F32), jax.ShapeDtypeStruct((SHORT_KERNEL, d_conf), F32)],
        scratch_shapes=[pltpu.VMEM((t + CONV_PAD, cl), F32)] * 4,
        compiler_params=pltpu.CompilerParams(dimension_semantics=("parallel",)),
    )(proj, proj, proj, proj, d_ac, d_c3, wdw, w3)


def _elementwise(name, fn, ins, out_dtypes, tr, deps=()):
    ins = [(a, ()) if not isinstance(a, tuple) else a for a in ins]
    r, c = ins[0][0].shape[-2:]
    assert r % tr == 0, (name, r, tr)
    n_in = len(ins)

    def body(*refs):
        tiles = fn(*[x[...] for x in refs[:n_in]])
        for o_ref, tile in zip(refs[n_in + len(deps):], tiles):
            o_ref[...] = tile.astype(o_ref.dtype)

    def spec(lead):
        return pl.BlockSpec((None,) * len(lead) + (tr, c), lambda i: (*lead, i, 0))

    res = pl.pallas_call(
        body,
        name=name,
        grid=(r // tr,),
        in_specs=[spec(lead) for _, lead in ins] + [ANY_SPEC] * len(deps),
        out_specs=[pl.BlockSpec((tr, c), lambda i: (i, 0))] * len(out_dtypes),
        out_shape=[jax.ShapeDtypeStruct((r, c), d) for d in out_dtypes],
        compiler_params=pltpu.CompilerParams(dimension_semantics=("parallel",)),
    )(*[a for a, _ in ins], *deps)
    return res


def _adamw_tiles(w, g, m, v):
    m = ADAM_B1 * m + (1.0 - ADAM_B1) * g
    v = ADAM_B2 * v + (1.0 - ADAM_B2) * jnp.square(g)
    m_hat = m / (1.0 - ADAM_B1 ** ADAM_STEP)
    v_hat = v / (1.0 - ADAM_B2 ** ADAM_STEP)
    delta = -ADAM_LR * (m_hat / (jnp.sqrt(v_hat) + ADAM_EPS) + ADAM_WD * w)
    return g, delta, m, v


def _adamw(name, w, g, m, v, tr, deps=()):
    shape = w.shape
    flat = [a.reshape(shape[-2:]) if a.ndim > 2 else a for a in (w, g, m, v)]
    res = _elementwise(name, _adamw_tiles, flat, [F32] * 4, tr, deps)
    return [a.reshape(shape) for a in res]


def _pair_sum(name, p, q, core, tr):
    n_p, _, hr, c = p.shape
    assert hr % tr == 0

    def body(core_ref, p_ref, q_ref, o_ref):
        o_ref[...] = (p_ref[...] + q_ref[...]).astype(o_ref.dtype)

    return pl.pallas_call(
        body,
        name=name,
        grid_spec=pltpu.PrefetchScalarGridSpec(
            num_scalar_prefetch=1,
            grid=(n_p, hr // tr),
            in_specs=[pl.BlockSpec((None, None, tr, c), lambda a, i, core_ref: (a, core_ref[0], i, 0)),
                      pl.BlockSpec((None, tr, c), lambda a, i, core_ref: (a, i, 0))],
            out_specs=pl.BlockSpec((None, tr, c), lambda a, i, core_ref: (a, i, 0)),
        ),
        out_shape=jax.ShapeDtypeStruct((n_p, hr, c), WIRE_DTYPE),
        compiler_params=pltpu.CompilerParams(dimension_semantics=("parallel", "parallel")),
    )(core, p, q)


def _into_slot(name, w, slots, slot, dtype, tr, deps=()):
    r, c = w.shape
    assert r % tr == 0

    def body(slot_ref, w_ref, *rest):
        o_ref = rest[len(deps)]
        o_ref[...] = w_ref[...].astype(o_ref.dtype)

    return pl.pallas_call(
        body,
        name=name,
        grid_spec=pltpu.PrefetchScalarGridSpec(
            num_scalar_prefetch=1,
            grid=(r // tr,),
            in_specs=[pl.BlockSpec((tr, c), lambda i, slot_ref: (i, 0))] + [ANY_SPEC] * len(deps),
            out_specs=pl.BlockSpec((None, tr, c), lambda i, slot_ref: (slot_ref[0], i, 0)),
        ),
        out_shape=jax.ShapeDtypeStruct((slots, r, c), dtype),
        compiler_params=pltpu.CompilerParams(dimension_semantics=("parallel",)),
    )(slot, w, *deps)


def _sum_pieces(name, own, rb, place, tr):
    n_p, hr, c = rb.shape
    assert hr % tr == 0

    def body(place_ref, own_ref, *refs):
        chip = place_ref[0]
        acc = None
        for k in range(n_p):
            tile = jnp.where(chip == k, own_ref[...], refs[k][...]).astype(F32)
            acc = tile if acc is None else acc + tile
        refs[n_p][...] = acc

    def landed(k):
        return pl.BlockSpec((None, tr, c), lambda i, place_ref: (jnp.where(place_ref[0] == k, (k + 1) % n_p, k), i, 0))

    return pl.pallas_call(
        body,
        name=name,
        grid_spec=pltpu.PrefetchScalarGridSpec(
            num_scalar_prefetch=1,
            grid=(hr // tr,),
            in_specs=[pl.BlockSpec((None, tr, c), lambda i, place_ref: (place_ref[0], i, 0))]
            + [landed(k) for k in range(n_p)],
            out_specs=pl.BlockSpec((None, tr, c), lambda i, place_ref: (place_ref[1], i, 0)),
        ),
        out_shape=jax.ShapeDtypeStruct((2, hr, c), F32),
        compiler_params=pltpu.CompilerParams(dimension_semantics=("parallel",)),
    )(place, own, *([rb] * n_p))


HBM_SPEC = pl.BlockSpec(memory_space=pl.ANY)


def _place():
    x, y, c = lax.axis_index("x"), lax.axis_index("y"), lax.axis_index("c")
    chips = [(1 - x, y), (x, 1 - y), (1 - x, 1 - y)]
    return x, y, c, chips


def _gather_shards(bufs, split):
    n = len(bufs)
    n_split = sum(split)
    fwd_slot = {a: s for s, a in enumerate([a for a in range(n) if split[a]])}

    def body(*refs):
        outs = refs[n:2 * n]
        send_sems, recv_sems, fsend_sems, frecv_sems = refs[2 * n:]
        x, y, c, chips = _place()
        me = 2 * x + y
        sibling = (x, y, 1 - c)

        def part(a, slot, h):
            if not split[a]:
                return outs[a].at[slot]
            hr = bufs[a].shape[1] // 2
            return outs[a].at[slot, pl.ds(h * hr, hr), :]

        sends = []
        for a in range(n):
            for j, chip in enumerate(chips):
                sends.append(pltpu.make_async_remote_copy(
                    src_ref=part(a, me, c), dst_ref=part(a, me, c),
                    send_sem=send_sems.at[3 * a + j], recv_sem=recv_sems.at[3 * a + j],
                    device_id=(*chip, c), device_id_type=MESH))
        for cp in sends:
            cp.start()
        passed = []
        for a in range(n):
            for j, chip in enumerate(chips):
                landed = part(a, 2 * chip[0] + chip[1], c)
                pltpu.make_async_remote_copy(
                    src_ref=landed, dst_ref=landed, send_sem=send_sems.at[3 * a + j], recv_sem=recv_sems.at[3 * a + j],
                    device_id=(*chip, c), device_id_type=MESH).wait_recv()
                if split[a]:
                    s = 3 * fwd_slot[a] + j
                    fwd = pltpu.make_async_remote_copy(
                        src_ref=landed, dst_ref=landed, send_sem=fsend_sems.at[s], recv_sem=frecv_sems.at[s],
                        device_id=sibling, device_id_type=MESH)
                    fwd.start()
                    passed.append(fwd)
        for a in range(n):
            if split[a]:
                for j, chip in enumerate(chips):
                    s = 3 * fwd_slot[a] + j
                    other = part(a, 2 * chip[0] + chip[1], 1 - c)
                    pltpu.make_async_remote_copy(
                        src_ref=other, dst_ref=other, send_sem=fsend_sems.at[s], recv_sem=frecv_sems.at[s],
                        device_id=sibling, device_id_type=MESH).wait_recv()
        for cp in sends + passed:
            cp.wait_send()

    return pl.pallas_call(
        body,
        name="gather_shards",
        in_specs=[HBM_SPEC] * n,
        out_specs=[HBM_SPEC] * n,
        out_shape=[jax.ShapeDtypeStruct(b.shape, b.dtype) for b in bufs],
        input_output_aliases={a: a for a in range(n)},
        scratch_shapes=[pltpu.SemaphoreType.DMA((3 * n,)), pltpu.SemaphoreType.DMA((3 * n,)),
                        pltpu.SemaphoreType.DMA((3 * n_split,)), pltpu.SemaphoreType.DMA((3 * n_split,))],
    )(*bufs)


def _swap_halves(name, parts):
    n = len(parts)
    n_p = parts[0].shape[0]

    def body(*refs):
        ins, outs = refs[:n], refs[n:2 * n]
        send_sems, recv_sems = refs[2 * n:]
        x, y, c, _ = _place()
        copies = []
        for a in range(n):
            for p in range(n_p):
                copies.append(pltpu.make_async_remote_copy(
                    src_ref=ins[a].at[p, 1 - c], dst_ref=outs[a].at[p],
                    send_sem=send_sems.at[n_p * a + p], recv_sem=recv_sems.at[n_p * a + p],
                    device_id=(x, y, 1 - c), device_id_type=MESH))
        for cp in copies:
            cp.start()
        for cp in copies:
            cp.wait()

    return pl.pallas_call(
        body,
        name=name,
        in_specs=[HBM_SPEC] * n,
        out_specs=[HBM_SPEC] * n,
        out_shape=[jax.ShapeDtypeStruct((n_p, *p.shape[2:]), p.dtype) for p in parts],
        scratch_shapes=[pltpu.SemaphoreType.DMA((n_p * n,)), pltpu.SemaphoreType.DMA((n_p * n,))],
    )(*parts)


def _exchange_pieces(sums):
    n = len(sums)

    def body(*refs):
        ins, outs = refs[:n], refs[n:2 * n]
        send_sems, recv_sems = refs[2 * n:]
        x, y, c, chips = _place()
        me = 2 * x + y
        sends = []
        for a in range(n):
            for j, chip in enumerate(chips):
                sends.append(pltpu.make_async_remote_copy(
                    src_ref=ins[a].at[2 * chip[0] + chip[1]], dst_ref=outs[a].at[me],
                    send_sem=send_sems.at[3 * a + j], recv_sem=recv_sems.at[3 * a + j],
                    device_id=(*chip, c), device_id_type=MESH))
        for cp in sends:
            cp.start()
        for a in range(n):
            for j, chip in enumerate(chips):
                slot = outs[a].at[2 * chip[0] + chip[1]]
                pltpu.make_async_remote_copy(
                    src_ref=slot, dst_ref=slot, send_sem=send_sems.at[3 * a + j], recv_sem=recv_sems.at[3 * a + j],
                    device_id=(*chip, c), device_id_type=MESH).wait_recv()
        for cp in sends:
            cp.wait_send()

    return pl.pallas_call(
        body,
        name="exchange_pieces",
        in_specs=[HBM_SPEC] * n,
        out_specs=[HBM_SPEC] * n,
        out_shape=[jax.ShapeDtypeStruct(s.shape, s.dtype) for s in sums],
        scratch_shapes=[pltpu.SemaphoreType.DMA((3 * n,)), pltpu.SemaphoreType.DMA((3 * n,))],
    )(*sums)


def _join_halves(name, bufs, deps=()):
    n = len(bufs)

    def body(*refs):
        outs = refs[n + len(deps):2 * n + len(deps)]
        send_sems, recv_sems = refs[2 * n + len(deps):]
        x, y, c, _ = _place()
        copies = [pltpu.make_async_remote_copy(
            src_ref=outs[a].at[c], dst_ref=outs[a].at[c], send_sem=send_sems.at[a], recv_sem=recv_sems.at[a],
            device_id=(x, y, 1 - c), device_id_type=MESH) for a in range(n)]
        for cp in copies:
            cp.start()
        for a in range(n):
            other = outs[a].at[1 - c]
            pltpu.make_async_remote_copy(
                src_ref=other, dst_ref=other, send_sem=send_sems.at[a], recv_sem=recv_sems.at[a],
                device_id=(x, y, 1 - c), device_id_type=MESH).wait_recv()
        for cp in copies:
            cp.wait_send()

    return pl.pallas_call(
        body,
        name=name,
        in_specs=[HBM_SPEC] * n + [ANY_SPEC] * len(deps),
        out_specs=[HBM_SPEC] * n,
        out_shape=[jax.ShapeDtypeStruct(b.shape, b.dtype) for b in bufs],
        input_output_aliases={a: a for a in range(n)},
        scratch_shapes=[pltpu.SemaphoreType.DMA((n,)), pltpu.SemaphoreType.DMA((n,))],
    )(*bufs, *deps)


HBM_ONLY = pl.BlockSpec(memory_space=pltpu.HBM)
SEM_SPEC = pl.BlockSpec(memory_space=pltpu.SEMAPHORE)
DATAFLOW = pltpu.SideEffectType.DATAFLOW_SIDE_EFFECTING


def _in_hbm(a):
    return pltpu.with_memory_space_constraint(a, pltpu.HBM)


def _shard_part(ref, is_split, slot, h):
    if not is_split:
        return ref.at[slot]
    hr = ref.shape[1] // 2
    return ref.at[slot, pl.ds(h * hr, hr), :]


TOKEN = jax.ShapeDtypeStruct((8, LANES), F32)
VMEM_SPEC = pl.BlockSpec(memory_space=pltpu.VMEM)


def _gather_start(name, bufs, split, groups, deps=()):
    n, ng = len(bufs), len(groups)

    def body(*refs):
        ins, sems = refs[:n], refs[n + len(deps):n + len(deps) + 2 * ng]
        refs[-1][...] = jnp.zeros(TOKEN.shape, TOKEN.dtype)
        x, y, c, chips = _place()
        me = 2 * x + y
        for g, members in enumerate(groups):
            for s, a in enumerate(members):
                mine = _shard_part(ins[a], split[a], me, c)
                for j, chip in enumerate(chips):
                    pltpu.make_async_remote_copy(
                        src_ref=mine, dst_ref=mine, send_sem=sems[2 * g].at[3 * s + j], recv_sem=sems[2 * g + 1].at[3 * s + j],
                        device_id=(*chip, c), device_id_type=MESH).start()

    res = pl.pallas_call(
        body,
        name=name,
        in_specs=[HBM_ONLY] * n + [ANY_SPEC] * len(deps),
        out_specs=[SEM_SPEC] * (2 * ng) + [HBM_ONLY] * n + [VMEM_SPEC],
        out_shape=[pltpu.SemaphoreType.DMA((3 * len(members),)) for members in groups for _ in range(2)]
        + [pltpu.HBM(b.shape, b.dtype) for b in bufs] + [TOKEN],
        input_output_aliases={a: 2 * ng + a for a in range(n)},
        compiler_params=pltpu.CompilerParams(has_side_effects=DATAFLOW),
    )(*[_in_hbm(b) for b in bufs], *deps)
    return [(res[2 * g], res[2 * g + 1]) for g in range(ng)], list(res[2 * ng:2 * ng + n]), res[-1]


def _gather_wait(name, bufs, split, sems, after):
    n = len(bufs)

    def body(*refs):
        ins, send_sems, recv_sems = refs[:n], refs[n], refs[n + 1]
        x, y, c, chips = _place()
        me = 2 * x + y
        for s in range(n):
            for j, chip in enumerate(chips):
                copy = pltpu.make_async_remote_copy(
                    src_ref=_shard_part(ins[s], split[s], me, c),
                    dst_ref=_shard_part(ins[s], split[s], 2 * chip[0] + chip[1], c),
                    send_sem=send_sems.at[3 * s + j], recv_sem=recv_sems.at[3 * s + j],
                    device_id=(*chip, c), device_id_type=MESH)
                copy.wait_send()
                copy.wait_recv()

    res = pl.pallas_call(
        body,
        name=name,
        in_specs=[HBM_ONLY] * n + [SEM_SPEC, SEM_SPEC] + [ANY_SPEC] * len(after),
        out_specs=[HBM_ONLY] * n,
        out_shape=[pltpu.HBM(b.shape, b.dtype) for b in bufs],
        input_output_aliases={a: a for a in range(n)},
        compiler_params=pltpu.CompilerParams(has_side_effects=DATAFLOW),
    )(*bufs, *sems, *after)
    return list(res)


def _pass_halves(name, bufs):
    n = len(bufs)

    def body(*refs):
        outs = refs[n:2 * n]
        send_sems, recv_sems = refs[2 * n:]
        x, y, c, chips = _place()
        sibling = (x, y, 1 - c)

        def copy(a, j, h):
            blk = _shard_part(outs[a], True, 2 * chips[j][0] + chips[j][1], h)
            return pltpu.make_async_remote_copy(
                src_ref=blk, dst_ref=blk, send_sem=send_sems.at[3 * a + j], recv_sem=recv_sems.at[3 * a + j],
                device_id=sibling, device_id_type=MESH)

        sends = [copy(a, j, c) for a in range(n) for j in range(3)]
        for cp in sends:
            cp.start()
        for a in range(n):
            for j in range(3):
                copy(a, j, 1 - c).wait_recv()
        for cp in sends:
            cp.wait_send()

    res = pl.pallas_call(
        body,
        name=name,
        in_specs=[HBM_SPEC] * n,
        out_specs=[HBM_SPEC] * n,
        out_shape=[jax.ShapeDtypeStruct(b.shape, b.dtype) for b in bufs],
        input_output_aliases={a: a for a in range(n)},
        scratch_shapes=[pltpu.SemaphoreType.DMA((3 * n,)), pltpu.SemaphoreType.DMA((3 * n,))],
    )(*bufs)
    return list(res)


def _exchange_start(name, pair):
    def body(pair_ref, land_ref, send_sems, recv_sems, pair_thru, land_thru, token):
        x, y, c, chips = _place()
        me = 2 * x + y
        for j, chip in enumerate(chips):
            pltpu.make_async_remote_copy(
                src_ref=pair_ref.at[2 * chip[0] + chip[1]], dst_ref=land_ref.at[me],
                send_sem=send_sems.at[j], recv_sem=recv_sems.at[j], device_id=(*chip, c), device_id_type=MESH).start()
        token[...] = jnp.zeros(TOKEN.shape, TOKEN.dtype)

    send_sems, recv_sems, pair_thru, land_thru, token = pl.pallas_call(
        body,
        name=name,
        in_specs=[HBM_ONLY, HBM_ONLY],
        out_specs=[SEM_SPEC, SEM_SPEC, HBM_ONLY, HBM_ONLY, VMEM_SPEC],
        out_shape=[pltpu.SemaphoreType.DMA((3,)), pltpu.SemaphoreType.DMA((3,)),
                   pltpu.HBM(pair.shape, pair.dtype), pltpu.HBM(pair.shape, pair.dtype), TOKEN],
        input_output_aliases={0: 2, 1: 3},
        compiler_params=pltpu.CompilerParams(has_side_effects=DATAFLOW),
    )(_in_hbm(pair), _in_hbm(lax.empty(pair.shape, pair.dtype)))
    return (send_sems, recv_sems), pair_thru, land_thru, token


def _swap_start(name, halves):
    n_p, _, hr, cols = halves.shape
    land_shape = (n_p, hr, cols)

    def body(halves_ref, land_ref, send_sems, recv_sems, halves_thru, land_thru, token):
        x, y, c, _ = _place()
        for p in range(n_p):
            pltpu.make_async_remote_copy(
                src_ref=halves_ref.at[p, 1 - c], dst_ref=land_ref.at[p], send_sem=send_sems.at[p], recv_sem=recv_sems.at[p],
                device_id=(x, y, 1 - c), device_id_type=MESH).start()
        token[...] = jnp.zeros(TOKEN.shape, TOKEN.dtype)

    send_sems, recv_sems, halves_thru, land_thru, token = pl.pallas_call(
        body,
        name=name,
        in_specs=[HBM_ONLY, HBM_ONLY],
        out_specs=[SEM_SPEC, SEM_SPEC, HBM_ONLY, HBM_ONLY, VMEM_SPEC],
        out_shape=[pltpu.SemaphoreType.DMA((n_p,)), pltpu.SemaphoreType.DMA((n_p,)),
                   pltpu.HBM(halves.shape, halves.dtype), pltpu.HBM(land_shape, halves.dtype), TOKEN],
        input_output_aliases={0: 2, 1: 3},
        compiler_params=pltpu.CompilerParams(has_side_effects=DATAFLOW),
    )(_in_hbm(halves), _in_hbm(lax.empty(land_shape, halves.dtype)))
    return (send_sems, recv_sems), halves_thru, land_thru, token


def _swap_wait(name, halves, land, sems, after):
    n_p = halves.shape[0]

    def body(halves_ref, land_ref, send_sems, recv_sems, *rest):
        x, y, c, _ = _place()
        for p in range(n_p):
            copy = pltpu.make_async_remote_copy(
                src_ref=halves_ref.at[p, 1 - c], dst_ref=land_ref.at[p], send_sem=send_sems.at[p], recv_sem=recv_sems.at[p],
                device_id=(x, y, 1 - c), device_id_type=MESH)
            copy.wait_send()
            copy.wait_recv()

    return pl.pallas_call(
        body,
        name=name,
        in_specs=[HBM_ONLY, HBM_ONLY, SEM_SPEC, SEM_SPEC] + [ANY_SPEC] * len(after),
        out_specs=[HBM_ONLY, HBM_ONLY],
        out_shape=[pltpu.HBM(halves.shape, halves.dtype), pltpu.HBM(land.shape, land.dtype)],
        input_output_aliases={0: 0, 1: 1},
        compiler_params=pltpu.CompilerParams(has_side_effects=DATAFLOW),
    )(halves, land, *sems, *after)


def _exchange_wait(name, pair, land, sems, after):
    def body(pair_ref, land_ref, send_sems, recv_sems, *rest):
        x, y, c, chips = _place()
        for j, chip in enumerate(chips):
            k = 2 * chip[0] + chip[1]
            copy = pltpu.make_async_remote_copy(
                src_ref=pair_ref.at[k], dst_ref=land_ref.at[k], send_sem=send_sems.at[j], recv_sem=recv_sems.at[j],
                device_id=(*chip, c), device_id_type=MESH)
            copy.wait_send()
            copy.wait_recv()

    return pl.pallas_call(
        body,
        name=name,
        in_specs=[HBM_ONLY, HBM_ONLY, SEM_SPEC, SEM_SPEC] + [HBM_SPEC] * len(after),
        out_specs=[HBM_ONLY, HBM_ONLY],
        out_shape=[pltpu.HBM(pair.shape, pair.dtype), pltpu.HBM(land.shape, land.dtype)],
        input_output_aliases={0: 0, 1: 1},
        compiler_params=pltpu.CompilerParams(has_side_effects=DATAFLOW),
    )(pair, land, *sems, *after)


def _allgather_small(name, block, deps=()):
    m_per, n = block.shape

    def body(x_ref, *rest):
        out_ref, send_sems, recv_sems, local_sem = rest[len(deps):]
        x, y, c, chips = _place()
        me, sibling = (x, y, c), (x, y, 1 - c)

        def rows(px, py, pc):
            return out_ref.at[pl.ds((4 * px + 2 * py + pc) * m_per, m_per), :]

        def copy(k, blk, to, src=None):
            return pltpu.make_async_remote_copy(
                src_ref=rows(*blk) if src is None else src, dst_ref=rows(*blk),
                send_sem=send_sems.at[k], recv_sem=recv_sems.at[k], device_id=to, device_id_type=MESH)

        mine = pltpu.make_async_copy(x_ref, rows(*me), local_sem)
        mine.start()
        first = [copy(0, me, sibling, src=x_ref)]
        first += [copy(1 + j, me, (*chip, c), src=x_ref) for j, chip in enumerate(chips)]
        for cp in first:
            cp.start()
        passed = [copy(4 + j, (*chip, c), sibling) for j, chip in enumerate(chips)]
        for j, chip in enumerate(chips):
            copy(1 + j, (*chip, c), me).wait_recv()
            passed[j].start()
        copy(0, sibling, me).wait_recv()
        for j, chip in enumerate(chips):
            copy(4 + j, (*chip, 1 - c), me).wait_recv()
        for cp in first + passed:
            cp.wait_send()
        mine.wait()

    return pl.pallas_call(
        body,
        name=name,
        out_shape=jax.ShapeDtypeStruct((8 * m_per, n), block.dtype),
        in_specs=[pl.BlockSpec(memory_space=pltpu.VMEM)] + [ANY_SPEC] * len(deps),
        out_specs=pl.BlockSpec(memory_space=pltpu.VMEM),
        scratch_shapes=[pltpu.SemaphoreType.DMA((7,)), pltpu.SemaphoreType.DMA((7,)), pltpu.SemaphoreType.DMA],
    )(block, *deps)


def _sum_blocks(name, gathered, n_blocks):
    r = gathered.shape[0] // n_blocks
    c = gathered.shape[1]

    def body(g_ref, o_ref):
        acc = g_ref[0:r, :]
        for b in range(1, n_blocks):
            acc = acc + g_ref[b * r:(b + 1) * r, :]
        o_ref[...] = acc

    return pl.pallas_call(body, name=name, out_shape=jax.ShapeDtypeStruct((r, c), F32))(gathered)


def _largest_tile(n, cap, mult):
    best = None
    for d in range(mult, min(n, cap) + 1, mult):
        if n % d == 0:
            best = d
    assert best is not None, (n, cap, mult)
    return best


def kernel(x, meta, g_pre_mix, w_in, b_gates, conf_dw_w, conf_dw_b, conf_ln_g, conf_ln_b, conf_w_pw, short_dw_w, short_w_out, w_o, g_post_mix, g_pre_mlp, w_up, w_down, g_post_mlp, loss_target, m_meta, m_g_pre_mix, m_w_in, m_b_gates, m_conf_dw_w, m_conf_dw_b, m_conf_ln_g, m_conf_ln_b, m_conf_w_pw, m_short_dw_w, m_short_w_out, m_w_o, m_g_post_mix, m_g_pre_mlp, m_w_up, m_w_down, m_g_post_mlp, v_meta, v_g_pre_mix, v_w_in, v_b_gates, v_conf_dw_w, v_conf_dw_b, v_conf_ln_g, v_conf_ln_b, v_conf_w_pw, v_short_dw_w, v_short_w_out, v_w_o, v_g_post_mix, v_g_pre_mlp, v_w_up, v_w_down, v_g_post_mlp):
    seq, d = x.shape[1], x.shape[2]
    t_real = seq + N_META
    t = -(-t_real // LANES) * LANES
    d_conf = conf_dw_b.shape[1]
    d_ff = w_up.shape[2] * N_CHIPS
    in_cols = w_in.shape[2] * N_CHIPS
    assert in_cols == 5 * d_conf + 2 * d and d == 2 * d_conf
    cw = d_conf
    core = lax.axis_index("c")
    chip = 2 * lax.axis_index("x") + lax.axis_index("y")

    tr = _largest_tile(t, 272, ROW_CHUNK)
    tm = t
    tn_in = _largest_tile(in_cols // N_CHIPS, 768, LANES)
    tn_d = _largest_tile(d, 1024, LANES)
    tn_h = _largest_tile(d, 512, LANES)
    tn_ff = _largest_tile(d_ff // N_CHIPS, 1024, LANES)
    tn_pw = _largest_tile(d // N_CHIPS, 512, LANES)

    big = [w_in[0], conf_w_pw[0], short_w_out[0], w_o[0], w_up[0], w_down[0]]
    chip_arr = chip.astype(jnp.int32).reshape(1)

    def cast(a, deps):
        return _into_slot(f"cast_w{a}", big[a], N_CHIPS, chip_arr, MXU_DTYPE, _largest_tile(big[a].shape[0], 256, 16), deps)

    small = [_into_slot(f"place_w{a}", w, N_CHIPS, chip_arr, F32, w.shape[0])
             for a, w in enumerate([meta, conf_dw_w[0], short_dw_w[0]])]
    sems_small, fly_small, tok_small = _gather_start("gather_start_small", small, [False] * 3, [[0, 1, 2]])
    sems_in, fly_in, tok_in = _gather_start("gather_start_in", [cast(0, [tok_small])], [True], [[0]])
    rest_groups = [[0, 1], [2], [3], [4]]
    sems_rest, fly_rest, tok_rest = _gather_start(
        "gather_start_rest", [cast(a, [tok_in]) for a in range(1, 6)], [True] * 5, rest_groups)

    def arrive(g, after):
        members = rest_groups[g]
        got = _gather_wait(f"gather_wait_rest{g}", [fly_rest[a] for a in members], [True] * len(members),
                           sems_rest[g], after)
        return _pass_halves(f"gather_pass_rest{g}", got)

    meta_g, wdw_g, w3_g = _gather_wait("gather_wait_small", fly_small, [False] * 3, sems_small[0], [tok_in])
    meta_full = jnp.transpose(meta_g, (1, 0, 2)).reshape(N_META, d)
    wdw = jnp.transpose(wdw_g, (1, 0, 2)).reshape(CONF_KERNEL, d_conf)
    w3 = jnp.transpose(w3_g, (1, 0, 2)).reshape(SHORT_KERNEL, d_conf)

    tail = jnp.zeros((t - t_real, d), F32)
    h0 = jnp.concatenate([meta_full, x[0], tail], axis=0)
    target = jnp.concatenate([jnp.zeros((N_META, d), F32), loss_target[0], tail], axis=0)

    def norm_in(i, rows, vecs):
        return [_rms_fwd(rows[0], vecs[0])], []

    (n_lp,) = _rowwise("norm_in", norm_in, [(h0, d, 0)], [g_pre_mix], [(d, MXU_DTYPE)], [], tr, deps=[tok_rest])
    (wg_in,) = _pass_halves("gather_pass_in", _gather_wait("gather_wait_in", fly_in, [True], sems_in[0], [n_lp]))
    proj =_matmul("proj", n_lp, wg_in, kind="nn", tm=tm, tn=tn_in, tk=d, out_dtypes=[F32])
    ac, c3 = _conv_fwd(proj, wdw, conf_dw_b, w3, d_conf)

    def ln_parts(ac_t, ln_g, ln_b):
        mu = jnp.mean(ac_t, axis=-1, keepdims=True)
        xc = ac_t - mu
        rstd = lax.rsqrt(jnp.mean(xc * xc, axis=-1, keepdims=True) + LN_EPS)
        xh = xc * rstd
        return xh, rstd, xh * ln_g + ln_b

    def branch_act(i, rows, vecs):
        ac_t, c3_t, bg_t = rows
        _, _, al = ln_parts(ac_t, vecs[0], vecs[1])
        return [al * _sigmoid(al), bg_t * c3_t], []

    a_act, s_lp = _rowwise("branch_act", branch_act, [(ac, cw, 0), (c3, cw, 0), (proj, cw, 2)],
                           [conf_ln_g, conf_ln_b], [(d_conf, MXU_DTYPE), (d_conf, MXU_DTYPE)], [], tr)
    wg_pw, wg_sout = arrive(0, [a_act])
    y_a = _matmul("y_a", a_act, wg_pw, kind="nn", tm=tm, tn=tn_pw, tk=d_conf, out_dtypes=[F32])
    y_b = _matmul("y_b", s_lp, wg_sout, kind="nn", tm=tm, tn=tn_pw, tk=d_conf, out_dtypes=[F32])

    gate_rows = [(proj, cw, 5), (proj, cw, 6), (proj, cw, 7), (proj, cw, 8)]

    def gates_of(rows, b):
        ga = _sigmoid(jnp.concatenate([rows[0], rows[1]], axis=1) + b[:, :d])
        gb = _sigmoid(jnp.concatenate([rows[2], rows[3]], axis=1) + b[:, d:])
        return ga, gb

    def gate(i, rows, vecs):
        ga, gb = gates_of(rows[2:], vecs[0])
        return [ga * rows[0] + gb * rows[1]], []

    (m_lp,) = _rowwise("gate", gate, [(y_a, d, 0), (y_b, d, 0)] + gate_rows, [b_gates], [(d, MXU_DTYPE)], [], tr)
    wg_o = arrive(1, [m_lp])[0].reshape(d, d)
    mix = _matmul("mix", m_lp, wg_o, kind="nn", tm=tm, tn=tn_d, tk=d, out_dtypes=[F32])

    def post_mix(i, rows, vecs):
        h1_t = rows[0] + _rms_fwd(rows[1], vecs[0])
        return [h1_t, _rms_fwd(h1_t, vecs[1])], []

    h1, n2_lp = _rowwise("post_mix", post_mix, [(h0, d, 0), (mix, d, 0)], [g_post_mix, g_pre_mlp],
                         [(d, F32), (d, MXU_DTYPE)], [], tr)
    (wg_up,) = arrive(2, [n2_lp])
    up, f_lp = _matmul("up", n2_lp, wg_up, kind="nn", tm=tm, tn=tn_h, tk=d, out_dtypes=[F32, MXU_DTYPE],
                       epilogue=lambda acc: (acc, jnp.square(jnp.maximum(acc, 0.0))))
    wg_down = arrive(3, [f_lp])[0].reshape(d_ff, d)
    dn = _matmul("down", f_lp, wg_down, kind="nn", tm=tm, tn=tn_h, tk=_largest_tile(d_ff, 2048, LANES),
                 out_dtypes=[F32])

    def head(i, rows, vecs):
        h1_t, dn_t, tgt = rows
        y = h1_t + _rms_fwd(dn_t, vecs[0])
        row = i + lax.broadcasted_iota(jnp.int32, (ROW_CHUNK, 1), 0)
        err = jnp.where(jnp.logical_and(row >= N_META, row < t_real), y - tgt, 0.0)
        dy = err / d
        d_dn, dg = _rms_bwd(dn_t, vecs[0], dy)
        loss_rows = 0.5 * jnp.mean(err * err, axis=-1, keepdims=True)
        return [dy, d_dn], [dg, jnp.broadcast_to(loss_rows, (ROW_CHUNK, LANES))]

    dy, d_dn, dg_post_mlp, loss_vec = _rowwise(
        "head", head, [(h1, d, 0), (dn, d, 0), (target, d, 0)], [g_post_mlp], [(d, F32), (d, MXU_DTYPE)], [d, LANES], tr)
    loss = lax.psum(loss_vec[0, 0], ("x", "y", "c"))

    core_arr = core.astype(jnp.int32).reshape(1)
    place = jnp.stack([chip, core]).astype(jnp.int32)
    in_flight = {}


    def swap_start(a, gw):
        halves = gw.reshape(N_CHIPS, 2, gw.shape[1] // 2, gw.shape[2])
        *in_flight[a], token = _swap_start(f"swap_start{a}", halves)
        return token

    def exchange_start(a, after):
        sems, halves, land = in_flight[a]
        halves, land = _swap_wait(f"swap_wait{a}", halves, land, sems, after)
        pair = _pair_sum(f"pair_sum{a}", halves, land, core_arr, _largest_tile(land.shape[1], 256, 16))
        *in_flight[a], token = _exchange_start(f"exchange_start{a}", pair)
        return token

    def reduce_finish(a, after):
        sems, pair, land = in_flight[a]
        pair, land = _exchange_wait(f"exchange_wait{a}", pair, land, sems, after)
        return _sum_pieces(f"sum_pieces{a}", pair, land, place, _largest_tile(land.shape[1], 256, 16))

    d_up = _matmul("d_up", d_dn, wg_down, kind="nt", tm=tm, tn=tn_h, tk=d, out_dtypes=[MXU_DTYPE], extras=[up],
                   epilogue=lambda acc, up_t: (acc * (2.0 * jnp.maximum(up_t, 0.0)),))
    tk_t = t
    gw_down = _matmul("gw_down", f_lp, d_dn, kind="tn", tm=_largest_tile(d_ff, 2048, LANES), tn=tn_d, tk=tk_t,
                      out_dtypes=[F32])
    tok = swap_start(5, gw_down.reshape(N_CHIPS, d_ff // N_CHIPS, d))
    d_n2 = _matmul("d_n2", d_up, wg_up, kind="nt", tm=tm, tn=tn_h, tk=d_ff // N_CHIPS, out_dtypes=[F32], deps=[tok])
    tok = exchange_start(5, [d_n2])
    gw_up = _matmul("gw_up", n2_lp, d_up, kind="tn", tm=_largest_tile(d, 2048, LANES), tn=tn_ff, tk=tk_t,
                    out_dtypes=[F32], out_pieces=N_CHIPS, deps=[tok])
    tok = swap_start(4, gw_up)

    def bwd_mid(i, rows, vecs):
        dy_t, dn2_t, h1_t, mix_t = rows
        d_h1a, dg_pre_mlp = _rms_bwd(h1_t, vecs[1], dn2_t)
        d_h1 = dy_t + d_h1a
        d_mix, dg_post_mix = _rms_bwd(mix_t, vecs[0], d_h1)
        return [d_h1, d_mix], [dg_pre_mlp, dg_post_mix]

    d_h1, d_mix, dg_pre_mlp, dg_post_mix = _rowwise(
        "bwd_mid", bwd_mid, [(dy, d, 0), (d_n2, d, 0), (h1, d, 0), (mix, d, 0)], [g_post_mix, g_pre_mlp],
        [(d, F32), (d, MXU_DTYPE)], [d, d], tr, deps=[tok])
    d_m = _matmul("d_m", d_mix, wg_o, kind="nt", tm=tm, tn=tn_d, tk=d, out_dtypes=[F32])
    tok = exchange_start(4, [d_m])
    gw_o = _matmul("gw_o", m_lp, d_mix, kind="tn", tm=_largest_tile(d, 2048, LANES), tn=tn_d, tk=tk_t, out_dtypes=[F32],
                   deps=[tok])
    tok = swap_start(3, gw_o.reshape(N_CHIPS, d // N_CHIPS, d))

    def gate_bwd(i, rows, vecs):
        dm_t, ya_t, yb_t = rows[:3]
        ga, gb = gates_of(rows[3:], vecs[0])
        d_gpre = jnp.concatenate([dm_t * ya_t * ga * (1.0 - ga), dm_t * yb_t * gb * (1.0 - gb)], axis=1)
        return [dm_t * ga, dm_t * gb, d_gpre], [d_gpre]

    d_ya, d_yb, d_gpre, dg_b_gates = _rowwise(
        "gate_bwd", gate_bwd, [(d_m, d, 0), (y_a, d, 0), (y_b, d, 0)] + gate_rows, [b_gates],
        [(d, MXU_DTYPE), (d, MXU_DTYPE), (2 * d, MXU_DTYPE)], [2 * d], tr, deps=[tok])
    tok = exchange_start(3, [d_ya])
    d_aact = _matmul("d_aact", d_ya, wg_pw, kind="nt", tm=tm, tn=d_conf, tk=tn_pw, out_dtypes=[F32], deps=[tok])
    gw_pw = _matmul("gw_pw", a_act, d_ya, kind="tn", tm=_largest_tile(d_conf, 2048, LANES), tn=tn_pw, tk=tk_t,
                    out_dtypes=[F32], out_pieces=N_CHIPS)
    tok = swap_start(1, gw_pw)
    d_s = _matmul("d_s", d_yb, wg_sout, kind="nt", tm=tm, tn=d_conf, tk=tn_pw, out_dtypes=[F32], deps=[tok])
    tok = exchange_start(1, [d_s])
    gw_sout = _matmul("gw_sout", s_lp, d_yb, kind="tn", tm=_largest_tile(d_conf, 2048, LANES), tn=tn_pw, tk=tk_t,
                      out_dtypes=[F32], out_pieces=N_CHIPS, deps=[tok])
    tok = swap_start(2, gw_sout)

    def branch_bwd(i, rows, vecs):
        daact_t, ds_t, ac_t, c3_t, bg_t = rows
        xh, rstd, al = ln_parts(ac_t, vecs[0], vecs[1])
        sg = _sigmoid(al)
        d_al = daact_t * (sg * (1.0 + al * (1.0 - sg)))
        dxh = d_al * vecs[0]
        d_ac = rstd * (dxh - jnp.mean(dxh, axis=-1, keepdims=True) - xh * jnp.mean(dxh * xh, axis=-1, keepdims=True))
        return [d_ac, ds_t * bg_t, ds_t * c3_t], [d_al * xh, d_al, d_ac]

    d_ac, d_c3, d_bg, dg_ln_g, dg_ln_b, dg_dw_b = _rowwise(
        "branch_bwd", branch_bwd, [(d_aact, cw, 0), (d_s, cw, 0), (ac, cw, 0), (c3, cw, 0), (proj, cw, 2)],
        [conf_ln_g, conf_ln_b], [(d_conf, F32), (d_conf, F32), (d_conf, MXU_DTYPE)], [d_conf] * 3, tr, deps=[tok])
    tok = exchange_start(2, [d_ac])
    d_av, d_ag, d_cg, d_v, dg_wdw, dg_w3 = _conv_bwd(proj, d_ac, d_c3, wdw, w3, d_conf)
    d_proj = jnp.concatenate([d_av, d_ag, d_bg, d_cg, d_v, d_gpre], axis=1)
    small_w = d_conf

    def pack(arrs):
        flat = jnp.concatenate([a.reshape(-1, small_w) for a in arrs], axis=0)
        return jnp.pad(flat, ((0, -flat.shape[0] % 8), (0, 0)))

    def unpack(buf, like):
        out, r0 = [], 0
        for a in like:
            nr = a.size // small_w
            out.append(buf[r0:r0 + nr].reshape(a.shape))
            r0 += nr
        return out

    def reduce_small(tag, arrs, deps):
        assert all(a.size % small_w == 0 for a in arrs)
        gathered = _allgather_small(f"allgather_small_{tag}", pack(arrs), deps)
        return unpack(_sum_blocks(f"sum_small_{tag}", gathered, 8), arrs)

    rep_w = [b_gates, conf_dw_b, conf_ln_g, conf_ln_b, g_post_mix, g_pre_mlp, g_post_mlp]
    rep_m = [m_b_gates, m_conf_dw_b, m_conf_ln_g, m_conf_ln_b, m_g_post_mix, m_g_pre_mlp, m_g_post_mlp]
    rep_v = [v_b_gates, v_conf_dw_b, v_conf_ln_g, v_conf_ln_b, v_g_post_mix, v_g_pre_mlp, v_g_post_mlp]
    *g_rep, g_wdw_full, g_w3_full = reduce_small(
        "mid", [dg_b_gates, dg_dw_b, dg_ln_g, dg_ln_b, dg_post_mix, dg_pre_mlp, dg_post_mlp, dg_wdw, dg_w3], [tok])
    sc = d_conf // N_CHIPS
    g_wdw = lax.dynamic_slice_in_dim(g_wdw_full, chip * sc, sc, axis=1)
    g_w3 = lax.dynamic_slice_in_dim(g_w3_full, chip * sc, sc, axis=1)

    gw_in = _matmul("gw_in", n_lp, d_proj, kind="tn", tm=_largest_tile(d, 2048, LANES), tn=tn_in, tk=tk_t,
                    out_dtypes=[F32], out_pieces=N_CHIPS, deps=[g_w3_full])
    tok = swap_start(0, gw_in)
    d_n = _matmul("d_n", d_proj, wg_in, kind="nt", tm=tm, tn=tn_h, tk=in_cols // N_CHIPS, out_dtypes=[F32], deps=[tok])
    tok = exchange_start(0, [d_n])

    def bwd_in(i, rows, vecs):
        d_h0a, dg = _rms_bwd(rows[2], vecs[0], rows[1])
        return [rows[0] + d_h0a], [dg]

    d_h0, dg_pre_mix = _rowwise("bwd_in", bwd_in, [(d_h1, d, 0), (d_n, d, 0), (h0, d, 0)], [g_pre_mix],
                                [(d, F32)], [d], tr, deps=[tok])
    grad_x = d_h0[N_META:t_real][None]

    big_m = [m_w_in, m_conf_w_pw, m_short_w_out, m_w_o, m_w_up, m_w_down]
    big_v = [v_w_in, v_conf_w_pw, v_short_w_out, v_w_o, v_w_up, v_w_down]
    big_res = {}

    def reduce_group(members, after):
        reduced = []
        for a in members:
            reduced.append(reduce_finish(a, after))
            after = [reduced[-1]]
        return reduced

    def adam_group(members, joined):
        for a, j in zip(members, joined):
            big_res[a] = _adamw(f"adamw_big{a}", big[a], j.reshape(big[a].shape), big_m[a][0], big_v[a][0],
                                _largest_tile(big[a].shape[0], 256, 8))

    early = [5, 4, 3, 1, 2]
    adam_group(early, _join_halves("join_halves_early", reduce_group(early, [d_h0])))
    rep_pack = [pack(arrs) for arrs in (rep_w, g_rep, rep_m, rep_v)]
    rep_res = [unpack(buf, rep_w) for buf in _adamw("adamw_rep", *rep_pack, rep_pack[0].shape[0])]
    col_res = {a: _adamw(f"adamw_col{a}", w, g, m, v, w.shape[0]) for a, (w, g, m, v) in (
        (1, (conf_dw_w[0], g_wdw, m_conf_dw_w[0], v_conf_dw_w[0])),
        (2, (short_dw_w[0], g_w3, m_short_dw_w[0], v_short_dw_w[0])))}
    reduced_late = reduce_group([0], [rep_res[1][0], col_res[1][1], col_res[2][1]] + [big_res[a][1] for a in early])
    adam_group([0], _join_halves("join_halves_late", reduced_late))
    g_pre_mix_full, g_meta_full = reduce_small("late", [dg_pre_mix, d_h0[:N_META]], [big_res[0][1]])
    g_meta = lax.dynamic_slice_in_dim(g_meta_full, chip * (d // N_CHIPS), d // N_CHIPS, axis=1)
    col_res[0] = _adamw("adamw_col0", meta, g_meta, m_meta, v_meta, meta.shape[0])
    pre_mix_res = _adamw("adamw_pre_mix", g_pre_mix, g_pre_mix_full, m_g_pre_mix, v_g_pre_mix, 1)
    rep_res = [[pre_mix_res[q]] + rep_res[q] for q in range(4)]

    def leaf(q):
        r = lambda a: rep_res[q][a]
        b = lambda a: big_res[a][q][None]
        return [col_res[0][q], r(0), b(0), r(1), col_res[1][q][None], r(2), r(3), r(4), b(1), col_res[2][q][None], b(2),
                b(3), r(5), r(6), b(4), b(5), r(7)]

    return (loss, grad_x, *leaf(0), *leaf(1), *leaf(2), *leaf(3))
```

```python
import jax
import jax.numpy as jnp
from jax import lax
from jax.experimental import pallas as pl
from jax.experimental.pallas import tpu as pltpu

F32 = jnp.float32
BF16 = jnp.bfloat16
MXU_DTYPE = BF16
WIRE_DTYPE = BF16

N_META = 16
CONF_KERNEL = 31
SHORT_KERNEL = 3
CONV_PAD = 32
RMS_EPS = 1e-6
LN_EPS = 1e-5
ADAM_LR = 0.001
ADAM_B1 = 0.9
ADAM_B2 = 0.999
ADAM_EPS = 1e-08
ADAM_WD = 0.01
ADAM_STEP = 10

N_CHIPS = 4
MESH = pl.DeviceIdType.MESH
LANES = 128


def _sigmoid(z):
    return 1.0 / (1.0 + jnp.exp(-z))


ANY_SPEC = pl.BlockSpec(memory_space=pl.ANY)


def _matmul(name, a, b, *, kind, tm, tn, tk, out_dtypes, out_pieces=1, epilogue=None, extras=(), deps=()):
    pieces = b.shape[0] if b.ndim == 3 else 1
    if kind == "nn":
        m, kdim = a.shape
        n = b.shape[-1] * pieces
        dims = (((1,), (0,)), ((), ()))
        a_spec = pl.BlockSpec((tm, tk), lambda i, j, k: (i, k))
        if b.ndim == 2:
            b_spec = pl.BlockSpec((tk, tn), lambda i, j, k: (k, j))
        else:
            npp = b.shape[-1] // tn
            b_spec = pl.BlockSpec((None, tk, tn), lambda i, j, k: (j // npp, k, j % npp))
    elif kind == "nt":
        m, kdim = a.shape
        n = b.shape[-2]
        dims = (((1,), (1,)), ((), ()))
        a_spec = pl.BlockSpec((tm, tk), lambda i, j, k: (i, k))
        if b.ndim == 2:
            b_spec = pl.BlockSpec((tn, tk), lambda i, j, k: (j, k))
        else:
            kpp = b.shape[-1] // tk
            b_spec = pl.BlockSpec((None, tn, tk), lambda i, j, k: (k // kpp, j, k % kpp))
    else:
        kdim, m = a.shape
        n = b.shape[-1]
        dims = (((0,), (0,)), ((), ()))
        a_spec = pl.BlockSpec((tk, tm), lambda i, j, k: (k, i))
        b_spec = pl.BlockSpec((tk, tn), lambda i, j, k: (k, j))
    assert m % tm == 0 and n % tn == 0 and kdim % tk == 0, (name, m, n, kdim, tm, tn, tk)
    nk = kdim // tk
    if out_pieces == 1:
        out_shape = (m, n)
        out_spec = pl.BlockSpec((tm, tn), lambda i, j, k: (i, j))
    else:
        onpp = n // out_pieces // tn
        out_shape = (out_pieces, m, n // out_pieces)
        out_spec = pl.BlockSpec((None, tm, tn), lambda i, j, k: (j // onpp, i, j % onpp))
    n_ex, n_out, n_in = len(extras), len(out_dtypes), len(extras) + len(deps)
    if epilogue is None:
        epilogue = lambda acc: (acc,)

    def body(a_ref, b_ref, *rest):
        ex_refs, o_refs = rest[:n_ex], rest[n_in:n_in + n_out]
        prod = lax.dot_general(a_ref[...], b_ref[...], dims, preferred_element_type=F32)

        def finish(acc):
            tiles = epilogue(acc, *[r[...] for r in ex_refs])
            for o_ref, t in zip(o_refs, tiles):
                o_ref[...] = t.astype(o_ref.dtype)

        if nk == 1:
            finish(prod)
        else:
            acc_ref = rest[n_in + n_out]
            k = pl.program_id(2)

            @pl.when(k == 0)
            def _():
                acc_ref[...] = prod

            @pl.when(jnp.logical_and(k > 0, k < nk - 1))
            def _():
                acc_ref[...] += prod

            @pl.when(k == nk - 1)
            def _():
                finish(acc_ref[...] + prod)

    ex_specs = [pl.BlockSpec((tm, tn), lambda i, j, k: (i, j)) for _ in extras]
    res = pl.pallas_call(
        body,
        name=name,
        grid=(m // tm, n // tn, nk),
        in_specs=[a_spec, b_spec, *ex_specs] + [ANY_SPEC] * len(deps),
        out_specs=[out_spec] * n_out,
        out_shape=[jax.ShapeDtypeStruct(out_shape, d) for d in out_dtypes],
        scratch_shapes=[pltpu.VMEM((tm, tn), F32)] if nk > 1 else [],
        compiler_params=pltpu.CompilerParams(dimension_semantics=("parallel", "parallel", "arbitrary")),
    )(a, b, *extras, *deps)
    return res[0] if n_out == 1 else res


def _matmul_row_pieces(name, a, b, order, tn, prev=None):
    m = a.shape[0]
    _, kp, n = b.shape
    n_steps = order.shape[0]
    assert n % tn == 0

    def body(order_ref, a_ref, b_ref, *rest):
        o_ref, acc_ref = rest[-2], rest[-1]
        s = pl.program_id(1)
        prod = jnp.dot(a_ref[...], b_ref[...], preferred_element_type=F32)

        @pl.when(s == 0)
        def _():
            acc_ref[...] = prod if prev is None else prod + rest[0][...]

        @pl.when(s > 0)
        def _():
            acc_ref[...] += prod

        @pl.when(s == n_steps - 1)
        def _():
            o_ref[...] = acc_ref[...]

    in_specs = [pl.BlockSpec((m, kp), lambda j, s, order_ref: (0, order_ref[s])),
                pl.BlockSpec((None, kp, tn), lambda j, s, order_ref: (order_ref[s], 0, j))]
    if prev is not None:
        in_specs.append(pl.BlockSpec((m, tn), lambda j, s, order_ref: (0, j)))
    return pl.pallas_call(
        body,
        name=name,
        grid_spec=pltpu.PrefetchScalarGridSpec(
            num_scalar_prefetch=1,
            grid=(n // tn, n_steps),
            in_specs=in_specs,
            out_specs=pl.BlockSpec((m, tn), lambda j, s, order_ref: (0, j)),
            scratch_shapes=[pltpu.VMEM((m, tn), F32)],
        ),
        out_shape=jax.ShapeDtypeStruct((m, n), F32),
        compiler_params=pltpu.CompilerParams(dimension_semantics=("parallel", "arbitrary")),
    )(order, a, b, *([] if prev is None else [prev]))


ROW_CHUNK = 16
SUBLANES = 8


def _rowwise(name, fn, rows, vecs, outs, sums, tr, deps=()):
    t = rows[0][0].shape[0]
    assert t % tr == 0 and tr % ROW_CHUNK == 0
    n_r, n_v, n_o, n_s = len(rows), len(vecs), len(outs), len(sums)
    n_in = n_r + n_v + len(deps)
    n_steps = t // tr

    def body(*refs):
        r_in, v_in = refs[:n_r], refs[n_r:n_r + n_v]
        o_refs = refs[n_in:n_in + n_o]
        s_refs = refs[n_in + n_o:n_in + n_o + n_s]
        acc_refs = refs[n_in + n_o + n_s:]
        i = pl.program_id(0)

        @pl.when(i == 0)
        def _():
            for acc_ref in acc_refs:
                acc_ref[...] = jnp.zeros(acc_ref.shape, F32)

        def chunk(ci):
            r0 = ci * ROW_CHUNK
            sl = pl.ds(r0, ROW_CHUNK)
            o_tiles, s_tiles = fn(i * tr + r0, [r[sl, :] for r in r_in], [v[...] for v in v_in])
            for o_ref, tile in zip(o_refs, o_tiles):
                o_ref[sl, :] = tile.astype(o_ref.dtype)
            for acc_ref, tile in zip(acc_refs, s_tiles):
                part = tile[0:SUBLANES]
                for s in range(1, ROW_CHUNK // SUBLANES):
                    part = part + tile[s * SUBLANES:(s + 1) * SUBLANES]
                acc_ref[...] += part

        for ci in range(tr // ROW_CHUNK):
            chunk(ci)

        @pl.when(i == n_steps - 1)
        def _():
            for s_ref, acc_ref in zip(s_refs, acc_refs):
                s_ref[...] = jnp.sum(acc_ref[...], axis=0, keepdims=True)

    def row_spec(width, blk):
        return pl.BlockSpec((tr, width), lambda i: (i, blk))

    res = pl.pallas_call(
        body,
        name=name,
        grid=(t // tr,),
        in_specs=[row_spec(w, blk) for _, w, blk in rows]
        + [pl.BlockSpec(v.shape, lambda i: (0, 0)) for v in vecs] + [ANY_SPEC] * len(deps),
        out_specs=[pl.BlockSpec((tr, c), lambda i: (i, 0)) for c, _ in outs]
        + [pl.BlockSpec((1, c), lambda i: (0, 0)) for c in sums],
        out_shape=[jax.ShapeDtypeStruct((t, c), d) for c, d in outs]
        + [jax.ShapeDtypeStruct((1, c), F32) for c in sums],
        scratch_shapes=[pltpu.VMEM((SUBLANES, c), F32) for c in sums],
        compiler_params=pltpu.CompilerParams(dimension_semantics=("arbitrary",)),
    )(*[r[0] for r in rows], *vecs, *deps)
    return res


def _rms_fwd(x, g):
    r = lax.rsqrt(jnp.mean(x * x, axis=-1, keepdims=True) + RMS_EPS)
    return x * r * g


def _rms_bwd(x, g, dy):
    r = lax.rsqrt(jnp.mean(x * x, axis=-1, keepdims=True) + RMS_EPS)
    xn = x * r
    dxn = dy * g
    dx = r * (dxn - xn * jnp.mean(dxn * xn, axis=-1, keepdims=True))
    return dx, dy * xn


CONV_ROWS = 64
CONV_LANES = 128


def _conv_fwd(proj, wdw, bdw, w3, d_conf):
    t = proj.shape[0]
    cl = CONV_LANES
    nb = d_conf // cl
    nchunk = t // CONV_ROWS
    assert t % CONV_ROWS == 0

    def body(av_ref, ag_ref, cg_ref, v_ref, wdw_ref, bdw_ref, w3_ref, ac_ref, c3_ref, apad, cpad):
        zeros = jnp.zeros((CONV_PAD, cl), F32)
        apad[0:CONV_PAD, :] = zeros
        cpad[0:CONV_PAD, :] = zeros
        apad[CONV_PAD:, :] = av_ref[...] * _sigmoid(ag_ref[...])
        cpad[CONV_PAD:, :] = cg_ref[...] * v_ref[...]

        def chunk(ci, carry):
            base = pl.multiple_of(ci * CONV_ROWS, 8)
            acc = jnp.zeros((CONV_ROWS, cl), F32) + bdw_ref[...]
            for k in range(CONF_KERNEL):
                off = CONV_PAD - (CONF_KERNEL - 1) + k
                acc = acc + apad[pl.ds(base + off, CONV_ROWS), :] * wdw_ref[k:k + 1, :]
            ac_ref[pl.ds(base, CONV_ROWS), :] = acc
            acc3 = jnp.zeros((CONV_ROWS, cl), F32)
            for k in range(SHORT_KERNEL):
                off = CONV_PAD - (SHORT_KERNEL - 1) + k
                acc3 = acc3 + cpad[pl.ds(base + off, CONV_ROWS), :] * w3_ref[k:k + 1, :]
            c3_ref[pl.ds(base, CONV_ROWS), :] = acc3
            return carry

        lax.fori_loop(0, nchunk, chunk, 0)

    def col(blk0):
        return pl.BlockSpec((t, cl), lambda j: (0, blk0 + j))

    return pl.pallas_call(
        body,
        name="conv_fwd",
        grid=(nb,),
        in_specs=[col(0), col(nb), col(3 * nb), col(4 * nb),
                  pl.BlockSpec((CONF_KERNEL, cl), lambda j: (0, j)),
                  pl.BlockSpec((1, cl), lambda j: (0, j)),
                  pl.BlockSpec((SHORT_KERNEL, cl), lambda j: (0, j))],
        out_specs=[pl.BlockSpec((t, cl), lambda j: (0, j))] * 2,
        out_shape=[jax.ShapeDtypeStruct((t, d_conf), F32)] * 2,
        scratch_shapes=[pltpu.VMEM((t + CONV_PAD, cl), F32)] * 2,
        compiler_params=pltpu.CompilerParams(dimension_semantics=("parallel",)),
    )(proj, proj, proj, proj, wdw, bdw, w3)


def _conv_bwd(proj, d_ac, d_c3, wdw, w3, d_conf):
    t = proj.shape[0]
    cl = CONV_LANES
    nb = d_conf // cl
    nchunk = t // CONV_ROWS
    nsub = CONV_ROWS // 8

    def fold(p):
        r = p[0:8]
        for s in range(1, nsub):
            r = r + p[8 * s:8 * s + 8]
        return r

    def body(av_ref, ag_ref, cg_ref, v_ref, dac_ref, dc3_ref, wdw_ref, w3_ref,
             dav_ref, dag_ref, dcg_ref, dv_ref, dwdw_ref, dw3_ref, apad, cpad, dapad, dcpad):
        zeros = jnp.zeros((CONV_PAD, cl), F32)
        apad[0:CONV_PAD, :] = zeros
        cpad[0:CONV_PAD, :] = zeros
        apad[CONV_PAD:, :] = av_ref[...] * _sigmoid(ag_ref[...])
        cpad[CONV_PAD:, :] = cg_ref[...] * v_ref[...]
        dapad[0:t, :] = dac_ref[...]
        dcpad[0:t, :] = dc3_ref[...]
        dapad[t:, :] = zeros
        dcpad[t:, :] = zeros

        def chunk(ci, accs):
            base = pl.multiple_of(ci * CONV_ROWS, 8)
            rows = pl.ds(base, CONV_ROWS)
            da = jnp.zeros((CONV_ROWS, cl), F32)
            for k in range(CONF_KERNEL):
                da = da + dapad[pl.ds(base + (CONF_KERNEL - 1 - k), CONV_ROWS), :] * wdw_ref[k:k + 1, :]
            dcv = jnp.zeros((CONV_ROWS, cl), F32)
            for k in range(SHORT_KERNEL):
                dcv = dcv + dcpad[pl.ds(base + (SHORT_KERNEL - 1 - k), CONV_ROWS), :] * w3_ref[k:k + 1, :]
            av, sg = av_ref[rows, :], _sigmoid(ag_ref[rows, :])
            dav_ref[rows, :] = (da * sg).astype(dav_ref.dtype)
            dag_ref[rows, :] = (da * av * sg * (1.0 - sg)).astype(dag_ref.dtype)
            dcg_ref[rows, :] = (dcv * v_ref[rows, :]).astype(dcg_ref.dtype)
            dv_ref[rows, :] = (dcv * cg_ref[rows, :]).astype(dv_ref.dtype)
            d_out, d_out3 = dac_ref[rows, :], dc3_ref[rows, :]
            new = []
            for k in range(CONF_KERNEL):
                off = CONV_PAD - (CONF_KERNEL - 1) + k
                new.append(accs[k] + fold(d_out * apad[pl.ds(base + off, CONV_ROWS), :]))
            for k in range(SHORT_KERNEL):
                off = CONV_PAD - (SHORT_KERNEL - 1) + k
                new.append(accs[CONF_KERNEL + k] + fold(d_out3 * cpad[pl.ds(base + off, CONV_ROWS), :]))
            return tuple(new)

        init = tuple(jnp.zeros((8, cl), F32) for _ in range(CONF_KERNEL + SHORT_KERNEL))
        accs = lax.fori_loop(0, nchunk, chunk, init)
        for k in range(CONF_KERNEL):
            dwdw_ref[k:k + 1, :] = jnp.sum(accs[k], axis=0, keepdims=True)
        for k in range(SHORT_KERNEL):
            dw3_ref[k:k + 1, :] = jnp.sum(accs[CONF_KERNEL + k], axis=0, keepdims=True)

    def col(blk0):
        return pl.BlockSpec((t, cl), lambda j: (0, blk0 + j))

    own = pl.BlockSpec((t, cl), lambda j: (0, j))
    return pl.pallas_call(
        body,
        name="conv_bwd",
        grid=(nb,),
        in_specs=[col(0), col(nb), col(3 * nb), col(4 * nb), own, own,
                  pl.BlockSpec((CONF_KERNEL, cl), lambda j: (0, j)),
                  pl.BlockSpec((SHORT_KERNEL, cl), lambda j: (0, j))],
        out_specs=[own] * 4 + [pl.BlockSpec((CONF_KERNEL, cl), lambda j: (0, j)),
                               pl.BlockSpec((SHORT_KERNEL, cl), lambda j: (0, j))],
        out_shape=[jax.ShapeDtypeStruct((t, d_conf), MXU_DTYPE)] * 4
        + [jax.ShapeDtypeStruct((CONF_KERNEL, d_conf), F32), jax.ShapeDtypeStruct((SHORT_KERNEL, d_conf), F32)],
        scratch_shapes=[pltpu.VMEM((t + CONV_PAD, cl), F32)] * 4,
        compiler_params=pltpu.CompilerParams(dimension_semantics=("parallel",)),
    )(proj, proj, proj, proj, d_ac, d_c3, wdw, w3)


def _elementwise(name, fn, ins, out_dtypes, tr, deps=()):
    ins = [(a, ()) if not isinstance(a, tuple) else a for a in ins]
    r, c = ins[0][0].shape[-2:]
    assert r % tr == 0, (name, r, tr)
    n_in = len(ins)

    def body(*refs):
        tiles = fn(*[x[...] for x in refs[:n_in]])
        for o_ref, tile in zip(refs[n_in + len(deps):], tiles):
            o_ref[...] = tile.astype(o_ref.dtype)

    def spec(lead):
        return pl.BlockSpec((None,) * len(lead) + (tr, c), lambda i: (*lead, i, 0))

    res = pl.pallas_call(
        body,
        name=name,
        grid=(r // tr,),
        in_specs=[spec(lead) for _, lead in ins] + [ANY_SPEC] * len(deps),
        out_specs=[pl.BlockSpec((tr, c), lambda i: (i, 0))] * len(out_dtypes),
        out_shape=[jax.ShapeDtypeStruct((r, c), d) for d in out_dtypes],
        compiler_params=pltpu.CompilerParams(dimension_semantics=("parallel",)),
    )(*[a for a, _ in ins], *deps)
    return res


def _adamw_tiles(w, g, m, v):
    m = ADAM_B1 * m + (1.0 - ADAM_B1) * g
    v = ADAM_B2 * v + (1.0 - ADAM_B2) * jnp.square(g)
    m_hat = m / (1.0 - ADAM_B1 ** ADAM_STEP)
    v_hat = v / (1.0 - ADAM_B2 ** ADAM_STEP)
    delta = -ADAM_LR * (m_hat / (jnp.sqrt(v_hat) + ADAM_EPS) + ADAM_WD * w)
    return g, delta, m, v


def _adamw(name, w, g, m, v, tr, deps=()):
    shape = w.shape
    flat = [a.reshape(shape[-2:]) if a.ndim > 2 else a for a in (w, g, m, v)]
    res = _elementwise(name, _adamw_tiles, flat, [F32] * 4, tr, deps)
    return [a.reshape(shape) for a in res]


def _pair_sum(name, p, q, core, tr):
    n_p, _, hr, c = p.shape
    assert hr % tr == 0

    def body(core_ref, p_ref, q_ref, o_ref):
        o_ref[...] = (p_ref[...] + q_ref[...]).astype(o_ref.dtype)

    return pl.pallas_call(
        body,
        name=name,
        grid_spec=pltpu.PrefetchScalarGridSpec(
            num_scalar_prefetch=1,
            grid=(n_p, hr // tr),
            in_specs=[pl.BlockSpec((None, None, tr, c), lambda a, i, core_ref: (a, core_ref[0], i, 0)),
                      pl.BlockSpec((None, tr, c), lambda a, i, core_ref: (a, i, 0))],
            out_specs=pl.BlockSpec((None, tr, c), lambda a, i, core_ref: (a, i, 0)),
        ),
        out_shape=jax.ShapeDtypeStruct((n_p, hr, c), WIRE_DTYPE),
        compiler_params=pltpu.CompilerParams(dimension_semantics=("parallel", "parallel")),
    )(core, p, q)


def _into_slot(name, w, slots, slot, dtype, tr, deps=()):
    r, c = w.shape
    assert r % tr == 0

    def body(slot_ref, w_ref, *rest):
        o_ref = rest[len(deps)]
        o_ref[...] = w_ref[...].astype(o_ref.dtype)

    return pl.pallas_call(
        body,
        name=name,
        grid_spec=pltpu.PrefetchScalarGridSpec(
            num_scalar_prefetch=1,
            grid=(r // tr,),
            in_specs=[pl.BlockSpec((tr, c), lambda i, slot_ref: (i, 0))] + [ANY_SPEC] * len(deps),
            out_specs=pl.BlockSpec((None, tr, c), lambda i, slot_ref: (slot_ref[0], i, 0)),
        ),
        out_shape=jax.ShapeDtypeStruct((slots, r, c), dtype),
        compiler_params=pltpu.CompilerParams(dimension_semantics=("parallel",)),
    )(slot, w, *deps)


def _sum_pieces(name, own, rb, place, tr):
    n_p, hr, c = rb.shape
    assert hr % tr == 0

    def body(place_ref, own_ref, *refs):
        chip = place_ref[0]
        acc = None
        for k in range(n_p):
            tile = jnp.where(chip == k, own_ref[...], refs[k][...]).astype(F32)
            acc = tile if acc is None else acc + tile
        refs[n_p][...] = acc

    def landed(k):
        return pl.BlockSpec((None, tr, c), lambda i, place_ref: (jnp.where(place_ref[0] == k, (k + 1) % n_p, k), i, 0))

    return pl.pallas_call(
        body,
        name=name,
        grid_spec=pltpu.PrefetchScalarGridSpec(
            num_scalar_prefetch=1,
            grid=(hr // tr,),
            in_specs=[pl.BlockSpec((None, tr, c), lambda i, place_ref: (place_ref[0], i, 0))]
            + [landed(k) for k in range(n_p)],
            out_specs=pl.BlockSpec((None, tr, c), lambda i, place_ref: (place_ref[1], i, 0)),
        ),
        out_shape=jax.ShapeDtypeStruct((2, hr, c), F32),
        compiler_params=pltpu.CompilerParams(dimension_semantics=("parallel",)),
    )(place, own, *([rb] * n_p))


HBM_SPEC = pl.BlockSpec(memory_space=pl.ANY)


def _place():
    x, y, c = lax.axis_index("x"), lax.axis_index("y"), lax.axis_index("c")
    chips = [(1 - x, y), (x, 1 - y), (1 - x, 1 - y)]
    return x, y, c, chips


def _join_halves(name, bufs, deps=()):
    n = len(bufs)

    def body(*refs):
        outs = refs[n + len(deps):2 * n + len(deps)]
        send_sems, recv_sems = refs[2 * n + len(deps):]
        x, y, c, _ = _place()
        copies = [pltpu.make_async_remote_copy(
            src_ref=outs[a].at[c], dst_ref=outs[a].at[c], send_sem=send_sems.at[a], recv_sem=recv_sems.at[a],
            device_id=(x, y, 1 - c), device_id_type=MESH) for a in range(n)]
        for cp in copies:
            cp.start()
        for a in range(n):
            other = outs[a].at[1 - c]
            pltpu.make_async_remote_copy(
                src_ref=other, dst_ref=other, send_sem=send_sems.at[a], recv_sem=recv_sems.at[a],
                device_id=(x, y, 1 - c), device_id_type=MESH).wait_recv()
        for cp in copies:
            cp.wait_send()

    return pl.pallas_call(
        body,
        name=name,
        in_specs=[HBM_SPEC] * n + [ANY_SPEC] * len(deps),
        out_specs=[HBM_SPEC] * n,
        out_shape=[jax.ShapeDtypeStruct(b.shape, b.dtype) for b in bufs],
        input_output_aliases={a: a for a in range(n)},
        scratch_shapes=[pltpu.SemaphoreType.DMA((n,)), pltpu.SemaphoreType.DMA((n,))],
    )(*bufs, *deps)


HBM_ONLY = pl.BlockSpec(memory_space=pltpu.HBM)
SEM_SPEC = pl.BlockSpec(memory_space=pltpu.SEMAPHORE)
DATAFLOW = pltpu.SideEffectType.DATAFLOW_SIDE_EFFECTING


def _in_hbm(a):
    return pltpu.with_memory_space_constraint(a, pltpu.HBM)


def _shard_part(ref, is_split, slot, h):
    if not is_split:
        return ref.at[slot]
    hr = ref.shape[1] // 2
    return ref.at[slot, pl.ds(h * hr, hr), :]


TOKEN = jax.ShapeDtypeStruct((8, LANES), F32)
VMEM_SPEC = pl.BlockSpec(memory_space=pltpu.VMEM)


def _gather_start(name, bufs, split, groups, deps=()):
    n, ng = len(bufs), len(groups)

    def body(*refs):
        ins, sems = refs[:n], refs[n + len(deps):n + len(deps) + 2 * ng]
        refs[-1][...] = jnp.zeros(TOKEN.shape, TOKEN.dtype)
        x, y, c, chips = _place()
        me = 2 * x + y
        for g, members in enumerate(groups):
            for s, a in enumerate(members):
                mine = _shard_part(ins[a], split[a], me, c)
                for j, chip in enumerate(chips):
                    pltpu.make_async_remote_copy(
                        src_ref=mine, dst_ref=mine, send_sem=sems[2 * g].at[3 * s + j], recv_sem=sems[2 * g + 1].at[3 * s + j],
                        device_id=(*chip, c), device_id_type=MESH).start()

    res = pl.pallas_call(
        body,
        name=name,
        in_specs=[HBM_ONLY] * n + [ANY_SPEC] * len(deps),
        out_specs=[SEM_SPEC] * (2 * ng) + [HBM_ONLY] * n + [VMEM_SPEC],
        out_shape=[pltpu.SemaphoreType.DMA((3 * len(members),)) for members in groups for _ in range(2)]
        + [pltpu.HBM(b.shape, b.dtype) for b in bufs] + [TOKEN],
        input_output_aliases={a: 2 * ng + a for a in range(n)},
        compiler_params=pltpu.CompilerParams(has_side_effects=DATAFLOW),
    )(*[_in_hbm(b) for b in bufs], *deps)
    return [(res[2 * g], res[2 * g + 1]) for g in range(ng)], list(res[2 * ng:2 * ng + n]), res[-1]


def _gather_wait(name, bufs, split, sems, after, which=(0, 1, 2)):
    n = len(bufs)

    def body(*refs):
        ins, send_sems, recv_sems = refs[:n], refs[n], refs[n + 1]
        x, y, c, chips = _place()
        me = 2 * x + y
        for s in range(n):
            for j in which:
                chip = chips[j]
                copy = pltpu.make_async_remote_copy(
                    src_ref=_shard_part(ins[s], split[s], me, c),
                    dst_ref=_shard_part(ins[s], split[s], 2 * chip[0] + chip[1], c),
                    send_sem=send_sems.at[3 * s + j], recv_sem=recv_sems.at[3 * s + j],
                    device_id=(*chip, c), device_id_type=MESH)
                copy.wait_send()
                copy.wait_recv()

    res = pl.pallas_call(
        body,
        name=name,
        in_specs=[HBM_ONLY] * n + [SEM_SPEC, SEM_SPEC] + [ANY_SPEC] * len(after),
        out_specs=[HBM_ONLY] * n,
        out_shape=[pltpu.HBM(b.shape, b.dtype) for b in bufs],
        input_output_aliases={a: a for a in range(n)},
        compiler_params=pltpu.CompilerParams(has_side_effects=DATAFLOW),
    )(*bufs, *sems, *after)
    return list(res)


def _pass_halves(name, bufs, which=(0, 1, 2)):
    n = len(bufs)

    def body(*refs):
        outs = refs[n:2 * n]
        send_sems, recv_sems = refs[2 * n:]
        x, y, c, chips = _place()
        sibling = (x, y, 1 - c)

        def copy(a, j, h):
            blk = _shard_part(outs[a], True, 2 * chips[j][0] + chips[j][1], h)
            return pltpu.make_async_remote_copy(
                src_ref=blk, dst_ref=blk, send_sem=send_sems.at[3 * a + j], recv_sem=recv_sems.at[3 * a + j],
                device_id=sibling, device_id_type=MESH)

        sends = [copy(a, j, c) for a in range(n) for j in which]
        for cp in sends:
            cp.start()
        for a in range(n):
            for j in which:
                copy(a, j, 1 - c).wait_recv()
        for cp in sends:
            cp.wait_send()

    res = pl.pallas_call(
        body,
        name=name,
        in_specs=[HBM_SPEC] * n,
        out_specs=[HBM_SPEC] * n,
        out_shape=[jax.ShapeDtypeStruct(b.shape, b.dtype) for b in bufs],
        input_output_aliases={a: a for a in range(n)},
        scratch_shapes=[pltpu.SemaphoreType.DMA((3 * n,)), pltpu.SemaphoreType.DMA((3 * n,))],
    )(*bufs)
    return list(res)


def _exchange_start(name, pair):
    def body(pair_ref, land_ref, send_sems, recv_sems, pair_thru, land_thru, token):
        x, y, c, chips = _place()
        me = 2 * x + y
        for j, chip in enumerate(chips):
            pltpu.make_async_remote_copy(
                src_ref=pair_ref.at[2 * chip[0] + chip[1]], dst_ref=land_ref.at[me],
                send_sem=send_sems.at[j], recv_sem=recv_sems.at[j], device_id=(*chip, c), device_id_type=MESH).start()
        token[...] = jnp.zeros(TOKEN.shape, TOKEN.dtype)

    send_sems, recv_sems, pair_thru, land_thru, token = pl.pallas_call(
        body,
        name=name,
        in_specs=[HBM_ONLY, HBM_ONLY],
        out_specs=[SEM_SPEC, SEM_SPEC, HBM_ONLY, HBM_ONLY, VMEM_SPEC],
        out_shape=[pltpu.SemaphoreType.DMA((3,)), pltpu.SemaphoreType.DMA((3,)),
                   pltpu.HBM(pair.shape, pair.dtype), pltpu.HBM(pair.shape, pair.dtype), TOKEN],
        input_output_aliases={0: 2, 1: 3},
        compiler_params=pltpu.CompilerParams(has_side_effects=DATAFLOW),
    )(_in_hbm(pair), _in_hbm(lax.empty(pair.shape, pair.dtype)))
    return (send_sems, recv_sems), pair_thru, land_thru, token


def _swap_start(name, halves):
    n_p, _, hr, cols = halves.shape
    land_shape = (n_p, hr, cols)

    def body(halves_ref, land_ref, send_sems, recv_sems, halves_thru, land_thru, token):
        x, y, c, _ = _place()
        for p in range(n_p):
            pltpu.make_async_remote_copy(
                src_ref=halves_ref.at[p, 1 - c], dst_ref=land_ref.at[p], send_sem=send_sems.at[p], recv_sem=recv_sems.at[p],
                device_id=(x, y, 1 - c), device_id_type=MESH).start()
        token[...] = jnp.zeros(TOKEN.shape, TOKEN.dtype)

    send_sems, recv_sems, halves_thru, land_thru, token = pl.pallas_call(
        body,
        name=name,
        in_specs=[HBM_ONLY, HBM_ONLY],
        out_specs=[SEM_SPEC, SEM_SPEC, HBM_ONLY, HBM_ONLY, VMEM_SPEC],
        out_shape=[pltpu.SemaphoreType.DMA((n_p,)), pltpu.SemaphoreType.DMA((n_p,)),
                   pltpu.HBM(halves.shape, halves.dtype), pltpu.HBM(land_shape, halves.dtype), TOKEN],
        input_output_aliases={0: 2, 1: 3},
        compiler_params=pltpu.CompilerParams(has_side_effects=DATAFLOW),
    )(_in_hbm(halves), _in_hbm(lax.empty(land_shape, halves.dtype)))
    return (send_sems, recv_sems), halves_thru, land_thru, token


def _swap_wait(name, halves, land, sems, after):
    n_p = halves.shape[0]

    def body(halves_ref, land_ref, send_sems, recv_sems, *rest):
        x, y, c, _ = _place()
        for p in range(n_p):
            copy = pltpu.make_async_remote_copy(
                src_ref=halves_ref.at[p, 1 - c], dst_ref=land_ref.at[p], send_sem=send_sems.at[p], recv_sem=recv_sems.at[p],
                device_id=(x, y, 1 - c), device_id_type=MESH)
            copy.wait_send()
            copy.wait_recv()

    return pl.pallas_call(
        body,
        name=name,
        in_specs=[HBM_ONLY, HBM_ONLY, SEM_SPEC, SEM_SPEC] + [ANY_SPEC] * len(after),
        out_specs=[HBM_ONLY, HBM_ONLY],
        out_shape=[pltpu.HBM(halves.shape, halves.dtype), pltpu.HBM(land.shape, land.dtype)],
        input_output_aliases={0: 0, 1: 1},
        compiler_params=pltpu.CompilerParams(has_side_effects=DATAFLOW),
    )(halves, land, *sems, *after)


def _exchange_wait(name, pair, land, sems, after):
    def body(pair_ref, land_ref, send_sems, recv_sems, *rest):
        x, y, c, chips = _place()
        for j, chip in enumerate(chips):
            k = 2 * chip[0] + chip[1]
            copy = pltpu.make_async_remote_copy(
                src_ref=pair_ref.at[k], dst_ref=land_ref.at[k], send_sem=send_sems.at[j], recv_sem=recv_sems.at[j],
                device_id=(*chip, c), device_id_type=MESH)
            copy.wait_send()
            copy.wait_recv()

    return pl.pallas_call(
        body,
        name=name,
        in_specs=[HBM_ONLY, HBM_ONLY, SEM_SPEC, SEM_SPEC] + [HBM_SPEC] * len(after),
        out_specs=[HBM_ONLY, HBM_ONLY],
        out_shape=[pltpu.HBM(pair.shape, pair.dtype), pltpu.HBM(land.shape, land.dtype)],
        input_output_aliases={0: 0, 1: 1},
        compiler_params=pltpu.CompilerParams(has_side_effects=DATAFLOW),
    )(pair, land, *sems, *after)


def _allgather_small(name, block, deps=()):
    m_per, n = block.shape

    def body(x_ref, *rest):
        out_ref, send_sems, recv_sems, local_sem = rest[len(deps):]
        x, y, c, chips = _place()
        me, sibling = (x, y, c), (x, y, 1 - c)

        def rows(px, py, pc):
            return out_ref.at[pl.ds((4 * px + 2 * py + pc) * m_per, m_per), :]

        def copy(k, blk, to, src=None):
            return pltpu.make_async_remote_copy(
                src_ref=rows(*blk) if src is None else src, dst_ref=rows(*blk),
                send_sem=send_sems.at[k], recv_sem=recv_sems.at[k], device_id=to, device_id_type=MESH)

        mine = pltpu.make_async_copy(x_ref, rows(*me), local_sem)
        mine.start()
        first = [copy(0, me, sibling, src=x_ref)]
        first += [copy(1 + j, me, (*chip, c), src=x_ref) for j, chip in enumerate(chips)]
        for cp in first:
            cp.start()
        passed = [copy(4 + j, (*chip, c), sibling) for j, chip in enumerate(chips)]
        for j, chip in enumerate(chips):
            copy(1 + j, (*chip, c), me).wait_recv()
            passed[j].start()
        copy(0, sibling, me).wait_recv()
        for j, chip in enumerate(chips):
            copy(4 + j, (*chip, 1 - c), me).wait_recv()
        for cp in first + passed:
            cp.wait_send()
        mine.wait()

    return pl.pallas_call(
        body,
        name=name,
        out_shape=jax.ShapeDtypeStruct((8 * m_per, n), block.dtype),
        in_specs=[pl.BlockSpec(memory_space=pltpu.VMEM)] + [ANY_SPEC] * len(deps),
        out_specs=pl.BlockSpec(memory_space=pltpu.VMEM),
        scratch_shapes=[pltpu.SemaphoreType.DMA((7,)), pltpu.SemaphoreType.DMA((7,)), pltpu.SemaphoreType.DMA],
    )(block, *deps)


def _sum_blocks(name, gathered, n_blocks):
    r = gathered.shape[0] // n_blocks
    c = gathered.shape[1]

    def body(g_ref, o_ref):
        acc = g_ref[0:r, :]
        for b in range(1, n_blocks):
            acc = acc + g_ref[b * r:(b + 1) * r, :]
        o_ref[...] = acc

    return pl.pallas_call(body, name=name, out_shape=jax.ShapeDtypeStruct((r, c), F32))(gathered)


def _largest_tile(n, cap, mult):
    best = None
    for d in range(mult, min(n, cap) + 1, mult):
        if n % d == 0:
            best = d
    assert best is not None, (n, cap, mult)
    return best


def kernel(x, meta, g_pre_mix, w_in, b_gates, conf_dw_w, conf_dw_b, conf_ln_g, conf_ln_b, conf_w_pw, short_dw_w, short_w_out, w_o, g_post_mix, g_pre_mlp, w_up, w_down, g_post_mlp, loss_target, m_meta, m_g_pre_mix, m_w_in, m_b_gates, m_conf_dw_w, m_conf_dw_b, m_conf_ln_g, m_conf_ln_b, m_conf_w_pw, m_short_dw_w, m_short_w_out, m_w_o, m_g_post_mix, m_g_pre_mlp, m_w_up, m_w_down, m_g_post_mlp, v_meta, v_g_pre_mix, v_w_in, v_b_gates, v_conf_dw_w, v_conf_dw_b, v_conf_ln_g, v_conf_ln_b, v_conf_w_pw, v_short_dw_w, v_short_w_out, v_w_o, v_g_post_mix, v_g_pre_mlp, v_w_up, v_w_down, v_g_post_mlp):
    seq, d = x.shape[1], x.shape[2]
    t_real = seq + N_META
    t = -(-t_real // LANES) * LANES
    d_conf = conf_dw_b.shape[1]
    d_ff = w_up.shape[2] * N_CHIPS
    in_cols = w_in.shape[2] * N_CHIPS
    assert in_cols == 5 * d_conf + 2 * d and d == 2 * d_conf
    cw = d_conf
    core = lax.axis_index("c")
    chip = 2 * lax.axis_index("x") + lax.axis_index("y")

    tr = _largest_tile(t, 272, ROW_CHUNK)
    tm = t
    tn_in = _largest_tile(in_cols // N_CHIPS, 768, LANES)
    tn_d = _largest_tile(d, 1024, LANES)
    tn_h = _largest_tile(d, 512, LANES)
    tn_ff = _largest_tile(d_ff // N_CHIPS, 1024, LANES)
    tn_pw = _largest_tile(d // N_CHIPS, 512, LANES)

    big = [w_in[0], conf_w_pw[0], short_w_out[0], w_o[0], w_up[0], w_down[0]]
    chip_arr = chip.astype(jnp.int32).reshape(1)

    def cast(a, deps):
        return _into_slot(f"cast_w{a}", big[a], N_CHIPS, chip_arr, MXU_DTYPE, _largest_tile(big[a].shape[0], 256, 16), deps)

    small = [_into_slot(f"place_w{a}", w, N_CHIPS, chip_arr, F32, w.shape[0])
             for a, w in enumerate([meta, conf_dw_w[0], short_dw_w[0]])]
    sems_small, fly_small, tok_small = _gather_start("gather_start_small", small, [False] * 3, [[0, 1, 2]])
    sems_in, fly_in, tok_in = _gather_start("gather_start_in", [cast(0, [tok_small])], [True], [[0]])
    rest_groups = [[0, 1], [2], [3], [4]]
    sems_rest, fly_rest, tok_rest = _gather_start(
        "gather_start_rest", [cast(a, [tok_in]) for a in range(1, 6)], [True] * 5, rest_groups)

    def arrive(g, after):
        members = rest_groups[g]
        got = _gather_wait(f"gather_wait_rest{g}", [fly_rest[a] for a in members], [True] * len(members),
                           sems_rest[g], after)
        return _pass_halves(f"gather_pass_rest{g}", got)

    meta_g, wdw_g, w3_g = _gather_wait("gather_wait_small", fly_small, [False] * 3, sems_small[0], [tok_in])
    meta_full = jnp.transpose(meta_g, (1, 0, 2)).reshape(N_META, d)
    wdw = jnp.transpose(wdw_g, (1, 0, 2)).reshape(CONF_KERNEL, d_conf)
    w3 = jnp.transpose(w3_g, (1, 0, 2)).reshape(SHORT_KERNEL, d_conf)

    tail = jnp.zeros((t - t_real, d), F32)
    h0 = jnp.concatenate([meta_full, x[0], tail], axis=0)
    target = jnp.concatenate([jnp.zeros((N_META, d), F32), loss_target[0], tail], axis=0)

    def norm_in(i, rows, vecs):
        return [_rms_fwd(rows[0], vecs[0])], []

    (n_lp,) = _rowwise("norm_in", norm_in, [(h0, d, 0)], [g_pre_mix], [(d, MXU_DTYPE)], [], tr, deps=[tok_rest])
    (wg_in,) = _pass_halves("gather_pass_in", _gather_wait("gather_wait_in", fly_in, [True], sems_in[0], [n_lp]))
    proj =_matmul("proj", n_lp, wg_in, kind="nn", tm=tm, tn=tn_in, tk=d, out_dtypes=[F32])
    ac, c3 = _conv_fwd(proj, wdw, conf_dw_b, w3, d_conf)

    def ln_parts(ac_t, ln_g, ln_b):
        mu = jnp.mean(ac_t, axis=-1, keepdims=True)
        xc = ac_t - mu
        rstd = lax.rsqrt(jnp.mean(xc * xc, axis=-1, keepdims=True) + LN_EPS)
        xh = xc * rstd
        return xh, rstd, xh * ln_g + ln_b

    def branch_act(i, rows, vecs):
        ac_t, c3_t, bg_t = rows
        _, _, al = ln_parts(ac_t, vecs[0], vecs[1])
        return [al * _sigmoid(al), bg_t * c3_t], []

    a_act, s_lp = _rowwise("branch_act", branch_act, [(ac, cw, 0), (c3, cw, 0), (proj, cw, 2)],
                           [conf_ln_g, conf_ln_b], [(d_conf, MXU_DTYPE), (d_conf, MXU_DTYPE)], [], tr)
    wg_pw, wg_sout = arrive(0, [a_act])
    y_a = _matmul("y_a", a_act, wg_pw, kind="nn", tm=tm, tn=tn_pw, tk=d_conf, out_dtypes=[F32])
    y_b = _matmul("y_b", s_lp, wg_sout, kind="nn", tm=tm, tn=tn_pw, tk=d_conf, out_dtypes=[F32])

    gate_rows = [(proj, cw, 5), (proj, cw, 6), (proj, cw, 7), (proj, cw, 8)]

    def gates_of(rows, b):
        ga = _sigmoid(jnp.concatenate([rows[0], rows[1]], axis=1) + b[:, :d])
        gb = _sigmoid(jnp.concatenate([rows[2], rows[3]], axis=1) + b[:, d:])
        return ga, gb

    def gate(i, rows, vecs):
        ga, gb = gates_of(rows[2:], vecs[0])
        return [ga * rows[0] + gb * rows[1]], []

    (m_lp,) = _rowwise("gate", gate, [(y_a, d, 0), (y_b, d, 0)] + gate_rows, [b_gates], [(d, MXU_DTYPE)], [], tr)
    wg_o = arrive(1, [m_lp])[0].reshape(d, d)
    mix = _matmul("mix", m_lp, wg_o, kind="nn", tm=tm, tn=tn_d, tk=d, out_dtypes=[F32])

    def post_mix(i, rows, vecs):
        h1_t = rows[0] + _rms_fwd(rows[1], vecs[0])
        return [h1_t, _rms_fwd(h1_t, vecs[1])], []

    h1, n2_lp = _rowwise("post_mix", post_mix, [(h0, d, 0), (mix, d, 0)], [g_post_mix, g_pre_mlp],
                         [(d, F32), (d, MXU_DTYPE)], [], tr)
    (wg_up,) = arrive(2, [n2_lp])
    up, f_lp = _matmul("up", n2_lp, wg_up, kind="nn", tm=tm, tn=tn_h, tk=d, out_dtypes=[F32, MXU_DTYPE],
                       epilogue=lambda acc: (acc, jnp.square(jnp.maximum(acc, 0.0))))
    cx, cy = lax.axis_index("x"), lax.axis_index("y")
    near = jnp.stack([2 * cx + cy, 2 * (1 - cx) + cy, 2 * cx + (1 - cy)]).astype(jnp.int32)
    far = (2 * (1 - cx) + (1 - cy)).astype(jnp.int32).reshape(1)
    down_buf = _gather_wait("gather_wait_down_near", [fly_rest[4]], [True], sems_rest[3], [f_lp], which=(0, 1))
    down_buf = _pass_halves("gather_pass_down_near", down_buf, which=(0, 1))
    dn_near = _matmul_row_pieces("down_near", f_lp, down_buf[0], near, tn_h)
    down_buf = _gather_wait("gather_wait_down_far", down_buf, [True], sems_rest[3], [dn_near], which=(2,))
    down_buf = _pass_halves("gather_pass_down_far", down_buf, which=(2,))
    dn = _matmul_row_pieces("down_far", f_lp, down_buf[0], far, tn_h, prev=dn_near)
    wg_down = down_buf[0].reshape(d_ff, d)

    def head(row0, rows, vecs):
        h1_t, dn_t, tgt = rows
        y = h1_t + _rms_fwd(dn_t, vecs[0])
        row = row0 + lax.broadcasted_iota(jnp.int32, (ROW_CHUNK, 1), 0)
        err = jnp.where(jnp.logical_and(row >= N_META, row < t_real), y - tgt, 0.0)
        dy = err / d
        d_dn, dg = _rms_bwd(dn_t, vecs[0], dy)
        loss_rows = 0.5 * jnp.mean(err * err, axis=-1, keepdims=True)
        return [dy, d_dn], [dg, jnp.broadcast_to(loss_rows, (ROW_CHUNK, LANES))]

    dy, d_dn, dg_post_mlp, loss_vec = _rowwise(
        "head", head, [(h1, d, 0), (dn, d, 0), (target, d, 0)], [g_post_mlp], [(d, F32), (d, MXU_DTYPE)], [d, LANES], tr)
    loss = lax.psum(loss_vec[0, 0], ("x", "y", "c"))

    core_arr = core.astype(jnp.int32).reshape(1)
    place = jnp.stack([chip, core]).astype(jnp.int32)
    in_flight = {}


    def swap_start(a, gw):
        halves = gw.reshape(N_CHIPS, 2, gw.shape[1] // 2, gw.shape[2])
        *in_flight[a], token = _swap_start(f"swap_start{a}", halves)
        return token

    def exchange_start(a, after):
        sems, halves, land = in_flight[a]
        halves, land = _swap_wait(f"swap_wait{a}", halves, land, sems, after)
        pair = _pair_sum(f"pair_sum{a}", halves, land, core_arr, _largest_tile(land.shape[1], 256, 16))
        *in_flight[a], token = _exchange_start(f"exchange_start{a}", pair)
        return token

    def reduce_finish(a, after):
        sems, pair, land = in_flight[a]
        pair, land = _exchange_wait(f"exchange_wait{a}", pair, land, sems, after)
        return _sum_pieces(f"sum_pieces{a}", pair, land, place, _largest_tile(land.shape[1], 256, 16))

    d_up = _matmul("d_up", d_dn, wg_down, kind="nt", tm=tm, tn=tn_h, tk=d, out_dtypes=[MXU_DTYPE], extras=[up],
                   epilogue=lambda acc, up_t: (acc * (2.0 * jnp.maximum(up_t, 0.0)),))
    tk_t = t
    gw_down = _matmul("gw_down", f_lp, d_dn, kind="tn", tm=_largest_tile(d_ff, 2048, LANES), tn=tn_d, tk=tk_t,
                      out_dtypes=[F32])
    tok = swap_start(5, gw_down.reshape(N_CHIPS, d_ff // N_CHIPS, d))
    d_n2 = _matmul("d_n2", d_up, wg_up, kind="nt", tm=tm, tn=tn_h, tk=d_ff // N_CHIPS, out_dtypes=[F32], deps=[tok])
    tok = exchange_start(5, [d_n2])
    gw_up = _matmul("gw_up", n2_lp, d_up, kind="tn", tm=_largest_tile(d, 2048, LANES), tn=tn_ff, tk=tk_t,
                    out_dtypes=[F32], out_pieces=N_CHIPS, deps=[tok])
    tok = swap_start(4, gw_up)

    def bwd_mid(i, rows, vecs):
        dy_t, dn2_t, h1_t, mix_t = rows
        d_h1a, dg_pre_mlp = _rms_bwd(h1_t, vecs[1], dn2_t)
        d_h1 = dy_t + d_h1a
        d_mix, dg_post_mix = _rms_bwd(mix_t, vecs[0], d_h1)
        return [d_h1, d_mix], [dg_pre_mlp, dg_post_mix]

    d_h1, d_mix, dg_pre_mlp, dg_post_mix = _rowwise(
        "bwd_mid", bwd_mid, [(dy, d, 0), (d_n2, d, 0), (h1, d, 0), (mix, d, 0)], [g_post_mix, g_pre_mlp],
        [(d, F32), (d, MXU_DTYPE)], [d, d], tr, deps=[tok])
    d_m = _matmul("d_m", d_mix, wg_o, kind="nt", tm=tm, tn=tn_d, tk=d, out_dtypes=[F32])
    tok = exchange_start(4, [d_m])
    gw_o = _matmul("gw_o", m_lp, d_mix, kind="tn", tm=_largest_tile(d, 2048, LANES), tn=tn_d, tk=tk_t, out_dtypes=[F32],
                   deps=[tok])
    tok = swap_start(3, gw_o.reshape(N_CHIPS, d // N_CHIPS, d))

    def gate_bwd(i, rows, vecs):
        dm_t, ya_t, yb_t = rows[:3]
        ga, gb = gates_of(rows[3:], vecs[0])
        d_gpre = jnp.concatenate([dm_t * ya_t * ga * (1.0 - ga), dm_t * yb_t * gb * (1.0 - gb)], axis=1)
        return [dm_t * ga, dm_t * gb, d_gpre], [d_gpre]

    d_ya, d_yb, d_gpre, dg_b_gates = _rowwise(
        "gate_bwd", gate_bwd, [(d_m, d, 0), (y_a, d, 0), (y_b, d, 0)] + gate_rows, [b_gates],
        [(d, MXU_DTYPE), (d, MXU_DTYPE), (2 * d, MXU_DTYPE)], [2 * d], tr, deps=[tok])
    tok = exchange_start(3, [d_ya])
    d_aact = _matmul("d_aact", d_ya, wg_pw, kind="nt", tm=tm, tn=d_conf, tk=tn_pw, out_dtypes=[F32], deps=[tok])
    gw_pw = _matmul("gw_pw", a_act, d_ya, kind="tn", tm=_largest_tile(d_conf, 2048, LANES), tn=tn_pw, tk=tk_t,
                    out_dtypes=[F32], out_pieces=N_CHIPS)
    tok = swap_start(1, gw_pw)
    d_s = _matmul("d_s", d_yb, wg_sout, kind="nt", tm=tm, tn=d_conf, tk=tn_pw, out_dtypes=[F32], deps=[tok])
    tok = exchange_start(1, [d_s])
    gw_sout = _matmul("gw_sout", s_lp, d_yb, kind="tn", tm=_largest_tile(d_conf, 2048, LANES), tn=tn_pw, tk=tk_t,
                      out_dtypes=[F32], out_pieces=N_CHIPS, deps=[tok])
    tok = swap_start(2, gw_sout)

    def branch_bwd(i, rows, vecs):
        daact_t, ds_t, ac_t, c3_t, bg_t = rows
        xh, rstd, al = ln_parts(ac_t, vecs[0], vecs[1])
        sg = _sigmoid(al)
        d_al = daact_t * (sg * (1.0 + al * (1.0 - sg)))
        dxh = d_al * vecs[0]
        d_ac = rstd * (dxh - jnp.mean(dxh, axis=-1, keepdims=True) - xh * jnp.mean(dxh * xh, axis=-1, keepdims=True))
        return [d_ac, ds_t * bg_t, ds_t * c3_t], [d_al * xh, d_al, d_ac]

    d_ac, d_c3, d_bg, dg_ln_g, dg_ln_b, dg_dw_b = _rowwise(
        "branch_bwd", branch_bwd, [(d_aact, cw, 0), (d_s, cw, 0), (ac, cw, 0), (c3, cw, 0), (proj, cw, 2)],
        [conf_ln_g, conf_ln_b], [(d_conf, F32), (d_conf, F32), (d_conf, MXU_DTYPE)], [d_conf] * 3, tr, deps=[tok])
    tok = exchange_start(2, [d_ac])
    d_av, d_ag, d_cg, d_v, dg_wdw, dg_w3 = _conv_bwd(proj, d_ac, d_c3, wdw, w3, d_conf)
    d_proj = jnp.concatenate([d_av, d_ag, d_bg, d_cg, d_v, d_gpre], axis=1)
    small_w = d_conf

    def pack(arrs):
        flat = jnp.concatenate([a.reshape(-1, small_w) for a in arrs], axis=0)
        return jnp.pad(flat, ((0, -flat.shape[0] % 8), (0, 0)))

    def unpack(buf, like):
        out, r0 = [], 0
        for a in like:
            nr = a.size // small_w
            out.append(buf[r0:r0 + nr].reshape(a.shape))
            r0 += nr
        return out

    def reduce_small(tag, arrs, deps):
        assert all(a.size % small_w == 0 for a in arrs)
        gathered = _allgather_small(f"allgather_small_{tag}", pack(arrs), deps)
        return unpack(_sum_blocks(f"sum_small_{tag}", gathered, 8), arrs)

    rep_w = [b_gates, conf_dw_b, conf_ln_g, conf_ln_b, g_post_mix, g_pre_mlp, g_post_mlp]
    rep_m = [m_b_gates, m_conf_dw_b, m_conf_ln_g, m_conf_ln_b, m_g_post_mix, m_g_pre_mlp, m_g_post_mlp]
    rep_v = [v_b_gates, v_conf_dw_b, v_conf_ln_g, v_conf_ln_b, v_g_post_mix, v_g_pre_mlp, v_g_post_mlp]
    *g_rep, g_wdw_full, g_w3_full = reduce_small(
        "mid", [dg_b_gates, dg_dw_b, dg_ln_g, dg_ln_b, dg_post_mix, dg_pre_mlp, dg_post_mlp, dg_wdw, dg_w3], [tok])
    sc = d_conf // N_CHIPS
    g_wdw = lax.dynamic_slice_in_dim(g_wdw_full, chip * sc, sc, axis=1)
    g_w3 = lax.dynamic_slice_in_dim(g_w3_full, chip * sc, sc, axis=1)

    gw_in = _matmul("gw_in", n_lp, d_proj, kind="tn", tm=_largest_tile(d, 2048, LANES), tn=tn_in, tk=tk_t,
                    out_dtypes=[F32], out_pieces=N_CHIPS, deps=[g_w3_full])
    tok = swap_start(0, gw_in)
    d_n = _matmul("d_n", d_proj, wg_in, kind="nt", tm=tm, tn=tn_h, tk=in_cols // N_CHIPS, out_dtypes=[F32], deps=[tok])
    tok = exchange_start(0, [d_n])

    def bwd_in(i, rows, vecs):
        d_h0a, dg = _rms_bwd(rows[2], vecs[0], rows[1])
        return [rows[0] + d_h0a], [dg]

    d_h0, dg_pre_mix = _rowwise("bwd_in", bwd_in, [(d_h1, d, 0), (d_n, d, 0), (h0, d, 0)], [g_pre_mix],
                                [(d, F32)], [d], tr, deps=[tok])
    grad_x = d_h0[N_META:t_real][None]

    big_m = [m_w_in, m_conf_w_pw, m_short_w_out, m_w_o, m_w_up, m_w_down]
    big_v = [v_w_in, v_conf_w_pw, v_short_w_out, v_w_o, v_w_up, v_w_down]
    big_res = {}

    def reduce_group(members, after):
        reduced = []
        for a in members:
            reduced.append(reduce_finish(a, after))
            after = [reduced[-1]]
        return reduced

    def adam_group(members, joined):
        for a, j in zip(members, joined):
            big_res[a] = _adamw(f"adamw_big{a}", big[a], j.reshape(big[a].shape), big_m[a][0], big_v[a][0],
                                _largest_tile(big[a].shape[0], 256, 8))

    early = [5, 4, 3, 1, 2]
    adam_group(early, _join_halves("join_halves_early", reduce_group(early, [d_h0])))
    rep_pack = [pack(arrs) for arrs in (rep_w, g_rep, rep_m, rep_v)]
    rep_res = [unpack(buf, rep_w) for buf in _adamw("adamw_rep", *rep_pack, rep_pack[0].shape[0])]
    col_res = {a: _adamw(f"adamw_col{a}", w, g, m, v, w.shape[0]) for a, (w, g, m, v) in (
        (1, (conf_dw_w[0], g_wdw, m_conf_dw_w[0], v_conf_dw_w[0])),
        (2, (short_dw_w[0], g_w3, m_short_dw_w[0], v_short_dw_w[0])))}
    reduced_late = reduce_group([0], [rep_res[1][0], col_res[1][1], col_res[2][1]] + [big_res[a][1] for a in early])
    adam_group([0], _join_halves("join_halves_late", reduced_late))
    g_pre_mix_full, g_meta_full = reduce_small("late", [dg_pre_mix, d_h0[:N_META]], [big_res[0][1]])
    g_meta = lax.dynamic_slice_in_dim(g_meta_full, chip * (d // N_CHIPS), d // N_CHIPS, axis=1)
    col_res[0] = _adamw("adamw_col0", meta, g_meta, m_meta, v_meta, meta.shape[0])
    pre_mix_res = _adamw("adamw_pre_mix", g_pre_mix, g_pre_mix_full, m_g_pre_mix, v_g_pre_mix, 1)
    rep_res = [[pre_mix_res[q]] + rep_res[q] for q in range(4)]

    def leaf(q):
        r = lambda a: rep_res[q][a]
        b = lambda a: big_res[a][q][None]
        return [col_res[0][q], r(0), b(0), r(1), col_res[1][q][None], r(2), r(3), r(4), b(1), col_res[2][q][None], b(2),
                b(3), r(5), r(6), b(4), b(5), r(7)]

    return (loss, grad_x, *leaf(0), *leaf(1), *leaf(2), *leaf(3))
```

```python
import jax
import jax.numpy as jnp
from jax import lax
from jax.experimental import pallas as pl
from jax.experimental.pallas import tpu as pltpu

F32 = jnp.float32
BF16 = jnp.bfloat16
MXU_DTYPE = BF16
WIRE_DTYPE = BF16

N_META = 16
CONF_KERNEL = 31
SHORT_KERNEL = 3
CONV_PAD = 32
RMS_EPS = 1e-6
LN_EPS = 1e-5
ADAM_LR = 0.001
ADAM_B1 = 0.9
ADAM_B2 = 0.999
ADAM_EPS = 1e-08
ADAM_WD = 0.01
ADAM_STEP = 10

N_CHIPS = 4
MESH = pl.DeviceIdType.MESH
LANES = 128


def _sigmoid(z):
    return 1.0 / (1.0 + jnp.exp(-z))


ANY_SPEC = pl.BlockSpec(memory_space=pl.ANY)


def _matmul(name, a, b, *, kind, tm, tn, tk, out_dtypes, out_pieces=1, epilogue=None, extras=(), deps=()):
    pieces = b.shape[0] if b.ndim == 3 else 1
    if kind == "nn":
        m, kdim = a.shape
        n = b.shape[-1] * pieces
        dims = (((1,), (0,)), ((), ()))
        a_spec = pl.BlockSpec((tm, tk), lambda i, j, k: (i, k))
        if b.ndim == 2:
            b_spec = pl.BlockSpec((tk, tn), lambda i, j, k: (k, j))
        else:
            npp = b.shape[-1] // tn
            b_spec = pl.BlockSpec((None, tk, tn), lambda i, j, k: (j // npp, k, j % npp))
    elif kind == "nt":
        m, kdim = a.shape
        n = b.shape[-2]
        dims = (((1,), (1,)), ((), ()))
        a_spec = pl.BlockSpec((tm, tk), lambda i, j, k: (i, k))
        if b.ndim == 2:
            b_spec = pl.BlockSpec((tn, tk), lambda i, j, k: (j, k))
        else:
            kpp = b.shape[-1] // tk
            b_spec = pl.BlockSpec((None, tn, tk), lambda i, j, k: (k // kpp, j, k % kpp))
    else:
        kdim, m = a.shape
        n = b.shape[-1]
        dims = (((0,), (0,)), ((), ()))
        a_spec = pl.BlockSpec((tk, tm), lambda i, j, k: (k, i))
        b_spec = pl.BlockSpec((tk, tn), lambda i, j, k: (k, j))
    assert m % tm == 0 and n % tn == 0 and kdim % tk == 0, (name, m, n, kdim, tm, tn, tk)
    nk = kdim // tk
    if out_pieces == 1:
        out_shape = (m, n)
        out_spec = pl.BlockSpec((tm, tn), lambda i, j, k: (i, j))
    else:
        onpp = n // out_pieces // tn
        out_shape = (out_pieces, m, n // out_pieces)
        out_spec = pl.BlockSpec((None, tm, tn), lambda i, j, k: (j // onpp, i, j % onpp))
    n_ex, n_out, n_in = len(extras), len(out_dtypes), len(extras) + len(deps)
    if epilogue is None:
        epilogue = lambda acc: (acc,)

    def body(a_ref, b_ref, *rest):
        ex_refs, o_refs = rest[:n_ex], rest[n_in:n_in + n_out]
        prod = lax.dot_general(a_ref[...], b_ref[...], dims, preferred_element_type=F32)

        def finish(acc):
            tiles = epilogue(acc, *[r[...] for r in ex_refs])
            for o_ref, t in zip(o_refs, tiles):
                o_ref[...] = t.astype(o_ref.dtype)

        if nk == 1:
            finish(prod)
        else:
            acc_ref = rest[n_in + n_out]
            k = pl.program_id(2)

            @pl.when(k == 0)
            def _():
                acc_ref[...] = prod

            @pl.when(jnp.logical_and(k > 0, k < nk - 1))
            def _():
                acc_ref[...] += prod

            @pl.when(k == nk - 1)
            def _():
                finish(acc_ref[...] + prod)

    ex_specs = [pl.BlockSpec((tm, tn), lambda i, j, k: (i, j)) for _ in extras]
    res = pl.pallas_call(
        body,
        name=name,
        grid=(m // tm, n // tn, nk),
        in_specs=[a_spec, b_spec, *ex_specs] + [ANY_SPEC] * len(deps),
        out_specs=[out_spec] * n_out,
        out_shape=[jax.ShapeDtypeStruct(out_shape, d) for d in out_dtypes],
        scratch_shapes=[pltpu.VMEM((tm, tn), F32)] if nk > 1 else [],
        compiler_params=pltpu.CompilerParams(dimension_semantics=("parallel", "parallel", "arbitrary")),
    )(a, b, *extras, *deps)
    return res[0] if n_out == 1 else res


ROW_CHUNK = 16
SUBLANES = 8


def _rowwise(name, fn, rows, vecs, outs, sums, tr, deps=()):
    t = rows[0][0].shape[0]
    assert t % tr == 0 and tr % ROW_CHUNK == 0
    n_r, n_v, n_o, n_s = len(rows), len(vecs), len(outs), len(sums)
    n_in = n_r + n_v + len(deps)
    n_steps = t // tr

    def body(*refs):
        r_in, v_in = refs[:n_r], refs[n_r:n_r + n_v]
        o_refs = refs[n_in:n_in + n_o]
        s_refs = refs[n_in + n_o:n_in + n_o + n_s]
        acc_refs = refs[n_in + n_o + n_s:]
        i = pl.program_id(0)

        @pl.when(i == 0)
        def _():
            for acc_ref in acc_refs:
                acc_ref[...] = jnp.zeros(acc_ref.shape, F32)

        def chunk(ci):
            r0 = ci * ROW_CHUNK
            sl = pl.ds(r0, ROW_CHUNK)
            o_tiles, s_tiles = fn(i * tr + r0, [r[sl, :] for r in r_in], [v[...] for v in v_in])
            for o_ref, tile in zip(o_refs, o_tiles):
                o_ref[sl, :] = tile.astype(o_ref.dtype)
            for acc_ref, tile in zip(acc_refs, s_tiles):
                part = tile[0:SUBLANES]
                for s in range(1, ROW_CHUNK // SUBLANES):
                    part = part + tile[s * SUBLANES:(s + 1) * SUBLANES]
                acc_ref[...] += part

        for ci in range(tr // ROW_CHUNK):
            chunk(ci)

        @pl.when(i == n_steps - 1)
        def _():
            for s_ref, acc_ref in zip(s_refs, acc_refs):
                s_ref[...] = jnp.sum(acc_ref[...], axis=0, keepdims=True)

    def row_spec(width, blk):
        return pl.BlockSpec((tr, width), lambda i: (i, blk))

    res = pl.pallas_call(
        body,
        name=name,
        grid=(t // tr,),
        in_specs=[row_spec(w, blk) for _, w, blk in rows]
        + [pl.BlockSpec(v.shape, lambda i: (0, 0)) for v in vecs] + [ANY_SPEC] * len(deps),
        out_specs=[pl.BlockSpec((tr, c), lambda i: (i, 0)) for c, _ in outs]
        + [pl.BlockSpec((1, c), lambda i: (0, 0)) for c in sums],
        out_shape=[jax.ShapeDtypeStruct((t, c), d) for c, d in outs]
        + [jax.ShapeDtypeStruct((1, c), F32) for c in sums],
        scratch_shapes=[pltpu.VMEM((SUBLANES, c), F32) for c in sums],
        compiler_params=pltpu.CompilerParams(dimension_semantics=("arbitrary",)),
    )(*[r[0] for r in rows], *vecs, *deps)
    return res


def _rms_fwd(x, g):
    r = lax.rsqrt(jnp.mean(x * x, axis=-1, keepdims=True) + RMS_EPS)
    return x * r * g


def _rms_bwd(x, g, dy):
    r = lax.rsqrt(jnp.mean(x * x, axis=-1, keepdims=True) + RMS_EPS)
    xn = x * r
    dxn = dy * g
    dx = r * (dxn - xn * jnp.mean(dxn * xn, axis=-1, keepdims=True))
    return dx, dy * xn


CONV_ROWS = 64
CONV_LANES = 128


def _conv_fwd(proj, wdw, bdw, w3, d_conf):
    t = proj.shape[0]
    cl = CONV_LANES
    nb = d_conf // cl
    nchunk = t // CONV_ROWS
    assert t % CONV_ROWS == 0

    def body(av_ref, ag_ref, cg_ref, v_ref, wdw_ref, bdw_ref, w3_ref, ac_ref, c3_ref, apad, cpad):
        zeros = jnp.zeros((CONV_PAD, cl), F32)
        apad[0:CONV_PAD, :] = zeros
        cpad[0:CONV_PAD, :] = zeros
        apad[CONV_PAD:, :] = av_ref[...] * _sigmoid(ag_ref[...])
        cpad[CONV_PAD:, :] = cg_ref[...] * v_ref[...]

        def chunk(ci, carry):
            base = pl.multiple_of(ci * CONV_ROWS, 8)
            acc = jnp.zeros((CONV_ROWS, cl), F32) + bdw_ref[...]
            for k in range(CONF_KERNEL):
                off = CONV_PAD - (CONF_KERNEL - 1) + k
                acc = acc + apad[pl.ds(base + off, CONV_ROWS), :] * wdw_ref[k:k + 1, :]
            ac_ref[pl.ds(base, CONV_ROWS), :] = acc
            acc3 = jnp.zeros((CONV_ROWS, cl), F32)
            for k in range(SHORT_KERNEL):
                off = CONV_PAD - (SHORT_KERNEL - 1) + k
                acc3 = acc3 + cpad[pl.ds(base + off, CONV_ROWS), :] * w3_ref[k:k + 1, :]
            c3_ref[pl.ds(base, CONV_ROWS), :] = acc3
            return carry

        lax.fori_loop(0, nchunk, chunk, 0)

    def col(blk0):
        return pl.BlockSpec((t, cl), lambda j: (0, blk0 + j))

    return pl.pallas_call(
        body,
        name="conv_fwd",
        grid=(nb,),
        in_specs=[col(0), col(nb), col(3 * nb), col(4 * nb),
                  pl.BlockSpec((CONF_KERNEL, cl), lambda j: (0, j)),
                  pl.BlockSpec((1, cl), lambda j: (0, j)),
                  pl.BlockSpec((SHORT_KERNEL, cl), lambda j: (0, j))],
        out_specs=[pl.BlockSpec((t, cl), lambda j: (0, j))] * 2,
        out_shape=[jax.ShapeDtypeStruct((t, d_conf), F32)] * 2,
        scratch_shapes=[pltpu.VMEM((t + CONV_PAD, cl), F32)] * 2,
        compiler_params=pltpu.CompilerParams(dimension_semantics=("parallel",)),
    )(proj, proj, proj, proj, wdw, bdw, w3)


def _conv_bwd(proj, d_ac, d_c3, wdw, w3, d_conf):
    t = proj.shape[0]
    cl = CONV_LANES
    nb = d_conf // cl
    nchunk = t // CONV_ROWS
    nsub = CONV_ROWS // 8

    def fold(p):
        r = p[0:8]
        for s in range(1, nsub):
            r = r + p[8 * s:8 * s + 8]
        return r

    def body(av_ref, ag_ref, cg_ref, v_ref, dac_ref, dc3_ref, wdw_ref, w3_ref,
             dav_ref, dag_ref, dcg_ref, dv_ref, dwdw_ref, dw3_ref, apad, cpad, dapad, dcpad):
        zeros = jnp.zeros((CONV_PAD, cl), F32)
        apad[0:CONV_PAD, :] = zeros
        cpad[0:CONV_PAD, :] = zeros
        apad[CONV_PAD:, :] = av_ref[...] * _sigmoid(ag_ref[...])
        cpad[CONV_PAD:, :] = cg_ref[...] * v_ref[...]
        dapad[0:t, :] = dac_ref[...]
        dcpad[0:t, :] = dc3_ref[...]
        dapad[t:, :] = zeros
        dcpad[t:, :] = zeros

        def chunk(ci, accs):
            base = pl.multiple_of(ci * CONV_ROWS, 8)
            rows = pl.ds(base, CONV_ROWS)
            da = jnp.zeros((CONV_ROWS, cl), F32)
            for k in range(CONF_KERNEL):
                da = da + dapad[pl.ds(base + (CONF_KERNEL - 1 - k), CONV_ROWS), :] * wdw_ref[k:k + 1, :]
            dcv = jnp.zeros((CONV_ROWS, cl), F32)
            for k in range(SHORT_KERNEL):
                dcv = dcv + dcpad[pl.ds(base + (SHORT_KERNEL - 1 - k), CONV_ROWS), :] * w3_ref[k:k + 1, :]
            av, sg = av_ref[rows, :], _sigmoid(ag_ref[rows, :])
            dav_ref[rows, :] = (da * sg).astype(dav_ref.dtype)
            dag_ref[rows, :] = (da * av * sg * (1.0 - sg)).astype(dag_ref.dtype)
            dcg_ref[rows, :] = (dcv * v_ref[rows, :]).astype(dcg_ref.dtype)
            dv_ref[rows, :] = (dcv * cg_ref[rows, :]).astype(dv_ref.dtype)
            d_out, d_out3 = dac_ref[rows, :], dc3_ref[rows, :]
            new = []
            for k in range(CONF_KERNEL):
                off = CONV_PAD - (CONF_KERNEL - 1) + k
                new.append(accs[k] + fold(d_out * apad[pl.ds(base + off, CONV_ROWS), :]))
            for k in range(SHORT_KERNEL):
                off = CONV_PAD - (SHORT_KERNEL - 1) + k
                new.append(accs[CONF_KERNEL + k] + fold(d_out3 * cpad[pl.ds(base + off, CONV_ROWS), :]))
            return tuple(new)

        init = tuple(jnp.zeros((8, cl), F32) for _ in range(CONF_KERNEL + SHORT_KERNEL))
        accs = lax.fori_loop(0, nchunk, chunk, init)
        for k in range(CONF_KERNEL):
            dwdw_ref[k:k + 1, :] = jnp.sum(accs[k], axis=0, keepdims=True)
        for k in range(SHORT_KERNEL):
            dw3_ref[k:k + 1, :] = jnp.sum(accs[CONF_KERNEL + k], axis=0, keepdims=True)

    def col(blk0):
        return pl.BlockSpec((t, cl), lambda j: (0, blk0 + j))

    own = pl.BlockSpec((t, cl), lambda j: (0, j))
    return pl.pallas_call(
        body,
        name="conv_bwd",
        grid=(nb,),
        in_specs=[col(0), col(nb), col(3 * nb), col(4 * nb), own, own,
                  pl.BlockSpec((CONF_KERNEL, cl), lambda j: (0, j)),
                  pl.BlockSpec((SHORT_KERNEL, cl), lambda j: (0, j))],
        out_specs=[own] * 4 + [pl.BlockSpec((CONF_KERNEL, cl), lambda j: (0, j)),
                               pl.BlockSpec((SHORT_KERNEL, cl), lambda j: (0, j))],
        out_shape=[jax.ShapeDtypeStruct((t, d_conf), MXU_DTYPE)] * 4
        + [jax.ShapeDtypeStruct((CONF_KERNEL, d_conf), F32), jax.ShapeDtypeStruct((SHORT_KERNEL, d_conf), F32)],
        scratch_shapes=[pltpu.VMEM((t + CONV_PAD, cl), F32)] * 4,
        compiler_params=pltpu.CompilerParams(dimension_semantics=("parallel",)),
    )(proj, proj, proj, proj, d_ac, d_c3, wdw, w3)


def _elementwise(name, fn, ins, out_dtypes, tr, deps=()):
    ins = [(a, ()) if not isinstance(a, tuple) else a for a in ins]
    r, c = ins[0][0].shape[-2:]
    assert r % tr == 0, (name, r, tr)
    n_in = len(ins)

    def body(*refs):
        tiles = fn(*[x[...] for x in refs[:n_in]])
        for o_ref, tile in zip(refs[n_in + len(deps):], tiles):
            o_ref[...] = tile.astype(o_ref.dtype)

    def spec(lead):
        return pl.BlockSpec((None,) * len(lead) + (tr, c), lambda i: (*lead, i, 0))

    res = pl.pallas_call(
        body,
        name=name,
        grid=(r // tr,),
        in_specs=[spec(lead) for _, lead in ins] + [ANY_SPEC] * len(deps),
        out_specs=[pl.BlockSpec((tr, c), lambda i: (i, 0))] * len(out_dtypes),
        out_shape=[jax.ShapeDtypeStruct((r, c), d) for d in out_dtypes],
        compiler_params=pltpu.CompilerParams(dimension_semantics=("parallel",)),
    )(*[a for a, _ in ins], *deps)
    return res


def _adamw_tiles(w, g, m, v):
    m = ADAM_B1 * m + (1.0 - ADAM_B1) * g
    v = ADAM_B2 * v + (1.0 - ADAM_B2) * jnp.square(g)
    m_hat = m / (1.0 - ADAM_B1 ** ADAM_STEP)
    v_hat = v / (1.0 - ADAM_B2 ** ADAM_STEP)
    delta = -ADAM_LR * (m_hat / (jnp.sqrt(v_hat) + ADAM_EPS) + ADAM_WD * w)
    return g, delta, m, v


def _adamw(name, w, g, m, v, tr, deps=()):
    shape = w.shape
    flat = [a.reshape(shape[-2:]) if a.ndim > 2 else a for a in (w, g, m, v)]
    res = _elementwise(name, _adamw_tiles, flat, [F32] * 4, tr, deps)
    return [a.reshape(shape) for a in res]


def _pair_sum(name, p, q, core, tr):
    n_p, _, hr, c = p.shape
    assert hr % tr == 0

    def body(core_ref, p_ref, q_ref, o_ref):
        o_ref[...] = (p_ref[...] + q_ref[...]).astype(o_ref.dtype)

    return pl.pallas_call(
        body,
        name=name,
        grid_spec=pltpu.PrefetchScalarGridSpec(
            num_scalar_prefetch=1,
            grid=(n_p, hr // tr),
            in_specs=[pl.BlockSpec((None, None, tr, c), lambda a, i, core_ref: (a, core_ref[0], i, 0)),
                      pl.BlockSpec((None, tr, c), lambda a, i, core_ref: (a, i, 0))],
            out_specs=pl.BlockSpec((None, tr, c), lambda a, i, core_ref: (a, i, 0)),
        ),
        out_shape=jax.ShapeDtypeStruct((n_p, hr, c), WIRE_DTYPE),
        compiler_params=pltpu.CompilerParams(dimension_semantics=("parallel", "parallel")),
    )(core, p, q)


def _into_slot(name, w, slots, slot, dtype, tr, deps=()):
    r, c = w.shape
    assert r % tr == 0

    def body(slot_ref, w_ref, *rest):
        o_ref = rest[len(deps)]
        o_ref[...] = w_ref[...].astype(o_ref.dtype)

    return pl.pallas_call(
        body,
        name=name,
        grid_spec=pltpu.PrefetchScalarGridSpec(
            num_scalar_prefetch=1,
            grid=(r // tr,),
            in_specs=[pl.BlockSpec((tr, c), lambda i, slot_ref: (i, 0))] + [ANY_SPEC] * len(deps),
            out_specs=pl.BlockSpec((None, tr, c), lambda i, slot_ref: (slot_ref[0], i, 0)),
        ),
        out_shape=jax.ShapeDtypeStruct((slots, r, c), dtype),
        compiler_params=pltpu.CompilerParams(dimension_semantics=("parallel",)),
    )(slot, w, *deps)


def _sum_pieces(name, own, rb, place, tr):
    n_p, hr, c = rb.shape
    assert hr % tr == 0

    def body(place_ref, own_ref, *refs):
        chip = place_ref[0]
        acc = None
        for k in range(n_p):
            tile = jnp.where(chip == k, own_ref[...], refs[k][...]).astype(F32)
            acc = tile if acc is None else acc + tile
        refs[n_p][...] = acc

    def landed(k):
        return pl.BlockSpec((None, tr, c), lambda i, place_ref: (jnp.where(place_ref[0] == k, (k + 1) % n_p, k), i, 0))

    return pl.pallas_call(
        body,
        name=name,
        grid_spec=pltpu.PrefetchScalarGridSpec(
            num_scalar_prefetch=1,
            grid=(hr // tr,),
            in_specs=[pl.BlockSpec((None, tr, c), lambda i, place_ref: (place_ref[0], i, 0))]
            + [landed(k) for k in range(n_p)],
            out_specs=pl.BlockSpec((None, tr, c), lambda i, place_ref: (place_ref[1], i, 0)),
        ),
        out_shape=jax.ShapeDtypeStruct((2, hr, c), F32),
        compiler_params=pltpu.CompilerParams(dimension_semantics=("parallel",)),
    )(place, own, *([rb] * n_p))


HBM_SPEC = pl.BlockSpec(memory_space=pl.ANY)


def _place():
    x, y, c = lax.axis_index("x"), lax.axis_index("y"), lax.axis_index("c")
    chips = [(1 - x, y), (x, 1 - y), (1 - x, 1 - y)]
    return x, y, c, chips


def _join_halves(name, bufs, deps=()):
    n = len(bufs)

    def body(*refs):
        outs = refs[n + len(deps):2 * n + len(deps)]
        send_sems, recv_sems = refs[2 * n + len(deps):]
        x, y, c, _ = _place()
        copies = [pltpu.make_async_remote_copy(
            src_ref=outs[a].at[c], dst_ref=outs[a].at[c], send_sem=send_sems.at[a], recv_sem=recv_sems.at[a],
            device_id=(x, y, 1 - c), device_id_type=MESH) for a in range(n)]
        for cp in copies:
            cp.start()
        for a in range(n):
            other = outs[a].at[1 - c]
            pltpu.make_async_remote_copy(
                src_ref=other, dst_ref=other, send_sem=send_sems.at[a], recv_sem=recv_sems.at[a],
                device_id=(x, y, 1 - c), device_id_type=MESH).wait_recv()
        for cp in copies:
            cp.wait_send()

    return pl.pallas_call(
        body,
        name=name,
        in_specs=[HBM_SPEC] * n + [ANY_SPEC] * len(deps),
        out_specs=[HBM_SPEC] * n,
        out_shape=[jax.ShapeDtypeStruct(b.shape, b.dtype) for b in bufs],
        input_output_aliases={a: a for a in range(n)},
        scratch_shapes=[pltpu.SemaphoreType.DMA((n,)), pltpu.SemaphoreType.DMA((n,))],
    )(*bufs, *deps)


HBM_ONLY = pl.BlockSpec(memory_space=pltpu.HBM)
SEM_SPEC = pl.BlockSpec(memory_space=pltpu.SEMAPHORE)
DATAFLOW = pltpu.SideEffectType.DATAFLOW_SIDE_EFFECTING


def _in_hbm(a):
    return pltpu.with_memory_space_constraint(a, pltpu.HBM)


def _shard_part(ref, is_split, slot, h):
    if not is_split:
        return ref.at[slot]
    hr = ref.shape[1] // 2
    return ref.at[slot, pl.ds(h * hr, hr), :]


TOKEN = jax.ShapeDtypeStruct((8, LANES), F32)
VMEM_SPEC = pl.BlockSpec(memory_space=pltpu.VMEM)


def _gather_start(name, bufs, split, groups, deps=()):
    n, ng = len(bufs), len(groups)

    def body(*refs):
        ins, sems = refs[:n], refs[n + len(deps):n + len(deps) + 2 * ng]
        refs[-1][...] = jnp.zeros(TOKEN.shape, TOKEN.dtype)
        x, y, c, chips = _place()
        me = 2 * x + y
        for g, members in enumerate(groups):
            for s, a in enumerate(members):
                mine = _shard_part(ins[a], split[a], me, c)
                for j, chip in enumerate(chips):
                    pltpu.make_async_remote_copy(
                        src_ref=mine, dst_ref=mine, send_sem=sems[2 * g].at[3 * s + j], recv_sem=sems[2 * g + 1].at[3 * s + j],
                        device_id=(*chip, c), device_id_type=MESH).start()

    res = pl.pallas_call(
        body,
        name=name,
        in_specs=[HBM_ONLY] * n + [ANY_SPEC] * len(deps),
        out_specs=[SEM_SPEC] * (2 * ng) + [HBM_ONLY] * n + [VMEM_SPEC],
        out_shape=[pltpu.SemaphoreType.DMA((3 * len(members),)) for members in groups for _ in range(2)]
        + [pltpu.HBM(b.shape, b.dtype) for b in bufs] + [TOKEN],
        input_output_aliases={a: 2 * ng + a for a in range(n)},
        compiler_params=pltpu.CompilerParams(has_side_effects=DATAFLOW),
    )(*[_in_hbm(b) for b in bufs], *deps)
    return [(res[2 * g], res[2 * g + 1]) for g in range(ng)], list(res[2 * ng:2 * ng + n]), res[-1]


def _gather_wait(name, bufs, split, sems, after):
    n = len(bufs)

    def body(*refs):
        ins, send_sems, recv_sems = refs[:n], refs[n], refs[n + 1]
        x, y, c, chips = _place()
        me = 2 * x + y
        for s in range(n):
            for j, chip in enumerate(chips):
                copy = pltpu.make_async_remote_copy(
                    src_ref=_shard_part(ins[s], split[s], me, c),
                    dst_ref=_shard_part(ins[s], split[s], 2 * chip[0] + chip[1], c),
                    send_sem=send_sems.at[3 * s + j], recv_sem=recv_sems.at[3 * s + j],
                    device_id=(*chip, c), device_id_type=MESH)
                copy.wait_send()
                copy.wait_recv()

    res = pl.pallas_call(
        body,
        name=name,
        in_specs=[HBM_ONLY] * n + [SEM_SPEC, SEM_SPEC] + [ANY_SPEC] * len(after),
        out_specs=[HBM_ONLY] * n,
        out_shape=[pltpu.HBM(b.shape, b.dtype) for b in bufs],
        input_output_aliases={a: a for a in range(n)},
        compiler_params=pltpu.CompilerParams(has_side_effects=DATAFLOW),
    )(*bufs, *sems, *after)
    return list(res)


def _pass_halves(name, bufs):
    n = len(bufs)

    def body(*refs):
        outs = refs[n:2 * n]
        send_sems, recv_sems = refs[2 * n:]
        x, y, c, chips = _place()
        sibling = (x, y, 1 - c)

        def copy(a, j, h):
            blk = _shard_part(outs[a], True, 2 * chips[j][0] + chips[j][1], h)
            return pltpu.make_async_remote_copy(
                src_ref=blk, dst_ref=blk, send_sem=send_sems.at[3 * a + j], recv_sem=recv_sems.at[3 * a + j],
                device_id=sibling, device_id_type=MESH)

        sends = [copy(a, j, c) for a in range(n) for j in range(3)]
        for cp in sends:
            cp.start()
        for a in range(n):
            for j in range(3):
                copy(a, j, 1 - c).wait_recv()
        for cp in sends:
            cp.wait_send()

    res = pl.pallas_call(
        body,
        name=name,
        in_specs=[HBM_SPEC] * n,
        out_specs=[HBM_SPEC] * n,
        out_shape=[jax.ShapeDtypeStruct(b.shape, b.dtype) for b in bufs],
        input_output_aliases={a: a for a in range(n)},
        scratch_shapes=[pltpu.SemaphoreType.DMA((3 * n,)), pltpu.SemaphoreType.DMA((3 * n,))],
    )(*bufs)
    return list(res)


def _exchange_start(name, pair):
    def body(pair_ref, land_ref, send_sems, recv_sems, pair_thru, land_thru, token):
        x, y, c, chips = _place()
        me = 2 * x + y
        for j, chip in enumerate(chips):
            pltpu.make_async_remote_copy(
                src_ref=pair_ref.at[2 * chip[0] + chip[1]], dst_ref=land_ref.at[me],
                send_sem=send_sems.at[j], recv_sem=recv_sems.at[j], device_id=(*chip, c), device_id_type=MESH).start()
        token[...] = jnp.zeros(TOKEN.shape, TOKEN.dtype)

    send_sems, recv_sems, pair_thru, land_thru, token = pl.pallas_call(
        body,
        name=name,
        in_specs=[HBM_ONLY, HBM_ONLY],
        out_specs=[SEM_SPEC, SEM_SPEC, HBM_ONLY, HBM_ONLY, VMEM_SPEC],
        out_shape=[pltpu.SemaphoreType.DMA((3,)), pltpu.SemaphoreType.DMA((3,)),
                   pltpu.HBM(pair.shape, pair.dtype), pltpu.HBM(pair.shape, pair.dtype), TOKEN],
        input_output_aliases={0: 2, 1: 3},
        compiler_params=pltpu.CompilerParams(has_side_effects=DATAFLOW),
    )(_in_hbm(pair), _in_hbm(lax.empty(pair.shape, pair.dtype)))
    return (send_sems, recv_sems), pair_thru, land_thru, token


def _swap_start(name, halves):
    n_p, _, hr, cols = halves.shape
    land_shape = (n_p, hr, cols)

    def body(halves_ref, land_ref, send_sems, recv_sems, halves_thru, land_thru, token):
        x, y, c, _ = _place()
        for p in range(n_p):
            pltpu.make_async_remote_copy(
                src_ref=halves_ref.at[p, 1 - c], dst_ref=land_ref.at[p], send_sem=send_sems.at[p], recv_sem=recv_sems.at[p],
                device_id=(x, y, 1 - c), device_id_type=MESH).start()
        token[...] = jnp.zeros(TOKEN.shape, TOKEN.dtype)

    send_sems, recv_sems, halves_thru, land_thru, token = pl.pallas_call(
        body,
        name=name,
        in_specs=[HBM_ONLY, HBM_ONLY],
        out_specs=[SEM_SPEC, SEM_SPEC, HBM_ONLY, HBM_ONLY, VMEM_SPEC],
        out_shape=[pltpu.SemaphoreType.DMA((n_p,)), pltpu.SemaphoreType.DMA((n_p,)),
                   pltpu.HBM(halves.shape, halves.dtype), pltpu.HBM(land_shape, halves.dtype), TOKEN],
        input_output_aliases={0: 2, 1: 3},
        compiler_params=pltpu.CompilerParams(has_side_effects=DATAFLOW),
    )(_in_hbm(halves), _in_hbm(lax.empty(land_shape, halves.dtype)))
    return (send_sems, recv_sems), halves_thru, land_thru, token


def _swap_wait(name, halves, land, sems, after):
    n_p = halves.shape[0]

    def body(halves_ref, land_ref, send_sems, recv_sems, *rest):
        x, y, c, _ = _place()
        for p in range(n_p):
            copy = pltpu.make_async_remote_copy(
                src_ref=halves_ref.at[p, 1 - c], dst_ref=land_ref.at[p], send_sem=send_sems.at[p], recv_sem=recv_sems.at[p],
                device_id=(x, y, 1 - c), device_id_type=MESH)
            copy.wait_send()
            copy.wait_recv()

    return pl.pallas_call(
        body,
        name=name,
        in_specs=[HBM_ONLY, HBM_ONLY, SEM_SPEC, SEM_SPEC] + [ANY_SPEC] * len(after),
        out_specs=[HBM_ONLY, HBM_ONLY],
        out_shape=[pltpu.HBM(halves.shape, halves.dtype), pltpu.HBM(land.shape, land.dtype)],
        input_output_aliases={0: 0, 1: 1},
        compiler_params=pltpu.CompilerParams(has_side_effects=DATAFLOW),
    )(halves, land, *sems, *after)


def _exchange_wait(name, pair, land, sems, after):
    def body(pair_ref, land_ref, send_sems, recv_sems, *rest):
        x, y, c, chips = _place()
        for j, chip in enumerate(chips):
            k = 2 * chip[0] + chip[1]
            copy = pltpu.make_async_remote_copy(
                src_ref=pair_ref.at[k], dst_ref=land_ref.at[k], send_sem=send_sems.at[j], recv_sem=recv_sems.at[j],
                device_id=(*chip, c), device_id_type=MESH)
            copy.wait_send()
            copy.wait_recv()

    return pl.pallas_call(
        body,
        name=name,
        in_specs=[HBM_ONLY, HBM_ONLY, SEM_SPEC, SEM_SPEC] + [HBM_SPEC] * len(after),
        out_specs=[HBM_ONLY, HBM_ONLY],
        out_shape=[pltpu.HBM(pair.shape, pair.dtype), pltpu.HBM(land.shape, land.dtype)],
        input_output_aliases={0: 0, 1: 1},
        compiler_params=pltpu.CompilerParams(has_side_effects=DATAFLOW),
    )(pair, land, *sems, *after)


def _allgather_small(name, block, deps=()):
    m_per, n = block.shape

    def body(x_ref, *rest):
        out_ref, send_sems, recv_sems, local_sem = rest[len(deps):]
        x, y, c, chips = _place()
        me, sibling = (x, y, c), (x, y, 1 - c)

        def rows(px, py, pc):
            return out_ref.at[pl.ds((4 * px + 2 * py + pc) * m_per, m_per), :]

        def copy(k, blk, to, src=None):
            return pltpu.make_async_remote_copy(
                src_ref=rows(*blk) if src is None else src, dst_ref=rows(*blk),
                send_sem=send_sems.at[k], recv_sem=recv_sems.at[k], device_id=to, device_id_type=MESH)

        mine = pltpu.make_async_copy(x_ref, rows(*me), local_sem)
        mine.start()
        first = [copy(0, me, sibling, src=x_ref)]
        first += [copy(1 + j, me, (*chip, c), src=x_ref) for j, chip in enumerate(chips)]
        for cp in first:
            cp.start()
        passed = [copy(4 + j, (*chip, c), sibling) for j, chip in enumerate(chips)]
        for j, chip in enumerate(chips):
            copy(1 + j, (*chip, c), me).wait_recv()
            passed[j].start()
        copy(0, sibling, me).wait_recv()
        for j, chip in enumerate(chips):
            copy(4 + j, (*chip, 1 - c), me).wait_recv()
        for cp in first + passed:
            cp.wait_send()
        mine.wait()

    return pl.pallas_call(
        body,
        name=name,
        out_shape=jax.ShapeDtypeStruct((8 * m_per, n), block.dtype),
        in_specs=[pl.BlockSpec(memory_space=pltpu.VMEM)] + [ANY_SPEC] * len(deps),
        out_specs=pl.BlockSpec(memory_space=pltpu.VMEM),
        scratch_shapes=[pltpu.SemaphoreType.DMA((7,)), pltpu.SemaphoreType.DMA((7,)), pltpu.SemaphoreType.DMA],
    )(block, *deps)


def _sum_blocks(name, gathered, n_blocks):
    r = gathered.shape[0] // n_blocks
    c = gathered.shape[1]

    def body(g_ref, o_ref):
        acc = g_ref[0:r, :]
        for b in range(1, n_blocks):
            acc = acc + g_ref[b * r:(b + 1) * r, :]
        o_ref[...] = acc

    return pl.pallas_call(body, name=name, out_shape=jax.ShapeDtypeStruct((r, c), F32))(gathered)


def _largest_tile(n, cap, mult):
    best = None
    for d in range(mult, min(n, cap) + 1, mult):
        if n % d == 0:
            best = d
    assert best is not None, (n, cap, mult)
    return best


def kernel(x, meta, g_pre_mix, w_in, b_gates, conf_dw_w, conf_dw_b, conf_ln_g, conf_ln_b, conf_w_pw, short_dw_w, short_w_out, w_o, g_post_mix, g_pre_mlp, w_up, w_down, g_post_mlp, loss_target, m_meta, m_g_pre_mix, m_w_in, m_b_gates, m_conf_dw_w, m_conf_dw_b, m_conf_ln_g, m_conf_ln_b, m_conf_w_pw, m_short_dw_w, m_short_w_out, m_w_o, m_g_post_mix, m_g_pre_mlp, m_w_up, m_w_down, m_g_post_mlp, v_meta, v_g_pre_mix, v_w_in, v_b_gates, v_conf_dw_w, v_conf_dw_b, v_conf_ln_g, v_conf_ln_b, v_conf_w_pw, v_short_dw_w, v_short_w_out, v_w_o, v_g_post_mix, v_g_pre_mlp, v_w_up, v_w_down, v_g_post_mlp):
    seq, d = x.shape[1], x.shape[2]
    t_real = seq + N_META
    t = -(-t_real // LANES) * LANES
    d_conf = conf_dw_b.shape[1]
    d_ff = w_up.shape[2] * N_CHIPS
    in_cols = w_in.shape[2] * N_CHIPS
    assert in_cols == 5 * d_conf + 2 * d and d == 2 * d_conf
    cw = d_conf
    core = lax.axis_index("c")
    chip = 2 * lax.axis_index("x") + lax.axis_index("y")

    tr = _largest_tile(t, 272, ROW_CHUNK)
    tm = t
    tn_in = _largest_tile(in_cols // N_CHIPS, 768, LANES)
    tn_d = _largest_tile(d, 1024, LANES)
    tn_h = _largest_tile(d, 512, LANES)
    tn_ff = _largest_tile(d_ff // N_CHIPS, 1024, LANES)
    tn_pw = _largest_tile(d // N_CHIPS, 512, LANES)

    big = [w_in[0], conf_w_pw[0], short_w_out[0], w_o[0], w_up[0], w_down[0]]
    chip_arr = chip.astype(jnp.int32).reshape(1)

    def cast(a, deps):
        return _into_slot(f"cast_w{a}", big[a], N_CHIPS, chip_arr, MXU_DTYPE, _largest_tile(big[a].shape[0], 256, 16), deps)

    small = [_into_slot(f"place_w{a}", w, N_CHIPS, chip_arr, F32, w.shape[0])
             for a, w in enumerate([meta, conf_dw_w[0], short_dw_w[0]])]
    sems_small, fly_small, tok_small = _gather_start("gather_start_small", small, [False] * 3, [[0, 1, 2]])
    sems_in, fly_in, tok_in = _gather_start("gather_start_in", [cast(0, [tok_small])], [True], [[0]])
    rest_groups = [[0, 1], [2], [3], [4]]
    sems_rest, fly_rest, tok_rest = _gather_start(
        "gather_start_rest", [cast(a, [tok_in]) for a in range(1, 6)], [True] * 5, rest_groups)

    def arrive(g, after):
        members = rest_groups[g]
        got = _gather_wait(f"gather_wait_rest{g}", [fly_rest[a] for a in members], [True] * len(members),
                           sems_rest[g], after)
        return _pass_halves(f"gather_pass_rest{g}", got)

    meta_g, wdw_g, w3_g = _gather_wait("gather_wait_small", fly_small, [False] * 3, sems_small[0], [tok_in])
    meta_full = jnp.transpose(meta_g, (1, 0, 2)).reshape(N_META, d)
    wdw = jnp.transpose(wdw_g, (1, 0, 2)).reshape(CONF_KERNEL, d_conf)
    w3 = jnp.transpose(w3_g, (1, 0, 2)).reshape(SHORT_KERNEL, d_conf)

    tail = jnp.zeros((t - t_real, d), F32)
    h0 = jnp.concatenate([meta_full, x[0], tail], axis=0)
    target = jnp.concatenate([jnp.zeros((N_META, d), F32), loss_target[0], tail], axis=0)

    def norm_in(i, rows, vecs):
        return [_rms_fwd(rows[0], vecs[0])], []

    (n_lp,) = _rowwise("norm_in", norm_in, [(h0, d, 0)], [g_pre_mix], [(d, MXU_DTYPE)], [], tr, deps=[tok_rest])
    (wg_in,) = _pass_halves("gather_pass_in", _gather_wait("gather_wait_in", fly_in, [True], sems_in[0], [n_lp]))
    proj =_matmul("proj", n_lp, wg_in, kind="nn", tm=tm, tn=tn_in, tk=d, out_dtypes=[F32])
    ac, c3 = _conv_fwd(proj, wdw, conf_dw_b, w3, d_conf)

    def ln_parts(ac_t, ln_g, ln_b):
        mu = jnp.mean(ac_t, axis=-1, keepdims=True)
        xc = ac_t - mu
        rstd = lax.rsqrt(jnp.mean(xc * xc, axis=-1, keepdims=True) + LN_EPS)
        xh = xc * rstd
        return xh, rstd, xh * ln_g + ln_b

    def branch_act(i, rows, vecs):
        ac_t, c3_t, bg_t = rows
        _, _, al = ln_parts(ac_t, vecs[0], vecs[1])
        return [al * _sigmoid(al), bg_t * c3_t], []

    a_act, s_lp = _rowwise("branch_act", branch_act, [(ac, cw, 0), (c3, cw, 0), (proj, cw, 2)],
                           [conf_ln_g, conf_ln_b], [(d_conf, MXU_DTYPE), (d_conf, MXU_DTYPE)], [], tr)
    wg_pw, wg_sout = arrive(0, [a_act])
    y_a = _matmul("y_a", a_act, wg_pw, kind="nn", tm=tm, tn=tn_pw, tk=d_conf, out_dtypes=[F32])
    y_b = _matmul("y_b", s_lp, wg_sout, kind="nn", tm=tm, tn=tn_pw, tk=d_conf, out_dtypes=[F32])

    gate_rows = [(proj, cw, 5), (proj, cw, 6), (proj, cw, 7), (proj, cw, 8)]

    def gates_of(rows, b):
        ga = _sigmoid(jnp.concatenate([rows[0], rows[1]], axis=1) + b[:, :d])
        gb = _sigmoid(jnp.concatenate([rows[2], rows[3]], axis=1) + b[:, d:])
        return ga, gb

    def gate(i, rows, vecs):
        ga, gb = gates_of(rows[2:], vecs[0])
        return [ga * rows[0] + gb * rows[1]], []

    (m_lp,) = _rowwise("gate", gate, [(y_a, d, 0), (y_b, d, 0)] + gate_rows, [b_gates], [(d, MXU_DTYPE)], [], tr)
    wg_o = arrive(1, [m_lp])[0].reshape(d, d)
    mix = _matmul("mix", m_lp, wg_o, kind="nn", tm=tm, tn=tn_d, tk=d, out_dtypes=[F32])

    def post_mix(i, rows, vecs):
        h1_t = rows[0] + _rms_fwd(rows[1], vecs[0])
        return [h1_t, _rms_fwd(h1_t, vecs[1])], []

    h1, n2_lp = _rowwise("post_mix", post_mix, [(h0, d, 0), (mix, d, 0)], [g_post_mix, g_pre_mlp],
                         [(d, F32), (d, MXU_DTYPE)], [], tr)
    (wg_up,) = arrive(2, [n2_lp])
    up, f_lp = _matmul("up", n2_lp, wg_up, kind="nn", tm=tm, tn=tn_h, tk=d, out_dtypes=[F32, MXU_DTYPE],
                       epilogue=lambda acc: (acc, jnp.square(jnp.maximum(acc, 0.0))))
    wg_down = arrive(3, [f_lp])[0].reshape(d_ff, d)
    dn = _matmul("down", f_lp, wg_down, kind="nn", tm=tm, tn=tn_h, tk=_largest_tile(d_ff, 2048, LANES),
                 out_dtypes=[F32])

    def head(row0, rows, vecs):
        h1_t, dn_t, tgt = rows
        y = h1_t + _rms_fwd(dn_t, vecs[0])
        row = row0 + lax.broadcasted_iota(jnp.int32, (ROW_CHUNK, 1), 0)
        err = jnp.where(jnp.logical_and(row >= N_META, row < t_real), y - tgt, 0.0)
        dy = err / d
        d_dn, dg = _rms_bwd(dn_t, vecs[0], dy)
        loss_rows = 0.5 * jnp.mean(err * err, axis=-1, keepdims=True)
        return [dy, d_dn], [dg, jnp.broadcast_to(loss_rows, (ROW_CHUNK, LANES))]

    dy, d_dn, dg_post_mlp, loss_vec = _rowwise(
        "head", head, [(h1, d, 0), (dn, d, 0), (target, d, 0)], [g_post_mlp], [(d, F32), (d, MXU_DTYPE)], [d, LANES], tr)
    loss = lax.psum(loss_vec[0, 0], ("x", "y", "c"))

    core_arr = core.astype(jnp.int32).reshape(1)
    place = jnp.stack([chip, core]).astype(jnp.int32)
    in_flight = {}


    def swap_start(a, gw):
        halves = gw.reshape(N_CHIPS, 2, gw.shape[1] // 2, gw.shape[2])
        *in_flight[a], token = _swap_start(f"swap_start{a}", halves)
        return token

    def exchange_start(a, after):
        sems, halves, land = in_flight[a]
        halves, land = _swap_wait(f"swap_wait{a}", halves, land, sems, after)
        pair = _pair_sum(f"pair_sum{a}", halves, land, core_arr, _largest_tile(land.shape[1], 256, 16))
        *in_flight[a], token = _exchange_start(f"exchange_start{a}", pair)
        return token

    def reduce_finish(a, after):
        sems, pair, land = in_flight[a]
        pair, land = _exchange_wait(f"exchange_wait{a}", pair, land, sems, after)
        return _sum_pieces(f"sum_pieces{a}", pair, land, place, _largest_tile(land.shape[1], 256, 16))

    d_up = _matmul("d_up", d_dn, wg_down, kind="nt", tm=tm, tn=tn_h, tk=d, out_dtypes=[MXU_DTYPE], extras=[up],
                   epilogue=lambda acc, up_t: (acc * (2.0 * jnp.maximum(up_t, 0.0)),))
    tk_t = t
    gw_down = _matmul("gw_down", f_lp, d_dn, kind="tn", tm=_largest_tile(d_ff, 2048, LANES), tn=tn_d, tk=tk_t,
                      out_dtypes=[F32])
    tok = swap_start(5, gw_down.reshape(N_CHIPS, d_ff // N_CHIPS, d))
    d_n2 = _matmul("d_n2", d_up, wg_up, kind="nt", tm=tm, tn=tn_h, tk=d_ff // N_CHIPS, out_dtypes=[F32], deps=[tok])
    tok = exchange_start(5, [d_n2])
    gw_up = _matmul("gw_up", n2_lp, d_up, kind="tn", tm=_largest_tile(d, 2048, LANES), tn=tn_ff, tk=tk_t,
                    out_dtypes=[F32], out_pieces=N_CHIPS, deps=[tok])
    tok = swap_start(4, gw_up)

    def bwd_mid(i, rows, vecs):
        dy_t, dn2_t, h1_t, mix_t = rows
        d_h1a, dg_pre_mlp = _rms_bwd(h1_t, vecs[1], dn2_t)
        d_h1 = dy_t + d_h1a
        d_mix, dg_post_mix = _rms_bwd(mix_t, vecs[0], d_h1)
        return [d_h1, d_mix], [dg_pre_mlp, dg_post_mix]

    d_h1, d_mix, dg_pre_mlp, dg_post_mix = _rowwise(
        "bwd_mid", bwd_mid, [(dy, d, 0), (d_n2, d, 0), (h1, d, 0), (mix, d, 0)], [g_post_mix, g_pre_mlp],
        [(d, F32), (d, MXU_DTYPE)], [d, d], tr, deps=[tok])
    d_m = _matmul("d_m", d_mix, wg_o, kind="nt", tm=tm, tn=tn_h, tk=d, out_dtypes=[F32])
    tok = exchange_start(4, [d_m])
    gw_o = _matmul("gw_o", m_lp, d_mix, kind="tn", tm=_largest_tile(d, 2048, LANES), tn=tn_h, tk=tk_t, out_dtypes=[F32],
                   deps=[tok])
    tok = swap_start(3, gw_o.reshape(N_CHIPS, d // N_CHIPS, d))

    def gate_bwd(i, rows, vecs):
        dm_t, ya_t, yb_t = rows[:3]
        ga, gb = gates_of(rows[3:], vecs[0])
        d_gpre = jnp.concatenate([dm_t * ya_t * ga * (1.0 - ga), dm_t * yb_t * gb * (1.0 - gb)], axis=1)
        return [dm_t * ga, dm_t * gb, d_gpre], [d_gpre]

    d_ya, d_yb, d_gpre, dg_b_gates = _rowwise(
        "gate_bwd", gate_bwd, [(d_m, d, 0), (y_a, d, 0), (y_b, d, 0)] + gate_rows, [b_gates],
        [(d, MXU_DTYPE), (d, MXU_DTYPE), (2 * d, MXU_DTYPE)], [2 * d], tr, deps=[tok])
    tok = exchange_start(3, [d_ya])
    d_aact = _matmul("d_aact", d_ya, wg_pw, kind="nt", tm=tm, tn=d_conf // 2, tk=tn_pw, out_dtypes=[F32], deps=[tok])
    gw_pw = _matmul("gw_pw", a_act, d_ya, kind="tn", tm=_largest_tile(d_conf, 2048, LANES), tn=tn_pw, tk=tk_t,
                    out_dtypes=[F32], out_pieces=N_CHIPS)
    tok = swap_start(1, gw_pw)
    d_s = _matmul("d_s", d_yb, wg_sout, kind="nt", tm=tm, tn=d_conf // 2, tk=tn_pw, out_dtypes=[F32], deps=[tok])
    tok = exchange_start(1, [d_s])
    gw_sout = _matmul("gw_sout", s_lp, d_yb, kind="tn", tm=_largest_tile(d_conf, 2048, LANES), tn=tn_pw, tk=tk_t,
                      out_dtypes=[F32], out_pieces=N_CHIPS, deps=[tok])
    tok = swap_start(2, gw_sout)

    def branch_bwd(i, rows, vecs):
        daact_t, ds_t, ac_t, c3_t, bg_t = rows
        xh, rstd, al = ln_parts(ac_t, vecs[0], vecs[1])
        sg = _sigmoid(al)
        d_al = daact_t * (sg * (1.0 + al * (1.0 - sg)))
        dxh = d_al * vecs[0]
        d_ac = rstd * (dxh - jnp.mean(dxh, axis=-1, keepdims=True) - xh * jnp.mean(dxh * xh, axis=-1, keepdims=True))
        return [d_ac, ds_t * bg_t, ds_t * c3_t], [d_al * xh, d_al, d_ac]

    d_ac, d_c3, d_bg, dg_ln_g, dg_ln_b, dg_dw_b = _rowwise(
        "branch_bwd", branch_bwd, [(d_aact, cw, 0), (d_s, cw, 0), (ac, cw, 0), (c3, cw, 0), (proj, cw, 2)],
        [conf_ln_g, conf_ln_b], [(d_conf, F32), (d_conf, F32), (d_conf, MXU_DTYPE)], [d_conf] * 3, tr, deps=[tok])
    tok = exchange_start(2, [d_ac])
    d_av, d_ag, d_cg, d_v, dg_wdw, dg_w3 = _conv_bwd(proj, d_ac, d_c3, wdw, w3, d_conf)
    d_proj = jnp.concatenate([d_av, d_ag, d_bg, d_cg, d_v, d_gpre], axis=1)
    small_w = d_conf

    def pack(arrs):
        flat = jnp.concatenate([a.reshape(-1, small_w) for a in arrs], axis=0)
        return jnp.pad(flat, ((0, -flat.shape[0] % 8), (0, 0)))

    def unpack(buf, like):
        out, r0 = [], 0
        for a in like:
            nr = a.size // small_w
            out.append(buf[r0:r0 + nr].reshape(a.shape))
            r0 += nr
        return out

    def reduce_small(tag, arrs, deps):
        assert all(a.size % small_w == 0 for a in arrs)
        gathered = _allgather_small(f"allgather_small_{tag}", pack(arrs), deps)
        return unpack(_sum_blocks(f"sum_small_{tag}", gathered, 8), arrs)

    rep_w = [b_gates, conf_dw_b, conf_ln_g, conf_ln_b, g_post_mix, g_pre_mlp, g_post_mlp]
    rep_m = [m_b_gates, m_conf_dw_b, m_conf_ln_g, m_conf_ln_b, m_g_post_mix, m_g_pre_mlp, m_g_post_mlp]
    rep_v = [v_b_gates, v_conf_dw_b, v_conf_ln_g, v_conf_ln_b, v_g_post_mix, v_g_pre_mlp, v_g_post_mlp]
    *g_rep, g_wdw_full, g_w3_full = reduce_small(
        "mid", [dg_b_gates, dg_dw_b, dg_ln_g, dg_ln_b, dg_post_mix, dg_pre_mlp, dg_post_mlp, dg_wdw, dg_w3], [tok])
    sc = d_conf // N_CHIPS
    g_wdw = lax.dynamic_slice_in_dim(g_wdw_full, chip * sc, sc, axis=1)
    g_w3 = lax.dynamic_slice_in_dim(g_w3_full, chip * sc, sc, axis=1)

    gw_in = _matmul("gw_in", n_lp, d_proj, kind="tn", tm=_largest_tile(d, 2048, LANES), tn=tn_in, tk=tk_t,
                    out_dtypes=[F32], out_pieces=N_CHIPS, deps=[g_w3_full])
    tok = swap_start(0, gw_in)
    d_n = _matmul("d_n", d_proj, wg_in, kind="nt", tm=tm, tn=tn_h, tk=in_cols // N_CHIPS, out_dtypes=[F32], deps=[tok])
    tok = exchange_start(0, [d_n])

    def bwd_in(i, rows, vecs):
        d_h0a, dg = _rms_bwd(rows[2], vecs[0], rows[1])
        return [rows[0] + d_h0a], [dg]

    d_h0, dg_pre_mix = _rowwise("bwd_in", bwd_in, [(d_h1, d, 0), (d_n, d, 0), (h0, d, 0)], [g_pre_mix],
                                [(d, F32)], [d], tr, deps=[tok])
    grad_x = d_h0[N_META:t_real][None]

    big_m = [m_w_in, m_conf_w_pw, m_short_w_out, m_w_o, m_w_up, m_w_down]
    big_v = [v_w_in, v_conf_w_pw, v_short_w_out, v_w_o, v_w_up, v_w_down]
    big_res = {}

    def reduce_group(members, after):
        reduced = []
        for a in members:
            reduced.append(reduce_finish(a, after))
            after = [reduced[-1]]
        return reduced

    def adam_group(members, joined):
        for a, j in zip(members, joined):
            big_res[a] = _adamw(f"adamw_big{a}", big[a], j.reshape(big[a].shape), big_m[a][0], big_v[a][0],
                                _largest_tile(big[a].shape[0], 256, 8))

    early = [5, 4, 3, 1, 2]
    adam_group(early, _join_halves("join_halves_early", reduce_group(early, [d_h0])))
    rep_pack = [pack(arrs) for arrs in (rep_w, g_rep, rep_m, rep_v)]
    rep_res = [unpack(buf, rep_w) for buf in _adamw("adamw_rep", *rep_pack, rep_pack[0].shape[0])]
    col_res = {a: _adamw(f"adamw_col{a}", w, g, m, v, w.shape[0]) for a, (w, g, m, v) in (
        (1, (conf_dw_w[0], g_wdw, m_conf_dw_w[0], v_conf_dw_w[0])),
        (2, (short_dw_w[0], g_w3, m_short_dw_w[0], v_short_dw_w[0])))}
    reduced_late = reduce_group([0], [rep_res[1][0], col_res[1][1], col_res[2][1]] + [big_res[a][1] for a in early])
    adam_group([0], _join_halves("join_halves_late", reduced_late))
    g_pre_mix_full, g_meta_full = reduce_small("late", [dg_pre_mix, d_h0[:N_META]], [big_res[0][1]])
    g_meta = lax.dynamic_slice_in_dim(g_meta_full, chip * (d // N_CHIPS), d // N_CHIPS, axis=1)
    col_res[0] = _adamw("adamw_col0", meta, g_meta, m_meta, v_meta, meta.shape[0])
    pre_mix_res = _adamw("adamw_pre_mix", g_pre_mix, g_pre_mix_full, m_g_pre_mix, v_g_pre_mix, 1)
    rep_res = [[pre_mix_res[q]] + rep_res[q] for q in range(4)]

    def leaf(q):
        r = lambda a: rep_res[q][a]
        b = lambda a: big_res[a][q][None]
        return [col_res[0][q], r(0), b(0), r(1), col_res[1][q][None], r(2), r(3), r(4), b(1), col_res[2][q][None], b(2),
                b(3), r(5), r(6), b(4), b(5), r(7)]

    return (loss, grad_x, *leaf(0), *leaf(1), *leaf(2), *leaf(3))
```

```python
import jax
import jax.numpy as jnp
from jax import lax
from jax.experimental import pallas as pl
from jax.experimental.pallas import tpu as pltpu

F32 = jnp.float32
BF16 = jnp.bfloat16
MXU_DTYPE = BF16
WIRE_DTYPE = BF16

N_META = 16
CONF_KERNEL = 31
SHORT_KERNEL = 3
CONV_PAD = 32
RMS_EPS = 1e-6
LN_EPS = 1e-5
ADAM_LR = 0.001
ADAM_B1 = 0.9
ADAM_B2 = 0.999
ADAM_EPS = 1e-08
ADAM_WD = 0.01
ADAM_STEP = 10

N_CHIPS = 4
MESH = pl.DeviceIdType.MESH
LANES = 128


def _sigmoid(z):
    return 1.0 / (1.0 + jnp.exp(-z))


ANY_SPEC = pl.BlockSpec(memory_space=pl.ANY)


def _matmul(name, a, b, *, kind, tm, tn, tk, out_dtypes, out_pieces=1, epilogue=None, extras=(), deps=()):
    pieces = b.shape[0] if b.ndim == 3 else 1
    if kind == "nn":
        m, kdim = a.shape
        n = b.shape[-1] * pieces
        dims = (((1,), (0,)), ((), ()))
        a_spec = pl.BlockSpec((tm, tk), lambda i, j, k: (i, k))
        if b.ndim == 2:
            b_spec = pl.BlockSpec((tk, tn), lambda i, j, k: (k, j))
        else:
            npp = b.shape[-1] // tn
            b_spec = pl.BlockSpec((None, tk, tn), lambda i, j, k: (j // npp, k, j % npp))
    elif kind == "nt":
        m, kdim = a.shape
        n = b.shape[-2]
        dims = (((1,), (1,)), ((), ()))
        a_spec = pl.BlockSpec((tm, tk), lambda i, j, k: (i, k))
        if b.ndim == 2:
            b_spec = pl.BlockSpec((tn, tk), lambda i, j, k: (j, k))
        else:
            kpp = b.shape[-1] // tk
            b_spec = pl.BlockSpec((None, tn, tk), lambda i, j, k: (k // kpp, j, k % kpp))
    else:
        kdim, m = a.shape
        n = b.shape[-1]
        dims = (((0,), (0,)), ((), ()))
        a_spec = pl.BlockSpec((tk, tm), lambda i, j, k: (k, i))
        b_spec = pl.BlockSpec((tk, tn), lambda i, j, k: (k, j))
    assert m % tm == 0 and n % tn == 0 and kdim % tk == 0, (name, m, n, kdim, tm, tn, tk)
    nk = kdim // tk
    if out_pieces == 1:
        out_shape = (m, n)
        out_spec = pl.BlockSpec((tm, tn), lambda i, j, k: (i, j))
    else:
        onpp = n // out_pieces // tn
        out_shape = (out_pieces, m, n // out_pieces)
        out_spec = pl.BlockSpec((None, tm, tn), lambda i, j, k: (j // onpp, i, j % onpp))
    n_ex, n_out, n_in = len(extras), len(out_dtypes), len(extras) + len(deps)
    if epilogue is None:
        epilogue = lambda acc: (acc,)

    def body(a_ref, b_ref, *rest):
        ex_refs, o_refs = rest[:n_ex], rest[n_in:n_in + n_out]
        prod = lax.dot_general(a_ref[...], b_ref[...], dims, preferred_element_type=F32)

        def finish(acc):
            tiles = epilogue(acc, *[r[...] for r in ex_refs])
            for o_ref, t in zip(o_refs, tiles):
                o_ref[...] = t.astype(o_ref.dtype)

        if nk == 1:
            finish(prod)
        else:
            acc_ref = rest[n_in + n_out]
            k = pl.program_id(2)

            @pl.when(k == 0)
            def _():
                acc_ref[...] = prod

            @pl.when(jnp.logical_and(k > 0, k < nk - 1))
            def _():
                acc_ref[...] += prod

            @pl.when(k == nk - 1)
            def _():
                finish(acc_ref[...] + prod)

    ex_specs = [pl.BlockSpec((tm, tn), lambda i, j, k: (i, j)) for _ in extras]
    res = pl.pallas_call(
        body,
        name=name,
        grid=(m // tm, n // tn, nk),
        in_specs=[a_spec, b_spec, *ex_specs] + [ANY_SPEC] * len(deps),
        out_specs=[out_spec] * n_out,
        out_shape=[jax.ShapeDtypeStruct(out_shape, d) for d in out_dtypes],
        scratch_shapes=[pltpu.VMEM((tm, tn), F32)] if nk > 1 else [],
        compiler_params=pltpu.CompilerParams(dimension_semantics=("parallel", "parallel", "arbitrary")),
    )(a, b, *extras, *deps)
    return res[0] if n_out == 1 else res


ROW_CHUNK = 16
SUBLANES = 8


def _rowwise(name, fn, rows, vecs, outs, sums, tr, deps=()):
    t = rows[0][0].shape[0]
    assert t % tr == 0 and tr % ROW_CHUNK == 0
    n_r, n_v, n_o, n_s = len(rows), len(vecs), len(outs), len(sums)
    n_in = n_r + n_v + len(deps)
    n_steps = t // tr

    def body(*refs):
        r_in, v_in = refs[:n_r], refs[n_r:n_r + n_v]
        o_refs = refs[n_in:n_in + n_o]
        s_refs = refs[n_in + n_o:n_in + n_o + n_s]
        acc_refs = refs[n_in + n_o + n_s:]
        i = pl.program_id(0)

        @pl.when(i == 0)
        def _():
            for acc_ref in acc_refs:
                acc_ref[...] = jnp.zeros(acc_ref.shape, F32)

        def chunk(ci):
            r0 = ci * ROW_CHUNK
            sl = pl.ds(r0, ROW_CHUNK)
            o_tiles, s_tiles = fn(i * tr + r0, [r[sl, :] for r in r_in], [v[...] for v in v_in])
            for o_ref, tile in zip(o_refs, o_tiles):
                o_ref[sl, :] = tile.astype(o_ref.dtype)
            for acc_ref, tile in zip(acc_refs, s_tiles):
                part = tile[0:SUBLANES]
                for s in range(1, ROW_CHUNK // SUBLANES):
                    part = part + tile[s * SUBLANES:(s + 1) * SUBLANES]
                acc_ref[...] += part

        for ci in range(tr // ROW_CHUNK):
            chunk(ci)

        @pl.when(i == n_steps - 1)
        def _():
            for s_ref, acc_ref in zip(s_refs, acc_refs):
                s_ref[...] = jnp.sum(acc_ref[...], axis=0, keepdims=True)

    def row_spec(width, blk):
        return pl.BlockSpec((tr, width), lambda i: (i, blk))

    res = pl.pallas_call(
        body,
        name=name,
        grid=(t // tr,),
        in_specs=[row_spec(w, blk) for _, w, blk in rows]
        + [pl.BlockSpec(v.shape, lambda i: (0, 0)) for v in vecs] + [ANY_SPEC] * len(deps),
        out_specs=[pl.BlockSpec((tr, c), lambda i: (i, 0)) for c, _ in outs]
        + [pl.BlockSpec((1, c), lambda i: (0, 0)) for c in sums],
        out_shape=[jax.ShapeDtypeStruct((t, c), d) for c, d in outs]
        + [jax.ShapeDtypeStruct((1, c), F32) for c in sums],
        scratch_shapes=[pltpu.VMEM((SUBLANES, c), F32) for c in sums],
        compiler_params=pltpu.CompilerParams(dimension_semantics=("arbitrary",)),
    )(*[r[0] for r in rows], *vecs, *deps)
    return res


def _rms_fwd(x, g):
    r = lax.rsqrt(jnp.mean(x * x, axis=-1, keepdims=True) + RMS_EPS)
    return x * r * g


def _rms_bwd(x, g, dy):
    r = lax.rsqrt(jnp.mean(x * x, axis=-1, keepdims=True) + RMS_EPS)
    xn = x * r
    dxn = dy * g
    dx = r * (dxn - xn * jnp.mean(dxn * xn, axis=-1, keepdims=True))
    return dx, dy * xn


CONV_ROWS = 64
CONV_LANES = 128


def _conv_fwd(proj, wdw, bdw, w3, d_conf):
    t = proj.shape[0]
    cl = CONV_LANES
    nb = d_conf // cl
    nchunk = t // CONV_ROWS
    assert t % CONV_ROWS == 0

    def body(av_ref, ag_ref, cg_ref, v_ref, wdw_ref, bdw_ref, w3_ref, ac_ref, c3_ref, apad, cpad):
        zeros = jnp.zeros((CONV_PAD, cl), F32)
        apad[0:CONV_PAD, :] = zeros
        cpad[0:CONV_PAD, :] = zeros
        apad[CONV_PAD:, :] = av_ref[...] * _sigmoid(ag_ref[...])
        cpad[CONV_PAD:, :] = cg_ref[...] * v_ref[...]

        def chunk(ci, carry):
            base = pl.multiple_of(ci * CONV_ROWS, 8)
            acc = jnp.zeros((CONV_ROWS, cl), F32) + bdw_ref[...]
            for k in range(CONF_KERNEL):
                off = CONV_PAD - (CONF_KERNEL - 1) + k
                acc = acc + apad[pl.ds(base + off, CONV_ROWS), :] * wdw_ref[k:k + 1, :]
            ac_ref[pl.ds(base, CONV_ROWS), :] = acc
            acc3 = jnp.zeros((CONV_ROWS, cl), F32)
            for k in range(SHORT_KERNEL):
                off = CONV_PAD - (SHORT_KERNEL - 1) + k
                acc3 = acc3 + cpad[pl.ds(base + off, CONV_ROWS), :] * w3_ref[k:k + 1, :]
            c3_ref[pl.ds(base, CONV_ROWS), :] = acc3
            return carry

        lax.fori_loop(0, nchunk, chunk, 0)

    def col(blk0):
        return pl.BlockSpec((t, cl), lambda j: (0, blk0 + j))

    return pl.pallas_call(
        body,
        name="conv_fwd",
        grid=(nb,),
        in_specs=[col(0), col(nb), col(3 * nb), col(4 * nb),
                  pl.BlockSpec((CONF_KERNEL, cl), lambda j: (0, j)),
                  pl.BlockSpec((1, cl), lambda j: (0, j)),
                  pl.BlockSpec((SHORT_KERNEL, cl), lambda j: (0, j))],
        out_specs=[pl.BlockSpec((t, cl), lambda j: (0, j))] * 2,
        out_shape=[jax.ShapeDtypeStruct((t, d_conf), F32)] * 2,
        scratch_shapes=[pltpu.VMEM((t + CONV_PAD, cl), F32)] * 2,
        compiler_params=pltpu.CompilerParams(dimension_semantics=("parallel",)),
    )(proj, proj, proj, proj, wdw, bdw, w3)


def _conv_bwd(proj, d_ac, d_c3, wdw, w3, d_conf):
    t = proj.shape[0]
    cl = CONV_LANES
    nb = d_conf // cl
    nchunk = t // CONV_ROWS
    nsub = CONV_ROWS // 8

    def fold(p):
        r = p[0:8]
        for s in range(1, nsub):
            r = r + p[8 * s:8 * s + 8]
        return r

    def body(av_ref, ag_ref, cg_ref, v_ref, dac_ref, dc3_ref, wdw_ref, w3_ref,
             dav_ref, dag_ref, dcg_ref, dv_ref, dwdw_ref, dw3_ref, apad, cpad, dapad, dcpad):
        zeros = jnp.zeros((CONV_PAD, cl), F32)
        apad[0:CONV_PAD, :] = zeros
        cpad[0:CONV_PAD, :] = zeros
        apad[CONV_PAD:, :] = av_ref[...] * _sigmoid(ag_ref[...])
        cpad[CONV_PAD:, :] = cg_ref[...] * v_ref[...]
        dapad[0:t, :] = dac_ref[...]
        dcpad[0:t, :] = dc3_ref[...]
        dapad[t:, :] = zeros
        dcpad[t:, :] = zeros

        def chunk(ci, accs):
            base = pl.multiple_of(ci * CONV_ROWS, 8)
            rows = pl.ds(base, CONV_ROWS)
            da = jnp.zeros((CONV_ROWS, cl), F32)
            for k in range(CONF_KERNEL):
                da = da + dapad[pl.ds(base + (CONF_KERNEL - 1 - k), CONV_ROWS), :] * wdw_ref[k:k + 1, :]
            dcv = jnp.zeros((CONV_ROWS, cl), F32)
            for k in range(SHORT_KERNEL):
                dcv = dcv + dcpad[pl.ds(base + (SHORT_KERNEL - 1 - k), CONV_ROWS), :] * w3_ref[k:k + 1, :]
            av, sg = av_ref[rows, :], _sigmoid(ag_ref[rows, :])
            dav_ref[rows, :] = (da * sg).astype(dav_ref.dtype)
            dag_ref[rows, :] = (da * av * sg * (1.0 - sg)).astype(dag_ref.dtype)
            dcg_ref[rows, :] = (dcv * v_ref[rows, :]).astype(dcg_ref.dtype)
            dv_ref[rows, :] = (dcv * cg_ref[rows, :]).astype(dv_ref.dtype)
            d_out, d_out3 = dac_ref[rows, :], dc3_ref[rows, :]
            new = []
            for k in range(CONF_KERNEL):
                off = CONV_PAD - (CONF_KERNEL - 1) + k
                new.append(accs[k] + fold(d_out * apad[pl.ds(base + off, CONV_ROWS), :]))
            for k in range(SHORT_KERNEL):
                off = CONV_PAD - (SHORT_KERNEL - 1) + k
                new.append(accs[CONF_KERNEL + k] + fold(d_out3 * cpad[pl.ds(base + off, CONV_ROWS), :]))
            return tuple(new)

        init = tuple(jnp.zeros((8, cl), F32) for _ in range(CONF_KERNEL + SHORT_KERNEL))
        accs = lax.fori_loop(0, nchunk, chunk, init)
        for k in range(CONF_KERNEL):
            dwdw_ref[k:k + 1, :] = jnp.sum(accs[k], axis=0, keepdims=True)
        for k in range(SHORT_KERNEL):
            dw3_ref[k:k + 1, :] = jnp.sum(accs[CONF_KERNEL + k], axis=0, keepdims=True)

    def col(blk0):
        return pl.BlockSpec((t, cl), lambda j: (0, blk0 + j))

    own = pl.BlockSpec((t, cl), lambda j: (0, j))
    return pl.pallas_call(
        body,
        name="conv_bwd",
        grid=(nb,),
        in_specs=[col(0), col(nb), col(3 * nb), col(4 * nb), own, own,
                  pl.BlockSpec((CONF_KERNEL, cl), lambda j: (0, j)),
                  pl.BlockSpec((SHORT_KERNEL, cl), lambda j: (0, j))],
        out_specs=[own] * 4 + [pl.BlockSpec((CONF_KERNEL, cl), lambda j: (0, j)),
                               pl.BlockSpec((SHORT_KERNEL, cl), lambda j: (0, j))],
        out_shape=[jax.ShapeDtypeStruct((t, d_conf), MXU_DTYPE)] * 4
        + [jax.ShapeDtypeStruct((CONF_KERNEL, d_conf), F32), jax.ShapeDtypeStruct((SHORT_KERNEL, d_conf), F32)],
        scratch_shapes=[pltpu.VMEM((t + CONV_PAD, cl), F32)] * 4,
        compiler_params=pltpu.CompilerParams(dimension_semantics=("parallel",)),
    )(proj, proj, proj, proj, d_ac, d_c3, wdw, w3)


def _elementwise(name, fn, ins, out_dtypes, tr, deps=()):
    ins = [(a, ()) if not isinstance(a, tuple) else a for a in ins]
    r, c = ins[0][0].shape[-2:]
    assert r % tr == 0, (name, r, tr)
    n_in = len(ins)

    def body(*refs):
        tiles = fn(*[x[...] for x in refs[:n_in]])
        for o_ref, tile in zip(refs[n_in + len(deps):], tiles):
            o_ref[...] = tile.astype(o_ref.dtype)

    def spec(lead):
        return pl.BlockSpec((None,) * len(lead) + (tr, c), lambda i: (*lead, i, 0))

    res = pl.pallas_call(
        body,
        name=name,
        grid=(r // tr,),
        in_specs=[spec(lead) for _, lead in ins] + [ANY_SPEC] * len(deps),
        out_specs=[pl.BlockSpec((tr, c), lambda i: (i, 0))] * len(out_dtypes),
        out_shape=[jax.ShapeDtypeStruct((r, c), d) for d in out_dtypes],
        compiler_params=pltpu.CompilerParams(dimension_semantics=("parallel",)),
    )(*[a for a, _ in ins], *deps)
    return res


def _adamw_tiles(w, g, m, v):
    m = ADAM_B1 * m + (1.0 - ADAM_B1) * g
    v = ADAM_B2 * v + (1.0 - ADAM_B2) * jnp.square(g)
    m_hat = m / (1.0 - ADAM_B1 ** ADAM_STEP)
    v_hat = v / (1.0 - ADAM_B2 ** ADAM_STEP)
    delta = -ADAM_LR * (m_hat / (jnp.sqrt(v_hat) + ADAM_EPS) + ADAM_WD * w)
    return g, delta, m, v


def _adamw(name, w, g, m, v, tr, deps=()):
    shape = w.shape
    flat = [a.reshape(shape[-2:]) if a.ndim > 2 else a for a in (w, g, m, v)]
    res = _elementwise(name, _adamw_tiles, flat, [F32] * 4, tr, deps)
    return [a.reshape(shape) for a in res]


def _pair_sum(name, p, q, core, tr):
    n_p, _, hr, c = p.shape
    assert hr % tr == 0

    def body(core_ref, p_ref, q_ref, o_ref):
        o_ref[...] = (p_ref[...] + q_ref[...]).astype(o_ref.dtype)

    return pl.pallas_call(
        body,
        name=name,
        grid_spec=pltpu.PrefetchScalarGridSpec(
            num_scalar_prefetch=1,
            grid=(n_p, hr // tr),
            in_specs=[pl.BlockSpec((None, None, tr, c), lambda a, i, core_ref: (a, core_ref[0], i, 0)),
                      pl.BlockSpec((None, tr, c), lambda a, i, core_ref: (a, i, 0))],
            out_specs=pl.BlockSpec((None, tr, c), lambda a, i, core_ref: (a, i, 0)),
        ),
        out_shape=jax.ShapeDtypeStruct((n_p, hr, c), WIRE_DTYPE),
        compiler_params=pltpu.CompilerParams(dimension_semantics=("parallel", "parallel")),
    )(core, p, q)


def _into_slot(name, w, slots, slot, dtype, tr, deps=()):
    r, c = w.shape
    assert r % tr == 0

    def body(slot_ref, w_ref, *rest):
        o_ref = rest[len(deps)]
        o_ref[...] = w_ref[...].astype(o_ref.dtype)

    return pl.pallas_call(
        body,
        name=name,
        grid_spec=pltpu.PrefetchScalarGridSpec(
            num_scalar_prefetch=1,
            grid=(r // tr,),
            in_specs=[pl.BlockSpec((tr, c), lambda i, slot_ref: (i, 0))] + [ANY_SPEC] * len(deps),
            out_specs=pl.BlockSpec((None, tr, c), lambda i, slot_ref: (slot_ref[0], i, 0)),
        ),
        out_shape=jax.ShapeDtypeStruct((slots, r, c), dtype),
        compiler_params=pltpu.CompilerParams(dimension_semantics=("parallel",)),
    )(slot, w, *deps)


def _sum_pieces(name, own, rb, place, tr):
    n_p, hr, c = rb.shape
    assert hr % tr == 0

    def body(place_ref, own_ref, *refs):
        chip = place_ref[0]
        acc = None
        for k in range(n_p):
            tile = jnp.where(chip == k, own_ref[...], refs[k][...]).astype(F32)
            acc = tile if acc is None else acc + tile
        refs[n_p][...] = acc

    def landed(k):
        return pl.BlockSpec((None, tr, c), lambda i, place_ref: (jnp.where(place_ref[0] == k, (k + 1) % n_p, k), i, 0))

    return pl.pallas_call(
        body,
        name=name,
        grid_spec=pltpu.PrefetchScalarGridSpec(
            num_scalar_prefetch=1,
            grid=(hr // tr,),
            in_specs=[pl.BlockSpec((None, tr, c), lambda i, place_ref: (place_ref[0], i, 0))]
            + [landed(k) for k in range(n_p)],
            out_specs=pl.BlockSpec((None, tr, c), lambda i, place_ref: (place_ref[1], i, 0)),
        ),
        out_shape=jax.ShapeDtypeStruct((2, hr, c), F32),
        compiler_params=pltpu.CompilerParams(dimension_semantics=("parallel",)),
    )(place, own, *([rb] * n_p))


HBM_SPEC = pl.BlockSpec(memory_space=pl.ANY)


def _place():
    x, y, c = lax.axis_index("x"), lax.axis_index("y"), lax.axis_index("c")
    chips = [(1 - x, y), (x, 1 - y), (1 - x, 1 - y)]
    return x, y, c, chips


def _join_halves(name, bufs, deps=()):
    n = len(bufs)

    def body(*refs):
        outs = refs[n + len(deps):2 * n + len(deps)]
        send_sems, recv_sems = refs[2 * n + len(deps):]
        x, y, c, _ = _place()
        copies = [pltpu.make_async_remote_copy(
            src_ref=outs[a].at[c], dst_ref=outs[a].at[c], send_sem=send_sems.at[a], recv_sem=recv_sems.at[a],
            device_id=(x, y, 1 - c), device_id_type=MESH) for a in range(n)]
        for cp in copies:
            cp.start()
        for a in range(n):
            other = outs[a].at[1 - c]
            pltpu.make_async_remote_copy(
                src_ref=other, dst_ref=other, send_sem=send_sems.at[a], recv_sem=recv_sems.at[a],
                device_id=(x, y, 1 - c), device_id_type=MESH).wait_recv()
        for cp in copies:
            cp.wait_send()

    return pl.pallas_call(
        body,
        name=name,
        in_specs=[HBM_SPEC] * n + [ANY_SPEC] * len(deps),
        out_specs=[HBM_SPEC] * n,
        out_shape=[jax.ShapeDtypeStruct(b.shape, b.dtype) for b in bufs],
        input_output_aliases={a: a for a in range(n)},
        scratch_shapes=[pltpu.SemaphoreType.DMA((n,)), pltpu.SemaphoreType.DMA((n,))],
    )(*bufs, *deps)


HBM_ONLY = pl.BlockSpec(memory_space=pltpu.HBM)
SEM_SPEC = pl.BlockSpec(memory_space=pltpu.SEMAPHORE)
DATAFLOW = pltpu.SideEffectType.DATAFLOW_SIDE_EFFECTING


def _in_hbm(a):
    return pltpu.with_memory_space_constraint(a, pltpu.HBM)


def _shard_part(ref, is_split, slot, h):
    if not is_split:
        return ref.at[slot]
    hr = ref.shape[1] // 2
    return ref.at[slot, pl.ds(h * hr, hr), :]


TOKEN = jax.ShapeDtypeStruct((8, LANES), F32)
VMEM_SPEC = pl.BlockSpec(memory_space=pltpu.VMEM)


def _gather_start(name, bufs, split, groups, deps=()):
    n, ng = len(bufs), len(groups)

    def body(*refs):
        ins, sems = refs[:n], refs[n + len(deps):n + len(deps) + 2 * ng]
        refs[-1][...] = jnp.zeros(TOKEN.shape, TOKEN.dtype)
        x, y, c, chips = _place()
        me = 2 * x + y
        for g, members in enumerate(groups):
            for s, a in enumerate(members):
                mine = _shard_part(ins[a], split[a], me, c)
                for j, chip in enumerate(chips):
                    pltpu.make_async_remote_copy(
                        src_ref=mine, dst_ref=mine, send_sem=sems[2 * g].at[3 * s + j], recv_sem=sems[2 * g + 1].at[3 * s + j],
                        device_id=(*chip, c), device_id_type=MESH).start()

    res = pl.pallas_call(
        body,
        name=name,
        in_specs=[HBM_ONLY] * n + [ANY_SPEC] * len(deps),
        out_specs=[SEM_SPEC] * (2 * ng) + [HBM_ONLY] * n + [VMEM_SPEC],
        out_shape=[pltpu.SemaphoreType.DMA((3 * len(members),)) for members in groups for _ in range(2)]
        + [pltpu.HBM(b.shape, b.dtype) for b in bufs] + [TOKEN],
        input_output_aliases={a: 2 * ng + a for a in range(n)},
        compiler_params=pltpu.CompilerParams(has_side_effects=DATAFLOW),
    )(*[_in_hbm(b) for b in bufs], *deps)
    return [(res[2 * g], res[2 * g + 1]) for g in range(ng)], list(res[2 * ng:2 * ng + n]), res[-1]


def _gather_wait(name, bufs, split, sems, after):
    n = len(bufs)

    def body(*refs):
        ins, send_sems, recv_sems = refs[:n], refs[n], refs[n + 1]
        x, y, c, chips = _place()
        me = 2 * x + y
        for s in range(n):
            for j, chip in enumerate(chips):
                copy = pltpu.make_async_remote_copy(
                    src_ref=_shard_part(ins[s], split[s], me, c),
                    dst_ref=_shard_part(ins[s], split[s], 2 * chip[0] + chip[1], c),
                    send_sem=send_sems.at[3 * s + j], recv_sem=recv_sems.at[3 * s + j],
                    device_id=(*chip, c), device_id_type=MESH)
                copy.wait_send()
                copy.wait_recv()

    res = pl.pallas_call(
        body,
        name=name,
        in_specs=[HBM_ONLY] * n + [SEM_SPEC, SEM_SPEC] + [ANY_SPEC] * len(after),
        out_specs=[HBM_ONLY] * n,
        out_shape=[pltpu.HBM(b.shape, b.dtype) for b in bufs],
        input_output_aliases={a: a for a in range(n)},
        compiler_params=pltpu.CompilerParams(has_side_effects=DATAFLOW),
    )(*bufs, *sems, *after)
    return list(res)


def _pass_halves(name, bufs):
    n = len(bufs)

    def body(*refs):
        outs = refs[n:2 * n]
        send_sems, recv_sems = refs[2 * n:]
        x, y, c, chips = _place()
        sibling = (x, y, 1 - c)

        def copy(a, j, h):
            blk = _shard_part(outs[a], True, 2 * chips[j][0] + chips[j][1], h)
            return pltpu.make_async_remote_copy(
                src_ref=blk, dst_ref=blk, send_sem=send_sems.at[3 * a + j], recv_sem=recv_sems.at[3 * a + j],
                device_id=sibling, device_id_type=MESH)

        sends = [copy(a, j, c) for a in range(n) for j in range(3)]
        for cp in sends:
            cp.start()
        for a in range(n):
            for j in range(3):
                copy(a, j, 1 - c).wait_recv()
        for cp in sends:
            cp.wait_send()

    res = pl.pallas_call(
        body,
        name=name,
        in_specs=[HBM_SPEC] * n,
        out_specs=[HBM_SPEC] * n,
        out_shape=[jax.ShapeDtypeStruct(b.shape, b.dtype) for b in bufs],
        input_output_aliases={a: a for a in range(n)},
        scratch_shapes=[pltpu.SemaphoreType.DMA((3 * n,)), pltpu.SemaphoreType.DMA((3 * n,))],
    )(*bufs)
    return list(res)


def _exchange_start(name, pairs):
    n = len(pairs)

    def body(*refs):
        pair_refs, land_refs, send_sems, recv_sems = refs[:n], refs[n:2 * n], refs[2 * n], refs[2 * n + 1]
        x, y, c, chips = _place()
        me = 2 * x + y
        for a in range(n):
            for j, chip in enumerate(chips):
                pltpu.make_async_remote_copy(
                    src_ref=pair_refs[a].at[2 * chip[0] + chip[1]], dst_ref=land_refs[a].at[me],
                    send_sem=send_sems.at[3 * a + j], recv_sem=recv_sems.at[3 * a + j],
                    device_id=(*chip, c), device_id_type=MESH).start()
        refs[-1][...] = jnp.zeros(TOKEN.shape, TOKEN.dtype)

    res = pl.pallas_call(
        body,
        name=name,
        in_specs=[HBM_ONLY] * (2 * n),
        out_specs=[SEM_SPEC, SEM_SPEC] + [HBM_ONLY] * (2 * n) + [VMEM_SPEC],
        out_shape=[pltpu.SemaphoreType.DMA((3 * n,)), pltpu.SemaphoreType.DMA((3 * n,))]
        + [pltpu.HBM(p.shape, p.dtype) for p in pairs] * 2 + [TOKEN],
        input_output_aliases={a: 2 + a for a in range(2 * n)},
        compiler_params=pltpu.CompilerParams(has_side_effects=DATAFLOW),
    )(*[_in_hbm(p) for p in pairs], *[_in_hbm(lax.empty(p.shape, p.dtype)) for p in pairs])
    return (res[0], res[1]), list(res[2:2 + n]), list(res[2 + n:2 + 2 * n]), res[-1]


def _exchange_wait(name, groups, after):
    sizes = [len(pairs) for _, pairs, _ in groups]
    n_buf = 2 * sum(sizes)

    def body(*refs):
        x, y, c, chips = _place()
        at_buf, at_sem = 0, n_buf
        for n in sizes:
            pair_refs, land_refs = refs[at_buf:at_buf + n], refs[at_buf + n:at_buf + 2 * n]
            send_sems, recv_sems = refs[at_sem], refs[at_sem + 1]
            at_buf, at_sem = at_buf + 2 * n, at_sem + 2
            for a in range(n):
                for j, chip in enumerate(chips):
                    k = 2 * chip[0] + chip[1]
                    copy = pltpu.make_async_remote_copy(
                        src_ref=pair_refs[a].at[k], dst_ref=land_refs[a].at[k],
                        send_sem=send_sems.at[3 * a + j], recv_sem=recv_sems.at[3 * a + j],
                        device_id=(*chip, c), device_id_type=MESH)
                    copy.wait_send()
                    copy.wait_recv()

    bufs = [b for _, pairs, lands in groups for b in (*pairs, *lands)]
    sems = [s for group_sems, _, _ in groups for s in group_sems]
    res = pl.pallas_call(
        body,
        name=name,
        in_specs=[HBM_ONLY] * n_buf + [SEM_SPEC] * len(sems) + [ANY_SPEC] * len(after),
        out_specs=[HBM_ONLY] * n_buf,
        out_shape=[pltpu.HBM(b.shape, b.dtype) for b in bufs],
        input_output_aliases={a: a for a in range(n_buf)},
        compiler_params=pltpu.CompilerParams(has_side_effects=DATAFLOW),
    )(*bufs, *sems, *after)
    out, at = [], 0
    for n in sizes:
        out.append((list(res[at:at + n]), list(res[at + n:at + 2 * n])))
        at += 2 * n
    return out


def _swap_start(name, halves):
    n = len(halves)
    n_p = halves[0].shape[0]
    land_shapes = [(n_p, *h.shape[2:]) for h in halves]

    def body(*refs):
        half_refs, land_refs, send_sems, recv_sems = refs[:n], refs[n:2 * n], refs[2 * n], refs[2 * n + 1]
        x, y, c, _ = _place()
        for a in range(n):
            for p in range(n_p):
                pltpu.make_async_remote_copy(
                    src_ref=half_refs[a].at[p, 1 - c], dst_ref=land_refs[a].at[p],
                    send_sem=send_sems.at[n_p * a + p], recv_sem=recv_sems.at[n_p * a + p],
                    device_id=(x, y, 1 - c), device_id_type=MESH).start()
        refs[-1][...] = jnp.zeros(TOKEN.shape, TOKEN.dtype)

    res = pl.pallas_call(
        body,
        name=name,
        in_specs=[HBM_ONLY] * (2 * n),
        out_specs=[SEM_SPEC, SEM_SPEC] + [HBM_ONLY] * (2 * n) + [VMEM_SPEC],
        out_shape=[pltpu.SemaphoreType.DMA((n_p * n,)), pltpu.SemaphoreType.DMA((n_p * n,))]
        + [pltpu.HBM(h.shape, h.dtype) for h in halves]
        + [pltpu.HBM(s, h.dtype) for s, h in zip(land_shapes, halves)] + [TOKEN],
        input_output_aliases={a: 2 + a for a in range(2 * n)},
        compiler_params=pltpu.CompilerParams(has_side_effects=DATAFLOW),
    )(*[_in_hbm(h) for h in halves], *[_in_hbm(lax.empty(s, h.dtype)) for s, h in zip(land_shapes, halves)])
    return (res[0], res[1]), list(res[2:2 + n]), list(res[2 + n:2 + 2 * n]), res[-1]


def _swap_wait(name, halves, lands, sems, after):
    n = len(halves)
    n_p = halves[0].shape[0]

    def body(*refs):
        half_refs, land_refs, send_sems, recv_sems = refs[:n], refs[n:2 * n], refs[2 * n], refs[2 * n + 1]
        x, y, c, _ = _place()
        for a in range(n):
            for p in range(n_p):
                copy = pltpu.make_async_remote_copy(
                    src_ref=half_refs[a].at[p, 1 - c], dst_ref=land_refs[a].at[p],
                    send_sem=send_sems.at[n_p * a + p], recv_sem=recv_sems.at[n_p * a + p],
                    device_id=(x, y, 1 - c), device_id_type=MESH)
                copy.wait_send()
                copy.wait_recv()

    res = pl.pallas_call(
        body,
        name=name,
        in_specs=[HBM_ONLY] * (2 * n) + [SEM_SPEC, SEM_SPEC] + [ANY_SPEC] * len(after),
        out_specs=[HBM_ONLY] * (2 * n),
        out_shape=[pltpu.HBM(b.shape, b.dtype) for b in (*halves, *lands)],
        input_output_aliases={a: a for a in range(2 * n)},
        compiler_params=pltpu.CompilerParams(has_side_effects=DATAFLOW),
    )(*halves, *lands, *sems, *after)
    return list(res[:n]), list(res[n:])


def _allgather_small(name, block, deps=()):
    m_per, n = block.shape

    def body(x_ref, *rest):
        out_ref, send_sems, recv_sems, local_sem = rest[len(deps):]
        x, y, c, chips = _place()
        me, sibling = (x, y, c), (x, y, 1 - c)

        def rows(px, py, pc):
            return out_ref.at[pl.ds((4 * px + 2 * py + pc) * m_per, m_per), :]

        def copy(k, blk, to, src=None):
            return pltpu.make_async_remote_copy(
                src_ref=rows(*blk) if src is None else src, dst_ref=rows(*blk),
                send_sem=send_sems.at[k], recv_sem=recv_sems.at[k], device_id=to, device_id_type=MESH)

        mine = pltpu.make_async_copy(x_ref, rows(*me), local_sem)
        mine.start()
        first = [copy(0, me, sibling, src=x_ref)]
        first += [copy(1 + j, me, (*chip, c), src=x_ref) for j, chip in enumerate(chips)]
        for cp in first:
            cp.start()
        passed = [copy(4 + j, (*chip, c), sibling) for j, chip in enumerate(chips)]
        for j, chip in enumerate(chips):
            copy(1 + j, (*chip, c), me).wait_recv()
            passed[j].start()
        copy(0, sibling, me).wait_recv()
        for j, chip in enumerate(chips):
            copy(4 + j, (*chip, 1 - c), me).wait_recv()
        for cp in first + passed:
            cp.wait_send()
        mine.wait()

    return pl.pallas_call(
        body,
        name=name,
        out_shape=jax.ShapeDtypeStruct((8 * m_per, n), block.dtype),
        in_specs=[pl.BlockSpec(memory_space=pltpu.VMEM)] + [ANY_SPEC] * len(deps),
        out_specs=pl.BlockSpec(memory_space=pltpu.VMEM),
        scratch_shapes=[pltpu.SemaphoreType.DMA((7,)), pltpu.SemaphoreType.DMA((7,)), pltpu.SemaphoreType.DMA],
    )(block, *deps)


def _sum_blocks(name, gathered, n_blocks):
    r = gathered.shape[0] // n_blocks
    c = gathered.shape[1]

    def body(g_ref, o_ref):
        acc = g_ref[0:r, :]
        for b in range(1, n_blocks):
            acc = acc + g_ref[b * r:(b + 1) * r, :]
        o_ref[...] = acc

    return pl.pallas_call(body, name=name, out_shape=jax.ShapeDtypeStruct((r, c), F32))(gathered)


def _largest_tile(n, cap, mult):
    best = None
    for d in range(mult, min(n, cap) + 1, mult):
        if n % d == 0:
            best = d
    assert best is not None, (n, cap, mult)
    return best


def kernel(x, meta, g_pre_mix, w_in, b_gates, conf_dw_w, conf_dw_b, conf_ln_g, conf_ln_b, conf_w_pw, short_dw_w, short_w_out, w_o, g_post_mix, g_pre_mlp, w_up, w_down, g_post_mlp, loss_target, m_meta, m_g_pre_mix, m_w_in, m_b_gates, m_conf_dw_w, m_conf_dw_b, m_conf_ln_g, m_conf_ln_b, m_conf_w_pw, m_short_dw_w, m_short_w_out, m_w_o, m_g_post_mix, m_g_pre_mlp, m_w_up, m_w_down, m_g_post_mlp, v_meta, v_g_pre_mix, v_w_in, v_b_gates, v_conf_dw_w, v_conf_dw_b, v_conf_ln_g, v_conf_ln_b, v_conf_w_pw, v_short_dw_w, v_short_w_out, v_w_o, v_g_post_mix, v_g_pre_mlp, v_w_up, v_w_down, v_g_post_mlp):
    seq, d = x.shape[1], x.shape[2]
    t_real = seq + N_META
    t = -(-t_real // LANES) * LANES
    d_conf = conf_dw_b.shape[1]
    d_ff = w_up.shape[2] * N_CHIPS
    in_cols = w_in.shape[2] * N_CHIPS
    assert in_cols == 5 * d_conf + 2 * d and d == 2 * d_conf
    cw = d_conf
    core = lax.axis_index("c")
    chip = 2 * lax.axis_index("x") + lax.axis_index("y")

    tr = _largest_tile(t, 272, ROW_CHUNK)
    tm = t
    tn_in = _largest_tile(in_cols // N_CHIPS, 768, LANES)
    tn_d = _largest_tile(d, 1024, LANES)
    tn_h = _largest_tile(d, 512, LANES)
    tn_ff = _largest_tile(d_ff // N_CHIPS, 1024, LANES)
    tn_pw = _largest_tile(d // N_CHIPS, 512, LANES)

    big = [w_in[0], conf_w_pw[0], short_w_out[0], w_o[0], w_up[0], w_down[0]]
    chip_arr = chip.astype(jnp.int32).reshape(1)

    def cast(a, deps):
        return _into_slot(f"cast_w{a}", big[a], N_CHIPS, chip_arr, MXU_DTYPE, _largest_tile(big[a].shape[0], 256, 16), deps)

    small = [_into_slot(f"place_w{a}", w, N_CHIPS, chip_arr, F32, w.shape[0])
             for a, w in enumerate([meta, conf_dw_w[0], short_dw_w[0]])]
    sems_small, fly_small, tok_small = _gather_start("gather_start_small", small, [False] * 3, [[0, 1, 2]])
    sems_in, fly_in, tok_in = _gather_start("gather_start_in", [cast(0, [tok_small])], [True], [[0]])
    rest_groups = [[0, 1], [2], [3], [4]]
    sems_rest, fly_rest, tok_rest = _gather_start(
        "gather_start_rest", [cast(a, [tok_in]) for a in range(1, 6)], [True] * 5, rest_groups)

    def arrive(g, after):
        members = rest_groups[g]
        got = _gather_wait(f"gather_wait_rest{g}", [fly_rest[a] for a in members], [True] * len(members),
                           sems_rest[g], after)
        return _pass_halves(f"gather_pass_rest{g}", got)

    meta_g, wdw_g, w3_g = _gather_wait("gather_wait_small", fly_small, [False] * 3, sems_small[0], [tok_in])
    meta_full = jnp.transpose(meta_g, (1, 0, 2)).reshape(N_META, d)
    wdw = jnp.transpose(wdw_g, (1, 0, 2)).reshape(CONF_KERNEL, d_conf)
    w3 = jnp.transpose(w3_g, (1, 0, 2)).reshape(SHORT_KERNEL, d_conf)

    tail = jnp.zeros((t - t_real, d), F32)
    h0 = jnp.concatenate([meta_full, x[0], tail], axis=0)
    target = jnp.concatenate([jnp.zeros((N_META, d), F32), loss_target[0], tail], axis=0)

    def norm_in(i, rows, vecs):
        return [_rms_fwd(rows[0], vecs[0])], []

    (n_lp,) = _rowwise("norm_in", norm_in, [(h0, d, 0)], [g_pre_mix], [(d, MXU_DTYPE)], [], tr, deps=[tok_rest])
    (wg_in,) = _pass_halves("gather_pass_in", _gather_wait("gather_wait_in", fly_in, [True], sems_in[0], [n_lp]))
    proj =_matmul("proj", n_lp, wg_in, kind="nn", tm=tm, tn=tn_in, tk=d, out_dtypes=[F32])
    ac, c3 = _conv_fwd(proj, wdw, conf_dw_b, w3, d_conf)

    def ln_parts(ac_t, ln_g, ln_b):
        mu = jnp.mean(ac_t, axis=-1, keepdims=True)
        xc = ac_t - mu
        rstd = lax.rsqrt(jnp.mean(xc * xc, axis=-1, keepdims=True) + LN_EPS)
        xh = xc * rstd
        return xh, rstd, xh * ln_g + ln_b

    def branch_act(i, rows, vecs):
        ac_t, c3_t, bg_t = rows
        _, _, al = ln_parts(ac_t, vecs[0], vecs[1])
        return [al * _sigmoid(al), bg_t * c3_t], []

    a_act, s_lp = _rowwise("branch_act", branch_act, [(ac, cw, 0), (c3, cw, 0), (proj, cw, 2)],
                           [conf_ln_g, conf_ln_b], [(d_conf, MXU_DTYPE), (d_conf, MXU_DTYPE)], [], tr)
    wg_pw, wg_sout = arrive(0, [a_act])
    y_a = _matmul("y_a", a_act, wg_pw, kind="nn", tm=tm, tn=tn_pw, tk=d_conf, out_dtypes=[F32])
    y_b = _matmul("y_b", s_lp, wg_sout, kind="nn", tm=tm, tn=tn_pw, tk=d_conf, out_dtypes=[F32])

    gate_rows = [(proj, cw, 5), (proj, cw, 6), (proj, cw, 7), (proj, cw, 8)]

    def gates_of(rows, b):
        ga = _sigmoid(jnp.concatenate([rows[0], rows[1]], axis=1) + b[:, :d])
        gb = _sigmoid(jnp.concatenate([rows[2], rows[3]], axis=1) + b[:, d:])
        return ga, gb

    def gate(i, rows, vecs):
        ga, gb = gates_of(rows[2:], vecs[0])
        return [ga * rows[0] + gb * rows[1]], []

    (m_lp,) = _rowwise("gate", gate, [(y_a, d, 0), (y_b, d, 0)] + gate_rows, [b_gates], [(d, MXU_DTYPE)], [], tr)
    wg_o = arrive(1, [m_lp])[0].reshape(d, d)
    mix = _matmul("mix", m_lp, wg_o, kind="nn", tm=tm, tn=tn_d, tk=d, out_dtypes=[F32])

    def post_mix(i, rows, vecs):
        h1_t = rows[0] + _rms_fwd(rows[1], vecs[0])
        return [h1_t, _rms_fwd(h1_t, vecs[1])], []

    h1, n2_lp = _rowwise("post_mix", post_mix, [(h0, d, 0), (mix, d, 0)], [g_post_mix, g_pre_mlp],
                         [(d, F32), (d, MXU_DTYPE)], [], tr)
    (wg_up,) = arrive(2, [n2_lp])
    up, f_lp = _matmul("up", n2_lp, wg_up, kind="nn", tm=tm, tn=tn_h, tk=d, out_dtypes=[F32, MXU_DTYPE],
                       epilogue=lambda acc: (acc, jnp.square(jnp.maximum(acc, 0.0))))
    wg_down = arrive(3, [f_lp])[0].reshape(d_ff, d)
    dn = _matmul("down", f_lp, wg_down, kind="nn", tm=tm, tn=tn_h, tk=_largest_tile(d_ff, 2048, LANES),
                 out_dtypes=[F32])

    def head(row0, rows, vecs):
        h1_t, dn_t, tgt = rows
        y = h1_t + _rms_fwd(dn_t, vecs[0])
        row = row0 + lax.broadcasted_iota(jnp.int32, (ROW_CHUNK, 1), 0)
        err = jnp.where(jnp.logical_and(row >= N_META, row < t_real), y - tgt, 0.0)
        dy = err / d
        d_dn, dg = _rms_bwd(dn_t, vecs[0], dy)
        loss_rows = 0.5 * jnp.mean(err * err, axis=-1, keepdims=True)
        return [dy, d_dn], [dg, jnp.broadcast_to(loss_rows, (ROW_CHUNK, LANES))]

    dy, d_dn, dg_post_mlp, loss_vec = _rowwise(
        "head", head, [(h1, d, 0), (dn, d, 0), (target, d, 0)], [g_post_mlp], [(d, F32), (d, MXU_DTYPE)], [d, LANES], tr)
    loss = lax.psum(loss_vec[0, 0], ("x", "y", "c"))

    core_arr = core.astype(jnp.int32).reshape(1)
    place = jnp.stack([chip, core]).astype(jnp.int32)
    in_flight = {}


    def tag(members):
        return "".join(str(a) for a in members)

    def swap_start(members, gws):
        halves = [gw.reshape(N_CHIPS, 2, gw.shape[1] // 2, gw.shape[2]) for gw in gws]
        *in_flight[members], token = _swap_start(f"swap_start{tag(members)}", halves)
        return token

    def exchange_start(members, after):
        sems, halves, lands = in_flight[members]
        halves, lands = _swap_wait(f"swap_wait{tag(members)}", halves, lands, sems, after)
        pairs = [_pair_sum(f"pair_sum{a}", h, q, core_arr, _largest_tile(q.shape[1], 256, 16))
                 for a, h, q in zip(members, halves, lands)]
        *in_flight[members], token = _exchange_start(f"exchange_start{tag(members)}", pairs)
        return token

    def reduce_finish(groups, after):
        waited = _exchange_wait("exchange_wait" + "_".join(tag(g) for g in groups), [in_flight[g] for g in groups], after)
        return {a: _sum_pieces(f"sum_pieces{a}", pair, land, place, _largest_tile(land.shape[1], 256, 16))
                for g, (pairs, lands) in zip(groups, waited) for a, pair, land in zip(g, pairs, lands)}

    d_up = _matmul("d_up", d_dn, wg_down, kind="nt", tm=tm, tn=tn_h, tk=d, out_dtypes=[MXU_DTYPE], extras=[up],
                   epilogue=lambda acc, up_t: (acc * (2.0 * jnp.maximum(up_t, 0.0)),))
    tk_t = t
    gw_down = _matmul("gw_down", f_lp, d_dn, kind="tn", tm=_largest_tile(d_ff, 2048, LANES), tn=tn_d, tk=tk_t,
                      out_dtypes=[F32])
    tok = swap_start((5,), [gw_down.reshape(N_CHIPS, d_ff // N_CHIPS, d)])
    d_n2 = _matmul("d_n2", d_up, wg_up, kind="nt", tm=tm, tn=tn_h, tk=d_ff // N_CHIPS, out_dtypes=[F32], deps=[tok])
    tok = exchange_start((5,), [d_n2])
    gw_up = _matmul("gw_up", n2_lp, d_up, kind="tn", tm=_largest_tile(d, 2048, LANES), tn=tn_ff, tk=tk_t,
                    out_dtypes=[F32], out_pieces=N_CHIPS, deps=[tok])
    tok = swap_start((4,), [gw_up])

    def bwd_mid(i, rows, vecs):
        dy_t, dn2_t, h1_t, mix_t = rows
        d_h1a, dg_pre_mlp = _rms_bwd(h1_t, vecs[1], dn2_t)
        d_h1 = dy_t + d_h1a
        d_mix, dg_post_mix = _rms_bwd(mix_t, vecs[0], d_h1)
        return [d_h1, d_mix], [dg_pre_mlp, dg_post_mix]

    d_h1, d_mix, dg_pre_mlp, dg_post_mix = _rowwise(
        "bwd_mid", bwd_mid, [(dy, d, 0), (d_n2, d, 0), (h1, d, 0), (mix, d, 0)], [g_post_mix, g_pre_mlp],
        [(d, F32), (d, MXU_DTYPE)], [d, d], tr, deps=[tok])
    d_m = _matmul("d_m", d_mix, wg_o, kind="nt", tm=tm, tn=tn_h, tk=d, out_dtypes=[F32])
    tok = exchange_start((4,), [d_m])
    gw_o = _matmul("gw_o", m_lp, d_mix, kind="tn", tm=_largest_tile(d, 2048, LANES), tn=tn_h, tk=tk_t, out_dtypes=[F32],
                   deps=[tok])

    def gate_bwd(i, rows, vecs):
        dm_t, ya_t, yb_t = rows[:3]
        ga, gb = gates_of(rows[3:], vecs[0])
        d_gpre = jnp.concatenate([dm_t * ya_t * ga * (1.0 - ga), dm_t * yb_t * gb * (1.0 - gb)], axis=1)
        return [dm_t * ga, dm_t * gb, d_gpre], [d_gpre]

    d_ya, d_yb, d_gpre, dg_b_gates = _rowwise(
        "gate_bwd", gate_bwd, [(d_m, d, 0), (y_a, d, 0), (y_b, d, 0)] + gate_rows, [b_gates],
        [(d, MXU_DTYPE), (d, MXU_DTYPE), (2 * d, MXU_DTYPE)], [2 * d], tr, deps=[tok])
    d_aact = _matmul("d_aact", d_ya, wg_pw, kind="nt", tm=tm, tn=d_conf // 2, tk=tn_pw, out_dtypes=[F32], deps=[tok])
    gw_pw = _matmul("gw_pw", a_act, d_ya, kind="tn", tm=_largest_tile(d_conf, 2048, LANES), tn=tn_pw, tk=tk_t,
                    out_dtypes=[F32], out_pieces=N_CHIPS)
    d_s = _matmul("d_s", d_yb, wg_sout, kind="nt", tm=tm, tn=d_conf // 2, tk=tn_pw, out_dtypes=[F32], deps=[tok])
    gw_sout = _matmul("gw_sout", s_lp, d_yb, kind="tn", tm=_largest_tile(d_conf, 2048, LANES), tn=tn_pw, tk=tk_t,
                      out_dtypes=[F32], out_pieces=N_CHIPS, deps=[tok])
    small_big = (3, 1, 2)
    tok = swap_start(small_big, [gw_o.reshape(N_CHIPS, d // N_CHIPS, d), gw_pw, gw_sout])

    def branch_bwd(i, rows, vecs):
        daact_t, ds_t, ac_t, c3_t, bg_t = rows
        xh, rstd, al = ln_parts(ac_t, vecs[0], vecs[1])
        sg = _sigmoid(al)
        d_al = daact_t * (sg * (1.0 + al * (1.0 - sg)))
        dxh = d_al * vecs[0]
        d_ac = rstd * (dxh - jnp.mean(dxh, axis=-1, keepdims=True) - xh * jnp.mean(dxh * xh, axis=-1, keepdims=True))
        return [d_ac, ds_t * bg_t, ds_t * c3_t], [d_al * xh, d_al, d_ac]

    d_ac, d_c3, d_bg, dg_ln_g, dg_ln_b, dg_dw_b = _rowwise(
        "branch_bwd", branch_bwd, [(d_aact, cw, 0), (d_s, cw, 0), (ac, cw, 0), (c3, cw, 0), (proj, cw, 2)],
        [conf_ln_g, conf_ln_b], [(d_conf, F32), (d_conf, F32), (d_conf, MXU_DTYPE)], [d_conf] * 3, tr, deps=[tok])
    tok = exchange_start(small_big, [d_ac])
    d_av, d_ag, d_cg, d_v, dg_wdw, dg_w3 = _conv_bwd(proj, d_ac, d_c3, wdw, w3, d_conf)
    d_proj = jnp.concatenate([d_av, d_ag, d_bg, d_cg, d_v, d_gpre], axis=1)
    small_w = d_conf

    def pack(arrs):
        flat = jnp.concatenate([a.reshape(-1, small_w) for a in arrs], axis=0)
        return jnp.pad(flat, ((0, -flat.shape[0] % 8), (0, 0)))

    def unpack(buf, like):
        out, r0 = [], 0
        for a in like:
            nr = a.size // small_w
            out.append(buf[r0:r0 + nr].reshape(a.shape))
            r0 += nr
        return out

    def reduce_small(tag, arrs, deps):
        assert all(a.size % small_w == 0 for a in arrs)
        gathered = _allgather_small(f"allgather_small_{tag}", pack(arrs), deps)
        return unpack(_sum_blocks(f"sum_small_{tag}", gathered, 8), arrs)

    rep_w = [b_gates, conf_dw_b, conf_ln_g, conf_ln_b, g_post_mix, g_pre_mlp, g_post_mlp]
    rep_m = [m_b_gates, m_conf_dw_b, m_conf_ln_g, m_conf_ln_b, m_g_post_mix, m_g_pre_mlp, m_g_post_mlp]
    rep_v = [v_b_gates, v_conf_dw_b, v_conf_ln_g, v_conf_ln_b, v_g_post_mix, v_g_pre_mlp, v_g_post_mlp]
    *g_rep, g_wdw_full, g_w3_full = reduce_small(
        "mid", [dg_b_gates, dg_dw_b, dg_ln_g, dg_ln_b, dg_post_mix, dg_pre_mlp, dg_post_mlp, dg_wdw, dg_w3], [tok])
    sc = d_conf // N_CHIPS
    g_wdw = lax.dynamic_slice_in_dim(g_wdw_full, chip * sc, sc, axis=1)
    g_w3 = lax.dynamic_slice_in_dim(g_w3_full, chip * sc, sc, axis=1)

    gw_in = _matmul("gw_in", n_lp, d_proj, kind="tn", tm=_largest_tile(d, 2048, LANES), tn=tn_in, tk=tk_t,
                    out_dtypes=[F32], out_pieces=N_CHIPS, deps=[g_w3_full])
    tok = swap_start((0,), [gw_in])
    d_n = _matmul("d_n", d_proj, wg_in, kind="nt", tm=tm, tn=tn_h, tk=in_cols // N_CHIPS, out_dtypes=[F32], deps=[tok])
    tok = exchange_start((0,), [d_n])

    def bwd_in(i, rows, vecs):
        d_h0a, dg = _rms_bwd(rows[2], vecs[0], rows[1])
        return [rows[0] + d_h0a], [dg]

    d_h0, dg_pre_mix = _rowwise("bwd_in", bwd_in, [(d_h1, d, 0), (d_n, d, 0), (h0, d, 0)], [g_pre_mix],
                                [(d, F32)], [d], tr, deps=[tok])
    grad_x = d_h0[N_META:t_real][None]

    big_m = [m_w_in, m_conf_w_pw, m_short_w_out, m_w_o, m_w_up, m_w_down]
    big_v = [v_w_in, v_conf_w_pw, v_short_w_out, v_w_o, v_w_up, v_w_down]
    big_res = {}

    def adam_group(members, joined):
        for a, j in zip(members, joined):
            big_res[a] = _adamw(f"adamw_big{a}", big[a], j.reshape(big[a].shape), big_m[a][0], big_v[a][0],
                                _largest_tile(big[a].shape[0], 256, 8))

    early = [5, 4, 3, 1, 2]
    reduced = reduce_finish([(5,), (4,), small_big], [d_h0])
    adam_group(early, _join_halves("join_halves_early", [reduced[a] for a in early]))
    rep_pack = [pack(arrs) for arrs in (rep_w, g_rep, rep_m, rep_v)]
    rep_res = [unpack(buf, rep_w) for buf in _adamw("adamw_rep", *rep_pack, rep_pack[0].shape[0])]
    col_res = {a: _adamw(f"adamw_col{a}", w, g, m, v, w.shape[0]) for a, (w, g, m, v) in (
        (1, (conf_dw_w[0], g_wdw, m_conf_dw_w[0], v_conf_dw_w[0])),
        (2, (short_dw_w[0], g_w3, m_short_dw_w[0], v_short_dw_w[0])))}
    reduced = reduce_finish([(0,)], [rep_res[1][0], col_res[1][1], col_res[2][1]] + [big_res[a][1] for a in early])
    adam_group([0], _join_halves("join_halves_late", [reduced[0]]))
    g_pre_mix_full, g_meta_full = reduce_small("late", [dg_pre_mix, d_h0[:N_META]], [big_res[0][1]])
    g_meta = lax.dynamic_slice_in_dim(g_meta_full, chip * (d // N_CHIPS), d // N_CHIPS, axis=1)
    col_res[0] = _adamw("adamw_col0", meta, g_meta, m_meta, v_meta, meta.shape[0])
    pre_mix_res = _adamw("adamw_pre_mix", g_pre_mix, g_pre_mix_full, m_g_pre_mix, v_g_pre_mix, 1)
    rep_res = [[pre_mix_res[q]] + rep_res[q] for q in range(4)]

    def leaf(q):
        r = lambda a: rep_res[q][a]
        b = lambda a: big_res[a][q][None]
        return [col_res[0][q], r(0), b(0), r(1), col_res[1][q][None], r(2), r(3), r(4), b(1), col_res[2][q][None], b(2),
                b(3), r(5), r(6), b(4), b(5), r(7)]

    return (loss, grad_x, *leaf(0), *leaf(1), *leaf(2), *leaf(3))
```

```python
import jax
import jax.numpy as jnp
from jax import lax
from jax.experimental import pallas as pl
from jax.experimental.pallas import tpu as pltpu

F32 = jnp.float32
BF16 = jnp.bfloat16
MXU_DTYPE = BF16
WIRE_DTYPE = BF16

N_META = 16
CONF_KERNEL = 31
SHORT_KERNEL = 3
CONV_PAD = 32
RMS_EPS = 1e-6
LN_EPS = 1e-5
ADAM_LR = 0.001
ADAM_B1 = 0.9
ADAM_B2 = 0.999
ADAM_EPS = 1e-08
ADAM_WD = 0.01
ADAM_STEP = 10

N_CHIPS = 4
MESH = pl.DeviceIdType.MESH
LANES = 128


def _sigmoid(z):
    return 1.0 / (1.0 + jnp.exp(-z))


ANY_SPEC = pl.BlockSpec(memory_space=pl.ANY)


def _matmul(name, a, b, *, kind, tm, tn, tk, out_dtypes, out_pieces=1, epilogue=None, extras=(), deps=()):
    pieces = b.shape[0] if b.ndim == 3 else 1
    if kind == "nn":
        m, kdim = a.shape
        n = b.shape[-1] * pieces
        dims = (((1,), (0,)), ((), ()))
        a_spec = pl.BlockSpec((tm, tk), lambda i, j, k: (i, k))
        if b.ndim == 2:
            b_spec = pl.BlockSpec((tk, tn), lambda i, j, k: (k, j))
        else:
            npp = b.shape[-1] // tn
            b_spec = pl.BlockSpec((None, tk, tn), lambda i, j, k: (j // npp, k, j % npp))
    elif kind == "nt":
        m, kdim = a.shape
        n = b.shape[-2]
        dims = (((1,), (1,)), ((), ()))
        a_spec = pl.BlockSpec((tm, tk), lambda i, j, k: (i, k))
        if b.ndim == 2:
            b_spec = pl.BlockSpec((tn, tk), lambda i, j, k: (j, k))
        else:
            kpp = b.shape[-1] // tk
            b_spec = pl.BlockSpec((None, tn, tk), lambda i, j, k: (k // kpp, j, k % kpp))
    else:
        kdim, m = a.shape
        n = b.shape[-1]
        dims = (((0,), (0,)), ((), ()))
        a_spec = pl.BlockSpec((tk, tm), lambda i, j, k: (k, i))
        b_spec = pl.BlockSpec((tk, tn), lambda i, j, k: (k, j))
    assert m % tm == 0 and n % tn == 0 and kdim % tk == 0, (name, m, n, kdim, tm, tn, tk)
    nk = kdim // tk
    if out_pieces == 1:
        out_shape = (m, n)
        out_spec = pl.BlockSpec((tm, tn), lambda i, j, k: (i, j))
    else:
        onpp = n // out_pieces // tn
        out_shape = (out_pieces, m, n // out_pieces)
        out_spec = pl.BlockSpec((None, tm, tn), lambda i, j, k: (j // onpp, i, j % onpp))
    n_ex, n_out, n_in = len(extras), len(out_dtypes), len(extras) + len(deps)
    if epilogue is None:
        epilogue = lambda acc: (acc,)

    def body(a_ref, b_ref, *rest):
        ex_refs, o_refs = rest[:n_ex], rest[n_in:n_in + n_out]
        prod = lax.dot_general(a_ref[...], b_ref[...], dims, preferred_element_type=F32)

        def finish(acc):
            tiles = epilogue(acc, *[r[...] for r in ex_refs])
            for o_ref, t in zip(o_refs, tiles):
                o_ref[...] = t.astype(o_ref.dtype)

        if nk == 1:
            finish(prod)
        else:
            acc_ref = rest[n_in + n_out]
            k = pl.program_id(2)

            @pl.when(k == 0)
            def _():
                acc_ref[...] = prod

            @pl.when(jnp.logical_and(k > 0, k < nk - 1))
            def _():
                acc_ref[...] += prod

            @pl.when(k == nk - 1)
            def _():
                finish(acc_ref[...] + prod)

    ex_specs = [pl.BlockSpec((tm, tn), lambda i, j, k: (i, j)) for _ in extras]
    res = pl.pallas_call(
        body,
        name=name,
        grid=(m // tm, n // tn, nk),
        in_specs=[a_spec, b_spec, *ex_specs] + [ANY_SPEC] * len(deps),
        out_specs=[out_spec] * n_out,
        out_shape=[jax.ShapeDtypeStruct(out_shape, d) for d in out_dtypes],
        scratch_shapes=[pltpu.VMEM((tm, tn), F32)] if nk > 1 else [],
        compiler_params=pltpu.CompilerParams(dimension_semantics=("parallel", "parallel", "arbitrary")),
    )(a, b, *extras, *deps)
    return res[0] if n_out == 1 else res


ROW_CHUNK = 16
SUBLANES = 8


def _rowwise(name, fn, rows, vecs, outs, sums, tr, deps=()):
    t = rows[0][0].shape[0]
    assert t % tr == 0 and tr % ROW_CHUNK == 0
    n_r, n_v, n_o, n_s = len(rows), len(vecs), len(outs), len(sums)
    n_in = n_r + n_v + len(deps)
    n_steps = t // tr

    def body(*refs):
        r_in, v_in = refs[:n_r], refs[n_r:n_r + n_v]
        o_refs = refs[n_in:n_in + n_o]
        s_refs = refs[n_in + n_o:n_in + n_o + n_s]
        acc_refs = refs[n_in + n_o + n_s:]
        i = pl.program_id(0)

        @pl.when(i == 0)
        def _():
            for acc_ref in acc_refs:
                acc_ref[...] = jnp.zeros(acc_ref.shape, F32)

        def chunk(ci):
            r0 = ci * ROW_CHUNK
            sl = pl.ds(r0, ROW_CHUNK)
            o_tiles, s_tiles = fn(i * tr + r0, [r[sl, :] for r in r_in], [v[...] for v in v_in])
            for o_ref, tile in zip(o_refs, o_tiles):
                o_ref[sl, :] = tile.astype(o_ref.dtype)
            for acc_ref, tile in zip(acc_refs, s_tiles):
                part = tile[0:SUBLANES]
                for s in range(1, ROW_CHUNK // SUBLANES):
                    part = part + tile[s * SUBLANES:(s + 1) * SUBLANES]
                acc_ref[...] += part

        for ci in range(tr // ROW_CHUNK):
            chunk(ci)

        @pl.when(i == n_steps - 1)
        def _():
            for s_ref, acc_ref in zip(s_refs, acc_refs):
                s_ref[...] = jnp.sum(acc_ref[...], axis=0, keepdims=True)

    def row_spec(width, blk):
        return pl.BlockSpec((tr, width), lambda i: (i, blk))

    res = pl.pallas_call(
        body,
        name=name,
        grid=(t // tr,),
        in_specs=[row_spec(w, blk) for _, w, blk in rows]
        + [pl.BlockSpec(v.shape, lambda i: (0, 0)) for v in vecs] + [ANY_SPEC] * len(deps),
        out_specs=[pl.BlockSpec((tr, c), lambda i: (i, 0)) for c, _ in outs]
        + [pl.BlockSpec((1, c), lambda i: (0, 0)) for c in sums],
        out_shape=[jax.ShapeDtypeStruct((t, c), d) for c, d in outs]
        + [jax.ShapeDtypeStruct((1, c), F32) for c in sums],
        scratch_shapes=[pltpu.VMEM((SUBLANES, c), F32) for c in sums],
        compiler_params=pltpu.CompilerParams(dimension_semantics=("arbitrary",)),
    )(*[r[0] for r in rows], *vecs, *deps)
    return res


def _rms_fwd(x, g):
    r = lax.rsqrt(jnp.mean(x * x, axis=-1, keepdims=True) + RMS_EPS)
    return x * r * g


def _rms_bwd(x, g, dy):
    r = lax.rsqrt(jnp.mean(x * x, axis=-1, keepdims=True) + RMS_EPS)
    xn = x * r
    dxn = dy * g
    dx = r * (dxn - xn * jnp.mean(dxn * xn, axis=-1, keepdims=True))
    return dx, dy * xn


CONV_ROWS = 64
CONV_LANES = 128


def _conv_fwd(proj, wdw, bdw, w3, d_conf):
    t = proj.shape[0]
    cl = CONV_LANES
    nb = d_conf // cl
    nchunk = t // CONV_ROWS
    assert t % CONV_ROWS == 0

    def body(av_ref, ag_ref, cg_ref, v_ref, wdw_ref, bdw_ref, w3_ref, ac_ref, c3_ref, apad, cpad):
        zeros = jnp.zeros((CONV_PAD, cl), F32)
        apad[0:CONV_PAD, :] = zeros
        cpad[0:CONV_PAD, :] = zeros
        apad[CONV_PAD:, :] = av_ref[...] * _sigmoid(ag_ref[...])
        cpad[CONV_PAD:, :] = cg_ref[...] * v_ref[...]

        def chunk(ci, carry):
            base = pl.multiple_of(ci * CONV_ROWS, 8)
            acc = jnp.zeros((CONV_ROWS, cl), F32) + bdw_ref[...]
            for k in range(CONF_KERNEL):
                off = CONV_PAD - (CONF_KERNEL - 1) + k
                acc = acc + apad[pl.ds(base + off, CONV_ROWS), :] * wdw_ref[k:k + 1, :]
            ac_ref[pl.ds(base, CONV_ROWS), :] = acc
            acc3 = jnp.zeros((CONV_ROWS, cl), F32)
            for k in range(SHORT_KERNEL):
                off = CONV_PAD - (SHORT_KERNEL - 1) + k
                acc3 = acc3 + cpad[pl.ds(base + off, CONV_ROWS), :] * w3_ref[k:k + 1, :]
            c3_ref[pl.ds(base, CONV_ROWS), :] = acc3
            return carry

        lax.fori_loop(0, nchunk, chunk, 0)

    def col(blk0):
        return pl.BlockSpec((t, cl), lambda j: (0, blk0 + j))

    return pl.pallas_call(
        body,
        name="conv_fwd",
        grid=(nb,),
        in_specs=[col(0), col(nb), col(3 * nb), col(4 * nb),
                  pl.BlockSpec((CONF_KERNEL, cl), lambda j: (0, j)),
                  pl.BlockSpec((1, cl), lambda j: (0, j)),
                  pl.BlockSpec((SHORT_KERNEL, cl), lambda j: (0, j))],
        out_specs=[pl.BlockSpec((t, cl), lambda j: (0, j))] * 2,
        out_shape=[jax.ShapeDtypeStruct((t, d_conf), F32)] * 2,
        scratch_shapes=[pltpu.VMEM((t + CONV_PAD, cl), F32)] * 2,
        compiler_params=pltpu.CompilerParams(dimension_semantics=("parallel",)),
    )(proj, proj, proj, proj, wdw, bdw, w3)


def _conv_bwd(proj, d_ac, d_c3, wdw, w3, d_conf):
    t = proj.shape[0]
    cl = CONV_LANES
    nb = d_conf // cl
    nchunk = t // CONV_ROWS
    nsub = CONV_ROWS // 8

    def fold(p):
        r = p[0:8]
        for s in range(1, nsub):
            r = r + p[8 * s:8 * s + 8]
        return r

    def body(av_ref, ag_ref, cg_ref, v_ref, dac_ref, dc3_ref, wdw_ref, w3_ref,
             dav_ref, dag_ref, dcg_ref, dv_ref, dwdw_ref, dw3_ref, apad, cpad, dapad, dcpad):
        zeros = jnp.zeros((CONV_PAD, cl), F32)
        apad[0:CONV_PAD, :] = zeros
        cpad[0:CONV_PAD, :] = zeros
        apad[CONV_PAD:, :] = av_ref[...] * _sigmoid(ag_ref[...])
        cpad[CONV_PAD:, :] = cg_ref[...] * v_ref[...]
        dapad[0:t, :] = dac_ref[...]
        dcpad[0:t, :] = dc3_ref[...]
        dapad[t:, :] = zeros
        dcpad[t:, :] = zeros

        def chunk(ci, accs):
            base = pl.multiple_of(ci * CONV_ROWS, 8)
            rows = pl.ds(base, CONV_ROWS)
            da = jnp.zeros((CONV_ROWS, cl), F32)
            for k in range(CONF_KERNEL):
                da = da + dapad[pl.ds(base + (CONF_KERNEL - 1 - k), CONV_ROWS), :] * wdw_ref[k:k + 1, :]
            dcv = jnp.zeros((CONV_ROWS, cl), F32)
            for k in range(SHORT_KERNEL):
                dcv = dcv + dcpad[pl.ds(base + (SHORT_KERNEL - 1 - k), CONV_ROWS), :] * w3_ref[k:k + 1, :]
            av, sg = av_ref[rows, :], _sigmoid(ag_ref[rows, :])
            dav_ref[rows, :] = (da * sg).astype(dav_ref.dtype)
            dag_ref[rows, :] = (da * av * sg * (1.0 - sg)).astype(dag_ref.dtype)
            dcg_ref[rows, :] = (dcv * v_ref[rows, :]).astype(dcg_ref.dtype)
            dv_ref[rows, :] = (dcv * cg_ref[rows, :]).astype(dv_ref.dtype)
            d_out, d_out3 = dac_ref[rows, :], dc3_ref[rows, :]
            new = []
            for k in range(CONF_KERNEL):
                off = CONV_PAD - (CONF_KERNEL - 1) + k
                new.append(accs[k] + fold(d_out * apad[pl.ds(base + off, CONV_ROWS), :]))
            for k in range(SHORT_KERNEL):
                off = CONV_PAD - (SHORT_KERNEL - 1) + k
                new.append(accs[CONF_KERNEL + k] + fold(d_out3 * cpad[pl.ds(base + off, CONV_ROWS), :]))
            return tuple(new)

        init = tuple(jnp.zeros((8, cl), F32) for _ in range(CONF_KERNEL + SHORT_KERNEL))
        accs = lax.fori_loop(0, nchunk, chunk, init)
        for k in range(CONF_KERNEL):
            dwdw_ref[k:k + 1, :] = jnp.sum(accs[k], axis=0, keepdims=True)
        for k in range(SHORT_KERNEL):
            dw3_ref[k:k + 1, :] = jnp.sum(accs[CONF_KERNEL + k], axis=0, keepdims=True)

    def col(blk0):
        return pl.BlockSpec((t, cl), lambda j: (0, blk0 + j))

    own = pl.BlockSpec((t, cl), lambda j: (0, j))
    return pl.pallas_call(
        body,
        name="conv_bwd",
        grid=(nb,),
        in_specs=[col(0), col(nb), col(3 * nb), col(4 * nb), own, own,
                  pl.BlockSpec((CONF_KERNEL, cl), lambda j: (0, j)),
                  pl.BlockSpec((SHORT_KERNEL, cl), lambda j: (0, j))],
        out_specs=[own] * 4 + [pl.BlockSpec((CONF_KERNEL, cl), lambda j: (0, j)),
                               pl.BlockSpec((SHORT_KERNEL, cl), lambda j: (0, j))],
        out_shape=[jax.ShapeDtypeStruct((t, d_conf), MXU_DTYPE)] * 4
        + [jax.ShapeDtypeStruct((CONF_KERNEL, d_conf), F32), jax.ShapeDtypeStruct((SHORT_KERNEL, d_conf), F32)],
        scratch_shapes=[pltpu.VMEM((t + CONV_PAD, cl), F32)] * 4,
        compiler_params=pltpu.CompilerParams(dimension_semantics=("parallel",)),
    )(proj, proj, proj, proj, d_ac, d_c3, wdw, w3)


def _elementwise(name, fn, ins, out_dtypes, tr, deps=()):
    ins = [(a, ()) if not isinstance(a, tuple) else a for a in ins]
    r, c = ins[0][0].shape[-2:]
    assert r % tr == 0, (name, r, tr)
    n_in = len(ins)

    def body(*refs):
        tiles = fn(*[x[...] for x in refs[:n_in]])
        for o_ref, tile in zip(refs[n_in + len(deps):], tiles):
            o_ref[...] = tile.astype(o_ref.dtype)

    def spec(lead):
        return pl.BlockSpec((None,) * len(lead) + (tr, c), lambda i: (*lead, i, 0))

    res = pl.pallas_call(
        body,
        name=name,
        grid=(r // tr,),
        in_specs=[spec(lead) for _, lead in ins] + [ANY_SPEC] * len(deps),
        out_specs=[pl.BlockSpec((tr, c), lambda i: (i, 0))] * len(out_dtypes),
        out_shape=[jax.ShapeDtypeStruct((r, c), d) for d in out_dtypes],
        compiler_params=pltpu.CompilerParams(dimension_semantics=("parallel",)),
    )(*[a for a, _ in ins], *deps)
    return res


def _adamw_tiles(w, g, m, v):
    m = ADAM_B1 * m + (1.0 - ADAM_B1) * g
    v = ADAM_B2 * v + (1.0 - ADAM_B2) * jnp.square(g)
    m_hat = m / (1.0 - ADAM_B1 ** ADAM_STEP)
    v_hat = v / (1.0 - ADAM_B2 ** ADAM_STEP)
    delta = -ADAM_LR * (m_hat / (jnp.sqrt(v_hat) + ADAM_EPS) + ADAM_WD * w)
    return g, delta, m, v


def _adamw(name, w, g, m, v, tr, deps=()):
    shape = w.shape
    flat = [a.reshape(shape[-2:]) if a.ndim > 2 else a for a in (w, g, m, v)]
    res = _elementwise(name, _adamw_tiles, flat, [F32] * 4, tr, deps)
    return [a.reshape(shape) for a in res]


def _pair_sum(name, p, q, core, tr):
    n_p, _, hr, c = p.shape
    assert hr % tr == 0

    def body(core_ref, p_ref, q_ref, o_ref):
        o_ref[...] = (p_ref[...] + q_ref[...]).astype(o_ref.dtype)

    return pl.pallas_call(
        body,
        name=name,
        grid_spec=pltpu.PrefetchScalarGridSpec(
            num_scalar_prefetch=1,
            grid=(n_p, hr // tr),
            in_specs=[pl.BlockSpec((None, None, tr, c), lambda a, i, core_ref: (a, core_ref[0], i, 0)),
                      pl.BlockSpec((None, tr, c), lambda a, i, core_ref: (a, i, 0))],
            out_specs=pl.BlockSpec((None, tr, c), lambda a, i, core_ref: (a, i, 0)),
        ),
        out_shape=jax.ShapeDtypeStruct((n_p, hr, c), WIRE_DTYPE),
        compiler_params=pltpu.CompilerParams(dimension_semantics=("parallel", "parallel")),
    )(core, p, q)


def _into_slot(name, w, slots, slot, dtype, tr, deps=()):
    r, c = w.shape
    assert r % tr == 0

    def body(slot_ref, w_ref, *rest):
        o_ref = rest[len(deps)]
        o_ref[...] = w_ref[...].astype(o_ref.dtype)

    return pl.pallas_call(
        body,
        name=name,
        grid_spec=pltpu.PrefetchScalarGridSpec(
            num_scalar_prefetch=1,
            grid=(r // tr,),
            in_specs=[pl.BlockSpec((tr, c), lambda i, slot_ref: (i, 0))] + [ANY_SPEC] * len(deps),
            out_specs=pl.BlockSpec((None, tr, c), lambda i, slot_ref: (slot_ref[0], i, 0)),
        ),
        out_shape=jax.ShapeDtypeStruct((slots, r, c), dtype),
        compiler_params=pltpu.CompilerParams(dimension_semantics=("parallel",)),
    )(slot, w, *deps)


def _sum_pieces(name, own, rb, place, tr):
    n_p, hr, c = rb.shape
    assert hr % tr == 0

    def body(place_ref, own_ref, *refs):
        chip = place_ref[0]
        acc = None
        for k in range(n_p):
            tile = jnp.where(chip == k, own_ref[...], refs[k][...]).astype(F32)
            acc = tile if acc is None else acc + tile
        refs[n_p][...] = acc

    def landed(k):
        return pl.BlockSpec((None, tr, c), lambda i, place_ref: (jnp.where(place_ref[0] == k, (k + 1) % n_p, k), i, 0))

    return pl.pallas_call(
        body,
        name=name,
        grid_spec=pltpu.PrefetchScalarGridSpec(
            num_scalar_prefetch=1,
            grid=(hr // tr,),
            in_specs=[pl.BlockSpec((None, tr, c), lambda i, place_ref: (place_ref[0], i, 0))]
            + [landed(k) for k in range(n_p)],
            out_specs=pl.BlockSpec((None, tr, c), lambda i, place_ref: (place_ref[1], i, 0)),
        ),
        out_shape=jax.ShapeDtypeStruct((2, hr, c), F32),
        compiler_params=pltpu.CompilerParams(dimension_semantics=("parallel",)),
    )(place, own, *([rb] * n_p))


HBM_SPEC = pl.BlockSpec(memory_space=pl.ANY)


def _place():
    x, y, c = lax.axis_index("x"), lax.axis_index("y"), lax.axis_index("c")
    chips = [(1 - x, y), (x, 1 - y), (1 - x, 1 - y)]
    return x, y, c, chips


def _join_halves(name, bufs, deps=()):
    n = len(bufs)

    def body(*refs):
        outs = refs[n + len(deps):2 * n + len(deps)]
        send_sems, recv_sems = refs[2 * n + len(deps):]
        x, y, c, _ = _place()
        copies = [pltpu.make_async_remote_copy(
            src_ref=outs[a].at[c], dst_ref=outs[a].at[c], send_sem=send_sems.at[a], recv_sem=recv_sems.at[a],
            device_id=(x, y, 1 - c), device_id_type=MESH) for a in range(n)]
        for cp in copies:
            cp.start()
        for a in range(n):
            other = outs[a].at[1 - c]
            pltpu.make_async_remote_copy(
                src_ref=other, dst_ref=other, send_sem=send_sems.at[a], recv_sem=recv_sems.at[a],
                device_id=(x, y, 1 - c), device_id_type=MESH).wait_recv()
        for cp in copies:
            cp.wait_send()

    return pl.pallas_call(
        body,
        name=name,
        in_specs=[HBM_SPEC] * n + [ANY_SPEC] * len(deps),
        out_specs=[HBM_SPEC] * n,
        out_shape=[jax.ShapeDtypeStruct(b.shape, b.dtype) for b in bufs],
        input_output_aliases={a: a for a in range(n)},
        scratch_shapes=[pltpu.SemaphoreType.DMA((n,)), pltpu.SemaphoreType.DMA((n,))],
    )(*bufs, *deps)


HBM_ONLY = pl.BlockSpec(memory_space=pltpu.HBM)
SEM_SPEC = pl.BlockSpec(memory_space=pltpu.SEMAPHORE)
DATAFLOW = pltpu.SideEffectType.DATAFLOW_SIDE_EFFECTING


def _in_hbm(a):
    return pltpu.with_memory_space_constraint(a, pltpu.HBM)


def _shard_part(ref, is_split, slot, h):
    if not is_split:
        return ref.at[slot]
    hr = ref.shape[1] // 2
    return ref.at[slot, pl.ds(h * hr, hr), :]


TOKEN = jax.ShapeDtypeStruct((8, LANES), F32)
VMEM_SPEC = pl.BlockSpec(memory_space=pltpu.VMEM)


def _gather_start(name, bufs, split, groups, deps=()):
    n, ng = len(bufs), len(groups)

    def body(*refs):
        ins, sems = refs[:n], refs[n + len(deps):n + len(deps) + 2 * ng]
        refs[-1][...] = jnp.zeros(TOKEN.shape, TOKEN.dtype)
        x, y, c, chips = _place()
        me = 2 * x + y
        for g, members in enumerate(groups):
            for s, a in enumerate(members):
                mine = _shard_part(ins[a], split[a], me, c)
                for j, chip in enumerate(chips):
                    pltpu.make_async_remote_copy(
                        src_ref=mine, dst_ref=mine, send_sem=sems[2 * g].at[3 * s + j], recv_sem=sems[2 * g + 1].at[3 * s + j],
                        device_id=(*chip, c), device_id_type=MESH).start()

    res = pl.pallas_call(
        body,
        name=name,
        in_specs=[HBM_ONLY] * n + [ANY_SPEC] * len(deps),
        out_specs=[SEM_SPEC] * (2 * ng) + [HBM_ONLY] * n + [VMEM_SPEC],
        out_shape=[pltpu.SemaphoreType.DMA((3 * len(members),)) for members in groups for _ in range(2)]
        + [pltpu.HBM(b.shape, b.dtype) for b in bufs] + [TOKEN],
        input_output_aliases={a: 2 * ng + a for a in range(n)},
        compiler_params=pltpu.CompilerParams(has_side_effects=DATAFLOW),
    )(*[_in_hbm(b) for b in bufs], *deps)
    return [(res[2 * g], res[2 * g + 1]) for g in range(ng)], list(res[2 * ng:2 * ng + n]), res[-1]


def _gather_wait(name, bufs, split, sems, after):
    n = len(bufs)

    def body(*refs):
        ins, send_sems, recv_sems = refs[:n], refs[n], refs[n + 1]
        x, y, c, chips = _place()
        me = 2 * x + y
        for s in range(n):
            for j, chip in enumerate(chips):
                copy = pltpu.make_async_remote_copy(
                    src_ref=_shard_part(ins[s], split[s], me, c),
                    dst_ref=_shard_part(ins[s], split[s], 2 * chip[0] + chip[1], c),
                    send_sem=send_sems.at[3 * s + j], recv_sem=recv_sems.at[3 * s + j],
                    device_id=(*chip, c), device_id_type=MESH)
                copy.wait_send()
                copy.wait_recv()

    res = pl.pallas_call(
        body,
        name=name,
        in_specs=[HBM_ONLY] * n + [SEM_SPEC, SEM_SPEC] + [ANY_SPEC] * len(after),
        out_specs=[HBM_ONLY] * n,
        out_shape=[pltpu.HBM(b.shape, b.dtype) for b in bufs],
        input_output_aliases={a: a for a in range(n)},
        compiler_params=pltpu.CompilerParams(has_side_effects=DATAFLOW),
    )(*bufs, *sems, *after)
    return list(res)


def _pass_halves(name, bufs):
    n = len(bufs)

    def body(*refs):
        outs = refs[n:2 * n]
        send_sems, recv_sems = refs[2 * n:]
        x, y, c, chips = _place()
        sibling = (x, y, 1 - c)

        def copy(a, j, h):
            blk = _shard_part(outs[a], True, 2 * chips[j][0] + chips[j][1], h)
            return pltpu.make_async_remote_copy(
                src_ref=blk, dst_ref=blk, send_sem=send_sems.at[3 * a + j], recv_sem=recv_sems.at[3 * a + j],
                device_id=sibling, device_id_type=MESH)

        sends = [copy(a, j, c) for a in range(n) for j in range(3)]
        for cp in sends:
            cp.start()
        for a in range(n):
            for j in range(3):
                copy(a, j, 1 - c).wait_recv()
        for cp in sends:
            cp.wait_send()

    res = pl.pallas_call(
        body,
        name=name,
        in_specs=[HBM_SPEC] * n,
        out_specs=[HBM_SPEC] * n,
        out_shape=[jax.ShapeDtypeStruct(b.shape, b.dtype) for b in bufs],
        input_output_aliases={a: a for a in range(n)},
        scratch_shapes=[pltpu.SemaphoreType.DMA((3 * n,)), pltpu.SemaphoreType.DMA((3 * n,))],
    )(*bufs)
    return list(res)


def _exchange_start(name, pairs):
    n = len(pairs)

    def body(*refs):
        pair_refs, land_refs, send_sems, recv_sems = refs[:n], refs[n:2 * n], refs[2 * n], refs[2 * n + 1]
        x, y, c, chips = _place()
        me = 2 * x + y
        for a in range(n):
            for j, chip in enumerate(chips):
                pltpu.make_async_remote_copy(
                    src_ref=pair_refs[a].at[2 * chip[0] + chip[1]], dst_ref=land_refs[a].at[me],
                    send_sem=send_sems.at[3 * a + j], recv_sem=recv_sems.at[3 * a + j],
                    device_id=(*chip, c), device_id_type=MESH).start()
        refs[-1][...] = jnp.zeros(TOKEN.shape, TOKEN.dtype)

    res = pl.pallas_call(
        body,
        name=name,
        in_specs=[HBM_ONLY] * (2 * n),
        out_specs=[SEM_SPEC, SEM_SPEC] + [HBM_ONLY] * (2 * n) + [VMEM_SPEC],
        out_shape=[pltpu.SemaphoreType.DMA((3 * n,)), pltpu.SemaphoreType.DMA((3 * n,))]
        + [pltpu.HBM(p.shape, p.dtype) for p in pairs] * 2 + [TOKEN],
        input_output_aliases={a: 2 + a for a in range(2 * n)},
        compiler_params=pltpu.CompilerParams(has_side_effects=DATAFLOW),
    )(*[_in_hbm(p) for p in pairs], *[_in_hbm(lax.empty(p.shape, p.dtype)) for p in pairs])
    return (res[0], res[1]), list(res[2:2 + n]), list(res[2 + n:2 + 2 * n]), res[-1]


def _exchange_wait(name, groups, after):
    sizes = [len(pairs) for _, pairs, _ in groups]
    n_buf = 2 * sum(sizes)

    def body(*refs):
        x, y, c, chips = _place()
        at_buf, at_sem = 0, n_buf
        for n in sizes:
            pair_refs, land_refs = refs[at_buf:at_buf + n], refs[at_buf + n:at_buf + 2 * n]
            send_sems, recv_sems = refs[at_sem], refs[at_sem + 1]
            at_buf, at_sem = at_buf + 2 * n, at_sem + 2
            for a in range(n):
                for j, chip in enumerate(chips):
                    k = 2 * chip[0] + chip[1]
                    copy = pltpu.make_async_remote_copy(
                        src_ref=pair_refs[a].at[k], dst_ref=land_refs[a].at[k],
                        send_sem=send_sems.at[3 * a + j], recv_sem=recv_sems.at[3 * a + j],
                        device_id=(*chip, c), device_id_type=MESH)
                    copy.wait_send()
                    copy.wait_recv()

    bufs = [b for _, pairs, lands in groups for b in (*pairs, *lands)]
    sems = [s for group_sems, _, _ in groups for s in group_sems]
    res = pl.pallas_call(
        body,
        name=name,
        in_specs=[HBM_ONLY] * n_buf + [SEM_SPEC] * len(sems) + [ANY_SPEC] * len(after),
        out_specs=[HBM_ONLY] * n_buf,
        out_shape=[pltpu.HBM(b.shape, b.dtype) for b in bufs],
        input_output_aliases={a: a for a in range(n_buf)},
        compiler_params=pltpu.CompilerParams(has_side_effects=DATAFLOW),
    )(*bufs, *sems, *after)
    out, at = [], 0
    for n in sizes:
        out.append((list(res[at:at + n]), list(res[at + n:at + 2 * n])))
        at += 2 * n
    return out


def _swap_start(name, halves):
    n = len(halves)
    n_p = halves[0].shape[0]
    land_shapes = [(n_p, *h.shape[2:]) for h in halves]

    def body(*refs):
        half_refs, land_refs, send_sems, recv_sems = refs[:n], refs[n:2 * n], refs[2 * n], refs[2 * n + 1]
        x, y, c, _ = _place()
        for a in range(n):
            for p in range(n_p):
                pltpu.make_async_remote_copy(
                    src_ref=half_refs[a].at[p, 1 - c], dst_ref=land_refs[a].at[p],
                    send_sem=send_sems.at[n_p * a + p], recv_sem=recv_sems.at[n_p * a + p],
                    device_id=(x, y, 1 - c), device_id_type=MESH).start()
        refs[-1][...] = jnp.zeros(TOKEN.shape, TOKEN.dtype)

    res = pl.pallas_call(
        body,
        name=name,
        in_specs=[HBM_ONLY] * (2 * n),
        out_specs=[SEM_SPEC, SEM_SPEC] + [HBM_ONLY] * (2 * n) + [VMEM_SPEC],
        out_shape=[pltpu.SemaphoreType.DMA((n_p * n,)), pltpu.SemaphoreType.DMA((n_p * n,))]
        + [pltpu.HBM(h.shape, h.dtype) for h in halves]
        + [pltpu.HBM(s, h.dtype) for s, h in zip(land_shapes, halves)] + [TOKEN],
        input_output_aliases={a: 2 + a for a in range(2 * n)},
        compiler_params=pltpu.CompilerParams(has_side_effects=DATAFLOW),
    )(*[_in_hbm(h) for h in halves], *[_in_hbm(lax.empty(s, h.dtype)) for s, h in zip(land_shapes, halves)])
    return (res[0], res[1]), list(res[2:2 + n]), list(res[2 + n:2 + 2 * n]), res[-1]


def _swap_wait(name, halves, lands, sems, after):
    n = len(halves)
    n_p = halves[0].shape[0]

    def body(*refs):
        half_refs, land_refs, send_sems, recv_sems = refs[:n], refs[n:2 * n], refs[2 * n], refs[2 * n + 1]
        x, y, c, _ = _place()
        for a in range(n):
            for p in range(n_p):
                copy = pltpu.make_async_remote_copy(
                    src_ref=half_refs[a].at[p, 1 - c], dst_ref=land_refs[a].at[p],
                    send_sem=send_sems.at[n_p * a + p], recv_sem=recv_sems.at[n_p * a + p],
                    device_id=(x, y, 1 - c), device_id_type=MESH)
                copy.wait_send()
                copy.wait_recv()

    res = pl.pallas_call(
        body,
        name=name,
        in_specs=[HBM_ONLY] * (2 * n) + [SEM_SPEC, SEM_SPEC] + [ANY_SPEC] * len(after),
        out_specs=[HBM_ONLY] * (2 * n),
        out_shape=[pltpu.HBM(b.shape, b.dtype) for b in (*halves, *lands)],
        input_output_aliases={a: a for a in range(2 * n)},
        compiler_params=pltpu.CompilerParams(has_side_effects=DATAFLOW),
    )(*halves, *lands, *sems, *after)
    return list(res[:n]), list(res[n:])


def _allgather_small(name, block, deps=()):
    m_per, n = block.shape

    def body(x_ref, *rest):
        out_ref, send_sems, recv_sems, local_sem = rest[len(deps):]
        x, y, c, chips = _place()
        me, sibling = (x, y, c), (x, y, 1 - c)

        def rows(px, py, pc):
            return out_ref.at[pl.ds((4 * px + 2 * py + pc) * m_per, m_per), :]

        def copy(k, blk, to, src=None):
            return pltpu.make_async_remote_copy(
                src_ref=rows(*blk) if src is None else src, dst_ref=rows(*blk),
                send_sem=send_sems.at[k], recv_sem=recv_sems.at[k], device_id=to, device_id_type=MESH)

        mine = pltpu.make_async_copy(x_ref, rows(*me), local_sem)
        mine.start()
        first = [copy(0, me, sibling, src=x_ref)]
        first += [copy(1 + j, me, (*chip, c), src=x_ref) for j, chip in enumerate(chips)]
        for cp in first:
            cp.start()
        passed = [copy(4 + j, (*chip, c), sibling) for j, chip in enumerate(chips)]
        for j, chip in enumerate(chips):
            copy(1 + j, (*chip, c), me).wait_recv()
            passed[j].start()
        copy(0, sibling, me).wait_recv()
        for j, chip in enumerate(chips):
            copy(4 + j, (*chip, 1 - c), me).wait_recv()
        for cp in first + passed:
            cp.wait_send()
        mine.wait()

    return pl.pallas_call(
        body,
        name=name,
        out_shape=jax.ShapeDtypeStruct((8 * m_per, n), block.dtype),
        in_specs=[pl.BlockSpec(memory_space=pltpu.VMEM)] + [ANY_SPEC] * len(deps),
        out_specs=pl.BlockSpec(memory_space=pltpu.VMEM),
        scratch_shapes=[pltpu.SemaphoreType.DMA((7,)), pltpu.SemaphoreType.DMA((7,)), pltpu.SemaphoreType.DMA],
    )(block, *deps)


def _sum_blocks(name, gathered, n_blocks):
    r = gathered.shape[0] // n_blocks
    c = gathered.shape[1]

    def body(g_ref, o_ref):
        acc = g_ref[0:r, :]
        for b in range(1, n_blocks):
            acc = acc + g_ref[b * r:(b + 1) * r, :]
        o_ref[...] = acc

    return pl.pallas_call(body, name=name, out_shape=jax.ShapeDtypeStruct((r, c), F32))(gathered)


def _largest_tile(n, cap, mult):
    best = None
    for d in range(mult, min(n, cap) + 1, mult):
        if n % d == 0:
            best = d
    assert best is not None, (n, cap, mult)
    return best


def kernel(x, meta, g_pre_mix, w_in, b_gates, conf_dw_w, conf_dw_b, conf_ln_g, conf_ln_b, conf_w_pw, short_dw_w, short_w_out, w_o, g_post_mix, g_pre_mlp, w_up, w_down, g_post_mlp, loss_target, m_meta, m_g_pre_mix, m_w_in, m_b_gates, m_conf_dw_w, m_conf_dw_b, m_conf_ln_g, m_conf_ln_b, m_conf_w_pw, m_short_dw_w, m_short_w_out, m_w_o, m_g_post_mix, m_g_pre_mlp, m_w_up, m_w_down, m_g_post_mlp, v_meta, v_g_pre_mix, v_w_in, v_b_gates, v_conf_dw_w, v_conf_dw_b, v_conf_ln_g, v_conf_ln_b, v_conf_w_pw, v_short_dw_w, v_short_w_out, v_w_o, v_g_post_mix, v_g_pre_mlp, v_w_up, v_w_down, v_g_post_mlp):
    seq, d = x.shape[1], x.shape[2]
    t_real = seq + N_META
    t = -(-t_real // LANES) * LANES
    d_conf = conf_dw_b.shape[1]
    d_ff = w_up.shape[2] * N_CHIPS
    in_cols = w_in.shape[2] * N_CHIPS
    assert in_cols == 5 * d_conf + 2 * d and d == 2 * d_conf
    cw = d_conf
    core = lax.axis_index("c")
    chip = 2 * lax.axis_index("x") + lax.axis_index("y")

    tr = _largest_tile(t, 272, ROW_CHUNK)
    tm = t
    tn_in = _largest_tile(in_cols // N_CHIPS, 768, LANES)
    tn_d = _largest_tile(d, 1024, LANES)
    tn_h = _largest_tile(d, 512, LANES)
    tn_ff = _largest_tile(d_ff // N_CHIPS, 1024, LANES)
    tn_pw = _largest_tile(d // N_CHIPS, 512, LANES)

    big = [w_in[0], conf_w_pw[0], short_w_out[0], w_o[0], w_up[0], w_down[0]]
    chip_arr = chip.astype(jnp.int32).reshape(1)

    def cast(a, deps):
        return _into_slot(f"cast_w{a}", big[a], N_CHIPS, chip_arr, MXU_DTYPE, _largest_tile(big[a].shape[0], 256, 16), deps)

    small = [_into_slot(f"place_w{a}", w, N_CHIPS, chip_arr, F32, w.shape[0])
             for a, w in enumerate([meta, conf_dw_w[0], short_dw_w[0]])]
    sems_small, fly_small, tok_small = _gather_start("gather_start_small", small, [False] * 3, [[0, 1, 2]])
    sems_in, fly_in, tok_in = _gather_start("gather_start_in", [cast(0, [tok_small])], [True], [[0]])
    rest_groups = [[0, 1], [2], [3], [4]]
    sems_rest, fly_rest, tok_rest = _gather_start(
        "gather_start_rest", [cast(a, [tok_in]) for a in range(1, 6)], [True] * 5, rest_groups)

    def arrive(g, after):
        members = rest_groups[g]
        got = _gather_wait(f"gather_wait_rest{g}", [fly_rest[a] for a in members], [True] * len(members),
                           sems_rest[g], after)
        return _pass_halves(f"gather_pass_rest{g}", got)

    meta_g, wdw_g, w3_g = _gather_wait("gather_wait_small", fly_small, [False] * 3, sems_small[0], [tok_in])
    meta_full = jnp.transpose(meta_g, (1, 0, 2)).reshape(N_META, d)
    wdw = jnp.transpose(wdw_g, (1, 0, 2)).reshape(CONF_KERNEL, d_conf)
    w3 = jnp.transpose(w3_g, (1, 0, 2)).reshape(SHORT_KERNEL, d_conf)

    tail = jnp.zeros((t - t_real, d), F32)
    h0 = jnp.concatenate([meta_full, x[0], tail], axis=0)
    target = jnp.concatenate([jnp.zeros((N_META, d), F32), loss_target[0], tail], axis=0)

    def norm_in(i, rows, vecs):
        return [_rms_fwd(rows[0], vecs[0])], []

    (n_lp,) = _rowwise("norm_in", norm_in, [(h0, d, 0)], [g_pre_mix], [(d, MXU_DTYPE)], [], tr, deps=[tok_rest])
    (wg_in,) = _pass_halves("gather_pass_in", _gather_wait("gather_wait_in", fly_in, [True], sems_in[0], [n_lp]))
    proj =_matmul("proj", n_lp, wg_in, kind="nn", tm=tm, tn=tn_in, tk=d, out_dtypes=[F32])
    ac, c3 = _conv_fwd(proj, wdw, conf_dw_b, w3, d_conf)

    def ln_parts(ac_t, ln_g, ln_b):
        mu = jnp.mean(ac_t, axis=-1, keepdims=True)
        xc = ac_t - mu
        rstd = lax.rsqrt(jnp.mean(xc * xc, axis=-1, keepdims=True) + LN_EPS)
        xh = xc * rstd
        return xh, rstd, xh * ln_g + ln_b

    def branch_act(i, rows, vecs):
        ac_t, c3_t, bg_t = rows
        _, _, al = ln_parts(ac_t, vecs[0], vecs[1])
        return [al * _sigmoid(al), bg_t * c3_t], []

    a_act, s_lp = _rowwise("branch_act", branch_act, [(ac, cw, 0), (c3, cw, 0), (proj, cw, 2)],
                           [conf_ln_g, conf_ln_b], [(d_conf, MXU_DTYPE), (d_conf, MXU_DTYPE)], [], tr)
    wg_pw, wg_sout = arrive(0, [a_act])
    y_a = _matmul("y_a", a_act, wg_pw, kind="nn", tm=tm, tn=tn_pw, tk=d_conf, out_dtypes=[F32])
    y_b = _matmul("y_b", s_lp, wg_sout, kind="nn", tm=tm, tn=tn_pw, tk=d_conf, out_dtypes=[F32])

    gate_rows = [(proj, cw, 5), (proj, cw, 6), (proj, cw, 7), (proj, cw, 8)]

    def gates_of(rows, b):
        ga = _sigmoid(jnp.concatenate([rows[0], rows[1]], axis=1) + b[:, :d])
        gb = _sigmoid(jnp.concatenate([rows[2], rows[3]], axis=1) + b[:, d:])
        return ga, gb

    def gate(i, rows, vecs):
        ga, gb = gates_of(rows[2:], vecs[0])
        return [ga * rows[0] + gb * rows[1]], []

    (m_lp,) = _rowwise("gate", gate, [(y_a, d, 0), (y_b, d, 0)] + gate_rows, [b_gates], [(d, MXU_DTYPE)], [], tr)
    wg_o = arrive(1, [m_lp])[0].reshape(d, d)
    mix = _matmul("mix", m_lp, wg_o, kind="nn", tm=tm, tn=tn_d, tk=d, out_dtypes=[F32])

    def post_mix(i, rows, vecs):
        h1_t = rows[0] + _rms_fwd(rows[1], vecs[0])
        return [h1_t, _rms_fwd(h1_t, vecs[1])], []

    h1, n2_lp = _rowwise("post_mix", post_mix, [(h0, d, 0), (mix, d, 0)], [g_post_mix, g_pre_mlp],
                         [(d, F32), (d, MXU_DTYPE)], [], tr)
    (wg_up,) = arrive(2, [n2_lp])
    up, f_lp = _matmul("up", n2_lp, wg_up, kind="nn", tm=tm, tn=tn_h, tk=d, out_dtypes=[F32, MXU_DTYPE],
                       epilogue=lambda acc: (acc, jnp.square(jnp.maximum(acc, 0.0))))
    wg_down = arrive(3, [f_lp])[0].reshape(d_ff, d)
    dn = _matmul("down", f_lp, wg_down, kind="nn", tm=tm, tn=tn_h, tk=_largest_tile(d_ff, 2048, LANES),
                 out_dtypes=[F32])

    def head(row0, rows, vecs):
        h1_t, dn_t, tgt = rows
        y = h1_t + _rms_fwd(dn_t, vecs[0])
        row = row0 + lax.broadcasted_iota(jnp.int32, (ROW_CHUNK, 1), 0)
        err = jnp.where(jnp.logical_and(row >= N_META, row < t_real), y - tgt, 0.0)
        dy = err / d
        d_dn, dg = _rms_bwd(dn_t, vecs[0], dy)
        loss_rows = 0.5 * jnp.mean(err * err, axis=-1, keepdims=True)
        return [dy, d_dn], [dg, jnp.broadcast_to(loss_rows, (ROW_CHUNK, LANES))]

    dy, d_dn, dg_post_mlp, loss_vec = _rowwise(
        "head", head, [(h1, d, 0), (dn, d, 0), (target, d, 0)], [g_post_mlp], [(d, F32), (d, MXU_DTYPE)], [d, LANES], tr)
    loss = lax.psum(loss_vec[0, 0], ("x", "y", "c"))

    core_arr = core.astype(jnp.int32).reshape(1)
    place = jnp.stack([chip, core]).astype(jnp.int32)
    in_flight = {}


    def tag(members):
        return "".join(str(a) for a in members)

    def swap_start(members, gws):
        halves = [gw.reshape(N_CHIPS, 2, gw.shape[1] // 2, gw.shape[2]) for gw in gws]
        *in_flight[members], token = _swap_start(f"swap_start{tag(members)}", halves)
        return token

    def exchange_start(members, after):
        sems, halves, lands = in_flight[members]
        halves, lands = _swap_wait(f"swap_wait{tag(members)}", halves, lands, sems, after)
        pairs = [_pair_sum(f"pair_sum{a}", h, q, core_arr, _largest_tile(q.shape[1], 256, 16))
                 for a, h, q in zip(members, halves, lands)]
        *in_flight[members], token = _exchange_start(f"exchange_start{tag(members)}", pairs)
        return token

    def reduce_finish(groups, after):
        waited = _exchange_wait("exchange_wait" + "_".join(tag(g) for g in groups), [in_flight[g] for g in groups], after)
        return {a: _sum_pieces(f"sum_pieces{a}", pair, land, place, _largest_tile(land.shape[1], 256, 16))
                for g, (pairs, lands) in zip(groups, waited) for a, pair, land in zip(g, pairs, lands)}

    d_up = _matmul("d_up", d_dn, wg_down, kind="nt", tm=tm, tn=tn_h, tk=d, out_dtypes=[MXU_DTYPE], extras=[up],
                   epilogue=lambda acc, up_t: (acc * (2.0 * jnp.maximum(up_t, 0.0)),))
    tk_t = t
    gw_down = _matmul("gw_down", f_lp, d_dn, kind="tn", tm=_largest_tile(d_ff, 2048, LANES), tn=tn_d, tk=tk_t,
                      out_dtypes=[F32])
    tok = swap_start((5,), [gw_down.reshape(N_CHIPS, d_ff // N_CHIPS, d)])
    d_n2 = _matmul("d_n2", d_up, wg_up, kind="nt", tm=tm, tn=tn_h, tk=d_ff // N_CHIPS, out_dtypes=[F32], deps=[tok])
    tok = exchange_start((5,), [d_n2])
    gw_up = _matmul("gw_up", n2_lp, d_up, kind="tn", tm=_largest_tile(d, 2048, LANES), tn=tn_ff, tk=tk_t,
                    out_dtypes=[F32], out_pieces=N_CHIPS, deps=[tok])
    tok = swap_start((4,), [gw_up])

    def bwd_mid(i, rows, vecs):
        dy_t, dn2_t, h1_t, mix_t = rows
        d_h1a, dg_pre_mlp = _rms_bwd(h1_t, vecs[1], dn2_t)
        d_h1 = dy_t + d_h1a
        d_mix, dg_post_mix = _rms_bwd(mix_t, vecs[0], d_h1)
        return [d_h1, d_mix], [dg_pre_mlp, dg_post_mix]

    d_h1, d_mix, dg_pre_mlp, dg_post_mix = _rowwise(
        "bwd_mid", bwd_mid, [(dy, d, 0), (d_n2, d, 0), (h1, d, 0), (mix, d, 0)], [g_post_mix, g_pre_mlp],
        [(d, F32), (d, MXU_DTYPE)], [d, d], tr, deps=[tok])
    d_m = _matmul("d_m", d_mix, wg_o, kind="nt", tm=tm, tn=tn_h, tk=d, out_dtypes=[F32])
    tok = exchange_start((4,), [d_m])
    gw_o = _matmul("gw_o", m_lp, d_mix, kind="tn", tm=_largest_tile(d, 2048, LANES), tn=tn_h, tk=tk_t, out_dtypes=[F32],
                   deps=[tok])

    def gate_bwd(i, rows, vecs):
        dm_t, ya_t, yb_t = rows[:3]
        ga, gb = gates_of(rows[3:], vecs[0])
        d_gpre = jnp.concatenate([dm_t * ya_t * ga * (1.0 - ga), dm_t * yb_t * gb * (1.0 - gb)], axis=1)
        return [dm_t * ga, dm_t * gb, d_gpre], [d_gpre]

    d_ya, d_yb, d_gpre, dg_b_gates = _rowwise(
        "gate_bwd", gate_bwd, [(d_m, d, 0), (y_a, d, 0), (y_b, d, 0)] + gate_rows, [b_gates],
        [(d, MXU_DTYPE), (d, MXU_DTYPE), (2 * d, MXU_DTYPE)], [2 * d], tr, deps=[tok])
    d_aact = _matmul("d_aact", d_ya, wg_pw, kind="nt", tm=tm, tn=d_conf // 2, tk=tn_pw, out_dtypes=[F32], deps=[tok])
    gw_pw = _matmul("gw_pw", a_act, d_ya, kind="tn", tm=_largest_tile(d_conf, 2048, LANES), tn=tn_pw, tk=tk_t,
                    out_dtypes=[F32], out_pieces=N_CHIPS)
    d_s = _matmul("d_s", d_yb, wg_sout, kind="nt", tm=tm, tn=d_conf // 2, tk=tn_pw, out_dtypes=[F32], deps=[tok])
    gw_sout = _matmul("gw_sout", s_lp, d_yb, kind="tn", tm=_largest_tile(d_conf, 2048, LANES), tn=tn_pw, tk=tk_t,
                      out_dtypes=[F32], out_pieces=N_CHIPS, deps=[tok])
    small_big = (3, 1, 2)
    tok = swap_start(small_big, [gw_o.reshape(N_CHIPS, d // N_CHIPS, d), gw_pw, gw_sout])

    def branch_bwd(i, rows, vecs):
        daact_t, ds_t, ac_t, c3_t, bg_t = rows
        xh, rstd, al = ln_parts(ac_t, vecs[0], vecs[1])
        sg = _sigmoid(al)
        d_al = daact_t * (sg * (1.0 + al * (1.0 - sg)))
        dxh = d_al * vecs[0]
        d_ac = rstd * (dxh - jnp.mean(dxh, axis=-1, keepdims=True) - xh * jnp.mean(dxh * xh, axis=-1, keepdims=True))
        return [d_ac, ds_t * bg_t, ds_t * c3_t], [d_al * xh, d_al, d_ac]

    d_ac, d_c3, d_bg, dg_ln_g, dg_ln_b, dg_dw_b = _rowwise(
        "branch_bwd", branch_bwd, [(d_aact, cw, 0), (d_s, cw, 0), (ac, cw, 0), (c3, cw, 0), (proj, cw, 2)],
        [conf_ln_g, conf_ln_b], [(d_conf, F32), (d_conf, F32), (d_conf, MXU_DTYPE)], [d_conf] * 3, tr, deps=[tok])
    d_av, d_ag, d_cg, d_v, dg_wdw, dg_w3 = _conv_bwd(proj, d_ac, d_c3, wdw, w3, d_conf)
    d_proj = jnp.concatenate([d_av, d_ag, d_bg, d_cg, d_v, d_gpre], axis=1)
    small_w = d_conf

    def pack(arrs):
        flat = jnp.concatenate([a.reshape(-1, small_w) for a in arrs], axis=0)
        return jnp.pad(flat, ((0, -flat.shape[0] % 8), (0, 0)))

    def unpack(buf, like):
        out, r0 = [], 0
        for a in like:
            nr = a.size // small_w
            out.append(buf[r0:r0 + nr].reshape(a.shape))
            r0 += nr
        return out

    def reduce_small(tag, arrs, deps):
        assert all(a.size % small_w == 0 for a in arrs)
        gathered = _allgather_small(f"allgather_small_{tag}", pack(arrs), deps)
        return unpack(_sum_blocks(f"sum_small_{tag}", gathered, 8), arrs)

    rep_w = [b_gates, conf_dw_b, conf_ln_g, conf_ln_b, g_post_mix, g_pre_mlp, g_post_mlp]
    rep_m = [m_b_gates, m_conf_dw_b, m_conf_ln_g, m_conf_ln_b, m_g_post_mix, m_g_pre_mlp, m_g_post_mlp]
    rep_v = [v_b_gates, v_conf_dw_b, v_conf_ln_g, v_conf_ln_b, v_g_post_mix, v_g_pre_mlp, v_g_post_mlp]
    *g_rep, g_wdw_full, g_w3_full = reduce_small(
        "mid", [dg_b_gates, dg_dw_b, dg_ln_g, dg_ln_b, dg_post_mix, dg_pre_mlp, dg_post_mlp, dg_wdw, dg_w3], [tok])
    sc = d_conf // N_CHIPS
    g_wdw = lax.dynamic_slice_in_dim(g_wdw_full, chip * sc, sc, axis=1)
    g_w3 = lax.dynamic_slice_in_dim(g_w3_full, chip * sc, sc, axis=1)
    tok = exchange_start(small_big, [g_w3_full])

    gw_in = _matmul("gw_in", n_lp, d_proj, kind="tn", tm=_largest_tile(d, 2048, LANES), tn=tn_in, tk=tk_t,
                    out_dtypes=[F32], out_pieces=N_CHIPS, deps=[tok])
    tok = swap_start((0,), [gw_in])
    d_n = _matmul("d_n", d_proj, wg_in, kind="nt", tm=tm, tn=tn_h, tk=in_cols // N_CHIPS, out_dtypes=[F32], deps=[tok])
    tok = exchange_start((0,), [d_n])

    def bwd_in(i, rows, vecs):
        d_h0a, dg = _rms_bwd(rows[2], vecs[0], rows[1])
        return [rows[0] + d_h0a], [dg]

    d_h0, dg_pre_mix = _rowwise("bwd_in", bwd_in, [(d_h1, d, 0), (d_n, d, 0), (h0, d, 0)], [g_pre_mix],
                                [(d, F32)], [d], tr, deps=[tok])
    grad_x = d_h0[N_META:t_real][None]

    big_m = [m_w_in, m_conf_w_pw, m_short_w_out, m_w_o, m_w_up, m_w_down]
    big_v = [v_w_in, v_conf_w_pw, v_short_w_out, v_w_o, v_w_up, v_w_down]
    big_res = {}

    def adam_group(members, joined):
        for a, j in zip(members, joined):
            big_res[a] = _adamw(f"adamw_big{a}", big[a], j.reshape(big[a].shape), big_m[a][0], big_v[a][0],
                                _largest_tile(big[a].shape[0], 256, 8))

    early = [5, 4, 3, 1, 2]
    reduced = reduce_finish([(5,), (4,), small_big], [d_h0])
    adam_group(early, _join_halves("join_halves_early", [reduced[a] for a in early]))
    rep_pack = [pack(arrs) for arrs in (rep_w, g_rep, rep_m, rep_v)]
    rep_res = [unpack(buf, rep_w) for buf in _adamw("adamw_rep", *rep_pack, rep_pack[0].shape[0])]
    col_res = {a: _adamw(f"adamw_col{a}", w, g, m, v, w.shape[0]) for a, (w, g, m, v) in (
        (1, (conf_dw_w[0], g_wdw, m_conf_dw_w[0], v_conf_dw_w[0])),
        (2, (short_dw_w[0], g_w3, m_short_dw_w[0], v_short_dw_w[0])))}
    reduced = reduce_finish([(0,)], [rep_res[1][0], col_res[1][1], col_res[2][1]] + [big_res[a][1] for a in early])
    adam_group([0], _join_halves("join_halves_late", [reduced[0]]))
    g_pre_mix_full, g_meta_full = reduce_small("late", [dg_pre_mix, d_h0[:N_META]], [big_res[0][1]])
    g_meta = lax.dynamic_slice_in_dim(g_meta_full, chip * (d // N_CHIPS), d // N_CHIPS, axis=1)
    col_res[0] = _adamw("adamw_col0", meta, g_meta, m_meta, v_meta, meta.shape[0])
    pre_mix_res = _adamw("adamw_pre_mix", g_pre_mix, g_pre_mix_full, m_g_pre_mix, v_g_pre_mix, 1)
    rep_res = [[pre_mix_res[q]] + rep_res[q] for q in range(4)]

    def leaf(q):
        r = lambda a: rep_res[q][a]
        b = lambda a: big_res[a][q][None]
        return [col_res[0][q], r(0), b(0), r(1), col_res[1][q][None], r(2), r(3), r(4), b(1), col_res[2][q][None], b(2),
                b(3), r(5), r(6), b(4), b(5), r(7)]

    return (loss, grad_x, *leaf(0), *leaf(1), *leaf(2), *leaf(3))
```

```python
import jax
import jax.numpy as jnp
from jax import lax
from jax.experimental import pallas as pl
from jax.experimental.pallas import tpu as pltpu

F32 = jnp.float32
BF16 = jnp.bfloat16
MXU_DTYPE = BF16
WIRE_DTYPE = BF16

N_META = 16
CONF_KERNEL = 31
SHORT_KERNEL = 3
CONV_PAD = 32
RMS_EPS = 1e-6
LN_EPS = 1e-5
ADAM_LR = 0.001
ADAM_B1 = 0.9
ADAM_B2 = 0.999
ADAM_EPS = 1e-08
ADAM_WD = 0.01
ADAM_STEP = 10

N_CHIPS = 4
MESH = pl.DeviceIdType.MESH
LANES = 128


def _sigmoid(z):
    return 1.0 / (1.0 + jnp.exp(-z))


ANY_SPEC = pl.BlockSpec(memory_space=pl.ANY)


def _matmul(name, a, b, *, kind, tm, tn, tk, out_dtypes, out_pieces=1, epilogue=None, extras=(), deps=()):
    pieces = b.shape[0] if b.ndim == 3 else 1
    if kind == "nn":
        m, kdim = a.shape
        n = b.shape[-1] * pieces
        dims = (((1,), (0,)), ((), ()))
        a_spec = pl.BlockSpec((tm, tk), lambda i, j, k: (i, k))
        if b.ndim == 2:
            b_spec = pl.BlockSpec((tk, tn), lambda i, j, k: (k, j))
        else:
            npp = b.shape[-1] // tn
            b_spec = pl.BlockSpec((None, tk, tn), lambda i, j, k: (j // npp, k, j % npp))
    elif kind == "nt":
        m, kdim = a.shape
        n = b.shape[-2]
        dims = (((1,), (1,)), ((), ()))
        a_spec = pl.BlockSpec((tm, tk), lambda i, j, k: (i, k))
        if b.ndim == 2:
            b_spec = pl.BlockSpec((tn, tk), lambda i, j, k: (j, k))
        else:
            kpp = b.shape[-1] // tk
            b_spec = pl.BlockSpec((None, tn, tk), lambda i, j, k: (k // kpp, j, k % kpp))
    else:
        kdim, m = a.shape
        n = b.shape[-1]
        dims = (((0,), (0,)), ((), ()))
        a_spec = pl.BlockSpec((tk, tm), lambda i, j, k: (k, i))
        b_spec = pl.BlockSpec((tk, tn), lambda i, j, k: (k, j))
    assert m % tm == 0 and n % tn == 0 and kdim % tk == 0, (name, m, n, kdim, tm, tn, tk)
    nk = kdim // tk
    if out_pieces == 1:
        out_shape = (m, n)
        out_spec = pl.BlockSpec((tm, tn), lambda i, j, k: (i, j))
    else:
        onpp = n // out_pieces // tn
        out_shape = (out_pieces, m, n // out_pieces)
        out_spec = pl.BlockSpec((None, tm, tn), lambda i, j, k: (j // onpp, i, j % onpp))
    n_ex, n_out, n_in = len(extras), len(out_dtypes), len(extras) + len(deps)
    if epilogue is None:
        epilogue = lambda acc: (acc,)

    def body(a_ref, b_ref, *rest):
        ex_refs, o_refs = rest[:n_ex], rest[n_in:n_in + n_out]
        prod = lax.dot_general(a_ref[...], b_ref[...], dims, preferred_element_type=F32)

        def finish(acc):
            tiles = epilogue(acc, *[r[...] for r in ex_refs])
            for o_ref, t in zip(o_refs, tiles):
                o_ref[...] = t.astype(o_ref.dtype)

        if nk == 1:
            finish(prod)
        else:
            acc_ref = rest[n_in + n_out]
            k = pl.program_id(2)

            @pl.when(k == 0)
            def _():
                acc_ref[...] = prod

            @pl.when(jnp.logical_and(k > 0, k < nk - 1))
            def _():
                acc_ref[...] += prod

            @pl.when(k == nk - 1)
            def _():
                finish(acc_ref[...] + prod)

    ex_specs = [pl.BlockSpec((tm, tn), lambda i, j, k: (i, j)) for _ in extras]
    res = pl.pallas_call(
        body,
        name=name,
        grid=(m // tm, n // tn, nk),
        in_specs=[a_spec, b_spec, *ex_specs] + [ANY_SPEC] * len(deps),
        out_specs=[out_spec] * n_out,
        out_shape=[jax.ShapeDtypeStruct(out_shape, d) for d in out_dtypes],
        scratch_shapes=[pltpu.VMEM((tm, tn), F32)] if nk > 1 else [],
        compiler_params=pltpu.CompilerParams(dimension_semantics=("parallel", "parallel", "arbitrary")),
    )(a, b, *extras, *deps)
    return res[0] if n_out == 1 else res


ROW_CHUNK = 16
SUBLANES = 8


def _rowwise(name, fn, rows, vecs, outs, sums, tr, deps=()):
    t = rows[0][0].shape[0]
    assert t % tr == 0 and tr % ROW_CHUNK == 0
    n_r, n_v, n_o, n_s = len(rows), len(vecs), len(outs), len(sums)
    n_in = n_r + n_v + len(deps)
    n_steps = t // tr

    def body(*refs):
        r_in, v_in = refs[:n_r], refs[n_r:n_r + n_v]
        o_refs = refs[n_in:n_in + n_o]
        s_refs = refs[n_in + n_o:n_in + n_o + n_s]
        acc_refs = refs[n_in + n_o + n_s:]
        i = pl.program_id(0)

        @pl.when(i == 0)
        def _():
            for acc_ref in acc_refs:
                acc_ref[...] = jnp.zeros(acc_ref.shape, F32)

        def chunk(ci):
            r0 = ci * ROW_CHUNK
            sl = pl.ds(r0, ROW_CHUNK)
            o_tiles, s_tiles = fn(i * tr + r0, [r[sl, :] for r in r_in], [v[...] for v in v_in])
            for o_ref, tile in zip(o_refs, o_tiles):
                o_ref[sl, :] = tile.astype(o_ref.dtype)
            for acc_ref, tile in zip(acc_refs, s_tiles):
                part = tile[0:SUBLANES]
                for s in range(1, ROW_CHUNK // SUBLANES):
                    part = part + tile[s * SUBLANES:(s + 1) * SUBLANES]
                acc_ref[...] += part

        for ci in range(tr // ROW_CHUNK):
            chunk(ci)

        @pl.when(i == n_steps - 1)
        def _():
            for s_ref, acc_ref in zip(s_refs, acc_refs):
                s_ref[...] = jnp.sum(acc_ref[...], axis=0, keepdims=True)

    def row_spec(width, blk):
        return pl.BlockSpec((tr, width), lambda i: (i, blk))

    res = pl.pallas_call(
        body,
        name=name,
        grid=(t // tr,),
        in_specs=[row_spec(w, blk) for _, w, blk in rows]
        + [pl.BlockSpec(v.shape, lambda i: (0, 0)) for v in vecs] + [ANY_SPEC] * len(deps),
        out_specs=[pl.BlockSpec((tr, c), lambda i: (i, 0)) for c, _ in outs]
        + [pl.BlockSpec((1, c), lambda i: (0, 0)) for c in sums],
        out_shape=[jax.ShapeDtypeStruct((t, c), d) for c, d in outs]
        + [jax.ShapeDtypeStruct((1, c), F32) for c in sums],
        scratch_shapes=[pltpu.VMEM((SUBLANES, c), F32) for c in sums],
        compiler_params=pltpu.CompilerParams(dimension_semantics=("arbitrary",)),
    )(*[r[0] for r in rows], *vecs, *deps)
    return res


def _rms_fwd(x, g):
    r = lax.rsqrt(jnp.mean(x * x, axis=-1, keepdims=True) + RMS_EPS)
    return x * r * g


def _rms_bwd(x, g, dy):
    r = lax.rsqrt(jnp.mean(x * x, axis=-1, keepdims=True) + RMS_EPS)
    xn = x * r
    dxn = dy * g
    dx = r * (dxn - xn * jnp.mean(dxn * xn, axis=-1, keepdims=True))
    return dx, dy * xn


CONV_ROWS = 64
CONV_LANES = 128


def _conv_fwd(proj, wdw, bdw, w3, d_conf):
    t = proj.shape[0]
    cl = CONV_LANES
    nb = d_conf // cl
    nchunk = t // CONV_ROWS
    assert t % CONV_ROWS == 0

    def body(av_ref, ag_ref, cg_ref, v_ref, wdw_ref, bdw_ref, w3_ref, ac_ref, c3_ref, apad, cpad):
        zeros = jnp.zeros((CONV_PAD, cl), F32)
        apad[0:CONV_PAD, :] = zeros
        cpad[0:CONV_PAD, :] = zeros
        apad[CONV_PAD:, :] = av_ref[...] * _sigmoid(ag_ref[...])
        cpad[CONV_PAD:, :] = cg_ref[...] * v_ref[...]

        def chunk(ci, carry):
            base = pl.multiple_of(ci * CONV_ROWS, 8)
            acc = jnp.zeros((CONV_ROWS, cl), F32) + bdw_ref[...]
            for k in range(CONF_KERNEL):
                off = CONV_PAD - (CONF_KERNEL - 1) + k
                acc = acc + apad[pl.ds(base + off, CONV_ROWS), :] * wdw_ref[k:k + 1, :]
            ac_ref[pl.ds(base, CONV_ROWS), :] = acc
            acc3 = jnp.zeros((CONV_ROWS, cl), F32)
            for k in range(SHORT_KERNEL):
                off = CONV_PAD - (SHORT_KERNEL - 1) + k
                acc3 = acc3 + cpad[pl.ds(base + off, CONV_ROWS), :] * w3_ref[k:k + 1, :]
            c3_ref[pl.ds(base, CONV_ROWS), :] = acc3
            return carry

        lax.fori_loop(0, nchunk, chunk, 0)

    def col(blk0):
        return pl.BlockSpec((t, cl), lambda j: (0, blk0 + j))

    return pl.pallas_call(
        body,
        name="conv_fwd",
        grid=(nb,),
        in_specs=[col(0), col(nb), col(3 * nb), col(4 * nb),
                  pl.BlockSpec((CONF_KERNEL, cl), lambda j: (0, j)),
                  pl.BlockSpec((1, cl), lambda j: (0, j)),
                  pl.BlockSpec((SHORT_KERNEL, cl), lambda j: (0, j))],
        out_specs=[pl.BlockSpec((t, cl), lambda j: (0, j))] * 2,
        out_shape=[jax.ShapeDtypeStruct((t, d_conf), F32)] * 2,
        scratch_shapes=[pltpu.VMEM((t + CONV_PAD, cl), F32)] * 2,
        compiler_params=pltpu.CompilerParams(dimension_semantics=("parallel",)),
    )(proj, proj, proj, proj, wdw, bdw, w3)


def _conv_bwd(proj, d_ac, d_c3, wdw, w3, d_conf):
    t = proj.shape[0]
    cl = CONV_LANES
    nb = d_conf // cl
    nchunk = t // CONV_ROWS
    nsub = CONV_ROWS // 8

    def fold(p):
        r = p[0:8]
        for s in range(1, nsub):
            r = r + p[8 * s:8 * s + 8]
        return r

    def body(av_ref, ag_ref, cg_ref, v_ref, dac_ref, dc3_ref, wdw_ref, w3_ref,
             dav_ref, dag_ref, dcg_ref, dv_ref, dwdw_ref, dw3_ref, apad, cpad, dapad, dcpad):
        zeros = jnp.zeros((CONV_PAD, cl), F32)
        apad[0:CONV_PAD, :] = zeros
        cpad[0:CONV_PAD, :] = zeros
        apad[CONV_PAD:, :] = av_ref[...] * _sigmoid(ag_ref[...])
        cpad[CONV_PAD:, :] = cg_ref[...] * v_ref[...]
        dapad[0:t, :] = dac_ref[...]
        dcpad[0:t, :] = dc3_ref[...]
        dapad[t:, :] = zeros
        dcpad[t:, :] = zeros

        def chunk(ci, accs):
            base = pl.multiple_of(ci * CONV_ROWS, 8)
            rows = pl.ds(base, CONV_ROWS)
            da = jnp.zeros((CONV_ROWS, cl), F32)
            for k in range(CONF_KERNEL):
                da = da + dapad[pl.ds(base + (CONF_KERNEL - 1 - k), CONV_ROWS), :] * wdw_ref[k:k + 1, :]
            dcv = jnp.zeros((CONV_ROWS, cl), F32)
            for k in range(SHORT_KERNEL):
                dcv = dcv + dcpad[pl.ds(base + (SHORT_KERNEL - 1 - k), CONV_ROWS), :] * w3_ref[k:k + 1, :]
            av, sg = av_ref[rows, :], _sigmoid(ag_ref[rows, :])
            dav_ref[rows, :] = (da * sg).astype(dav_ref.dtype)
            dag_ref[rows, :] = (da * av * sg * (1.0 - sg)).astype(dag_ref.dtype)
            dcg_ref[rows, :] = (dcv * v_ref[rows, :]).astype(dcg_ref.dtype)
            dv_ref[rows, :] = (dcv * cg_ref[rows, :]).astype(dv_ref.dtype)
            d_out, d_out3 = dac_ref[rows, :], dc3_ref[rows, :]
            new = []
            for k in range(CONF_KERNEL):
                off = CONV_PAD - (CONF_KERNEL - 1) + k
                new.append(accs[k] + fold(d_out * apad[pl.ds(base + off, CONV_ROWS), :]))
            for k in range(SHORT_KERNEL):
                off = CONV_PAD - (SHORT_KERNEL - 1) + k
                new.append(accs[CONF_KERNEL + k] + fold(d_out3 * cpad[pl.ds(base + off, CONV_ROWS), :]))
            return tuple(new)

        init = tuple(jnp.zeros((8, cl), F32) for _ in range(CONF_KERNEL + SHORT_KERNEL))
        accs = lax.fori_loop(0, nchunk, chunk, init)
        for k in range(CONF_KERNEL):
            dwdw_ref[k:k + 1, :] = jnp.sum(accs[k], axis=0, keepdims=True)
        for k in range(SHORT_KERNEL):
            dw3_ref[k:k + 1, :] = jnp.sum(accs[CONF_KERNEL + k], axis=0, keepdims=True)

    def col(blk0):
        return pl.BlockSpec((t, cl), lambda j: (0, blk0 + j))

    own = pl.BlockSpec((t, cl), lambda j: (0, j))
    return pl.pallas_call(
        body,
        name="conv_bwd",
        grid=(nb,),
        in_specs=[col(0), col(nb), col(3 * nb), col(4 * nb), own, own,
                  pl.BlockSpec((CONF_KERNEL, cl), lambda j: (0, j)),
                  pl.BlockSpec((SHORT_KERNEL, cl), lambda j: (0, j))],
        out_specs=[own] * 4 + [pl.BlockSpec((CONF_KERNEL, cl), lambda j: (0, j)),
                               pl.BlockSpec((SHORT_KERNEL, cl), lambda j: (0, j))],
        out_shape=[jax.ShapeDtypeStruct((t, d_conf), MXU_DTYPE)] * 4
        + [jax.ShapeDtypeStruct((CONF_KERNEL, d_conf), F32), jax.ShapeDtypeStruct((SHORT_KERNEL, d_conf), F32)],
        scratch_shapes=[pltpu.VMEM((t + CONV_PAD, cl), F32)] * 4,
        compiler_params=pltpu.CompilerParams(dimension_semantics=("parallel",)),
    )(proj, proj, proj, proj, d_ac, d_c3, wdw, w3)


def _elementwise(name, fn, ins, out_dtypes, tr, deps=()):
    ins = [(a, ()) if not isinstance(a, tuple) else a for a in ins]
    r, c = ins[0][0].shape[-2:]
    assert r % tr == 0, (name, r, tr)
    n_in = len(ins)

    def body(*refs):
        tiles = fn(*[x[...] for x in refs[:n_in]])
        for o_ref, tile in zip(refs[n_in + len(deps):], tiles):
            o_ref[...] = tile.astype(o_ref.dtype)

    def spec(lead):
        return pl.BlockSpec((None,) * len(lead) + (tr, c), lambda i: (*lead, i, 0))

    res = pl.pallas_call(
        body,
        name=name,
        grid=(r // tr,),
        in_specs=[spec(lead) for _, lead in ins] + [ANY_SPEC] * len(deps),
        out_specs=[pl.BlockSpec((tr, c), lambda i: (i, 0))] * len(out_dtypes),
        out_shape=[jax.ShapeDtypeStruct((r, c), d) for d in out_dtypes],
        compiler_params=pltpu.CompilerParams(dimension_semantics=("parallel",)),
    )(*[a for a, _ in ins], *deps)
    return res


def _adamw_tiles(w, g, m, v):
    m = ADAM_B1 * m + (1.0 - ADAM_B1) * g
    v = ADAM_B2 * v + (1.0 - ADAM_B2) * jnp.square(g)
    m_hat = m / (1.0 - ADAM_B1 ** ADAM_STEP)
    v_hat = v / (1.0 - ADAM_B2 ** ADAM_STEP)
    delta = -ADAM_LR * (m_hat / (jnp.sqrt(v_hat) + ADAM_EPS) + ADAM_WD * w)
    return g, delta, m, v


def _adamw(name, w, g, m, v, tr, deps=()):
    shape = w.shape
    flat = [a.reshape(shape[-2:]) if a.ndim > 2 else a for a in (w, g, m, v)]
    res = _elementwise(name, _adamw_tiles, flat, [F32] * 4, tr, deps)
    return [a.reshape(shape) for a in res]


def _pair_sum(name, p, q, core, tr):
    n_p, _, hr, c = p.shape
    assert hr % tr == 0

    def body(core_ref, p_ref, q_ref, o_ref):
        o_ref[...] = (p_ref[...] + q_ref[...]).astype(o_ref.dtype)

    return pl.pallas_call(
        body,
        name=name,
        grid_spec=pltpu.PrefetchScalarGridSpec(
            num_scalar_prefetch=1,
            grid=(n_p, hr // tr),
            in_specs=[pl.BlockSpec((None, None, tr, c), lambda a, i, core_ref: (a, core_ref[0], i, 0)),
                      pl.BlockSpec((None, tr, c), lambda a, i, core_ref: (a, i, 0))],
            out_specs=pl.BlockSpec((None, tr, c), lambda a, i, core_ref: (a, i, 0)),
        ),
        out_shape=jax.ShapeDtypeStruct((n_p, hr, c), WIRE_DTYPE),
        compiler_params=pltpu.CompilerParams(dimension_semantics=("parallel", "parallel")),
    )(core, p, q)


def _into_slot(name, w, slots, slot, dtype, tr, deps=()):
    r, c = w.shape
    assert r % tr == 0

    def body(slot_ref, w_ref, *rest):
        o_ref = rest[len(deps)]
        o_ref[...] = w_ref[...].astype(o_ref.dtype)

    return pl.pallas_call(
        body,
        name=name,
        grid_spec=pltpu.PrefetchScalarGridSpec(
            num_scalar_prefetch=1,
            grid=(r // tr,),
            in_specs=[pl.BlockSpec((tr, c), lambda i, slot_ref: (i, 0))] + [ANY_SPEC] * len(deps),
            out_specs=pl.BlockSpec((None, tr, c), lambda i, slot_ref: (slot_ref[0], i, 0)),
        ),
        out_shape=jax.ShapeDtypeStruct((slots, r, c), dtype),
        compiler_params=pltpu.CompilerParams(dimension_semantics=("parallel",)),
    )(slot, w, *deps)


def _sum_pieces(name, own, rb, place, tr):
    n_p, hr, c = rb.shape
    assert hr % tr == 0

    def body(place_ref, own_ref, *refs):
        chip = place_ref[0]
        acc = None
        for k in range(n_p):
            tile = jnp.where(chip == k, own_ref[...], refs[k][...]).astype(F32)
            acc = tile if acc is None else acc + tile
        refs[n_p][...] = acc

    def landed(k):
        return pl.BlockSpec((None, tr, c), lambda i, place_ref: (jnp.where(place_ref[0] == k, (k + 1) % n_p, k), i, 0))

    return pl.pallas_call(
        body,
        name=name,
        grid_spec=pltpu.PrefetchScalarGridSpec(
            num_scalar_prefetch=1,
            grid=(hr // tr,),
            in_specs=[pl.BlockSpec((None, tr, c), lambda i, place_ref: (place_ref[0], i, 0))]
            + [landed(k) for k in range(n_p)],
            out_specs=pl.BlockSpec((None, tr, c), lambda i, place_ref: (place_ref[1], i, 0)),
        ),
        out_shape=jax.ShapeDtypeStruct((2, hr, c), F32),
        compiler_params=pltpu.CompilerParams(dimension_semantics=("parallel",)),
    )(place, own, *([rb] * n_p))


HBM_SPEC = pl.BlockSpec(memory_space=pl.ANY)


def _place():
    x, y, c = lax.axis_index("x"), lax.axis_index("y"), lax.axis_index("c")
    chips = [(1 - x, y), (x, 1 - y), (1 - x, 1 - y)]
    return x, y, c, chips


def _join_halves(name, bufs, deps=()):
    n = len(bufs)

    def body(*refs):
        outs = refs[n + len(deps):2 * n + len(deps)]
        send_sems, recv_sems = refs[2 * n + len(deps):]
        x, y, c, _ = _place()
        copies = [pltpu.make_async_remote_copy(
            src_ref=outs[a].at[c], dst_ref=outs[a].at[c], send_sem=send_sems.at[a], recv_sem=recv_sems.at[a],
            device_id=(x, y, 1 - c), device_id_type=MESH) for a in range(n)]
        for cp in copies:
            cp.start()
        for a in range(n):
            other = outs[a].at[1 - c]
            pltpu.make_async_remote_copy(
                src_ref=other, dst_ref=other, send_sem=send_sems.at[a], recv_sem=recv_sems.at[a],
                device_id=(x, y, 1 - c), device_id_type=MESH).wait_recv()
        for cp in copies:
            cp.wait_send()

    return pl.pallas_call(
        body,
        name=name,
        in_specs=[HBM_SPEC] * n + [ANY_SPEC] * len(deps),
        out_specs=[HBM_SPEC] * n,
        out_shape=[jax.ShapeDtypeStruct(b.shape, b.dtype) for b in bufs],
        input_output_aliases={a: a for a in range(n)},
        scratch_shapes=[pltpu.SemaphoreType.DMA((n,)), pltpu.SemaphoreType.DMA((n,))],
    )(*bufs, *deps)


HBM_ONLY = pl.BlockSpec(memory_space=pltpu.HBM)
SEM_SPEC = pl.BlockSpec(memory_space=pltpu.SEMAPHORE)
DATAFLOW = pltpu.SideEffectType.DATAFLOW_SIDE_EFFECTING


def _in_hbm(a):
    return pltpu.with_memory_space_constraint(a, pltpu.HBM)


def _shard_part(ref, is_split, slot, h):
    if not is_split:
        return ref.at[slot]
    hr = ref.shape[1] // 2
    return ref.at[slot, pl.ds(h * hr, hr), :]


TOKEN = jax.ShapeDtypeStruct((8, LANES), F32)
VMEM_SPEC = pl.BlockSpec(memory_space=pltpu.VMEM)


def _gather_start(name, bufs, split, groups, deps=()):
    n, ng = len(bufs), len(groups)

    def body(*refs):
        ins, sems = refs[:n], refs[n + len(deps):n + len(deps) + 2 * ng]
        refs[-1][...] = jnp.zeros(TOKEN.shape, TOKEN.dtype)
        x, y, c, chips = _place()
        me = 2 * x + y
        for g, members in enumerate(groups):
            for s, a in enumerate(members):
                mine = _shard_part(ins[a], split[a], me, c)
                for j, chip in enumerate(chips):
                    pltpu.make_async_remote_copy(
                        src_ref=mine, dst_ref=mine, send_sem=sems[2 * g].at[3 * s + j], recv_sem=sems[2 * g + 1].at[3 * s + j],
                        device_id=(*chip, c), device_id_type=MESH).start()

    res = pl.pallas_call(
        body,
        name=name,
        in_specs=[HBM_ONLY] * n + [ANY_SPEC] * len(deps),
        out_specs=[SEM_SPEC] * (2 * ng) + [HBM_ONLY] * n + [VMEM_SPEC],
        out_shape=[pltpu.SemaphoreType.DMA((3 * len(members),)) for members in groups for _ in range(2)]
        + [pltpu.HBM(b.shape, b.dtype) for b in bufs] + [TOKEN],
        input_output_aliases={a: 2 * ng + a for a in range(n)},
        compiler_params=pltpu.CompilerParams(has_side_effects=DATAFLOW),
    )(*[_in_hbm(b) for b in bufs], *deps)
    return [(res[2 * g], res[2 * g + 1]) for g in range(ng)], list(res[2 * ng:2 * ng + n]), res[-1]


def _gather_wait(name, bufs, split, sems, after):
    n = len(bufs)

    def body(*refs):
        ins, send_sems, recv_sems = refs[:n], refs[n], refs[n + 1]
        x, y, c, chips = _place()
        me = 2 * x + y
        for s in range(n):
            for j, chip in enumerate(chips):
                copy = pltpu.make_async_remote_copy(
                    src_ref=_shard_part(ins[s], split[s], me, c),
                    dst_ref=_shard_part(ins[s], split[s], 2 * chip[0] + chip[1], c),
                    send_sem=send_sems.at[3 * s + j], recv_sem=recv_sems.at[3 * s + j],
                    device_id=(*chip, c), device_id_type=MESH)
                copy.wait_send()
                copy.wait_recv()

    res = pl.pallas_call(
        body,
        name=name,
        in_specs=[HBM_ONLY] * n + [SEM_SPEC, SEM_SPEC] + [ANY_SPEC] * len(after),
        out_specs=[HBM_ONLY] * n,
        out_shape=[pltpu.HBM(b.shape, b.dtype) for b in bufs],
        input_output_aliases={a: a for a in range(n)},
        compiler_params=pltpu.CompilerParams(has_side_effects=DATAFLOW),
    )(*bufs, *sems, *after)
    return list(res)


def _pass_halves(name, bufs):
    n = len(bufs)

    def body(*refs):
        outs = refs[n:2 * n]
        send_sems, recv_sems = refs[2 * n:]
        x, y, c, chips = _place()
        sibling = (x, y, 1 - c)

        def copy(a, j, h):
            blk = _shard_part(outs[a], True, 2 * chips[j][0] + chips[j][1], h)
            return pltpu.make_async_remote_copy(
                src_ref=blk, dst_ref=blk, send_sem=send_sems.at[3 * a + j], recv_sem=recv_sems.at[3 * a + j],
                device_id=sibling, device_id_type=MESH)

        sends = [copy(a, j, c) for a in range(n) for j in range(3)]
        for cp in sends:
            cp.start()
        for a in range(n):
            for j in range(3):
                copy(a, j, 1 - c).wait_recv()
        for cp in sends:
            cp.wait_send()

    res = pl.pallas_call(
        body,
        name=name,
        in_specs=[HBM_SPEC] * n,
        out_specs=[HBM_SPEC] * n,
        out_shape=[jax.ShapeDtypeStruct(b.shape, b.dtype) for b in bufs],
        input_output_aliases={a: a for a in range(n)},
        scratch_shapes=[pltpu.SemaphoreType.DMA((3 * n,)), pltpu.SemaphoreType.DMA((3 * n,))],
    )(*bufs)
    return list(res)


def _exchange_start(name, pairs):
    n = len(pairs)

    def body(*refs):
        pair_refs, land_refs, send_sems, recv_sems = refs[:n], refs[n:2 * n], refs[2 * n], refs[2 * n + 1]
        x, y, c, chips = _place()
        me = 2 * x + y
        for a in range(n):
            for j, chip in enumerate(chips):
                pltpu.make_async_remote_copy(
                    src_ref=pair_refs[a].at[2 * chip[0] + chip[1]], dst_ref=land_refs[a].at[me],
                    send_sem=send_sems.at[3 * a + j], recv_sem=recv_sems.at[3 * a + j],
                    device_id=(*chip, c), device_id_type=MESH).start()
        refs[-1][...] = jnp.zeros(TOKEN.shape, TOKEN.dtype)

    res = pl.pallas_call(
        body,
        name=name,
        in_specs=[HBM_ONLY] * (2 * n),
        out_specs=[SEM_SPEC, SEM_SPEC] + [HBM_ONLY] * (2 * n) + [VMEM_SPEC],
        out_shape=[pltpu.SemaphoreType.DMA((3 * n,)), pltpu.SemaphoreType.DMA((3 * n,))]
        + [pltpu.HBM(p.shape, p.dtype) for p in pairs] * 2 + [TOKEN],
        input_output_aliases={a: 2 + a for a in range(2 * n)},
        compiler_params=pltpu.CompilerParams(has_side_effects=DATAFLOW),
    )(*[_in_hbm(p) for p in pairs], *[_in_hbm(lax.empty(p.shape, p.dtype)) for p in pairs])
    return (res[0], res[1]), list(res[2:2 + n]), list(res[2 + n:2 + 2 * n]), res[-1]


def _exchange_wait(name, groups, after):
    sizes = [len(pairs) for _, pairs, _ in groups]
    n_buf = 2 * sum(sizes)

    def body(*refs):
        x, y, c, chips = _place()
        at_buf, at_sem = 0, n_buf
        for n in sizes:
            pair_refs, land_refs = refs[at_buf:at_buf + n], refs[at_buf + n:at_buf + 2 * n]
            send_sems, recv_sems = refs[at_sem], refs[at_sem + 1]
            at_buf, at_sem = at_buf + 2 * n, at_sem + 2
            for a in range(n):
                for j, chip in enumerate(chips):
                    k = 2 * chip[0] + chip[1]
                    copy = pltpu.make_async_remote_copy(
                        src_ref=pair_refs[a].at[k], dst_ref=land_refs[a].at[k],
                        send_sem=send_sems.at[3 * a + j], recv_sem=recv_sems.at[3 * a + j],
                        device_id=(*chip, c), device_id_type=MESH)
                    copy.wait_send()
                    copy.wait_recv()

    bufs = [b for _, pairs, lands in groups for b in (*pairs, *lands)]
    sems = [s for group_sems, _, _ in groups for s in group_sems]
    res = pl.pallas_call(
        body,
        name=name,
        in_specs=[HBM_ONLY] * n_buf + [SEM_SPEC] * len(sems) + [ANY_SPEC] * len(after),
        out_specs=[HBM_ONLY] * n_buf,
        out_shape=[pltpu.HBM(b.shape, b.dtype) for b in bufs],
        input_output_aliases={a: a for a in range(n_buf)},
        compiler_params=pltpu.CompilerParams(has_side_effects=DATAFLOW),
    )(*bufs, *sems, *after)
    out, at = [], 0
    for n in sizes:
        out.append((list(res[at:at + n]), list(res[at + n:at + 2 * n])))
        at += 2 * n
    return out


def _swap_start(name, halves):
    n = len(halves)
    n_p = halves[0].shape[0]
    land_shapes = [(n_p, *h.shape[2:]) for h in halves]

    def body(*refs):
        half_refs, land_refs, send_sems, recv_sems = refs[:n], refs[n:2 * n], refs[2 * n], refs[2 * n + 1]
        x, y, c, _ = _place()
        for a in range(n):
            for p in range(n_p):
                pltpu.make_async_remote_copy(
                    src_ref=half_refs[a].at[p, 1 - c], dst_ref=land_refs[a].at[p],
                    send_sem=send_sems.at[n_p * a + p], recv_sem=recv_sems.at[n_p * a + p],
                    device_id=(x, y, 1 - c), device_id_type=MESH).start()
        refs[-1][...] = jnp.zeros(TOKEN.shape, TOKEN.dtype)

    res = pl.pallas_call(
        body,
        name=name,
        in_specs=[HBM_ONLY] * (2 * n),
        out_specs=[SEM_SPEC, SEM_SPEC] + [HBM_ONLY] * (2 * n) + [VMEM_SPEC],
        out_shape=[pltpu.SemaphoreType.DMA((n_p * n,)), pltpu.SemaphoreType.DMA((n_p * n,))]
        + [pltpu.HBM(h.shape, h.dtype) for h in halves]
        + [pltpu.HBM(s, h.dtype) for s, h in zip(land_shapes, halves)] + [TOKEN],
        input_output_aliases={a: 2 + a for a in range(2 * n)},
        compiler_params=pltpu.CompilerParams(has_side_effects=DATAFLOW),
    )(*[_in_hbm(h) for h in halves], *[_in_hbm(lax.empty(s, h.dtype)) for s, h in zip(land_shapes, halves)])
    return (res[0], res[1]), list(res[2:2 + n]), list(res[2 + n:2 + 2 * n]), res[-1]


def _swap_wait(name, halves, lands, sems, after):
    n = len(halves)
    n_p = halves[0].shape[0]

    def body(*refs):
        half_refs, land_refs, send_sems, recv_sems = refs[:n], refs[n:2 * n], refs[2 * n], refs[2 * n + 1]
        x, y, c, _ = _place()
        for a in range(n):
            for p in range(n_p):
                copy = pltpu.make_async_remote_copy(
                    src_ref=half_refs[a].at[p, 1 - c], dst_ref=land_refs[a].at[p],
                    send_sem=send_sems.at[n_p * a + p], recv_sem=recv_sems.at[n_p * a + p],
                    device_id=(x, y, 1 - c), device_id_type=MESH)
                copy.wait_send()
                copy.wait_recv()

    res = pl.pallas_call(
        body,
        name=name,
        in_specs=[HBM_ONLY] * (2 * n) + [SEM_SPEC, SEM_SPEC] + [ANY_SPEC] * len(after),
        out_specs=[HBM_ONLY] * (2 * n),
        out_shape=[pltpu.HBM(b.shape, b.dtype) for b in (*halves, *lands)],
        input_output_aliases={a: a for a in range(2 * n)},
        compiler_params=pltpu.CompilerParams(has_side_effects=DATAFLOW),
    )(*halves, *lands, *sems, *after)
    return list(res[:n]), list(res[n:])


def _join_start(name, bufs):
    n = len(bufs)

    def body(*refs):
        ins, send_sems, recv_sems = refs[:n], refs[n], refs[n + 1]
        x, y, c, _ = _place()
        for a in range(n):
            pltpu.make_async_remote_copy(
                src_ref=ins[a].at[c], dst_ref=ins[a].at[c], send_sem=send_sems.at[a], recv_sem=recv_sems.at[a],
                device_id=(x, y, 1 - c), device_id_type=MESH).start()
        refs[-1][...] = jnp.zeros(TOKEN.shape, TOKEN.dtype)

    res = pl.pallas_call(
        body,
        name=name,
        in_specs=[HBM_ONLY] * n,
        out_specs=[SEM_SPEC, SEM_SPEC] + [HBM_ONLY] * n + [VMEM_SPEC],
        out_shape=[pltpu.SemaphoreType.DMA((n,)), pltpu.SemaphoreType.DMA((n,))]
        + [pltpu.HBM(b.shape, b.dtype) for b in bufs] + [TOKEN],
        input_output_aliases={a: 2 + a for a in range(n)},
        compiler_params=pltpu.CompilerParams(has_side_effects=DATAFLOW),
    )(*[_in_hbm(b) for b in bufs])
    return (res[0], res[1]), list(res[2:2 + n]), res[-1]


def _join_wait(name, bufs, sems, after):
    n = len(bufs)

    def body(*refs):
        ins, send_sems, recv_sems = refs[:n], refs[n], refs[n + 1]
        x, y, c, _ = _place()
        for a in range(n):
            copy = pltpu.make_async_remote_copy(
                src_ref=ins[a].at[c], dst_ref=ins[a].at[1 - c], send_sem=send_sems.at[a], recv_sem=recv_sems.at[a],
                device_id=(x, y, 1 - c), device_id_type=MESH)
            copy.wait_send()
            copy.wait_recv()

    res = pl.pallas_call(
        body,
        name=name,
        in_specs=[HBM_ONLY] * n + [SEM_SPEC, SEM_SPEC] + [ANY_SPEC] * len(after),
        out_specs=[HBM_ONLY] * n,
        out_shape=[pltpu.HBM(b.shape, b.dtype) for b in bufs],
        input_output_aliases={a: a for a in range(n)},
        compiler_params=pltpu.CompilerParams(has_side_effects=DATAFLOW),
    )(*bufs, *sems, *after)
    return list(res)


def _allgather_small(name, block, deps=()):
    m_per, n = block.shape

    def body(x_ref, *rest):
        out_ref, send_sems, recv_sems, local_sem = rest[len(deps):]
        x, y, c, chips = _place()
        me, sibling = (x, y, c), (x, y, 1 - c)

        def rows(px, py, pc):
            return out_ref.at[pl.ds((4 * px + 2 * py + pc) * m_per, m_per), :]

        def copy(k, blk, to, src=None):
            return pltpu.make_async_remote_copy(
                src_ref=rows(*blk) if src is None else src, dst_ref=rows(*blk),
                send_sem=send_sems.at[k], recv_sem=recv_sems.at[k], device_id=to, device_id_type=MESH)

        mine = pltpu.make_async_copy(x_ref, rows(*me), local_sem)
        mine.start()
        first = [copy(0, me, sibling, src=x_ref)]
        first += [copy(1 + j, me, (*chip, c), src=x_ref) for j, chip in enumerate(chips)]
        for cp in first:
            cp.start()
        passed = [copy(4 + j, (*chip, c), sibling) for j, chip in enumerate(chips)]
        for j, chip in enumerate(chips):
            copy(1 + j, (*chip, c), me).wait_recv()
            passed[j].start()
        copy(0, sibling, me).wait_recv()
        for j, chip in enumerate(chips):
            copy(4 + j, (*chip, 1 - c), me).wait_recv()
        for cp in first + passed:
            cp.wait_send()
        mine.wait()

    return pl.pallas_call(
        body,
        name=name,
        out_shape=jax.ShapeDtypeStruct((8 * m_per, n), block.dtype),
        in_specs=[pl.BlockSpec(memory_space=pltpu.VMEM)] + [ANY_SPEC] * len(deps),
        out_specs=pl.BlockSpec(memory_space=pltpu.VMEM),
        scratch_shapes=[pltpu.SemaphoreType.DMA((7,)), pltpu.SemaphoreType.DMA((7,)), pltpu.SemaphoreType.DMA],
    )(block, *deps)


def _sum_blocks(name, gathered, n_blocks):
    r = gathered.shape[0] // n_blocks
    c = gathered.shape[1]

    def body(g_ref, o_ref):
        acc = g_ref[0:r, :]
        for b in range(1, n_blocks):
            acc = acc + g_ref[b * r:(b + 1) * r, :]
        o_ref[...] = acc

    return pl.pallas_call(body, name=name, out_shape=jax.ShapeDtypeStruct((r, c), F32))(gathered)


def _largest_tile(n, cap, mult):
    best = None
    for d in range(mult, min(n, cap) + 1, mult):
        if n % d == 0:
            best = d
    assert best is not None, (n, cap, mult)
    return best


def kernel(x, meta, g_pre_mix, w_in, b_gates, conf_dw_w, conf_dw_b, conf_ln_g, conf_ln_b, conf_w_pw, short_dw_w, short_w_out, w_o, g_post_mix, g_pre_mlp, w_up, w_down, g_post_mlp, loss_target, m_meta, m_g_pre_mix, m_w_in, m_b_gates, m_conf_dw_w, m_conf_dw_b, m_conf_ln_g, m_conf_ln_b, m_conf_w_pw, m_short_dw_w, m_short_w_out, m_w_o, m_g_post_mix, m_g_pre_mlp, m_w_up, m_w_down, m_g_post_mlp, v_meta, v_g_pre_mix, v_w_in, v_b_gates, v_conf_dw_w, v_conf_dw_b, v_conf_ln_g, v_conf_ln_b, v_conf_w_pw, v_short_dw_w, v_short_w_out, v_w_o, v_g_post_mix, v_g_pre_mlp, v_w_up, v_w_down, v_g_post_mlp):
    seq, d = x.shape[1], x.shape[2]
    t_real = seq + N_META
    t = -(-t_real // LANES) * LANES
    d_conf = conf_dw_b.shape[1]
    d_ff = w_up.shape[2] * N_CHIPS
    in_cols = w_in.shape[2] * N_CHIPS
    assert in_cols == 5 * d_conf + 2 * d and d == 2 * d_conf
    cw = d_conf
    core = lax.axis_index("c")
    chip = 2 * lax.axis_index("x") + lax.axis_index("y")

    tr = _largest_tile(t, 272, ROW_CHUNK)
    tm = t
    tn_in = _largest_tile(in_cols // N_CHIPS, 768, LANES)
    tn_d = _largest_tile(d, 1024, LANES)
    tn_h = _largest_tile(d, 512, LANES)
    tn_ff = _largest_tile(d_ff // N_CHIPS, 1024, LANES)
    tn_pw = _largest_tile(d // N_CHIPS, 512, LANES)

    big = [w_in[0], conf_w_pw[0], short_w_out[0], w_o[0], w_up[0], w_down[0]]
    chip_arr = chip.astype(jnp.int32).reshape(1)

    def cast(a, deps):
        return _into_slot(f"cast_w{a}", big[a], N_CHIPS, chip_arr, MXU_DTYPE, _largest_tile(big[a].shape[0], 256, 16), deps)

    small = [_into_slot(f"place_w{a}", w, N_CHIPS, chip_arr, F32, w.shape[0])
             for a, w in enumerate([meta, conf_dw_w[0], short_dw_w[0]])]
    sems_small, fly_small, tok_small = _gather_start("gather_start_small", small, [False] * 3, [[0, 1, 2]])
    sems_in, fly_in, tok_in = _gather_start("gather_start_in", [cast(0, [tok_small])], [True], [[0]])
    rest_groups = [[0, 1], [2], [3], [4]]
    sems_rest, fly_rest, tok_rest = _gather_start(
        "gather_start_rest", [cast(a, [tok_in]) for a in range(1, 6)], [True] * 5, rest_groups)

    def arrive(g, after):
        members = rest_groups[g]
        got = _gather_wait(f"gather_wait_rest{g}", [fly_rest[a] for a in members], [True] * len(members),
                           sems_rest[g], after)
        return _pass_halves(f"gather_pass_rest{g}", got)

    meta_g, wdw_g, w3_g = _gather_wait("gather_wait_small", fly_small, [False] * 3, sems_small[0], [tok_in])
    meta_full = jnp.transpose(meta_g, (1, 0, 2)).reshape(N_META, d)
    wdw = jnp.transpose(wdw_g, (1, 0, 2)).reshape(CONF_KERNEL, d_conf)
    w3 = jnp.transpose(w3_g, (1, 0, 2)).reshape(SHORT_KERNEL, d_conf)

    tail = jnp.zeros((t - t_real, d), F32)
    h0 = jnp.concatenate([meta_full, x[0], tail], axis=0)
    target = jnp.concatenate([jnp.zeros((N_META, d), F32), loss_target[0], tail], axis=0)

    def norm_in(i, rows, vecs):
        return [_rms_fwd(rows[0], vecs[0])], []

    (n_lp,) = _rowwise("norm_in", norm_in, [(h0, d, 0)], [g_pre_mix], [(d, MXU_DTYPE)], [], tr, deps=[tok_rest])
    (wg_in,) = _pass_halves("gather_pass_in", _gather_wait("gather_wait_in", fly_in, [True], sems_in[0], [n_lp]))
    proj =_matmul("proj", n_lp, wg_in, kind="nn", tm=tm, tn=tn_in, tk=d, out_dtypes=[F32])
    ac, c3 = _conv_fwd(proj, wdw, conf_dw_b, w3, d_conf)

    def ln_parts(ac_t, ln_g, ln_b):
        mu = jnp.mean(ac_t, axis=-1, keepdims=True)
        xc = ac_t - mu
        rstd = lax.rsqrt(jnp.mean(xc * xc, axis=-1, keepdims=True) + LN_EPS)
        xh = xc * rstd
        return xh, rstd, xh * ln_g + ln_b

    def branch_act(i, rows, vecs):
        ac_t, c3_t, bg_t = rows
        _, _, al = ln_parts(ac_t, vecs[0], vecs[1])
        return [al * _sigmoid(al), bg_t * c3_t], []

    a_act, s_lp = _rowwise("branch_act", branch_act, [(ac, cw, 0), (c3, cw, 0), (proj, cw, 2)],
                           [conf_ln_g, conf_ln_b], [(d_conf, MXU_DTYPE), (d_conf, MXU_DTYPE)], [], tr)
    wg_pw, wg_sout = arrive(0, [a_act])
    y_a = _matmul("y_a", a_act, wg_pw, kind="nn", tm=tm, tn=tn_pw, tk=d_conf, out_dtypes=[F32])
    y_b = _matmul("y_b", s_lp, wg_sout, kind="nn", tm=tm, tn=tn_pw, tk=d_conf, out_dtypes=[F32])

    gate_rows = [(proj, cw, 5), (proj, cw, 6), (proj, cw, 7), (proj, cw, 8)]

    def gates_of(rows, b):
        ga = _sigmoid(jnp.concatenate([rows[0], rows[1]], axis=1) + b[:, :d])
        gb = _sigmoid(jnp.concatenate([rows[2], rows[3]], axis=1) + b[:, d:])
        return ga, gb

    def gate(i, rows, vecs):
        ga, gb = gates_of(rows[2:], vecs[0])
        return [ga * rows[0] + gb * rows[1]], []

    (m_lp,) = _rowwise("gate", gate, [(y_a, d, 0), (y_b, d, 0)] + gate_rows, [b_gates], [(d, MXU_DTYPE)], [], tr)
    wg_o = arrive(1, [m_lp])[0].reshape(d, d)
    mix = _matmul("mix", m_lp, wg_o, kind="nn", tm=tm, tn=tn_d, tk=d, out_dtypes=[F32])

    def post_mix(i, rows, vecs):
        h1_t = rows[0] + _rms_fwd(rows[1], vecs[0])
        return [h1_t, _rms_fwd(h1_t, vecs[1])], []

    h1, n2_lp = _rowwise("post_mix", post_mix, [(h0, d, 0), (mix, d, 0)], [g_post_mix, g_pre_mlp],
                         [(d, F32), (d, MXU_DTYPE)], [], tr)
    (wg_up,) = arrive(2, [n2_lp])
    up, f_lp = _matmul("up", n2_lp, wg_up, kind="nn", tm=tm, tn=tn_h, tk=d, out_dtypes=[F32, MXU_DTYPE],
                       epilogue=lambda acc: (acc, jnp.square(jnp.maximum(acc, 0.0))))
    wg_down = arrive(3, [f_lp])[0].reshape(d_ff, d)
    dn = _matmul("down", f_lp, wg_down, kind="nn", tm=tm, tn=tn_h, tk=_largest_tile(d_ff, 2048, LANES),
                 out_dtypes=[F32])

    def head(row0, rows, vecs):
        h1_t, dn_t, tgt = rows
        y = h1_t + _rms_fwd(dn_t, vecs[0])
        row = row0 + lax.broadcasted_iota(jnp.int32, (ROW_CHUNK, 1), 0)
        err = jnp.where(jnp.logical_and(row >= N_META, row < t_real), y - tgt, 0.0)
        dy = err / d
        d_dn, dg = _rms_bwd(dn_t, vecs[0], dy)
        loss_rows = 0.5 * jnp.mean(err * err, axis=-1, keepdims=True)
        return [dy, d_dn], [dg, jnp.broadcast_to(loss_rows, (ROW_CHUNK, LANES))]

    dy, d_dn, dg_post_mlp, loss_vec = _rowwise(
        "head", head, [(h1, d, 0), (dn, d, 0), (target, d, 0)], [g_post_mlp], [(d, F32), (d, MXU_DTYPE)], [d, LANES], tr)
    loss = lax.psum(loss_vec[0, 0], ("x", "y", "c"))

    core_arr = core.astype(jnp.int32).reshape(1)
    place = jnp.stack([chip, core]).astype(jnp.int32)
    in_flight = {}


    def tag(members):
        return "".join(str(a) for a in members)

    def swap_start(members, gws):
        halves = [gw.reshape(N_CHIPS, 2, gw.shape[1] // 2, gw.shape[2]) for gw in gws]
        *in_flight[members], token = _swap_start(f"swap_start{tag(members)}", halves)
        return token

    def exchange_start(members, after):
        sems, halves, lands = in_flight[members]
        halves, lands = _swap_wait(f"swap_wait{tag(members)}", halves, lands, sems, after)
        pairs = [_pair_sum(f"pair_sum{a}", h, q, core_arr, _largest_tile(q.shape[1], 256, 16))
                 for a, h, q in zip(members, halves, lands)]
        *in_flight[members], token = _exchange_start(f"exchange_start{tag(members)}", pairs)
        return token

    def reduce_finish(groups, after):
        waited = _exchange_wait("exchange_wait" + "_".join(tag(g) for g in groups), [in_flight[g] for g in groups], after)
        return {a: _sum_pieces(f"sum_pieces{a}", pair, land, place, _largest_tile(land.shape[1], 256, 16))
                for g, (pairs, lands) in zip(groups, waited) for a, pair, land in zip(g, pairs, lands)}

    d_up = _matmul("d_up", d_dn, wg_down, kind="nt", tm=tm, tn=tn_h, tk=d, out_dtypes=[MXU_DTYPE], extras=[up],
                   epilogue=lambda acc, up_t: (acc * (2.0 * jnp.maximum(up_t, 0.0)),))
    tk_t = t
    gw_down = _matmul("gw_down", f_lp, d_dn, kind="tn", tm=_largest_tile(d_ff, 2048, LANES), tn=tn_d, tk=tk_t,
                      out_dtypes=[F32])
    tok = swap_start((5,), [gw_down.reshape(N_CHIPS, d_ff // N_CHIPS, d)])
    d_n2 = _matmul("d_n2", d_up, wg_up, kind="nt", tm=tm, tn=tn_h, tk=d_ff // N_CHIPS, out_dtypes=[F32], deps=[tok])
    tok = exchange_start((5,), [d_n2])
    gw_up = _matmul("gw_up", n2_lp, d_up, kind="tn", tm=_largest_tile(d, 2048, LANES), tn=tn_ff, tk=tk_t,
                    out_dtypes=[F32], out_pieces=N_CHIPS, deps=[tok])
    tok = swap_start((4,), [gw_up])

    def bwd_mid(i, rows, vecs):
        dy_t, dn2_t, h1_t, mix_t = rows
        d_h1a, dg_pre_mlp = _rms_bwd(h1_t, vecs[1], dn2_t)
        d_h1 = dy_t + d_h1a
        d_mix, dg_post_mix = _rms_bwd(mix_t, vecs[0], d_h1)
        return [d_h1, d_mix], [dg_pre_mlp, dg_post_mix]

    d_h1, d_mix, dg_pre_mlp, dg_post_mix = _rowwise(
        "bwd_mid", bwd_mid, [(dy, d, 0), (d_n2, d, 0), (h1, d, 0), (mix, d, 0)], [g_post_mix, g_pre_mlp],
        [(d, F32), (d, MXU_DTYPE)], [d, d], tr, deps=[tok])
    d_m = _matmul("d_m", d_mix, wg_o, kind="nt", tm=tm, tn=tn_h, tk=d, out_dtypes=[F32])
    tok = exchange_start((4,), [d_m])
    gw_o = _matmul("gw_o", m_lp, d_mix, kind="tn", tm=_largest_tile(d, 2048, LANES), tn=tn_h, tk=tk_t, out_dtypes=[F32],
                   deps=[tok])

    def gate_bwd(i, rows, vecs):
        dm_t, ya_t, yb_t = rows[:3]
        ga, gb = gates_of(rows[3:], vecs[0])
        d_gpre = jnp.concatenate([dm_t * ya_t * ga * (1.0 - ga), dm_t * yb_t * gb * (1.0 - gb)], axis=1)
        return [dm_t * ga, dm_t * gb, d_gpre], [d_gpre]

    d_ya, d_yb, d_gpre, dg_b_gates = _rowwise(
        "gate_bwd", gate_bwd, [(d_m, d, 0), (y_a, d, 0), (y_b, d, 0)] + gate_rows, [b_gates],
        [(d, MXU_DTYPE), (d, MXU_DTYPE), (2 * d, MXU_DTYPE)], [2 * d], tr, deps=[tok])
    d_aact = _matmul("d_aact", d_ya, wg_pw, kind="nt", tm=tm, tn=d_conf // 2, tk=tn_pw, out_dtypes=[F32], deps=[tok])
    gw_pw = _matmul("gw_pw", a_act, d_ya, kind="tn", tm=_largest_tile(d_conf, 2048, LANES), tn=tn_pw, tk=tk_t,
                    out_dtypes=[F32], out_pieces=N_CHIPS)
    d_s = _matmul("d_s", d_yb, wg_sout, kind="nt", tm=tm, tn=d_conf // 2, tk=tn_pw, out_dtypes=[F32], deps=[tok])
    gw_sout = _matmul("gw_sout", s_lp, d_yb, kind="tn", tm=_largest_tile(d_conf, 2048, LANES), tn=tn_pw, tk=tk_t,
                      out_dtypes=[F32], out_pieces=N_CHIPS, deps=[tok])
    small_big = (3, 1, 2)
    tok = swap_start(small_big, [gw_o.reshape(N_CHIPS, d // N_CHIPS, d), gw_pw, gw_sout])

    def branch_bwd(i, rows, vecs):
        daact_t, ds_t, ac_t, c3_t, bg_t = rows
        xh, rstd, al = ln_parts(ac_t, vecs[0], vecs[1])
        sg = _sigmoid(al)
        d_al = daact_t * (sg * (1.0 + al * (1.0 - sg)))
        dxh = d_al * vecs[0]
        d_ac = rstd * (dxh - jnp.mean(dxh, axis=-1, keepdims=True) - xh * jnp.mean(dxh * xh, axis=-1, keepdims=True))
        return [d_ac, ds_t * bg_t, ds_t * c3_t], [d_al * xh, d_al, d_ac]

    d_ac, d_c3, d_bg, dg_ln_g, dg_ln_b, dg_dw_b = _rowwise(
        "branch_bwd", branch_bwd, [(d_aact, cw, 0), (d_s, cw, 0), (ac, cw, 0), (c3, cw, 0), (proj, cw, 2)],
        [conf_ln_g, conf_ln_b], [(d_conf, F32), (d_conf, F32), (d_conf, MXU_DTYPE)], [d_conf] * 3, tr, deps=[tok])
    d_av, d_ag, d_cg, d_v, dg_wdw, dg_w3 = _conv_bwd(proj, d_ac, d_c3, wdw, w3, d_conf)
    d_proj = jnp.concatenate([d_av, d_ag, d_bg, d_cg, d_v, d_gpre], axis=1)
    small_w = d_conf

    def pack(arrs):
        flat = jnp.concatenate([a.reshape(-1, small_w) for a in arrs], axis=0)
        return jnp.pad(flat, ((0, -flat.shape[0] % 8), (0, 0)))

    def unpack(buf, like):
        out, r0 = [], 0
        for a in like:
            nr = a.size // small_w
            out.append(buf[r0:r0 + nr].reshape(a.shape))
            r0 += nr
        return out

    def reduce_small(tag, arrs, deps):
        assert all(a.size % small_w == 0 for a in arrs)
        gathered = _allgather_small(f"allgather_small_{tag}", pack(arrs), deps)
        return unpack(_sum_blocks(f"sum_small_{tag}", gathered, 8), arrs)

    rep_w = [b_gates, conf_dw_b, conf_ln_g, conf_ln_b, g_post_mix, g_pre_mlp, g_post_mlp]
    rep_m = [m_b_gates, m_conf_dw_b, m_conf_ln_g, m_conf_ln_b, m_g_post_mix, m_g_pre_mlp, m_g_post_mlp]
    rep_v = [v_b_gates, v_conf_dw_b, v_conf_ln_g, v_conf_ln_b, v_g_post_mix, v_g_pre_mlp, v_g_post_mlp]
    *g_rep, g_wdw_full, g_w3_full = reduce_small(
        "mid", [dg_b_gates, dg_dw_b, dg_ln_g, dg_ln_b, dg_post_mix, dg_pre_mlp, dg_post_mlp, dg_wdw, dg_w3], [tok])
    sc = d_conf // N_CHIPS
    g_wdw = lax.dynamic_slice_in_dim(g_wdw_full, chip * sc, sc, axis=1)
    g_w3 = lax.dynamic_slice_in_dim(g_w3_full, chip * sc, sc, axis=1)
    tok = exchange_start(small_big, [g_w3_full])

    gw_in = _matmul("gw_in", n_lp, d_proj, kind="tn", tm=_largest_tile(d, 2048, LANES), tn=tn_in, tk=tk_t,
                    out_dtypes=[F32], out_pieces=N_CHIPS, deps=[tok])
    tok = swap_start((0,), [gw_in])
    d_n = _matmul("d_n", d_proj, wg_in, kind="nt", tm=tm, tn=tn_h, tk=in_cols // N_CHIPS, out_dtypes=[F32], deps=[tok])
    tok = exchange_start((0,), [d_n])

    early = [5, 4, 3, 1, 2]
    reduced = reduce_finish([(5,), (4,), small_big], [tok])
    join_sems, joining, tok = _join_start("join_start_early", [reduced[a] for a in early])

    def bwd_in(i, rows, vecs):
        d_h0a, dg = _rms_bwd(rows[2], vecs[0], rows[1])
        return [rows[0] + d_h0a], [dg]

    d_h0, dg_pre_mix = _rowwise("bwd_in", bwd_in, [(d_h1, d, 0), (d_n, d, 0), (h0, d, 0)], [g_pre_mix],
                                [(d, F32)], [d], tr, deps=[tok])
    grad_x = d_h0[N_META:t_real][None]

    big_m = [m_w_in, m_conf_w_pw, m_short_w_out, m_w_o, m_w_up, m_w_down]
    big_v = [v_w_in, v_conf_w_pw, v_short_w_out, v_w_o, v_w_up, v_w_down]
    big_res = {}

    def adam_group(members, joined):
        for a, j in zip(members, joined):
            big_res[a] = _adamw(f"adamw_big{a}", big[a], j.reshape(big[a].shape), big_m[a][0], big_v[a][0],
                                _largest_tile(big[a].shape[0], 256, 8))

    adam_group(early, _join_wait("join_wait_early", joining, join_sems, [d_h0]))
    rep_pack = [pack(arrs) for arrs in (rep_w, g_rep, rep_m, rep_v)]
    rep_res = [unpack(buf, rep_w) for buf in _adamw("adamw_rep", *rep_pack, rep_pack[0].shape[0])]
    col_res = {a: _adamw(f"adamw_col{a}", w, g, m, v, w.shape[0]) for a, (w, g, m, v) in (
        (1, (conf_dw_w[0], g_wdw, m_conf_dw_w[0], v_conf_dw_w[0])),
        (2, (short_dw_w[0], g_w3, m_short_dw_w[0], v_short_dw_w[0])))}
    reduced = reduce_finish([(0,)], [rep_res[1][0], col_res[1][1], col_res[2][1]] + [big_res[a][1] for a in early])
    join_sems, joining, tok = _join_start("join_start_late", [reduced[0]])
    g_pre_mix_full, g_meta_full = reduce_small("late", [dg_pre_mix, d_h0[:N_META]], [tok])
    g_meta = lax.dynamic_slice_in_dim(g_meta_full, chip * (d // N_CHIPS), d // N_CHIPS, axis=1)
    col_res[0] = _adamw("adamw_col0", meta, g_meta, m_meta, v_meta, meta.shape[0])
    pre_mix_res = _adamw("adamw_pre_mix", g_pre_mix, g_pre_mix_full, m_g_pre_mix, v_g_pre_mix, 1)
    rep_res = [[pre_mix_res[q]] + rep_res[q] for q in range(4)]
    adam_group([0], _join_wait("join_wait_late", joining, join_sems, [col_res[0][1], pre_mix_res[1]]))

    def leaf(q):
        r = lambda a: rep_res[q][a]
        b = lambda a: big_res[a][q][None]
        return [col_res[0][q], r(0), b(0), r(1), col_res[1][q][None], r(2), r(3), r(4), b(1), col_res[2][q][None], b(2),
                b(3), r(5), r(6), b(4), b(5), r(7)]

    return (loss, grad_x, *leaf(0), *leaf(1), *leaf(2), *leaf(3))
```

```python
import jax
import jax.numpy as jnp
from jax import lax
from jax.experimental import pallas as pl
from jax.experimental.pallas import tpu as pltpu

F32 = jnp.float32
BF16 = jnp.bfloat16
MXU_DTYPE = BF16
WIRE_DTYPE = BF16

N_META = 16
CONF_KERNEL = 31
SHORT_KERNEL = 3
CONV_PAD = 32
RMS_EPS = 1e-6
LN_EPS = 1e-5
ADAM_LR = 0.001
ADAM_B1 = 0.9
ADAM_B2 = 0.999
ADAM_EPS = 1e-08
ADAM_WD = 0.01
ADAM_STEP = 10

N_CHIPS = 4
MESH = pl.DeviceIdType.MESH
LANES = 128


def _sigmoid(z):
    return 1.0 / (1.0 + jnp.exp(-z))


ANY_SPEC = pl.BlockSpec(memory_space=pl.ANY)


def _matmul(name, a, b, *, kind, tm, tn, tk, out_dtypes, out_pieces=1, epilogue=None, extras=(), deps=()):
    pieces = b.shape[0] if b.ndim == 3 else 1
    if kind == "nn":
        m, kdim = a.shape
        n = b.shape[-1] * pieces
        dims = (((1,), (0,)), ((), ()))
        a_spec = pl.BlockSpec((tm, tk), lambda i, j, k: (i, k))
        if b.ndim == 2:
            b_spec = pl.BlockSpec((tk, tn), lambda i, j, k: (k, j))
        else:
            npp = b.shape[-1] // tn
            b_spec = pl.BlockSpec((None, tk, tn), lambda i, j, k: (j // npp, k, j % npp))
    elif kind == "nt":
        m, kdim = a.shape
        n = b.shape[-2]
        dims = (((1,), (1,)), ((), ()))
        a_spec = pl.BlockSpec((tm, tk), lambda i, j, k: (i, k))
        if b.ndim == 2:
            b_spec = pl.BlockSpec((tn, tk), lambda i, j, k: (j, k))
        else:
            kpp = b.shape[-1] // tk
            b_spec = pl.BlockSpec((None, tn, tk), lambda i, j, k: (k // kpp, j, k % kpp))
    else:
        kdim, m = a.shape
        n = b.shape[-1]
        dims = (((0,), (0,)), ((), ()))
        a_spec = pl.BlockSpec((tk, tm), lambda i, j, k: (k, i))
        b_spec = pl.BlockSpec((tk, tn), lambda i, j, k: (k, j))
    assert m % tm == 0 and n % tn == 0 and kdim % tk == 0, (name, m, n, kdim, tm, tn, tk)
    nk = kdim // tk
    if out_pieces == 1:
        out_shape = (m, n)
        out_spec = pl.BlockSpec((tm, tn), lambda i, j, k: (i, j))
    else:
        onpp = n // out_pieces // tn
        out_shape = (out_pieces, m, n // out_pieces)
        out_spec = pl.BlockSpec((None, tm, tn), lambda i, j, k: (j // onpp, i, j % onpp))
    n_ex, n_out, n_in = len(extras), len(out_dtypes), len(extras) + len(deps)
    if epilogue is None:
        epilogue = lambda acc: (acc,)

    def body(a_ref, b_ref, *rest):
        ex_refs, o_refs = rest[:n_ex], rest[n_in:n_in + n_out]
        prod = lax.dot_general(a_ref[...], b_ref[...], dims, preferred_element_type=F32)

        def finish(acc):
            tiles = epilogue(acc, *[r[...] for r in ex_refs])
            for o_ref, t in zip(o_refs, tiles):
                o_ref[...] = t.astype(o_ref.dtype)

        if nk == 1:
            finish(prod)
        else:
            acc_ref = rest[n_in + n_out]
            k = pl.program_id(2)

            @pl.when(k == 0)
            def _():
                acc_ref[...] = prod

            @pl.when(jnp.logical_and(k > 0, k < nk - 1))
            def _():
                acc_ref[...] += prod

            @pl.when(k == nk - 1)
            def _():
                finish(acc_ref[...] + prod)

    ex_specs = [pl.BlockSpec((tm, tn), lambda i, j, k: (i, j)) for _ in extras]
    res = pl.pallas_call(
        body,
        name=name,
        grid=(m // tm, n // tn, nk),
        in_specs=[a_spec, b_spec, *ex_specs] + [ANY_SPEC] * len(deps),
        out_specs=[out_spec] * n_out,
        out_shape=[jax.ShapeDtypeStruct(out_shape, d) for d in out_dtypes],
        scratch_shapes=[pltpu.VMEM((tm, tn), F32)] if nk > 1 else [],
        compiler_params=pltpu.CompilerParams(dimension_semantics=("parallel", "parallel", "arbitrary")),
    )(a, b, *extras, *deps)
    return res[0] if n_out == 1 else res


ROW_CHUNK = 16
SUBLANES = 8


def _rowwise(name, fn, rows, vecs, outs, sums, tr, deps=()):
    t = rows[0][0].shape[0]
    assert t % tr == 0 and tr % ROW_CHUNK == 0
    n_r, n_v, n_o, n_s = len(rows), len(vecs), len(outs), len(sums)
    n_in = n_r + n_v + len(deps)
    n_steps = t // tr

    def body(*refs):
        r_in, v_in = refs[:n_r], refs[n_r:n_r + n_v]
        o_refs = refs[n_in:n_in + n_o]
        s_refs = refs[n_in + n_o:n_in + n_o + n_s]
        acc_refs = refs[n_in + n_o + n_s:]
        i = pl.program_id(0)

        @pl.when(i == 0)
        def _():
            for acc_ref in acc_refs:
                acc_ref[...] = jnp.zeros(acc_ref.shape, F32)

        def chunk(ci):
            r0 = ci * ROW_CHUNK
            sl = pl.ds(r0, ROW_CHUNK)
            o_tiles, s_tiles = fn(i * tr + r0, [r[sl, :] for r in r_in], [v[...] for v in v_in])
            for o_ref, tile in zip(o_refs, o_tiles):
                o_ref[sl, :] = tile.astype(o_ref.dtype)
            for acc_ref, tile in zip(acc_refs, s_tiles):
                part = tile[0:SUBLANES]
                for s in range(1, ROW_CHUNK // SUBLANES):
                    part = part + tile[s * SUBLANES:(s + 1) * SUBLANES]
                acc_ref[...] += part

        for ci in range(tr // ROW_CHUNK):
            chunk(ci)

        @pl.when(i == n_steps - 1)
        def _():
            for s_ref, acc_ref in zip(s_refs, acc_refs):
                s_ref[...] = jnp.sum(acc_ref[...], axis=0, keepdims=True)

    def row_spec(width, blk):
        return pl.BlockSpec((tr, width), lambda i: (i, blk))

    res = pl.pallas_call(
        body,
        name=name,
        grid=(t // tr,),
        in_specs=[row_spec(w, blk) for _, w, blk in rows]
        + [pl.BlockSpec(v.shape, lambda i: (0, 0)) for v in vecs] + [ANY_SPEC] * len(deps),
        out_specs=[pl.BlockSpec((tr, c), lambda i: (i, 0)) for c, _ in outs]
        + [pl.BlockSpec((1, c), lambda i: (0, 0)) for c in sums],
        out_shape=[jax.ShapeDtypeStruct((t, c), d) for c, d in outs]
        + [jax.ShapeDtypeStruct((1, c), F32) for c in sums],
        scratch_shapes=[pltpu.VMEM((SUBLANES, c), F32) for c in sums],
        compiler_params=pltpu.CompilerParams(dimension_semantics=("arbitrary",)),
    )(*[r[0] for r in rows], *vecs, *deps)
    return res


def _rms_fwd(x, g):
    r = lax.rsqrt(jnp.mean(x * x, axis=-1, keepdims=True) + RMS_EPS)
    return x * r * g


def _rms_bwd(x, g, dy):
    r = lax.rsqrt(jnp.mean(x * x, axis=-1, keepdims=True) + RMS_EPS)
    xn = x * r
    dxn = dy * g
    dx = r * (dxn - xn * jnp.mean(dxn * xn, axis=-1, keepdims=True))
    return dx, dy * xn


CONV_ROWS = 64
CONV_LANES = 128


def _conv_fwd(proj, wdw, bdw, w3, d_conf):
    t = proj.shape[0]
    cl = CONV_LANES
    nb = d_conf // cl
    nchunk = t // CONV_ROWS
    assert t % CONV_ROWS == 0

    def body(av_ref, ag_ref, cg_ref, v_ref, wdw_ref, bdw_ref, w3_ref, ac_ref, c3_ref, apad, cpad):
        zeros = jnp.zeros((CONV_PAD, cl), F32)
        apad[0:CONV_PAD, :] = zeros
        cpad[0:CONV_PAD, :] = zeros
        apad[CONV_PAD:, :] = av_ref[...] * _sigmoid(ag_ref[...])
        cpad[CONV_PAD:, :] = cg_ref[...] * v_ref[...]

        def chunk(ci, carry):
            base = pl.multiple_of(ci * CONV_ROWS, 8)
            acc = jnp.zeros((CONV_ROWS, cl), F32) + bdw_ref[...]
            for k in range(CONF_KERNEL):
                off = CONV_PAD - (CONF_KERNEL - 1) + k
                acc = acc + apad[pl.ds(base + off, CONV_ROWS), :] * wdw_ref[k:k + 1, :]
            ac_ref[pl.ds(base, CONV_ROWS), :] = acc
            acc3 = jnp.zeros((CONV_ROWS, cl), F32)
            for k in range(SHORT_KERNEL):
                off = CONV_PAD - (SHORT_KERNEL - 1) + k
                acc3 = acc3 + cpad[pl.ds(base + off, CONV_ROWS), :] * w3_ref[k:k + 1, :]
            c3_ref[pl.ds(base, CONV_ROWS), :] = acc3
            return carry

        lax.fori_loop(0, nchunk, chunk, 0)

    def col(blk0):
        return pl.BlockSpec((t, cl), lambda j: (0, blk0 + j))

    return pl.pallas_call(
        body,
        name="conv_fwd",
        grid=(nb,),
        in_specs=[col(0), col(nb), col(3 * nb), col(4 * nb),
                  pl.BlockSpec((CONF_KERNEL, cl), lambda j: (0, j)),
                  pl.BlockSpec((1, cl), lambda j: (0, j)),
                  pl.BlockSpec((SHORT_KERNEL, cl), lambda j: (0, j))],
        out_specs=[pl.BlockSpec((t, cl), lambda j: (0, j))] * 2,
        out_shape=[jax.ShapeDtypeStruct((t, d_conf), F32)] * 2,
        scratch_shapes=[pltpu.VMEM((t + CONV_PAD, cl), F32)] * 2,
        compiler_params=pltpu.CompilerParams(dimension_semantics=("parallel",)),
    )(proj, proj, proj, proj, wdw, bdw, w3)


def _conv_bwd(proj, d_ac, d_c3, wdw, w3, d_conf):
    t = proj.shape[0]
    cl = CONV_LANES
    nb = d_conf // cl
    nchunk = t // CONV_ROWS
    nsub = CONV_ROWS // 8

    def fold(p):
        r = p[0:8]
        for s in range(1, nsub):
            r = r + p[8 * s:8 * s + 8]
        return r

    def body(av_ref, ag_ref, cg_ref, v_ref, dac_ref, dc3_ref, wdw_ref, w3_ref,
             dav_ref, dag_ref, dcg_ref, dv_ref, dwdw_ref, dw3_ref, apad, cpad, dapad, dcpad):
        zeros = jnp.zeros((CONV_PAD, cl), F32)
        apad[0:CONV_PAD, :] = zeros
        cpad[0:CONV_PAD, :] = zeros
        apad[CONV_PAD:, :] = av_ref[...] * _sigmoid(ag_ref[...])
        cpad[CONV_PAD:, :] = cg_ref[...] * v_ref[...]
        dapad[0:t, :] = dac_ref[...]
        dcpad[0:t, :] = dc3_ref[...]
        dapad[t:, :] = zeros
        dcpad[t:, :] = zeros

        def chunk(ci, accs):
            base = pl.multiple_of(ci * CONV_ROWS, 8)
            rows = pl.ds(base, CONV_ROWS)
            da = jnp.zeros((CONV_ROWS, cl), F32)
            for k in range(CONF_KERNEL):
                da = da + dapad[pl.ds(base + (CONF_KERNEL - 1 - k), CONV_ROWS), :] * wdw_ref[k:k + 1, :]
            dcv = jnp.zeros((CONV_ROWS, cl), F32)
            for k in range(SHORT_KERNEL):
                dcv = dcv + dcpad[pl.ds(base + (SHORT_KERNEL - 1 - k), CONV_ROWS), :] * w3_ref[k:k + 1, :]
            av, sg = av_ref[rows, :], _sigmoid(ag_ref[rows, :])
            dav_ref[rows, :] = (da * sg).astype(dav_ref.dtype)
            dag_ref[rows, :] = (da * av * sg * (1.0 - sg)).astype(dag_ref.dtype)
            dcg_ref[rows, :] = (dcv * v_ref[rows, :]).astype(dcg_ref.dtype)
            dv_ref[rows, :] = (dcv * cg_ref[rows, :]).astype(dv_ref.dtype)
            d_out, d_out3 = dac_ref[rows, :], dc3_ref[rows, :]
            new = []
            for k in range(CONF_KERNEL):
                off = CONV_PAD - (CONF_KERNEL - 1) + k
                new.append(accs[k] + fold(d_out * apad[pl.ds(base + off, CONV_ROWS), :]))
            for k in range(SHORT_KERNEL):
                off = CONV_PAD - (SHORT_KERNEL - 1) + k
                new.append(accs[CONF_KERNEL + k] + fold(d_out3 * cpad[pl.ds(base + off, CONV_ROWS), :]))
            return tuple(new)

        init = tuple(jnp.zeros((8, cl), F32) for _ in range(CONF_KERNEL + SHORT_KERNEL))
        accs = lax.fori_loop(0, nchunk, chunk, init)
        for k in range(CONF_KERNEL):
            dwdw_ref[k:k + 1, :] = jnp.sum(accs[k], axis=0, keepdims=True)
        for k in range(SHORT_KERNEL):
            dw3_ref[k:k + 1, :] = jnp.sum(accs[CONF_KERNEL + k], axis=0, keepdims=True)

    def col(blk0):
        return pl.BlockSpec((t, cl), lambda j: (0, blk0 + j))

    own = pl.BlockSpec((t, cl), lambda j: (0, j))
    return pl.pallas_call(
        body,
        name="conv_bwd",
        grid=(nb,),
        in_specs=[col(0), col(nb), col(3 * nb), col(4 * nb), own, own,
                  pl.BlockSpec((CONF_KERNEL, cl), lambda j: (0, j)),
                  pl.BlockSpec((SHORT_KERNEL, cl), lambda j: (0, j))],
        out_specs=[own] * 4 + [pl.BlockSpec((CONF_KERNEL, cl), lambda j: (0, j)),
                               pl.BlockSpec((SHORT_KERNEL, cl), lambda j: (0, j))],
        out_shape=[jax.ShapeDtypeStruct((t, d_conf), MXU_DTYPE)] * 4
        + [jax.ShapeDtypeStruct((CONF_KERNEL, d_conf), F32), jax.ShapeDtypeStruct((SHORT_KERNEL, d_conf), F32)],
        scratch_shapes=[pltpu.VMEM((t + CONV_PAD, cl), F32)] * 4,
        compiler_params=pltpu.CompilerParams(dimension_semantics=("parallel",)),
    )(proj, proj, proj, proj, d_ac, d_c3, wdw, w3)


def _elementwise(name, fn, ins, out_dtypes, tr, deps=()):
    ins = [(a, ()) if not isinstance(a, tuple) else a for a in ins]
    r, c = ins[0][0].shape[-2:]
    assert r % tr == 0, (name, r, tr)
    n_in = len(ins)

    def body(*refs):
        tiles = fn(*[x[...] for x in refs[:n_in]])
        for o_ref, tile in zip(refs[n_in + len(deps):], tiles):
            o_ref[...] = tile.astype(o_ref.dtype)

    def spec(lead):
        return pl.BlockSpec((None,) * len(lead) + (tr, c), lambda i: (*lead, i, 0))

    res = pl.pallas_call(
        body,
        name=name,
        grid=(r // tr,),
        in_specs=[spec(lead) for _, lead in ins] + [ANY_SPEC] * len(deps),
        out_specs=[pl.BlockSpec((tr, c), lambda i: (i, 0))] * len(out_dtypes),
        out_shape=[jax.ShapeDtypeStruct((r, c), d) for d in out_dtypes],
        compiler_params=pltpu.CompilerParams(dimension_semantics=("parallel",)),
    )(*[a for a, _ in ins], *deps)
    return res


def _adamw_tiles(w, g, m, v):
    m = ADAM_B1 * m + (1.0 - ADAM_B1) * g
    v = ADAM_B2 * v + (1.0 - ADAM_B2) * jnp.square(g)
    m_hat = m / (1.0 - ADAM_B1 ** ADAM_STEP)
    v_hat = v / (1.0 - ADAM_B2 ** ADAM_STEP)
    delta = -ADAM_LR * (m_hat / (jnp.sqrt(v_hat) + ADAM_EPS) + ADAM_WD * w)
    return g, delta, m, v


def _adamw(name, w, g, m, v, tr, deps=()):
    shape = w.shape
    flat = [a.reshape(shape[-2:]) if a.ndim > 2 else a for a in (w, g, m, v)]
    res = _elementwise(name, _adamw_tiles, flat, [F32] * 4, tr, deps)
    return [a.reshape(shape) for a in res]


def _small_adamw(name, total, entries):
    sw = total.shape[1]
    n = len(entries)

    def body(total_ref, *refs):
        ins, outs = refs[:3 * n], refs[3 * n:]
        for e, (row0, w, _, _) in enumerate(entries):
            for q in range(w.shape[1] // sw):
                cols = slice(q * sw, (q + 1) * sw)
                tiles = _adamw_tiles(ins[3 * e][:, cols], total_ref[row0 + q:row0 + q + 1, :],
                                     ins[3 * e + 1][:, cols], ins[3 * e + 2][:, cols])
                for o_ref, tile in zip(outs[4 * e:4 * e + 4], tiles):
                    o_ref[:, cols] = tile

    res = pl.pallas_call(
        body,
        name=name,
        out_shape=[jax.ShapeDtypeStruct(w.shape, F32) for _, w, _, _ in entries for _ in range(4)],
    )(total, *[a for _, w, m, v in entries for a in (w, m, v)])
    return [list(res[4 * e:4 * e + 4]) for e in range(n)]


def _pair_sum(name, p, q, core, tr):
    n_p, _, hr, c = p.shape
    assert hr % tr == 0

    def body(core_ref, p_ref, q_ref, o_ref):
        o_ref[...] = (p_ref[...] + q_ref[...]).astype(o_ref.dtype)

    return pl.pallas_call(
        body,
        name=name,
        grid_spec=pltpu.PrefetchScalarGridSpec(
            num_scalar_prefetch=1,
            grid=(n_p, hr // tr),
            in_specs=[pl.BlockSpec((None, None, tr, c), lambda a, i, core_ref: (a, core_ref[0], i, 0)),
                      pl.BlockSpec((None, tr, c), lambda a, i, core_ref: (a, i, 0))],
            out_specs=pl.BlockSpec((None, tr, c), lambda a, i, core_ref: (a, i, 0)),
        ),
        out_shape=jax.ShapeDtypeStruct((n_p, hr, c), WIRE_DTYPE),
        compiler_params=pltpu.CompilerParams(dimension_semantics=("parallel", "parallel")),
    )(core, p, q)


def _into_slot(name, w, slots, slot, dtype, tr, deps=()):
    r, c = w.shape
    assert r % tr == 0

    def body(slot_ref, w_ref, *rest):
        o_ref = rest[len(deps)]
        o_ref[...] = w_ref[...].astype(o_ref.dtype)

    return pl.pallas_call(
        body,
        name=name,
        grid_spec=pltpu.PrefetchScalarGridSpec(
            num_scalar_prefetch=1,
            grid=(r // tr,),
            in_specs=[pl.BlockSpec((tr, c), lambda i, slot_ref: (i, 0))] + [ANY_SPEC] * len(deps),
            out_specs=pl.BlockSpec((None, tr, c), lambda i, slot_ref: (slot_ref[0], i, 0)),
        ),
        out_shape=jax.ShapeDtypeStruct((slots, r, c), dtype),
        compiler_params=pltpu.CompilerParams(dimension_semantics=("parallel",)),
    )(slot, w, *deps)


def _sum_pieces(name, own, rb, place, tr):
    n_p, hr, c = rb.shape
    assert hr % tr == 0

    def body(place_ref, own_ref, *refs):
        chip = place_ref[0]
        acc = None
        for k in range(n_p):
            tile = jnp.where(chip == k, own_ref[...], refs[k][...]).astype(F32)
            acc = tile if acc is None else acc + tile
        refs[n_p][...] = acc

    def landed(k):
        return pl.BlockSpec((None, tr, c), lambda i, place_ref: (jnp.where(place_ref[0] == k, (k + 1) % n_p, k), i, 0))

    return pl.pallas_call(
        body,
        name=name,
        grid_spec=pltpu.PrefetchScalarGridSpec(
            num_scalar_prefetch=1,
            grid=(hr // tr,),
            in_specs=[pl.BlockSpec((None, tr, c), lambda i, place_ref: (place_ref[0], i, 0))]
            + [landed(k) for k in range(n_p)],
            out_specs=pl.BlockSpec((None, tr, c), lambda i, place_ref: (place_ref[1], i, 0)),
        ),
        out_shape=jax.ShapeDtypeStruct((2, hr, c), F32),
        compiler_params=pltpu.CompilerParams(dimension_semantics=("parallel",)),
    )(place, own, *([rb] * n_p))


HBM_SPEC = pl.BlockSpec(memory_space=pl.ANY)


def _place():
    x, y, c = lax.axis_index("x"), lax.axis_index("y"), lax.axis_index("c")
    chips = [(1 - x, y), (x, 1 - y), (1 - x, 1 - y)]
    return x, y, c, chips


HBM_ONLY = pl.BlockSpec(memory_space=pltpu.HBM)
SEM_SPEC = pl.BlockSpec(memory_space=pltpu.SEMAPHORE)
DATAFLOW = pltpu.SideEffectType.DATAFLOW_SIDE_EFFECTING


def _in_hbm(a):
    return pltpu.with_memory_space_constraint(a, pltpu.HBM)


def _shard_part(ref, is_split, slot, h):
    if not is_split:
        return ref.at[slot]
    hr = ref.shape[1] // 2
    return ref.at[slot, pl.ds(h * hr, hr), :]


TOKEN = jax.ShapeDtypeStruct((8, LANES), F32)
VMEM_SPEC = pl.BlockSpec(memory_space=pltpu.VMEM)


def _gather_start(name, bufs, split, groups, deps=()):
    n, ng = len(bufs), len(groups)

    def body(*refs):
        ins, sems = refs[:n], refs[n + len(deps):n + len(deps) + 2 * ng]
        refs[-1][...] = jnp.zeros(TOKEN.shape, TOKEN.dtype)
        x, y, c, chips = _place()
        me = 2 * x + y
        for g, members in enumerate(groups):
            for s, a in enumerate(members):
                mine = _shard_part(ins[a], split[a], me, c)
                for j, chip in enumerate(chips):
                    pltpu.make_async_remote_copy(
                        src_ref=mine, dst_ref=mine, send_sem=sems[2 * g].at[3 * s + j], recv_sem=sems[2 * g + 1].at[3 * s + j],
                        device_id=(*chip, c), device_id_type=MESH).start()

    res = pl.pallas_call(
        body,
        name=name,
        in_specs=[HBM_ONLY] * n + [ANY_SPEC] * len(deps),
        out_specs=[SEM_SPEC] * (2 * ng) + [HBM_ONLY] * n + [VMEM_SPEC],
        out_shape=[pltpu.SemaphoreType.DMA((3 * len(members),)) for members in groups for _ in range(2)]
        + [pltpu.HBM(b.shape, b.dtype) for b in bufs] + [TOKEN],
        input_output_aliases={a: 2 * ng + a for a in range(n)},
        compiler_params=pltpu.CompilerParams(has_side_effects=DATAFLOW),
    )(*[_in_hbm(b) for b in bufs], *deps)
    return [(res[2 * g], res[2 * g + 1]) for g in range(ng)], list(res[2 * ng:2 * ng + n]), res[-1]


def _gather_wait(name, bufs, split, sems, after):
    n = len(bufs)

    def body(*refs):
        ins, send_sems, recv_sems = refs[:n], refs[n], refs[n + 1]
        x, y, c, chips = _place()
        me = 2 * x + y
        for s in range(n):
            for j, chip in enumerate(chips):
                copy = pltpu.make_async_remote_copy(
                    src_ref=_shard_part(ins[s], split[s], me, c),
                    dst_ref=_shard_part(ins[s], split[s], 2 * chip[0] + chip[1], c),
                    send_sem=send_sems.at[3 * s + j], recv_sem=recv_sems.at[3 * s + j],
                    device_id=(*chip, c), device_id_type=MESH)
                copy.wait_send()
                copy.wait_recv()

    res = pl.pallas_call(
        body,
        name=name,
        in_specs=[HBM_ONLY] * n + [SEM_SPEC, SEM_SPEC] + [ANY_SPEC] * len(after),
        out_specs=[HBM_ONLY] * n,
        out_shape=[pltpu.HBM(b.shape, b.dtype) for b in bufs],
        input_output_aliases={a: a for a in range(n)},
        compiler_params=pltpu.CompilerParams(has_side_effects=DATAFLOW),
    )(*bufs, *sems, *after)
    return list(res)


def _pass_halves(name, bufs):
    n = len(bufs)

    def body(*refs):
        outs = refs[n:2 * n]
        send_sems, recv_sems = refs[2 * n:]
        x, y, c, chips = _place()
        sibling = (x, y, 1 - c)

        def copy(a, j, h):
            blk = _shard_part(outs[a], True, 2 * chips[j][0] + chips[j][1], h)
            return pltpu.make_async_remote_copy(
                src_ref=blk, dst_ref=blk, send_sem=send_sems.at[3 * a + j], recv_sem=recv_sems.at[3 * a + j],
                device_id=sibling, device_id_type=MESH)

        sends = [copy(a, j, c) for a in range(n) for j in range(3)]
        for cp in sends:
            cp.start()
        for a in range(n):
            for j in range(3):
                copy(a, j, 1 - c).wait_recv()
        for cp in sends:
            cp.wait_send()

    res = pl.pallas_call(
        body,
        name=name,
        in_specs=[HBM_SPEC] * n,
        out_specs=[HBM_SPEC] * n,
        out_shape=[jax.ShapeDtypeStruct(b.shape, b.dtype) for b in bufs],
        input_output_aliases={a: a for a in range(n)},
        scratch_shapes=[pltpu.SemaphoreType.DMA((3 * n,)), pltpu.SemaphoreType.DMA((3 * n,))],
    )(*bufs)
    return list(res)


def _exchange_start(name, pairs):
    n = len(pairs)

    def body(*refs):
        pair_refs, land_refs, send_sems, recv_sems = refs[:n], refs[n:2 * n], refs[2 * n], refs[2 * n + 1]
        x, y, c, chips = _place()
        me = 2 * x + y
        for a in range(n):
            for j, chip in enumerate(chips):
                pltpu.make_async_remote_copy(
                    src_ref=pair_refs[a].at[2 * chip[0] + chip[1]], dst_ref=land_refs[a].at[me],
                    send_sem=send_sems.at[3 * a + j], recv_sem=recv_sems.at[3 * a + j],
                    device_id=(*chip, c), device_id_type=MESH).start()
        refs[-1][...] = jnp.zeros(TOKEN.shape, TOKEN.dtype)

    res = pl.pallas_call(
        body,
        name=name,
        in_specs=[HBM_ONLY] * (2 * n),
        out_specs=[SEM_SPEC, SEM_SPEC] + [HBM_ONLY] * (2 * n) + [VMEM_SPEC],
        out_shape=[pltpu.SemaphoreType.DMA((3 * n,)), pltpu.SemaphoreType.DMA((3 * n,))]
        + [pltpu.HBM(p.shape, p.dtype) for p in pairs] * 2 + [TOKEN],
        input_output_aliases={a: 2 + a for a in range(2 * n)},
        compiler_params=pltpu.CompilerParams(has_side_effects=DATAFLOW),
    )(*[_in_hbm(p) for p in pairs], *[_in_hbm(lax.empty(p.shape, p.dtype)) for p in pairs])
    return (res[0], res[1]), list(res[2:2 + n]), list(res[2 + n:2 + 2 * n]), res[-1]


def _exchange_wait(name, groups, after):
    sizes = [len(pairs) for _, pairs, _ in groups]
    n_buf = 2 * sum(sizes)

    def body(*refs):
        x, y, c, chips = _place()
        at_buf, at_sem = 0, n_buf
        for n in sizes:
            pair_refs, land_refs = refs[at_buf:at_buf + n], refs[at_buf + n:at_buf + 2 * n]
            send_sems, recv_sems = refs[at_sem], refs[at_sem + 1]
            at_buf, at_sem = at_buf + 2 * n, at_sem + 2
            for a in range(n):
                for j, chip in enumerate(chips):
                    k = 2 * chip[0] + chip[1]
                    copy = pltpu.make_async_remote_copy(
                        src_ref=pair_refs[a].at[k], dst_ref=land_refs[a].at[k],
                        send_sem=send_sems.at[3 * a + j], recv_sem=recv_sems.at[3 * a + j],
                        device_id=(*chip, c), device_id_type=MESH)
                    copy.wait_send()
                    copy.wait_recv()

    bufs = [b for _, pairs, lands in groups for b in (*pairs, *lands)]
    sems = [s for group_sems, _, _ in groups for s in group_sems]
    res = pl.pallas_call(
        body,
        name=name,
        in_specs=[HBM_ONLY] * n_buf + [SEM_SPEC] * len(sems) + [ANY_SPEC] * len(after),
        out_specs=[HBM_ONLY] * n_buf,
        out_shape=[pltpu.HBM(b.shape, b.dtype) for b in bufs],
        input_output_aliases={a: a for a in range(n_buf)},
        compiler_params=pltpu.CompilerParams(has_side_effects=DATAFLOW),
    )(*bufs, *sems, *after)
    out, at = [], 0
    for n in sizes:
        out.append((list(res[at:at + n]), list(res[at + n:at + 2 * n])))
        at += 2 * n
    return out


def _swap_start(name, halves):
    n = len(halves)
    n_p = halves[0].shape[0]
    land_shapes = [(n_p, *h.shape[2:]) for h in halves]

    def body(*refs):
        half_refs, land_refs, send_sems, recv_sems = refs[:n], refs[n:2 * n], refs[2 * n], refs[2 * n + 1]
        x, y, c, _ = _place()
        for a in range(n):
            for p in range(n_p):
                pltpu.make_async_remote_copy(
                    src_ref=half_refs[a].at[p, 1 - c], dst_ref=land_refs[a].at[p],
                    send_sem=send_sems.at[n_p * a + p], recv_sem=recv_sems.at[n_p * a + p],
                    device_id=(x, y, 1 - c), device_id_type=MESH).start()
        refs[-1][...] = jnp.zeros(TOKEN.shape, TOKEN.dtype)

    res = pl.pallas_call(
        body,
        name=name,
        in_specs=[HBM_ONLY] * (2 * n),
        out_specs=[SEM_SPEC, SEM_SPEC] + [HBM_ONLY] * (2 * n) + [VMEM_SPEC],
        out_shape=[pltpu.SemaphoreType.DMA((n_p * n,)), pltpu.SemaphoreType.DMA((n_p * n,))]
        + [pltpu.HBM(h.shape, h.dtype) for h in halves]
        + [pltpu.HBM(s, h.dtype) for s, h in zip(land_shapes, halves)] + [TOKEN],
        input_output_aliases={a: 2 + a for a in range(2 * n)},
        compiler_params=pltpu.CompilerParams(has_side_effects=DATAFLOW),
    )(*[_in_hbm(h) for h in halves], *[_in_hbm(lax.empty(s, h.dtype)) for s, h in zip(land_shapes, halves)])
    return (res[0], res[1]), list(res[2:2 + n]), list(res[2 + n:2 + 2 * n]), res[-1]


def _swap_wait(name, halves, lands, sems, after):
    n = len(halves)
    n_p = halves[0].shape[0]

    def body(*refs):
        half_refs, land_refs, send_sems, recv_sems = refs[:n], refs[n:2 * n], refs[2 * n], refs[2 * n + 1]
        x, y, c, _ = _place()
        for a in range(n):
            for p in range(n_p):
                copy = pltpu.make_async_remote_copy(
                    src_ref=half_refs[a].at[p, 1 - c], dst_ref=land_refs[a].at[p],
                    send_sem=send_sems.at[n_p * a + p], recv_sem=recv_sems.at[n_p * a + p],
                    device_id=(x, y, 1 - c), device_id_type=MESH)
                copy.wait_send()
                copy.wait_recv()

    res = pl.pallas_call(
        body,
        name=name,
        in_specs=[HBM_ONLY] * (2 * n) + [SEM_SPEC, SEM_SPEC] + [ANY_SPEC] * len(after),
        out_specs=[HBM_ONLY] * (2 * n),
        out_shape=[pltpu.HBM(b.shape, b.dtype) for b in (*halves, *lands)],
        input_output_aliases={a: a for a in range(2 * n)},
        compiler_params=pltpu.CompilerParams(has_side_effects=DATAFLOW),
    )(*halves, *lands, *sems, *after)
    return list(res[:n]), list(res[n:])


def _join_start(name, bufs):
    n = len(bufs)

    def body(*refs):
        ins, send_sems, recv_sems = refs[:n], refs[n], refs[n + 1]
        x, y, c, _ = _place()
        for a in range(n):
            pltpu.make_async_remote_copy(
                src_ref=ins[a].at[c], dst_ref=ins[a].at[c], send_sem=send_sems.at[a], recv_sem=recv_sems.at[a],
                device_id=(x, y, 1 - c), device_id_type=MESH).start()
        refs[-1][...] = jnp.zeros(TOKEN.shape, TOKEN.dtype)

    res = pl.pallas_call(
        body,
        name=name,
        in_specs=[HBM_ONLY] * n,
        out_specs=[SEM_SPEC, SEM_SPEC] + [HBM_ONLY] * n + [VMEM_SPEC],
        out_shape=[pltpu.SemaphoreType.DMA((n,)), pltpu.SemaphoreType.DMA((n,))]
        + [pltpu.HBM(b.shape, b.dtype) for b in bufs] + [TOKEN],
        input_output_aliases={a: 2 + a for a in range(n)},
        compiler_params=pltpu.CompilerParams(has_side_effects=DATAFLOW),
    )(*[_in_hbm(b) for b in bufs])
    return (res[0], res[1]), list(res[2:2 + n]), res[-1]


def _join_wait(name, bufs, sems, after):
    n = len(bufs)

    def body(*refs):
        ins, send_sems, recv_sems = refs[:n], refs[n], refs[n + 1]
        x, y, c, _ = _place()
        for a in range(n):
            copy = pltpu.make_async_remote_copy(
                src_ref=ins[a].at[c], dst_ref=ins[a].at[1 - c], send_sem=send_sems.at[a], recv_sem=recv_sems.at[a],
                device_id=(x, y, 1 - c), device_id_type=MESH)
            copy.wait_send()
            copy.wait_recv()

    res = pl.pallas_call(
        body,
        name=name,
        in_specs=[HBM_ONLY] * n + [SEM_SPEC, SEM_SPEC] + [ANY_SPEC] * len(after),
        out_specs=[HBM_ONLY] * n,
        out_shape=[pltpu.HBM(b.shape, b.dtype) for b in bufs],
        input_output_aliases={a: a for a in range(n)},
        compiler_params=pltpu.CompilerParams(has_side_effects=DATAFLOW),
    )(*bufs, *sems, *after)
    return list(res)


def _allgather_small(name, block, deps=()):
    m_per, n = block.shape

    def body(x_ref, *rest):
        out_ref, send_sems, recv_sems, local_sem = rest[len(deps):]
        x, y, c, chips = _place()
        me, sibling = (x, y, c), (x, y, 1 - c)

        def rows(px, py, pc):
            return out_ref.at[pl.ds((4 * px + 2 * py + pc) * m_per, m_per), :]

        def copy(k, blk, to, src=None):
            return pltpu.make_async_remote_copy(
                src_ref=rows(*blk) if src is None else src, dst_ref=rows(*blk),
                send_sem=send_sems.at[k], recv_sem=recv_sems.at[k], device_id=to, device_id_type=MESH)

        mine = pltpu.make_async_copy(x_ref, rows(*me), local_sem)
        mine.start()
        first = [copy(0, me, sibling, src=x_ref)]
        first += [copy(1 + j, me, (*chip, c), src=x_ref) for j, chip in enumerate(chips)]
        for cp in first:
            cp.start()
        passed = [copy(4 + j, (*chip, c), sibling) for j, chip in enumerate(chips)]
        for j, chip in enumerate(chips):
            copy(1 + j, (*chip, c), me).wait_recv()
            passed[j].start()
        copy(0, sibling, me).wait_recv()
        for j, chip in enumerate(chips):
            copy(4 + j, (*chip, 1 - c), me).wait_recv()
        for cp in first + passed:
            cp.wait_send()
        mine.wait()

    return pl.pallas_call(
        body,
        name=name,
        out_shape=jax.ShapeDtypeStruct((8 * m_per, n), block.dtype),
        in_specs=[pl.BlockSpec(memory_space=pltpu.VMEM)] + [ANY_SPEC] * len(deps),
        out_specs=pl.BlockSpec(memory_space=pltpu.VMEM),
        scratch_shapes=[pltpu.SemaphoreType.DMA((7,)), pltpu.SemaphoreType.DMA((7,)), pltpu.SemaphoreType.DMA],
    )(block, *deps)


def _sum_blocks(name, gathered, n_blocks):
    r = gathered.shape[0] // n_blocks
    c = gathered.shape[1]

    def body(g_ref, o_ref):
        acc = g_ref[0:r, :]
        for b in range(1, n_blocks):
            acc = acc + g_ref[b * r:(b + 1) * r, :]
        o_ref[...] = acc

    return pl.pallas_call(body, name=name, out_shape=jax.ShapeDtypeStruct((r, c), F32))(gathered)


def _largest_tile(n, cap, mult):
    best = None
    for d in range(mult, min(n, cap) + 1, mult):
        if n % d == 0:
            best = d
    assert best is not None, (n, cap, mult)
    return best


def kernel(x, meta, g_pre_mix, w_in, b_gates, conf_dw_w, conf_dw_b, conf_ln_g, conf_ln_b, conf_w_pw, short_dw_w, short_w_out, w_o, g_post_mix, g_pre_mlp, w_up, w_down, g_post_mlp, loss_target, m_meta, m_g_pre_mix, m_w_in, m_b_gates, m_conf_dw_w, m_conf_dw_b, m_conf_ln_g, m_conf_ln_b, m_conf_w_pw, m_short_dw_w, m_short_w_out, m_w_o, m_g_post_mix, m_g_pre_mlp, m_w_up, m_w_down, m_g_post_mlp, v_meta, v_g_pre_mix, v_w_in, v_b_gates, v_conf_dw_w, v_conf_dw_b, v_conf_ln_g, v_conf_ln_b, v_conf_w_pw, v_short_dw_w, v_short_w_out, v_w_o, v_g_post_mix, v_g_pre_mlp, v_w_up, v_w_down, v_g_post_mlp):
    seq, d = x.shape[1], x.shape[2]
    t_real = seq + N_META
    t = -(-t_real // LANES) * LANES
    d_conf = conf_dw_b.shape[1]
    d_ff = w_up.shape[2] * N_CHIPS
    in_cols = w_in.shape[2] * N_CHIPS
    assert in_cols == 5 * d_conf + 2 * d and d == 2 * d_conf
    cw = d_conf
    core = lax.axis_index("c")
    chip = 2 * lax.axis_index("x") + lax.axis_index("y")

    tr = _largest_tile(t, 272, ROW_CHUNK)
    tm = t
    tn_in = _largest_tile(in_cols // N_CHIPS, 768, LANES)
    tn_d = _largest_tile(d, 1024, LANES)
    tn_h = _largest_tile(d, 512, LANES)
    tn_ff = _largest_tile(d_ff // N_CHIPS, 1024, LANES)
    tn_pw = _largest_tile(d // N_CHIPS, 512, LANES)

    big = [w_in[0], conf_w_pw[0], short_w_out[0], w_o[0], w_up[0], w_down[0]]
    chip_arr = chip.astype(jnp.int32).reshape(1)

    def cast(a, deps):
        return _into_slot(f"cast_w{a}", big[a], N_CHIPS, chip_arr, MXU_DTYPE, _largest_tile(big[a].shape[0], 256, 16), deps)

    small = [_into_slot(f"place_w{a}", w, N_CHIPS, chip_arr, F32, w.shape[0])
             for a, w in enumerate([meta, conf_dw_w[0], short_dw_w[0]])]
    sems_small, fly_small, tok_small = _gather_start("gather_start_small", small, [False] * 3, [[0, 1, 2]])
    sems_in, fly_in, tok_in = _gather_start("gather_start_in", [cast(0, [tok_small])], [True], [[0]])
    rest_groups = [[0, 1], [2], [3], [4]]
    sems_rest, fly_rest, tok_rest = _gather_start(
        "gather_start_rest", [cast(a, [tok_in]) for a in range(1, 6)], [True] * 5, rest_groups)

    def arrive(g, after):
        members = rest_groups[g]
        got = _gather_wait(f"gather_wait_rest{g}", [fly_rest[a] for a in members], [True] * len(members),
                           sems_rest[g], after)
        return _pass_halves(f"gather_pass_rest{g}", got)

    meta_g, wdw_g, w3_g = _gather_wait("gather_wait_small", fly_small, [False] * 3, sems_small[0], [tok_in])
    meta_full = jnp.transpose(meta_g, (1, 0, 2)).reshape(N_META, d)
    wdw = jnp.transpose(wdw_g, (1, 0, 2)).reshape(CONF_KERNEL, d_conf)
    w3 = jnp.transpose(w3_g, (1, 0, 2)).reshape(SHORT_KERNEL, d_conf)

    tail = jnp.zeros((t - t_real, d), F32)
    h0 = jnp.concatenate([meta_full, x[0], tail], axis=0)
    target = jnp.concatenate([jnp.zeros((N_META, d), F32), loss_target[0], tail], axis=0)

    def norm_in(i, rows, vecs):
        return [_rms_fwd(rows[0], vecs[0])], []

    (n_lp,) = _rowwise("norm_in", norm_in, [(h0, d, 0)], [g_pre_mix], [(d, MXU_DTYPE)], [], tr, deps=[tok_rest])
    (wg_in,) = _pass_halves("gather_pass_in", _gather_wait("gather_wait_in", fly_in, [True], sems_in[0], [n_lp]))
    proj =_matmul("proj", n_lp, wg_in, kind="nn", tm=tm, tn=tn_in, tk=d, out_dtypes=[F32])
    ac, c3 = _conv_fwd(proj, wdw, conf_dw_b, w3, d_conf)

    def ln_parts(ac_t, ln_g, ln_b):
        mu = jnp.mean(ac_t, axis=-1, keepdims=True)
        xc = ac_t - mu
        rstd = lax.rsqrt(jnp.mean(xc * xc, axis=-1, keepdims=True) + LN_EPS)
        xh = xc * rstd
        return xh, rstd, xh * ln_g + ln_b

    def branch_act(i, rows, vecs):
        ac_t, c3_t, bg_t = rows
        _, _, al = ln_parts(ac_t, vecs[0], vecs[1])
        return [al * _sigmoid(al), bg_t * c3_t], []

    a_act, s_lp = _rowwise("branch_act", branch_act, [(ac, cw, 0), (c3, cw, 0), (proj, cw, 2)],
                           [conf_ln_g, conf_ln_b], [(d_conf, MXU_DTYPE), (d_conf, MXU_DTYPE)], [], tr)
    wg_pw, wg_sout = arrive(0, [a_act])
    y_a = _matmul("y_a", a_act, wg_pw, kind="nn", tm=tm, tn=tn_pw, tk=d_conf, out_dtypes=[F32])
    y_b = _matmul("y_b", s_lp, wg_sout, kind="nn", tm=tm, tn=tn_pw, tk=d_conf, out_dtypes=[F32])

    gate_rows = [(proj, cw, 5), (proj, cw, 6), (proj, cw, 7), (proj, cw, 8)]

    def gates_of(rows, b):
        ga = _sigmoid(jnp.concatenate([rows[0], rows[1]], axis=1) + b[:, :d])
        gb = _sigmoid(jnp.concatenate([rows[2], rows[3]], axis=1) + b[:, d:])
        return ga, gb

    def gate(i, rows, vecs):
        ga, gb = gates_of(rows[2:], vecs[0])
        return [ga * rows[0] + gb * rows[1]], []

    (m_lp,) = _rowwise("gate", gate, [(y_a, d, 0), (y_b, d, 0)] + gate_rows, [b_gates], [(d, MXU_DTYPE)], [], tr)
    wg_o = arrive(1, [m_lp])[0].reshape(d, d)
    mix = _matmul("mix", m_lp, wg_o, kind="nn", tm=tm, tn=tn_d, tk=d, out_dtypes=[F32])

    def post_mix(i, rows, vecs):
        h1_t = rows[0] + _rms_fwd(rows[1], vecs[0])
        return [h1_t, _rms_fwd(h1_t, vecs[1])], []

    h1, n2_lp = _rowwise("post_mix", post_mix, [(h0, d, 0), (mix, d, 0)], [g_post_mix, g_pre_mlp],
                         [(d, F32), (d, MXU_DTYPE)], [], tr)
    (wg_up,) = arrive(2, [n2_lp])
    up, f_lp = _matmul("up", n2_lp, wg_up, kind="nn", tm=tm, tn=tn_h, tk=d, out_dtypes=[F32, MXU_DTYPE],
                       epilogue=lambda acc: (acc, jnp.square(jnp.maximum(acc, 0.0))))
    wg_down = arrive(3, [f_lp])[0].reshape(d_ff, d)
    dn = _matmul("down", f_lp, wg_down, kind="nn", tm=tm, tn=tn_h, tk=_largest_tile(d_ff, 2048, LANES),
                 out_dtypes=[F32])

    def head(row0, rows, vecs):
        h1_t, dn_t, tgt = rows
        y = h1_t + _rms_fwd(dn_t, vecs[0])
        row = row0 + lax.broadcasted_iota(jnp.int32, (ROW_CHUNK, 1), 0)
        err = jnp.where(jnp.logical_and(row >= N_META, row < t_real), y - tgt, 0.0)
        dy = err / d
        d_dn, dg = _rms_bwd(dn_t, vecs[0], dy)
        loss_rows = 0.5 * jnp.mean(err * err, axis=-1, keepdims=True)
        return [dy, d_dn], [dg, jnp.broadcast_to(loss_rows, (ROW_CHUNK, LANES))]

    dy, d_dn, dg_post_mlp, loss_vec = _rowwise(
        "head", head, [(h1, d, 0), (dn, d, 0), (target, d, 0)], [g_post_mlp], [(d, F32), (d, MXU_DTYPE)], [d, LANES], tr)

    core_arr = core.astype(jnp.int32).reshape(1)
    place = jnp.stack([chip, core]).astype(jnp.int32)
    in_flight = {}


    def tag(members):
        return "".join(str(a) for a in members)

    def swap_start(members, gws):
        halves = [gw.reshape(N_CHIPS, 2, gw.shape[1] // 2, gw.shape[2]) for gw in gws]
        *in_flight[members], token = _swap_start(f"swap_start{tag(members)}", halves)
        return token

    def exchange_start(members, after):
        sems, halves, lands = in_flight[members]
        halves, lands = _swap_wait(f"swap_wait{tag(members)}", halves, lands, sems, after)
        pairs = [_pair_sum(f"pair_sum{a}", h, q, core_arr, _largest_tile(q.shape[1], 256, 16))
                 for a, h, q in zip(members, halves, lands)]
        *in_flight[members], token = _exchange_start(f"exchange_start{tag(members)}", pairs)
        return token

    def reduce_finish(groups, after):
        waited = _exchange_wait("exchange_wait" + "_".join(tag(g) for g in groups), [in_flight[g] for g in groups], after)
        return {a: _sum_pieces(f"sum_pieces{a}", pair, land, place, _largest_tile(land.shape[1], 256, 16))
                for g, (pairs, lands) in zip(groups, waited) for a, pair, land in zip(g, pairs, lands)}

    d_up = _matmul("d_up", d_dn, wg_down, kind="nt", tm=tm, tn=tn_h, tk=d, out_dtypes=[MXU_DTYPE], extras=[up],
                   epilogue=lambda acc, up_t: (acc * (2.0 * jnp.maximum(up_t, 0.0)),))
    tk_t = t
    gw_down = _matmul("gw_down", f_lp, d_dn, kind="tn", tm=_largest_tile(d_ff, 2048, LANES), tn=tn_d, tk=tk_t,
                      out_dtypes=[F32])
    tok = swap_start((5,), [gw_down.reshape(N_CHIPS, d_ff // N_CHIPS, d)])
    d_n2 = _matmul("d_n2", d_up, wg_up, kind="nt", tm=tm, tn=tn_h, tk=d_ff // N_CHIPS, out_dtypes=[F32], deps=[tok])
    tok = exchange_start((5,), [d_n2])
    gw_up = _matmul("gw_up", n2_lp, d_up, kind="tn", tm=_largest_tile(d, 2048, LANES), tn=tn_ff, tk=tk_t,
                    out_dtypes=[F32], out_pieces=N_CHIPS, deps=[tok])
    tok = swap_start((4,), [gw_up])

    def bwd_mid(i, rows, vecs):
        dy_t, dn2_t, h1_t, mix_t = rows
        d_h1a, dg_pre_mlp = _rms_bwd(h1_t, vecs[1], dn2_t)
        d_h1 = dy_t + d_h1a
        d_mix, dg_post_mix = _rms_bwd(mix_t, vecs[0], d_h1)
        return [d_h1, d_mix], [dg_pre_mlp, dg_post_mix]

    d_h1, d_mix, dg_pre_mlp, dg_post_mix = _rowwise(
        "bwd_mid", bwd_mid, [(dy, d, 0), (d_n2, d, 0), (h1, d, 0), (mix, d, 0)], [g_post_mix, g_pre_mlp],
        [(d, F32), (d, MXU_DTYPE)], [d, d], tr, deps=[tok])
    d_m = _matmul("d_m", d_mix, wg_o, kind="nt", tm=tm, tn=tn_h, tk=d, out_dtypes=[F32])
    tok = exchange_start((4,), [d_m])
    gw_o = _matmul("gw_o", m_lp, d_mix, kind="tn", tm=_largest_tile(d, 2048, LANES), tn=tn_h, tk=tk_t, out_dtypes=[F32],
                   deps=[tok])

    def gate_bwd(i, rows, vecs):
        dm_t, ya_t, yb_t = rows[:3]
        ga, gb = gates_of(rows[3:], vecs[0])
        d_gpre = jnp.concatenate([dm_t * ya_t * ga * (1.0 - ga), dm_t * yb_t * gb * (1.0 - gb)], axis=1)
        return [dm_t * ga, dm_t * gb, d_gpre], [d_gpre]

    d_ya, d_yb, d_gpre, dg_b_gates = _rowwise(
        "gate_bwd", gate_bwd, [(d_m, d, 0), (y_a, d, 0), (y_b, d, 0)] + gate_rows, [b_gates],
        [(d, MXU_DTYPE), (d, MXU_DTYPE), (2 * d, MXU_DTYPE)], [2 * d], tr, deps=[tok])
    d_aact = _matmul("d_aact", d_ya, wg_pw, kind="nt", tm=tm, tn=d_conf // 2, tk=tn_pw, out_dtypes=[F32], deps=[tok])
    gw_pw = _matmul("gw_pw", a_act, d_ya, kind="tn", tm=_largest_tile(d_conf, 2048, LANES), tn=tn_pw, tk=tk_t,
                    out_dtypes=[F32], out_pieces=N_CHIPS)
    d_s = _matmul("d_s", d_yb, wg_sout, kind="nt", tm=tm, tn=d_conf // 2, tk=tn_pw, out_dtypes=[F32], deps=[tok])
    gw_sout = _matmul("gw_sout", s_lp, d_yb, kind="tn", tm=_largest_tile(d_conf, 2048, LANES), tn=tn_pw, tk=tk_t,
                      out_dtypes=[F32], out_pieces=N_CHIPS, deps=[tok])
    small_big = (3, 1, 2)
    tok = swap_start(small_big, [gw_o.reshape(N_CHIPS, d // N_CHIPS, d), gw_pw, gw_sout])

    def branch_bwd(i, rows, vecs):
        daact_t, ds_t, ac_t, c3_t, bg_t = rows
        xh, rstd, al = ln_parts(ac_t, vecs[0], vecs[1])
        sg = _sigmoid(al)
        d_al = daact_t * (sg * (1.0 + al * (1.0 - sg)))
        dxh = d_al * vecs[0]
        d_ac = rstd * (dxh - jnp.mean(dxh, axis=-1, keepdims=True) - xh * jnp.mean(dxh * xh, axis=-1, keepdims=True))
        return [d_ac, ds_t * bg_t, ds_t * c3_t], [d_al * xh, d_al, d_ac]

    d_ac, d_c3, d_bg, dg_ln_g, dg_ln_b, dg_dw_b = _rowwise(
        "branch_bwd", branch_bwd, [(d_aact, cw, 0), (d_s, cw, 0), (ac, cw, 0), (c3, cw, 0), (proj, cw, 2)],
        [conf_ln_g, conf_ln_b], [(d_conf, F32), (d_conf, F32), (d_conf, MXU_DTYPE)], [d_conf] * 3, tr, deps=[tok])
    d_av, d_ag, d_cg, d_v, dg_wdw, dg_w3 = _conv_bwd(proj, d_ac, d_c3, wdw, w3, d_conf)
    d_proj = jnp.concatenate([d_av, d_ag, d_bg, d_cg, d_v, d_gpre], axis=1)
    small_w = d_conf

    def pack(arrs):
        flat = jnp.concatenate([a.reshape(-1, small_w) for a in arrs], axis=0)
        return jnp.pad(flat, ((0, -flat.shape[0] % 8), (0, 0)))

    def reduce_small(tag, arrs, deps):
        assert all(a.size % small_w == 0 for a in arrs)
        gathered = _allgather_small(f"allgather_small_{tag}", pack(arrs), deps)
        first, r0 = [], 0
        for a in arrs:
            first.append(r0)
            r0 += a.size // small_w
        return _sum_blocks(f"sum_small_{tag}", gathered, 8), first

    def rows_of(total, r0, like):
        return total[r0:r0 + like.size // small_w].reshape(like.shape)

    rep_w = [b_gates, conf_dw_b, conf_ln_g, conf_ln_b, g_post_mix, g_pre_mlp, g_post_mlp]
    rep_m = [m_b_gates, m_conf_dw_b, m_conf_ln_g, m_conf_ln_b, m_g_post_mix, m_g_pre_mlp, m_g_post_mlp]
    rep_v = [v_b_gates, v_conf_dw_b, v_conf_ln_g, v_conf_ln_b, v_g_post_mix, v_g_pre_mlp, v_g_post_mlp]
    total_mid, first_mid = reduce_small(
        "mid", [dg_b_gates, dg_dw_b, dg_ln_g, dg_ln_b, dg_post_mix, dg_pre_mlp, dg_post_mlp, dg_wdw, dg_w3], [tok])
    g_wdw_full, g_w3_full = rows_of(total_mid, first_mid[7], dg_wdw), rows_of(total_mid, first_mid[8], dg_w3)
    sc = d_conf // N_CHIPS
    g_wdw = lax.dynamic_slice_in_dim(g_wdw_full, chip * sc, sc, axis=1)
    g_w3 = lax.dynamic_slice_in_dim(g_w3_full, chip * sc, sc, axis=1)
    tok = exchange_start(small_big, [g_w3_full])

    gw_in = _matmul("gw_in", n_lp, d_proj, kind="tn", tm=_largest_tile(d, 2048, LANES), tn=tn_in, tk=tk_t,
                    out_dtypes=[F32], out_pieces=N_CHIPS, deps=[tok])
    tok = swap_start((0,), [gw_in])
    d_n = _matmul("d_n", d_proj, wg_in, kind="nt", tm=tm, tn=tn_h, tk=in_cols // N_CHIPS, out_dtypes=[F32], deps=[tok])
    tok = exchange_start((0,), [d_n])

    early = [5, 4, 3, 1, 2]
    reduced = reduce_finish([(5,), (4,), small_big], [tok])
    join_sems, joining, tok = _join_start("join_start_early", [reduced[a] for a in early])

    def bwd_in(i, rows, vecs):
        d_h0a, dg = _rms_bwd(rows[2], vecs[0], rows[1])
        return [rows[0] + d_h0a], [dg]

    d_h0, dg_pre_mix = _rowwise("bwd_in", bwd_in, [(d_h1, d, 0), (d_n, d, 0), (h0, d, 0)], [g_pre_mix],
                                [(d, F32)], [d], tr, deps=[tok])
    grad_x = d_h0[N_META:t_real][None]

    big_m = [m_w_in, m_conf_w_pw, m_short_w_out, m_w_o, m_w_up, m_w_down]
    big_v = [v_w_in, v_conf_w_pw, v_short_w_out, v_w_o, v_w_up, v_w_down]
    big_res = {}

    def adam_group(members, joined):
        for a, j in zip(members, joined):
            big_res[a] = _adamw(f"adamw_big{a}", big[a], j.reshape(big[a].shape), big_m[a][0], big_v[a][0],
                                _largest_tile(big[a].shape[0], 256, 8))

    adam_group(early, _join_wait("join_wait_early", joining, join_sems, [d_h0]))
    rep_res = _small_adamw("adamw_rep", total_mid, [
        (first_mid[e], w, m, v) for e, (w, m, v) in enumerate(zip(rep_w, rep_m, rep_v))])
    col_res = {a: _adamw(f"adamw_col{a}", w, g, m, v, w.shape[0]) for a, (w, g, m, v) in (
        (1, (conf_dw_w[0], g_wdw, m_conf_dw_w[0], v_conf_dw_w[0])),
        (2, (short_dw_w[0], g_w3, m_short_dw_w[0], v_short_dw_w[0])))}
    reduced = reduce_finish([(0,)], [rep_res[0][1], col_res[1][1], col_res[2][1]] + [big_res[a][1] for a in early])
    join_sems, joining, tok = _join_start("join_start_late", [reduced[0]])
    loss_row = jnp.tile(loss_vec, (1, small_w // LANES))
    total_late, first_late = reduce_small("late", [dg_pre_mix, d_h0[:N_META], loss_row], [tok])
    loss = total_late[first_late[2], 0]
    g_meta_full = rows_of(total_late, first_late[1], d_h0[:N_META])
    g_meta = lax.dynamic_slice_in_dim(g_meta_full, chip * (d // N_CHIPS), d // N_CHIPS, axis=1)
    col_res[0] = _adamw("adamw_col0", meta, g_meta, m_meta, v_meta, meta.shape[0])
    (pre_mix_res,) = _small_adamw("adamw_pre_mix", total_late, [(first_late[0], g_pre_mix, m_g_pre_mix, v_g_pre_mix)])
    rep_res = [pre_mix_res] + rep_res
    adam_group([0], _join_wait("join_wait_late", joining, join_sems, [col_res[0][1], pre_mix_res[1]]))

    def leaf(q):
        r = lambda a: rep_res[a][q]
        b = lambda a: big_res[a][q][None]
        return [col_res[0][q], r(0), b(0), r(1), col_res[1][q][None], r(2), r(3), r(4), b(1), col_res[2][q][None], b(2),
                b(3), r(5), r(6), b(4), b(5), r(7)]

    return (loss, grad_x, *leaf(0), *leaf(1), *leaf(2), *leaf(3))
```

```python
import jax
import jax.numpy as jnp
from jax import lax
from jax.experimental import pallas as pl
from jax.experimental.pallas import tpu as pltpu

F32 = jnp.float32
BF16 = jnp.bfloat16
MXU_DTYPE = BF16
WIRE_DTYPE = BF16

N_META = 16
CONF_KERNEL = 31
SHORT_KERNEL = 3
CONV_PAD = 32
RMS_EPS = 1e-6
LN_EPS = 1e-5
ADAM_LR = 0.001
ADAM_B1 = 0.9
ADAM_B2 = 0.999
ADAM_EPS = 1e-08
ADAM_WD = 0.01
ADAM_STEP = 10

N_CHIPS = 4
MESH = pl.DeviceIdType.MESH
LANES = 128


def _sigmoid(z):
    return 1.0 / (1.0 + jnp.exp(-z))


ANY_SPEC = pl.BlockSpec(memory_space=pl.ANY)


def _matmul(name, a, b, *, kind, tm, tn, tk, out_dtypes, out_pieces=1, epilogue=None, extras=(), deps=()):
    pieces = b.shape[0] if b.ndim == 3 else 1
    if kind == "nn":
        m, kdim = a.shape
        n = b.shape[-1] * pieces
        dims = (((1,), (0,)), ((), ()))
        a_spec = pl.BlockSpec((tm, tk), lambda i, j, k: (i, k))
        if b.ndim == 2:
            b_spec = pl.BlockSpec((tk, tn), lambda i, j, k: (k, j))
        else:
            npp = b.shape[-1] // tn
            b_spec = pl.BlockSpec((None, tk, tn), lambda i, j, k: (j // npp, k, j % npp))
    elif kind == "nt":
        m, kdim = a.shape
        n = b.shape[-2]
        dims = (((1,), (1,)), ((), ()))
        a_spec = pl.BlockSpec((tm, tk), lambda i, j, k: (i, k))
        if b.ndim == 2:
            b_spec = pl.BlockSpec((tn, tk), lambda i, j, k: (j, k))
        else:
            kpp = b.shape[-1] // tk
            b_spec = pl.BlockSpec((None, tn, tk), lambda i, j, k: (k // kpp, j, k % kpp))
    else:
        kdim, m = a.shape
        n = b.shape[-1]
        dims = (((0,), (0,)), ((), ()))
        a_spec = pl.BlockSpec((tk, tm), lambda i, j, k: (k, i))
        b_spec = pl.BlockSpec((tk, tn), lambda i, j, k: (k, j))
    assert m % tm == 0 and n % tn == 0 and kdim % tk == 0, (name, m, n, kdim, tm, tn, tk)
    nk = kdim // tk
    if out_pieces == 1:
        out_shape = (m, n)
        out_spec = pl.BlockSpec((tm, tn), lambda i, j, k: (i, j))
    else:
        onpp = n // out_pieces // tn
        out_shape = (out_pieces, m, n // out_pieces)
        out_spec = pl.BlockSpec((None, tm, tn), lambda i, j, k: (j // onpp, i, j % onpp))
    n_ex, n_out, n_in = len(extras), len(out_dtypes), len(extras) + len(deps)
    if epilogue is None:
        epilogue = lambda acc: (acc,)

    def body(a_ref, b_ref, *rest):
        ex_refs, o_refs = rest[:n_ex], rest[n_in:n_in + n_out]
        prod = lax.dot_general(a_ref[...], b_ref[...], dims, preferred_element_type=F32)

        def finish(acc):
            tiles = epilogue(acc, *[r[...] for r in ex_refs])
            for o_ref, t in zip(o_refs, tiles):
                o_ref[...] = t.astype(o_ref.dtype)

        if nk == 1:
            finish(prod)
        else:
            acc_ref = rest[n_in + n_out]
            k = pl.program_id(2)

            @pl.when(k == 0)
            def _():
                acc_ref[...] = prod

            @pl.when(jnp.logical_and(k > 0, k < nk - 1))
            def _():
                acc_ref[...] += prod

            @pl.when(k == nk - 1)
            def _():
                finish(acc_ref[...] + prod)

    ex_specs = [pl.BlockSpec((tm, tn), lambda i, j, k: (i, j)) for _ in extras]
    res = pl.pallas_call(
        body,
        name=name,
        grid=(m // tm, n // tn, nk),
        in_specs=[a_spec, b_spec, *ex_specs] + [ANY_SPEC] * len(deps),
        out_specs=[out_spec] * n_out,
        out_shape=[jax.ShapeDtypeStruct(out_shape, d) for d in out_dtypes],
        scratch_shapes=[pltpu.VMEM((tm, tn), F32)] if nk > 1 else [],
        compiler_params=pltpu.CompilerParams(dimension_semantics=("parallel", "parallel", "arbitrary")),
    )(a, b, *extras, *deps)
    return res[0] if n_out == 1 else res


ROW_CHUNK = 16
SUBLANES = 8


def _rowwise(name, fn, rows, vecs, outs, sums, tr, deps=()):
    t = rows[0][0].shape[0]
    assert t % tr == 0 and tr % ROW_CHUNK == 0
    n_r, n_v, n_o, n_s = len(rows), len(vecs), len(outs), len(sums)
    n_in = n_r + n_v + len(deps)
    n_steps = t // tr

    def body(*refs):
        r_in, v_in = refs[:n_r], refs[n_r:n_r + n_v]
        o_refs = refs[n_in:n_in + n_o]
        s_refs = refs[n_in + n_o:n_in + n_o + n_s]
        acc_refs = refs[n_in + n_o + n_s:]
        i = pl.program_id(0)

        @pl.when(i == 0)
        def _():
            for acc_ref in acc_refs:
                acc_ref[...] = jnp.zeros(acc_ref.shape, F32)

        def chunk(ci):
            r0 = ci * ROW_CHUNK
            sl = pl.ds(r0, ROW_CHUNK)
            o_tiles, s_tiles = fn(i * tr + r0, [r[sl, :] for r in r_in], [v[...] for v in v_in])
            for o_ref, tile in zip(o_refs, o_tiles):
                o_ref[sl, :] = tile.astype(o_ref.dtype)
            for acc_ref, tile in zip(acc_refs, s_tiles):
                part = tile[0:SUBLANES]
                for s in range(1, ROW_CHUNK // SUBLANES):
                    part = part + tile[s * SUBLANES:(s + 1) * SUBLANES]
                acc_ref[...] += part

        for ci in range(tr // ROW_CHUNK):
            chunk(ci)

        @pl.when(i == n_steps - 1)
        def _():
            for s_ref, acc_ref in zip(s_refs, acc_refs):
                s_ref[...] = jnp.sum(acc_ref[...], axis=0, keepdims=True)

    def row_spec(width, blk):
        return pl.BlockSpec((tr, width), lambda i: (i, blk))

    res = pl.pallas_call(
        body,
        name=name,
        grid=(t // tr,),
        in_specs=[row_spec(w, blk) for _, w, blk in rows]
        + [pl.BlockSpec(v.shape, lambda i: (0, 0)) for v in vecs] + [ANY_SPEC] * len(deps),
        out_specs=[pl.BlockSpec((tr, c), lambda i: (i, 0)) for c, _ in outs]
        + [pl.BlockSpec((1, c), lambda i: (0, 0)) for c in sums],
        out_shape=[jax.ShapeDtypeStruct((t, c), d) for c, d in outs]
        + [jax.ShapeDtypeStruct((1, c), F32) for c in sums],
        scratch_shapes=[pltpu.VMEM((SUBLANES, c), F32) for c in sums],
        compiler_params=pltpu.CompilerParams(dimension_semantics=("arbitrary",)),
    )(*[r[0] for r in rows], *vecs, *deps)
    return res


def _rms_fwd(x, g):
    r = lax.rsqrt(jnp.mean(x * x, axis=-1, keepdims=True) + RMS_EPS)
    return x * r * g


def _rms_bwd(x, g, dy):
    r = lax.rsqrt(jnp.mean(x * x, axis=-1, keepdims=True) + RMS_EPS)
    xn = x * r
    dxn = dy * g
    dx = r * (dxn - xn * jnp.mean(dxn * xn, axis=-1, keepdims=True))
    return dx, dy * xn


CONV_ROWS = 64
CONV_LANES = 128


def _conv_fwd(proj, wdw, bdw, w3, d_conf):
    t = proj.shape[0]
    cl = CONV_LANES
    nb = d_conf // cl
    nchunk = t // CONV_ROWS
    assert t % CONV_ROWS == 0

    def body(av_ref, ag_ref, cg_ref, v_ref, wdw_ref, bdw_ref, w3_ref, ac_ref, c3_ref, apad, cpad):
        zeros = jnp.zeros((CONV_PAD, cl), F32)
        apad[0:CONV_PAD, :] = zeros
        cpad[0:CONV_PAD, :] = zeros
        apad[CONV_PAD:, :] = av_ref[...] * _sigmoid(ag_ref[...])
        cpad[CONV_PAD:, :] = cg_ref[...] * v_ref[...]

        def chunk(ci, carry):
            base = pl.multiple_of(ci * CONV_ROWS, 8)
            acc = jnp.zeros((CONV_ROWS, cl), F32) + bdw_ref[...]
            for k in range(CONF_KERNEL):
                off = CONV_PAD - (CONF_KERNEL - 1) + k
                acc = acc + apad[pl.ds(base + off, CONV_ROWS), :] * wdw_ref[k:k + 1, :]
            ac_ref[pl.ds(base, CONV_ROWS), :] = acc
            acc3 = jnp.zeros((CONV_ROWS, cl), F32)
            for k in range(SHORT_KERNEL):
                off = CONV_PAD - (SHORT_KERNEL - 1) + k
                acc3 = acc3 + cpad[pl.ds(base + off, CONV_ROWS), :] * w3_ref[k:k + 1, :]
            c3_ref[pl.ds(base, CONV_ROWS), :] = acc3
            return carry

        lax.fori_loop(0, nchunk, chunk, 0)

    def col(blk0):
        return pl.BlockSpec((t, cl), lambda j: (0, blk0 + j))

    return pl.pallas_call(
        body,
        name="conv_fwd",
        grid=(nb,),
        in_specs=[col(0), col(nb), col(3 * nb), col(4 * nb),
                  pl.BlockSpec((CONF_KERNEL, cl), lambda j: (0, j)),
                  pl.BlockSpec((1, cl), lambda j: (0, j)),
                  pl.BlockSpec((SHORT_KERNEL, cl), lambda j: (0, j))],
        out_specs=[pl.BlockSpec((t, cl), lambda j: (0, j))] * 2,
        out_shape=[jax.ShapeDtypeStruct((t, d_conf), F32)] * 2,
        scratch_shapes=[pltpu.VMEM((t + CONV_PAD, cl), F32)] * 2,
        compiler_params=pltpu.CompilerParams(dimension_semantics=("parallel",)),
    )(proj, proj, proj, proj, wdw, bdw, w3)


def _conv_bwd(proj, d_ac, d_c3, wdw, w3, d_conf):
    t = proj.shape[0]
    cl = CONV_LANES
    nb = d_conf // cl
    nchunk = t // CONV_ROWS
    nsub = CONV_ROWS // 8

    def fold(p):
        r = p[0:8]
        for s in range(1, nsub):
            r = r + p[8 * s:8 * s + 8]
        return r

    def body(av_ref, ag_ref, cg_ref, v_ref, dac_ref, dc3_ref, wdw_ref, w3_ref,
             dav_ref, dag_ref, dcg_ref, dv_ref, dwdw_ref, dw3_ref, apad, cpad, dapad, dcpad):
        zeros = jnp.zeros((CONV_PAD, cl), F32)
        apad[0:CONV_PAD, :] = zeros
        cpad[0:CONV_PAD, :] = zeros
        apad[CONV_PAD:, :] = av_ref[...] * _sigmoid(ag_ref[...])
        cpad[CONV_PAD:, :] = cg_ref[...] * v_ref[...]
        dapad[0:t, :] = dac_ref[...]
        dcpad[0:t, :] = dc3_ref[...]
        dapad[t:, :] = zeros
        dcpad[t:, :] = zeros

        def chunk(ci, accs):
            base = pl.multiple_of(ci * CONV_ROWS, 8)
            rows = pl.ds(base, CONV_ROWS)
            da = jnp.zeros((CONV_ROWS, cl), F32)
            for k in range(CONF_KERNEL):
                da = da + dapad[pl.ds(base + (CONF_KERNEL - 1 - k), CONV_ROWS), :] * wdw_ref[k:k + 1, :]
            dcv = jnp.zeros((CONV_ROWS, cl), F32)
            for k in range(SHORT_KERNEL):
                dcv = dcv + dcpad[pl.ds(base + (SHORT_KERNEL - 1 - k), CONV_ROWS), :] * w3_ref[k:k + 1, :]
            av, sg = av_ref[rows, :], _sigmoid(ag_ref[rows, :])
            dav_ref[rows, :] = (da * sg).astype(dav_ref.dtype)
            dag_ref[rows, :] = (da * av * sg * (1.0 - sg)).astype(dag_ref.dtype)
            dcg_ref[rows, :] = (dcv * v_ref[rows, :]).astype(dcg_ref.dtype)
            dv_ref[rows, :] = (dcv * cg_ref[rows, :]).astype(dv_ref.dtype)
            d_out, d_out3 = dac_ref[rows, :], dc3_ref[rows, :]
            new = []
            for k in range(CONF_KERNEL):
                off = CONV_PAD - (CONF_KERNEL - 1) + k
                new.append(accs[k] + fold(d_out * apad[pl.ds(base + off, CONV_ROWS), :]))
            for k in range(SHORT_KERNEL):
                off = CONV_PAD - (SHORT_KERNEL - 1) + k
                new.append(accs[CONF_KERNEL + k] + fold(d_out3 * cpad[pl.ds(base + off, CONV_ROWS), :]))
            return tuple(new)

        init = tuple(jnp.zeros((8, cl), F32) for _ in range(CONF_KERNEL + SHORT_KERNEL))
        accs = lax.fori_loop(0, nchunk, chunk, init)
        for k in range(CONF_KERNEL):
            dwdw_ref[k:k + 1, :] = jnp.sum(accs[k], axis=0, keepdims=True)
        for k in range(SHORT_KERNEL):
            dw3_ref[k:k + 1, :] = jnp.sum(accs[CONF_KERNEL + k], axis=0, keepdims=True)

    def col(blk0):
        return pl.BlockSpec((t, cl), lambda j: (0, blk0 + j))

    own = pl.BlockSpec((t, cl), lambda j: (0, j))
    return pl.pallas_call(
        body,
        name="conv_bwd",
        grid=(nb,),
        in_specs=[col(0), col(nb), col(3 * nb), col(4 * nb), own, own,
                  pl.BlockSpec((CONF_KERNEL, cl), lambda j: (0, j)),
                  pl.BlockSpec((SHORT_KERNEL, cl), lambda j: (0, j))],
        out_specs=[own] * 4 + [pl.BlockSpec((CONF_KERNEL, cl), lambda j: (0, j)),
                               pl.BlockSpec((SHORT_KERNEL, cl), lambda j: (0, j))],
        out_shape=[jax.ShapeDtypeStruct((t, d_conf), MXU_DTYPE)] * 4
        + [jax.ShapeDtypeStruct((CONF_KERNEL, d_conf), F32), jax.ShapeDtypeStruct((SHORT_KERNEL, d_conf), F32)],
        scratch_shapes=[pltpu.VMEM((t + CONV_PAD, cl), F32)] * 4,
        compiler_params=pltpu.CompilerParams(dimension_semantics=("parallel",)),
    )(proj, proj, proj, proj, d_ac, d_c3, wdw, w3)


def _elementwise(name, fn, ins, out_dtypes, tr, deps=()):
    ins = [(a, ()) if not isinstance(a, tuple) else a for a in ins]
    r, c = ins[0][0].shape[-2:]
    assert r % tr == 0, (name, r, tr)
    n_in = len(ins)

    def body(*refs):
        tiles = fn(*[x[...] for x in refs[:n_in]])
        for o_ref, tile in zip(refs[n_in + len(deps):], tiles):
            o_ref[...] = tile.astype(o_ref.dtype)

    def spec(lead):
        return pl.BlockSpec((None,) * len(lead) + (tr, c), lambda i: (*lead, i, 0))

    res = pl.pallas_call(
        body,
        name=name,
        grid=(r // tr,),
        in_specs=[spec(lead) for _, lead in ins] + [ANY_SPEC] * len(deps),
        out_specs=[pl.BlockSpec((tr, c), lambda i: (i, 0))] * len(out_dtypes),
        out_shape=[jax.ShapeDtypeStruct((r, c), d) for d in out_dtypes],
        compiler_params=pltpu.CompilerParams(dimension_semantics=("parallel",)),
    )(*[a for a, _ in ins], *deps)
    return res


def _adamw_tiles(w, g, m, v):
    m = ADAM_B1 * m + (1.0 - ADAM_B1) * g
    v = ADAM_B2 * v + (1.0 - ADAM_B2) * jnp.square(g)
    m_hat = m / (1.0 - ADAM_B1 ** ADAM_STEP)
    v_hat = v / (1.0 - ADAM_B2 ** ADAM_STEP)
    delta = -ADAM_LR * (m_hat / (jnp.sqrt(v_hat) + ADAM_EPS) + ADAM_WD * w)
    return g, delta, m, v


def _adamw(name, w, g, m, v, tr, deps=()):
    shape = w.shape
    flat = [a.reshape(shape[-2:]) if a.ndim > 2 else a for a in (w, g, m, v)]
    res = _elementwise(name, _adamw_tiles, flat, [F32] * 4, tr, deps)
    return [a.reshape(shape) for a in res]


def _small_adamw(name, total, entries):
    sw = total.shape[1]
    n = len(entries)

    def body(total_ref, *refs):
        ins, outs = refs[:3 * n], refs[3 * n:]
        for e, (row0, w, _, _) in enumerate(entries):
            for q in range(w.shape[1] // sw):
                cols = slice(q * sw, (q + 1) * sw)
                tiles = _adamw_tiles(ins[3 * e][:, cols], total_ref[row0 + q:row0 + q + 1, :],
                                     ins[3 * e + 1][:, cols], ins[3 * e + 2][:, cols])
                for o_ref, tile in zip(outs[4 * e:4 * e + 4], tiles):
                    o_ref[:, cols] = tile

    res = pl.pallas_call(
        body,
        name=name,
        out_shape=[jax.ShapeDtypeStruct(w.shape, F32) for _, w, _, _ in entries for _ in range(4)],
    )(total, *[a for _, w, m, v in entries for a in (w, m, v)])
    return [list(res[4 * e:4 * e + 4]) for e in range(n)]


def _pair_sum(name, p, q, core, tr):
    n_p, _, hr, c = p.shape
    assert hr % tr == 0

    def body(core_ref, p_ref, q_ref, o_ref):
        o_ref[...] = (p_ref[...] + q_ref[...]).astype(o_ref.dtype)

    return pl.pallas_call(
        body,
        name=name,
        grid_spec=pltpu.PrefetchScalarGridSpec(
            num_scalar_prefetch=1,
            grid=(n_p, hr // tr),
            in_specs=[pl.BlockSpec((None, None, tr, c), lambda a, i, core_ref: (a, core_ref[0], i, 0)),
                      pl.BlockSpec((None, tr, c), lambda a, i, core_ref: (a, i, 0))],
            out_specs=pl.BlockSpec((None, tr, c), lambda a, i, core_ref: (a, i, 0)),
        ),
        out_shape=jax.ShapeDtypeStruct((n_p, hr, c), WIRE_DTYPE),
        compiler_params=pltpu.CompilerParams(dimension_semantics=("parallel", "parallel")),
    )(core, p, q)


def _into_slot(name, w, slots, slot, dtype, tr, deps=()):
    r, c = w.shape
    assert r % tr == 0

    def body(slot_ref, w_ref, *rest):
        o_ref = rest[len(deps)]
        o_ref[...] = w_ref[...].astype(o_ref.dtype)

    return pl.pallas_call(
        body,
        name=name,
        grid_spec=pltpu.PrefetchScalarGridSpec(
            num_scalar_prefetch=1,
            grid=(r // tr,),
            in_specs=[pl.BlockSpec((tr, c), lambda i, slot_ref: (i, 0))] + [ANY_SPEC] * len(deps),
            out_specs=pl.BlockSpec((None, tr, c), lambda i, slot_ref: (slot_ref[0], i, 0)),
        ),
        out_shape=jax.ShapeDtypeStruct((slots, r, c), dtype),
        compiler_params=pltpu.CompilerParams(dimension_semantics=("parallel",)),
    )(slot, w, *deps)


def _sum_pieces(name, own, rb, place, tr):
    n_p, hr, c = rb.shape
    assert hr % tr == 0

    def body(place_ref, own_ref, *refs):
        chip = place_ref[0]
        acc = None
        for k in range(n_p):
            tile = jnp.where(chip == k, own_ref[...], refs[k][...]).astype(F32)
            acc = tile if acc is None else acc + tile
        refs[n_p][...] = acc

    def landed(k):
        return pl.BlockSpec((None, tr, c), lambda i, place_ref: (jnp.where(place_ref[0] == k, (k + 1) % n_p, k), i, 0))

    return pl.pallas_call(
        body,
        name=name,
        grid_spec=pltpu.PrefetchScalarGridSpec(
            num_scalar_prefetch=1,
            grid=(hr // tr,),
            in_specs=[pl.BlockSpec((None, tr, c), lambda i, place_ref: (place_ref[0], i, 0))]
            + [landed(k) for k in range(n_p)],
            out_specs=pl.BlockSpec((None, tr, c), lambda i, place_ref: (place_ref[1], i, 0)),
        ),
        out_shape=jax.ShapeDtypeStruct((2, hr, c), F32),
        compiler_params=pltpu.CompilerParams(dimension_semantics=("parallel",)),
    )(place, own, *([rb] * n_p))


HBM_SPEC = pl.BlockSpec(memory_space=pl.ANY)


def _place():
    x, y, c = lax.axis_index("x"), lax.axis_index("y"), lax.axis_index("c")
    chips = [(1 - x, y), (x, 1 - y), (1 - x, 1 - y)]
    return x, y, c, chips


HBM_ONLY = pl.BlockSpec(memory_space=pltpu.HBM)
SEM_SPEC = pl.BlockSpec(memory_space=pltpu.SEMAPHORE)
DATAFLOW = pltpu.SideEffectType.DATAFLOW_SIDE_EFFECTING


def _in_hbm(a):
    return pltpu.with_memory_space_constraint(a, pltpu.HBM)


def _shard_part(ref, is_split, slot, h):
    if not is_split:
        return ref.at[slot]
    hr = ref.shape[1] // 2
    return ref.at[slot, pl.ds(h * hr, hr), :]


TOKEN = jax.ShapeDtypeStruct((8, LANES), F32)
VMEM_SPEC = pl.BlockSpec(memory_space=pltpu.VMEM)


def _gather_start(name, bufs, split, groups, deps=()):
    n, ng = len(bufs), len(groups)

    def body(*refs):
        ins, sems = refs[:n], refs[n + len(deps):n + len(deps) + 2 * ng]
        refs[-1][...] = jnp.zeros(TOKEN.shape, TOKEN.dtype)
        x, y, c, chips = _place()
        me = 2 * x + y
        for g, members in enumerate(groups):
            for s, a in enumerate(members):
                mine = _shard_part(ins[a], split[a], me, c)
                for j, chip in enumerate(chips):
                    pltpu.make_async_remote_copy(
                        src_ref=mine, dst_ref=mine, send_sem=sems[2 * g].at[3 * s + j], recv_sem=sems[2 * g + 1].at[3 * s + j],
                        device_id=(*chip, c), device_id_type=MESH).start()

    res = pl.pallas_call(
        body,
        name=name,
        in_specs=[HBM_ONLY] * n + [ANY_SPEC] * len(deps),
        out_specs=[SEM_SPEC] * (2 * ng) + [HBM_ONLY] * n + [VMEM_SPEC],
        out_shape=[pltpu.SemaphoreType.DMA((3 * len(members),)) for members in groups for _ in range(2)]
        + [pltpu.HBM(b.shape, b.dtype) for b in bufs] + [TOKEN],
        input_output_aliases={a: 2 * ng + a for a in range(n)},
        compiler_params=pltpu.CompilerParams(has_side_effects=DATAFLOW),
    )(*[_in_hbm(b) for b in bufs], *deps)
    return [(res[2 * g], res[2 * g + 1]) for g in range(ng)], list(res[2 * ng:2 * ng + n]), res[-1]


def _gather_wait(name, bufs, split, sems, after):
    n = len(bufs)

    def body(*refs):
        ins, send_sems, recv_sems = refs[:n], refs[n], refs[n + 1]
        x, y, c, chips = _place()
        me = 2 * x + y
        for s in range(n):
            for j, chip in enumerate(chips):
                copy = pltpu.make_async_remote_copy(
                    src_ref=_shard_part(ins[s], split[s], me, c),
                    dst_ref=_shard_part(ins[s], split[s], 2 * chip[0] + chip[1], c),
                    send_sem=send_sems.at[3 * s + j], recv_sem=recv_sems.at[3 * s + j],
                    device_id=(*chip, c), device_id_type=MESH)
                copy.wait_send()
                copy.wait_recv()

    res = pl.pallas_call(
        body,
        name=name,
        in_specs=[HBM_ONLY] * n + [SEM_SPEC, SEM_SPEC] + [ANY_SPEC] * len(after),
        out_specs=[HBM_ONLY] * n,
        out_shape=[pltpu.HBM(b.shape, b.dtype) for b in bufs],
        input_output_aliases={a: a for a in range(n)},
        compiler_params=pltpu.CompilerParams(has_side_effects=DATAFLOW),
    )(*bufs, *sems, *after)
    return list(res)


def _pass_halves(name, bufs):
    n = len(bufs)

    def body(*refs):
        outs = refs[n:2 * n]
        send_sems, recv_sems = refs[2 * n:]
        x, y, c, chips = _place()
        sibling = (x, y, 1 - c)

        def copy(a, j, h):
            blk = _shard_part(outs[a], True, 2 * chips[j][0] + chips[j][1], h)
            return pltpu.make_async_remote_copy(
                src_ref=blk, dst_ref=blk, send_sem=send_sems.at[3 * a + j], recv_sem=recv_sems.at[3 * a + j],
                device_id=sibling, device_id_type=MESH)

        sends = [copy(a, j, c) for a in range(n) for j in range(3)]
        for cp in sends:
            cp.start()
        for a in range(n):
            for j in range(3):
                copy(a, j, 1 - c).wait_recv()
        for cp in sends:
            cp.wait_send()

    res = pl.pallas_call(
        body,
        name=name,
        in_specs=[HBM_SPEC] * n,
        out_specs=[HBM_SPEC] * n,
        out_shape=[jax.ShapeDtypeStruct(b.shape, b.dtype) for b in bufs],
        input_output_aliases={a: a for a in range(n)},
        scratch_shapes=[pltpu.SemaphoreType.DMA((3 * n,)), pltpu.SemaphoreType.DMA((3 * n,))],
    )(*bufs)
    return list(res)


def _exchange_start(name, pairs):
    n = len(pairs)

    def body(*refs):
        pair_refs, land_refs, send_sems, recv_sems = refs[:n], refs[n:2 * n], refs[2 * n], refs[2 * n + 1]
        x, y, c, chips = _place()
        me = 2 * x + y
        for a in range(n):
            for j, chip in enumerate(chips):
                pltpu.make_async_remote_copy(
                    src_ref=pair_refs[a].at[2 * chip[0] + chip[1]], dst_ref=land_refs[a].at[me],
                    send_sem=send_sems.at[3 * a + j], recv_sem=recv_sems.at[3 * a + j],
                    device_id=(*chip, c), device_id_type=MESH).start()
        refs[-1][...] = jnp.zeros(TOKEN.shape, TOKEN.dtype)

    res = pl.pallas_call(
        body,
        name=name,
        in_specs=[HBM_ONLY] * (2 * n),
        out_specs=[SEM_SPEC, SEM_SPEC] + [HBM_ONLY] * (2 * n) + [VMEM_SPEC],
        out_shape=[pltpu.SemaphoreType.DMA((3 * n,)), pltpu.SemaphoreType.DMA((3 * n,))]
        + [pltpu.HBM(p.shape, p.dtype) for p in pairs] * 2 + [TOKEN],
        input_output_aliases={a: 2 + a for a in range(2 * n)},
        compiler_params=pltpu.CompilerParams(has_side_effects=DATAFLOW),
    )(*[_in_hbm(p) for p in pairs], *[_in_hbm(lax.empty(p.shape, p.dtype)) for p in pairs])
    return (res[0], res[1]), list(res[2:2 + n]), list(res[2 + n:2 + 2 * n]), res[-1]


def _exchange_wait(name, groups, after):
    sizes = [len(pairs) for _, pairs, _ in groups]
    n_buf = 2 * sum(sizes)

    def body(*refs):
        x, y, c, chips = _place()
        at_buf, at_sem = 0, n_buf
        for n in sizes:
            pair_refs, land_refs = refs[at_buf:at_buf + n], refs[at_buf + n:at_buf + 2 * n]
            send_sems, recv_sems = refs[at_sem], refs[at_sem + 1]
            at_buf, at_sem = at_buf + 2 * n, at_sem + 2
            for a in range(n):
                for j, chip in enumerate(chips):
                    k = 2 * chip[0] + chip[1]
                    copy = pltpu.make_async_remote_copy(
                        src_ref=pair_refs[a].at[k], dst_ref=land_refs[a].at[k],
                        send_sem=send_sems.at[3 * a + j], recv_sem=recv_sems.at[3 * a + j],
                        device_id=(*chip, c), device_id_type=MESH)
                    copy.wait_send()
                    copy.wait_recv()

    bufs = [b for _, pairs, lands in groups for b in (*pairs, *lands)]
    sems = [s for group_sems, _, _ in groups for s in group_sems]
    res = pl.pallas_call(
        body,
        name=name,
        in_specs=[HBM_ONLY] * n_buf + [SEM_SPEC] * len(sems) + [ANY_SPEC] * len(after),
        out_specs=[HBM_ONLY] * n_buf,
        out_shape=[pltpu.HBM(b.shape, b.dtype) for b in bufs],
        input_output_aliases={a: a for a in range(n_buf)},
        compiler_params=pltpu.CompilerParams(has_side_effects=DATAFLOW),
    )(*bufs, *sems, *after)
    out, at = [], 0
    for n in sizes:
        out.append((list(res[at:at + n]), list(res[at + n:at + 2 * n])))
        at += 2 * n
    return out


def _swap_start(name, halves):
    n = len(halves)
    n_p = halves[0].shape[0]
    land_shapes = [(n_p, *h.shape[2:]) for h in halves]

    def body(*refs):
        half_refs, land_refs, send_sems, recv_sems = refs[:n], refs[n:2 * n], refs[2 * n], refs[2 * n + 1]
        x, y, c, _ = _place()
        for a in range(n):
            for p in range(n_p):
                pltpu.make_async_remote_copy(
                    src_ref=half_refs[a].at[p, 1 - c], dst_ref=land_refs[a].at[p],
                    send_sem=send_sems.at[n_p * a + p], recv_sem=recv_sems.at[n_p * a + p],
                    device_id=(x, y, 1 - c), device_id_type=MESH).start()
        refs[-1][...] = jnp.zeros(TOKEN.shape, TOKEN.dtype)

    res = pl.pallas_call(
        body,
        name=name,
        in_specs=[HBM_ONLY] * (2 * n),
        out_specs=[SEM_SPEC, SEM_SPEC] + [HBM_ONLY] * (2 * n) + [VMEM_SPEC],
        out_shape=[pltpu.SemaphoreType.DMA((n_p * n,)), pltpu.SemaphoreType.DMA((n_p * n,))]
        + [pltpu.HBM(h.shape, h.dtype) for h in halves]
        + [pltpu.HBM(s, h.dtype) for s, h in zip(land_shapes, halves)] + [TOKEN],
        input_output_aliases={a: 2 + a for a in range(2 * n)},
        compiler_params=pltpu.CompilerParams(has_side_effects=DATAFLOW),
    )(*[_in_hbm(h) for h in halves], *[_in_hbm(lax.empty(s, h.dtype)) for s, h in zip(land_shapes, halves)])
    return (res[0], res[1]), list(res[2:2 + n]), list(res[2 + n:2 + 2 * n]), res[-1]


def _swap_wait(name, halves, lands, sems, after):
    n = len(halves)
    n_p = halves[0].shape[0]

    def body(*refs):
        half_refs, land_refs, send_sems, recv_sems = refs[:n], refs[n:2 * n], refs[2 * n], refs[2 * n + 1]
        x, y, c, _ = _place()
        for a in range(n):
            for p in range(n_p):
                copy = pltpu.make_async_remote_copy(
                    src_ref=half_refs[a].at[p, 1 - c], dst_ref=land_refs[a].at[p],
                    send_sem=send_sems.at[n_p * a + p], recv_sem=recv_sems.at[n_p * a + p],
                    device_id=(x, y, 1 - c), device_id_type=MESH)
                copy.wait_send()
                copy.wait_recv()

    res = pl.pallas_call(
        body,
        name=name,
        in_specs=[HBM_ONLY] * (2 * n) + [SEM_SPEC, SEM_SPEC] + [ANY_SPEC] * len(after),
        out_specs=[HBM_ONLY] * (2 * n),
        out_shape=[pltpu.HBM(b.shape, b.dtype) for b in (*halves, *lands)],
        input_output_aliases={a: a for a in range(2 * n)},
        compiler_params=pltpu.CompilerParams(has_side_effects=DATAFLOW),
    )(*halves, *lands, *sems, *after)
    return list(res[:n]), list(res[n:])


def _join_start(name, bufs):
    n = len(bufs)

    def body(*refs):
        ins, send_sems, recv_sems = refs[:n], refs[n], refs[n + 1]
        x, y, c, _ = _place()
        for a in range(n):
            pltpu.make_async_remote_copy(
                src_ref=ins[a].at[c], dst_ref=ins[a].at[c], send_sem=send_sems.at[a], recv_sem=recv_sems.at[a],
                device_id=(x, y, 1 - c), device_id_type=MESH).start()
        refs[-1][...] = jnp.zeros(TOKEN.shape, TOKEN.dtype)

    res = pl.pallas_call(
        body,
        name=name,
        in_specs=[HBM_ONLY] * n,
        out_specs=[SEM_SPEC, SEM_SPEC] + [HBM_ONLY] * n + [VMEM_SPEC],
        out_shape=[pltpu.SemaphoreType.DMA((n,)), pltpu.SemaphoreType.DMA((n,))]
        + [pltpu.HBM(b.shape, b.dtype) for b in bufs] + [TOKEN],
        input_output_aliases={a: 2 + a for a in range(n)},
        compiler_params=pltpu.CompilerParams(has_side_effects=DATAFLOW),
    )(*[_in_hbm(b) for b in bufs])
    return (res[0], res[1]), list(res[2:2 + n]), res[-1]


def _join_wait(name, bufs, sems, after):
    n = len(bufs)

    def body(*refs):
        ins, send_sems, recv_sems = refs[:n], refs[n], refs[n + 1]
        x, y, c, _ = _place()
        for a in range(n):
            copy = pltpu.make_async_remote_copy(
                src_ref=ins[a].at[c], dst_ref=ins[a].at[1 - c], send_sem=send_sems.at[a], recv_sem=recv_sems.at[a],
                device_id=(x, y, 1 - c), device_id_type=MESH)
            copy.wait_send()
            copy.wait_recv()

    res = pl.pallas_call(
        body,
        name=name,
        in_specs=[HBM_ONLY] * n + [SEM_SPEC, SEM_SPEC] + [ANY_SPEC] * len(after),
        out_specs=[HBM_ONLY] * n,
        out_shape=[pltpu.HBM(b.shape, b.dtype) for b in bufs],
        input_output_aliases={a: a for a in range(n)},
        compiler_params=pltpu.CompilerParams(has_side_effects=DATAFLOW),
    )(*bufs, *sems, *after)
    return list(res)


def _allgather_small(name, block, deps=()):
    m_per, n = block.shape

    def body(x_ref, *rest):
        out_ref, send_sems, recv_sems, local_sem = rest[len(deps):]
        x, y, c, chips = _place()
        me, sibling = (x, y, c), (x, y, 1 - c)

        def rows(px, py, pc):
            return out_ref.at[pl.ds((4 * px + 2 * py + pc) * m_per, m_per), :]

        def copy(k, blk, to, src=None):
            return pltpu.make_async_remote_copy(
                src_ref=rows(*blk) if src is None else src, dst_ref=rows(*blk),
                send_sem=send_sems.at[k], recv_sem=recv_sems.at[k], device_id=to, device_id_type=MESH)

        mine = pltpu.make_async_copy(x_ref, rows(*me), local_sem)
        mine.start()
        first = [copy(0, me, sibling, src=x_ref)]
        first += [copy(1 + j, me, (*chip, c), src=x_ref) for j, chip in enumerate(chips)]
        for cp in first:
            cp.start()
        passed = [copy(4 + j, (*chip, c), sibling) for j, chip in enumerate(chips)]
        for j, chip in enumerate(chips):
            copy(1 + j, (*chip, c), me).wait_recv()
            passed[j].start()
        copy(0, sibling, me).wait_recv()
        for j, chip in enumerate(chips):
            copy(4 + j, (*chip, 1 - c), me).wait_recv()
        for cp in first + passed:
            cp.wait_send()
        mine.wait()

    return pl.pallas_call(
        body,
        name=name,
        out_shape=jax.ShapeDtypeStruct((8 * m_per, n), block.dtype),
        in_specs=[pl.BlockSpec(memory_space=pltpu.VMEM)] + [ANY_SPEC] * len(deps),
        out_specs=pl.BlockSpec(memory_space=pltpu.VMEM),
        scratch_shapes=[pltpu.SemaphoreType.DMA((7,)), pltpu.SemaphoreType.DMA((7,)), pltpu.SemaphoreType.DMA],
    )(block, *deps)


def _sum_blocks(name, gathered, n_blocks):
    r = gathered.shape[0] // n_blocks
    c = gathered.shape[1]

    def body(g_ref, o_ref):
        acc = g_ref[0:r, :]
        for b in range(1, n_blocks):
            acc = acc + g_ref[b * r:(b + 1) * r, :]
        o_ref[...] = acc

    return pl.pallas_call(body, name=name, out_shape=jax.ShapeDtypeStruct((r, c), F32))(gathered)


def _largest_tile(n, cap, mult):
    best = None
    for d in range(mult, min(n, cap) + 1, mult):
        if n % d == 0:
            best = d
    assert best is not None, (n, cap, mult)
    return best


def kernel(x, meta, g_pre_mix, w_in, b_gates, conf_dw_w, conf_dw_b, conf_ln_g, conf_ln_b, conf_w_pw, short_dw_w, short_w_out, w_o, g_post_mix, g_pre_mlp, w_up, w_down, g_post_mlp, loss_target, m_meta, m_g_pre_mix, m_w_in, m_b_gates, m_conf_dw_w, m_conf_dw_b, m_conf_ln_g, m_conf_ln_b, m_conf_w_pw, m_short_dw_w, m_short_w_out, m_w_o, m_g_post_mix, m_g_pre_mlp, m_w_up, m_w_down, m_g_post_mlp, v_meta, v_g_pre_mix, v_w_in, v_b_gates, v_conf_dw_w, v_conf_dw_b, v_conf_ln_g, v_conf_ln_b, v_conf_w_pw, v_short_dw_w, v_short_w_out, v_w_o, v_g_post_mix, v_g_pre_mlp, v_w_up, v_w_down, v_g_post_mlp):
    seq, d = x.shape[1], x.shape[2]
    t_real = seq + N_META
    t = -(-t_real // LANES) * LANES
    d_conf = conf_dw_b.shape[1]
    d_ff = w_up.shape[2] * N_CHIPS
    in_cols = w_in.shape[2] * N_CHIPS
    assert in_cols == 5 * d_conf + 2 * d and d == 2 * d_conf
    cw = d_conf
    core = lax.axis_index("c")
    chip = 2 * lax.axis_index("x") + lax.axis_index("y")

    tr = _largest_tile(t, 272, ROW_CHUNK)
    tm = t
    tn_in = _largest_tile(in_cols // N_CHIPS, 768, LANES)
    tn_d = _largest_tile(d, 1024, LANES)
    tn_h = _largest_tile(d, 512, LANES)
    tn_ff = _largest_tile(d_ff // N_CHIPS, 1024, LANES)
    tn_pw = _largest_tile(d // N_CHIPS, 512, LANES)

    big = [w_in[0], conf_w_pw[0], short_w_out[0], w_o[0], w_up[0], w_down[0]]
    chip_arr = chip.astype(jnp.int32).reshape(1)

    def cast(a, deps):
        return _into_slot(f"cast_w{a}", big[a], N_CHIPS, chip_arr, MXU_DTYPE, _largest_tile(big[a].shape[0], 256, 16), deps)

    small = [_into_slot(f"place_w{a}", w, N_CHIPS, chip_arr, F32, w.shape[0])
             for a, w in enumerate([meta, conf_dw_w[0], short_dw_w[0]])]
    sems_small, fly_small, tok_small = _gather_start("gather_start_small", small, [False] * 3, [[0, 1, 2]])
    sems_in, fly_in, tok_in = _gather_start("gather_start_in", [cast(0, [tok_small])], [True], [[0]])
    rest_groups = [[0, 1], [2], [3], [4]]
    sems_rest, fly_rest, tok_rest = _gather_start(
        "gather_start_rest", [cast(a, [tok_in]) for a in range(1, 6)], [True] * 5, rest_groups)

    def arrive(g, after):
        members = rest_groups[g]
        got = _gather_wait(f"gather_wait_rest{g}", [fly_rest[a] for a in members], [True] * len(members),
                           sems_rest[g], after)
        return _pass_halves(f"gather_pass_rest{g}", got)

    meta_g, wdw_g, w3_g = _gather_wait("gather_wait_small", fly_small, [False] * 3, sems_small[0], [tok_in])
    meta_full = jnp.transpose(meta_g, (1, 0, 2)).reshape(N_META, d)
    wdw = jnp.transpose(wdw_g, (1, 0, 2)).reshape(CONF_KERNEL, d_conf)
    w3 = jnp.transpose(w3_g, (1, 0, 2)).reshape(SHORT_KERNEL, d_conf)

    tail = jnp.zeros((t - t_real, d), F32)
    h0 = jnp.concatenate([meta_full, x[0], tail], axis=0)
    target = jnp.concatenate([jnp.zeros((N_META, d), F32), loss_target[0], tail], axis=0)

    def norm_in(i, rows, vecs):
        return [_rms_fwd(rows[0], vecs[0])], []

    (n_lp,) = _rowwise("norm_in", norm_in, [(h0, d, 0)], [g_pre_mix], [(d, MXU_DTYPE)], [], tr, deps=[tok_rest])
    (wg_in,) = _pass_halves("gather_pass_in", _gather_wait("gather_wait_in", fly_in, [True], sems_in[0], [n_lp]))
    proj =_matmul("proj", n_lp, wg_in, kind="nn", tm=tm, tn=tn_in, tk=d, out_dtypes=[F32])
    ac, c3 = _conv_fwd(proj, wdw, conf_dw_b, w3, d_conf)

    def ln_parts(ac_t, ln_g, ln_b):
        mu = jnp.mean(ac_t, axis=-1, keepdims=True)
        xc = ac_t - mu
        rstd = lax.rsqrt(jnp.mean(xc * xc, axis=-1, keepdims=True) + LN_EPS)
        xh = xc * rstd
        return xh, rstd, xh * ln_g + ln_b

    def branch_act(i, rows, vecs):
        ac_t, c3_t, bg_t = rows
        _, _, al = ln_parts(ac_t, vecs[0], vecs[1])
        return [al * _sigmoid(al), bg_t * c3_t], []

    a_act, s_lp = _rowwise("branch_act", branch_act, [(ac, cw, 0), (c3, cw, 0), (proj, cw, 2)],
                           [conf_ln_g, conf_ln_b], [(d_conf, MXU_DTYPE), (d_conf, MXU_DTYPE)], [], tr)
    wg_pw, wg_sout = arrive(0, [a_act])
    y_a = _matmul("y_a", a_act, wg_pw, kind="nn", tm=tm, tn=tn_pw, tk=d_conf, out_dtypes=[F32])
    y_b = _matmul("y_b", s_lp, wg_sout, kind="nn", tm=tm, tn=tn_pw, tk=d_conf, out_dtypes=[F32])

    gate_rows = [(proj, cw, 5), (proj, cw, 6), (proj, cw, 7), (proj, cw, 8)]

    def gates_of(rows, b):
        ga = _sigmoid(jnp.concatenate([rows[0], rows[1]], axis=1) + b[:, :d])
        gb = _sigmoid(jnp.concatenate([rows[2], rows[3]], axis=1) + b[:, d:])
        return ga, gb

    def gate(i, rows, vecs):
        ga, gb = gates_of(rows[2:], vecs[0])
        return [ga * rows[0] + gb * rows[1]], []

    (m_lp,) = _rowwise("gate", gate, [(y_a, d, 0), (y_b, d, 0)] + gate_rows, [b_gates], [(d, MXU_DTYPE)], [], tr)
    wg_o = arrive(1, [m_lp])[0].reshape(d, d)
    mix = _matmul("mix", m_lp, wg_o, kind="nn", tm=tm, tn=tn_d, tk=d, out_dtypes=[F32])

    def post_mix(i, rows, vecs):
        h1_t = rows[0] + _rms_fwd(rows[1], vecs[0])
        return [h1_t, _rms_fwd(h1_t, vecs[1])], []

    h1, n2_lp = _rowwise("post_mix", post_mix, [(h0, d, 0), (mix, d, 0)], [g_post_mix, g_pre_mlp],
                         [(d, F32), (d, MXU_DTYPE)], [], tr)
    (wg_up,) = arrive(2, [n2_lp])
    up, f_lp = _matmul("up", n2_lp, wg_up, kind="nn", tm=tm, tn=tn_h, tk=d, out_dtypes=[F32, MXU_DTYPE],
                       epilogue=lambda acc: (acc, jnp.square(jnp.maximum(acc, 0.0))))
    wg_down = arrive(3, [f_lp])[0].reshape(d_ff, d)
    dn = _matmul("down", f_lp, wg_down, kind="nn", tm=tm, tn=tn_h, tk=_largest_tile(d_ff, 2048, LANES),
                 out_dtypes=[F32])

    def head(row0, rows, vecs):
        h1_t, dn_t, tgt = rows
        y = h1_t + _rms_fwd(dn_t, vecs[0])
        row = row0 + lax.broadcasted_iota(jnp.int32, (ROW_CHUNK, 1), 0)
        err = jnp.where(jnp.logical_and(row >= N_META, row < t_real), y - tgt, 0.0)
        dy = err / d
        d_dn, dg = _rms_bwd(dn_t, vecs[0], dy)
        loss_rows = 0.5 * jnp.mean(err * err, axis=-1, keepdims=True)
        return [dy, d_dn], [dg, jnp.broadcast_to(loss_rows, (ROW_CHUNK, LANES))]

    dy, d_dn, dg_post_mlp, loss_vec = _rowwise(
        "head", head, [(h1, d, 0), (dn, d, 0), (target, d, 0)], [g_post_mlp], [(d, F32), (d, MXU_DTYPE)], [d, LANES], tr)

    core_arr = core.astype(jnp.int32).reshape(1)
    place = jnp.stack([chip, core]).astype(jnp.int32)
    in_flight = {}


    def tag(members):
        return "".join(str(a) for a in members)

    def swap_start(members, gws):
        halves = [gw.reshape(N_CHIPS, 2, gw.shape[1] // 2, gw.shape[2]) for gw in gws]
        *in_flight[members], token = _swap_start(f"swap_start{tag(members)}", halves)
        return token

    def exchange_start(members, after):
        sems, halves, lands = in_flight[members]
        halves, lands = _swap_wait(f"swap_wait{tag(members)}", halves, lands, sems, after)
        pairs = [_pair_sum(f"pair_sum{a}", h, q, core_arr, _largest_tile(q.shape[1], 256, 16))
                 for a, h, q in zip(members, halves, lands)]
        *in_flight[members], token = _exchange_start(f"exchange_start{tag(members)}", pairs)
        return token

    def reduce_finish(groups, after):
        waited = _exchange_wait("exchange_wait" + "_".join(tag(g) for g in groups), [in_flight[g] for g in groups], after)
        return {a: _sum_pieces(f"sum_pieces{a}", pair, land, place, _largest_tile(land.shape[1], 256, 16))
                for g, (pairs, lands) in zip(groups, waited) for a, pair, land in zip(g, pairs, lands)}

    d_up = _matmul("d_up", d_dn, wg_down, kind="nt", tm=tm, tn=tn_h, tk=d, out_dtypes=[MXU_DTYPE], extras=[up],
                   epilogue=lambda acc, up_t: (acc * (2.0 * jnp.maximum(up_t, 0.0)),))
    tk_t = t
    gw_down = _matmul("gw_down", f_lp, d_dn, kind="tn", tm=_largest_tile(d_ff, 2048, LANES), tn=tn_d, tk=tk_t,
                      out_dtypes=[F32])
    tok = swap_start((5,), [gw_down.reshape(N_CHIPS, d_ff // N_CHIPS, d)])
    d_n2 = _matmul("d_n2", d_up, wg_up, kind="nt", tm=tm, tn=tn_h, tk=d_ff // N_CHIPS, out_dtypes=[F32], deps=[tok])
    tok = exchange_start((5,), [d_n2])
    gw_up = _matmul("gw_up", n2_lp, d_up, kind="tn", tm=_largest_tile(d, 2048, LANES), tn=tn_ff, tk=tk_t,
                    out_dtypes=[F32], out_pieces=N_CHIPS, deps=[tok])
    tok = swap_start((4,), [gw_up])

    def bwd_mid(i, rows, vecs):
        dy_t, dn2_t, h1_t, mix_t = rows
        d_h1a, dg_pre_mlp = _rms_bwd(h1_t, vecs[1], dn2_t)
        d_h1 = dy_t + d_h1a
        d_mix, dg_post_mix = _rms_bwd(mix_t, vecs[0], d_h1)
        return [d_h1, d_mix], [dg_pre_mlp, dg_post_mix]

    d_h1, d_mix, dg_pre_mlp, dg_post_mix = _rowwise(
        "bwd_mid", bwd_mid, [(dy, d, 0), (d_n2, d, 0), (h1, d, 0), (mix, d, 0)], [g_post_mix, g_pre_mlp],
        [(d, F32), (d, MXU_DTYPE)], [d, d], tr, deps=[tok])
    d_m = _matmul("d_m", d_mix, wg_o, kind="nt", tm=tm, tn=tn_h, tk=d, out_dtypes=[F32])
    tok = exchange_start((4,), [d_m])
    gw_o = _matmul("gw_o", m_lp, d_mix, kind="tn", tm=_largest_tile(d, 2048, LANES), tn=tn_h, tk=tk_t, out_dtypes=[F32],
                   deps=[tok])

    def gate_bwd(i, rows, vecs):
        dm_t, ya_t, yb_t = rows[:3]
        ga, gb = gates_of(rows[3:], vecs[0])
        d_gpre = jnp.concatenate([dm_t * ya_t * ga * (1.0 - ga), dm_t * yb_t * gb * (1.0 - gb)], axis=1)
        return [dm_t * ga, dm_t * gb, d_gpre], [d_gpre]

    d_ya, d_yb, d_gpre, dg_b_gates = _rowwise(
        "gate_bwd", gate_bwd, [(d_m, d, 0), (y_a, d, 0), (y_b, d, 0)] + gate_rows, [b_gates],
        [(d, MXU_DTYPE), (d, MXU_DTYPE), (2 * d, MXU_DTYPE)], [2 * d], tr, deps=[tok])
    d_aact = _matmul("d_aact", d_ya, wg_pw, kind="nt", tm=tm, tn=d_conf // 2, tk=tn_pw, out_dtypes=[F32], deps=[tok])
    gw_pw = _matmul("gw_pw", a_act, d_ya, kind="tn", tm=_largest_tile(d_conf, 2048, LANES), tn=tn_pw, tk=tk_t,
                    out_dtypes=[F32], out_pieces=N_CHIPS)
    d_s = _matmul("d_s", d_yb, wg_sout, kind="nt", tm=tm, tn=d_conf // 2, tk=tn_pw, out_dtypes=[F32], deps=[tok])
    gw_sout = _matmul("gw_sout", s_lp, d_yb, kind="tn", tm=_largest_tile(d_conf, 2048, LANES), tn=tn_pw, tk=tk_t,
                      out_dtypes=[F32], out_pieces=N_CHIPS, deps=[tok])
    small_big = (3, 1, 2)
    tok = swap_start(small_big, [gw_o.reshape(N_CHIPS, d // N_CHIPS, d), gw_pw, gw_sout])

    def branch_bwd(i, rows, vecs):
        daact_t, ds_t, ac_t, c3_t, bg_t = rows
        xh, rstd, al = ln_parts(ac_t, vecs[0], vecs[1])
        sg = _sigmoid(al)
        d_al = daact_t * (sg * (1.0 + al * (1.0 - sg)))
        dxh = d_al * vecs[0]
        d_ac = rstd * (dxh - jnp.mean(dxh, axis=-1, keepdims=True) - xh * jnp.mean(dxh * xh, axis=-1, keepdims=True))
        return [d_ac, ds_t * bg_t, ds_t * c3_t], [d_al * xh, d_al, d_ac]

    d_ac, d_c3, d_bg, dg_ln_g, dg_ln_b, dg_dw_b = _rowwise(
        "branch_bwd", branch_bwd, [(d_aact, cw, 0), (d_s, cw, 0), (ac, cw, 0), (c3, cw, 0), (proj, cw, 2)],
        [conf_ln_g, conf_ln_b], [(d_conf, F32), (d_conf, F32), (d_conf, MXU_DTYPE)], [d_conf] * 3, tr, deps=[tok])
    d_av, d_ag, d_cg, d_v, dg_wdw, dg_w3 = _conv_bwd(proj, d_ac, d_c3, wdw, w3, d_conf)
    d_proj = jnp.concatenate([d_av, d_ag, d_bg, d_cg, d_v, d_gpre], axis=1)
    small_w = d_conf

    def pack(arrs):
        flat = jnp.concatenate([a.reshape(-1, small_w) for a in arrs], axis=0)
        return jnp.pad(flat, ((0, -flat.shape[0] % 8), (0, 0)))

    def reduce_small(tag, arrs, deps):
        assert all(a.size % small_w == 0 for a in arrs)
        gathered = _allgather_small(f"allgather_small_{tag}", pack(arrs), deps)
        first, r0 = [], 0
        for a in arrs:
            first.append(r0)
            r0 += a.size // small_w
        return _sum_blocks(f"sum_small_{tag}", gathered, 8), first

    def rows_of(total, r0, like):
        return total[r0:r0 + like.size // small_w].reshape(like.shape)

    tok = exchange_start(small_big, [d_proj])

    gw_in = _matmul("gw_in", n_lp, d_proj, kind="tn", tm=_largest_tile(d, 2048, LANES), tn=tn_in, tk=tk_t,
                    out_dtypes=[F32], out_pieces=N_CHIPS, deps=[tok])
    tok = swap_start((0,), [gw_in])
    d_n = _matmul("d_n", d_proj, wg_in, kind="nt", tm=tm, tn=tn_h, tk=in_cols // N_CHIPS, out_dtypes=[F32], deps=[tok])
    tok = exchange_start((0,), [d_n])

    early = [5, 4, 3, 1, 2]
    reduced = reduce_finish([(5,), (4,), small_big], [tok])
    join_sems, joining, tok = _join_start("join_start_early", [reduced[a] for a in early])

    def bwd_in(i, rows, vecs):
        d_h0a, dg = _rms_bwd(rows[2], vecs[0], rows[1])
        return [rows[0] + d_h0a], [dg]

    d_h0, dg_pre_mix = _rowwise("bwd_in", bwd_in, [(d_h1, d, 0), (d_n, d, 0), (h0, d, 0)], [g_pre_mix],
                                [(d, F32)], [d], tr, deps=[tok])
    grad_x = d_h0[N_META:t_real][None]

    big_m = [m_w_in, m_conf_w_pw, m_short_w_out, m_w_o, m_w_up, m_w_down]
    big_v = [v_w_in, v_conf_w_pw, v_short_w_out, v_w_o, v_w_up, v_w_down]
    big_res = {}

    def adam_group(members, joined):
        for a, j in zip(members, joined):
            big_res[a] = _adamw(f"adamw_big{a}", big[a], j.reshape(big[a].shape), big_m[a][0], big_v[a][0],
                                _largest_tile(big[a].shape[0], 256, 8))

    adam_group(early, _join_wait("join_wait_early", joining, join_sems, [d_h0]))
    reduced = reduce_finish([(0,)], [big_res[a][1] for a in early])
    join_sems, joining, tok = _join_start("join_start_late", [reduced[0]])
    rep_g = [dg_pre_mix, dg_b_gates, dg_dw_b, dg_ln_g, dg_ln_b, dg_post_mix, dg_pre_mlp, dg_post_mlp]
    rep_w = [g_pre_mix, b_gates, conf_dw_b, conf_ln_g, conf_ln_b, g_post_mix, g_pre_mlp, g_post_mlp]
    rep_m = [m_g_pre_mix, m_b_gates, m_conf_dw_b, m_conf_ln_g, m_conf_ln_b, m_g_post_mix, m_g_pre_mlp, m_g_post_mlp]
    rep_v = [v_g_pre_mix, v_b_gates, v_conf_dw_b, v_conf_ln_g, v_conf_ln_b, v_g_post_mix, v_g_pre_mlp, v_g_post_mlp]
    col_g = [d_h0[:N_META], dg_wdw, dg_w3]
    loss_row = jnp.tile(loss_vec, (1, small_w // LANES))
    total, first = reduce_small("all", rep_g + col_g + [loss_row], [tok])
    loss = total[first[-1], 0]
    rep_res = _small_adamw("adamw_rep", total, [(first[e], w, m, v) for e, (w, m, v) in enumerate(zip(rep_w, rep_m, rep_v))])
    col_res = {}
    for a, (w, g_part, m, v) in enumerate([(meta, col_g[0], m_meta, v_meta), (conf_dw_w[0], col_g[1], m_conf_dw_w[0], v_conf_dw_w[0]),
                                           (short_dw_w[0], col_g[2], m_short_dw_w[0], v_short_dw_w[0])]):
        g_full = rows_of(total, first[len(rep_g) + a], g_part)
        g_own = lax.dynamic_slice_in_dim(g_full, chip * w.shape[1], w.shape[1], axis=1)
        col_res[a] = _adamw(f"adamw_col{a}", w, g_own, m, v, w.shape[0])
    adam_group([0], _join_wait("join_wait_late", joining, join_sems, [rep_res[0][1]] + [col_res[a][1] for a in range(3)]))

    def leaf(q):
        r = lambda a: rep_res[a][q]
        b = lambda a: big_res[a][q][None]
        return [col_res[0][q], r(0), b(0), r(1), col_res[1][q][None], r(2), r(3), r(4), b(1), col_res[2][q][None], b(2),
                b(3), r(5), r(6), b(4), b(5), r(7)]

    return (loss, grad_x, *leaf(0), *leaf(1), *leaf(2), *leaf(3))
```

```python
import jax
import jax.numpy as jnp
from jax import lax
from jax.experimental import pallas as pl
from jax.experimental.pallas import tpu as pltpu

F32 = jnp.float32
BF16 = jnp.bfloat16
MXU_DTYPE = BF16
WIRE_DTYPE = BF16

N_META = 16
CONF_KERNEL = 31
SHORT_KERNEL = 3
CONV_PAD = 32
RMS_EPS = 1e-6
LN_EPS = 1e-5
ADAM_LR = 0.001
ADAM_B1 = 0.9
ADAM_B2 = 0.999
ADAM_EPS = 1e-08
ADAM_WD = 0.01
ADAM_STEP = 10

N_CHIPS = 4
MESH = pl.DeviceIdType.MESH
LANES = 128


def _sigmoid(z):
    return 1.0 / (1.0 + jnp.exp(-z))


ANY_SPEC = pl.BlockSpec(memory_space=pl.ANY)


def _matmul(name, a, b, *, kind, tm, tn, tk, out_dtypes, out_pieces=1, epilogue=None, extras=(), deps=()):
    pieces = b.shape[0] if b.ndim == 3 else 1
    if kind == "nn":
        m, kdim = a.shape
        n = b.shape[-1] * pieces
        dims = (((1,), (0,)), ((), ()))
        a_spec = pl.BlockSpec((tm, tk), lambda i, j, k: (i, k))
        if b.ndim == 2:
            b_spec = pl.BlockSpec((tk, tn), lambda i, j, k: (k, j))
        else:
            npp = b.shape[-1] // tn
            b_spec = pl.BlockSpec((None, tk, tn), lambda i, j, k: (j // npp, k, j % npp))
    elif kind == "nt":
        m, kdim = a.shape
        n = b.shape[-2]
        dims = (((1,), (1,)), ((), ()))
        a_spec = pl.BlockSpec((tm, tk), lambda i, j, k: (i, k))
        if b.ndim == 2:
            b_spec = pl.BlockSpec((tn, tk), lambda i, j, k: (j, k))
        else:
            kpp = b.shape[-1] // tk
            b_spec = pl.BlockSpec((None, tn, tk), lambda i, j, k: (k // kpp, j, k % kpp))
    else:
        kdim, m = a.shape
        n = b.shape[-1]
        dims = (((0,), (0,)), ((), ()))
        a_spec = pl.BlockSpec((tk, tm), lambda i, j, k: (k, i))
        b_spec = pl.BlockSpec((tk, tn), lambda i, j, k: (k, j))
    assert m % tm == 0 and n % tn == 0 and kdim % tk == 0, (name, m, n, kdim, tm, tn, tk)
    nk = kdim // tk
    if out_pieces == 1:
        out_shape = (m, n)
        out_spec = pl.BlockSpec((tm, tn), lambda i, j, k: (i, j))
    else:
        onpp = n // out_pieces // tn
        out_shape = (out_pieces, m, n // out_pieces)
        out_spec = pl.BlockSpec((None, tm, tn), lambda i, j, k: (j // onpp, i, j % onpp))
    n_ex, n_out, n_in = len(extras), len(out_dtypes), len(extras) + len(deps)
    if epilogue is None:
        epilogue = lambda acc: (acc,)

    def body(a_ref, b_ref, *rest):
        ex_refs, o_refs = rest[:n_ex], rest[n_in:n_in + n_out]
        prod = lax.dot_general(a_ref[...], b_ref[...], dims, preferred_element_type=F32)

        def finish(acc):
            tiles = epilogue(acc, *[r[...] for r in ex_refs])
            for o_ref, t in zip(o_refs, tiles):
                o_ref[...] = t.astype(o_ref.dtype)

        if nk == 1:
            finish(prod)
        else:
            acc_ref = rest[n_in + n_out]
            k = pl.program_id(2)

            @pl.when(k == 0)
            def _():
                acc_ref[...] = prod

            @pl.when(jnp.logical_and(k > 0, k < nk - 1))
            def _():
                acc_ref[...] += prod

            @pl.when(k == nk - 1)
            def _():
                finish(acc_ref[...] + prod)

    ex_specs = [pl.BlockSpec((tm, tn), lambda i, j, k: (i, j)) for _ in extras]
    res = pl.pallas_call(
        body,
        name=name,
        grid=(m // tm, n // tn, nk),
        in_specs=[a_spec, b_spec, *ex_specs] + [ANY_SPEC] * len(deps),
        out_specs=[out_spec] * n_out,
        out_shape=[jax.ShapeDtypeStruct(out_shape, d) for d in out_dtypes],
        scratch_shapes=[pltpu.VMEM((tm, tn), F32)] if nk > 1 else [],
        compiler_params=pltpu.CompilerParams(dimension_semantics=("parallel", "parallel", "arbitrary")),
    )(a, b, *extras, *deps)
    return res[0] if n_out == 1 else res


ROW_CHUNK = 16
SUBLANES = 8


def _rowwise(name, fn, rows, vecs, outs, sums, tr, deps=()):
    t = rows[0][0].shape[0]
    assert t % tr == 0 and tr % ROW_CHUNK == 0
    n_r, n_v, n_o, n_s = len(rows), len(vecs), len(outs), len(sums)
    n_in = n_r + n_v + len(deps)
    n_steps = t // tr

    def body(*refs):
        r_in, v_in = refs[:n_r], refs[n_r:n_r + n_v]
        o_refs = refs[n_in:n_in + n_o]
        s_refs = refs[n_in + n_o:n_in + n_o + n_s]
        acc_refs = refs[n_in + n_o + n_s:]
        i = pl.program_id(0)

        @pl.when(i == 0)
        def _():
            for acc_ref in acc_refs:
                acc_ref[...] = jnp.zeros(acc_ref.shape, F32)

        def chunk(ci):
            r0 = ci * ROW_CHUNK
            sl = pl.ds(r0, ROW_CHUNK)
            o_tiles, s_tiles = fn(i * tr + r0, [r[sl, :] for r in r_in], [v[...] for v in v_in])
            for o_ref, tile in zip(o_refs, o_tiles):
                o_ref[sl, :] = tile.astype(o_ref.dtype)
            for acc_ref, tile in zip(acc_refs, s_tiles):
                part = tile[0:SUBLANES]
                for s in range(1, ROW_CHUNK // SUBLANES):
                    part = part + tile[s * SUBLANES:(s + 1) * SUBLANES]
                acc_ref[...] += part

        for ci in range(tr // ROW_CHUNK):
            chunk(ci)

        @pl.when(i == n_steps - 1)
        def _():
            for s_ref, acc_ref in zip(s_refs, acc_refs):
                s_ref[...] = jnp.sum(acc_ref[...], axis=0, keepdims=True)

    def row_spec(width, blk):
        return pl.BlockSpec((tr, width), lambda i: (i, blk))

    res = pl.pallas_call(
        body,
        name=name,
        grid=(t // tr,),
        in_specs=[row_spec(w, blk) for _, w, blk in rows]
        + [pl.BlockSpec(v.shape, lambda i: (0, 0)) for v in vecs] + [ANY_SPEC] * len(deps),
        out_specs=[pl.BlockSpec((tr, c), lambda i: (i, 0)) for c, _ in outs]
        + [pl.BlockSpec((1, c), lambda i: (0, 0)) for c in sums],
        out_shape=[jax.ShapeDtypeStruct((t, c), d) for c, d in outs]
        + [jax.ShapeDtypeStruct((1, c), F32) for c in sums],
        scratch_shapes=[pltpu.VMEM((SUBLANES, c), F32) for c in sums],
        compiler_params=pltpu.CompilerParams(dimension_semantics=("arbitrary",)),
    )(*[r[0] for r in rows], *vecs, *deps)
    return res


def _rms_fwd(x, g):
    r = lax.rsqrt(jnp.mean(x * x, axis=-1, keepdims=True) + RMS_EPS)
    return x * r * g


def _rms_bwd(x, g, dy):
    r = lax.rsqrt(jnp.mean(x * x, axis=-1, keepdims=True) + RMS_EPS)
    xn = x * r
    dxn = dy * g
    dx = r * (dxn - xn * jnp.mean(dxn * xn, axis=-1, keepdims=True))
    return dx, dy * xn


CONV_ROWS = 64
CONV_LANES = 128


def _conv_fwd(proj, wdw, bdw, w3, d_conf):
    t = proj.shape[0]
    cl = CONV_LANES
    nb = d_conf // cl
    nchunk = t // CONV_ROWS
    assert t % CONV_ROWS == 0

    def body(av_ref, ag_ref, cg_ref, v_ref, wdw_ref, bdw_ref, w3_ref, ac_ref, c3_ref, apad, cpad):
        zeros = jnp.zeros((CONV_PAD, cl), F32)
        apad[0:CONV_PAD, :] = zeros
        cpad[0:CONV_PAD, :] = zeros
        apad[CONV_PAD:, :] = av_ref[...] * _sigmoid(ag_ref[...])
        cpad[CONV_PAD:, :] = cg_ref[...] * v_ref[...]

        def chunk(ci, carry):
            base = pl.multiple_of(ci * CONV_ROWS, 8)
            acc = jnp.zeros((CONV_ROWS, cl), F32) + bdw_ref[...]
            for k in range(CONF_KERNEL):
                off = CONV_PAD - (CONF_KERNEL - 1) + k
                acc = acc + apad[pl.ds(base + off, CONV_ROWS), :] * wdw_ref[k:k + 1, :]
            ac_ref[pl.ds(base, CONV_ROWS), :] = acc
            acc3 = jnp.zeros((CONV_ROWS, cl), F32)
            for k in range(SHORT_KERNEL):
                off = CONV_PAD - (SHORT_KERNEL - 1) + k
                acc3 = acc3 + cpad[pl.ds(base + off, CONV_ROWS), :] * w3_ref[k:k + 1, :]
            c3_ref[pl.ds(base, CONV_ROWS), :] = acc3
            return carry

        lax.fori_loop(0, nchunk, chunk, 0)

    def col(blk0):
        return pl.BlockSpec((t, cl), lambda j: (0, blk0 + j))

    return pl.pallas_call(
        body,
        name="conv_fwd",
        grid=(nb,),
        in_specs=[col(0), col(nb), col(3 * nb), col(4 * nb),
                  pl.BlockSpec((CONF_KERNEL, cl), lambda j: (0, j)),
                  pl.BlockSpec((1, cl), lambda j: (0, j)),
                  pl.BlockSpec((SHORT_KERNEL, cl), lambda j: (0, j))],
        out_specs=[pl.BlockSpec((t, cl), lambda j: (0, j))] * 2,
        out_shape=[jax.ShapeDtypeStruct((t, d_conf), F32)] * 2,
        scratch_shapes=[pltpu.VMEM((t + CONV_PAD, cl), F32)] * 2,
        compiler_params=pltpu.CompilerParams(dimension_semantics=("parallel",)),
    )(proj, proj, proj, proj, wdw, bdw, w3)


def _conv_bwd(proj, d_ac, d_c3, wdw, w3, d_conf):
    t = proj.shape[0]
    cl = CONV_LANES
    nb = d_conf // cl
    nchunk = t // CONV_ROWS
    nsub = CONV_ROWS // 8

    def fold(p):
        r = p[0:8]
        for s in range(1, nsub):
            r = r + p[8 * s:8 * s + 8]
        return r

    def body(av_ref, ag_ref, cg_ref, v_ref, dac_ref, dc3_ref, wdw_ref, w3_ref,
             dav_ref, dag_ref, dcg_ref, dv_ref, dwdw_ref, dw3_ref, apad, cpad, dapad, dcpad):
        zeros = jnp.zeros((CONV_PAD, cl), F32)
        apad[0:CONV_PAD, :] = zeros
        cpad[0:CONV_PAD, :] = zeros
        apad[CONV_PAD:, :] = av_ref[...] * _sigmoid(ag_ref[...])
        cpad[CONV_PAD:, :] = cg_ref[...] * v_ref[...]
        dapad[0:t, :] = dac_ref[...]
        dcpad[0:t, :] = dc3_ref[...]
        dapad[t:, :] = zeros
        dcpad[t:, :] = zeros

        def chunk(ci, accs):
            base = pl.multiple_of(ci * CONV_ROWS, 8)
            rows = pl.ds(base, CONV_ROWS)
            da = jnp.zeros((CONV_ROWS, cl), F32)
            for k in range(CONF_KERNEL):
                da = da + dapad[pl.ds(base + (CONF_KERNEL - 1 - k), CONV_ROWS), :] * wdw_ref[k:k + 1, :]
            dcv = jnp.zeros((CONV_ROWS, cl), F32)
            for k in range(SHORT_KERNEL):
                dcv = dcv + dcpad[pl.ds(base + (SHORT_KERNEL - 1 - k), CONV_ROWS), :] * w3_ref[k:k + 1, :]
            av, sg = av_ref[rows, :], _sigmoid(ag_ref[rows, :])
            dav_ref[rows, :] = (da * sg).astype(dav_ref.dtype)
            dag_ref[rows, :] = (da * av * sg * (1.0 - sg)).astype(dag_ref.dtype)
            dcg_ref[rows, :] = (dcv * v_ref[rows, :]).astype(dcg_ref.dtype)
            dv_ref[rows, :] = (dcv * cg_ref[rows, :]).astype(dv_ref.dtype)
            d_out, d_out3 = dac_ref[rows, :], dc3_ref[rows, :]
            new = []
            for k in range(CONF_KERNEL):
                off = CONV_PAD - (CONF_KERNEL - 1) + k
                new.append(accs[k] + fold(d_out * apad[pl.ds(base + off, CONV_ROWS), :]))
            for k in range(SHORT_KERNEL):
                off = CONV_PAD - (SHORT_KERNEL - 1) + k
                new.append(accs[CONF_KERNEL + k] + fold(d_out3 * cpad[pl.ds(base + off, CONV_ROWS), :]))
            return tuple(new)

        init = tuple(jnp.zeros((8, cl), F32) for _ in range(CONF_KERNEL + SHORT_KERNEL))
        accs = lax.fori_loop(0, nchunk, chunk, init)
        for k in range(CONF_KERNEL):
            dwdw_ref[k:k + 1, :] = jnp.sum(accs[k], axis=0, keepdims=True)
        for k in range(SHORT_KERNEL):
            dw3_ref[k:k + 1, :] = jnp.sum(accs[CONF_KERNEL + k], axis=0, keepdims=True)

    def col(blk0):
        return pl.BlockSpec((t, cl), lambda j: (0, blk0 + j))

    own = pl.BlockSpec((t, cl), lambda j: (0, j))
    return pl.pallas_call(
        body,
        name="conv_bwd",
        grid=(nb,),
        in_specs=[col(0), col(nb), col(3 * nb), col(4 * nb), own, own,
                  pl.BlockSpec((CONF_KERNEL, cl), lambda j: (0, j)),
                  pl.BlockSpec((SHORT_KERNEL, cl), lambda j: (0, j))],
        out_specs=[own] * 4 + [pl.BlockSpec((CONF_KERNEL, cl), lambda j: (0, j)),
                               pl.BlockSpec((SHORT_KERNEL, cl), lambda j: (0, j))],
        out_shape=[jax.ShapeDtypeStruct((t, d_conf), MXU_DTYPE)] * 4
        + [jax.ShapeDtypeStruct((CONF_KERNEL, d_conf), F32), jax.ShapeDtypeStruct((SHORT_KERNEL, d_conf), F32)],
        scratch_shapes=[pltpu.VMEM((t + CONV_PAD, cl), F32)] * 4,
        compiler_params=pltpu.CompilerParams(dimension_semantics=("parallel",)),
    )(proj, proj, proj, proj, d_ac, d_c3, wdw, w3)


def _elementwise(name, fn, ins, out_dtypes, tr, deps=()):
    ins = [(a, ()) if not isinstance(a, tuple) else a for a in ins]
    r, c = ins[0][0].shape[-2:]
    assert r % tr == 0, (name, r, tr)
    n_in = len(ins)

    def body(*refs):
        tiles = fn(*[x[...] for x in refs[:n_in]])
        for o_ref, tile in zip(refs[n_in + len(deps):], tiles):
            o_ref[...] = tile.astype(o_ref.dtype)

    def spec(lead):
        return pl.BlockSpec((None,) * len(lead) + (tr, c), lambda i: (*lead, i, 0))

    res = pl.pallas_call(
        body,
        name=name,
        grid=(r // tr,),
        in_specs=[spec(lead) for _, lead in ins] + [ANY_SPEC] * len(deps),
        out_specs=[pl.BlockSpec((tr, c), lambda i: (i, 0))] * len(out_dtypes),
        out_shape=[jax.ShapeDtypeStruct((r, c), d) for d in out_dtypes],
        compiler_params=pltpu.CompilerParams(dimension_semantics=("parallel",)),
    )(*[a for a, _ in ins], *deps)
    return res


def _adamw_tiles(w, g, m, v):
    m = ADAM_B1 * m + (1.0 - ADAM_B1) * g
    v = ADAM_B2 * v + (1.0 - ADAM_B2) * jnp.square(g)
    m_hat = m / (1.0 - ADAM_B1 ** ADAM_STEP)
    v_hat = v / (1.0 - ADAM_B2 ** ADAM_STEP)
    delta = -ADAM_LR * (m_hat / (jnp.sqrt(v_hat) + ADAM_EPS) + ADAM_WD * w)
    return g, delta, m, v


def _adamw(name, w, g, m, v, tr, deps=()):
    shape = w.shape
    flat = [a.reshape(shape[-2:]) if a.ndim > 2 else a for a in (w, g, m, v)]
    res = _elementwise(name, _adamw_tiles, flat, [F32] * 4, tr, deps)
    return [a.reshape(shape) for a in res]


def _small_adamw(name, total, entries):
    sw = total.shape[1]
    n = len(entries)

    def body(total_ref, *refs):
        ins, outs = refs[:3 * n], refs[3 * n:]
        for e, (row0, w, _, _) in enumerate(entries):
            for q in range(w.shape[1] // sw):
                cols = slice(q * sw, (q + 1) * sw)
                tiles = _adamw_tiles(ins[3 * e][:, cols], total_ref[row0 + q:row0 + q + 1, :],
                                     ins[3 * e + 1][:, cols], ins[3 * e + 2][:, cols])
                for o_ref, tile in zip(outs[4 * e:4 * e + 4], tiles):
                    o_ref[:, cols] = tile

    res = pl.pallas_call(
        body,
        name=name,
        out_shape=[jax.ShapeDtypeStruct(w.shape, F32) for _, w, _, _ in entries for _ in range(4)],
    )(total, *[a for _, w, m, v in entries for a in (w, m, v)])
    return [list(res[4 * e:4 * e + 4]) for e in range(n)]


def _pair_sum(name, p, q, core, tr):
    n_p, _, hr, c = p.shape
    assert hr % tr == 0

    def body(core_ref, p_ref, q_ref, o_ref):
        o_ref[...] = (p_ref[...] + q_ref[...]).astype(o_ref.dtype)

    return pl.pallas_call(
        body,
        name=name,
        grid_spec=pltpu.PrefetchScalarGridSpec(
            num_scalar_prefetch=1,
            grid=(n_p, hr // tr),
            in_specs=[pl.BlockSpec((None, None, tr, c), lambda a, i, core_ref: (a, core_ref[0], i, 0)),
                      pl.BlockSpec((None, tr, c), lambda a, i, core_ref: (a, i, 0))],
            out_specs=pl.BlockSpec((None, tr, c), lambda a, i, core_ref: (a, i, 0)),
        ),
        out_shape=jax.ShapeDtypeStruct((n_p, hr, c), WIRE_DTYPE),
        compiler_params=pltpu.CompilerParams(dimension_semantics=("parallel", "parallel")),
    )(core, p, q)


def _into_slot(name, w, slots, slot, dtype, tr, deps=()):
    r, c = w.shape
    assert r % tr == 0

    def body(slot_ref, w_ref, *rest):
        o_ref = rest[len(deps)]
        o_ref[...] = w_ref[...].astype(o_ref.dtype)

    return pl.pallas_call(
        body,
        name=name,
        grid_spec=pltpu.PrefetchScalarGridSpec(
            num_scalar_prefetch=1,
            grid=(r // tr,),
            in_specs=[pl.BlockSpec((tr, c), lambda i, slot_ref: (i, 0))] + [ANY_SPEC] * len(deps),
            out_specs=pl.BlockSpec((None, tr, c), lambda i, slot_ref: (slot_ref[0], i, 0)),
        ),
        out_shape=jax.ShapeDtypeStruct((slots, r, c), dtype),
        compiler_params=pltpu.CompilerParams(dimension_semantics=("parallel",)),
    )(slot, w, *deps)


def _sum_pieces(name, own, rb, place, tr):
    n_p, hr, c = rb.shape
    assert hr % tr == 0

    def body(place_ref, own_ref, *refs):
        chip = place_ref[0]
        acc = None
        for k in range(n_p):
            tile = jnp.where(chip == k, own_ref[...], refs[k][...]).astype(F32)
            acc = tile if acc is None else acc + tile
        refs[n_p][...] = acc

    def landed(k):
        return pl.BlockSpec((None, tr, c), lambda i, place_ref: (jnp.where(place_ref[0] == k, (k + 1) % n_p, k), i, 0))

    return pl.pallas_call(
        body,
        name=name,
        grid_spec=pltpu.PrefetchScalarGridSpec(
            num_scalar_prefetch=1,
            grid=(hr // tr,),
            in_specs=[pl.BlockSpec((None, tr, c), lambda i, place_ref: (place_ref[0], i, 0))]
            + [landed(k) for k in range(n_p)],
            out_specs=pl.BlockSpec((None, tr, c), lambda i, place_ref: (place_ref[1], i, 0)),
        ),
        out_shape=jax.ShapeDtypeStruct((2, hr, c), F32),
        compiler_params=pltpu.CompilerParams(dimension_semantics=("parallel",)),
    )(place, own, *([rb] * n_p))


HBM_SPEC = pl.BlockSpec(memory_space=pl.ANY)


def _place():
    x, y, c = lax.axis_index("x"), lax.axis_index("y"), lax.axis_index("c")
    chips = [(1 - x, y), (x, 1 - y), (1 - x, 1 - y)]
    return x, y, c, chips


HBM_ONLY = pl.BlockSpec(memory_space=pltpu.HBM)
SEM_SPEC = pl.BlockSpec(memory_space=pltpu.SEMAPHORE)
DATAFLOW = pltpu.SideEffectType.DATAFLOW_SIDE_EFFECTING


def _in_hbm(a):
    return pltpu.with_memory_space_constraint(a, pltpu.HBM)


def _shard_part(ref, is_split, slot, h):
    if not is_split:
        return ref.at[slot]
    hr = ref.shape[1] // 2
    return ref.at[slot, pl.ds(h * hr, hr), :]


TOKEN = jax.ShapeDtypeStruct((8, LANES), F32)
VMEM_SPEC = pl.BlockSpec(memory_space=pltpu.VMEM)


def _gather_start(name, bufs, split, groups, deps=()):
    n, ng = len(bufs), len(groups)

    def body(*refs):
        ins, sems = refs[:n], refs[n + len(deps):n + len(deps) + 2 * ng]
        refs[-1][...] = jnp.zeros(TOKEN.shape, TOKEN.dtype)
        x, y, c, chips = _place()
        me = 2 * x + y
        for g, members in enumerate(groups):
            for s, a in enumerate(members):
                mine = _shard_part(ins[a], split[a], me, c)
                for j, chip in enumerate(chips):
                    pltpu.make_async_remote_copy(
                        src_ref=mine, dst_ref=mine, send_sem=sems[2 * g].at[3 * s + j], recv_sem=sems[2 * g + 1].at[3 * s + j],
                        device_id=(*chip, c), device_id_type=MESH).start()

    res = pl.pallas_call(
        body,
        name=name,
        in_specs=[HBM_ONLY] * n + [ANY_SPEC] * len(deps),
        out_specs=[SEM_SPEC] * (2 * ng) + [HBM_ONLY] * n + [VMEM_SPEC],
        out_shape=[pltpu.SemaphoreType.DMA((3 * len(members),)) for members in groups for _ in range(2)]
        + [pltpu.HBM(b.shape, b.dtype) for b in bufs] + [TOKEN],
        input_output_aliases={a: 2 * ng + a for a in range(n)},
        compiler_params=pltpu.CompilerParams(has_side_effects=DATAFLOW),
    )(*[_in_hbm(b) for b in bufs], *deps)
    return [(res[2 * g], res[2 * g + 1]) for g in range(ng)], list(res[2 * ng:2 * ng + n]), res[-1]


def _gather_wait(name, bufs, split, sems, after):
    n = len(bufs)

    def body(*refs):
        ins, send_sems, recv_sems = refs[:n], refs[n], refs[n + 1]
        x, y, c, chips = _place()
        me = 2 * x + y
        for s in range(n):
            for j, chip in enumerate(chips):
                copy = pltpu.make_async_remote_copy(
                    src_ref=_shard_part(ins[s], split[s], me, c),
                    dst_ref=_shard_part(ins[s], split[s], 2 * chip[0] + chip[1], c),
                    send_sem=send_sems.at[3 * s + j], recv_sem=recv_sems.at[3 * s + j],
                    device_id=(*chip, c), device_id_type=MESH)
                copy.wait_send()
                copy.wait_recv()

    res = pl.pallas_call(
        body,
        name=name,
        in_specs=[HBM_ONLY] * n + [SEM_SPEC, SEM_SPEC] + [ANY_SPEC] * len(after),
        out_specs=[HBM_ONLY] * n,
        out_shape=[pltpu.HBM(b.shape, b.dtype) for b in bufs],
        input_output_aliases={a: a for a in range(n)},
        compiler_params=pltpu.CompilerParams(has_side_effects=DATAFLOW),
    )(*bufs, *sems, *after)
    return list(res)


def _pass_halves(name, bufs):
    n = len(bufs)

    def body(*refs):
        outs = refs[n:2 * n]
        send_sems, recv_sems = refs[2 * n:]
        x, y, c, chips = _place()
        sibling = (x, y, 1 - c)

        def copy(a, j, h):
            blk = _shard_part(outs[a], True, 2 * chips[j][0] + chips[j][1], h)
            return pltpu.make_async_remote_copy(
                src_ref=blk, dst_ref=blk, send_sem=send_sems.at[3 * a + j], recv_sem=recv_sems.at[3 * a + j],
                device_id=sibling, device_id_type=MESH)

        sends = [copy(a, j, c) for a in range(n) for j in range(3)]
        for cp in sends:
            cp.start()
        for a in range(n):
            for j in range(3):
                copy(a, j, 1 - c).wait_recv()
        for cp in sends:
            cp.wait_send()

    res = pl.pallas_call(
        body,
        name=name,
        in_specs=[HBM_SPEC] * n,
        out_specs=[HBM_SPEC] * n,
        out_shape=[jax.ShapeDtypeStruct(b.shape, b.dtype) for b in bufs],
        input_output_aliases={a: a for a in range(n)},
        scratch_shapes=[pltpu.SemaphoreType.DMA((3 * n,)), pltpu.SemaphoreType.DMA((3 * n,))],
    )(*bufs)
    return list(res)


def _exchange_start(name, pairs):
    n = len(pairs)

    def body(*refs):
        pair_refs, land_refs, send_sems, recv_sems = refs[:n], refs[n:2 * n], refs[2 * n], refs[2 * n + 1]
        x, y, c, chips = _place()
        me = 2 * x + y
        for a in range(n):
            for j, chip in enumerate(chips):
                pltpu.make_async_remote_copy(
                    src_ref=pair_refs[a].at[2 * chip[0] + chip[1]], dst_ref=land_refs[a].at[me],
                    send_sem=send_sems.at[3 * a + j], recv_sem=recv_sems.at[3 * a + j],
                    device_id=(*chip, c), device_id_type=MESH).start()
        refs[-1][...] = jnp.zeros(TOKEN.shape, TOKEN.dtype)

    res = pl.pallas_call(
        body,
        name=name,
        in_specs=[HBM_ONLY] * (2 * n),
        out_specs=[SEM_SPEC, SEM_SPEC] + [HBM_ONLY] * (2 * n) + [VMEM_SPEC],
        out_shape=[pltpu.SemaphoreType.DMA((3 * n,)), pltpu.SemaphoreType.DMA((3 * n,))]
        + [pltpu.HBM(p.shape, p.dtype) for p in pairs] * 2 + [TOKEN],
        input_output_aliases={a: 2 + a for a in range(2 * n)},
        compiler_params=pltpu.CompilerParams(has_side_effects=DATAFLOW),
    )(*[_in_hbm(p) for p in pairs], *[_in_hbm(lax.empty(p.shape, p.dtype)) for p in pairs])
    return (res[0], res[1]), list(res[2:2 + n]), list(res[2 + n:2 + 2 * n]), res[-1]


def _exchange_wait(name, groups, after):
    sizes = [len(pairs) for _, pairs, _ in groups]
    n_buf = 2 * sum(sizes)

    def body(*refs):
        x, y, c, chips = _place()
        at_buf, at_sem = 0, n_buf
        for n in sizes:
            pair_refs, land_refs = refs[at_buf:at_buf + n], refs[at_buf + n:at_buf + 2 * n]
            send_sems, recv_sems = refs[at_sem], refs[at_sem + 1]
            at_buf, at_sem = at_buf + 2 * n, at_sem + 2
            for a in range(n):
                for j, chip in enumerate(chips):
                    k = 2 * chip[0] + chip[1]
                    copy = pltpu.make_async_remote_copy(
                        src_ref=pair_refs[a].at[k], dst_ref=land_refs[a].at[k],
                        send_sem=send_sems.at[3 * a + j], recv_sem=recv_sems.at[3 * a + j],
                        device_id=(*chip, c), device_id_type=MESH)
                    copy.wait_send()
                    copy.wait_recv()

    bufs = [b for _, pairs, lands in groups for b in (*pairs, *lands)]
    sems = [s for group_sems, _, _ in groups for s in group_sems]
    res = pl.pallas_call(
        body,
        name=name,
        in_specs=[HBM_ONLY] * n_buf + [SEM_SPEC] * len(sems) + [ANY_SPEC] * len(after),
        out_specs=[HBM_ONLY] * n_buf,
        out_shape=[pltpu.HBM(b.shape, b.dtype) for b in bufs],
        input_output_aliases={a: a for a in range(n_buf)},
        compiler_params=pltpu.CompilerParams(has_side_effects=DATAFLOW),
    )(*bufs, *sems, *after)
    out, at = [], 0
    for n in sizes:
        out.append((list(res[at:at + n]), list(res[at + n:at + 2 * n])))
        at += 2 * n
    return out


def _swap_start(name, halves):
    n = len(halves)
    n_p = halves[0].shape[0]
    land_shapes = [(n_p, *h.shape[2:]) for h in halves]

    def body(*refs):
        half_refs, land_refs, send_sems, recv_sems = refs[:n], refs[n:2 * n], refs[2 * n], refs[2 * n + 1]
        x, y, c, _ = _place()
        for a in range(n):
            for p in range(n_p):
                pltpu.make_async_remote_copy(
                    src_ref=half_refs[a].at[p, 1 - c], dst_ref=land_refs[a].at[p],
                    send_sem=send_sems.at[n_p * a + p], recv_sem=recv_sems.at[n_p * a + p],
                    device_id=(x, y, 1 - c), device_id_type=MESH).start()
        refs[-1][...] = jnp.zeros(TOKEN.shape, TOKEN.dtype)

    res = pl.pallas_call(
        body,
        name=name,
        in_specs=[HBM_ONLY] * (2 * n),
        out_specs=[SEM_SPEC, SEM_SPEC] + [HBM_ONLY] * (2 * n) + [VMEM_SPEC],
        out_shape=[pltpu.SemaphoreType.DMA((n_p * n,)), pltpu.SemaphoreType.DMA((n_p * n,))]
        + [pltpu.HBM(h.shape, h.dtype) for h in halves]
        + [pltpu.HBM(s, h.dtype) for s, h in zip(land_shapes, halves)] + [TOKEN],
        input_output_aliases={a: 2 + a for a in range(2 * n)},
        compiler_params=pltpu.CompilerParams(has_side_effects=DATAFLOW),
    )(*[_in_hbm(h) for h in halves], *[_in_hbm(lax.empty(s, h.dtype)) for s, h in zip(land_shapes, halves)])
    return (res[0], res[1]), list(res[2:2 + n]), list(res[2 + n:2 + 2 * n]), res[-1]


def _swap_wait(name, halves, lands, sems, after):
    n = len(halves)
    n_p = halves[0].shape[0]

    def body(*refs):
        half_refs, land_refs, send_sems, recv_sems = refs[:n], refs[n:2 * n], refs[2 * n], refs[2 * n + 1]
        x, y, c, _ = _place()
        for a in range(n):
            for p in range(n_p):
                copy = pltpu.make_async_remote_copy(
                    src_ref=half_refs[a].at[p, 1 - c], dst_ref=land_refs[a].at[p],
                    send_sem=send_sems.at[n_p * a + p], recv_sem=recv_sems.at[n_p * a + p],
                    device_id=(x, y, 1 - c), device_id_type=MESH)
                copy.wait_send()
                copy.wait_recv()

    res = pl.pallas_call(
        body,
        name=name,
        in_specs=[HBM_ONLY] * (2 * n) + [SEM_SPEC, SEM_SPEC] + [ANY_SPEC] * len(after),
        out_specs=[HBM_ONLY] * (2 * n),
        out_shape=[pltpu.HBM(b.shape, b.dtype) for b in (*halves, *lands)],
        input_output_aliases={a: a for a in range(2 * n)},
        compiler_params=pltpu.CompilerParams(has_side_effects=DATAFLOW),
    )(*halves, *lands, *sems, *after)
    return list(res[:n]), list(res[n:])


def _join_start(name, bufs):
    n = len(bufs)

    def body(*refs):
        ins, send_sems, recv_sems = refs[:n], refs[n], refs[n + 1]
        x, y, c, _ = _place()
        for a in range(n):
            pltpu.make_async_remote_copy(
                src_ref=ins[a].at[c], dst_ref=ins[a].at[c], send_sem=send_sems.at[a], recv_sem=recv_sems.at[a],
                device_id=(x, y, 1 - c), device_id_type=MESH).start()
        refs[-1][...] = jnp.zeros(TOKEN.shape, TOKEN.dtype)

    res = pl.pallas_call(
        body,
        name=name,
        in_specs=[HBM_ONLY] * n,
        out_specs=[SEM_SPEC, SEM_SPEC] + [HBM_ONLY] * n + [VMEM_SPEC],
        out_shape=[pltpu.SemaphoreType.DMA((n,)), pltpu.SemaphoreType.DMA((n,))]
        + [pltpu.HBM(b.shape, b.dtype) for b in bufs] + [TOKEN],
        input_output_aliases={a: 2 + a for a in range(n)},
        compiler_params=pltpu.CompilerParams(has_side_effects=DATAFLOW),
    )(*[_in_hbm(b) for b in bufs])
    return (res[0], res[1]), list(res[2:2 + n]), res[-1]


def _join_wait(name, bufs, sems, after):
    n = len(bufs)

    def body(*refs):
        ins, send_sems, recv_sems = refs[:n], refs[n], refs[n + 1]
        x, y, c, _ = _place()
        for a in range(n):
            copy = pltpu.make_async_remote_copy(
                src_ref=ins[a].at[c], dst_ref=ins[a].at[1 - c], send_sem=send_sems.at[a], recv_sem=recv_sems.at[a],
                device_id=(x, y, 1 - c), device_id_type=MESH)
            copy.wait_send()
            copy.wait_recv()

    res = pl.pallas_call(
        body,
        name=name,
        in_specs=[HBM_ONLY] * n + [SEM_SPEC, SEM_SPEC] + [ANY_SPEC] * len(after),
        out_specs=[HBM_ONLY] * n,
        out_shape=[pltpu.HBM(b.shape, b.dtype) for b in bufs],
        input_output_aliases={a: a for a in range(n)},
        compiler_params=pltpu.CompilerParams(has_side_effects=DATAFLOW),
    )(*bufs, *sems, *after)
    return list(res)


def _allgather_small(name, block, deps=()):
    m_per, n = block.shape

    def body(x_ref, *rest):
        out_ref, send_sems, recv_sems, local_sem = rest[len(deps):]
        x, y, c, chips = _place()
        me, sibling = (x, y, c), (x, y, 1 - c)

        def rows(px, py, pc):
            return out_ref.at[pl.ds((4 * px + 2 * py + pc) * m_per, m_per), :]

        def copy(k, blk, to, src=None):
            return pltpu.make_async_remote_copy(
                src_ref=rows(*blk) if src is None else src, dst_ref=rows(*blk),
                send_sem=send_sems.at[k], recv_sem=recv_sems.at[k], device_id=to, device_id_type=MESH)

        mine = pltpu.make_async_copy(x_ref, rows(*me), local_sem)
        mine.start()
        first = [copy(0, me, sibling, src=x_ref)]
        first += [copy(1 + j, me, (*chip, c), src=x_ref) for j, chip in enumerate(chips)]
        for cp in first:
            cp.start()
        passed = [copy(4 + j, (*chip, c), sibling) for j, chip in enumerate(chips)]
        for j, chip in enumerate(chips):
            copy(1 + j, (*chip, c), me).wait_recv()
            passed[j].start()
        copy(0, sibling, me).wait_recv()
        for j, chip in enumerate(chips):
            copy(4 + j, (*chip, 1 - c), me).wait_recv()
        for cp in first + passed:
            cp.wait_send()
        mine.wait()

    return pl.pallas_call(
        body,
        name=name,
        out_shape=jax.ShapeDtypeStruct((8 * m_per, n), block.dtype),
        in_specs=[pl.BlockSpec(memory_space=pltpu.VMEM)] + [ANY_SPEC] * len(deps),
        out_specs=pl.BlockSpec(memory_space=pltpu.VMEM),
        scratch_shapes=[pltpu.SemaphoreType.DMA((7,)), pltpu.SemaphoreType.DMA((7,)), pltpu.SemaphoreType.DMA],
    )(block, *deps)


def _sum_blocks(name, gathered, n_blocks):
    r = gathered.shape[0] // n_blocks
    c = gathered.shape[1]

    def body(g_ref, o_ref):
        acc = g_ref[0:r, :]
        for b in range(1, n_blocks):
            acc = acc + g_ref[b * r:(b + 1) * r, :]
        o_ref[...] = acc

    return pl.pallas_call(body, name=name, out_shape=jax.ShapeDtypeStruct((r, c), F32))(gathered)


def _largest_tile(n, cap, mult):
    best = None
    for d in range(mult, min(n, cap) + 1, mult):
        if n % d == 0:
            best = d
    assert best is not None, (n, cap, mult)
    return best


def kernel(x, meta, g_pre_mix, w_in, b_gates, conf_dw_w, conf_dw_b, conf_ln_g, conf_ln_b, conf_w_pw, short_dw_w, short_w_out, w_o, g_post_mix, g_pre_mlp, w_up, w_down, g_post_mlp, loss_target, m_meta, m_g_pre_mix, m_w_in, m_b_gates, m_conf_dw_w, m_conf_dw_b, m_conf_ln_g, m_conf_ln_b, m_conf_w_pw, m_short_dw_w, m_short_w_out, m_w_o, m_g_post_mix, m_g_pre_mlp, m_w_up, m_w_down, m_g_post_mlp, v_meta, v_g_pre_mix, v_w_in, v_b_gates, v_conf_dw_w, v_conf_dw_b, v_conf_ln_g, v_conf_ln_b, v_conf_w_pw, v_short_dw_w, v_short_w_out, v_w_o, v_g_post_mix, v_g_pre_mlp, v_w_up, v_w_down, v_g_post_mlp):
    seq, d = x.shape[1], x.shape[2]
    t_real = seq + N_META
    t = -(-t_real // LANES) * LANES
    d_conf = conf_dw_b.shape[1]
    d_ff = w_up.shape[2] * N_CHIPS
    in_cols = w_in.shape[2] * N_CHIPS
    assert in_cols == 5 * d_conf + 2 * d and d == 2 * d_conf
    cw = d_conf
    core = lax.axis_index("c")
    chip = 2 * lax.axis_index("x") + lax.axis_index("y")

    tr = _largest_tile(t, 128, ROW_CHUNK)
    tm = t
    tn_in = _largest_tile(in_cols // N_CHIPS, 768, LANES)
    tn_d = _largest_tile(d, 1024, LANES)
    tn_h = _largest_tile(d, 512, LANES)
    tn_ff = _largest_tile(d_ff // N_CHIPS, 1024, LANES)
    tn_pw = _largest_tile(d // N_CHIPS, 512, LANES)

    big = [w_in[0], conf_w_pw[0], short_w_out[0], w_o[0], w_up[0], w_down[0]]
    chip_arr = chip.astype(jnp.int32).reshape(1)

    def cast(a, deps):
        return _into_slot(f"cast_w{a}", big[a], N_CHIPS, chip_arr, MXU_DTYPE, _largest_tile(big[a].shape[0], 256, 16), deps)

    small = [_into_slot(f"place_w{a}", w, N_CHIPS, chip_arr, F32, w.shape[0])
             for a, w in enumerate([meta, conf_dw_w[0], short_dw_w[0]])]
    sems_small, fly_small, tok_small = _gather_start("gather_start_small", small, [False] * 3, [[0, 1, 2]])
    sems_in, fly_in, tok_in = _gather_start("gather_start_in", [cast(0, [tok_small])], [True], [[0]])
    rest_groups = [[0, 1], [2], [3], [4]]
    sems_rest, fly_rest, tok_rest = _gather_start(
        "gather_start_rest", [cast(a, [tok_in]) for a in range(1, 6)], [True] * 5, rest_groups)

    def arrive(g, after):
        members = rest_groups[g]
        got = _gather_wait(f"gather_wait_rest{g}", [fly_rest[a] for a in members], [True] * len(members),
                           sems_rest[g], after)
        return _pass_halves(f"gather_pass_rest{g}", got)

    meta_g, wdw_g, w3_g = _gather_wait("gather_wait_small", fly_small, [False] * 3, sems_small[0], [tok_in])
    meta_full = jnp.transpose(meta_g, (1, 0, 2)).reshape(N_META, d)
    wdw = jnp.transpose(wdw_g, (1, 0, 2)).reshape(CONF_KERNEL, d_conf)
    w3 = jnp.transpose(w3_g, (1, 0, 2)).reshape(SHORT_KERNEL, d_conf)

    tail = jnp.zeros((t - t_real, d), F32)
    h0 = jnp.concatenate([meta_full, x[0], tail], axis=0)
    target = jnp.concatenate([jnp.zeros((N_META, d), F32), loss_target[0], tail], axis=0)

    def norm_in(i, rows, vecs):
        return [_rms_fwd(rows[0], vecs[0])], []

    (n_lp,) = _rowwise("norm_in", norm_in, [(h0, d, 0)], [g_pre_mix], [(d, MXU_DTYPE)], [], tr, deps=[tok_rest])
    (wg_in,) = _pass_halves("gather_pass_in", _gather_wait("gather_wait_in", fly_in, [True], sems_in[0], [n_lp]))
    proj =_matmul("proj", n_lp, wg_in, kind="nn", tm=tm, tn=tn_in, tk=d, out_dtypes=[F32])
    ac, c3 = _conv_fwd(proj, wdw, conf_dw_b, w3, d_conf)

    def ln_parts(ac_t, ln_g, ln_b):
        mu = jnp.mean(ac_t, axis=-1, keepdims=True)
        xc = ac_t - mu
        rstd = lax.rsqrt(jnp.mean(xc * xc, axis=-1, keepdims=True) + LN_EPS)
        xh = xc * rstd
        return xh, rstd, xh * ln_g + ln_b

    def branch_act(i, rows, vecs):
        ac_t, c3_t, bg_t = rows
        _, _, al = ln_parts(ac_t, vecs[0], vecs[1])
        return [al * _sigmoid(al), bg_t * c3_t], []

    a_act, s_lp = _rowwise("branch_act", branch_act, [(ac, cw, 0), (c3, cw, 0), (proj, cw, 2)],
                           [conf_ln_g, conf_ln_b], [(d_conf, MXU_DTYPE), (d_conf, MXU_DTYPE)], [], tr)
    wg_pw, wg_sout = arrive(0, [a_act])
    y_a = _matmul("y_a", a_act, wg_pw, kind="nn", tm=tm, tn=tn_pw, tk=d_conf, out_dtypes=[F32])
    y_b = _matmul("y_b", s_lp, wg_sout, kind="nn", tm=tm, tn=tn_pw, tk=d_conf, out_dtypes=[F32])

    gate_rows = [(proj, cw, 5), (proj, cw, 6), (proj, cw, 7), (proj, cw, 8)]

    def gates_of(rows, b):
        ga = _sigmoid(jnp.concatenate([rows[0], rows[1]], axis=1) + b[:, :d])
        gb = _sigmoid(jnp.concatenate([rows[2], rows[3]], axis=1) + b[:, d:])
        return ga, gb

    def gate(i, rows, vecs):
        ga, gb = gates_of(rows[2:], vecs[0])
        return [ga * rows[0] + gb * rows[1]], []

    (m_lp,) = _rowwise("gate", gate, [(y_a, d, 0), (y_b, d, 0)] + gate_rows, [b_gates], [(d, MXU_DTYPE)], [], tr)
    wg_o = arrive(1, [m_lp])[0].reshape(d, d)
    mix = _matmul("mix", m_lp, wg_o, kind="nn", tm=tm, tn=tn_d, tk=d, out_dtypes=[F32])

    def post_mix(i, rows, vecs):
        h1_t = rows[0] + _rms_fwd(rows[1], vecs[0])
        return [h1_t, _rms_fwd(h1_t, vecs[1])], []

    h1, n2_lp = _rowwise("post_mix", post_mix, [(h0, d, 0), (mix, d, 0)], [g_post_mix, g_pre_mlp],
                         [(d, F32), (d, MXU_DTYPE)], [], tr)
    (wg_up,) = arrive(2, [n2_lp])
    up, f_lp = _matmul("up", n2_lp, wg_up, kind="nn", tm=tm, tn=tn_h, tk=d, out_dtypes=[F32, MXU_DTYPE],
                       epilogue=lambda acc: (acc, jnp.square(jnp.maximum(acc, 0.0))))
    wg_down = arrive(3, [f_lp])[0].reshape(d_ff, d)
    dn = _matmul("down", f_lp, wg_down, kind="nn", tm=tm, tn=tn_h, tk=_largest_tile(d_ff, 2048, LANES),
                 out_dtypes=[F32])

    def head(row0, rows, vecs):
        h1_t, dn_t, tgt = rows
        y = h1_t + _rms_fwd(dn_t, vecs[0])
        row = row0 + lax.broadcasted_iota(jnp.int32, (ROW_CHUNK, 1), 0)
        err = jnp.where(jnp.logical_and(row >= N_META, row < t_real), y - tgt, 0.0)
        dy = err / d
        d_dn, dg = _rms_bwd(dn_t, vecs[0], dy)
        loss_rows = 0.5 * jnp.mean(err * err, axis=-1, keepdims=True)
        return [dy, d_dn], [dg, jnp.broadcast_to(loss_rows, (ROW_CHUNK, LANES))]

    dy, d_dn, dg_post_mlp, loss_vec = _rowwise(
        "head", head, [(h1, d, 0), (dn, d, 0), (target, d, 0)], [g_post_mlp], [(d, F32), (d, MXU_DTYPE)], [d, LANES], tr)

    core_arr = core.astype(jnp.int32).reshape(1)
    place = jnp.stack([chip, core]).astype(jnp.int32)
    in_flight = {}


    def tag(members):
        return "".join(str(a) for a in members)

    def swap_start(members, gws):
        halves = [gw.reshape(N_CHIPS, 2, gw.shape[1] // 2, gw.shape[2]) for gw in gws]
        *in_flight[members], token = _swap_start(f"swap_start{tag(members)}", halves)
        return token

    def exchange_start(members, after):
        sems, halves, lands = in_flight[members]
        halves, lands = _swap_wait(f"swap_wait{tag(members)}", halves, lands, sems, after)
        pairs = [_pair_sum(f"pair_sum{a}", h, q, core_arr, _largest_tile(q.shape[1], 256, 16))
                 for a, h, q in zip(members, halves, lands)]
        *in_flight[members], token = _exchange_start(f"exchange_start{tag(members)}", pairs)
        return token

    def reduce_finish(groups, after):
        waited = _exchange_wait("exchange_wait" + "_".join(tag(g) for g in groups), [in_flight[g] for g in groups], after)
        return {a: _sum_pieces(f"sum_pieces{a}", pair, land, place, _largest_tile(land.shape[1], 256, 16))
                for g, (pairs, lands) in zip(groups, waited) for a, pair, land in zip(g, pairs, lands)}

    d_up = _matmul("d_up", d_dn, wg_down, kind="nt", tm=tm, tn=tn_h, tk=d, out_dtypes=[MXU_DTYPE], extras=[up],
                   epilogue=lambda acc, up_t: (acc * (2.0 * jnp.maximum(up_t, 0.0)),))
    tk_t = t
    gw_down = _matmul("gw_down", f_lp, d_dn, kind="tn", tm=_largest_tile(d_ff, 2048, LANES), tn=tn_d, tk=tk_t,
                      out_dtypes=[F32])
    tok = swap_start((5,), [gw_down.reshape(N_CHIPS, d_ff // N_CHIPS, d)])
    d_n2 = _matmul("d_n2", d_up, wg_up, kind="nt", tm=tm, tn=tn_h, tk=d_ff // N_CHIPS, out_dtypes=[F32], deps=[tok])
    tok = exchange_start((5,), [d_n2])
    gw_up = _matmul("gw_up", n2_lp, d_up, kind="tn", tm=_largest_tile(d, 2048, LANES), tn=tn_ff, tk=tk_t,
                    out_dtypes=[F32], out_pieces=N_CHIPS, deps=[tok])
    tok = swap_start((4,), [gw_up])

    def bwd_mid(i, rows, vecs):
        dy_t, dn2_t, h1_t, mix_t = rows
        d_h1a, dg_pre_mlp = _rms_bwd(h1_t, vecs[1], dn2_t)
        d_h1 = dy_t + d_h1a
        d_mix, dg_post_mix = _rms_bwd(mix_t, vecs[0], d_h1)
        return [d_h1, d_mix], [dg_pre_mlp, dg_post_mix]

    d_h1, d_mix, dg_pre_mlp, dg_post_mix = _rowwise(
        "bwd_mid", bwd_mid, [(dy, d, 0), (d_n2, d, 0), (h1, d, 0), (mix, d, 0)], [g_post_mix, g_pre_mlp],
        [(d, F32), (d, MXU_DTYPE)], [d, d], tr, deps=[tok])
    d_m = _matmul("d_m", d_mix, wg_o, kind="nt", tm=tm, tn=tn_h, tk=d, out_dtypes=[F32])
    tok = exchange_start((4,), [d_m])
    gw_o = _matmul("gw_o", m_lp, d_mix, kind="tn", tm=_largest_tile(d, 2048, LANES), tn=tn_h, tk=tk_t, out_dtypes=[F32],
                   deps=[tok])

    def gate_bwd(i, rows, vecs):
        dm_t, ya_t, yb_t = rows[:3]
        ga, gb = gates_of(rows[3:], vecs[0])
        d_gpre = jnp.concatenate([dm_t * ya_t * ga * (1.0 - ga), dm_t * yb_t * gb * (1.0 - gb)], axis=1)
        return [dm_t * ga, dm_t * gb, d_gpre], [d_gpre]

    d_ya, d_yb, d_gpre, dg_b_gates = _rowwise(
        "gate_bwd", gate_bwd, [(d_m, d, 0), (y_a, d, 0), (y_b, d, 0)] + gate_rows, [b_gates],
        [(d, MXU_DTYPE), (d, MXU_DTYPE), (2 * d, MXU_DTYPE)], [2 * d], tr, deps=[tok])
    d_aact = _matmul("d_aact", d_ya, wg_pw, kind="nt", tm=tm, tn=d_conf // 2, tk=tn_pw, out_dtypes=[F32], deps=[tok])
    gw_pw = _matmul("gw_pw", a_act, d_ya, kind="tn", tm=_largest_tile(d_conf, 2048, LANES), tn=tn_pw, tk=tk_t,
                    out_dtypes=[F32], out_pieces=N_CHIPS)
    d_s = _matmul("d_s", d_yb, wg_sout, kind="nt", tm=tm, tn=d_conf // 2, tk=tn_pw, out_dtypes=[F32], deps=[tok])
    gw_sout = _matmul("gw_sout", s_lp, d_yb, kind="tn", tm=_largest_tile(d_conf, 2048, LANES), tn=tn_pw, tk=tk_t,
                      out_dtypes=[F32], out_pieces=N_CHIPS, deps=[tok])
    small_big = (3, 1, 2)
    tok = swap_start(small_big, [gw_o.reshape(N_CHIPS, d // N_CHIPS, d), gw_pw, gw_sout])

    def branch_bwd(i, rows, vecs):
        daact_t, ds_t, ac_t, c3_t, bg_t = rows
        xh, rstd, al = ln_parts(ac_t, vecs[0], vecs[1])
        sg = _sigmoid(al)
        d_al = daact_t * (sg * (1.0 + al * (1.0 - sg)))
        dxh = d_al * vecs[0]
        d_ac = rstd * (dxh - jnp.mean(dxh, axis=-1, keepdims=True) - xh * jnp.mean(dxh * xh, axis=-1, keepdims=True))
        return [d_ac, ds_t * bg_t, ds_t * c3_t], [d_al * xh, d_al, d_ac]

    d_ac, d_c3, d_bg, dg_ln_g, dg_ln_b, dg_dw_b = _rowwise(
        "branch_bwd", branch_bwd, [(d_aact, cw, 0), (d_s, cw, 0), (ac, cw, 0), (c3, cw, 0), (proj, cw, 2)],
        [conf_ln_g, conf_ln_b], [(d_conf, F32), (d_conf, F32), (d_conf, MXU_DTYPE)], [d_conf] * 3, tr, deps=[tok])
    d_av, d_ag, d_cg, d_v, dg_wdw, dg_w3 = _conv_bwd(proj, d_ac, d_c3, wdw, w3, d_conf)
    d_proj = jnp.concatenate([d_av, d_ag, d_bg, d_cg, d_v, d_gpre], axis=1)
    small_w = d_conf

    def pack(arrs):
        flat = jnp.concatenate([a.reshape(-1, small_w) for a in arrs], axis=0)
        return jnp.pad(flat, ((0, -flat.shape[0] % 8), (0, 0)))

    def reduce_small(tag, arrs, deps):
        assert all(a.size % small_w == 0 for a in arrs)
        gathered = _allgather_small(f"allgather_small_{tag}", pack(arrs), deps)
        first, r0 = [], 0
        for a in arrs:
            first.append(r0)
            r0 += a.size // small_w
        return _sum_blocks(f"sum_small_{tag}", gathered, 8), first

    def rows_of(total, r0, like):
        return total[r0:r0 + like.size // small_w].reshape(like.shape)

    tok = exchange_start(small_big, [d_proj])

    gw_in = _matmul("gw_in", n_lp, d_proj, kind="tn", tm=_largest_tile(d, 2048, LANES), tn=tn_in, tk=tk_t,
                    out_dtypes=[F32], out_pieces=N_CHIPS, deps=[tok])
    tok = swap_start((0,), [gw_in])
    d_n = _matmul("d_n", d_proj, wg_in, kind="nt", tm=tm, tn=tn_h, tk=in_cols // N_CHIPS, out_dtypes=[F32], deps=[tok])
    tok = exchange_start((0,), [d_n])

    early = [5, 4, 3, 1, 2]
    reduced = reduce_finish([(5,), (4,), small_big], [tok])
    join_sems, joining, tok = _join_start("join_start_early", [reduced[a] for a in early])

    def bwd_in(i, rows, vecs):
        d_h0a, dg = _rms_bwd(rows[2], vecs[0], rows[1])
        return [rows[0] + d_h0a], [dg]

    d_h0, dg_pre_mix = _rowwise("bwd_in", bwd_in, [(d_h1, d, 0), (d_n, d, 0), (h0, d, 0)], [g_pre_mix],
                                [(d, F32)], [d], tr, deps=[tok])
    grad_x = d_h0[N_META:t_real][None]

    big_m = [m_w_in, m_conf_w_pw, m_short_w_out, m_w_o, m_w_up, m_w_down]
    big_v = [v_w_in, v_conf_w_pw, v_short_w_out, v_w_o, v_w_up, v_w_down]
    big_res = {}

    def adam_group(members, joined):
        for a, j in zip(members, joined):
            big_res[a] = _adamw(f"adamw_big{a}", big[a], j.reshape(big[a].shape), big_m[a][0], big_v[a][0],
                                _largest_tile(big[a].shape[0], 256, 8))

    adam_group(early, _join_wait("join_wait_early", joining, join_sems, [d_h0]))
    reduced = reduce_finish([(0,)], [big_res[a][1] for a in early])
    join_sems, joining, tok = _join_start("join_start_late", [reduced[0]])
    rep_g = [dg_pre_mix, dg_b_gates, dg_dw_b, dg_ln_g, dg_ln_b, dg_post_mix, dg_pre_mlp, dg_post_mlp]
    rep_w = [g_pre_mix, b_gates, conf_dw_b, conf_ln_g, conf_ln_b, g_post_mix, g_pre_mlp, g_post_mlp]
    rep_m = [m_g_pre_mix, m_b_gates, m_conf_dw_b, m_conf_ln_g, m_conf_ln_b, m_g_post_mix, m_g_pre_mlp, m_g_post_mlp]
    rep_v = [v_g_pre_mix, v_b_gates, v_conf_dw_b, v_conf_ln_g, v_conf_ln_b, v_g_post_mix, v_g_pre_mlp, v_g_post_mlp]
    col_g = [d_h0[:N_META], dg_wdw, dg_w3]
    loss_row = jnp.tile(loss_vec, (1, small_w // LANES))
    total, first = reduce_small("all", rep_g + col_g + [loss_row], [tok])
    loss = total[first[-1], 0]
    rep_res = _small_adamw("adamw_rep", total, [(first[e], w, m, v) for e, (w, m, v) in enumerate(zip(rep_w, rep_m, rep_v))])
    col_res = {}
    for a, (w, g_part, m, v) in enumerate([(meta, col_g[0], m_meta, v_meta), (conf_dw_w[0], col_g[1], m_conf_dw_w[0], v_conf_dw_w[0]),
                                           (short_dw_w[0], col_g[2], m_short_dw_w[0], v_short_dw_w[0])]):
        g_full = rows_of(total, first[len(rep_g) + a], g_part)
        g_own = lax.dynamic_slice_in_dim(g_full, chip * w.shape[1], w.shape[1], axis=1)
        col_res[a] = _adamw(f"adamw_col{a}", w, g_own, m, v, w.shape[0])
    adam_group([0], _join_wait("join_wait_late", joining, join_sems, [rep_res[0][1]] + [col_res[a][1] for a in range(3)]))

    def leaf(q):
        r = lambda a: rep_res[a][q]
        b = lambda a: big_res[a][q][None]
        return [col_res[0][q], r(0), b(0), r(1), col_res[1][q][None], r(2), r(3), r(4), b(1), col_res[2][q][None], b(2),
                b(3), r(5), r(6), b(4), b(5), r(7)]

    return (loss, grad_x, *leaf(0), *leaf(1), *leaf(2), *leaf(3))
```

```python
import jax
import jax.numpy as jnp
from jax import lax
from jax.experimental import pallas as pl
from jax.experimental.pallas import tpu as pltpu

F32 = jnp.float32
BF16 = jnp.bfloat16
MXU_DTYPE = BF16
WIRE_DTYPE = BF16

N_META = 16
CONF_KERNEL = 31
SHORT_KERNEL = 3
CONV_PAD = 32
RMS_EPS = 1e-6
LN_EPS = 1e-5
ADAM_LR = 0.001
ADAM_B1 = 0.9
ADAM_B2 = 0.999
ADAM_EPS = 1e-08
ADAM_WD = 0.01
ADAM_STEP = 10

N_CHIPS = 4
MESH = pl.DeviceIdType.MESH
LANES = 128


def _sigmoid(z):
    return 1.0 / (1.0 + jnp.exp(-z))


ANY_SPEC = pl.BlockSpec(memory_space=pl.ANY)


def _matmul(name, a, b, *, kind, tm, tn, tk, out_dtypes, out_pieces=1, epilogue=None, extras=(), deps=()):
    pieces = b.shape[0] if b.ndim == 3 else 1
    if kind == "nn":
        m, kdim = a.shape
        n = b.shape[-1] * pieces
        dims = (((1,), (0,)), ((), ()))
        a_spec = pl.BlockSpec((tm, tk), lambda i, j, k: (i, k))
        if b.ndim == 2:
            b_spec = pl.BlockSpec((tk, tn), lambda i, j, k: (k, j))
        else:
            npp = b.shape[-1] // tn
            b_spec = pl.BlockSpec((None, tk, tn), lambda i, j, k: (j // npp, k, j % npp))
    elif kind == "nt":
        m, kdim = a.shape
        n = b.shape[-2]
        dims = (((1,), (1,)), ((), ()))
        a_spec = pl.BlockSpec((tm, tk), lambda i, j, k: (i, k))
        if b.ndim == 2:
            b_spec = pl.BlockSpec((tn, tk), lambda i, j, k: (j, k))
        else:
            kpp = b.shape[-1] // tk
            b_spec = pl.BlockSpec((None, tn, tk), lambda i, j, k: (k // kpp, j, k % kpp))
    else:
        kdim, m = a.shape
        n = b.shape[-1]
        dims = (((0,), (0,)), ((), ()))
        a_spec = pl.BlockSpec((tk, tm), lambda i, j, k: (k, i))
        b_spec = pl.BlockSpec((tk, tn), lambda i, j, k: (k, j))
    assert m % tm == 0 and n % tn == 0 and kdim % tk == 0, (name, m, n, kdim, tm, tn, tk)
    nk = kdim // tk
    if out_pieces == 1:
        out_shape = (m, n)
        out_spec = pl.BlockSpec((tm, tn), lambda i, j, k: (i, j))
    else:
        onpp = n // out_pieces // tn
        out_shape = (out_pieces, m, n // out_pieces)
        out_spec = pl.BlockSpec((None, tm, tn), lambda i, j, k: (j // onpp, i, j % onpp))
    n_ex, n_out, n_in = len(extras), len(out_dtypes), len(extras) + len(deps)
    if epilogue is None:
        epilogue = lambda acc: (acc,)

    def body(a_ref, b_ref, *rest):
        ex_refs, o_refs = rest[:n_ex], rest[n_in:n_in + n_out]
        prod = lax.dot_general(a_ref[...], b_ref[...], dims, preferred_element_type=F32)

        def finish(acc):
            tiles = epilogue(acc, *[r[...] for r in ex_refs])
            for o_ref, t in zip(o_refs, tiles):
                o_ref[...] = t.astype(o_ref.dtype)

        if nk == 1:
            finish(prod)
        else:
            acc_ref = rest[n_in + n_out]
            k = pl.program_id(2)

            @pl.when(k == 0)
            def _():
                acc_ref[...] = prod

            @pl.when(jnp.logical_and(k > 0, k < nk - 1))
            def _():
                acc_ref[...] += prod

            @pl.when(k == nk - 1)
            def _():
                finish(acc_ref[...] + prod)

    ex_specs = [pl.BlockSpec((tm, tn), lambda i, j, k: (i, j)) for _ in extras]
    res = pl.pallas_call(
        body,
        name=name,
        grid=(m // tm, n // tn, nk),
        in_specs=[a_spec, b_spec, *ex_specs] + [ANY_SPEC] * len(deps),
        out_specs=[out_spec] * n_out,
        out_shape=[jax.ShapeDtypeStruct(out_shape, d) for d in out_dtypes],
        scratch_shapes=[pltpu.VMEM((tm, tn), F32)] if nk > 1 else [],
        compiler_params=pltpu.CompilerParams(dimension_semantics=("parallel", "parallel", "arbitrary")),
    )(a, b, *extras, *deps)
    return res[0] if n_out == 1 else res


ROW_CHUNK = 16
SUBLANES = 8


def _rowwise(name, fn, rows, vecs, outs, sums, tr, deps=()):
    t = rows[0][0].shape[0]
    assert t % tr == 0 and tr % ROW_CHUNK == 0
    n_r, n_v, n_o, n_s = len(rows), len(vecs), len(outs), len(sums)
    n_in = n_r + n_v + len(deps)
    n_steps = t // tr

    def body(*refs):
        r_in, v_in = refs[:n_r], refs[n_r:n_r + n_v]
        o_refs = refs[n_in:n_in + n_o]
        s_refs = refs[n_in + n_o:n_in + n_o + n_s]
        acc_refs = refs[n_in + n_o + n_s:]
        i = pl.program_id(0)

        @pl.when(i == 0)
        def _():
            for acc_ref in acc_refs:
                acc_ref[...] = jnp.zeros(acc_ref.shape, F32)

        def chunk(ci):
            r0 = ci * ROW_CHUNK
            sl = pl.ds(r0, ROW_CHUNK)
            o_tiles, s_tiles = fn(i * tr + r0, [r[sl, :] for r in r_in], [v[...] for v in v_in])
            for o_ref, tile in zip(o_refs, o_tiles):
                o_ref[sl, :] = tile.astype(o_ref.dtype)
            for acc_ref, tile in zip(acc_refs, s_tiles):
                part = tile[0:SUBLANES]
                for s in range(1, ROW_CHUNK // SUBLANES):
                    part = part + tile[s * SUBLANES:(s + 1) * SUBLANES]
                acc_ref[...] += part

        for ci in range(tr // ROW_CHUNK):
            chunk(ci)

        @pl.when(i == n_steps - 1)
        def _():
            for s_ref, acc_ref in zip(s_refs, acc_refs):
                s_ref[...] = jnp.sum(acc_ref[...], axis=0, keepdims=True)

    def row_spec(width, blk):
        return pl.BlockSpec((tr, width), lambda i: (i, blk))

    res = pl.pallas_call(
        body,
        name=name,
        grid=(t // tr,),
        in_specs=[row_spec(w, blk) for _, w, blk in rows]
        + [pl.BlockSpec(v.shape, lambda i: (0, 0)) for v in vecs] + [ANY_SPEC] * len(deps),
        out_specs=[pl.BlockSpec((tr, c), lambda i: (i, 0)) for c, _ in outs]
        + [pl.BlockSpec((1, c), lambda i: (0, 0)) for c in sums],
        out_shape=[jax.ShapeDtypeStruct((t, c), d) for c, d in outs]
        + [jax.ShapeDtypeStruct((1, c), F32) for c in sums],
        scratch_shapes=[pltpu.VMEM((SUBLANES, c), F32) for c in sums],
        compiler_params=pltpu.CompilerParams(dimension_semantics=("arbitrary",)),
    )(*[r[0] for r in rows], *vecs, *deps)
    return res


def _rms_fwd(x, g):
    r = lax.rsqrt(jnp.mean(x * x, axis=-1, keepdims=True) + RMS_EPS)
    return x * r * g


def _rms_bwd(x, g, dy):
    r = lax.rsqrt(jnp.mean(x * x, axis=-1, keepdims=True) + RMS_EPS)
    xn = x * r
    dxn = dy * g
    dx = r * (dxn - xn * jnp.mean(dxn * xn, axis=-1, keepdims=True))
    return dx, dy * xn


CONV_ROWS = 64
CONV_LANES = 128


def _conv_fwd(proj, wdw, bdw, w3, d_conf):
    t = proj.shape[0]
    cl = CONV_LANES
    nb = d_conf // cl
    nchunk = t // CONV_ROWS
    assert t % CONV_ROWS == 0

    def body(av_ref, ag_ref, cg_ref, v_ref, wdw_ref, bdw_ref, w3_ref, ac_ref, c3_ref, apad, cpad):
        zeros = jnp.zeros((CONV_PAD, cl), F32)
        apad[0:CONV_PAD, :] = zeros
        cpad[0:CONV_PAD, :] = zeros
        apad[CONV_PAD:, :] = av_ref[...] * _sigmoid(ag_ref[...])
        cpad[CONV_PAD:, :] = cg_ref[...] * v_ref[...]

        def chunk(ci, carry):
            base = pl.multiple_of(ci * CONV_ROWS, 8)
            acc = jnp.zeros((CONV_ROWS, cl), F32) + bdw_ref[...]
            for k in range(CONF_KERNEL):
                off = CONV_PAD - (CONF_KERNEL - 1) + k
                acc = acc + apad[pl.ds(base + off, CONV_ROWS), :] * wdw_ref[k:k + 1, :]
            ac_ref[pl.ds(base, CONV_ROWS), :] = acc
            acc3 = jnp.zeros((CONV_ROWS, cl), F32)
            for k in range(SHORT_KERNEL):
                off = CONV_PAD - (SHORT_KERNEL - 1) + k
                acc3 = acc3 + cpad[pl.ds(base + off, CONV_ROWS), :] * w3_ref[k:k + 1, :]
            c3_ref[pl.ds(base, CONV_ROWS), :] = acc3
            return carry

        lax.fori_loop(0, nchunk, chunk, 0)

    def col(blk0):
        return pl.BlockSpec((t, cl), lambda j: (0, blk0 + j))

    return pl.pallas_call(
        body,
        name="conv_fwd",
        grid=(nb,),
        in_specs=[col(0), col(nb), col(3 * nb), col(4 * nb),
                  pl.BlockSpec((CONF_KERNEL, cl), lambda j: (0, j)),
                  pl.BlockSpec((1, cl), lambda j: (0, j)),
                  pl.BlockSpec((SHORT_KERNEL, cl), lambda j: (0, j))],
        out_specs=[pl.BlockSpec((t, cl), lambda j: (0, j))] * 2,
        out_shape=[jax.ShapeDtypeStruct((t, d_conf), F32)] * 2,
        scratch_shapes=[pltpu.VMEM((t + CONV_PAD, cl), F32)] * 2,
        compiler_params=pltpu.CompilerParams(dimension_semantics=("parallel",)),
    )(proj, proj, proj, proj, wdw, bdw, w3)


def _conv_bwd(proj, d_ac, d_c3, wdw, w3, d_conf):
    t = proj.shape[0]
    cl = CONV_LANES
    nb = d_conf // cl
    nchunk = t // CONV_ROWS
    nsub = CONV_ROWS // 8

    def fold(p):
        r = p[0:8]
        for s in range(1, nsub):
            r = r + p[8 * s:8 * s + 8]
        return r

    def body(av_ref, ag_ref, cg_ref, v_ref, dac_ref, dc3_ref, wdw_ref, w3_ref,
             dav_ref, dag_ref, dcg_ref, dv_ref, dwdw_ref, dw3_ref, apad, cpad, dapad, dcpad):
        zeros = jnp.zeros((CONV_PAD, cl), F32)
        apad[0:CONV_PAD, :] = zeros
        cpad[0:CONV_PAD, :] = zeros
        apad[CONV_PAD:, :] = av_ref[...] * _sigmoid(ag_ref[...])
        cpad[CONV_PAD:, :] = cg_ref[...] * v_ref[...]
        dapad[0:t, :] = dac_ref[...]
        dcpad[0:t, :] = dc3_ref[...]
        dapad[t:, :] = zeros
        dcpad[t:, :] = zeros

        def chunk(ci, accs):
            base = pl.multiple_of(ci * CONV_ROWS, 8)
            rows = pl.ds(base, CONV_ROWS)
            da = jnp.zeros((CONV_ROWS, cl), F32)
            for k in range(CONF_KERNEL):
                da = da + dapad[pl.ds(base + (CONF_KERNEL - 1 - k), CONV_ROWS), :] * wdw_ref[k:k + 1, :]
            dcv = jnp.zeros((CONV_ROWS, cl), F32)
            for k in range(SHORT_KERNEL):
                dcv = dcv + dcpad[pl.ds(base + (SHORT_KERNEL - 1 - k), CONV_ROWS), :] * w3_ref[k:k + 1, :]
            av, sg = av_ref[rows, :], _sigmoid(ag_ref[rows, :])
            dav_ref[rows, :] = (da * sg).astype(dav_ref.dtype)
            dag_ref[rows, :] = (da * av * sg * (1.0 - sg)).astype(dag_ref.dtype)
            dcg_ref[rows, :] = (dcv * v_ref[rows, :]).astype(dcg_ref.dtype)
            dv_ref[rows, :] = (dcv * cg_ref[rows, :]).astype(dv_ref.dtype)
            d_out, d_out3 = dac_ref[rows, :], dc3_ref[rows, :]
            new = []
            for k in range(CONF_KERNEL):
                off = CONV_PAD - (CONF_KERNEL - 1) + k
                new.append(accs[k] + fold(d_out * apad[pl.ds(base + off, CONV_ROWS), :]))
            for k in range(SHORT_KERNEL):
                off = CONV_PAD - (SHORT_KERNEL - 1) + k
                new.append(accs[CONF_KERNEL + k] + fold(d_out3 * cpad[pl.ds(base + off, CONV_ROWS), :]))
            return tuple(new)

        init = tuple(jnp.zeros((8, cl), F32) for _ in range(CONF_KERNEL + SHORT_KERNEL))
        accs = lax.fori_loop(0, nchunk, chunk, init)
        for k in range(CONF_KERNEL):
            dwdw_ref[k:k + 1, :] = jnp.sum(accs[k], axis=0, keepdims=True)
        for k in range(SHORT_KERNEL):
            dw3_ref[k:k + 1, :] = jnp.sum(accs[CONF_KERNEL + k], axis=0, keepdims=True)

    def col(blk0):
        return pl.BlockSpec((t, cl), lambda j: (0, blk0 + j))

    own = pl.BlockSpec((t, cl), lambda j: (0, j))
    return pl.pallas_call(
        body,
        name="conv_bwd",
        grid=(nb,),
        in_specs=[col(0), col(nb), col(3 * nb), col(4 * nb), own, own,
                  pl.BlockSpec((CONF_KERNEL, cl), lambda j: (0, j)),
                  pl.BlockSpec((SHORT_KERNEL, cl), lambda j: (0, j))],
        out_specs=[own] * 4 + [pl.BlockSpec((CONF_KERNEL, cl), lambda j: (0, j)),
                               pl.BlockSpec((SHORT_KERNEL, cl), lambda j: (0, j))],
        out_shape=[jax.ShapeDtypeStruct((t, d_conf), MXU_DTYPE)] * 4
        + [jax.ShapeDtypeStruct((CONF_KERNEL, d_conf), F32), jax.ShapeDtypeStruct((SHORT_KERNEL, d_conf), F32)],
        scratch_shapes=[pltpu.VMEM((t + CONV_PAD, cl), F32)] * 4,
        compiler_params=pltpu.CompilerParams(dimension_semantics=("parallel",)),
    )(proj, proj, proj, proj, d_ac, d_c3, wdw, w3)


def _elementwise(name, fn, ins, out_dtypes, tr, deps=()):
    ins = [(a, ()) if not isinstance(a, tuple) else a for a in ins]
    r, c = ins[0][0].shape[-2:]
    assert r % tr == 0, (name, r, tr)
    n_in = len(ins)

    def body(*refs):
        tiles = fn(*[x[...] for x in refs[:n_in]])
        for o_ref, tile in zip(refs[n_in + len(deps):], tiles):
            o_ref[...] = tile.astype(o_ref.dtype)

    def spec(lead):
        return pl.BlockSpec((None,) * len(lead) + (tr, c), lambda i: (*lead, i, 0))

    res = pl.pallas_call(
        body,
        name=name,
        grid=(r // tr,),
        in_specs=[spec(lead) for _, lead in ins] + [ANY_SPEC] * len(deps),
        out_specs=[pl.BlockSpec((tr, c), lambda i: (i, 0))] * len(out_dtypes),
        out_shape=[jax.ShapeDtypeStruct((r, c), d) for d in out_dtypes],
        compiler_params=pltpu.CompilerParams(dimension_semantics=("parallel",)),
    )(*[a for a, _ in ins], *deps)
    return res


def _adamw_tiles(w, g, m, v):
    m = ADAM_B1 * m + (1.0 - ADAM_B1) * g
    v = ADAM_B2 * v + (1.0 - ADAM_B2) * jnp.square(g)
    m_hat = m / (1.0 - ADAM_B1 ** ADAM_STEP)
    v_hat = v / (1.0 - ADAM_B2 ** ADAM_STEP)
    delta = -ADAM_LR * (m_hat / (jnp.sqrt(v_hat) + ADAM_EPS) + ADAM_WD * w)
    return g, delta, m, v


def _adamw(name, w, g, m, v, tr, deps=()):
    shape = w.shape
    flat = [a.reshape(shape[-2:]) if a.ndim > 2 else a for a in (w, g, m, v)]
    res = _elementwise(name, _adamw_tiles, flat, [F32] * 4, tr, deps)
    return [a.reshape(shape) for a in res]


def _small_adamw(name, total, entries):
    sw = total.shape[1]
    n = len(entries)

    def body(total_ref, *refs):
        ins, outs = refs[:3 * n], refs[3 * n:]
        for e, (row0, w, _, _) in enumerate(entries):
            for q in range(w.shape[1] // sw):
                cols = slice(q * sw, (q + 1) * sw)
                tiles = _adamw_tiles(ins[3 * e][:, cols], total_ref[row0 + q:row0 + q + 1, :],
                                     ins[3 * e + 1][:, cols], ins[3 * e + 2][:, cols])
                for o_ref, tile in zip(outs[4 * e:4 * e + 4], tiles):
                    o_ref[:, cols] = tile

    res = pl.pallas_call(
        body,
        name=name,
        out_shape=[jax.ShapeDtypeStruct(w.shape, F32) for _, w, _, _ in entries for _ in range(4)],
    )(total, *[a for _, w, m, v in entries for a in (w, m, v)])
    return [list(res[4 * e:4 * e + 4]) for e in range(n)]


def _pair_sum(name, p, q, core, tr):
    n_p, _, hr, c = p.shape
    assert hr % tr == 0

    def body(core_ref, p_ref, q_ref, o_ref):
        o_ref[...] = (p_ref[...] + q_ref[...].astype(F32)).astype(o_ref.dtype)

    return pl.pallas_call(
        body,
        name=name,
        grid_spec=pltpu.PrefetchScalarGridSpec(
            num_scalar_prefetch=1,
            grid=(n_p, hr // tr),
            in_specs=[pl.BlockSpec((None, None, tr, c), lambda a, i, core_ref: (a, core_ref[0], i, 0)),
                      pl.BlockSpec((None, tr, c), lambda a, i, core_ref: (a, i, 0))],
            out_specs=pl.BlockSpec((None, tr, c), lambda a, i, core_ref: (a, i, 0)),
        ),
        out_shape=jax.ShapeDtypeStruct((n_p, hr, c), WIRE_DTYPE),
        compiler_params=pltpu.CompilerParams(dimension_semantics=("parallel", "parallel")),
    )(core, p, q)


def _into_slot(name, w, slots, slot, dtype, tr, deps=()):
    r, c = w.shape
    assert r % tr == 0

    def body(slot_ref, w_ref, *rest):
        o_ref = rest[len(deps)]
        o_ref[...] = w_ref[...].astype(o_ref.dtype)

    return pl.pallas_call(
        body,
        name=name,
        grid_spec=pltpu.PrefetchScalarGridSpec(
            num_scalar_prefetch=1,
            grid=(r // tr,),
            in_specs=[pl.BlockSpec((tr, c), lambda i, slot_ref: (i, 0))] + [ANY_SPEC] * len(deps),
            out_specs=pl.BlockSpec((None, tr, c), lambda i, slot_ref: (slot_ref[0], i, 0)),
        ),
        out_shape=jax.ShapeDtypeStruct((slots, r, c), dtype),
        compiler_params=pltpu.CompilerParams(dimension_semantics=("parallel",)),
    )(slot, w, *deps)


def _sum_pieces(name, own, rb, place, tr):
    n_p, hr, c = rb.shape
    assert hr % tr == 0

    def body(place_ref, own_ref, *refs):
        chip = place_ref[0]
        acc = None
        for k in range(n_p):
            tile = jnp.where(chip == k, own_ref[...], refs[k][...]).astype(F32)
            acc = tile if acc is None else acc + tile
        refs[n_p][...] = acc

    def landed(k):
        return pl.BlockSpec((None, tr, c), lambda i, place_ref: (jnp.where(place_ref[0] == k, (k + 1) % n_p, k), i, 0))

    return pl.pallas_call(
        body,
        name=name,
        grid_spec=pltpu.PrefetchScalarGridSpec(
            num_scalar_prefetch=1,
            grid=(hr // tr,),
            in_specs=[pl.BlockSpec((None, tr, c), lambda i, place_ref: (place_ref[0], i, 0))]
            + [landed(k) for k in range(n_p)],
            out_specs=pl.BlockSpec((None, tr, c), lambda i, place_ref: (place_ref[1], i, 0)),
        ),
        out_shape=jax.ShapeDtypeStruct((2, hr, c), F32),
        compiler_params=pltpu.CompilerParams(dimension_semantics=("parallel",)),
    )(place, own, *([rb] * n_p))


HBM_SPEC = pl.BlockSpec(memory_space=pl.ANY)


def _place():
    x, y, c = lax.axis_index("x"), lax.axis_index("y"), lax.axis_index("c")
    chips = [(1 - x, y), (x, 1 - y), (1 - x, 1 - y)]
    return x, y, c, chips


HBM_ONLY = pl.BlockSpec(memory_space=pltpu.HBM)
SEM_SPEC = pl.BlockSpec(memory_space=pltpu.SEMAPHORE)
DATAFLOW = pltpu.SideEffectType.DATAFLOW_SIDE_EFFECTING


def _in_hbm(a):
    return pltpu.with_memory_space_constraint(a, pltpu.HBM)


def _shard_part(ref, is_split, slot, h):
    if not is_split:
        return ref.at[slot]
    hr = ref.shape[1] // 2
    return ref.at[slot, pl.ds(h * hr, hr), :]


TOKEN = jax.ShapeDtypeStruct((8, LANES), F32)
VMEM_SPEC = pl.BlockSpec(memory_space=pltpu.VMEM)


def _gather_start(name, bufs, split, groups, deps=()):
    n, ng = len(bufs), len(groups)

    def body(*refs):
        ins, sems = refs[:n], refs[n + len(deps):n + len(deps) + 2 * ng]
        refs[-1][...] = jnp.zeros(TOKEN.shape, TOKEN.dtype)
        x, y, c, chips = _place()
        me = 2 * x + y
        for g, members in enumerate(groups):
            for s, a in enumerate(members):
                mine = _shard_part(ins[a], split[a], me, c)
                for j, chip in enumerate(chips):
                    pltpu.make_async_remote_copy(
                        src_ref=mine, dst_ref=mine, send_sem=sems[2 * g].at[3 * s + j], recv_sem=sems[2 * g + 1].at[3 * s + j],
                        device_id=(*chip, c), device_id_type=MESH).start()

    res = pl.pallas_call(
        body,
        name=name,
        in_specs=[HBM_ONLY] * n + [ANY_SPEC] * len(deps),
        out_specs=[SEM_SPEC] * (2 * ng) + [HBM_ONLY] * n + [VMEM_SPEC],
        out_shape=[pltpu.SemaphoreType.DMA((3 * len(members),)) for members in groups for _ in range(2)]
        + [pltpu.HBM(b.shape, b.dtype) for b in bufs] + [TOKEN],
        input_output_aliases={a: 2 * ng + a for a in range(n)},
        compiler_params=pltpu.CompilerParams(has_side_effects=DATAFLOW),
    )(*[_in_hbm(b) for b in bufs], *deps)
    return [(res[2 * g], res[2 * g + 1]) for g in range(ng)], list(res[2 * ng:2 * ng + n]), res[-1]


def _gather_wait(name, bufs, split, sems, after):
    n = len(bufs)

    def body(*refs):
        ins, send_sems, recv_sems = refs[:n], refs[n], refs[n + 1]
        x, y, c, chips = _place()
        me = 2 * x + y
        for s in range(n):
            for j, chip in enumerate(chips):
                copy = pltpu.make_async_remote_copy(
                    src_ref=_shard_part(ins[s], split[s], me, c),
                    dst_ref=_shard_part(ins[s], split[s], 2 * chip[0] + chip[1], c),
                    send_sem=send_sems.at[3 * s + j], recv_sem=recv_sems.at[3 * s + j],
                    device_id=(*chip, c), device_id_type=MESH)
                copy.wait_send()
                copy.wait_recv()

    res = pl.pallas_call(
        body,
        name=name,
        in_specs=[HBM_ONLY] * n + [SEM_SPEC, SEM_SPEC] + [ANY_SPEC] * len(after),
        out_specs=[HBM_ONLY] * n,
        out_shape=[pltpu.HBM(b.shape, b.dtype) for b in bufs],
        input_output_aliases={a: a for a in range(n)},
        compiler_params=pltpu.CompilerParams(has_side_effects=DATAFLOW),
    )(*bufs, *sems, *after)
    return list(res)


def _pass_halves(name, bufs):
    n = len(bufs)

    def body(*refs):
        outs = refs[n:2 * n]
        send_sems, recv_sems = refs[2 * n:]
        x, y, c, chips = _place()
        sibling = (x, y, 1 - c)

        def copy(a, j, h):
            blk = _shard_part(outs[a], True, 2 * chips[j][0] + chips[j][1], h)
            return pltpu.make_async_remote_copy(
                src_ref=blk, dst_ref=blk, send_sem=send_sems.at[3 * a + j], recv_sem=recv_sems.at[3 * a + j],
                device_id=sibling, device_id_type=MESH)

        sends = [copy(a, j, c) for a in range(n) for j in range(3)]
        for cp in sends:
            cp.start()
        for a in range(n):
            for j in range(3):
                copy(a, j, 1 - c).wait_recv()
        for cp in sends:
            cp.wait_send()

    res = pl.pallas_call(
        body,
        name=name,
        in_specs=[HBM_SPEC] * n,
        out_specs=[HBM_SPEC] * n,
        out_shape=[jax.ShapeDtypeStruct(b.shape, b.dtype) for b in bufs],
        input_output_aliases={a: a for a in range(n)},
        scratch_shapes=[pltpu.SemaphoreType.DMA((3 * n,)), pltpu.SemaphoreType.DMA((3 * n,))],
    )(*bufs)
    return list(res)


def _exchange_start(name, pairs):
    n = len(pairs)

    def body(*refs):
        pair_refs, land_refs, send_sems, recv_sems = refs[:n], refs[n:2 * n], refs[2 * n], refs[2 * n + 1]
        x, y, c, chips = _place()
        me = 2 * x + y
        for a in range(n):
            for j, chip in enumerate(chips):
                pltpu.make_async_remote_copy(
                    src_ref=pair_refs[a].at[2 * chip[0] + chip[1]], dst_ref=land_refs[a].at[me],
                    send_sem=send_sems.at[3 * a + j], recv_sem=recv_sems.at[3 * a + j],
                    device_id=(*chip, c), device_id_type=MESH).start()
        refs[-1][...] = jnp.zeros(TOKEN.shape, TOKEN.dtype)

    res = pl.pallas_call(
        body,
        name=name,
        in_specs=[HBM_ONLY] * (2 * n),
        out_specs=[SEM_SPEC, SEM_SPEC] + [HBM_ONLY] * (2 * n) + [VMEM_SPEC],
        out_shape=[pltpu.SemaphoreType.DMA((3 * n,)), pltpu.SemaphoreType.DMA((3 * n,))]
        + [pltpu.HBM(p.shape, p.dtype) for p in pairs] * 2 + [TOKEN],
        input_output_aliases={a: 2 + a for a in range(2 * n)},
        compiler_params=pltpu.CompilerParams(has_side_effects=DATAFLOW),
    )(*[_in_hbm(p) for p in pairs], *[_in_hbm(lax.empty(p.shape, p.dtype)) for p in pairs])
    return (res[0], res[1]), list(res[2:2 + n]), list(res[2 + n:2 + 2 * n]), res[-1]


def _exchange_wait(name, groups, after):
    sizes = [len(pairs) for _, pairs, _ in groups]
    n_buf = 2 * sum(sizes)

    def body(*refs):
        x, y, c, chips = _place()
        at_buf, at_sem = 0, n_buf
        for n in sizes:
            pair_refs, land_refs = refs[at_buf:at_buf + n], refs[at_buf + n:at_buf + 2 * n]
            send_sems, recv_sems = refs[at_sem], refs[at_sem + 1]
            at_buf, at_sem = at_buf + 2 * n, at_sem + 2
            for a in range(n):
                for j, chip in enumerate(chips):
                    k = 2 * chip[0] + chip[1]
                    copy = pltpu.make_async_remote_copy(
                        src_ref=pair_refs[a].at[k], dst_ref=land_refs[a].at[k],
                        send_sem=send_sems.at[3 * a + j], recv_sem=recv_sems.at[3 * a + j],
                        device_id=(*chip, c), device_id_type=MESH)
                    copy.wait_send()
                    copy.wait_recv()

    bufs = [b for _, pairs, lands in groups for b in (*pairs, *lands)]
    sems = [s for group_sems, _, _ in groups for s in group_sems]
    res = pl.pallas_call(
        body,
        name=name,
        in_specs=[HBM_ONLY] * n_buf + [SEM_SPEC] * len(sems) + [ANY_SPEC] * len(after),
        out_specs=[HBM_ONLY] * n_buf,
        out_shape=[pltpu.HBM(b.shape, b.dtype) for b in bufs],
        input_output_aliases={a: a for a in range(n_buf)},
        compiler_params=pltpu.CompilerParams(has_side_effects=DATAFLOW),
    )(*bufs, *sems, *after)
    out, at = [], 0
    for n in sizes:
        out.append((list(res[at:at + n]), list(res[at + n:at + 2 * n])))
        at += 2 * n
    return out


def _swap_start(name, halves):
    n = len(halves)
    n_p = halves[0].shape[0]
    land_shapes = [(n_p, *h.shape[2:]) for h in halves]

    def body(*refs):
        half_refs, land_refs, send_sems, recv_sems = refs[:n], refs[n:2 * n], refs[2 * n], refs[2 * n + 1]
        x, y, c, _ = _place()
        for a in range(n):
            for p in range(n_p):
                pltpu.make_async_remote_copy(
                    src_ref=half_refs[a].at[p, 1 - c], dst_ref=land_refs[a].at[p],
                    send_sem=send_sems.at[n_p * a + p], recv_sem=recv_sems.at[n_p * a + p],
                    device_id=(x, y, 1 - c), device_id_type=MESH).start()
        refs[-1][...] = jnp.zeros(TOKEN.shape, TOKEN.dtype)

    res = pl.pallas_call(
        body,
        name=name,
        in_specs=[HBM_ONLY] * (2 * n),
        out_specs=[SEM_SPEC, SEM_SPEC] + [HBM_ONLY] * (2 * n) + [VMEM_SPEC],
        out_shape=[pltpu.SemaphoreType.DMA((n_p * n,)), pltpu.SemaphoreType.DMA((n_p * n,))]
        + [pltpu.HBM(h.shape, h.dtype) for h in halves]
        + [pltpu.HBM(s, h.dtype) for s, h in zip(land_shapes, halves)] + [TOKEN],
        input_output_aliases={a: 2 + a for a in range(2 * n)},
        compiler_params=pltpu.CompilerParams(has_side_effects=DATAFLOW),
    )(*[_in_hbm(h) for h in halves], *[_in_hbm(lax.empty(s, h.dtype)) for s, h in zip(land_shapes, halves)])
    return (res[0], res[1]), list(res[2:2 + n]), list(res[2 + n:2 + 2 * n]), res[-1]


def _swap_wait(name, halves, lands, sems, after):
    n = len(halves)
    n_p = halves[0].shape[0]

    def body(*refs):
        half_refs, land_refs, send_sems, recv_sems = refs[:n], refs[n:2 * n], refs[2 * n], refs[2 * n + 1]
        x, y, c, _ = _place()
        for a in range(n):
            for p in range(n_p):
                copy = pltpu.make_async_remote_copy(
                    src_ref=half_refs[a].at[p, 1 - c], dst_ref=land_refs[a].at[p],
                    send_sem=send_sems.at[n_p * a + p], recv_sem=recv_sems.at[n_p * a + p],
                    device_id=(x, y, 1 - c), device_id_type=MESH)
                copy.wait_send()
                copy.wait_recv()

    res = pl.pallas_call(
        body,
        name=name,
        in_specs=[HBM_ONLY] * (2 * n) + [SEM_SPEC, SEM_SPEC] + [ANY_SPEC] * len(after),
        out_specs=[HBM_ONLY] * (2 * n),
        out_shape=[pltpu.HBM(b.shape, b.dtype) for b in (*halves, *lands)],
        input_output_aliases={a: a for a in range(2 * n)},
        compiler_params=pltpu.CompilerParams(has_side_effects=DATAFLOW),
    )(*halves, *lands, *sems, *after)
    return list(res[:n]), list(res[n:])


def _join_start(name, bufs):
    n = len(bufs)

    def body(*refs):
        ins, send_sems, recv_sems = refs[:n], refs[n], refs[n + 1]
        x, y, c, _ = _place()
        for a in range(n):
            pltpu.make_async_remote_copy(
                src_ref=ins[a].at[c], dst_ref=ins[a].at[c], send_sem=send_sems.at[a], recv_sem=recv_sems.at[a],
                device_id=(x, y, 1 - c), device_id_type=MESH).start()
        refs[-1][...] = jnp.zeros(TOKEN.shape, TOKEN.dtype)

    res = pl.pallas_call(
        body,
        name=name,
        in_specs=[HBM_ONLY] * n,
        out_specs=[SEM_SPEC, SEM_SPEC] + [HBM_ONLY] * n + [VMEM_SPEC],
        out_shape=[pltpu.SemaphoreType.DMA((n,)), pltpu.SemaphoreType.DMA((n,))]
        + [pltpu.HBM(b.shape, b.dtype) for b in bufs] + [TOKEN],
        input_output_aliases={a: 2 + a for a in range(n)},
        compiler_params=pltpu.CompilerParams(has_side_effects=DATAFLOW),
    )(*[_in_hbm(b) for b in bufs])
    return (res[0], res[1]), list(res[2:2 + n]), res[-1]


def _join_wait(name, bufs, sems, after):
    n = len(bufs)

    def body(*refs):
        ins, send_sems, recv_sems = refs[:n], refs[n], refs[n + 1]
        x, y, c, _ = _place()
        for a in range(n):
            copy = pltpu.make_async_remote_copy(
                src_ref=ins[a].at[c], dst_ref=ins[a].at[1 - c], send_sem=send_sems.at[a], recv_sem=recv_sems.at[a],
                device_id=(x, y, 1 - c), device_id_type=MESH)
            copy.wait_send()
            copy.wait_recv()

    res = pl.pallas_call(
        body,
        name=name,
        in_specs=[HBM_ONLY] * n + [SEM_SPEC, SEM_SPEC] + [ANY_SPEC] * len(after),
        out_specs=[HBM_ONLY] * n,
        out_shape=[pltpu.HBM(b.shape, b.dtype) for b in bufs],
        input_output_aliases={a: a for a in range(n)},
        compiler_params=pltpu.CompilerParams(has_side_effects=DATAFLOW),
    )(*bufs, *sems, *after)
    return list(res)


def _allgather_small(name, block, deps=()):
    m_per, n = block.shape

    def body(x_ref, *rest):
        out_ref, send_sems, recv_sems, local_sem = rest[len(deps):]
        x, y, c, chips = _place()
        me, sibling = (x, y, c), (x, y, 1 - c)

        def rows(px, py, pc):
            return out_ref.at[pl.ds((4 * px + 2 * py + pc) * m_per, m_per), :]

        def copy(k, blk, to, src=None):
            return pltpu.make_async_remote_copy(
                src_ref=rows(*blk) if src is None else src, dst_ref=rows(*blk),
                send_sem=send_sems.at[k], recv_sem=recv_sems.at[k], device_id=to, device_id_type=MESH)

        mine = pltpu.make_async_copy(x_ref, rows(*me), local_sem)
        mine.start()
        first = [copy(0, me, sibling, src=x_ref)]
        first += [copy(1 + j, me, (*chip, c), src=x_ref) for j, chip in enumerate(chips)]
        for cp in first:
            cp.start()
        passed = [copy(4 + j, (*chip, c), sibling) for j, chip in enumerate(chips)]
        for j, chip in enumerate(chips):
            copy(1 + j, (*chip, c), me).wait_recv()
            passed[j].start()
        copy(0, sibling, me).wait_recv()
        for j, chip in enumerate(chips):
            copy(4 + j, (*chip, 1 - c), me).wait_recv()
        for cp in first + passed:
            cp.wait_send()
        mine.wait()

    return pl.pallas_call(
        body,
        name=name,
        out_shape=jax.ShapeDtypeStruct((8 * m_per, n), block.dtype),
        in_specs=[pl.BlockSpec(memory_space=pltpu.VMEM)] + [ANY_SPEC] * len(deps),
        out_specs=pl.BlockSpec(memory_space=pltpu.VMEM),
        scratch_shapes=[pltpu.SemaphoreType.DMA((7,)), pltpu.SemaphoreType.DMA((7,)), pltpu.SemaphoreType.DMA],
    )(block, *deps)


def _sum_blocks(name, gathered, n_blocks):
    r = gathered.shape[0] // n_blocks
    c = gathered.shape[1]

    def body(g_ref, o_ref):
        acc = g_ref[0:r, :]
        for b in range(1, n_blocks):
            acc = acc + g_ref[b * r:(b + 1) * r, :]
        o_ref[...] = acc

    return pl.pallas_call(body, name=name, out_shape=jax.ShapeDtypeStruct((r, c), F32))(gathered)


def _largest_tile(n, cap, mult):
    best = None
    for d in range(mult, min(n, cap) + 1, mult):
        if n % d == 0:
            best = d
    assert best is not None, (n, cap, mult)
    return best


def kernel(x, meta, g_pre_mix, w_in, b_gates, conf_dw_w, conf_dw_b, conf_ln_g, conf_ln_b, conf_w_pw, short_dw_w, short_w_out, w_o, g_post_mix, g_pre_mlp, w_up, w_down, g_post_mlp, loss_target, m_meta, m_g_pre_mix, m_w_in, m_b_gates, m_conf_dw_w, m_conf_dw_b, m_conf_ln_g, m_conf_ln_b, m_conf_w_pw, m_short_dw_w, m_short_w_out, m_w_o, m_g_post_mix, m_g_pre_mlp, m_w_up, m_w_down, m_g_post_mlp, v_meta, v_g_pre_mix, v_w_in, v_b_gates, v_conf_dw_w, v_conf_dw_b, v_conf_ln_g, v_conf_ln_b, v_conf_w_pw, v_short_dw_w, v_short_w_out, v_w_o, v_g_post_mix, v_g_pre_mlp, v_w_up, v_w_down, v_g_post_mlp):
    seq, d = x.shape[1], x.shape[2]
    t_real = seq + N_META
    t = -(-t_real // LANES) * LANES
    d_conf = conf_dw_b.shape[1]
    d_ff = w_up.shape[2] * N_CHIPS
    in_cols = w_in.shape[2] * N_CHIPS
    assert in_cols == 5 * d_conf + 2 * d and d == 2 * d_conf
    cw = d_conf
    core = lax.axis_index("c")
    chip = 2 * lax.axis_index("x") + lax.axis_index("y")

    tr = _largest_tile(t, 272, ROW_CHUNK)
    tm = t
    tn_in = _largest_tile(in_cols // N_CHIPS, 768, LANES)
    tn_d = _largest_tile(d, 1024, LANES)
    tn_h = _largest_tile(d, 512, LANES)
    tn_ff = _largest_tile(d_ff // N_CHIPS, 1024, LANES)
    tn_pw = _largest_tile(d // N_CHIPS, 512, LANES)

    big = [w_in[0], conf_w_pw[0], short_w_out[0], w_o[0], w_up[0], w_down[0]]
    chip_arr = chip.astype(jnp.int32).reshape(1)

    def cast(a, deps):
        return _into_slot(f"cast_w{a}", big[a], N_CHIPS, chip_arr, MXU_DTYPE, _largest_tile(big[a].shape[0], 256, 16), deps)

    small = [_into_slot(f"place_w{a}", w, N_CHIPS, chip_arr, F32, w.shape[0])
             for a, w in enumerate([meta, conf_dw_w[0], short_dw_w[0]])]
    sems_small, fly_small, tok_small = _gather_start("gather_start_small", small, [False] * 3, [[0, 1, 2]])
    sems_in, fly_in, tok_in = _gather_start("gather_start_in", [cast(0, [tok_small])], [True], [[0]])
    rest_groups = [[0, 1], [2], [3], [4]]
    sems_rest, fly_rest, tok_rest = _gather_start(
        "gather_start_rest", [cast(a, [tok_in]) for a in range(1, 6)], [True] * 5, rest_groups)

    def arrive(g, after):
        members = rest_groups[g]
        got = _gather_wait(f"gather_wait_rest{g}", [fly_rest[a] for a in members], [True] * len(members),
                           sems_rest[g], after)
        return _pass_halves(f"gather_pass_rest{g}", got)

    meta_g, wdw_g, w3_g = _gather_wait("gather_wait_small", fly_small, [False] * 3, sems_small[0], [tok_in])
    meta_full = jnp.transpose(meta_g, (1, 0, 2)).reshape(N_META, d)
    wdw = jnp.transpose(wdw_g, (1, 0, 2)).reshape(CONF_KERNEL, d_conf)
    w3 = jnp.transpose(w3_g, (1, 0, 2)).reshape(SHORT_KERNEL, d_conf)

    tail = jnp.zeros((t - t_real, d), F32)
    h0 = jnp.concatenate([meta_full, x[0], tail], axis=0)
    target = jnp.concatenate([jnp.zeros((N_META, d), F32), loss_target[0], tail], axis=0)

    def norm_in(i, rows, vecs):
        return [_rms_fwd(rows[0], vecs[0])], []

    (n_lp,) = _rowwise("norm_in", norm_in, [(h0, d, 0)], [g_pre_mix], [(d, MXU_DTYPE)], [], tr, deps=[tok_rest])
    (wg_in,) = _pass_halves("gather_pass_in", _gather_wait("gather_wait_in", fly_in, [True], sems_in[0], [n_lp]))
    proj =_matmul("proj", n_lp, wg_in, kind="nn", tm=tm, tn=tn_in, tk=d, out_dtypes=[F32])
    ac, c3 = _conv_fwd(proj, wdw, conf_dw_b, w3, d_conf)

    def ln_parts(ac_t, ln_g, ln_b):
        mu = jnp.mean(ac_t, axis=-1, keepdims=True)
        xc = ac_t - mu
        rstd = lax.rsqrt(jnp.mean(xc * xc, axis=-1, keepdims=True) + LN_EPS)
        xh = xc * rstd
        return xh, rstd, xh * ln_g + ln_b

    def branch_act(i, rows, vecs):
        ac_t, c3_t, bg_t = rows
        _, _, al = ln_parts(ac_t, vecs[0], vecs[1])
        return [al * _sigmoid(al), bg_t * c3_t], []

    a_act, s_lp = _rowwise("branch_act", branch_act, [(ac, cw, 0), (c3, cw, 0), (proj, cw, 2)],
                           [conf_ln_g, conf_ln_b], [(d_conf, MXU_DTYPE), (d_conf, MXU_DTYPE)], [], tr)
    wg_pw, wg_sout = arrive(0, [a_act])
    y_a = _matmul("y_a", a_act, wg_pw, kind="nn", tm=tm, tn=tn_pw, tk=d_conf, out_dtypes=[F32])
    y_b = _matmul("y_b", s_lp, wg_sout, kind="nn", tm=tm, tn=tn_pw, tk=d_conf, out_dtypes=[F32])

    gate_rows = [(proj, cw, 5), (proj, cw, 6), (proj, cw, 7), (proj, cw, 8)]

    def gates_of(rows, b):
        ga = _sigmoid(jnp.concatenate([rows[0], rows[1]], axis=1) + b[:, :d])
        gb = _sigmoid(jnp.concatenate([rows[2], rows[3]], axis=1) + b[:, d:])
        return ga, gb

    def gate(i, rows, vecs):
        ga, gb = gates_of(rows[2:], vecs[0])
        return [ga * rows[0] + gb * rows[1]], []

    (m_lp,) = _rowwise("gate", gate, [(y_a, d, 0), (y_b, d, 0)] + gate_rows, [b_gates], [(d, MXU_DTYPE)], [], tr)
    wg_o = arrive(1, [m_lp])[0].reshape(d, d)
    mix = _matmul("mix", m_lp, wg_o, kind="nn", tm=tm, tn=tn_d, tk=d, out_dtypes=[F32])

    def post_mix(i, rows, vecs):
        h1_t = rows[0] + _rms_fwd(rows[1], vecs[0])
        return [h1_t, _rms_fwd(h1_t, vecs[1])], []

    h1, n2_lp = _rowwise("post_mix", post_mix, [(h0, d, 0), (mix, d, 0)], [g_post_mix, g_pre_mlp],
                         [(d, F32), (d, MXU_DTYPE)], [], tr)
    (wg_up,) = arrive(2, [n2_lp])
    up, f_lp = _matmul("up", n2_lp, wg_up, kind="nn", tm=tm, tn=tn_h, tk=d, out_dtypes=[F32, MXU_DTYPE],
                       epilogue=lambda acc: (acc, jnp.square(jnp.maximum(acc, 0.0))))
    wg_down = arrive(3, [f_lp])[0].reshape(d_ff, d)
    dn = _matmul("down", f_lp, wg_down, kind="nn", tm=tm, tn=tn_h, tk=_largest_tile(d_ff, 2048, LANES),
                 out_dtypes=[F32])

    def head(row0, rows, vecs):
        h1_t, dn_t, tgt = rows
        y = h1_t + _rms_fwd(dn_t, vecs[0])
        row = row0 + lax.broadcasted_iota(jnp.int32, (ROW_CHUNK, 1), 0)
        err = jnp.where(jnp.logical_and(row >= N_META, row < t_real), y - tgt, 0.0)
        dy = err / d
        d_dn, dg = _rms_bwd(dn_t, vecs[0], dy)
        loss_rows = 0.5 * jnp.mean(err * err, axis=-1, keepdims=True)
        return [dy, d_dn], [dg, jnp.broadcast_to(loss_rows, (ROW_CHUNK, LANES))]

    dy, d_dn, dg_post_mlp, loss_vec = _rowwise(
        "head", head, [(h1, d, 0), (dn, d, 0), (target, d, 0)], [g_post_mlp], [(d, F32), (d, MXU_DTYPE)], [d, LANES], tr)

    core_arr = core.astype(jnp.int32).reshape(1)
    place = jnp.stack([chip, core]).astype(jnp.int32)
    in_flight = {}


    def tag(members):
        return "".join(str(a) for a in members)

    own_halves = {}
    two_types = dict(out_dtypes=[F32, WIRE_DTYPE], epilogue=lambda acc: (acc, acc))

    def swap_start(members, gws):
        def in_halves(gw):
            return gw.reshape(N_CHIPS, 2, gw.shape[-2] // 2, gw.shape[-1])

        own_halves[members] = [in_halves(gw.reshape(N_CHIPS, -1, gw.shape[-1])) for gw, _ in gws]
        for_sibling = [in_halves(lp.reshape(N_CHIPS, -1, lp.shape[-1])) for _, lp in gws]
        *in_flight[members], token = _swap_start(f"swap_start{tag(members)}", for_sibling)
        return token

    def exchange_start(members, after):
        sems, halves, lands = in_flight[members]
        _, lands = _swap_wait(f"swap_wait{tag(members)}", halves, lands, sems, after)
        pairs = [_pair_sum(f"pair_sum{a}", h, q, core_arr, _largest_tile(q.shape[1], 256, 16))
                 for a, h, q in zip(members, own_halves[members], lands)]
        *in_flight[members], token = _exchange_start(f"exchange_start{tag(members)}", pairs)
        return token

    def reduce_finish(groups, after):
        waited = _exchange_wait("exchange_wait" + "_".join(tag(g) for g in groups), [in_flight[g] for g in groups], after)
        return {a: _sum_pieces(f"sum_pieces{a}", pair, land, place, _largest_tile(land.shape[1], 256, 16))
                for g, (pairs, lands) in zip(groups, waited) for a, pair, land in zip(g, pairs, lands)}

    d_up = _matmul("d_up", d_dn, wg_down, kind="nt", tm=tm, tn=tn_h, tk=d, out_dtypes=[MXU_DTYPE], extras=[up],
                   epilogue=lambda acc, up_t: (acc * (2.0 * jnp.maximum(up_t, 0.0)),))
    tk_t = t
    gw_down = _matmul("gw_down", f_lp, d_dn, kind="tn", tm=_largest_tile(d_ff, 2048, LANES), tn=tn_h, tk=tk_t,
                      **two_types)
    tok = swap_start((5,), [gw_down])
    d_n2 = _matmul("d_n2", d_up, wg_up, kind="nt", tm=tm, tn=tn_h, tk=d_ff // N_CHIPS, out_dtypes=[F32], deps=[tok])
    tok = exchange_start((5,), [d_n2])
    gw_up = _matmul("gw_up", n2_lp, d_up, kind="tn", tm=_largest_tile(d, 2048, LANES), tn=tn_h, tk=tk_t,
                    out_pieces=N_CHIPS, deps=[tok], **two_types)
    tok = swap_start((4,), [gw_up])

    def bwd_mid(i, rows, vecs):
        dy_t, dn2_t, h1_t, mix_t = rows
        d_h1a, dg_pre_mlp = _rms_bwd(h1_t, vecs[1], dn2_t)
        d_h1 = dy_t + d_h1a
        d_mix, dg_post_mix = _rms_bwd(mix_t, vecs[0], d_h1)
        return [d_h1, d_mix], [dg_pre_mlp, dg_post_mix]

    d_h1, d_mix, dg_pre_mlp, dg_post_mix = _rowwise(
        "bwd_mid", bwd_mid, [(dy, d, 0), (d_n2, d, 0), (h1, d, 0), (mix, d, 0)], [g_post_mix, g_pre_mlp],
        [(d, F32), (d, MXU_DTYPE)], [d, d], tr, deps=[tok])
    d_m = _matmul("d_m", d_mix, wg_o, kind="nt", tm=tm, tn=tn_h, tk=d, out_dtypes=[F32])
    tok = exchange_start((4,), [d_m])
    gw_o = _matmul("gw_o", m_lp, d_mix, kind="tn", tm=_largest_tile(d, 2048, LANES), tn=tn_h, tk=tk_t, deps=[tok],
                   **two_types)

    def gate_bwd(i, rows, vecs):
        dm_t, ya_t, yb_t = rows[:3]
        ga, gb = gates_of(rows[3:], vecs[0])
        d_gpre = jnp.concatenate([dm_t * ya_t * ga * (1.0 - ga), dm_t * yb_t * gb * (1.0 - gb)], axis=1)
        return [dm_t * ga, dm_t * gb, d_gpre], [d_gpre]

    d_ya, d_yb, d_gpre, dg_b_gates = _rowwise(
        "gate_bwd", gate_bwd, [(d_m, d, 0), (y_a, d, 0), (y_b, d, 0)] + gate_rows, [b_gates],
        [(d, MXU_DTYPE), (d, MXU_DTYPE), (2 * d, MXU_DTYPE)], [2 * d], tr, deps=[tok])
    d_aact = _matmul("d_aact", d_ya, wg_pw, kind="nt", tm=tm, tn=d_conf // 2, tk=tn_pw, out_dtypes=[F32], deps=[tok])
    gw_pw = _matmul("gw_pw", a_act, d_ya, kind="tn", tm=_largest_tile(d_conf, 2048, LANES), tn=tn_pw, tk=tk_t,
                    out_pieces=N_CHIPS, **two_types)
    d_s = _matmul("d_s", d_yb, wg_sout, kind="nt", tm=tm, tn=d_conf // 2, tk=tn_pw, out_dtypes=[F32], deps=[tok])
    gw_sout = _matmul("gw_sout", s_lp, d_yb, kind="tn", tm=_largest_tile(d_conf, 2048, LANES), tn=tn_pw, tk=tk_t,
                      out_pieces=N_CHIPS, deps=[tok], **two_types)
    small_big = (3, 1, 2)
    tok = swap_start(small_big, [gw_o, gw_pw, gw_sout])

    def branch_bwd(i, rows, vecs):
        daact_t, ds_t, ac_t, c3_t, bg_t = rows
        xh, rstd, al = ln_parts(ac_t, vecs[0], vecs[1])
        sg = _sigmoid(al)
        d_al = daact_t * (sg * (1.0 + al * (1.0 - sg)))
        dxh = d_al * vecs[0]
        d_ac = rstd * (dxh - jnp.mean(dxh, axis=-1, keepdims=True) - xh * jnp.mean(dxh * xh, axis=-1, keepdims=True))
        return [d_ac, ds_t * bg_t, ds_t * c3_t], [d_al * xh, d_al, d_ac]

    d_ac, d_c3, d_bg, dg_ln_g, dg_ln_b, dg_dw_b = _rowwise(
        "branch_bwd", branch_bwd, [(d_aact, cw, 0), (d_s, cw, 0), (ac, cw, 0), (c3, cw, 0), (proj, cw, 2)],
        [conf_ln_g, conf_ln_b], [(d_conf, F32), (d_conf, F32), (d_conf, MXU_DTYPE)], [d_conf] * 3, tr, deps=[tok])
    d_av, d_ag, d_cg, d_v, dg_wdw, dg_w3 = _conv_bwd(proj, d_ac, d_c3, wdw, w3, d_conf)
    d_proj = jnp.concatenate([d_av, d_ag, d_bg, d_cg, d_v, d_gpre], axis=1)
    small_w = d_conf

    def pack(arrs):
        flat = jnp.concatenate([a.reshape(-1, small_w) for a in arrs], axis=0)
        return jnp.pad(flat, ((0, -flat.shape[0] % 8), (0, 0)))

    def reduce_small(tag, arrs, deps):
        assert all(a.size % small_w == 0 for a in arrs)
        gathered = _allgather_small(f"allgather_small_{tag}", pack(arrs), deps)
        first, r0 = [], 0
        for a in arrs:
            first.append(r0)
            r0 += a.size // small_w
        return _sum_blocks(f"sum_small_{tag}", gathered, 8), first

    def rows_of(total, r0, like):
        return total[r0:r0 + like.size // small_w].reshape(like.shape)

    tok = exchange_start(small_big, [d_proj])

    gw_in = _matmul("gw_in", n_lp, d_proj, kind="tn", tm=_largest_tile(d, 2048, LANES), tn=tn_in, tk=tk_t,
                    out_pieces=N_CHIPS, deps=[tok], **two_types)
    tok = swap_start((0,), [gw_in])
    d_n = _matmul("d_n", d_proj, wg_in, kind="nt", tm=tm, tn=tn_h, tk=in_cols // N_CHIPS, out_dtypes=[F32], deps=[tok])
    tok = exchange_start((0,), [d_n])

    early = [5, 4, 3, 1, 2]
    reduced = reduce_finish([(5,), (4,), small_big], [tok])
    join_sems, joining, tok = _join_start("join_start_early", [reduced[a] for a in early])

    def bwd_in(i, rows, vecs):
        d_h0a, dg = _rms_bwd(rows[2], vecs[0], rows[1])
        return [rows[0] + d_h0a], [dg]

    d_h0, dg_pre_mix = _rowwise("bwd_in", bwd_in, [(d_h1, d, 0), (d_n, d, 0), (h0, d, 0)], [g_pre_mix],
                                [(d, F32)], [d], tr, deps=[tok])
    grad_x = d_h0[N_META:t_real][None]

    big_m = [m_w_in, m_conf_w_pw, m_short_w_out, m_w_o, m_w_up, m_w_down]
    big_v = [v_w_in, v_conf_w_pw, v_short_w_out, v_w_o, v_w_up, v_w_down]
    big_res = {}

    def adam_group(members, joined):
        for a, j in zip(members, joined):
            big_res[a] = _adamw(f"adamw_big{a}", big[a], j.reshape(big[a].shape), big_m[a][0], big_v[a][0],
                                _largest_tile(big[a].shape[0], 256, 8))

    adam_group(early, _join_wait("join_wait_early", joining, join_sems, [d_h0]))
    reduced = reduce_finish([(0,)], [big_res[a][1] for a in early])
    join_sems, joining, tok = _join_start("join_start_late", [reduced[0]])
    rep_g = [dg_pre_mix, dg_b_gates, dg_dw_b, dg_ln_g, dg_ln_b, dg_post_mix, dg_pre_mlp, dg_post_mlp]
    rep_w = [g_pre_mix, b_gates, conf_dw_b, conf_ln_g, conf_ln_b, g_post_mix, g_pre_mlp, g_post_mlp]
    rep_m = [m_g_pre_mix, m_b_gates, m_conf_dw_b, m_conf_ln_g, m_conf_ln_b, m_g_post_mix, m_g_pre_mlp, m_g_post_mlp]
    rep_v = [v_g_pre_mix, v_b_gates, v_conf_dw_b, v_conf_ln_g, v_conf_ln_b, v_g_post_mix, v_g_pre_mlp, v_g_post_mlp]
    col_g = [d_h0[:N_META], dg_wdw, dg_w3]
    loss_row = jnp.tile(loss_vec, (1, small_w // LANES))
    total, first = reduce_small("all", rep_g + col_g + [loss_row], [tok])
    loss = total[first[-1], 0]
    rep_res = _small_adamw("adamw_rep", total, [(first[e], w, m, v) for e, (w, m, v) in enumerate(zip(rep_w, rep_m, rep_v))])
    col_res = {}
    for a, (w, g_part, m, v) in enumerate([(meta, col_g[0], m_meta, v_meta), (conf_dw_w[0], col_g[1], m_conf_dw_w[0], v_conf_dw_w[0]),
                                           (short_dw_w[0], col_g[2], m_short_dw_w[0], v_short_dw_w[0])]):
        g_full = rows_of(total, first[len(rep_g) + a], g_part)
        g_own = lax.dynamic_slice_in_dim(g_full, chip * w.shape[1], w.shape[1], axis=1)
        col_res[a] = _adamw(f"adamw_col{a}", w, g_own, m, v, w.shape[0])
    adam_group([0], _join_wait("join_wait_late", joining, join_sems, [rep_res[0][1]] + [col_res[a][1] for a in range(3)]))

    def leaf(q):
        r = lambda a: rep_res[a][q]
        b = lambda a: big_res[a][q][None]
        return [col_res[0][q], r(0), b(0), r(1), col_res[1][q][None], r(2), r(3), r(4), b(1), col_res[2][q][None], b(2),
                b(3), r(5), r(6), b(4), b(5), r(7)]

    return (loss, grad_x, *leaf(0), *leaf(1), *leaf(2), *leaf(3))
```

```python
import jax
import jax.numpy as jnp
from jax import lax
from jax.experimental import pallas as pl
from jax.experimental.pallas import tpu as pltpu

F32 = jnp.float32
BF16 = jnp.bfloat16
MXU_DTYPE = BF16
WIRE_DTYPE = BF16

N_META = 16
CONF_KERNEL = 31
SHORT_KERNEL = 3
CONV_PAD = 32
RMS_EPS = 1e-6
LN_EPS = 1e-5
ADAM_LR = 0.001
ADAM_B1 = 0.9
ADAM_B2 = 0.999
ADAM_EPS = 1e-08
ADAM_WD = 0.01
ADAM_STEP = 10

N_CHIPS = 4
MESH = pl.DeviceIdType.MESH
LANES = 128


def _sigmoid(z):
    return 1.0 / (1.0 + jnp.exp(-z))


ANY_SPEC = pl.BlockSpec(memory_space=pl.ANY)


def _matmul(name, a, b, *, kind, tm, tn, tk, out_dtypes, out_pieces=1, epilogue=None, extras=(), deps=()):
    pieces = b.shape[0] if b.ndim == 3 else 1
    if kind == "nn":
        m, kdim = a.shape
        n = b.shape[-1] * pieces
        dims = (((1,), (0,)), ((), ()))
        a_spec = pl.BlockSpec((tm, tk), lambda i, j, k: (i, k))
        if b.ndim == 2:
            b_spec = pl.BlockSpec((tk, tn), lambda i, j, k: (k, j))
        else:
            npp = b.shape[-1] // tn
            b_spec = pl.BlockSpec((None, tk, tn), lambda i, j, k: (j // npp, k, j % npp))
    elif kind == "nt":
        m, kdim = a.shape
        n = b.shape[-2]
        dims = (((1,), (1,)), ((), ()))
        a_spec = pl.BlockSpec((tm, tk), lambda i, j, k: (i, k))
        if b.ndim == 2:
            b_spec = pl.BlockSpec((tn, tk), lambda i, j, k: (j, k))
        else:
            kpp = b.shape[-1] // tk
            b_spec = pl.BlockSpec((None, tn, tk), lambda i, j, k: (k // kpp, j, k % kpp))
    else:
        kdim, m = a.shape
        n = b.shape[-1]
        dims = (((0,), (0,)), ((), ()))
        a_spec = pl.BlockSpec((tk, tm), lambda i, j, k: (k, i))
        b_spec = pl.BlockSpec((tk, tn), lambda i, j, k: (k, j))
    assert m % tm == 0 and n % tn == 0 and kdim % tk == 0, (name, m, n, kdim, tm, tn, tk)
    nk = kdim // tk
    if out_pieces == 1:
        out_shape = (m, n)
        out_spec = pl.BlockSpec((tm, tn), lambda i, j, k: (i, j))
    else:
        onpp = n // out_pieces // tn
        out_shape = (out_pieces, m, n // out_pieces)
        out_spec = pl.BlockSpec((None, tm, tn), lambda i, j, k: (j // onpp, i, j % onpp))
    n_ex, n_out, n_in = len(extras), len(out_dtypes), len(extras) + len(deps)
    if epilogue is None:
        epilogue = lambda acc: (acc,)

    def body(a_ref, b_ref, *rest):
        ex_refs, o_refs = rest[:n_ex], rest[n_in:n_in + n_out]
        prod = lax.dot_general(a_ref[...], b_ref[...], dims, preferred_element_type=F32)

        def finish(acc):
            tiles = epilogue(acc, *[r[...] for r in ex_refs])
            for o_ref, t in zip(o_refs, tiles):
                o_ref[...] = t.astype(o_ref.dtype)

        if nk == 1:
            finish(prod)
        else:
            acc_ref = rest[n_in + n_out]
            k = pl.program_id(2)

            @pl.when(k == 0)
            def _():
                acc_ref[...] = prod

            @pl.when(jnp.logical_and(k > 0, k < nk - 1))
            def _():
                acc_ref[...] += prod

            @pl.when(k == nk - 1)
            def _():
                finish(acc_ref[...] + prod)

    ex_specs = [pl.BlockSpec((tm, tn), lambda i, j, k: (i, j)) for _ in extras]
    res = pl.pallas_call(
        body,
        name=name,
        grid=(m // tm, n // tn, nk),
        in_specs=[a_spec, b_spec, *ex_specs] + [ANY_SPEC] * len(deps),
        out_specs=[out_spec] * n_out,
        out_shape=[jax.ShapeDtypeStruct(out_shape, d) for d in out_dtypes],
        scratch_shapes=[pltpu.VMEM((tm, tn), F32)] if nk > 1 else [],
        compiler_params=pltpu.CompilerParams(dimension_semantics=("parallel", "parallel", "arbitrary")),
    )(a, b, *extras, *deps)
    return res[0] if n_out == 1 else res


ROW_CHUNK = 16
SUBLANES = 8


def _rowwise(name, fn, rows, vecs, outs, sums, tr, deps=()):
    t = rows[0][0].shape[0]
    assert t % tr == 0 and tr % ROW_CHUNK == 0
    n_r, n_v, n_o, n_s = len(rows), len(vecs), len(outs), len(sums)
    n_in = n_r + n_v + len(deps)
    n_steps = t // tr

    def body(*refs):
        r_in, v_in = refs[:n_r], refs[n_r:n_r + n_v]
        o_refs = refs[n_in:n_in + n_o]
        s_refs = refs[n_in + n_o:n_in + n_o + n_s]
        acc_refs = refs[n_in + n_o + n_s:]
        i = pl.program_id(0)

        @pl.when(i == 0)
        def _():
            for acc_ref in acc_refs:
                acc_ref[...] = jnp.zeros(acc_ref.shape, F32)

        def chunk(ci):
            r0 = ci * ROW_CHUNK
            sl = pl.ds(r0, ROW_CHUNK)
            o_tiles, s_tiles = fn(i * tr + r0, [r[sl, :] for r in r_in], [v[...] for v in v_in])
            for o_ref, tile in zip(o_refs, o_tiles):
                o_ref[sl, :] = tile.astype(o_ref.dtype)
            for acc_ref, tile in zip(acc_refs, s_tiles):
                part = tile[0:SUBLANES]
                for s in range(1, ROW_CHUNK // SUBLANES):
                    part = part + tile[s * SUBLANES:(s + 1) * SUBLANES]
                acc_ref[...] += part

        for ci in range(tr // ROW_CHUNK):
            chunk(ci)

        @pl.when(i == n_steps - 1)
        def _():
            for s_ref, acc_ref in zip(s_refs, acc_refs):
                s_ref[...] = jnp.sum(acc_ref[...], axis=0, keepdims=True)

    def row_spec(width, blk):
        return pl.BlockSpec((tr, width), lambda i: (i, blk))

    res = pl.pallas_call(
        body,
        name=name,
        grid=(t // tr,),
        in_specs=[row_spec(w, blk) for _, w, blk in rows]
        + [pl.BlockSpec(v.shape, lambda i: (0, 0)) for v in vecs] + [ANY_SPEC] * len(deps),
        out_specs=[pl.BlockSpec((tr, c), lambda i: (i, 0)) for c, _ in outs]
        + [pl.BlockSpec((1, c), lambda i: (0, 0)) for c in sums],
        out_shape=[jax.ShapeDtypeStruct((t, c), d) for c, d in outs]
        + [jax.ShapeDtypeStruct((1, c), F32) for c in sums],
        scratch_shapes=[pltpu.VMEM((SUBLANES, c), F32) for c in sums],
        compiler_params=pltpu.CompilerParams(dimension_semantics=("arbitrary",)),
    )(*[r[0] for r in rows], *vecs, *deps)
    return res


def _rms_fwd(x, g):
    r = lax.rsqrt(jnp.mean(x * x, axis=-1, keepdims=True) + RMS_EPS)
    return x * r * g


def _rms_bwd(x, g, dy):
    r = lax.rsqrt(jnp.mean(x * x, axis=-1, keepdims=True) + RMS_EPS)
    xn = x * r
    dxn = dy * g
    dx = r * (dxn - xn * jnp.mean(dxn * xn, axis=-1, keepdims=True))
    return dx, dy * xn


CONV_ROWS = 64
CONV_LANES = 128


def _conv_fwd(proj, wdw, bdw, w3, d_conf):
    t = proj.shape[0]
    cl = CONV_LANES
    nb = d_conf // cl
    nchunk = t // CONV_ROWS
    assert t % CONV_ROWS == 0

    def body(av_ref, ag_ref, cg_ref, v_ref, wdw_ref, bdw_ref, w3_ref, ac_ref, c3_ref, apad, cpad):
        zeros = jnp.zeros((CONV_PAD, cl), F32)
        apad[0:CONV_PAD, :] = zeros
        cpad[0:CONV_PAD, :] = zeros
        apad[CONV_PAD:, :] = av_ref[...] * _sigmoid(ag_ref[...])
        cpad[CONV_PAD:, :] = cg_ref[...] * v_ref[...]

        def chunk(ci, carry):
            base = pl.multiple_of(ci * CONV_ROWS, 8)
            acc = jnp.zeros((CONV_ROWS, cl), F32) + bdw_ref[...]
            for k in range(CONF_KERNEL):
                off = CONV_PAD - (CONF_KERNEL - 1) + k
                acc = acc + apad[pl.ds(base + off, CONV_ROWS), :] * wdw_ref[k:k + 1, :]
            ac_ref[pl.ds(base, CONV_ROWS), :] = acc
            acc3 = jnp.zeros((CONV_ROWS, cl), F32)
            for k in range(SHORT_KERNEL):
                off = CONV_PAD - (SHORT_KERNEL - 1) + k
                acc3 = acc3 + cpad[pl.ds(base + off, CONV_ROWS), :] * w3_ref[k:k + 1, :]
            c3_ref[pl.ds(base, CONV_ROWS), :] = acc3
            return carry

        lax.fori_loop(0, nchunk, chunk, 0)

    def col(blk0):
        return pl.BlockSpec((t, cl), lambda j: (0, blk0 + j))

    return pl.pallas_call(
        body,
        name="conv_fwd",
        grid=(nb,),
        in_specs=[col(0), col(nb), col(3 * nb), col(4 * nb),
                  pl.BlockSpec((CONF_KERNEL, cl), lambda j: (0, j)),
                  pl.BlockSpec((1, cl), lambda j: (0, j)),
                  pl.BlockSpec((SHORT_KERNEL, cl), lambda j: (0, j))],
        out_specs=[pl.BlockSpec((t, cl), lambda j: (0, j))] * 2,
        out_shape=[jax.ShapeDtypeStruct((t, d_conf), F32)] * 2,
        scratch_shapes=[pltpu.VMEM((t + CONV_PAD, cl), F32)] * 2,
        compiler_params=pltpu.CompilerParams(dimension_semantics=("parallel",)),
    )(proj, proj, proj, proj, wdw, bdw, w3)


def _conv_bwd(proj, d_ac, d_c3, wdw, w3, d_conf):
    t = proj.shape[0]
    cl = CONV_LANES
    nb = d_conf // cl
    nchunk = t // CONV_ROWS
    nsub = CONV_ROWS // 8

    def fold(p):
        r = p[0:8]
        for s in range(1, nsub):
            r = r + p[8 * s:8 * s + 8]
        return r

    def body(av_ref, ag_ref, cg_ref, v_ref, dac_ref, dc3_ref, wdw_ref, w3_ref,
             dav_ref, dag_ref, dcg_ref, dv_ref, dwdw_ref, dw3_ref, apad, cpad, dapad, dcpad):
        zeros = jnp.zeros((CONV_PAD, cl), F32)
        apad[0:CONV_PAD, :] = zeros
        cpad[0:CONV_PAD, :] = zeros
        apad[CONV_PAD:, :] = av_ref[...] * _sigmoid(ag_ref[...])
        cpad[CONV_PAD:, :] = cg_ref[...] * v_ref[...]
        dapad[0:t, :] = dac_ref[...]
        dcpad[0:t, :] = dc3_ref[...]
        dapad[t:, :] = zeros
        dcpad[t:, :] = zeros

        def chunk(ci, accs):
            base = pl.multiple_of(ci * CONV_ROWS, 8)
            rows = pl.ds(base, CONV_ROWS)
            da = jnp.zeros((CONV_ROWS, cl), F32)
            for k in range(CONF_KERNEL):
                da = da + dapad[pl.ds(base + (CONF_KERNEL - 1 - k), CONV_ROWS), :] * wdw_ref[k:k + 1, :]
            dcv = jnp.zeros((CONV_ROWS, cl), F32)
            for k in range(SHORT_KERNEL):
                dcv = dcv + dcpad[pl.ds(base + (SHORT_KERNEL - 1 - k), CONV_ROWS), :] * w3_ref[k:k + 1, :]
            av, sg = av_ref[rows, :], _sigmoid(ag_ref[rows, :])
            dav_ref[rows, :] = (da * sg).astype(dav_ref.dtype)
            dag_ref[rows, :] = (da * av * sg * (1.0 - sg)).astype(dag_ref.dtype)
            dcg_ref[rows, :] = (dcv * v_ref[rows, :]).astype(dcg_ref.dtype)
            dv_ref[rows, :] = (dcv * cg_ref[rows, :]).astype(dv_ref.dtype)
            d_out, d_out3 = dac_ref[rows, :], dc3_ref[rows, :]
            new = []
            for k in range(CONF_KERNEL):
                off = CONV_PAD - (CONF_KERNEL - 1) + k
                new.append(accs[k] + fold(d_out * apad[pl.ds(base + off, CONV_ROWS), :]))
            for k in range(SHORT_KERNEL):
                off = CONV_PAD - (SHORT_KERNEL - 1) + k
                new.append(accs[CONF_KERNEL + k] + fold(d_out3 * cpad[pl.ds(base + off, CONV_ROWS), :]))
            return tuple(new)

        init = tuple(jnp.zeros((8, cl), F32) for _ in range(CONF_KERNEL + SHORT_KERNEL))
        accs = lax.fori_loop(0, nchunk, chunk, init)
        for k in range(CONF_KERNEL):
            dwdw_ref[k:k + 1, :] = jnp.sum(accs[k], axis=0, keepdims=True)
        for k in range(SHORT_KERNEL):
            dw3_ref[k:k + 1, :] = jnp.sum(accs[CONF_KERNEL + k], axis=0, keepdims=True)

    def col(blk0):
        return pl.BlockSpec((t, cl), lambda j: (0, blk0 + j))

    own = pl.BlockSpec((t, cl), lambda j: (0, j))
    return pl.pallas_call(
        body,
        name="conv_bwd",
        grid=(nb,),
        in_specs=[col(0), col(nb), col(3 * nb), col(4 * nb), own, own,
                  pl.BlockSpec((CONF_KERNEL, cl), lambda j: (0, j)),
                  pl.BlockSpec((SHORT_KERNEL, cl), lambda j: (0, j))],
        out_specs=[own] * 4 + [pl.BlockSpec((CONF_KERNEL, cl), lambda j: (0, j)),
                               pl.BlockSpec((SHORT_KERNEL, cl), lambda j: (0, j))],
        out_shape=[jax.ShapeDtypeStruct((t, d_conf), MXU_DTYPE)] * 4
        + [jax.ShapeDtypeStruct((CONF_KERNEL, d_conf), F32), jax.ShapeDtypeStruct((SHORT_KERNEL, d_conf), F32)],
        scratch_shapes=[pltpu.VMEM((t + CONV_PAD, cl), F32)] * 4,
        compiler_params=pltpu.CompilerParams(dimension_semantics=("parallel",)),
    )(proj, proj, proj, proj, d_ac, d_c3, wdw, w3)


def _elementwise(name, fn, ins, out_dtypes, tr, deps=()):
    ins = [(a, ()) if not isinstance(a, tuple) else a for a in ins]
    r, c = ins[0][0].shape[-2:]
    assert r % tr == 0, (name, r, tr)
    n_in = len(ins)

    def body(*refs):
        tiles = fn(*[x[...] for x in refs[:n_in]])
        for o_ref, tile in zip(refs[n_in + len(deps):], tiles):
            o_ref[...] = tile.astype(o_ref.dtype)

    def spec(lead):
        return pl.BlockSpec((None,) * len(lead) + (tr, c), lambda i: (*lead, i, 0))

    res = pl.pallas_call(
        body,
        name=name,
        grid=(r // tr,),
        in_specs=[spec(lead) for _, lead in ins] + [ANY_SPEC] * len(deps),
        out_specs=[pl.BlockSpec((tr, c), lambda i: (i, 0))] * len(out_dtypes),
        out_shape=[jax.ShapeDtypeStruct((r, c), d) for d in out_dtypes],
        compiler_params=pltpu.CompilerParams(dimension_semantics=("parallel",)),
    )(*[a for a, _ in ins], *deps)
    return res


def _adamw_tiles(w, g, m, v):
    m = ADAM_B1 * m + (1.0 - ADAM_B1) * g
    v = ADAM_B2 * v + (1.0 - ADAM_B2) * jnp.square(g)
    m_hat = m / (1.0 - ADAM_B1 ** ADAM_STEP)
    v_hat = v / (1.0 - ADAM_B2 ** ADAM_STEP)
    delta = -ADAM_LR * (m_hat / (jnp.sqrt(v_hat) + ADAM_EPS) + ADAM_WD * w)
    return g, delta, m, v


def _adamw(name, w, g, m, v, tr, deps=()):
    shape = w.shape
    flat = [a.reshape(shape[-2:]) if a.ndim > 2 else a for a in (w, g, m, v)]
    res = _elementwise(name, _adamw_tiles, flat, [F32] * 4, tr, deps)
    return [a.reshape(shape) for a in res]


def _small_adamw(name, total, entries):
    sw = total.shape[1]
    n = len(entries)

    def body(total_ref, *refs):
        ins, outs = refs[:3 * n], refs[3 * n:]
        for e, (row0, w, _, _) in enumerate(entries):
            for q in range(w.shape[1] // sw):
                cols = slice(q * sw, (q + 1) * sw)
                tiles = _adamw_tiles(ins[3 * e][:, cols], total_ref[row0 + q:row0 + q + 1, :],
                                     ins[3 * e + 1][:, cols], ins[3 * e + 2][:, cols])
                for o_ref, tile in zip(outs[4 * e:4 * e + 4], tiles):
                    o_ref[:, cols] = tile

    res = pl.pallas_call(
        body,
        name=name,
        out_shape=[jax.ShapeDtypeStruct(w.shape, F32) for _, w, _, _ in entries for _ in range(4)],
    )(total, *[a for _, w, m, v in entries for a in (w, m, v)])
    return [list(res[4 * e:4 * e + 4]) for e in range(n)]


def _pair_sum(name, p, q, core, tr):
    n_p, _, hr, c = p.shape
    assert hr % tr == 0

    def body(core_ref, p_ref, q_ref, o_ref):
        o_ref[...] = (p_ref[...] + q_ref[...].astype(F32)).astype(o_ref.dtype)

    return pl.pallas_call(
        body,
        name=name,
        grid_spec=pltpu.PrefetchScalarGridSpec(
            num_scalar_prefetch=1,
            grid=(n_p, hr // tr),
            in_specs=[pl.BlockSpec((None, None, tr, c), lambda a, i, core_ref: (a, core_ref[0], i, 0)),
                      pl.BlockSpec((None, tr, c), lambda a, i, core_ref: (a, i, 0))],
            out_specs=pl.BlockSpec((None, tr, c), lambda a, i, core_ref: (a, i, 0)),
        ),
        out_shape=jax.ShapeDtypeStruct((n_p, hr, c), WIRE_DTYPE),
        compiler_params=pltpu.CompilerParams(dimension_semantics=("parallel", "parallel")),
    )(core, p, q)


def _into_slot(name, w, slots, slot, dtype, tr, deps=()):
    r, c = w.shape
    assert r % tr == 0

    def body(slot_ref, w_ref, *rest):
        o_ref = rest[len(deps)]
        o_ref[...] = w_ref[...].astype(o_ref.dtype)

    return pl.pallas_call(
        body,
        name=name,
        grid_spec=pltpu.PrefetchScalarGridSpec(
            num_scalar_prefetch=1,
            grid=(r // tr,),
            in_specs=[pl.BlockSpec((tr, c), lambda i, slot_ref: (i, 0))] + [ANY_SPEC] * len(deps),
            out_specs=pl.BlockSpec((None, tr, c), lambda i, slot_ref: (slot_ref[0], i, 0)),
        ),
        out_shape=jax.ShapeDtypeStruct((slots, r, c), dtype),
        compiler_params=pltpu.CompilerParams(dimension_semantics=("parallel",)),
    )(slot, w, *deps)


def _sum_pieces(name, own, rb, place, tr):
    n_p, hr, c = rb.shape
    assert hr % tr == 0

    def body(place_ref, own_ref, *refs):
        chip = place_ref[0]
        acc = None
        for k in range(n_p):
            tile = jnp.where(chip == k, own_ref[...], refs[k][...]).astype(F32)
            acc = tile if acc is None else acc + tile
        refs[n_p][...] = acc

    def landed(k):
        return pl.BlockSpec((None, tr, c), lambda i, place_ref: (jnp.where(place_ref[0] == k, (k + 1) % n_p, k), i, 0))

    return pl.pallas_call(
        body,
        name=name,
        grid_spec=pltpu.PrefetchScalarGridSpec(
            num_scalar_prefetch=1,
            grid=(hr // tr,),
            in_specs=[pl.BlockSpec((None, tr, c), lambda i, place_ref: (place_ref[0], i, 0))]
            + [landed(k) for k in range(n_p)],
            out_specs=pl.BlockSpec((None, tr, c), lambda i, place_ref: (place_ref[1], i, 0)),
        ),
        out_shape=jax.ShapeDtypeStruct((2, hr, c), F32),
        compiler_params=pltpu.CompilerParams(dimension_semantics=("parallel",)),
    )(place, own, *([rb] * n_p))


def _pair_sum_group(name, ps, qs, core):
    n = len(ps)

    def body(core_ref, *refs):
        for s in range(n):
            refs[2 * n + s][...] = (refs[s][...] + refs[n + s][...].astype(F32)).astype(WIRE_DTYPE)

    def piece(shape):
        return pl.BlockSpec((None, *shape), lambda a, core_ref: (a, 0, 0))

    return pl.pallas_call(
        body,
        name=name,
        grid_spec=pltpu.PrefetchScalarGridSpec(
            num_scalar_prefetch=1,
            grid=(ps[0].shape[0],),
            in_specs=[pl.BlockSpec((None, None, *p.shape[2:]), lambda a, core_ref: (a, core_ref[0], 0, 0)) for p in ps]
            + [piece(q.shape[1:]) for q in qs],
            out_specs=[piece(q.shape[1:]) for q in qs],
        ),
        out_shape=[jax.ShapeDtypeStruct(q.shape, WIRE_DTYPE) for q in qs],
        compiler_params=pltpu.CompilerParams(dimension_semantics=("parallel",)),
    )(core, *ps, *qs)


def _sum_pieces_group(name, owns, rbs, place):
    n = len(owns)
    n_p = rbs[0].shape[0]

    def body(place_ref, *refs):
        chip = place_ref[0]
        for s in range(n):
            lands = refs[n + s * n_p:n + (s + 1) * n_p]
            acc = None
            for k in range(n_p):
                tile = jnp.where(chip == k, refs[s][...], lands[k][...]).astype(F32)
                acc = tile if acc is None else acc + tile
            refs[n + n * n_p + s][...] = acc

    def landed(k, shape):
        return pl.BlockSpec((None, *shape), lambda i, place_ref: (jnp.where(place_ref[0] == k, (k + 1) % n_p, k), 0, 0))

    return pl.pallas_call(
        body,
        name=name,
        grid_spec=pltpu.PrefetchScalarGridSpec(
            num_scalar_prefetch=1,
            grid=(1,),
            in_specs=[pl.BlockSpec((None, *o.shape[1:]), lambda i, place_ref: (place_ref[0], 0, 0)) for o in owns]
            + [landed(k, rb.shape[1:]) for rb in rbs for k in range(n_p)],
            out_specs=[pl.BlockSpec((None, *rb.shape[1:]), lambda i, place_ref: (place_ref[1], 0, 0)) for rb in rbs],
        ),
        out_shape=[jax.ShapeDtypeStruct((2, *rb.shape[1:]), F32) for rb in rbs],
        compiler_params=pltpu.CompilerParams(dimension_semantics=("arbitrary",)),
    )(place, *owns, *[rb for rb in rbs for _ in range(n_p)])


HBM_SPEC = pl.BlockSpec(memory_space=pl.ANY)


def _place():
    x, y, c = lax.axis_index("x"), lax.axis_index("y"), lax.axis_index("c")
    chips = [(1 - x, y), (x, 1 - y), (1 - x, 1 - y)]
    return x, y, c, chips


HBM_ONLY = pl.BlockSpec(memory_space=pltpu.HBM)
SEM_SPEC = pl.BlockSpec(memory_space=pltpu.SEMAPHORE)
DATAFLOW = pltpu.SideEffectType.DATAFLOW_SIDE_EFFECTING


def _in_hbm(a):
    return pltpu.with_memory_space_constraint(a, pltpu.HBM)


def _shard_part(ref, is_split, slot, h):
    if not is_split:
        return ref.at[slot]
    hr = ref.shape[1] // 2
    return ref.at[slot, pl.ds(h * hr, hr), :]


TOKEN = jax.ShapeDtypeStruct((8, LANES), F32)
VMEM_SPEC = pl.BlockSpec(memory_space=pltpu.VMEM)


def _gather_start(name, bufs, split, groups, deps=()):
    n, ng = len(bufs), len(groups)

    def body(*refs):
        ins, sems = refs[:n], refs[n + len(deps):n + len(deps) + 2 * ng]
        refs[-1][...] = jnp.zeros(TOKEN.shape, TOKEN.dtype)
        x, y, c, chips = _place()
        me = 2 * x + y
        for g, members in enumerate(groups):
            for s, a in enumerate(members):
                mine = _shard_part(ins[a], split[a], me, c)
                for j, chip in enumerate(chips):
                    pltpu.make_async_remote_copy(
                        src_ref=mine, dst_ref=mine, send_sem=sems[2 * g].at[3 * s + j], recv_sem=sems[2 * g + 1].at[3 * s + j],
                        device_id=(*chip, c), device_id_type=MESH).start()

    res = pl.pallas_call(
        body,
        name=name,
        in_specs=[HBM_ONLY] * n + [ANY_SPEC] * len(deps),
        out_specs=[SEM_SPEC] * (2 * ng) + [HBM_ONLY] * n + [VMEM_SPEC],
        out_shape=[pltpu.SemaphoreType.DMA((3 * len(members),)) for members in groups for _ in range(2)]
        + [pltpu.HBM(b.shape, b.dtype) for b in bufs] + [TOKEN],
        input_output_aliases={a: 2 * ng + a for a in range(n)},
        compiler_params=pltpu.CompilerParams(has_side_effects=DATAFLOW),
    )(*[_in_hbm(b) for b in bufs], *deps)
    return [(res[2 * g], res[2 * g + 1]) for g in range(ng)], list(res[2 * ng:2 * ng + n]), res[-1]


def _gather_wait(name, bufs, split, sems, after):
    n = len(bufs)

    def body(*refs):
        ins, send_sems, recv_sems = refs[:n], refs[n], refs[n + 1]
        x, y, c, chips = _place()
        me = 2 * x + y
        for s in range(n):
            for j, chip in enumerate(chips):
                copy = pltpu.make_async_remote_copy(
                    src_ref=_shard_part(ins[s], split[s], me, c),
                    dst_ref=_shard_part(ins[s], split[s], 2 * chip[0] + chip[1], c),
                    send_sem=send_sems.at[3 * s + j], recv_sem=recv_sems.at[3 * s + j],
                    device_id=(*chip, c), device_id_type=MESH)
                copy.wait_send()
                copy.wait_recv()

    res = pl.pallas_call(
        body,
        name=name,
        in_specs=[HBM_ONLY] * n + [SEM_SPEC, SEM_SPEC] + [ANY_SPEC] * len(after),
        out_specs=[HBM_ONLY] * n,
        out_shape=[pltpu.HBM(b.shape, b.dtype) for b in bufs],
        input_output_aliases={a: a for a in range(n)},
        compiler_params=pltpu.CompilerParams(has_side_effects=DATAFLOW),
    )(*bufs, *sems, *after)
    return list(res)


def _pass_halves(name, bufs):
    n = len(bufs)

    def body(*refs):
        outs = refs[n:2 * n]
        send_sems, recv_sems = refs[2 * n:]
        x, y, c, chips = _place()
        sibling = (x, y, 1 - c)

        def copy(a, j, h):
            blk = _shard_part(outs[a], True, 2 * chips[j][0] + chips[j][1], h)
            return pltpu.make_async_remote_copy(
                src_ref=blk, dst_ref=blk, send_sem=send_sems.at[3 * a + j], recv_sem=recv_sems.at[3 * a + j],
                device_id=sibling, device_id_type=MESH)

        sends = [copy(a, j, c) for a in range(n) for j in range(3)]
        for cp in sends:
            cp.start()
        for a in range(n):
            for j in range(3):
                copy(a, j, 1 - c).wait_recv()
        for cp in sends:
            cp.wait_send()

    res = pl.pallas_call(
        body,
        name=name,
        in_specs=[HBM_SPEC] * n,
        out_specs=[HBM_SPEC] * n,
        out_shape=[jax.ShapeDtypeStruct(b.shape, b.dtype) for b in bufs],
        input_output_aliases={a: a for a in range(n)},
        scratch_shapes=[pltpu.SemaphoreType.DMA((3 * n,)), pltpu.SemaphoreType.DMA((3 * n,))],
    )(*bufs)
    return list(res)


def _exchange_start(name, pairs):
    n = len(pairs)

    def body(*refs):
        pair_refs, land_refs, send_sems, recv_sems = refs[:n], refs[n:2 * n], refs[2 * n], refs[2 * n + 1]
        x, y, c, chips = _place()
        me = 2 * x + y
        for a in range(n):
            for j, chip in enumerate(chips):
                pltpu.make_async_remote_copy(
                    src_ref=pair_refs[a].at[2 * chip[0] + chip[1]], dst_ref=land_refs[a].at[me],
                    send_sem=send_sems.at[3 * a + j], recv_sem=recv_sems.at[3 * a + j],
                    device_id=(*chip, c), device_id_type=MESH).start()
        refs[-1][...] = jnp.zeros(TOKEN.shape, TOKEN.dtype)

    res = pl.pallas_call(
        body,
        name=name,
        in_specs=[HBM_ONLY] * (2 * n),
        out_specs=[SEM_SPEC, SEM_SPEC] + [HBM_ONLY] * (2 * n) + [VMEM_SPEC],
        out_shape=[pltpu.SemaphoreType.DMA((3 * n,)), pltpu.SemaphoreType.DMA((3 * n,))]
        + [pltpu.HBM(p.shape, p.dtype) for p in pairs] * 2 + [TOKEN],
        input_output_aliases={a: 2 + a for a in range(2 * n)},
        compiler_params=pltpu.CompilerParams(has_side_effects=DATAFLOW),
    )(*[_in_hbm(p) for p in pairs], *[_in_hbm(lax.empty(p.shape, p.dtype)) for p in pairs])
    return (res[0], res[1]), list(res[2:2 + n]), list(res[2 + n:2 + 2 * n]), res[-1]


def _exchange_wait(name, groups, after):
    sizes = [len(pairs) for _, pairs, _ in groups]
    n_buf = 2 * sum(sizes)

    def body(*refs):
        x, y, c, chips = _place()
        at_buf, at_sem = 0, n_buf
        for n in sizes:
            pair_refs, land_refs = refs[at_buf:at_buf + n], refs[at_buf + n:at_buf + 2 * n]
            send_sems, recv_sems = refs[at_sem], refs[at_sem + 1]
            at_buf, at_sem = at_buf + 2 * n, at_sem + 2
            for a in range(n):
                for j, chip in enumerate(chips):
                    k = 2 * chip[0] + chip[1]
                    copy = pltpu.make_async_remote_copy(
                        src_ref=pair_refs[a].at[k], dst_ref=land_refs[a].at[k],
                        send_sem=send_sems.at[3 * a + j], recv_sem=recv_sems.at[3 * a + j],
                        device_id=(*chip, c), device_id_type=MESH)
                    copy.wait_send()
                    copy.wait_recv()

    bufs = [b for _, pairs, lands in groups for b in (*pairs, *lands)]
    sems = [s for group_sems, _, _ in groups for s in group_sems]
    res = pl.pallas_call(
        body,
        name=name,
        in_specs=[HBM_ONLY] * n_buf + [SEM_SPEC] * len(sems) + [ANY_SPEC] * len(after),
        out_specs=[HBM_ONLY] * n_buf,
        out_shape=[pltpu.HBM(b.shape, b.dtype) for b in bufs],
        input_output_aliases={a: a for a in range(n_buf)},
        compiler_params=pltpu.CompilerParams(has_side_effects=DATAFLOW),
    )(*bufs, *sems, *after)
    out, at = [], 0
    for n in sizes:
        out.append((list(res[at:at + n]), list(res[at + n:at + 2 * n])))
        at += 2 * n
    return out


def _swap_start(name, halves):
    n = len(halves)
    n_p = halves[0].shape[0]
    land_shapes = [(n_p, *h.shape[2:]) for h in halves]

    def body(*refs):
        half_refs, land_refs, send_sems, recv_sems = refs[:n], refs[n:2 * n], refs[2 * n], refs[2 * n + 1]
        x, y, c, _ = _place()
        for a in range(n):
            for p in range(n_p):
                pltpu.make_async_remote_copy(
                    src_ref=half_refs[a].at[p, 1 - c], dst_ref=land_refs[a].at[p],
                    send_sem=send_sems.at[n_p * a + p], recv_sem=recv_sems.at[n_p * a + p],
                    device_id=(x, y, 1 - c), device_id_type=MESH).start()
        refs[-1][...] = jnp.zeros(TOKEN.shape, TOKEN.dtype)

    res = pl.pallas_call(
        body,
        name=name,
        in_specs=[HBM_ONLY] * (2 * n),
        out_specs=[SEM_SPEC, SEM_SPEC] + [HBM_ONLY] * (2 * n) + [VMEM_SPEC],
        out_shape=[pltpu.SemaphoreType.DMA((n_p * n,)), pltpu.SemaphoreType.DMA((n_p * n,))]
        + [pltpu.HBM(h.shape, h.dtype) for h in halves]
        + [pltpu.HBM(s, h.dtype) for s, h in zip(land_shapes, halves)] + [TOKEN],
        input_output_aliases={a: 2 + a for a in range(2 * n)},
        compiler_params=pltpu.CompilerParams(has_side_effects=DATAFLOW),
    )(*[_in_hbm(h) for h in halves], *[_in_hbm(lax.empty(s, h.dtype)) for s, h in zip(land_shapes, halves)])
    return (res[0], res[1]), list(res[2:2 + n]), list(res[2 + n:2 + 2 * n]), res[-1]


def _swap_wait(name, halves, lands, sems, after):
    n = len(halves)
    n_p = halves[0].shape[0]

    def body(*refs):
        half_refs, land_refs, send_sems, recv_sems = refs[:n], refs[n:2 * n], refs[2 * n], refs[2 * n + 1]
        x, y, c, _ = _place()
        for a in range(n):
            for p in range(n_p):
                copy = pltpu.make_async_remote_copy(
                    src_ref=half_refs[a].at[p, 1 - c], dst_ref=land_refs[a].at[p],
                    send_sem=send_sems.at[n_p * a + p], recv_sem=recv_sems.at[n_p * a + p],
                    device_id=(x, y, 1 - c), device_id_type=MESH)
                copy.wait_send()
                copy.wait_recv()

    res = pl.pallas_call(
        body,
        name=name,
        in_specs=[HBM_ONLY] * (2 * n) + [SEM_SPEC, SEM_SPEC] + [ANY_SPEC] * len(after),
        out_specs=[HBM_ONLY] * (2 * n),
        out_shape=[pltpu.HBM(b.shape, b.dtype) for b in (*halves, *lands)],
        input_output_aliases={a: a for a in range(2 * n)},
        compiler_params=pltpu.CompilerParams(has_side_effects=DATAFLOW),
    )(*halves, *lands, *sems, *after)
    return list(res[:n]), list(res[n:])


def _join_start(name, bufs):
    n = len(bufs)

    def body(*refs):
        ins, send_sems, recv_sems = refs[:n], refs[n], refs[n + 1]
        x, y, c, _ = _place()
        for a in range(n):
            pltpu.make_async_remote_copy(
                src_ref=ins[a].at[c], dst_ref=ins[a].at[c], send_sem=send_sems.at[a], recv_sem=recv_sems.at[a],
                device_id=(x, y, 1 - c), device_id_type=MESH).start()
        refs[-1][...] = jnp.zeros(TOKEN.shape, TOKEN.dtype)

    res = pl.pallas_call(
        body,
        name=name,
        in_specs=[HBM_ONLY] * n,
        out_specs=[SEM_SPEC, SEM_SPEC] + [HBM_ONLY] * n + [VMEM_SPEC],
        out_shape=[pltpu.SemaphoreType.DMA((n,)), pltpu.SemaphoreType.DMA((n,))]
        + [pltpu.HBM(b.shape, b.dtype) for b in bufs] + [TOKEN],
        input_output_aliases={a: 2 + a for a in range(n)},
        compiler_params=pltpu.CompilerParams(has_side_effects=DATAFLOW),
    )(*[_in_hbm(b) for b in bufs])
    return (res[0], res[1]), list(res[2:2 + n]), res[-1]


def _join_wait(name, bufs, sems, after):
    n = len(bufs)

    def body(*refs):
        ins, send_sems, recv_sems = refs[:n], refs[n], refs[n + 1]
        x, y, c, _ = _place()
        for a in range(n):
            copy = pltpu.make_async_remote_copy(
                src_ref=ins[a].at[c], dst_ref=ins[a].at[1 - c], send_sem=send_sems.at[a], recv_sem=recv_sems.at[a],
                device_id=(x, y, 1 - c), device_id_type=MESH)
            copy.wait_send()
            copy.wait_recv()

    res = pl.pallas_call(
        body,
        name=name,
        in_specs=[HBM_ONLY] * n + [SEM_SPEC, SEM_SPEC] + [ANY_SPEC] * len(after),
        out_specs=[HBM_ONLY] * n,
        out_shape=[pltpu.HBM(b.shape, b.dtype) for b in bufs],
        input_output_aliases={a: a for a in range(n)},
        compiler_params=pltpu.CompilerParams(has_side_effects=DATAFLOW),
    )(*bufs, *sems, *after)
    return list(res)


def _allgather_small(name, block, deps=()):
    m_per, n = block.shape

    def body(x_ref, *rest):
        out_ref, send_sems, recv_sems, local_sem = rest[len(deps):]
        x, y, c, chips = _place()
        me, sibling = (x, y, c), (x, y, 1 - c)

        def rows(px, py, pc):
            return out_ref.at[pl.ds((4 * px + 2 * py + pc) * m_per, m_per), :]

        def copy(k, blk, to, src=None):
            return pltpu.make_async_remote_copy(
                src_ref=rows(*blk) if src is None else src, dst_ref=rows(*blk),
                send_sem=send_sems.at[k], recv_sem=recv_sems.at[k], device_id=to, device_id_type=MESH)

        mine = pltpu.make_async_copy(x_ref, rows(*me), local_sem)
        mine.start()
        first = [copy(0, me, sibling, src=x_ref)]
        first += [copy(1 + j, me, (*chip, c), src=x_ref) for j, chip in enumerate(chips)]
        for cp in first:
            cp.start()
        passed = [copy(4 + j, (*chip, c), sibling) for j, chip in enumerate(chips)]
        for j, chip in enumerate(chips):
            copy(1 + j, (*chip, c), me).wait_recv()
            passed[j].start()
        copy(0, sibling, me).wait_recv()
        for j, chip in enumerate(chips):
            copy(4 + j, (*chip, 1 - c), me).wait_recv()
        for cp in first + passed:
            cp.wait_send()
        mine.wait()

    return pl.pallas_call(
        body,
        name=name,
        out_shape=jax.ShapeDtypeStruct((8 * m_per, n), block.dtype),
        in_specs=[pl.BlockSpec(memory_space=pltpu.VMEM)] + [ANY_SPEC] * len(deps),
        out_specs=pl.BlockSpec(memory_space=pltpu.VMEM),
        scratch_shapes=[pltpu.SemaphoreType.DMA((7,)), pltpu.SemaphoreType.DMA((7,)), pltpu.SemaphoreType.DMA],
    )(block, *deps)


def _sum_blocks(name, gathered, n_blocks):
    r = gathered.shape[0] // n_blocks
    c = gathered.shape[1]

    def body(g_ref, o_ref):
        acc = g_ref[0:r, :]
        for b in range(1, n_blocks):
            acc = acc + g_ref[b * r:(b + 1) * r, :]
        o_ref[...] = acc

    return pl.pallas_call(body, name=name, out_shape=jax.ShapeDtypeStruct((r, c), F32))(gathered)


def _largest_tile(n, cap, mult):
    best = None
    for d in range(mult, min(n, cap) + 1, mult):
        if n % d == 0:
            best = d
    assert best is not None, (n, cap, mult)
    return best


def kernel(x, meta, g_pre_mix, w_in, b_gates, conf_dw_w, conf_dw_b, conf_ln_g, conf_ln_b, conf_w_pw, short_dw_w, short_w_out, w_o, g_post_mix, g_pre_mlp, w_up, w_down, g_post_mlp, loss_target, m_meta, m_g_pre_mix, m_w_in, m_b_gates, m_conf_dw_w, m_conf_dw_b, m_conf_ln_g, m_conf_ln_b, m_conf_w_pw, m_short_dw_w, m_short_w_out, m_w_o, m_g_post_mix, m_g_pre_mlp, m_w_up, m_w_down, m_g_post_mlp, v_meta, v_g_pre_mix, v_w_in, v_b_gates, v_conf_dw_w, v_conf_dw_b, v_conf_ln_g, v_conf_ln_b, v_conf_w_pw, v_short_dw_w, v_short_w_out, v_w_o, v_g_post_mix, v_g_pre_mlp, v_w_up, v_w_down, v_g_post_mlp):
    seq, d = x.shape[1], x.shape[2]
    t_real = seq + N_META
    t = -(-t_real // LANES) * LANES
    d_conf = conf_dw_b.shape[1]
    d_ff = w_up.shape[2] * N_CHIPS
    in_cols = w_in.shape[2] * N_CHIPS
    assert in_cols == 5 * d_conf + 2 * d and d == 2 * d_conf
    cw = d_conf
    core = lax.axis_index("c")
    chip = 2 * lax.axis_index("x") + lax.axis_index("y")

    tr = _largest_tile(t, 272, ROW_CHUNK)
    tm = t
    tn_in = _largest_tile(in_cols // N_CHIPS, 768, LANES)
    tn_d = _largest_tile(d, 1024, LANES)
    tn_h = _largest_tile(d, 512, LANES)
    tn_ff = _largest_tile(d_ff // N_CHIPS, 1024, LANES)
    tn_pw = _largest_tile(d // N_CHIPS, 512, LANES)

    big = [w_in[0], conf_w_pw[0], short_w_out[0], w_o[0], w_up[0], w_down[0]]
    chip_arr = chip.astype(jnp.int32).reshape(1)

    def cast(a, deps):
        return _into_slot(f"cast_w{a}", big[a], N_CHIPS, chip_arr, MXU_DTYPE, _largest_tile(big[a].shape[0], 256, 16), deps)

    small = [_into_slot(f"place_w{a}", w, N_CHIPS, chip_arr, F32, w.shape[0])
             for a, w in enumerate([meta, conf_dw_w[0], short_dw_w[0]])]
    sems_small, fly_small, tok_small = _gather_start("gather_start_small", small, [False] * 3, [[0, 1, 2]])
    sems_in, fly_in, tok_in = _gather_start("gather_start_in", [cast(0, [tok_small])], [True], [[0]])
    rest_groups = [[0, 1], [2], [3], [4]]
    sems_rest, fly_rest, tok_rest = _gather_start(
        "gather_start_rest", [cast(a, [tok_in]) for a in range(1, 6)], [True] * 5, rest_groups)

    def arrive(g, after):
        members = rest_groups[g]
        got = _gather_wait(f"gather_wait_rest{g}", [fly_rest[a] for a in members], [True] * len(members),
                           sems_rest[g], after)
        return _pass_halves(f"gather_pass_rest{g}", got)

    meta_g, wdw_g, w3_g = _gather_wait("gather_wait_small", fly_small, [False] * 3, sems_small[0], [tok_in])
    meta_full = jnp.transpose(meta_g, (1, 0, 2)).reshape(N_META, d)
    wdw = jnp.transpose(wdw_g, (1, 0, 2)).reshape(CONF_KERNEL, d_conf)
    w3 = jnp.transpose(w3_g, (1, 0, 2)).reshape(SHORT_KERNEL, d_conf)

    tail = jnp.zeros((t - t_real, d), F32)
    h0 = jnp.concatenate([meta_full, x[0], tail], axis=0)
    target = jnp.concatenate([jnp.zeros((N_META, d), F32), loss_target[0], tail], axis=0)

    def norm_in(i, rows, vecs):
        return [_rms_fwd(rows[0], vecs[0])], []

    (n_lp,) = _rowwise("norm_in", norm_in, [(h0, d, 0)], [g_pre_mix], [(d, MXU_DTYPE)], [], tr, deps=[tok_rest])
    (wg_in,) = _pass_halves("gather_pass_in", _gather_wait("gather_wait_in", fly_in, [True], sems_in[0], [n_lp]))
    proj =_matmul("proj", n_lp, wg_in, kind="nn", tm=tm, tn=tn_in, tk=d, out_dtypes=[F32])
    ac, c3 = _conv_fwd(proj, wdw, conf_dw_b, w3, d_conf)

    def ln_parts(ac_t, ln_g, ln_b):
        mu = jnp.mean(ac_t, axis=-1, keepdims=True)
        xc = ac_t - mu
        rstd = lax.rsqrt(jnp.mean(xc * xc, axis=-1, keepdims=True) + LN_EPS)
        xh = xc * rstd
        return xh, rstd, xh * ln_g + ln_b

    def branch_act(i, rows, vecs):
        ac_t, c3_t, bg_t = rows
        _, _, al = ln_parts(ac_t, vecs[0], vecs[1])
        return [al * _sigmoid(al), bg_t * c3_t], []

    a_act, s_lp = _rowwise("branch_act", branch_act, [(ac, cw, 0), (c3, cw, 0), (proj, cw, 2)],
                           [conf_ln_g, conf_ln_b], [(d_conf, MXU_DTYPE), (d_conf, MXU_DTYPE)], [], tr)
    wg_pw, wg_sout = arrive(0, [a_act])
    y_a = _matmul("y_a", a_act, wg_pw, kind="nn", tm=tm, tn=tn_pw, tk=d_conf, out_dtypes=[F32])
    y_b = _matmul("y_b", s_lp, wg_sout, kind="nn", tm=tm, tn=tn_pw, tk=d_conf, out_dtypes=[F32])

    gate_rows = [(proj, cw, 5), (proj, cw, 6), (proj, cw, 7), (proj, cw, 8)]

    def gates_of(rows, b):
        ga = _sigmoid(jnp.concatenate([rows[0], rows[1]], axis=1) + b[:, :d])
        gb = _sigmoid(jnp.concatenate([rows[2], rows[3]], axis=1) + b[:, d:])
        return ga, gb

    def gate(i, rows, vecs):
        ga, gb = gates_of(rows[2:], vecs[0])
        return [ga * rows[0] + gb * rows[1]], []

    (m_lp,) = _rowwise("gate", gate, [(y_a, d, 0), (y_b, d, 0)] + gate_rows, [b_gates], [(d, MXU_DTYPE)], [], tr)
    wg_o = arrive(1, [m_lp])[0].reshape(d, d)
    mix = _matmul("mix", m_lp, wg_o, kind="nn", tm=tm, tn=tn_d, tk=d, out_dtypes=[F32])

    def post_mix(i, rows, vecs):
        h1_t = rows[0] + _rms_fwd(rows[1], vecs[0])
        return [h1_t, _rms_fwd(h1_t, vecs[1])], []

    h1, n2_lp = _rowwise("post_mix", post_mix, [(h0, d, 0), (mix, d, 0)], [g_post_mix, g_pre_mlp],
                         [(d, F32), (d, MXU_DTYPE)], [], tr)
    (wg_up,) = arrive(2, [n2_lp])
    up, f_lp = _matmul("up", n2_lp, wg_up, kind="nn", tm=tm, tn=tn_h, tk=d, out_dtypes=[F32, MXU_DTYPE],
                       epilogue=lambda acc: (acc, jnp.square(jnp.maximum(acc, 0.0))))
    wg_down = arrive(3, [f_lp])[0].reshape(d_ff, d)
    dn = _matmul("down", f_lp, wg_down, kind="nn", tm=tm, tn=tn_h, tk=_largest_tile(d_ff, 2048, LANES),
                 out_dtypes=[F32])

    def head(row0, rows, vecs):
        h1_t, dn_t, tgt = rows
        y = h1_t + _rms_fwd(dn_t, vecs[0])
        row = row0 + lax.broadcasted_iota(jnp.int32, (ROW_CHUNK, 1), 0)
        err = jnp.where(jnp.logical_and(row >= N_META, row < t_real), y - tgt, 0.0)
        dy = err / d
        d_dn, dg = _rms_bwd(dn_t, vecs[0], dy)
        loss_rows = 0.5 * jnp.mean(err * err, axis=-1, keepdims=True)
        return [dy, d_dn], [dg, jnp.broadcast_to(loss_rows, (ROW_CHUNK, LANES))]

    dy, d_dn, dg_post_mlp, loss_vec = _rowwise(
        "head", head, [(h1, d, 0), (dn, d, 0), (target, d, 0)], [g_post_mlp], [(d, F32), (d, MXU_DTYPE)], [d, LANES], tr)

    core_arr = core.astype(jnp.int32).reshape(1)
    place = jnp.stack([chip, core]).astype(jnp.int32)
    in_flight = {}


    def tag(members):
        return "".join(str(a) for a in members)

    own_halves = {}
    two_types = dict(out_dtypes=[F32, WIRE_DTYPE], epilogue=lambda acc: (acc, acc))

    def swap_start(members, gws):
        def in_halves(gw):
            return gw.reshape(N_CHIPS, 2, gw.shape[-2] // 2, gw.shape[-1])

        own_halves[members] = [in_halves(gw.reshape(N_CHIPS, -1, gw.shape[-1])) for gw, _ in gws]
        for_sibling = [in_halves(lp.reshape(N_CHIPS, -1, lp.shape[-1])) for _, lp in gws]
        *in_flight[members], token = _swap_start(f"swap_start{tag(members)}", for_sibling)
        return token

    def exchange_start(members, after):
        sems, halves, lands = in_flight[members]
        _, lands = _swap_wait(f"swap_wait{tag(members)}", halves, lands, sems, after)
        if len(members) > 1:
            pairs = _pair_sum_group(f"pair_sum{tag(members)}", own_halves[members], lands, core_arr)
        else:
            pairs = [_pair_sum(f"pair_sum{a}", h, q, core_arr, _largest_tile(q.shape[1], 256, 16))
                     for a, h, q in zip(members, own_halves[members], lands)]
        *in_flight[members], token = _exchange_start(f"exchange_start{tag(members)}", pairs)
        return token

    def reduce_finish(groups, after):
        waited = _exchange_wait("exchange_wait" + "_".join(tag(g) for g in groups), [in_flight[g] for g in groups], after)
        out = {}
        for g, (pairs, lands) in zip(groups, waited):
            if len(g) > 1:
                out.update(zip(g, _sum_pieces_group(f"sum_pieces{tag(g)}", pairs, lands, place)))
            else:
                out[g[0]] = _sum_pieces(f"sum_pieces{g[0]}", pairs[0], lands[0], place,
                                        _largest_tile(lands[0].shape[1], 256, 16))
        return out

    d_up = _matmul("d_up", d_dn, wg_down, kind="nt", tm=tm, tn=tn_h, tk=d, out_dtypes=[MXU_DTYPE], extras=[up],
                   epilogue=lambda acc, up_t: (acc * (2.0 * jnp.maximum(up_t, 0.0)),))
    tk_t = t
    gw_down = _matmul("gw_down", f_lp, d_dn, kind="tn", tm=_largest_tile(d_ff, 2048, LANES), tn=tn_h, tk=tk_t,
                      **two_types)
    tok = swap_start((5,), [gw_down])
    d_n2 = _matmul("d_n2", d_up, wg_up, kind="nt", tm=tm, tn=tn_h, tk=d_ff // N_CHIPS, out_dtypes=[F32], deps=[tok])
    tok = exchange_start((5,), [d_n2])
    gw_up = _matmul("gw_up", n2_lp, d_up, kind="tn", tm=_largest_tile(d, 2048, LANES), tn=tn_h, tk=tk_t,
                    out_pieces=N_CHIPS, deps=[tok], **two_types)
    tok = swap_start((4,), [gw_up])

    def bwd_mid(i, rows, vecs):
        dy_t, dn2_t, h1_t, mix_t = rows
        d_h1a, dg_pre_mlp = _rms_bwd(h1_t, vecs[1], dn2_t)
        d_h1 = dy_t + d_h1a
        d_mix, dg_post_mix = _rms_bwd(mix_t, vecs[0], d_h1)
        return [d_h1, d_mix], [dg_pre_mlp, dg_post_mix]

    d_h1, d_mix, dg_pre_mlp, dg_post_mix = _rowwise(
        "bwd_mid", bwd_mid, [(dy, d, 0), (d_n2, d, 0), (h1, d, 0), (mix, d, 0)], [g_post_mix, g_pre_mlp],
        [(d, F32), (d, MXU_DTYPE)], [d, d], tr, deps=[tok])
    d_m = _matmul("d_m", d_mix, wg_o, kind="nt", tm=tm, tn=tn_h, tk=d, out_dtypes=[F32])
    tok = exchange_start((4,), [d_m])
    gw_o = _matmul("gw_o", m_lp, d_mix, kind="tn", tm=_largest_tile(d, 2048, LANES), tn=tn_h, tk=tk_t, deps=[tok],
                   **two_types)

    def gate_bwd(i, rows, vecs):
        dm_t, ya_t, yb_t = rows[:3]
        ga, gb = gates_of(rows[3:], vecs[0])
        d_gpre = jnp.concatenate([dm_t * ya_t * ga * (1.0 - ga), dm_t * yb_t * gb * (1.0 - gb)], axis=1)
        return [dm_t * ga, dm_t * gb, d_gpre], [d_gpre]

    d_ya, d_yb, d_gpre, dg_b_gates = _rowwise(
        "gate_bwd", gate_bwd, [(d_m, d, 0), (y_a, d, 0), (y_b, d, 0)] + gate_rows, [b_gates],
        [(d, MXU_DTYPE), (d, MXU_DTYPE), (2 * d, MXU_DTYPE)], [2 * d], tr, deps=[tok])
    d_aact = _matmul("d_aact", d_ya, wg_pw, kind="nt", tm=tm, tn=d_conf // 2, tk=tn_pw, out_dtypes=[F32], deps=[tok])
    gw_pw = _matmul("gw_pw", a_act, d_ya, kind="tn", tm=_largest_tile(d_conf, 2048, LANES), tn=tn_pw, tk=tk_t,
                    out_pieces=N_CHIPS, **two_types)
    d_s = _matmul("d_s", d_yb, wg_sout, kind="nt", tm=tm, tn=d_conf // 2, tk=tn_pw, out_dtypes=[F32], deps=[tok])
    gw_sout = _matmul("gw_sout", s_lp, d_yb, kind="tn", tm=_largest_tile(d_conf, 2048, LANES), tn=tn_pw, tk=tk_t,
                      out_pieces=N_CHIPS, deps=[tok], **two_types)
    small_big = (3, 1, 2)
    tok = swap_start(small_big, [gw_o, gw_pw, gw_sout])

    def branch_bwd(i, rows, vecs):
        daact_t, ds_t, ac_t, c3_t, bg_t = rows
        xh, rstd, al = ln_parts(ac_t, vecs[0], vecs[1])
        sg = _sigmoid(al)
        d_al = daact_t * (sg * (1.0 + al * (1.0 - sg)))
        dxh = d_al * vecs[0]
        d_ac = rstd * (dxh - jnp.mean(dxh, axis=-1, keepdims=True) - xh * jnp.mean(dxh * xh, axis=-1, keepdims=True))
        return [d_ac, ds_t * bg_t, ds_t * c3_t], [d_al * xh, d_al, d_ac]

    d_ac, d_c3, d_bg, dg_ln_g, dg_ln_b, dg_dw_b = _rowwise(
        "branch_bwd", branch_bwd, [(d_aact, cw, 0), (d_s, cw, 0), (ac, cw, 0), (c3, cw, 0), (proj, cw, 2)],
        [conf_ln_g, conf_ln_b], [(d_conf, F32), (d_conf, F32), (d_conf, MXU_DTYPE)], [d_conf] * 3, tr, deps=[tok])
    d_av, d_ag, d_cg, d_v, dg_wdw, dg_w3 = _conv_bwd(proj, d_ac, d_c3, wdw, w3, d_conf)
    d_proj = jnp.concatenate([d_av, d_ag, d_bg, d_cg, d_v, d_gpre], axis=1)
    small_w = d_conf

    def pack(arrs):
        flat = jnp.concatenate([a.reshape(-1, small_w) for a in arrs], axis=0)
        return jnp.pad(flat, ((0, -flat.shape[0] % 8), (0, 0)))

    def reduce_small(tag, arrs, deps):
        assert all(a.size % small_w == 0 for a in arrs)
        gathered = _allgather_small(f"allgather_small_{tag}", pack(arrs), deps)
        first, r0 = [], 0
        for a in arrs:
            first.append(r0)
            r0 += a.size // small_w
        return _sum_blocks(f"sum_small_{tag}", gathered, 8), first

    def rows_of(total, r0, like):
        return total[r0:r0 + like.size // small_w].reshape(like.shape)

    tok = exchange_start(small_big, [d_proj])

    gw_in = _matmul("gw_in", n_lp, d_proj, kind="tn", tm=_largest_tile(d, 2048, LANES), tn=tn_in, tk=tk_t,
                    out_pieces=N_CHIPS, deps=[tok], **two_types)
    tok = swap_start((0,), [gw_in])
    d_n = _matmul("d_n", d_proj, wg_in, kind="nt", tm=tm, tn=tn_h, tk=in_cols // N_CHIPS, out_dtypes=[F32], deps=[tok])
    tok = exchange_start((0,), [d_n])

    early = [5, 4, 3, 1, 2]
    reduced = reduce_finish([(5,), (4,), small_big], [tok])
    join_sems, joining, tok = _join_start("join_start_early", [reduced[a] for a in early])

    def bwd_in(i, rows, vecs):
        d_h0a, dg = _rms_bwd(rows[2], vecs[0], rows[1])
        return [rows[0] + d_h0a], [dg]

    d_h0, dg_pre_mix = _rowwise("bwd_in", bwd_in, [(d_h1, d, 0), (d_n, d, 0), (h0, d, 0)], [g_pre_mix],
                                [(d, F32)], [d], tr, deps=[tok])
    grad_x = d_h0[N_META:t_real][None]

    big_m = [m_w_in, m_conf_w_pw, m_short_w_out, m_w_o, m_w_up, m_w_down]
    big_v = [v_w_in, v_conf_w_pw, v_short_w_out, v_w_o, v_w_up, v_w_down]
    big_res = {}

    def adam_group(members, joined):
        for a, j in zip(members, joined):
            big_res[a] = _adamw(f"adamw_big{a}", big[a], j.reshape(big[a].shape), big_m[a][0], big_v[a][0],
                                _largest_tile(big[a].shape[0], 256, 8))

    adam_group(early, _join_wait("join_wait_early", joining, join_sems, [d_h0]))
    reduced = reduce_finish([(0,)], [big_res[a][1] for a in early])
    join_sems, joining, tok = _join_start("join_start_late", [reduced[0]])
    rep_g = [dg_pre_mix, dg_b_gates, dg_dw_b, dg_ln_g, dg_ln_b, dg_post_mix, dg_pre_mlp, dg_post_mlp]
    rep_w = [g_pre_mix, b_gates, conf_dw_b, conf_ln_g, conf_ln_b, g_post_mix, g_pre_mlp, g_post_mlp]
    rep_m = [m_g_pre_mix, m_b_gates, m_conf_dw_b, m_conf_ln_g, m_conf_ln_b, m_g_post_mix, m_g_pre_mlp, m_g_post_mlp]
    rep_v = [v_g_pre_mix, v_b_gates, v_conf_dw_b, v_conf_ln_g, v_conf_ln_b, v_g_post_mix, v_g_pre_mlp, v_g_post_mlp]
    col_g = [d_h0[:N_META], dg_wdw, dg_w3]
    loss_row = jnp.tile(loss_vec, (1, small_w // LANES))
    total, first = reduce_small("all", rep_g + col_g + [loss_row], [tok])
    loss = total[first[-1], 0]
    rep_res = _small_adamw("adamw_rep", total, [(first[e], w, m, v) for e, (w, m, v) in enumerate(zip(rep_w, rep_m, rep_v))])
    col_res = {}
    for a, (w, g_part, m, v) in enumerate([(meta, col_g[0], m_meta, v_meta), (conf_dw_w[0], col_g[1], m_conf_dw_w[0], v_conf_dw_w[0]),
                                           (short_dw_w[0], col_g[2], m_short_dw_w[0], v_short_dw_w[0])]):
        g_full = rows_of(total, first[len(rep_g) + a], g_part)
        g_own = lax.dynamic_slice_in_dim(g_full, chip * w.shape[1], w.shape[1], axis=1)
        col_res[a] = _adamw(f"adamw_col{a}", w, g_own, m, v, w.shape[0])
    adam_group([0], _join_wait("join_wait_late", joining, join_sems, [rep_res[0][1]] + [col_res[a][1] for a in range(3)]))

    def leaf(q):
        r = lambda a: rep_res[a][q]
        b = lambda a: big_res[a][q][None]
        return [col_res[0][q], r(0), b(0), r(1), col_res[1][q][None], r(2), r(3), r(4), b(1), col_res[2][q][None], b(2),
                b(3), r(5), r(6), b(4), b(5), r(7)]

    return (loss, grad_x, *leaf(0), *leaf(1), *leaf(2), *leaf(3))
```

```python
import jax
import jax.numpy as jnp
from jax import lax
from jax.experimental import pallas as pl
from jax.experimental.pallas import tpu as pltpu

F32 = jnp.float32
BF16 = jnp.bfloat16
MXU_DTYPE = BF16
WIRE_DTYPE = BF16

N_META = 16
CONF_KERNEL = 31
SHORT_KERNEL = 3
CONV_PAD = 32
RMS_EPS = 1e-6
LN_EPS = 1e-5
ADAM_LR = 0.001
ADAM_B1 = 0.9
ADAM_B2 = 0.999
ADAM_EPS = 1e-08
ADAM_WD = 0.01
ADAM_STEP = 10

N_CHIPS = 4
MESH = pl.DeviceIdType.MESH
LANES = 128


def _sigmoid(z):
    return 1.0 / (1.0 + jnp.exp(-z))


ANY_SPEC = pl.BlockSpec(memory_space=pl.ANY)


def _matmul(name, a, b, *, kind, tm, tn, tk, out_dtypes, out_pieces=1, epilogue=None, extras=(), deps=()):
    pieces = b.shape[0] if b.ndim == 3 else 1
    if kind == "nn":
        m, kdim = a.shape
        n = b.shape[-1] * pieces
        dims = (((1,), (0,)), ((), ()))
        a_spec = pl.BlockSpec((tm, tk), lambda i, j, k: (i, k))
        if b.ndim == 2:
            b_spec = pl.BlockSpec((tk, tn), lambda i, j, k: (k, j))
        else:
            npp = b.shape[-1] // tn
            b_spec = pl.BlockSpec((None, tk, tn), lambda i, j, k: (j // npp, k, j % npp))
    elif kind == "nt":
        m, kdim = a.shape
        n = b.shape[-2]
        dims = (((1,), (1,)), ((), ()))
        a_spec = pl.BlockSpec((tm, tk), lambda i, j, k: (i, k))
        if b.ndim == 2:
            b_spec = pl.BlockSpec((tn, tk), lambda i, j, k: (j, k))
        else:
            kpp = b.shape[-1] // tk
            b_spec = pl.BlockSpec((None, tn, tk), lambda i, j, k: (k // kpp, j, k % kpp))
    else:
        kdim, m = a.shape
        n = b.shape[-1]
        dims = (((0,), (0,)), ((), ()))
        a_spec = pl.BlockSpec((tk, tm), lambda i, j, k: (k, i))
        b_spec = pl.BlockSpec((tk, tn), lambda i, j, k: (k, j))
    assert m % tm == 0 and n % tn == 0 and kdim % tk == 0, (name, m, n, kdim, tm, tn, tk)
    nk = kdim // tk
    if out_pieces == 1:
        out_shape = (m, n)
        out_spec = pl.BlockSpec((tm, tn), lambda i, j, k: (i, j))
    else:
        onpp = n // out_pieces // tn
        out_shape = (out_pieces, m, n // out_pieces)
        out_spec = pl.BlockSpec((None, tm, tn), lambda i, j, k: (j // onpp, i, j % onpp))
    n_ex, n_out, n_in = len(extras), len(out_dtypes), len(extras) + len(deps)
    if epilogue is None:
        epilogue = lambda acc: (acc,)

    def body(a_ref, b_ref, *rest):
        ex_refs, o_refs = rest[:n_ex], rest[n_in:n_in + n_out]
        prod = lax.dot_general(a_ref[...], b_ref[...], dims, preferred_element_type=F32)

        def finish(acc):
            tiles = epilogue(acc, *[r[...] for r in ex_refs])
            for o_ref, t in zip(o_refs, tiles):
                o_ref[...] = t.astype(o_ref.dtype)

        if nk == 1:
            finish(prod)
        else:
            acc_ref = rest[n_in + n_out]
            k = pl.program_id(2)

            @pl.when(k == 0)
            def _():
                acc_ref[...] = prod

            @pl.when(jnp.logical_and(k > 0, k < nk - 1))
            def _():
                acc_ref[...] += prod

            @pl.when(k == nk - 1)
            def _():
                finish(acc_ref[...] + prod)

    ex_specs = [pl.BlockSpec((tm, tn), lambda i, j, k: (i, j)) for _ in extras]
    res = pl.pallas_call(
        body,
        name=name,
        grid=(m // tm, n // tn, nk),
        in_specs=[a_spec, b_spec, *ex_specs] + [ANY_SPEC] * len(deps),
        out_specs=[out_spec] * n_out,
        out_shape=[jax.ShapeDtypeStruct(out_shape, d) for d in out_dtypes],
        scratch_shapes=[pltpu.VMEM((tm, tn), F32)] if nk > 1 else [],
        compiler_params=pltpu.CompilerParams(dimension_semantics=("parallel", "parallel", "arbitrary")),
    )(a, b, *extras, *deps)
    return res[0] if n_out == 1 else res


STREAM_BUFFERS = 3
ROW_CHUNK = 16
SUBLANES = 8


def _rowwise(name, fn, rows, vecs, outs, sums, tr, deps=()):
    t = rows[0][0].shape[0]
    assert t % tr == 0 and tr % ROW_CHUNK == 0
    n_r, n_v, n_o, n_s = len(rows), len(vecs), len(outs), len(sums)
    n_in = n_r + n_v + len(deps)
    n_steps = t // tr

    def body(*refs):
        r_in, v_in = refs[:n_r], refs[n_r:n_r + n_v]
        o_refs = refs[n_in:n_in + n_o]
        s_refs = refs[n_in + n_o:n_in + n_o + n_s]
        acc_refs = refs[n_in + n_o + n_s:]
        i = pl.program_id(0)

        @pl.when(i == 0)
        def _():
            for acc_ref in acc_refs:
                acc_ref[...] = jnp.zeros(acc_ref.shape, F32)

        def chunk(ci):
            r0 = ci * ROW_CHUNK
            sl = pl.ds(r0, ROW_CHUNK)
            o_tiles, s_tiles = fn(i * tr + r0, [r[sl, :] for r in r_in], [v[...] for v in v_in])
            for o_ref, tile in zip(o_refs, o_tiles):
                o_ref[sl, :] = tile.astype(o_ref.dtype)
            for acc_ref, tile in zip(acc_refs, s_tiles):
                part = tile[0:SUBLANES]
                for s in range(1, ROW_CHUNK // SUBLANES):
                    part = part + tile[s * SUBLANES:(s + 1) * SUBLANES]
                acc_ref[...] += part

        for ci in range(tr // ROW_CHUNK):
            chunk(ci)

        @pl.when(i == n_steps - 1)
        def _():
            for s_ref, acc_ref in zip(s_refs, acc_refs):
                s_ref[...] = jnp.sum(acc_ref[...], axis=0, keepdims=True)

    def row_spec(width, blk):
        return pl.BlockSpec((tr, width), lambda i: (i, blk))

    res = pl.pallas_call(
        body,
        name=name,
        grid=(t // tr,),
        in_specs=[row_spec(w, blk) for _, w, blk in rows]
        + [pl.BlockSpec(v.shape, lambda i: (0, 0)) for v in vecs] + [ANY_SPEC] * len(deps),
        out_specs=[pl.BlockSpec((tr, c), lambda i: (i, 0)) for c, _ in outs]
        + [pl.BlockSpec((1, c), lambda i: (0, 0)) for c in sums],
        out_shape=[jax.ShapeDtypeStruct((t, c), d) for c, d in outs]
        + [jax.ShapeDtypeStruct((1, c), F32) for c in sums],
        scratch_shapes=[pltpu.VMEM((SUBLANES, c), F32) for c in sums],
        compiler_params=pltpu.CompilerParams(dimension_semantics=("arbitrary",)),
    )(*[r[0] for r in rows], *vecs, *deps)
    return res


def _rms_fwd(x, g):
    r = lax.rsqrt(jnp.mean(x * x, axis=-1, keepdims=True) + RMS_EPS)
    return x * r * g


def _rms_bwd(x, g, dy):
    r = lax.rsqrt(jnp.mean(x * x, axis=-1, keepdims=True) + RMS_EPS)
    xn = x * r
    dxn = dy * g
    dx = r * (dxn - xn * jnp.mean(dxn * xn, axis=-1, keepdims=True))
    return dx, dy * xn


CONV_ROWS = 64
CONV_LANES = 128


def _conv_fwd(proj, wdw, bdw, w3, d_conf):
    t = proj.shape[0]
    cl = CONV_LANES
    nb = d_conf // cl
    nchunk = t // CONV_ROWS
    assert t % CONV_ROWS == 0

    def body(av_ref, ag_ref, cg_ref, v_ref, wdw_ref, bdw_ref, w3_ref, ac_ref, c3_ref, apad, cpad):
        zeros = jnp.zeros((CONV_PAD, cl), F32)
        apad[0:CONV_PAD, :] = zeros
        cpad[0:CONV_PAD, :] = zeros
        apad[CONV_PAD:, :] = av_ref[...] * _sigmoid(ag_ref[...])
        cpad[CONV_PAD:, :] = cg_ref[...] * v_ref[...]

        def chunk(ci, carry):
            base = pl.multiple_of(ci * CONV_ROWS, 8)
            acc = jnp.zeros((CONV_ROWS, cl), F32) + bdw_ref[...]
            for k in range(CONF_KERNEL):
                off = CONV_PAD - (CONF_KERNEL - 1) + k
                acc = acc + apad[pl.ds(base + off, CONV_ROWS), :] * wdw_ref[k:k + 1, :]
            ac_ref[pl.ds(base, CONV_ROWS), :] = acc
            acc3 = jnp.zeros((CONV_ROWS, cl), F32)
            for k in range(SHORT_KERNEL):
                off = CONV_PAD - (SHORT_KERNEL - 1) + k
                acc3 = acc3 + cpad[pl.ds(base + off, CONV_ROWS), :] * w3_ref[k:k + 1, :]
            c3_ref[pl.ds(base, CONV_ROWS), :] = acc3
            return carry

        lax.fori_loop(0, nchunk, chunk, 0)

    def col(blk0):
        return pl.BlockSpec((t, cl), lambda j: (0, blk0 + j))

    return pl.pallas_call(
        body,
        name="conv_fwd",
        grid=(nb,),
        in_specs=[col(0), col(nb), col(3 * nb), col(4 * nb),
                  pl.BlockSpec((CONF_KERNEL, cl), lambda j: (0, j)),
                  pl.BlockSpec((1, cl), lambda j: (0, j)),
                  pl.BlockSpec((SHORT_KERNEL, cl), lambda j: (0, j))],
        out_specs=[pl.BlockSpec((t, cl), lambda j: (0, j))] * 2,
        out_shape=[jax.ShapeDtypeStruct((t, d_conf), F32)] * 2,
        scratch_shapes=[pltpu.VMEM((t + CONV_PAD, cl), F32)] * 2,
        compiler_params=pltpu.CompilerParams(dimension_semantics=("parallel",)),
    )(proj, proj, proj, proj, wdw, bdw, w3)


def _conv_bwd(proj, d_ac, d_c3, wdw, w3, d_conf):
    t = proj.shape[0]
    cl = CONV_LANES
    nb = d_conf // cl
    nchunk = t // CONV_ROWS
    nsub = CONV_ROWS // 8

    def fold(p):
        r = p[0:8]
        for s in range(1, nsub):
            r = r + p[8 * s:8 * s + 8]
        return r

    def body(av_ref, ag_ref, cg_ref, v_ref, dac_ref, dc3_ref, wdw_ref, w3_ref,
             dav_ref, dag_ref, dcg_ref, dv_ref, dwdw_ref, dw3_ref, apad, cpad, dapad, dcpad):
        zeros = jnp.zeros((CONV_PAD, cl), F32)
        apad[0:CONV_PAD, :] = zeros
        cpad[0:CONV_PAD, :] = zeros
        apad[CONV_PAD:, :] = av_ref[...] * _sigmoid(ag_ref[...])
        cpad[CONV_PAD:, :] = cg_ref[...] * v_ref[...]
        dapad[0:t, :] = dac_ref[...]
        dcpad[0:t, :] = dc3_ref[...]
        dapad[t:, :] = zeros
        dcpad[t:, :] = zeros

        def chunk(ci, accs):
            base = pl.multiple_of(ci * CONV_ROWS, 8)
            rows = pl.ds(base, CONV_ROWS)
            da = jnp.zeros((CONV_ROWS, cl), F32)
            for k in range(CONF_KERNEL):
                da = da + dapad[pl.ds(base + (CONF_KERNEL - 1 - k), CONV_ROWS), :] * wdw_ref[k:k + 1, :]
            dcv = jnp.zeros((CONV_ROWS, cl), F32)
            for k in range(SHORT_KERNEL):
                dcv = dcv + dcpad[pl.ds(base + (SHORT_KERNEL - 1 - k), CONV_ROWS), :] * w3_ref[k:k + 1, :]
            av, sg = av_ref[rows, :], _sigmoid(ag_ref[rows, :])
            dav_ref[rows, :] = (da * sg).astype(dav_ref.dtype)
            dag_ref[rows, :] = (da * av * sg * (1.0 - sg)).astype(dag_ref.dtype)
            dcg_ref[rows, :] = (dcv * v_ref[rows, :]).astype(dcg_ref.dtype)
            dv_ref[rows, :] = (dcv * cg_ref[rows, :]).astype(dv_ref.dtype)
            d_out, d_out3 = dac_ref[rows, :], dc3_ref[rows, :]
            new = []
            for k in range(CONF_KERNEL):
                off = CONV_PAD - (CONF_KERNEL - 1) + k
                new.append(accs[k] + fold(d_out * apad[pl.ds(base + off, CONV_ROWS), :]))
            for k in range(SHORT_KERNEL):
                off = CONV_PAD - (SHORT_KERNEL - 1) + k
                new.append(accs[CONF_KERNEL + k] + fold(d_out3 * cpad[pl.ds(base + off, CONV_ROWS), :]))
            return tuple(new)

        init = tuple(jnp.zeros((8, cl), F32) for _ in range(CONF_KERNEL + SHORT_KERNEL))
        accs = lax.fori_loop(0, nchunk, chunk, init)
        for k in range(CONF_KERNEL):
            dwdw_ref[k:k + 1, :] = jnp.sum(accs[k], axis=0, keepdims=True)
        for k in range(SHORT_KERNEL):
            dw3_ref[k:k + 1, :] = jnp.sum(accs[CONF_KERNEL + k], axis=0, keepdims=True)

    def col(blk0):
        return pl.BlockSpec((t, cl), lambda j: (0, blk0 + j))

    own = pl.BlockSpec((t, cl), lambda j: (0, j))
    return pl.pallas_call(
        body,
        name="conv_bwd",
        grid=(nb,),
        in_specs=[col(0), col(nb), col(3 * nb), col(4 * nb), own, own,
                  pl.BlockSpec((CONF_KERNEL, cl), lambda j: (0, j)),
                  pl.BlockSpec((SHORT_KERNEL, cl), lambda j: (0, j))],
        out_specs=[own] * 4 + [pl.BlockSpec((CONF_KERNEL, cl), lambda j: (0, j)),
                               pl.BlockSpec((SHORT_KERNEL, cl), lambda j: (0, j))],
        out_shape=[jax.ShapeDtypeStruct((t, d_conf), MXU_DTYPE)] * 4
        + [jax.ShapeDtypeStruct((CONF_KERNEL, d_conf), F32), jax.ShapeDtypeStruct((SHORT_KERNEL, d_conf), F32)],
        scratch_shapes=[pltpu.VMEM((t + CONV_PAD, cl), F32)] * 4,
        compiler_params=pltpu.CompilerParams(dimension_semantics=("parallel",)),
    )(proj, proj, proj, proj, d_ac, d_c3, wdw, w3)


def _elementwise(name, fn, ins, out_dtypes, tr, deps=()):
    ins = [(a, ()) if not isinstance(a, tuple) else a for a in ins]
    r, c = ins[0][0].shape[-2:]
    assert r % tr == 0, (name, r, tr)
    n_in = len(ins)

    n_out = len(out_dtypes)

    def step(*refs):
        tiles = fn(*[x[...] for x in refs[:n_in]])
        for o_ref, tile in zip(refs[n_in:], tiles):
            o_ref[...] = tile.astype(o_ref.dtype)

    def spec(lead):
        return pl.BlockSpec((None,) * len(lead) + (tr, c), lambda i: (*lead, i, 0),
                            pipeline_mode=pl.Buffered(STREAM_BUFFERS))

    def body(*refs):
        pltpu.emit_pipeline(
            step, grid=(r // tr,), in_specs=[spec(lead) for _, lead in ins],
            out_specs=[pl.BlockSpec((tr, c), lambda i: (i, 0))] * n_out,
        )(*refs[:n_in], *refs[n_in + len(deps):])

    res = pl.pallas_call(
        body,
        name=name,
        in_specs=[ANY_SPEC] * (n_in + len(deps)),
        out_specs=[ANY_SPEC] * n_out,
        out_shape=[jax.ShapeDtypeStruct((r, c), d) for d in out_dtypes],
    )(*[a for a, _ in ins], *deps)
    return res


def _adamw_tiles(w, g, m, v):
    m = ADAM_B1 * m + (1.0 - ADAM_B1) * g
    v = ADAM_B2 * v + (1.0 - ADAM_B2) * jnp.square(g)
    m_hat = m / (1.0 - ADAM_B1 ** ADAM_STEP)
    v_hat = v / (1.0 - ADAM_B2 ** ADAM_STEP)
    delta = -ADAM_LR * (m_hat / (jnp.sqrt(v_hat) + ADAM_EPS) + ADAM_WD * w)
    return g, delta, m, v


def _adamw(name, w, g, m, v, tr, deps=()):
    shape = w.shape
    flat = [a.reshape(shape[-2:]) if a.ndim > 2 else a for a in (w, g, m, v)]
    res = _elementwise(name, _adamw_tiles, flat, [F32] * 4, tr, deps)
    return [a.reshape(shape) for a in res]


def _small_adamw(name, total, entries):
    sw = total.shape[1]
    n = len(entries)

    def body(total_ref, *refs):
        ins, outs = refs[:3 * n], refs[3 * n:]
        for e, (row0, w, _, _) in enumerate(entries):
            for q in range(w.shape[1] // sw):
                cols = slice(q * sw, (q + 1) * sw)
                tiles = _adamw_tiles(ins[3 * e][:, cols], total_ref[row0 + q:row0 + q + 1, :],
                                     ins[3 * e + 1][:, cols], ins[3 * e + 2][:, cols])
                for o_ref, tile in zip(outs[4 * e:4 * e + 4], tiles):
                    o_ref[:, cols] = tile

    res = pl.pallas_call(
        body,
        name=name,
        out_shape=[jax.ShapeDtypeStruct(w.shape, F32) for _, w, _, _ in entries for _ in range(4)],
    )(total, *[a for _, w, m, v in entries for a in (w, m, v)])
    return [list(res[4 * e:4 * e + 4]) for e in range(n)]


def _pair_sum(name, p, q, core, tr):
    n_p, _, hr, c = p.shape
    assert hr % tr == 0

    def body(core_ref, p_ref, q_ref, o_ref):
        o_ref[...] = (p_ref[...] + q_ref[...].astype(F32)).astype(o_ref.dtype)

    return pl.pallas_call(
        body,
        name=name,
        grid_spec=pltpu.PrefetchScalarGridSpec(
            num_scalar_prefetch=1,
            grid=(n_p, hr // tr),
            in_specs=[pl.BlockSpec((None, None, tr, c), lambda a, i, core_ref: (a, core_ref[0], i, 0)),
                      pl.BlockSpec((None, tr, c), lambda a, i, core_ref: (a, i, 0))],
            out_specs=pl.BlockSpec((None, tr, c), lambda a, i, core_ref: (a, i, 0)),
        ),
        out_shape=jax.ShapeDtypeStruct((n_p, hr, c), WIRE_DTYPE),
        compiler_params=pltpu.CompilerParams(dimension_semantics=("parallel", "parallel")),
    )(core, p, q)


def _into_slot(name, w, slots, slot, dtype, tr, deps=()):
    r, c = w.shape
    assert r % tr == 0

    def body(slot_ref, w_ref, *rest):
        o_ref = rest[len(deps)]
        o_ref[...] = w_ref[...].astype(o_ref.dtype)

    return pl.pallas_call(
        body,
        name=name,
        grid_spec=pltpu.PrefetchScalarGridSpec(
            num_scalar_prefetch=1,
            grid=(r // tr,),
            in_specs=[pl.BlockSpec((tr, c), lambda i, slot_ref: (i, 0))] + [ANY_SPEC] * len(deps),
            out_specs=pl.BlockSpec((None, tr, c), lambda i, slot_ref: (slot_ref[0], i, 0)),
        ),
        out_shape=jax.ShapeDtypeStruct((slots, r, c), dtype),
        compiler_params=pltpu.CompilerParams(dimension_semantics=("parallel",)),
    )(slot, w, *deps)


def _sum_pieces(name, own, rb, place, tr):
    n_p, hr, c = rb.shape
    assert hr % tr == 0

    def body(place_ref, own_ref, *refs):
        chip = place_ref[0]
        acc = None
        for k in range(n_p):
            tile = jnp.where(chip == k, own_ref[...], refs[k][...]).astype(F32)
            acc = tile if acc is None else acc + tile
        refs[n_p][...] = acc

    def landed(k):
        return pl.BlockSpec((None, tr, c), lambda i, place_ref: (jnp.where(place_ref[0] == k, (k + 1) % n_p, k), i, 0))

    return pl.pallas_call(
        body,
        name=name,
        grid_spec=pltpu.PrefetchScalarGridSpec(
            num_scalar_prefetch=1,
            grid=(hr // tr,),
            in_specs=[pl.BlockSpec((None, tr, c), lambda i, place_ref: (place_ref[0], i, 0))]
            + [landed(k) for k in range(n_p)],
            out_specs=pl.BlockSpec((None, tr, c), lambda i, place_ref: (place_ref[1], i, 0)),
        ),
        out_shape=jax.ShapeDtypeStruct((2, hr, c), F32),
        compiler_params=pltpu.CompilerParams(dimension_semantics=("parallel",)),
    )(place, own, *([rb] * n_p))


def _pair_sum_group(name, ps, qs, core):
    n = len(ps)

    def body(core_ref, *refs):
        for s in range(n):
            refs[2 * n + s][...] = (refs[s][...] + refs[n + s][...].astype(F32)).astype(WIRE_DTYPE)

    def piece(shape):
        return pl.BlockSpec((None, *shape), lambda a, core_ref: (a, 0, 0))

    return pl.pallas_call(
        body,
        name=name,
        grid_spec=pltpu.PrefetchScalarGridSpec(
            num_scalar_prefetch=1,
            grid=(ps[0].shape[0],),
            in_specs=[pl.BlockSpec((None, None, *p.shape[2:]), lambda a, core_ref: (a, core_ref[0], 0, 0)) for p in ps]
            + [piece(q.shape[1:]) for q in qs],
            out_specs=[piece(q.shape[1:]) for q in qs],
        ),
        out_shape=[jax.ShapeDtypeStruct(q.shape, WIRE_DTYPE) for q in qs],
        compiler_params=pltpu.CompilerParams(dimension_semantics=("parallel",)),
    )(core, *ps, *qs)


def _sum_pieces_group(name, owns, rbs, place):
    n = len(owns)
    n_p = rbs[0].shape[0]

    def body(place_ref, *refs):
        chip = place_ref[0]
        for s in range(n):
            lands = refs[n + s * n_p:n + (s + 1) * n_p]
            acc = None
            for k in range(n_p):
                tile = jnp.where(chip == k, refs[s][...], lands[k][...]).astype(F32)
                acc = tile if acc is None else acc + tile
            refs[n + n * n_p + s][...] = acc

    def landed(k, shape):
        return pl.BlockSpec((None, *shape), lambda i, place_ref: (jnp.where(place_ref[0] == k, (k + 1) % n_p, k), 0, 0))

    return pl.pallas_call(
        body,
        name=name,
        grid_spec=pltpu.PrefetchScalarGridSpec(
            num_scalar_prefetch=1,
            grid=(1,),
            in_specs=[pl.BlockSpec((None, *o.shape[1:]), lambda i, place_ref: (place_ref[0], 0, 0)) for o in owns]
            + [landed(k, rb.shape[1:]) for rb in rbs for k in range(n_p)],
            out_specs=[pl.BlockSpec((None, *rb.shape[1:]), lambda i, place_ref: (place_ref[1], 0, 0)) for rb in rbs],
        ),
        out_shape=[jax.ShapeDtypeStruct((2, *rb.shape[1:]), F32) for rb in rbs],
        compiler_params=pltpu.CompilerParams(dimension_semantics=("arbitrary",)),
    )(place, *owns, *[rb for rb in rbs for _ in range(n_p)])


HBM_SPEC = pl.BlockSpec(memory_space=pl.ANY)


def _place():
    x, y, c = lax.axis_index("x"), lax.axis_index("y"), lax.axis_index("c")
    chips = [(1 - x, y), (x, 1 - y), (1 - x, 1 - y)]
    return x, y, c, chips


HBM_ONLY = pl.BlockSpec(memory_space=pltpu.HBM)
SEM_SPEC = pl.BlockSpec(memory_space=pltpu.SEMAPHORE)
DATAFLOW = pltpu.SideEffectType.DATAFLOW_SIDE_EFFECTING


def _in_hbm(a):
    return pltpu.with_memory_space_constraint(a, pltpu.HBM)


def _shard_part(ref, is_split, slot, h):
    if not is_split:
        return ref.at[slot]
    hr = ref.shape[1] // 2
    return ref.at[slot, pl.ds(h * hr, hr), :]


TOKEN = jax.ShapeDtypeStruct((8, LANES), F32)
VMEM_SPEC = pl.BlockSpec(memory_space=pltpu.VMEM)


def _gather_start(name, bufs, split, groups, deps=()):
    n, ng = len(bufs), len(groups)

    def body(*refs):
        ins, sems = refs[:n], refs[n + len(deps):n + len(deps) + 2 * ng]
        refs[-1][...] = jnp.zeros(TOKEN.shape, TOKEN.dtype)
        x, y, c, chips = _place()
        me = 2 * x + y
        for g, members in enumerate(groups):
            for s, a in enumerate(members):
                mine = _shard_part(ins[a], split[a], me, c)
                for j, chip in enumerate(chips):
                    pltpu.make_async_remote_copy(
                        src_ref=mine, dst_ref=mine, send_sem=sems[2 * g].at[3 * s + j], recv_sem=sems[2 * g + 1].at[3 * s + j],
                        device_id=(*chip, c), device_id_type=MESH).start()

    res = pl.pallas_call(
        body,
        name=name,
        in_specs=[HBM_ONLY] * n + [ANY_SPEC] * len(deps),
        out_specs=[SEM_SPEC] * (2 * ng) + [HBM_ONLY] * n + [VMEM_SPEC],
        out_shape=[pltpu.SemaphoreType.DMA((3 * len(members),)) for members in groups for _ in range(2)]
        + [pltpu.HBM(b.shape, b.dtype) for b in bufs] + [TOKEN],
        input_output_aliases={a: 2 * ng + a for a in range(n)},
        compiler_params=pltpu.CompilerParams(has_side_effects=DATAFLOW),
    )(*[_in_hbm(b) for b in bufs], *deps)
    return [(res[2 * g], res[2 * g + 1]) for g in range(ng)], list(res[2 * ng:2 * ng + n]), res[-1]


def _gather_wait(name, bufs, split, sems, after):
    n = len(bufs)

    def body(*refs):
        ins, send_sems, recv_sems = refs[:n], refs[n], refs[n + 1]
        x, y, c, chips = _place()
        me = 2 * x + y
        for s in range(n):
            for j, chip in enumerate(chips):
                copy = pltpu.make_async_remote_copy(
                    src_ref=_shard_part(ins[s], split[s], me, c),
                    dst_ref=_shard_part(ins[s], split[s], 2 * chip[0] + chip[1], c),
                    send_sem=send_sems.at[3 * s + j], recv_sem=recv_sems.at[3 * s + j],
                    device_id=(*chip, c), device_id_type=MESH)
                copy.wait_send()
                copy.wait_recv()

    res = pl.pallas_call(
        body,
        name=name,
        in_specs=[HBM_ONLY] * n + [SEM_SPEC, SEM_SPEC] + [ANY_SPEC] * len(after),
        out_specs=[HBM_ONLY] * n,
        out_shape=[pltpu.HBM(b.shape, b.dtype) for b in bufs],
        input_output_aliases={a: a for a in range(n)},
        compiler_params=pltpu.CompilerParams(has_side_effects=DATAFLOW),
    )(*bufs, *sems, *after)
    return list(res)


def _pass_halves(name, bufs):
    n = len(bufs)

    def body(*refs):
        outs = refs[n:2 * n]
        send_sems, recv_sems = refs[2 * n:]
        x, y, c, chips = _place()
        sibling = (x, y, 1 - c)

        def copy(a, j, h):
            blk = _shard_part(outs[a], True, 2 * chips[j][0] + chips[j][1], h)
            return pltpu.make_async_remote_copy(
                src_ref=blk, dst_ref=blk, send_sem=send_sems.at[3 * a + j], recv_sem=recv_sems.at[3 * a + j],
                device_id=sibling, device_id_type=MESH)

        sends = [copy(a, j, c) for a in range(n) for j in range(3)]
        for cp in sends:
            cp.start()
        for a in range(n):
            for j in range(3):
                copy(a, j, 1 - c).wait_recv()
        for cp in sends:
            cp.wait_send()

    res = pl.pallas_call(
        body,
        name=name,
        in_specs=[HBM_SPEC] * n,
        out_specs=[HBM_SPEC] * n,
        out_shape=[jax.ShapeDtypeStruct(b.shape, b.dtype) for b in bufs],
        input_output_aliases={a: a for a in range(n)},
        scratch_shapes=[pltpu.SemaphoreType.DMA((3 * n,)), pltpu.SemaphoreType.DMA((3 * n,))],
    )(*bufs)
    return list(res)


def _exchange_start(name, pairs):
    n = len(pairs)

    def body(*refs):
        pair_refs, land_refs, send_sems, recv_sems = refs[:n], refs[n:2 * n], refs[2 * n], refs[2 * n + 1]
        x, y, c, chips = _place()
        me = 2 * x + y
        for a in range(n):
            for j, chip in enumerate(chips):
                pltpu.make_async_remote_copy(
                    src_ref=pair_refs[a].at[2 * chip[0] + chip[1]], dst_ref=land_refs[a].at[me],
                    send_sem=send_sems.at[3 * a + j], recv_sem=recv_sems.at[3 * a + j],
                    device_id=(*chip, c), device_id_type=MESH).start()
        refs[-1][...] = jnp.zeros(TOKEN.shape, TOKEN.dtype)

    res = pl.pallas_call(
        body,
        name=name,
        in_specs=[HBM_ONLY] * (2 * n),
        out_specs=[SEM_SPEC, SEM_SPEC] + [HBM_ONLY] * (2 * n) + [VMEM_SPEC],
        out_shape=[pltpu.SemaphoreType.DMA((3 * n,)), pltpu.SemaphoreType.DMA((3 * n,))]
        + [pltpu.HBM(p.shape, p.dtype) for p in pairs] * 2 + [TOKEN],
        input_output_aliases={a: 2 + a for a in range(2 * n)},
        compiler_params=pltpu.CompilerParams(has_side_effects=DATAFLOW),
    )(*[_in_hbm(p) for p in pairs], *[_in_hbm(lax.empty(p.shape, p.dtype)) for p in pairs])
    return (res[0], res[1]), list(res[2:2 + n]), list(res[2 + n:2 + 2 * n]), res[-1]


def _exchange_wait(name, groups, after):
    sizes = [len(pairs) for _, pairs, _ in groups]
    n_buf = 2 * sum(sizes)

    def body(*refs):
        x, y, c, chips = _place()
        at_buf, at_sem = 0, n_buf
        for n in sizes:
            pair_refs, land_refs = refs[at_buf:at_buf + n], refs[at_buf + n:at_buf + 2 * n]
            send_sems, recv_sems = refs[at_sem], refs[at_sem + 1]
            at_buf, at_sem = at_buf + 2 * n, at_sem + 2
            for a in range(n):
                for j, chip in enumerate(chips):
                    k = 2 * chip[0] + chip[1]
                    copy = pltpu.make_async_remote_copy(
                        src_ref=pair_refs[a].at[k], dst_ref=land_refs[a].at[k],
                        send_sem=send_sems.at[3 * a + j], recv_sem=recv_sems.at[3 * a + j],
                        device_id=(*chip, c), device_id_type=MESH)
                    copy.wait_send()
                    copy.wait_recv()

    bufs = [b for _, pairs, lands in groups for b in (*pairs, *lands)]
    sems = [s for group_sems, _, _ in groups for s in group_sems]
    res = pl.pallas_call(
        body,
        name=name,
        in_specs=[HBM_ONLY] * n_buf + [SEM_SPEC] * len(sems) + [ANY_SPEC] * len(after),
        out_specs=[HBM_ONLY] * n_buf,
        out_shape=[pltpu.HBM(b.shape, b.dtype) for b in bufs],
        input_output_aliases={a: a for a in range(n_buf)},
        compiler_params=pltpu.CompilerParams(has_side_effects=DATAFLOW),
    )(*bufs, *sems, *after)
    out, at = [], 0
    for n in sizes:
        out.append((list(res[at:at + n]), list(res[at + n:at + 2 * n])))
        at += 2 * n
    return out


def _swap_start(name, halves):
    n = len(halves)
    n_p = halves[0].shape[0]
    land_shapes = [(n_p, *h.shape[2:]) for h in halves]

    def body(*refs):
        half_refs, land_refs, send_sems, recv_sems = refs[:n], refs[n:2 * n], refs[2 * n], refs[2 * n + 1]
        x, y, c, _ = _place()
        for a in range(n):
            for p in range(n_p):
                pltpu.make_async_remote_copy(
                    src_ref=half_refs[a].at[p, 1 - c], dst_ref=land_refs[a].at[p],
                    send_sem=send_sems.at[n_p * a + p], recv_sem=recv_sems.at[n_p * a + p],
                    device_id=(x, y, 1 - c), device_id_type=MESH).start()
        refs[-1][...] = jnp.zeros(TOKEN.shape, TOKEN.dtype)

    res = pl.pallas_call(
        body,
        name=name,
        in_specs=[HBM_ONLY] * (2 * n),
        out_specs=[SEM_SPEC, SEM_SPEC] + [HBM_ONLY] * (2 * n) + [VMEM_SPEC],
        out_shape=[pltpu.SemaphoreType.DMA((n_p * n,)), pltpu.SemaphoreType.DMA((n_p * n,))]
        + [pltpu.HBM(h.shape, h.dtype) for h in halves]
        + [pltpu.HBM(s, h.dtype) for s, h in zip(land_shapes, halves)] + [TOKEN],
        input_output_aliases={a: 2 + a for a in range(2 * n)},
        compiler_params=pltpu.CompilerParams(has_side_effects=DATAFLOW),
    )(*[_in_hbm(h) for h in halves], *[_in_hbm(lax.empty(s, h.dtype)) for s, h in zip(land_shapes, halves)])
    return (res[0], res[1]), list(res[2:2 + n]), list(res[2 + n:2 + 2 * n]), res[-1]


def _swap_wait(name, halves, lands, sems, after):
    n = len(halves)
    n_p = halves[0].shape[0]

    def body(*refs):
        half_refs, land_refs, send_sems, recv_sems = refs[:n], refs[n:2 * n], refs[2 * n], refs[2 * n + 1]
        x, y, c, _ = _place()
        for a in range(n):
            for p in range(n_p):
                copy = pltpu.make_async_remote_copy(
                    src_ref=half_refs[a].at[p, 1 - c], dst_ref=land_refs[a].at[p],
                    send_sem=send_sems.at[n_p * a + p], recv_sem=recv_sems.at[n_p * a + p],
                    device_id=(x, y, 1 - c), device_id_type=MESH)
                copy.wait_send()
                copy.wait_recv()

    res = pl.pallas_call(
        body,
        name=name,
        in_specs=[HBM_ONLY] * (2 * n) + [SEM_SPEC, SEM_SPEC] + [ANY_SPEC] * len(after),
        out_specs=[HBM_ONLY] * (2 * n),
        out_shape=[pltpu.HBM(b.shape, b.dtype) for b in (*halves, *lands)],
        input_output_aliases={a: a for a in range(2 * n)},
        compiler_params=pltpu.CompilerParams(has_side_effects=DATAFLOW),
    )(*halves, *lands, *sems, *after)
    return list(res[:n]), list(res[n:])


def _join_start(name, bufs):
    n = len(bufs)

    def body(*refs):
        ins, send_sems, recv_sems = refs[:n], refs[n], refs[n + 1]
        x, y, c, _ = _place()
        for a in range(n):
            pltpu.make_async_remote_copy(
                src_ref=ins[a].at[c], dst_ref=ins[a].at[c], send_sem=send_sems.at[a], recv_sem=recv_sems.at[a],
                device_id=(x, y, 1 - c), device_id_type=MESH).start()
        refs[-1][...] = jnp.zeros(TOKEN.shape, TOKEN.dtype)

    res = pl.pallas_call(
        body,
        name=name,
        in_specs=[HBM_ONLY] * n,
        out_specs=[SEM_SPEC, SEM_SPEC] + [HBM_ONLY] * n + [VMEM_SPEC],
        out_shape=[pltpu.SemaphoreType.DMA((n,)), pltpu.SemaphoreType.DMA((n,))]
        + [pltpu.HBM(b.shape, b.dtype) for b in bufs] + [TOKEN],
        input_output_aliases={a: 2 + a for a in range(n)},
        compiler_params=pltpu.CompilerParams(has_side_effects=DATAFLOW),
    )(*[_in_hbm(b) for b in bufs])
    return (res[0], res[1]), list(res[2:2 + n]), res[-1]


def _join_wait(name, bufs, sems, after):
    n = len(bufs)

    def body(*refs):
        ins, send_sems, recv_sems = refs[:n], refs[n], refs[n + 1]
        x, y, c, _ = _place()
        for a in range(n):
            copy = pltpu.make_async_remote_copy(
                src_ref=ins[a].at[c], dst_ref=ins[a].at[1 - c], send_sem=send_sems.at[a], recv_sem=recv_sems.at[a],
                device_id=(x, y, 1 - c), device_id_type=MESH)
            copy.wait_send()
            copy.wait_recv()

    res = pl.pallas_call(
        body,
        name=name,
        in_specs=[HBM_ONLY] * n + [SEM_SPEC, SEM_SPEC] + [ANY_SPEC] * len(after),
        out_specs=[HBM_ONLY] * n,
        out_shape=[pltpu.HBM(b.shape, b.dtype) for b in bufs],
        input_output_aliases={a: a for a in range(n)},
        compiler_params=pltpu.CompilerParams(has_side_effects=DATAFLOW),
    )(*bufs, *sems, *after)
    return list(res)


def _allgather_small(name, block, deps=()):
    m_per, n = block.shape

    def body(x_ref, *rest):
        out_ref, send_sems, recv_sems, local_sem = rest[len(deps):]
        x, y, c, chips = _place()
        me, sibling = (x, y, c), (x, y, 1 - c)

        def rows(px, py, pc):
            return out_ref.at[pl.ds((4 * px + 2 * py + pc) * m_per, m_per), :]

        def copy(k, blk, to, src=None):
            return pltpu.make_async_remote_copy(
                src_ref=rows(*blk) if src is None else src, dst_ref=rows(*blk),
                send_sem=send_sems.at[k], recv_sem=recv_sems.at[k], device_id=to, device_id_type=MESH)

        mine = pltpu.make_async_copy(x_ref, rows(*me), local_sem)
        mine.start()
        first = [copy(0, me, sibling, src=x_ref)]
        first += [copy(1 + j, me, (*chip, c), src=x_ref) for j, chip in enumerate(chips)]
        for cp in first:
            cp.start()
        passed = [copy(4 + j, (*chip, c), sibling) for j, chip in enumerate(chips)]
        for j, chip in enumerate(chips):
            copy(1 + j, (*chip, c), me).wait_recv()
            passed[j].start()
        copy(0, sibling, me).wait_recv()
        for j, chip in enumerate(chips):
            copy(4 + j, (*chip, 1 - c), me).wait_recv()
        for cp in first + passed:
            cp.wait_send()
        mine.wait()

    return pl.pallas_call(
        body,
        name=name,
        out_shape=jax.ShapeDtypeStruct((8 * m_per, n), block.dtype),
        in_specs=[pl.BlockSpec(memory_space=pltpu.VMEM)] + [ANY_SPEC] * len(deps),
        out_specs=pl.BlockSpec(memory_space=pltpu.VMEM),
        scratch_shapes=[pltpu.SemaphoreType.DMA((7,)), pltpu.SemaphoreType.DMA((7,)), pltpu.SemaphoreType.DMA],
    )(block, *deps)


def _sum_blocks(name, gathered, n_blocks):
    r = gathered.shape[0] // n_blocks
    c = gathered.shape[1]

    def body(g_ref, o_ref):
        acc = g_ref[0:r, :]
        for b in range(1, n_blocks):
            acc = acc + g_ref[b * r:(b + 1) * r, :]
        o_ref[...] = acc

    return pl.pallas_call(body, name=name, out_shape=jax.ShapeDtypeStruct((r, c), F32))(gathered)


def _largest_tile(n, cap, mult):
    best = None
    for d in range(mult, min(n, cap) + 1, mult):
        if n % d == 0:
            best = d
    assert best is not None, (n, cap, mult)
    return best


def kernel(x, meta, g_pre_mix, w_in, b_gates, conf_dw_w, conf_dw_b, conf_ln_g, conf_ln_b, conf_w_pw, short_dw_w, short_w_out, w_o, g_post_mix, g_pre_mlp, w_up, w_down, g_post_mlp, loss_target, m_meta, m_g_pre_mix, m_w_in, m_b_gates, m_conf_dw_w, m_conf_dw_b, m_conf_ln_g, m_conf_ln_b, m_conf_w_pw, m_short_dw_w, m_short_w_out, m_w_o, m_g_post_mix, m_g_pre_mlp, m_w_up, m_w_down, m_g_post_mlp, v_meta, v_g_pre_mix, v_w_in, v_b_gates, v_conf_dw_w, v_conf_dw_b, v_conf_ln_g, v_conf_ln_b, v_conf_w_pw, v_short_dw_w, v_short_w_out, v_w_o, v_g_post_mix, v_g_pre_mlp, v_w_up, v_w_down, v_g_post_mlp):
    seq, d = x.shape[1], x.shape[2]
    t_real = seq + N_META
    t = -(-t_real // LANES) * LANES
    d_conf = conf_dw_b.shape[1]
    d_ff = w_up.shape[2] * N_CHIPS
    in_cols = w_in.shape[2] * N_CHIPS
    assert in_cols == 5 * d_conf + 2 * d and d == 2 * d_conf
    cw = d_conf
    core = lax.axis_index("c")
    chip = 2 * lax.axis_index("x") + lax.axis_index("y")

    tr = _largest_tile(t, 272, ROW_CHUNK)
    tm = t
    tn_in = _largest_tile(in_cols // N_CHIPS, 768, LANES)
    tn_d = _largest_tile(d, 1024, LANES)
    tn_h = _largest_tile(d, 512, LANES)
    tn_ff = _largest_tile(d_ff // N_CHIPS, 1024, LANES)
    tn_pw = _largest_tile(d // N_CHIPS, 512, LANES)

    big = [w_in[0], conf_w_pw[0], short_w_out[0], w_o[0], w_up[0], w_down[0]]
    chip_arr = chip.astype(jnp.int32).reshape(1)

    def cast(a, deps):
        return _into_slot(f"cast_w{a}", big[a], N_CHIPS, chip_arr, MXU_DTYPE, _largest_tile(big[a].shape[0], 256, 16), deps)

    small = [_into_slot(f"place_w{a}", w, N_CHIPS, chip_arr, F32, w.shape[0])
             for a, w in enumerate([meta, conf_dw_w[0], short_dw_w[0]])]
    sems_small, fly_small, tok_small = _gather_start("gather_start_small", small, [False] * 3, [[0, 1, 2]])
    sems_in, fly_in, tok_in = _gather_start("gather_start_in", [cast(0, [tok_small])], [True], [[0]])
    rest_groups = [[0, 1], [2], [3], [4]]
    sems_rest, fly_rest, tok_rest = _gather_start(
        "gather_start_rest", [cast(a, [tok_in]) for a in range(1, 6)], [True] * 5, rest_groups)

    def arrive(g, after):
        members = rest_groups[g]
        got = _gather_wait(f"gather_wait_rest{g}", [fly_rest[a] for a in members], [True] * len(members),
                           sems_rest[g], after)
        return _pass_halves(f"gather_pass_rest{g}", got)

    meta_g, wdw_g, w3_g = _gather_wait("gather_wait_small", fly_small, [False] * 3, sems_small[0], [tok_in])
    meta_full = jnp.transpose(meta_g, (1, 0, 2)).reshape(N_META, d)
    wdw = jnp.transpose(wdw_g, (1, 0, 2)).reshape(CONF_KERNEL, d_conf)
    w3 = jnp.transpose(w3_g, (1, 0, 2)).reshape(SHORT_KERNEL, d_conf)

    tail = jnp.zeros((t - t_real, d), F32)
    h0 = jnp.concatenate([meta_full, x[0], tail], axis=0)
    target = jnp.concatenate([jnp.zeros((N_META, d), F32), loss_target[0], tail], axis=0)

    def norm_in(i, rows, vecs):
        return [_rms_fwd(rows[0], vecs[0])], []

    (n_lp,) = _rowwise("norm_in", norm_in, [(h0, d, 0)], [g_pre_mix], [(d, MXU_DTYPE)], [], tr, deps=[tok_rest])
    (wg_in,) = _pass_halves("gather_pass_in", _gather_wait("gather_wait_in", fly_in, [True], sems_in[0], [n_lp]))
    proj =_matmul("proj", n_lp, wg_in, kind="nn", tm=tm, tn=tn_in, tk=d, out_dtypes=[F32])
    ac, c3 = _conv_fwd(proj, wdw, conf_dw_b, w3, d_conf)

    def ln_parts(ac_t, ln_g, ln_b):
        mu = jnp.mean(ac_t, axis=-1, keepdims=True)
        xc = ac_t - mu
        rstd = lax.rsqrt(jnp.mean(xc * xc, axis=-1, keepdims=True) + LN_EPS)
        xh = xc * rstd
        return xh, rstd, xh * ln_g + ln_b

    def branch_act(i, rows, vecs):
        ac_t, c3_t, bg_t = rows
        _, _, al = ln_parts(ac_t, vecs[0], vecs[1])
        return [al * _sigmoid(al), bg_t * c3_t], []

    a_act, s_lp = _rowwise("branch_act", branch_act, [(ac, cw, 0), (c3, cw, 0), (proj, cw, 2)],
                           [conf_ln_g, conf_ln_b], [(d_conf, MXU_DTYPE), (d_conf, MXU_DTYPE)], [], tr)
    wg_pw, wg_sout = arrive(0, [a_act])
    y_a = _matmul("y_a", a_act, wg_pw, kind="nn", tm=tm, tn=tn_pw, tk=d_conf, out_dtypes=[F32])
    y_b = _matmul("y_b", s_lp, wg_sout, kind="nn", tm=tm, tn=tn_pw, tk=d_conf, out_dtypes=[F32])

    gate_rows = [(proj, cw, 5), (proj, cw, 6), (proj, cw, 7), (proj, cw, 8)]

    def gates_of(rows, b):
        ga = _sigmoid(jnp.concatenate([rows[0], rows[1]], axis=1) + b[:, :d])
        gb = _sigmoid(jnp.concatenate([rows[2], rows[3]], axis=1) + b[:, d:])
        return ga, gb

    def gate(i, rows, vecs):
        ga, gb = gates_of(rows[2:], vecs[0])
        return [ga * rows[0] + gb * rows[1]], []

    (m_lp,) = _rowwise("gate", gate, [(y_a, d, 0), (y_b, d, 0)] + gate_rows, [b_gates], [(d, MXU_DTYPE)], [], tr)
    wg_o = arrive(1, [m_lp])[0].reshape(d, d)
    mix = _matmul("mix", m_lp, wg_o, kind="nn", tm=tm, tn=tn_d, tk=d, out_dtypes=[F32])

    def post_mix(i, rows, vecs):
        h1_t = rows[0] + _rms_fwd(rows[1], vecs[0])
        return [h1_t, _rms_fwd(h1_t, vecs[1])], []

    h1, n2_lp = _rowwise("post_mix", post_mix, [(h0, d, 0), (mix, d, 0)], [g_post_mix, g_pre_mlp],
                         [(d, F32), (d, MXU_DTYPE)], [], tr)
    (wg_up,) = arrive(2, [n2_lp])
    up, f_lp = _matmul("up", n2_lp, wg_up, kind="nn", tm=tm, tn=tn_h, tk=d, out_dtypes=[F32, MXU_DTYPE],
                       epilogue=lambda acc: (acc, jnp.square(jnp.maximum(acc, 0.0))))
    wg_down = arrive(3, [f_lp])[0].reshape(d_ff, d)
    dn = _matmul("down", f_lp, wg_down, kind="nn", tm=tm, tn=tn_h, tk=_largest_tile(d_ff, 2048, LANES),
                 out_dtypes=[F32])

    def head(row0, rows, vecs):
        h1_t, dn_t, tgt = rows
        y = h1_t + _rms_fwd(dn_t, vecs[0])
        row = row0 + lax.broadcasted_iota(jnp.int32, (ROW_CHUNK, 1), 0)
        err = jnp.where(jnp.logical_and(row >= N_META, row < t_real), y - tgt, 0.0)
        dy = err / d
        d_dn, dg = _rms_bwd(dn_t, vecs[0], dy)
        loss_rows = 0.5 * jnp.mean(err * err, axis=-1, keepdims=True)
        return [dy, d_dn], [dg, jnp.broadcast_to(loss_rows, (ROW_CHUNK, LANES))]

    dy, d_dn, dg_post_mlp, loss_vec = _rowwise(
        "head", head, [(h1, d, 0), (dn, d, 0), (target, d, 0)], [g_post_mlp], [(d, F32), (d, MXU_DTYPE)], [d, LANES], tr)

    core_arr = core.astype(jnp.int32).reshape(1)
    place = jnp.stack([chip, core]).astype(jnp.int32)
    in_flight = {}


    def tag(members):
        return "".join(str(a) for a in members)

    own_halves = {}
    two_types = dict(out_dtypes=[F32, WIRE_DTYPE], epilogue=lambda acc: (acc, acc))

    def swap_start(members, gws):
        def in_halves(gw):
            return gw.reshape(N_CHIPS, 2, gw.shape[-2] // 2, gw.shape[-1])

        own_halves[members] = [in_halves(gw.reshape(N_CHIPS, -1, gw.shape[-1])) for gw, _ in gws]
        for_sibling = [in_halves(lp.reshape(N_CHIPS, -1, lp.shape[-1])) for _, lp in gws]
        *in_flight[members], token = _swap_start(f"swap_start{tag(members)}", for_sibling)
        return token

    def exchange_start(members, after):
        sems, halves, lands = in_flight[members]
        _, lands = _swap_wait(f"swap_wait{tag(members)}", halves, lands, sems, after)
        if len(members) > 1:
            pairs = _pair_sum_group(f"pair_sum{tag(members)}", own_halves[members], lands, core_arr)
        else:
            pairs = [_pair_sum(f"pair_sum{a}", h, q, core_arr, _largest_tile(q.shape[1], 256, 16))
                     for a, h, q in zip(members, own_halves[members], lands)]
        *in_flight[members], token = _exchange_start(f"exchange_start{tag(members)}", pairs)
        return token

    def reduce_finish(groups, after):
        waited = _exchange_wait("exchange_wait" + "_".join(tag(g) for g in groups), [in_flight[g] for g in groups], after)
        out = {}
        for g, (pairs, lands) in zip(groups, waited):
            if len(g) > 1:
                out.update(zip(g, _sum_pieces_group(f"sum_pieces{tag(g)}", pairs, lands, place)))
            else:
                out[g[0]] = _sum_pieces(f"sum_pieces{g[0]}", pairs[0], lands[0], place,
                                        _largest_tile(lands[0].shape[1], 256, 16))
        return out

    d_up = _matmul("d_up", d_dn, wg_down, kind="nt", tm=tm, tn=tn_h, tk=d, out_dtypes=[MXU_DTYPE], extras=[up],
                   epilogue=lambda acc, up_t: (acc * (2.0 * jnp.maximum(up_t, 0.0)),))
    tk_t = t
    gw_down = _matmul("gw_down", f_lp, d_dn, kind="tn", tm=_largest_tile(d_ff, 2048, LANES), tn=tn_h, tk=tk_t,
                      **two_types)
    tok = swap_start((5,), [gw_down])
    d_n2 = _matmul("d_n2", d_up, wg_up, kind="nt", tm=tm, tn=tn_h, tk=d_ff // N_CHIPS, out_dtypes=[F32], deps=[tok])
    tok = exchange_start((5,), [d_n2])
    gw_up = _matmul("gw_up", n2_lp, d_up, kind="tn", tm=_largest_tile(d, 2048, LANES), tn=tn_h, tk=tk_t,
                    out_pieces=N_CHIPS, deps=[tok], **two_types)
    tok = swap_start((4,), [gw_up])

    def bwd_mid(i, rows, vecs):
        dy_t, dn2_t, h1_t, mix_t = rows
        d_h1a, dg_pre_mlp = _rms_bwd(h1_t, vecs[1], dn2_t)
        d_h1 = dy_t + d_h1a
        d_mix, dg_post_mix = _rms_bwd(mix_t, vecs[0], d_h1)
        return [d_h1, d_mix], [dg_pre_mlp, dg_post_mix]

    d_h1, d_mix, dg_pre_mlp, dg_post_mix = _rowwise(
        "bwd_mid", bwd_mid, [(dy, d, 0), (d_n2, d, 0), (h1, d, 0), (mix, d, 0)], [g_post_mix, g_pre_mlp],
        [(d, F32), (d, MXU_DTYPE)], [d, d], tr, deps=[tok])
    d_m = _matmul("d_m", d_mix, wg_o, kind="nt", tm=tm, tn=tn_h, tk=d, out_dtypes=[F32])
    tok = exchange_start((4,), [d_m])
    gw_o = _matmul("gw_o", m_lp, d_mix, kind="tn", tm=_largest_tile(d, 2048, LANES), tn=tn_h, tk=tk_t, deps=[tok],
                   **two_types)

    def gate_bwd(i, rows, vecs):
        dm_t, ya_t, yb_t = rows[:3]
        ga, gb = gates_of(rows[3:], vecs[0])
        d_gpre = jnp.concatenate([dm_t * ya_t * ga * (1.0 - ga), dm_t * yb_t * gb * (1.0 - gb)], axis=1)
        return [dm_t * ga, dm_t * gb, d_gpre], [d_gpre]

    d_ya, d_yb, d_gpre, dg_b_gates = _rowwise(
        "gate_bwd", gate_bwd, [(d_m, d, 0), (y_a, d, 0), (y_b, d, 0)] + gate_rows, [b_gates],
        [(d, MXU_DTYPE), (d, MXU_DTYPE), (2 * d, MXU_DTYPE)], [2 * d], tr, deps=[tok])
    d_aact = _matmul("d_aact", d_ya, wg_pw, kind="nt", tm=tm, tn=d_conf // 2, tk=tn_pw, out_dtypes=[F32], deps=[tok])
    gw_pw = _matmul("gw_pw", a_act, d_ya, kind="tn", tm=_largest_tile(d_conf, 2048, LANES), tn=tn_pw, tk=tk_t,
                    out_pieces=N_CHIPS, **two_types)
    d_s = _matmul("d_s", d_yb, wg_sout, kind="nt", tm=tm, tn=d_conf // 2, tk=tn_pw, out_dtypes=[F32], deps=[tok])
    gw_sout = _matmul("gw_sout", s_lp, d_yb, kind="tn", tm=_largest_tile(d_conf, 2048, LANES), tn=tn_pw, tk=tk_t,
                      out_pieces=N_CHIPS, deps=[tok], **two_types)
    small_big = (3, 1, 2)
    tok = swap_start(small_big, [gw_o, gw_pw, gw_sout])

    def branch_bwd(i, rows, vecs):
        daact_t, ds_t, ac_t, c3_t, bg_t = rows
        xh, rstd, al = ln_parts(ac_t, vecs[0], vecs[1])
        sg = _sigmoid(al)
        d_al = daact_t * (sg * (1.0 + al * (1.0 - sg)))
        dxh = d_al * vecs[0]
        d_ac = rstd * (dxh - jnp.mean(dxh, axis=-1, keepdims=True) - xh * jnp.mean(dxh * xh, axis=-1, keepdims=True))
        return [d_ac, ds_t * bg_t, ds_t * c3_t], [d_al * xh, d_al, d_ac]

    d_ac, d_c3, d_bg, dg_ln_g, dg_ln_b, dg_dw_b = _rowwise(
        "branch_bwd", branch_bwd, [(d_aact, cw, 0), (d_s, cw, 0), (ac, cw, 0), (c3, cw, 0), (proj, cw, 2)],
        [conf_ln_g, conf_ln_b], [(d_conf, F32), (d_conf, F32), (d_conf, MXU_DTYPE)], [d_conf] * 3, tr, deps=[tok])
    d_av, d_ag, d_cg, d_v, dg_wdw, dg_w3 = _conv_bwd(proj, d_ac, d_c3, wdw, w3, d_conf)
    d_proj = jnp.concatenate([d_av, d_ag, d_bg, d_cg, d_v, d_gpre], axis=1)
    small_w = d_conf

    def pack(arrs):
        flat = jnp.concatenate([a.reshape(-1, small_w) for a in arrs], axis=0)
        return jnp.pad(flat, ((0, -flat.shape[0] % 8), (0, 0)))

    def reduce_small(tag, arrs, deps):
        assert all(a.size % small_w == 0 for a in arrs)
        gathered = _allgather_small(f"allgather_small_{tag}", pack(arrs), deps)
        first, r0 = [], 0
        for a in arrs:
            first.append(r0)
            r0 += a.size // small_w
        return _sum_blocks(f"sum_small_{tag}", gathered, 8), first

    def rows_of(total, r0, like):
        return total[r0:r0 + like.size // small_w].reshape(like.shape)

    tok = exchange_start(small_big, [d_proj])

    gw_in = _matmul("gw_in", n_lp, d_proj, kind="tn", tm=_largest_tile(d, 2048, LANES), tn=tn_in, tk=tk_t,
                    out_pieces=N_CHIPS, deps=[tok], **two_types)
    tok = swap_start((0,), [gw_in])
    d_n = _matmul("d_n", d_proj, wg_in, kind="nt", tm=tm, tn=tn_h, tk=in_cols // N_CHIPS, out_dtypes=[F32], deps=[tok])
    tok = exchange_start((0,), [d_n])

    early = [5, 4, 3, 1, 2]
    reduced = reduce_finish([(5,), (4,), small_big], [tok])
    join_sems, joining, tok = _join_start("join_start_early", [reduced[a] for a in early])

    def bwd_in(i, rows, vecs):
        d_h0a, dg = _rms_bwd(rows[2], vecs[0], rows[1])
        return [rows[0] + d_h0a], [dg]

    d_h0, dg_pre_mix = _rowwise("bwd_in", bwd_in, [(d_h1, d, 0), (d_n, d, 0), (h0, d, 0)], [g_pre_mix],
                                [(d, F32)], [d], tr, deps=[tok])
    grad_x = d_h0[N_META:t_real][None]

    big_m = [m_w_in, m_conf_w_pw, m_short_w_out, m_w_o, m_w_up, m_w_down]
    big_v = [v_w_in, v_conf_w_pw, v_short_w_out, v_w_o, v_w_up, v_w_down]
    big_res = {}

    def adam_group(members, joined):
        for a, j in zip(members, joined):
            big_res[a] = _adamw(f"adamw_big{a}", big[a], j.reshape(big[a].shape), big_m[a][0], big_v[a][0],
                                _largest_tile(big[a].shape[0], 256, 8))

    adam_group(early, _join_wait("join_wait_early", joining, join_sems, [d_h0]))
    reduced = reduce_finish([(0,)], [big_res[a][1] for a in early])
    join_sems, joining, tok = _join_start("join_start_late", [reduced[0]])
    rep_g = [dg_pre_mix, dg_b_gates, dg_dw_b, dg_ln_g, dg_ln_b, dg_post_mix, dg_pre_mlp, dg_post_mlp]
    rep_w = [g_pre_mix, b_gates, conf_dw_b, conf_ln_g, conf_ln_b, g_post_mix, g_pre_mlp, g_post_mlp]
    rep_m = [m_g_pre_mix, m_b_gates, m_conf_dw_b, m_conf_ln_g, m_conf_ln_b, m_g_post_mix, m_g_pre_mlp, m_g_post_mlp]
    rep_v = [v_g_pre_mix, v_b_gates, v_conf_dw_b, v_conf_ln_g, v_conf_ln_b, v_g_post_mix, v_g_pre_mlp, v_g_post_mlp]
    col_g = [d_h0[:N_META], dg_wdw, dg_w3]
    loss_row = jnp.tile(loss_vec, (1, small_w // LANES))
    total, first = reduce_small("all", rep_g + col_g + [loss_row], [tok])
    loss = total[first[-1], 0]
    rep_res = _small_adamw("adamw_rep", total, [(first[e], w, m, v) for e, (w, m, v) in enumerate(zip(rep_w, rep_m, rep_v))])
    col_res = {}
    for a, (w, g_part, m, v) in enumerate([(meta, col_g[0], m_meta, v_meta), (conf_dw_w[0], col_g[1], m_conf_dw_w[0], v_conf_dw_w[0]),
                                           (short_dw_w[0], col_g[2], m_short_dw_w[0], v_short_dw_w[0])]):
        g_full = rows_of(total, first[len(rep_g) + a], g_part)
        g_own = lax.dynamic_slice_in_dim(g_full, chip * w.shape[1], w.shape[1], axis=1)
        col_res[a] = _adamw(f"adamw_col{a}", w, g_own, m, v, w.shape[0])
    adam_group([0], _join_wait("join_wait_late", joining, join_sems, [rep_res[0][1]] + [col_res[a][1] for a in range(3)]))

    def leaf(q):
        r = lambda a: rep_res[a][q]
        b = lambda a: big_res[a][q][None]
        return [col_res[0][q], r(0), b(0), r(1), col_res[1][q][None], r(2), r(3), r(4), b(1), col_res[2][q][None], b(2),
                b(3), r(5), r(6), b(4), b(5), r(7)]

    return (loss, grad_x, *leaf(0), *leaf(1), *leaf(2), *leaf(3))
```

```python
import jax
import jax.numpy as jnp
from jax import lax
from jax.experimental import pallas as pl
from jax.experimental.pallas import tpu as pltpu

F32 = jnp.float32
BF16 = jnp.bfloat16
MXU_DTYPE = BF16
WIRE_DTYPE = BF16

N_META = 16
CONF_KERNEL = 31
SHORT_KERNEL = 3
CONV_PAD = 32
RMS_EPS = 1e-6
LN_EPS = 1e-5
ADAM_LR = 0.001
ADAM_B1 = 0.9
ADAM_B2 = 0.999
ADAM_EPS = 1e-08
ADAM_WD = 0.01
ADAM_STEP = 10

N_CHIPS = 4
MESH = pl.DeviceIdType.MESH
LANES = 128


def _sigmoid(z):
    return 1.0 / (1.0 + jnp.exp(-z))


ANY_SPEC = pl.BlockSpec(memory_space=pl.ANY)


def _matmul(name, a, b, *, kind, tm, tn, tk, out_dtypes, out_pieces=1, epilogue=None, extras=(), deps=()):
    pieces = b.shape[0] if b.ndim == 3 else 1
    if kind == "nn":
        m, kdim = a.shape
        n = b.shape[-1] * pieces
        dims = (((1,), (0,)), ((), ()))
        a_spec = pl.BlockSpec((tm, tk), lambda i, j, k: (i, k))
        if b.ndim == 2:
            b_spec = pl.BlockSpec((tk, tn), lambda i, j, k: (k, j))
        else:
            npp = b.shape[-1] // tn
            b_spec = pl.BlockSpec((None, tk, tn), lambda i, j, k: (j // npp, k, j % npp))
    elif kind == "nt":
        m, kdim = a.shape
        n = b.shape[-2]
        dims = (((1,), (1,)), ((), ()))
        a_spec = pl.BlockSpec((tm, tk), lambda i, j, k: (i, k))
        if b.ndim == 2:
            b_spec = pl.BlockSpec((tn, tk), lambda i, j, k: (j, k))
        else:
            kpp = b.shape[-1] // tk
            b_spec = pl.BlockSpec((None, tn, tk), lambda i, j, k: (k // kpp, j, k % kpp))
    else:
        kdim, m = a.shape
        n = b.shape[-1]
        dims = (((0,), (0,)), ((), ()))
        a_spec = pl.BlockSpec((tk, tm), lambda i, j, k: (k, i))
        b_spec = pl.BlockSpec((tk, tn), lambda i, j, k: (k, j))
    assert m % tm == 0 and n % tn == 0 and kdim % tk == 0, (name, m, n, kdim, tm, tn, tk)
    nk = kdim // tk
    if out_pieces == 1:
        out_shape = (m, n)
        out_spec = pl.BlockSpec((tm, tn), lambda i, j, k: (i, j))
    else:
        onpp = n // out_pieces // tn
        out_shape = (out_pieces, m, n // out_pieces)
        out_spec = pl.BlockSpec((None, tm, tn), lambda i, j, k: (j // onpp, i, j % onpp))
    n_ex, n_out, n_in = len(extras), len(out_dtypes), len(extras) + len(deps)
    if epilogue is None:
        epilogue = lambda acc: (acc,)

    def body(a_ref, b_ref, *rest):
        ex_refs, o_refs = rest[:n_ex], rest[n_in:n_in + n_out]
        prod = lax.dot_general(a_ref[...], b_ref[...], dims, preferred_element_type=F32)

        def finish(acc):
            tiles = epilogue(acc, *[r[...] for r in ex_refs])
            for o_ref, t in zip(o_refs, tiles):
                o_ref[...] = t.astype(o_ref.dtype)

        if nk == 1:
            finish(prod)
        else:
            acc_ref = rest[n_in + n_out]
            k = pl.program_id(2)

            @pl.when(k == 0)
            def _():
                acc_ref[...] = prod

            @pl.when(jnp.logical_and(k > 0, k < nk - 1))
            def _():
                acc_ref[...] += prod

            @pl.when(k == nk - 1)
            def _():
                finish(acc_ref[...] + prod)

    ex_specs = [pl.BlockSpec((tm, tn), lambda i, j, k: (i, j)) for _ in extras]
    res = pl.pallas_call(
        body,
        name=name,
        grid=(m // tm, n // tn, nk),
        in_specs=[a_spec, b_spec, *ex_specs] + [ANY_SPEC] * len(deps),
        out_specs=[out_spec] * n_out,
        out_shape=[jax.ShapeDtypeStruct(out_shape, d) for d in out_dtypes],
        scratch_shapes=[pltpu.VMEM((tm, tn), F32)] if nk > 1 else [],
        compiler_params=pltpu.CompilerParams(dimension_semantics=("parallel", "parallel", "arbitrary")),
    )(a, b, *extras, *deps)
    return res[0] if n_out == 1 else res


STREAM_BUFFERS = 3
ROW_CHUNK = 16
SUBLANES = 8


def _rowwise(name, fn, rows, vecs, outs, sums, tr, deps=()):
    t = rows[0][0].shape[0]
    assert t % tr == 0 and tr % ROW_CHUNK == 0
    n_r, n_v, n_o, n_s = len(rows), len(vecs), len(outs), len(sums)
    n_in = n_r + n_v + len(deps)
    n_steps = t // tr

    def body(*refs):
        r_in, v_in = refs[:n_r], refs[n_r:n_r + n_v]
        o_refs = refs[n_in:n_in + n_o]
        s_refs = refs[n_in + n_o:n_in + n_o + n_s]
        acc_refs = refs[n_in + n_o + n_s:]
        i = pl.program_id(0)

        @pl.when(i == 0)
        def _():
            for acc_ref in acc_refs:
                acc_ref[...] = jnp.zeros(acc_ref.shape, F32)

        def chunk(ci):
            r0 = ci * ROW_CHUNK
            sl = pl.ds(r0, ROW_CHUNK)
            o_tiles, s_tiles = fn(i * tr + r0, [r[sl, :] for r in r_in], [v[...] for v in v_in])
            for o_ref, tile in zip(o_refs, o_tiles):
                o_ref[sl, :] = tile.astype(o_ref.dtype)
            for acc_ref, tile in zip(acc_refs, s_tiles):
                part = tile[0:SUBLANES]
                for s in range(1, ROW_CHUNK // SUBLANES):
                    part = part + tile[s * SUBLANES:(s + 1) * SUBLANES]
                acc_ref[...] += part

        for ci in range(tr // ROW_CHUNK):
            chunk(ci)

        @pl.when(i == n_steps - 1)
        def _():
            for s_ref, acc_ref in zip(s_refs, acc_refs):
                s_ref[...] = jnp.sum(acc_ref[...], axis=0, keepdims=True)

    def row_spec(width, blk):
        return pl.BlockSpec((tr, width), lambda i: (i, blk))

    res = pl.pallas_call(
        body,
        name=name,
        grid=(t // tr,),
        in_specs=[row_spec(w, blk) for _, w, blk in rows]
        + [pl.BlockSpec(v.shape, lambda i: (0, 0)) for v in vecs] + [ANY_SPEC] * len(deps),
        out_specs=[pl.BlockSpec((tr, c), lambda i: (i, 0)) for c, _ in outs]
        + [pl.BlockSpec((1, c), lambda i: (0, 0)) for c in sums],
        out_shape=[jax.ShapeDtypeStruct((t, c), d) for c, d in outs]
        + [jax.ShapeDtypeStruct((1, c), F32) for c in sums],
        scratch_shapes=[pltpu.VMEM((SUBLANES, c), F32) for c in sums],
        compiler_params=pltpu.CompilerParams(dimension_semantics=("arbitrary",)),
    )(*[r[0] for r in rows], *vecs, *deps)
    return res


def _rms_fwd(x, g):
    r = lax.rsqrt(jnp.mean(x * x, axis=-1, keepdims=True) + RMS_EPS)
    return x * r * g


def _rms_bwd(x, g, dy):
    r = lax.rsqrt(jnp.mean(x * x, axis=-1, keepdims=True) + RMS_EPS)
    xn = x * r
    dxn = dy * g
    dx = r * (dxn - xn * jnp.mean(dxn * xn, axis=-1, keepdims=True))
    return dx, dy * xn


CONV_ROWS = 64
CONV_LANES = 128


def _conv_fwd(proj, wdw, bdw, w3, d_conf):
    t = proj.shape[0]
    cl = CONV_LANES
    nb = d_conf // cl
    nchunk = t // CONV_ROWS
    assert t % CONV_ROWS == 0

    def body(av_ref, ag_ref, cg_ref, v_ref, wdw_ref, bdw_ref, w3_ref, ac_ref, c3_ref, apad, cpad):
        zeros = jnp.zeros((CONV_PAD, cl), F32)
        apad[0:CONV_PAD, :] = zeros
        cpad[0:CONV_PAD, :] = zeros
        apad[CONV_PAD:, :] = av_ref[...] * _sigmoid(ag_ref[...])
        cpad[CONV_PAD:, :] = cg_ref[...] * v_ref[...]

        def chunk(ci, carry):
            base = pl.multiple_of(ci * CONV_ROWS, 8)
            acc = jnp.zeros((CONV_ROWS, cl), F32) + bdw_ref[...]
            for k in range(CONF_KERNEL):
                off = CONV_PAD - (CONF_KERNEL - 1) + k
                acc = acc + apad[pl.ds(base + off, CONV_ROWS), :] * wdw_ref[k:k + 1, :]
            ac_ref[pl.ds(base, CONV_ROWS), :] = acc
            acc3 = jnp.zeros((CONV_ROWS, cl), F32)
            for k in range(SHORT_KERNEL):
                off = CONV_PAD - (SHORT_KERNEL - 1) + k
                acc3 = acc3 + cpad[pl.ds(base + off, CONV_ROWS), :] * w3_ref[k:k + 1, :]
            c3_ref[pl.ds(base, CONV_ROWS), :] = acc3
            return carry

        lax.fori_loop(0, nchunk, chunk, 0)

    def col(blk0):
        return pl.BlockSpec((t, cl), lambda j: (0, blk0 + j))

    return pl.pallas_call(
        body,
        name="conv_fwd",
        grid=(nb,),
        in_specs=[col(0), col(nb), col(3 * nb), col(4 * nb),
                  pl.BlockSpec((CONF_KERNEL, cl), lambda j: (0, j)),
                  pl.BlockSpec((1, cl), lambda j: (0, j)),
                  pl.BlockSpec((SHORT_KERNEL, cl), lambda j: (0, j))],
        out_specs=[pl.BlockSpec((t, cl), lambda j: (0, j))] * 2,
        out_shape=[jax.ShapeDtypeStruct((t, d_conf), F32)] * 2,
        scratch_shapes=[pltpu.VMEM((t + CONV_PAD, cl), F32)] * 2,
        compiler_params=pltpu.CompilerParams(dimension_semantics=("parallel",)),
    )(proj, proj, proj, proj, wdw, bdw, w3)


def _conv_bwd(proj, d_ac, d_c3, wdw, w3, d_conf):
    t = proj.shape[0]
    cl = CONV_LANES
    nb = d_conf // cl
    nchunk = t // CONV_ROWS
    nsub = CONV_ROWS // 8

    def fold(p):
        r = p[0:8]
        for s in range(1, nsub):
            r = r + p[8 * s:8 * s + 8]
        return r

    def body(av_ref, ag_ref, cg_ref, v_ref, dac_ref, dc3_ref, wdw_ref, w3_ref,
             dav_ref, dag_ref, dcg_ref, dv_ref, dwdw_ref, dw3_ref, apad, cpad, dapad, dcpad):
        zeros = jnp.zeros((CONV_PAD, cl), F32)
        apad[0:CONV_PAD, :] = zeros
        cpad[0:CONV_PAD, :] = zeros
        apad[CONV_PAD:, :] = av_ref[...] * _sigmoid(ag_ref[...])
        cpad[CONV_PAD:, :] = cg_ref[...] * v_ref[...]
        dapad[0:t, :] = dac_ref[...]
        dcpad[0:t, :] = dc3_ref[...]
        dapad[t:, :] = zeros
        dcpad[t:, :] = zeros

        def chunk(ci, accs):
            base = pl.multiple_of(ci * CONV_ROWS, 8)
            rows = pl.ds(base, CONV_ROWS)
            da = jnp.zeros((CONV_ROWS, cl), F32)
            for k in range(CONF_KERNEL):
                da = da + dapad[pl.ds(base + (CONF_KERNEL - 1 - k), CONV_ROWS), :] * wdw_ref[k:k + 1, :]
            dcv = jnp.zeros((CONV_ROWS, cl), F32)
            for k in range(SHORT_KERNEL):
                dcv = dcv + dcpad[pl.ds(base + (SHORT_KERNEL - 1 - k), CONV_ROWS), :] * w3_ref[k:k + 1, :]
            av, sg = av_ref[rows, :], _sigmoid(ag_ref[rows, :])
            dav_ref[rows, :] = (da * sg).astype(dav_ref.dtype)
            dag_ref[rows, :] = (da * av * sg * (1.0 - sg)).astype(dag_ref.dtype)
            dcg_ref[rows, :] = (dcv * v_ref[rows, :]).astype(dcg_ref.dtype)
            dv_ref[rows, :] = (dcv * cg_ref[rows, :]).astype(dv_ref.dtype)
            d_out, d_out3 = dac_ref[rows, :], dc3_ref[rows, :]
            new = []
            for k in range(CONF_KERNEL):
                off = CONV_PAD - (CONF_KERNEL - 1) + k
                new.append(accs[k] + fold(d_out * apad[pl.ds(base + off, CONV_ROWS), :]))
            for k in range(SHORT_KERNEL):
                off = CONV_PAD - (SHORT_KERNEL - 1) + k
                new.append(accs[CONF_KERNEL + k] + fold(d_out3 * cpad[pl.ds(base + off, CONV_ROWS), :]))
            return tuple(new)

        init = tuple(jnp.zeros((8, cl), F32) for _ in range(CONF_KERNEL + SHORT_KERNEL))
        accs = lax.fori_loop(0, nchunk, chunk, init)
        for k in range(CONF_KERNEL):
            dwdw_ref[k:k + 1, :] = jnp.sum(accs[k], axis=0, keepdims=True)
        for k in range(SHORT_KERNEL):
            dw3_ref[k:k + 1, :] = jnp.sum(accs[CONF_KERNEL + k], axis=0, keepdims=True)

    def col(blk0):
        return pl.BlockSpec((t, cl), lambda j: (0, blk0 + j))

    own = pl.BlockSpec((t, cl), lambda j: (0, j))
    return pl.pallas_call(
        body,
        name="conv_bwd",
        grid=(nb,),
        in_specs=[col(0), col(nb), col(3 * nb), col(4 * nb), own, own,
                  pl.BlockSpec((CONF_KERNEL, cl), lambda j: (0, j)),
                  pl.BlockSpec((SHORT_KERNEL, cl), lambda j: (0, j))],
        out_specs=[own] * 4 + [pl.BlockSpec((CONF_KERNEL, cl), lambda j: (0, j)),
                               pl.BlockSpec((SHORT_KERNEL, cl), lambda j: (0, j))],
        out_shape=[jax.ShapeDtypeStruct((t, d_conf), MXU_DTYPE)] * 4
        + [jax.ShapeDtypeStruct((CONF_KERNEL, d_conf), F32), jax.ShapeDtypeStruct((SHORT_KERNEL, d_conf), F32)],
        scratch_shapes=[pltpu.VMEM((t + CONV_PAD, cl), F32)] * 4,
        compiler_params=pltpu.CompilerParams(dimension_semantics=("parallel",)),
    )(proj, proj, proj, proj, d_ac, d_c3, wdw, w3)


def _elementwise(name, fn, ins, out_dtypes, tr, deps=()):
    ins = [(a, ()) if not isinstance(a, tuple) else a for a in ins]
    r, c = ins[0][0].shape[-2:]
    assert r % tr == 0, (name, r, tr)
    n_in = len(ins)

    n_out = len(out_dtypes)

    def step(*refs):
        tiles = fn(*[x[...] for x in refs[:n_in]])
        for o_ref, tile in zip(refs[n_in:], tiles):
            o_ref[...] = tile.astype(o_ref.dtype)

    if r == tr:
        return pl.pallas_call(
            lambda *refs: step(*refs[:n_in], *refs[n_in + len(deps):]),
            name=name,
            in_specs=[pl.BlockSpec((None,) * len(lead) + (tr, c), lambda: (*lead, 0, 0)) for _, lead in ins]
            + [ANY_SPEC] * len(deps),
            out_specs=[pl.BlockSpec((tr, c), lambda: (0, 0))] * n_out,
            out_shape=[jax.ShapeDtypeStruct((r, c), d) for d in out_dtypes],
        )(*[a for a, _ in ins], *deps)

    def spec(lead):
        return pl.BlockSpec((None,) * len(lead) + (tr, c), lambda i: (*lead, i, 0),
                            pipeline_mode=pl.Buffered(STREAM_BUFFERS))

    def body(*refs):
        pltpu.emit_pipeline(
            step, grid=(r // tr,), in_specs=[spec(lead) for _, lead in ins],
            out_specs=[pl.BlockSpec((tr, c), lambda i: (i, 0))] * n_out,
        )(*refs[:n_in], *refs[n_in + len(deps):])

    res = pl.pallas_call(
        body,
        name=name,
        in_specs=[ANY_SPEC] * (n_in + len(deps)),
        out_specs=[ANY_SPEC] * n_out,
        out_shape=[jax.ShapeDtypeStruct((r, c), d) for d in out_dtypes],
    )(*[a for a, _ in ins], *deps)
    return res


def _adamw_tiles(w, g, m, v):
    m = ADAM_B1 * m + (1.0 - ADAM_B1) * g
    v = ADAM_B2 * v + (1.0 - ADAM_B2) * jnp.square(g)
    m_hat = m / (1.0 - ADAM_B1 ** ADAM_STEP)
    v_hat = v / (1.0 - ADAM_B2 ** ADAM_STEP)
    delta = -ADAM_LR * (m_hat / (jnp.sqrt(v_hat) + ADAM_EPS) + ADAM_WD * w)
    return g, delta, m, v


def _adamw(name, w, g, m, v, tr, deps=()):
    shape = w.shape
    flat = [a.reshape(shape[-2:]) if a.ndim > 2 else a for a in (w, g, m, v)]
    res = _elementwise(name, _adamw_tiles, flat, [F32] * 4, tr, deps)
    return [a.reshape(shape) for a in res]


def _small_adamw(name, total, entries):
    sw = total.shape[1]
    n = len(entries)

    def body(total_ref, *refs):
        ins, outs = refs[:3 * n], refs[3 * n:]
        for e, (row0, w, _, _) in enumerate(entries):
            for q in range(w.shape[1] // sw):
                cols = slice(q * sw, (q + 1) * sw)
                tiles = _adamw_tiles(ins[3 * e][:, cols], total_ref[row0 + q:row0 + q + 1, :],
                                     ins[3 * e + 1][:, cols], ins[3 * e + 2][:, cols])
                for o_ref, tile in zip(outs[4 * e:4 * e + 4], tiles):
                    o_ref[:, cols] = tile

    res = pl.pallas_call(
        body,
        name=name,
        out_shape=[jax.ShapeDtypeStruct(w.shape, F32) for _, w, _, _ in entries for _ in range(4)],
    )(total, *[a for _, w, m, v in entries for a in (w, m, v)])
    return [list(res[4 * e:4 * e + 4]) for e in range(n)]


def _pair_sum(name, p, q, core, tr):
    n_p, _, hr, c = p.shape
    assert hr % tr == 0

    def body(core_ref, p_ref, q_ref, o_ref):
        o_ref[...] = (p_ref[...] + q_ref[...].astype(F32)).astype(o_ref.dtype)

    return pl.pallas_call(
        body,
        name=name,
        grid_spec=pltpu.PrefetchScalarGridSpec(
            num_scalar_prefetch=1,
            grid=(n_p, hr // tr),
            in_specs=[pl.BlockSpec((None, None, tr, c), lambda a, i, core_ref: (a, core_ref[0], i, 0)),
                      pl.BlockSpec((None, tr, c), lambda a, i, core_ref: (a, i, 0))],
            out_specs=pl.BlockSpec((None, tr, c), lambda a, i, core_ref: (a, i, 0)),
        ),
        out_shape=jax.ShapeDtypeStruct((n_p, hr, c), WIRE_DTYPE),
        compiler_params=pltpu.CompilerParams(dimension_semantics=("parallel", "parallel")),
    )(core, p, q)


def _into_slot(name, w, slots, slot, dtype, tr, deps=()):
    r, c = w.shape
    assert r % tr == 0

    def body(slot_ref, w_ref, *rest):
        o_ref = rest[len(deps)]
        o_ref[...] = w_ref[...].astype(o_ref.dtype)

    return pl.pallas_call(
        body,
        name=name,
        grid_spec=pltpu.PrefetchScalarGridSpec(
            num_scalar_prefetch=1,
            grid=(r // tr,),
            in_specs=[pl.BlockSpec((tr, c), lambda i, slot_ref: (i, 0))] + [ANY_SPEC] * len(deps),
            out_specs=pl.BlockSpec((None, tr, c), lambda i, slot_ref: (slot_ref[0], i, 0)),
        ),
        out_shape=jax.ShapeDtypeStruct((slots, r, c), dtype),
        compiler_params=pltpu.CompilerParams(dimension_semantics=("parallel",)),
    )(slot, w, *deps)


def _sum_pieces(name, own, rb, place, tr):
    n_p, hr, c = rb.shape
    assert hr % tr == 0

    def body(place_ref, own_ref, *refs):
        chip = place_ref[0]
        acc = None
        for k in range(n_p):
            tile = jnp.where(chip == k, own_ref[...], refs[k][...]).astype(F32)
            acc = tile if acc is None else acc + tile
        refs[n_p][...] = acc

    def landed(k):
        return pl.BlockSpec((None, tr, c), lambda i, place_ref: (jnp.where(place_ref[0] == k, (k + 1) % n_p, k), i, 0))

    return pl.pallas_call(
        body,
        name=name,
        grid_spec=pltpu.PrefetchScalarGridSpec(
            num_scalar_prefetch=1,
            grid=(hr // tr,),
            in_specs=[pl.BlockSpec((None, tr, c), lambda i, place_ref: (place_ref[0], i, 0))]
            + [landed(k) for k in range(n_p)],
            out_specs=pl.BlockSpec((None, tr, c), lambda i, place_ref: (place_ref[1], i, 0)),
        ),
        out_shape=jax.ShapeDtypeStruct((2, hr, c), F32),
        compiler_params=pltpu.CompilerParams(dimension_semantics=("parallel",)),
    )(place, own, *([rb] * n_p))


def _pair_sum_group(name, ps, qs, core):
    n = len(ps)

    def body(core_ref, *refs):
        for s in range(n):
            refs[2 * n + s][...] = (refs[s][...] + refs[n + s][...].astype(F32)).astype(WIRE_DTYPE)

    def piece(shape):
        return pl.BlockSpec((None, *shape), lambda a, core_ref: (a, 0, 0))

    return pl.pallas_call(
        body,
        name=name,
        grid_spec=pltpu.PrefetchScalarGridSpec(
            num_scalar_prefetch=1,
            grid=(ps[0].shape[0],),
            in_specs=[pl.BlockSpec((None, None, *p.shape[2:]), lambda a, core_ref: (a, core_ref[0], 0, 0)) for p in ps]
            + [piece(q.shape[1:]) for q in qs],
            out_specs=[piece(q.shape[1:]) for q in qs],
        ),
        out_shape=[jax.ShapeDtypeStruct(q.shape, WIRE_DTYPE) for q in qs],
        compiler_params=pltpu.CompilerParams(dimension_semantics=("parallel",)),
    )(core, *ps, *qs)


def _sum_pieces_group(name, owns, rbs, place):
    n = len(owns)
    n_p = rbs[0].shape[0]

    def body(place_ref, *refs):
        chip = place_ref[0]
        for s in range(n):
            lands = refs[n + s * n_p:n + (s + 1) * n_p]
            acc = None
            for k in range(n_p):
                tile = jnp.where(chip == k, refs[s][...], lands[k][...]).astype(F32)
                acc = tile if acc is None else acc + tile
            refs[n + n * n_p + s][...] = acc

    def landed(k, shape):
        return pl.BlockSpec((None, *shape), lambda i, place_ref: (jnp.where(place_ref[0] == k, (k + 1) % n_p, k), 0, 0))

    return pl.pallas_call(
        body,
        name=name,
        grid_spec=pltpu.PrefetchScalarGridSpec(
            num_scalar_prefetch=1,
            grid=(1,),
            in_specs=[pl.BlockSpec((None, *o.shape[1:]), lambda i, place_ref: (place_ref[0], 0, 0)) for o in owns]
            + [landed(k, rb.shape[1:]) for rb in rbs for k in range(n_p)],
            out_specs=[pl.BlockSpec((None, *rb.shape[1:]), lambda i, place_ref: (place_ref[1], 0, 0)) for rb in rbs],
        ),
        out_shape=[jax.ShapeDtypeStruct((2, *rb.shape[1:]), F32) for rb in rbs],
        compiler_params=pltpu.CompilerParams(dimension_semantics=("arbitrary",)),
    )(place, *owns, *[rb for rb in rbs for _ in range(n_p)])


HBM_SPEC = pl.BlockSpec(memory_space=pl.ANY)


def _place():
    x, y, c = lax.axis_index("x"), lax.axis_index("y"), lax.axis_index("c")
    chips = [(1 - x, y), (x, 1 - y), (1 - x, 1 - y)]
    return x, y, c, chips


HBM_ONLY = pl.BlockSpec(memory_space=pltpu.HBM)
SEM_SPEC = pl.BlockSpec(memory_space=pltpu.SEMAPHORE)
DATAFLOW = pltpu.SideEffectType.DATAFLOW_SIDE_EFFECTING


def _in_hbm(a):
    return pltpu.with_memory_space_constraint(a, pltpu.HBM)


def _shard_part(ref, is_split, slot, h):
    if not is_split:
        return ref.at[slot]
    hr = ref.shape[1] // 2
    return ref.at[slot, pl.ds(h * hr, hr), :]


TOKEN = jax.ShapeDtypeStruct((8, LANES), F32)
VMEM_SPEC = pl.BlockSpec(memory_space=pltpu.VMEM)


def _gather_start(name, bufs, split, groups, deps=()):
    n, ng = len(bufs), len(groups)

    def body(*refs):
        ins, sems = refs[:n], refs[n + len(deps):n + len(deps) + 2 * ng]
        refs[-1][...] = jnp.zeros(TOKEN.shape, TOKEN.dtype)
        x, y, c, chips = _place()
        me = 2 * x + y
        for g, members in enumerate(groups):
            for s, a in enumerate(members):
                mine = _shard_part(ins[a], split[a], me, c)
                for j, chip in enumerate(chips):
                    pltpu.make_async_remote_copy(
                        src_ref=mine, dst_ref=mine, send_sem=sems[2 * g].at[3 * s + j], recv_sem=sems[2 * g + 1].at[3 * s + j],
                        device_id=(*chip, c), device_id_type=MESH).start()

    res = pl.pallas_call(
        body,
        name=name,
        in_specs=[HBM_ONLY] * n + [ANY_SPEC] * len(deps),
        out_specs=[SEM_SPEC] * (2 * ng) + [HBM_ONLY] * n + [VMEM_SPEC],
        out_shape=[pltpu.SemaphoreType.DMA((3 * len(members),)) for members in groups for _ in range(2)]
        + [pltpu.HBM(b.shape, b.dtype) for b in bufs] + [TOKEN],
        input_output_aliases={a: 2 * ng + a for a in range(n)},
        compiler_params=pltpu.CompilerParams(has_side_effects=DATAFLOW),
    )(*[_in_hbm(b) for b in bufs], *deps)
    return [(res[2 * g], res[2 * g + 1]) for g in range(ng)], list(res[2 * ng:2 * ng + n]), res[-1]


def _gather_wait(name, bufs, split, sems, after):
    n = len(bufs)

    def body(*refs):
        ins, send_sems, recv_sems = refs[:n], refs[n], refs[n + 1]
        x, y, c, chips = _place()
        me = 2 * x + y
        for s in range(n):
            for j, chip in enumerate(chips):
                copy = pltpu.make_async_remote_copy(
                    src_ref=_shard_part(ins[s], split[s], me, c),
                    dst_ref=_shard_part(ins[s], split[s], 2 * chip[0] + chip[1], c),
                    send_sem=send_sems.at[3 * s + j], recv_sem=recv_sems.at[3 * s + j],
                    device_id=(*chip, c), device_id_type=MESH)
                copy.wait_send()
                copy.wait_recv()

    res = pl.pallas_call(
        body,
        name=name,
        in_specs=[HBM_ONLY] * n + [SEM_SPEC, SEM_SPEC] + [ANY_SPEC] * len(after),
        out_specs=[HBM_ONLY] * n,
        out_shape=[pltpu.HBM(b.shape, b.dtype) for b in bufs],
        input_output_aliases={a: a for a in range(n)},
        compiler_params=pltpu.CompilerParams(has_side_effects=DATAFLOW),
    )(*bufs, *sems, *after)
    return list(res)


def _pass_halves(name, bufs):
    n = len(bufs)

    def body(*refs):
        outs = refs[n:2 * n]
        send_sems, recv_sems = refs[2 * n:]
        x, y, c, chips = _place()
        sibling = (x, y, 1 - c)

        def copy(a, j, h):
            blk = _shard_part(outs[a], True, 2 * chips[j][0] + chips[j][1], h)
            return pltpu.make_async_remote_copy(
                src_ref=blk, dst_ref=blk, send_sem=send_sems.at[3 * a + j], recv_sem=recv_sems.at[3 * a + j],
                device_id=sibling, device_id_type=MESH)

        sends = [copy(a, j, c) for a in range(n) for j in range(3)]
        for cp in sends:
            cp.start()
        for a in range(n):
            for j in range(3):
                copy(a, j, 1 - c).wait_recv()
        for cp in sends:
            cp.wait_send()

    res = pl.pallas_call(
        body,
        name=name,
        in_specs=[HBM_SPEC] * n,
        out_specs=[HBM_SPEC] * n,
        out_shape=[jax.ShapeDtypeStruct(b.shape, b.dtype) for b in bufs],
        input_output_aliases={a: a for a in range(n)},
        scratch_shapes=[pltpu.SemaphoreType.DMA((3 * n,)), pltpu.SemaphoreType.DMA((3 * n,))],
    )(*bufs)
    return list(res)


def _exchange_start(name, pairs):
    n = len(pairs)

    def body(*refs):
        pair_refs, land_refs, send_sems, recv_sems = refs[:n], refs[n:2 * n], refs[2 * n], refs[2 * n + 1]
        x, y, c, chips = _place()
        me = 2 * x + y
        for a in range(n):
            for j, chip in enumerate(chips):
                pltpu.make_async_remote_copy(
                    src_ref=pair_refs[a].at[2 * chip[0] + chip[1]], dst_ref=land_refs[a].at[me],
                    send_sem=send_sems.at[3 * a + j], recv_sem=recv_sems.at[3 * a + j],
                    device_id=(*chip, c), device_id_type=MESH).start()
        refs[-1][...] = jnp.zeros(TOKEN.shape, TOKEN.dtype)

    res = pl.pallas_call(
        body,
        name=name,
        in_specs=[HBM_ONLY] * (2 * n),
        out_specs=[SEM_SPEC, SEM_SPEC] + [HBM_ONLY] * (2 * n) + [VMEM_SPEC],
        out_shape=[pltpu.SemaphoreType.DMA((3 * n,)), pltpu.SemaphoreType.DMA((3 * n,))]
        + [pltpu.HBM(p.shape, p.dtype) for p in pairs] * 2 + [TOKEN],
        input_output_aliases={a: 2 + a for a in range(2 * n)},
        compiler_params=pltpu.CompilerParams(has_side_effects=DATAFLOW),
    )(*[_in_hbm(p) for p in pairs], *[_in_hbm(lax.empty(p.shape, p.dtype)) for p in pairs])
    return (res[0], res[1]), list(res[2:2 + n]), list(res[2 + n:2 + 2 * n]), res[-1]


def _exchange_wait(name, groups, after):
    sizes = [len(pairs) for _, pairs, _ in groups]
    n_buf = 2 * sum(sizes)

    def body(*refs):
        x, y, c, chips = _place()
        at_buf, at_sem = 0, n_buf
        for n in sizes:
            pair_refs, land_refs = refs[at_buf:at_buf + n], refs[at_buf + n:at_buf + 2 * n]
            send_sems, recv_sems = refs[at_sem], refs[at_sem + 1]
            at_buf, at_sem = at_buf + 2 * n, at_sem + 2
            for a in range(n):
                for j, chip in enumerate(chips):
                    k = 2 * chip[0] + chip[1]
                    copy = pltpu.make_async_remote_copy(
                        src_ref=pair_refs[a].at[k], dst_ref=land_refs[a].at[k],
                        send_sem=send_sems.at[3 * a + j], recv_sem=recv_sems.at[3 * a + j],
                        device_id=(*chip, c), device_id_type=MESH)
                    copy.wait_send()
                    copy.wait_recv()

    bufs = [b for _, pairs, lands in groups for b in (*pairs, *lands)]
    sems = [s for group_sems, _, _ in groups for s in group_sems]
    res = pl.pallas_call(
        body,
        name=name,
        in_specs=[HBM_ONLY] * n_buf + [SEM_SPEC] * len(sems) + [ANY_SPEC] * len(after),
        out_specs=[HBM_ONLY] * n_buf,
        out_shape=[pltpu.HBM(b.shape, b.dtype) for b in bufs],
        input_output_aliases={a: a for a in range(n_buf)},
        compiler_params=pltpu.CompilerParams(has_side_effects=DATAFLOW),
    )(*bufs, *sems, *after)
    out, at = [], 0
    for n in sizes:
        out.append((list(res[at:at + n]), list(res[at + n:at + 2 * n])))
        at += 2 * n
    return out


def _swap_start(name, halves):
    n = len(halves)
    n_p = halves[0].shape[0]
    land_shapes = [(n_p, *h.shape[2:]) for h in halves]

    def body(*refs):
        half_refs, land_refs, send_sems, recv_sems = refs[:n], refs[n:2 * n], refs[2 * n], refs[2 * n + 1]
        x, y, c, _ = _place()
        for a in range(n):
            for p in range(n_p):
                pltpu.make_async_remote_copy(
                    src_ref=half_refs[a].at[p, 1 - c], dst_ref=land_refs[a].at[p],
                    send_sem=send_sems.at[n_p * a + p], recv_sem=recv_sems.at[n_p * a + p],
                    device_id=(x, y, 1 - c), device_id_type=MESH).start()
        refs[-1][...] = jnp.zeros(TOKEN.shape, TOKEN.dtype)

    res = pl.pallas_call(
        body,
        name=name,
        in_specs=[HBM_ONLY] * (2 * n),
        out_specs=[SEM_SPEC, SEM_SPEC] + [HBM_ONLY] * (2 * n) + [VMEM_SPEC],
        out_shape=[pltpu.SemaphoreType.DMA((n_p * n,)), pltpu.SemaphoreType.DMA((n_p * n,))]
        + [pltpu.HBM(h.shape, h.dtype) for h in halves]
        + [pltpu.HBM(s, h.dtype) for s, h in zip(land_shapes, halves)] + [TOKEN],
        input_output_aliases={a: 2 + a for a in range(2 * n)},
        compiler_params=pltpu.CompilerParams(has_side_effects=DATAFLOW),
    )(*[_in_hbm(h) for h in halves], *[_in_hbm(lax.empty(s, h.dtype)) for s, h in zip(land_shapes, halves)])
    return (res[0], res[1]), list(res[2:2 + n]), list(res[2 + n:2 + 2 * n]), res[-1]


def _swap_wait(name, halves, lands, sems, after):
    n = len(halves)
    n_p = halves[0].shape[0]

    def body(*refs):
        half_refs, land_refs, send_sems, recv_sems = refs[:n], refs[n:2 * n], refs[2 * n], refs[2 * n + 1]
        x, y, c, _ = _place()
        for a in range(n):
            for p in range(n_p):
                copy = pltpu.make_async_remote_copy(
                    src_ref=half_refs[a].at[p, 1 - c], dst_ref=land_refs[a].at[p],
                    send_sem=send_sems.at[n_p * a + p], recv_sem=recv_sems.at[n_p * a + p],
                    device_id=(x, y, 1 - c), device_id_type=MESH)
                copy.wait_send()
                copy.wait_recv()

    res = pl.pallas_call(
        body,
        name=name,
        in_specs=[HBM_ONLY] * (2 * n) + [SEM_SPEC, SEM_SPEC] + [ANY_SPEC] * len(after),
        out_specs=[HBM_ONLY] * (2 * n),
        out_shape=[pltpu.HBM(b.shape, b.dtype) for b in (*halves, *lands)],
        input_output_aliases={a: a for a in range(2 * n)},
        compiler_params=pltpu.CompilerParams(has_side_effects=DATAFLOW),
    )(*halves, *lands, *sems, *after)
    return list(res[:n]), list(res[n:])


def _join_start(name, bufs):
    n = len(bufs)

    def body(*refs):
        ins, send_sems, recv_sems = refs[:n], refs[n], refs[n + 1]
        x, y, c, _ = _place()
        for a in range(n):
            pltpu.make_async_remote_copy(
                src_ref=ins[a].at[c], dst_ref=ins[a].at[c], send_sem=send_sems.at[a], recv_sem=recv_sems.at[a],
                device_id=(x, y, 1 - c), device_id_type=MESH).start()
        refs[-1][...] = jnp.zeros(TOKEN.shape, TOKEN.dtype)

    res = pl.pallas_call(
        body,
        name=name,
        in_specs=[HBM_ONLY] * n,
        out_specs=[SEM_SPEC, SEM_SPEC] + [HBM_ONLY] * n + [VMEM_SPEC],
        out_shape=[pltpu.SemaphoreType.DMA((n,)), pltpu.SemaphoreType.DMA((n,))]
        + [pltpu.HBM(b.shape, b.dtype) for b in bufs] + [TOKEN],
        input_output_aliases={a: 2 + a for a in range(n)},
        compiler_params=pltpu.CompilerParams(has_side_effects=DATAFLOW),
    )(*[_in_hbm(b) for b in bufs])
    return (res[0], res[1]), list(res[2:2 + n]), res[-1]


def _join_wait(name, bufs, sems, after):
    n = len(bufs)

    def body(*refs):
        ins, send_sems, recv_sems = refs[:n], refs[n], refs[n + 1]
        x, y, c, _ = _place()
        for a in range(n):
            copy = pltpu.make_async_remote_copy(
                src_ref=ins[a].at[c], dst_ref=ins[a].at[1 - c], send_sem=send_sems.at[a], recv_sem=recv_sems.at[a],
                device_id=(x, y, 1 - c), device_id_type=MESH)
            copy.wait_send()
            copy.wait_recv()

    res = pl.pallas_call(
        body,
        name=name,
        in_specs=[HBM_ONLY] * n + [SEM_SPEC, SEM_SPEC] + [ANY_SPEC] * len(after),
        out_specs=[HBM_ONLY] * n,
        out_shape=[pltpu.HBM(b.shape, b.dtype) for b in bufs],
        input_output_aliases={a: a for a in range(n)},
        compiler_params=pltpu.CompilerParams(has_side_effects=DATAFLOW),
    )(*bufs, *sems, *after)
    return list(res)


def _allgather_small(name, block, deps=()):
    m_per, n = block.shape

    def body(x_ref, *rest):
        out_ref, send_sems, recv_sems, local_sem = rest[len(deps):]
        x, y, c, chips = _place()
        me, sibling = (x, y, c), (x, y, 1 - c)

        def rows(px, py, pc):
            return out_ref.at[pl.ds((4 * px + 2 * py + pc) * m_per, m_per), :]

        def copy(k, blk, to, src=None):
            return pltpu.make_async_remote_copy(
                src_ref=rows(*blk) if src is None else src, dst_ref=rows(*blk),
                send_sem=send_sems.at[k], recv_sem=recv_sems.at[k], device_id=to, device_id_type=MESH)

        mine = pltpu.make_async_copy(x_ref, rows(*me), local_sem)
        mine.start()
        first = [copy(0, me, sibling, src=x_ref)]
        first += [copy(1 + j, me, (*chip, c), src=x_ref) for j, chip in enumerate(chips)]
        for cp in first:
            cp.start()
        passed = [copy(4 + j, (*chip, c), sibling) for j, chip in enumerate(chips)]
        for j, chip in enumerate(chips):
            copy(1 + j, (*chip, c), me).wait_recv()
            passed[j].start()
        copy(0, sibling, me).wait_recv()
        for j, chip in enumerate(chips):
            copy(4 + j, (*chip, 1 - c), me).wait_recv()
        for cp in first + passed:
            cp.wait_send()
        mine.wait()

    return pl.pallas_call(
        body,
        name=name,
        out_shape=jax.ShapeDtypeStruct((8 * m_per, n), block.dtype),
        in_specs=[pl.BlockSpec(memory_space=pltpu.VMEM)] + [ANY_SPEC] * len(deps),
        out_specs=pl.BlockSpec(memory_space=pltpu.VMEM),
        scratch_shapes=[pltpu.SemaphoreType.DMA((7,)), pltpu.SemaphoreType.DMA((7,)), pltpu.SemaphoreType.DMA],
    )(block, *deps)


def _sum_blocks(name, gathered, n_blocks):
    r = gathered.shape[0] // n_blocks
    c = gathered.shape[1]

    def body(g_ref, o_ref):
        acc = g_ref[0:r, :]
        for b in range(1, n_blocks):
            acc = acc + g_ref[b * r:(b + 1) * r, :]
        o_ref[...] = acc

    return pl.pallas_call(body, name=name, out_shape=jax.ShapeDtypeStruct((r, c), F32))(gathered)


def _largest_tile(n, cap, mult):
    best = None
    for d in range(mult, min(n, cap) + 1, mult):
        if n % d == 0:
            best = d
    assert best is not None, (n, cap, mult)
    return best


def kernel(x, meta, g_pre_mix, w_in, b_gates, conf_dw_w, conf_dw_b, conf_ln_g, conf_ln_b, conf_w_pw, short_dw_w, short_w_out, w_o, g_post_mix, g_pre_mlp, w_up, w_down, g_post_mlp, loss_target, m_meta, m_g_pre_mix, m_w_in, m_b_gates, m_conf_dw_w, m_conf_dw_b, m_conf_ln_g, m_conf_ln_b, m_conf_w_pw, m_short_dw_w, m_short_w_out, m_w_o, m_g_post_mix, m_g_pre_mlp, m_w_up, m_w_down, m_g_post_mlp, v_meta, v_g_pre_mix, v_w_in, v_b_gates, v_conf_dw_w, v_conf_dw_b, v_conf_ln_g, v_conf_ln_b, v_conf_w_pw, v_short_dw_w, v_short_w_out, v_w_o, v_g_post_mix, v_g_pre_mlp, v_w_up, v_w_down, v_g_post_mlp):
    seq, d = x.shape[1], x.shape[2]
    t_real = seq + N_META
    t = -(-t_real // LANES) * LANES
    d_conf = conf_dw_b.shape[1]
    d_ff = w_up.shape[2] * N_CHIPS
    in_cols = w_in.shape[2] * N_CHIPS
    assert in_cols == 5 * d_conf + 2 * d and d == 2 * d_conf
    cw = d_conf
    core = lax.axis_index("c")
    chip = 2 * lax.axis_index("x") + lax.axis_index("y")

    tr = _largest_tile(t, 272, ROW_CHUNK)
    tm = t
    tn_in = _largest_tile(in_cols // N_CHIPS, 768, LANES)
    tn_d = _largest_tile(d, 1024, LANES)
    tn_h = _largest_tile(d, 512, LANES)
    tn_ff = _largest_tile(d_ff // N_CHIPS, 1024, LANES)
    tn_pw = _largest_tile(d // N_CHIPS, 512, LANES)

    big = [w_in[0], conf_w_pw[0], short_w_out[0], w_o[0], w_up[0], w_down[0]]
    chip_arr = chip.astype(jnp.int32).reshape(1)

    def cast(a, deps):
        return _into_slot(f"cast_w{a}", big[a], N_CHIPS, chip_arr, MXU_DTYPE, _largest_tile(big[a].shape[0], 256, 16), deps)

    small = [_into_slot(f"place_w{a}", w, N_CHIPS, chip_arr, F32, w.shape[0])
             for a, w in enumerate([meta, conf_dw_w[0], short_dw_w[0]])]
    sems_small, fly_small, tok_small = _gather_start("gather_start_small", small, [False] * 3, [[0, 1, 2]])
    sems_in, fly_in, tok_in = _gather_start("gather_start_in", [cast(0, [tok_small])], [True], [[0]])
    rest_groups = [[0, 1], [2], [3], [4]]
    sems_rest, fly_rest, tok_rest = _gather_start(
        "gather_start_rest", [cast(a, [tok_in]) for a in range(1, 6)], [True] * 5, rest_groups)

    def arrive(g, after):
        members = rest_groups[g]
        got = _gather_wait(f"gather_wait_rest{g}", [fly_rest[a] for a in members], [True] * len(members),
                           sems_rest[g], after)
        return _pass_halves(f"gather_pass_rest{g}", got)

    meta_g, wdw_g, w3_g = _gather_wait("gather_wait_small", fly_small, [False] * 3, sems_small[0], [tok_in])
    meta_full = jnp.transpose(meta_g, (1, 0, 2)).reshape(N_META, d)
    wdw = jnp.transpose(wdw_g, (1, 0, 2)).reshape(CONF_KERNEL, d_conf)
    w3 = jnp.transpose(w3_g, (1, 0, 2)).reshape(SHORT_KERNEL, d_conf)

    tail = jnp.zeros((t - t_real, d), F32)
    h0 = jnp.concatenate([meta_full, x[0], tail], axis=0)
    target = jnp.concatenate([jnp.zeros((N_META, d), F32), loss_target[0], tail], axis=0)

    def norm_in(i, rows, vecs):
        return [_rms_fwd(rows[0], vecs[0])], []

    (n_lp,) = _rowwise("norm_in", norm_in, [(h0, d, 0)], [g_pre_mix], [(d, MXU_DTYPE)], [], tr, deps=[tok_rest])
    (wg_in,) = _pass_halves("gather_pass_in", _gather_wait("gather_wait_in", fly_in, [True], sems_in[0], [n_lp]))
    proj =_matmul("proj", n_lp, wg_in, kind="nn", tm=tm, tn=tn_in, tk=d, out_dtypes=[F32])
    ac, c3 = _conv_fwd(proj, wdw, conf_dw_b, w3, d_conf)

    def ln_parts(ac_t, ln_g, ln_b):
        mu = jnp.mean(ac_t, axis=-1, keepdims=True)
        xc = ac_t - mu
        rstd = lax.rsqrt(jnp.mean(xc * xc, axis=-1, keepdims=True) + LN_EPS)
        xh = xc * rstd
        return xh, rstd, xh * ln_g + ln_b

    def branch_act(i, rows, vecs):
        ac_t, c3_t, bg_t = rows
        _, _, al = ln_parts(ac_t, vecs[0], vecs[1])
        return [al * _sigmoid(al), bg_t * c3_t], []

    a_act, s_lp = _rowwise("branch_act", branch_act, [(ac, cw, 0), (c3, cw, 0), (proj, cw, 2)],
                           [conf_ln_g, conf_ln_b], [(d_conf, MXU_DTYPE), (d_conf, MXU_DTYPE)], [], tr)
    wg_pw, wg_sout = arrive(0, [a_act])
    y_a = _matmul("y_a", a_act, wg_pw, kind="nn", tm=tm, tn=tn_pw, tk=d_conf, out_dtypes=[F32])
    y_b = _matmul("y_b", s_lp, wg_sout, kind="nn", tm=tm, tn=tn_pw, tk=d_conf, out_dtypes=[F32])

    gate_rows = [(proj, cw, 5), (proj, cw, 6), (proj, cw, 7), (proj, cw, 8)]

    def gates_of(rows, b):
        ga = _sigmoid(jnp.concatenate([rows[0], rows[1]], axis=1) + b[:, :d])
        gb = _sigmoid(jnp.concatenate([rows[2], rows[3]], axis=1) + b[:, d:])
        return ga, gb

    def gate(i, rows, vecs):
        ga, gb = gates_of(rows[2:], vecs[0])
        return [ga * rows[0] + gb * rows[1]], []

    (m_lp,) = _rowwise("gate", gate, [(y_a, d, 0), (y_b, d, 0)] + gate_rows, [b_gates], [(d, MXU_DTYPE)], [], tr)
    wg_o = arrive(1, [m_lp])[0].reshape(d, d)
    mix = _matmul("mix", m_lp, wg_o, kind="nn", tm=tm, tn=tn_d, tk=d, out_dtypes=[F32])

    def post_mix(i, rows, vecs):
        h1_t = rows[0] + _rms_fwd(rows[1], vecs[0])
        return [h1_t, _rms_fwd(h1_t, vecs[1])], []

    h1, n2_lp = _rowwise("post_mix", post_mix, [(h0, d, 0), (mix, d, 0)], [g_post_mix, g_pre_mlp],
                         [(d, F32), (d, MXU_DTYPE)], [], tr)
    (wg_up,) = arrive(2, [n2_lp])
    up, f_lp = _matmul("up", n2_lp, wg_up, kind="nn", tm=tm, tn=tn_h, tk=d, out_dtypes=[F32, MXU_DTYPE],
                       epilogue=lambda acc: (acc, jnp.square(jnp.maximum(acc, 0.0))))
    wg_down = arrive(3, [f_lp])[0].reshape(d_ff, d)
    dn = _matmul("down", f_lp, wg_down, kind="nn", tm=tm, tn=tn_h, tk=_largest_tile(d_ff, 2048, LANES),
                 out_dtypes=[F32])

    def head(row0, rows, vecs):
        h1_t, dn_t, tgt = rows
        y = h1_t + _rms_fwd(dn_t, vecs[0])
        row = row0 + lax.broadcasted_iota(jnp.int32, (ROW_CHUNK, 1), 0)
        err = jnp.where(jnp.logical_and(row >= N_META, row < t_real), y - tgt, 0.0)
        dy = err / d
        d_dn, dg = _rms_bwd(dn_t, vecs[0], dy)
        loss_rows = 0.5 * jnp.mean(err * err, axis=-1, keepdims=True)
        return [dy, d_dn], [dg, jnp.broadcast_to(loss_rows, (ROW_CHUNK, LANES))]

    dy, d_dn, dg_post_mlp, loss_vec = _rowwise(
        "head", head, [(h1, d, 0), (dn, d, 0), (target, d, 0)], [g_post_mlp], [(d, F32), (d, MXU_DTYPE)], [d, LANES], tr)

    core_arr = core.astype(jnp.int32).reshape(1)
    place = jnp.stack([chip, core]).astype(jnp.int32)
    in_flight = {}


    def tag(members):
        return "".join(str(a) for a in members)

    own_halves = {}
    two_types = dict(out_dtypes=[F32, WIRE_DTYPE], epilogue=lambda acc: (acc, acc))

    def swap_start(members, gws):
        def in_halves(gw):
            return gw.reshape(N_CHIPS, 2, gw.shape[-2] // 2, gw.shape[-1])

        own_halves[members] = [in_halves(gw.reshape(N_CHIPS, -1, gw.shape[-1])) for gw, _ in gws]
        for_sibling = [in_halves(lp.reshape(N_CHIPS, -1, lp.shape[-1])) for _, lp in gws]
        *in_flight[members], token = _swap_start(f"swap_start{tag(members)}", for_sibling)
        return token

    def exchange_start(members, after):
        sems, halves, lands = in_flight[members]
        _, lands = _swap_wait(f"swap_wait{tag(members)}", halves, lands, sems, after)
        if len(members) > 1:
            pairs = _pair_sum_group(f"pair_sum{tag(members)}", own_halves[members], lands, core_arr)
        else:
            pairs = [_pair_sum(f"pair_sum{a}", h, q, core_arr, _largest_tile(q.shape[1], 256, 16))
                     for a, h, q in zip(members, own_halves[members], lands)]
        *in_flight[members], token = _exchange_start(f"exchange_start{tag(members)}", pairs)
        return token

    def reduce_finish(groups, after):
        waited = _exchange_wait("exchange_wait" + "_".join(tag(g) for g in groups), [in_flight[g] for g in groups], after)
        out = {}
        for g, (pairs, lands) in zip(groups, waited):
            if len(g) > 1:
                out.update(zip(g, _sum_pieces_group(f"sum_pieces{tag(g)}", pairs, lands, place)))
            else:
                out[g[0]] = _sum_pieces(f"sum_pieces{g[0]}", pairs[0], lands[0], place,
                                        _largest_tile(lands[0].shape[1], 256, 16))
        return out

    d_up = _matmul("d_up", d_dn, wg_down, kind="nt", tm=tm, tn=tn_h, tk=d, out_dtypes=[MXU_DTYPE], extras=[up],
                   epilogue=lambda acc, up_t: (acc * (2.0 * jnp.maximum(up_t, 0.0)),))
    tk_t = t
    gw_down = _matmul("gw_down", f_lp, d_dn, kind="tn", tm=_largest_tile(d_ff, 2048, LANES), tn=tn_h, tk=tk_t,
                      **two_types)
    tok = swap_start((5,), [gw_down])
    d_n2 = _matmul("d_n2", d_up, wg_up, kind="nt", tm=tm, tn=tn_h, tk=d_ff // N_CHIPS, out_dtypes=[F32], deps=[tok])
    tok = exchange_start((5,), [d_n2])
    gw_up = _matmul("gw_up", n2_lp, d_up, kind="tn", tm=_largest_tile(d, 2048, LANES), tn=tn_h, tk=tk_t,
                    out_pieces=N_CHIPS, deps=[tok], **two_types)
    tok = swap_start((4,), [gw_up])

    def bwd_mid(i, rows, vecs):
        dy_t, dn2_t, h1_t, mix_t = rows
        d_h1a, dg_pre_mlp = _rms_bwd(h1_t, vecs[1], dn2_t)
        d_h1 = dy_t + d_h1a
        d_mix, dg_post_mix = _rms_bwd(mix_t, vecs[0], d_h1)
        return [d_h1, d_mix], [dg_pre_mlp, dg_post_mix]

    d_h1, d_mix, dg_pre_mlp, dg_post_mix = _rowwise(
        "bwd_mid", bwd_mid, [(dy, d, 0), (d_n2, d, 0), (h1, d, 0), (mix, d, 0)], [g_post_mix, g_pre_mlp],
        [(d, F32), (d, MXU_DTYPE)], [d, d], tr, deps=[tok])
    d_m = _matmul("d_m", d_mix, wg_o, kind="nt", tm=tm, tn=tn_h, tk=d, out_dtypes=[F32])
    tok = exchange_start((4,), [d_m])
    gw_o = _matmul("gw_o", m_lp, d_mix, kind="tn", tm=_largest_tile(d, 2048, LANES), tn=tn_h, tk=tk_t, deps=[tok],
                   **two_types)

    def gate_bwd(i, rows, vecs):
        dm_t, ya_t, yb_t = rows[:3]
        ga, gb = gates_of(rows[3:], vecs[0])
        d_gpre = jnp.concatenate([dm_t * ya_t * ga * (1.0 - ga), dm_t * yb_t * gb * (1.0 - gb)], axis=1)
        return [dm_t * ga, dm_t * gb, d_gpre], [d_gpre]

    d_ya, d_yb, d_gpre, dg_b_gates = _rowwise(
        "gate_bwd", gate_bwd, [(d_m, d, 0), (y_a, d, 0), (y_b, d, 0)] + gate_rows, [b_gates],
        [(d, MXU_DTYPE), (d, MXU_DTYPE), (2 * d, MXU_DTYPE)], [2 * d], tr, deps=[tok])
    d_aact = _matmul("d_aact", d_ya, wg_pw, kind="nt", tm=tm, tn=d_conf // 2, tk=tn_pw, out_dtypes=[F32], deps=[tok])
    gw_pw = _matmul("gw_pw", a_act, d_ya, kind="tn", tm=_largest_tile(d_conf, 2048, LANES), tn=tn_pw, tk=tk_t,
                    out_pieces=N_CHIPS, **two_types)
    d_s = _matmul("d_s", d_yb, wg_sout, kind="nt", tm=tm, tn=d_conf // 2, tk=tn_pw, out_dtypes=[F32], deps=[tok])
    gw_sout = _matmul("gw_sout", s_lp, d_yb, kind="tn", tm=_largest_tile(d_conf, 2048, LANES), tn=tn_pw, tk=tk_t,
                      out_pieces=N_CHIPS, deps=[tok], **two_types)
    small_big = (3, 1, 2)
    tok = swap_start(small_big, [gw_o, gw_pw, gw_sout])

    def branch_bwd(i, rows, vecs):
        daact_t, ds_t, ac_t, c3_t, bg_t = rows
        xh, rstd, al = ln_parts(ac_t, vecs[0], vecs[1])
        sg = _sigmoid(al)
        d_al = daact_t * (sg * (1.0 + al * (1.0 - sg)))
        dxh = d_al * vecs[0]
        d_ac = rstd * (dxh - jnp.mean(dxh, axis=-1, keepdims=True) - xh * jnp.mean(dxh * xh, axis=-1, keepdims=True))
        return [d_ac, ds_t * bg_t, ds_t * c3_t], [d_al * xh, d_al, d_ac]

    d_ac, d_c3, d_bg, dg_ln_g, dg_ln_b, dg_dw_b = _rowwise(
        "branch_bwd", branch_bwd, [(d_aact, cw, 0), (d_s, cw, 0), (ac, cw, 0), (c3, cw, 0), (proj, cw, 2)],
        [conf_ln_g, conf_ln_b], [(d_conf, F32), (d_conf, F32), (d_conf, MXU_DTYPE)], [d_conf] * 3, tr, deps=[tok])
    d_av, d_ag, d_cg, d_v, dg_wdw, dg_w3 = _conv_bwd(proj, d_ac, d_c3, wdw, w3, d_conf)
    d_proj = jnp.concatenate([d_av, d_ag, d_bg, d_cg, d_v, d_gpre], axis=1)
    small_w = d_conf

    def pack(arrs):
        flat = jnp.concatenate([a.reshape(-1, small_w) for a in arrs], axis=0)
        return jnp.pad(flat, ((0, -flat.shape[0] % 8), (0, 0)))

    def reduce_small(tag, arrs, deps):
        assert all(a.size % small_w == 0 for a in arrs)
        gathered = _allgather_small(f"allgather_small_{tag}", pack(arrs), deps)
        first, r0 = [], 0
        for a in arrs:
            first.append(r0)
            r0 += a.size // small_w
        return _sum_blocks(f"sum_small_{tag}", gathered, 8), first

    def rows_of(total, r0, like):
        return total[r0:r0 + like.size // small_w].reshape(like.shape)

    tok = exchange_start(small_big, [d_proj])

    gw_in = _matmul("gw_in", n_lp, d_proj, kind="tn", tm=_largest_tile(d, 2048, LANES), tn=tn_in, tk=tk_t,
                    out_pieces=N_CHIPS, deps=[tok], **two_types)
    tok = swap_start((0,), [gw_in])
    d_n = _matmul("d_n", d_proj, wg_in, kind="nt", tm=tm, tn=tn_h, tk=in_cols // N_CHIPS, out_dtypes=[F32], deps=[tok])
    tok = exchange_start((0,), [d_n])

    early = [5, 4, 3, 1, 2]
    reduced = reduce_finish([(5,), (4,), small_big], [tok])
    join_sems, joining, tok = _join_start("join_start_early", [reduced[a] for a in early])

    def bwd_in(i, rows, vecs):
        d_h0a, dg = _rms_bwd(rows[2], vecs[0], rows[1])
        return [rows[0] + d_h0a], [dg]

    d_h0, dg_pre_mix = _rowwise("bwd_in", bwd_in, [(d_h1, d, 0), (d_n, d, 0), (h0, d, 0)], [g_pre_mix],
                                [(d, F32)], [d], tr, deps=[tok])
    grad_x = d_h0[N_META:t_real][None]

    big_m = [m_w_in, m_conf_w_pw, m_short_w_out, m_w_o, m_w_up, m_w_down]
    big_v = [v_w_in, v_conf_w_pw, v_short_w_out, v_w_o, v_w_up, v_w_down]
    big_res = {}

    def adam_group(members, joined):
        for a, j in zip(members, joined):
            big_res[a] = _adamw(f"adamw_big{a}", big[a], j.reshape(big[a].shape), big_m[a][0], big_v[a][0],
                                _largest_tile(big[a].shape[0], 256, 8))

    adam_group(early, _join_wait("join_wait_early", joining, join_sems, [d_h0]))
    reduced = reduce_finish([(0,)], [big_res[a][1] for a in early])
    join_sems, joining, tok = _join_start("join_start_late", [reduced[0]])
    rep_g = [dg_pre_mix, dg_b_gates, dg_dw_b, dg_ln_g, dg_ln_b, dg_post_mix, dg_pre_mlp, dg_post_mlp]
    rep_w = [g_pre_mix, b_gates, conf_dw_b, conf_ln_g, conf_ln_b, g_post_mix, g_pre_mlp, g_post_mlp]
    rep_m = [m_g_pre_mix, m_b_gates, m_conf_dw_b, m_conf_ln_g, m_conf_ln_b, m_g_post_mix, m_g_pre_mlp, m_g_post_mlp]
    rep_v = [v_g_pre_mix, v_b_gates, v_conf_dw_b, v_conf_ln_g, v_conf_ln_b, v_g_post_mix, v_g_pre_mlp, v_g_post_mlp]
    col_g = [d_h0[:N_META], dg_wdw, dg_w3]
    loss_row = jnp.tile(loss_vec, (1, small_w // LANES))
    total, first = reduce_small("all", rep_g + col_g + [loss_row], [tok])
    loss = total[first[-1], 0]
    rep_res = _small_adamw("adamw_rep", total, [(first[e], w, m, v) for e, (w, m, v) in enumerate(zip(rep_w, rep_m, rep_v))])
    col_res = {}
    for a, (w, g_part, m, v) in enumerate([(meta, col_g[0], m_meta, v_meta), (conf_dw_w[0], col_g[1], m_conf_dw_w[0], v_conf_dw_w[0]),
                                           (short_dw_w[0], col_g[2], m_short_dw_w[0], v_short_dw_w[0])]):
        g_full = rows_of(total, first[len(rep_g) + a], g_part)
        g_own = lax.dynamic_slice_in_dim(g_full, chip * w.shape[1], w.shape[1], axis=1)
        col_res[a] = _adamw(f"adamw_col{a}", w, g_own, m, v, w.shape[0])
    adam_group([0], _join_wait("join_wait_late", joining, join_sems, [rep_res[0][1]] + [col_res[a][1] for a in range(3)]))

    def leaf(q):
        r = lambda a: rep_res[a][q]
        b = lambda a: big_res[a][q][None]
        return [col_res[0][q], r(0), b(0), r(1), col_res[1][q][None], r(2), r(3), r(4), b(1), col_res[2][q][None], b(2),
                b(3), r(5), r(6), b(4), b(5), r(7)]

    return (loss, grad_x, *leaf(0), *leaf(1), *leaf(2), *leaf(3))
```

```python
import jax
import jax.numpy as jnp
from jax import lax
from jax.experimental import pallas as pl
from jax.experimental.pallas import tpu as pltpu

F32 = jnp.float32
BF16 = jnp.bfloat16
MXU_DTYPE = BF16
WIRE_DTYPE = BF16

N_META = 16
CONF_KERNEL = 31
SHORT_KERNEL = 3
CONV_PAD = 32
RMS_EPS = 1e-6
LN_EPS = 1e-5
ADAM_LR = 0.001
ADAM_B1 = 0.9
ADAM_B2 = 0.999
ADAM_EPS = 1e-08
ADAM_WD = 0.01
ADAM_STEP = 10

N_CHIPS = 4
MESH = pl.DeviceIdType.MESH
LANES = 128


def _sigmoid(z):
    return 1.0 / (1.0 + jnp.exp(-z))


ANY_SPEC = pl.BlockSpec(memory_space=pl.ANY)


def _matmul(name, a, b, *, kind, tm, tn, tk, out_dtypes, out_pieces=1, epilogue=None, extras=(), deps=()):
    pieces = b.shape[0] if b.ndim == 3 else 1
    if kind == "nn":
        m, kdim = a.shape
        n = b.shape[-1] * pieces
        dims = (((1,), (0,)), ((), ()))
        a_spec = pl.BlockSpec((tm, tk), lambda i, j, k: (i, k))
        if b.ndim == 2:
            b_spec = pl.BlockSpec((tk, tn), lambda i, j, k: (k, j))
        else:
            npp = b.shape[-1] // tn
            b_spec = pl.BlockSpec((None, tk, tn), lambda i, j, k: (j // npp, k, j % npp))
    elif kind == "nt":
        m, kdim = a.shape
        n = b.shape[-2]
        dims = (((1,), (1,)), ((), ()))
        a_spec = pl.BlockSpec((tm, tk), lambda i, j, k: (i, k))
        if b.ndim == 2:
            b_spec = pl.BlockSpec((tn, tk), lambda i, j, k: (j, k))
        else:
            kpp = b.shape[-1] // tk
            b_spec = pl.BlockSpec((None, tn, tk), lambda i, j, k: (k // kpp, j, k % kpp))
    else:
        kdim, m = a.shape
        n = b.shape[-1]
        dims = (((0,), (0,)), ((), ()))
        a_spec = pl.BlockSpec((tk, tm), lambda i, j, k: (k, i))
        b_spec = pl.BlockSpec((tk, tn), lambda i, j, k: (k, j))
    assert m % tm == 0 and n % tn == 0 and kdim % tk == 0, (name, m, n, kdim, tm, tn, tk)
    nk = kdim // tk
    if out_pieces == 1:
        out_shape = (m, n)
        out_spec = pl.BlockSpec((tm, tn), lambda i, j, k: (i, j))
    else:
        onpp = n // out_pieces // tn
        out_shape = (out_pieces, m, n // out_pieces)
        out_spec = pl.BlockSpec((None, tm, tn), lambda i, j, k: (j // onpp, i, j % onpp))
    n_ex, n_out, n_in = len(extras), len(out_dtypes), len(extras) + len(deps)
    if epilogue is None:
        epilogue = lambda acc: (acc,)

    def body(a_ref, b_ref, *rest):
        ex_refs, o_refs = rest[:n_ex], rest[n_in:n_in + n_out]
        prod = lax.dot_general(a_ref[...], b_ref[...], dims, preferred_element_type=F32)

        def finish(acc):
            tiles = epilogue(acc, *[r[...] for r in ex_refs])
            for o_ref, t in zip(o_refs, tiles):
                o_ref[...] = t.astype(o_ref.dtype)

        if nk == 1:
            finish(prod)
        else:
            acc_ref = rest[n_in + n_out]
            k = pl.program_id(2)

            @pl.when(k == 0)
            def _():
                acc_ref[...] = prod

            @pl.when(jnp.logical_and(k > 0, k < nk - 1))
            def _():
                acc_ref[...] += prod

            @pl.when(k == nk - 1)
            def _():
                finish(acc_ref[...] + prod)

    ex_specs = [pl.BlockSpec((tm, tn), lambda i, j, k: (i, j)) for _ in extras]
    res = pl.pallas_call(
        body,
        name=name,
        grid=(m // tm, n // tn, nk),
        in_specs=[a_spec, b_spec, *ex_specs] + [ANY_SPEC] * len(deps),
        out_specs=[out_spec] * n_out,
        out_shape=[jax.ShapeDtypeStruct(out_shape, d) for d in out_dtypes],
        scratch_shapes=[pltpu.VMEM((tm, tn), F32)] if nk > 1 else [],
        compiler_params=pltpu.CompilerParams(dimension_semantics=("parallel", "parallel", "arbitrary")),
    )(a, b, *extras, *deps)
    return res[0] if n_out == 1 else res


STREAM_BUFFERS = 3
ROW_CHUNK = 16
SUBLANES = 8


def _rowwise(name, fn, rows, vecs, outs, sums, tr, deps=()):
    t = rows[0][0].shape[0]
    assert t % tr == 0 and tr % ROW_CHUNK == 0
    n_r, n_v, n_o, n_s = len(rows), len(vecs), len(outs), len(sums)
    n_in = n_r + n_v + len(deps)
    n_steps = t // tr

    def body(*refs):
        r_in, v_in = refs[:n_r], refs[n_r:n_r + n_v]
        o_refs = refs[n_in:n_in + n_o]
        s_refs = refs[n_in + n_o:n_in + n_o + n_s]
        acc_refs = refs[n_in + n_o + n_s:]
        i = pl.program_id(0)

        @pl.when(i == 0)
        def _():
            for acc_ref in acc_refs:
                acc_ref[...] = jnp.zeros(acc_ref.shape, F32)

        def chunk(ci):
            r0 = ci * ROW_CHUNK
            sl = pl.ds(r0, ROW_CHUNK)
            o_tiles, s_tiles = fn(i * tr + r0, [r[sl, :] for r in r_in], [v[...] for v in v_in])
            for o_ref, tile in zip(o_refs, o_tiles):
                o_ref[sl, :] = tile.astype(o_ref.dtype)
            for acc_ref, tile in zip(acc_refs, s_tiles):
                part = tile[0:SUBLANES]
                for s in range(1, ROW_CHUNK // SUBLANES):
                    part = part + tile[s * SUBLANES:(s + 1) * SUBLANES]
                acc_ref[...] += part

        for ci in range(tr // ROW_CHUNK):
            chunk(ci)

        @pl.when(i == n_steps - 1)
        def _():
            for s_ref, acc_ref in zip(s_refs, acc_refs):
                s_ref[...] = jnp.sum(acc_ref[...], axis=0, keepdims=True)

    def row_spec(width, blk):
        return pl.BlockSpec((tr, width), lambda i: (i, blk))

    res = pl.pallas_call(
        body,
        name=name,
        grid=(t // tr,),
        in_specs=[row_spec(w, blk) for _, w, blk in rows]
        + [pl.BlockSpec(v.shape, lambda i: (0, 0)) for v in vecs] + [ANY_SPEC] * len(deps),
        out_specs=[pl.BlockSpec((tr, c), lambda i: (i, 0)) for c, _ in outs]
        + [pl.BlockSpec((1, c), lambda i: (0, 0)) for c in sums],
        out_shape=[jax.ShapeDtypeStruct((t, c), d) for c, d in outs]
        + [jax.ShapeDtypeStruct((1, c), F32) for c in sums],
        scratch_shapes=[pltpu.VMEM((SUBLANES, c), F32) for c in sums],
        compiler_params=pltpu.CompilerParams(dimension_semantics=("arbitrary",)),
    )(*[r[0] for r in rows], *vecs, *deps)
    return res


def _rms_fwd(x, g):
    r = lax.rsqrt(jnp.mean(x * x, axis=-1, keepdims=True) + RMS_EPS)
    return x * r * g


def _rms_bwd(x, g, dy):
    r = lax.rsqrt(jnp.mean(x * x, axis=-1, keepdims=True) + RMS_EPS)
    xn = x * r
    dxn = dy * g
    dx = r * (dxn - xn * jnp.mean(dxn * xn, axis=-1, keepdims=True))
    return dx, dy * xn


CONV_ROWS = 64
CONV_LANES = 128


def _conv_fwd(proj, wdw, bdw, w3, d_conf):
    t = proj.shape[0]
    cl = CONV_LANES
    nb = d_conf // cl
    nchunk = t // CONV_ROWS
    assert t % CONV_ROWS == 0

    def body(av_ref, ag_ref, cg_ref, v_ref, wdw_ref, bdw_ref, w3_ref, ac_ref, c3_ref, apad, cpad):
        zeros = jnp.zeros((CONV_PAD, cl), F32)
        apad[0:CONV_PAD, :] = zeros
        cpad[0:CONV_PAD, :] = zeros
        apad[CONV_PAD:, :] = av_ref[...] * _sigmoid(ag_ref[...])
        cpad[CONV_PAD:, :] = cg_ref[...] * v_ref[...]

        def chunk(ci, carry):
            base = pl.multiple_of(ci * CONV_ROWS, 8)
            acc = jnp.zeros((CONV_ROWS, cl), F32) + bdw_ref[...]
            for k in range(CONF_KERNEL):
                off = CONV_PAD - (CONF_KERNEL - 1) + k
                acc = acc + apad[pl.ds(base + off, CONV_ROWS), :] * wdw_ref[k:k + 1, :]
            ac_ref[pl.ds(base, CONV_ROWS), :] = acc
            acc3 = jnp.zeros((CONV_ROWS, cl), F32)
            for k in range(SHORT_KERNEL):
                off = CONV_PAD - (SHORT_KERNEL - 1) + k
                acc3 = acc3 + cpad[pl.ds(base + off, CONV_ROWS), :] * w3_ref[k:k + 1, :]
            c3_ref[pl.ds(base, CONV_ROWS), :] = acc3
            return carry

        lax.fori_loop(0, nchunk, chunk, 0)

    def col(blk0):
        return pl.BlockSpec((t, cl), lambda j: (0, blk0 + j))

    return pl.pallas_call(
        body,
        name="conv_fwd",
        grid=(nb,),
        in_specs=[col(0), col(nb), col(3 * nb), col(4 * nb),
                  pl.BlockSpec((CONF_KERNEL, cl), lambda j: (0, j)),
                  pl.BlockSpec((1, cl), lambda j: (0, j)),
                  pl.BlockSpec((SHORT_KERNEL, cl), lambda j: (0, j))],
        out_specs=[pl.BlockSpec((t, cl), lambda j: (0, j))] * 2,
        out_shape=[jax.ShapeDtypeStruct((t, d_conf), F32)] * 2,
        scratch_shapes=[pltpu.VMEM((t + CONV_PAD, cl), F32)] * 2,
        compiler_params=pltpu.CompilerParams(dimension_semantics=("parallel",)),
    )(proj, proj, proj, proj, wdw, bdw, w3)


def _conv_bwd(proj, d_ac, d_c3, wdw, w3, d_conf):
    t = proj.shape[0]
    cl = CONV_LANES
    nb = d_conf // cl
    nchunk = t // CONV_ROWS
    nsub = CONV_ROWS // 8

    def fold(p):
        r = p[0:8]
        for s in range(1, nsub):
            r = r + p[8 * s:8 * s + 8]
        return r

    def body(av_ref, ag_ref, cg_ref, v_ref, dac_ref, dc3_ref, wdw_ref, w3_ref,
             dav_ref, dag_ref, dcg_ref, dv_ref, dwdw_ref, dw3_ref, apad, cpad, dapad, dcpad):
        zeros = jnp.zeros((CONV_PAD, cl), F32)
        apad[0:CONV_PAD, :] = zeros
        cpad[0:CONV_PAD, :] = zeros
        apad[CONV_PAD:, :] = av_ref[...] * _sigmoid(ag_ref[...])
        cpad[CONV_PAD:, :] = cg_ref[...] * v_ref[...]
        dapad[0:t, :] = dac_ref[...]
        dcpad[0:t, :] = dc3_ref[...]
        dapad[t:, :] = zeros
        dcpad[t:, :] = zeros

        def chunk(ci, accs):
            base = pl.multiple_of(ci * CONV_ROWS, 8)
            rows = pl.ds(base, CONV_ROWS)
            da = jnp.zeros((CONV_ROWS, cl), F32)
            for k in range(CONF_KERNEL):
                da = da + dapad[pl.ds(base + (CONF_KERNEL - 1 - k), CONV_ROWS), :] * wdw_ref[k:k + 1, :]
            dcv = jnp.zeros((CONV_ROWS, cl), F32)
            for k in range(SHORT_KERNEL):
                dcv = dcv + dcpad[pl.ds(base + (SHORT_KERNEL - 1 - k), CONV_ROWS), :] * w3_ref[k:k + 1, :]
            av, sg = av_ref[rows, :], _sigmoid(ag_ref[rows, :])
            dav_ref[rows, :] = (da * sg).astype(dav_ref.dtype)
            dag_ref[rows, :] = (da * av * sg * (1.0 - sg)).astype(dag_ref.dtype)
            dcg_ref[rows, :] = (dcv * v_ref[rows, :]).astype(dcg_ref.dtype)
            dv_ref[rows, :] = (dcv * cg_ref[rows, :]).astype(dv_ref.dtype)
            d_out, d_out3 = dac_ref[rows, :], dc3_ref[rows, :]
            new = []
            for k in range(CONF_KERNEL):
                off = CONV_PAD - (CONF_KERNEL - 1) + k
                new.append(accs[k] + fold(d_out * apad[pl.ds(base + off, CONV_ROWS), :]))
            for k in range(SHORT_KERNEL):
                off = CONV_PAD - (SHORT_KERNEL - 1) + k
                new.append(accs[CONF_KERNEL + k] + fold(d_out3 * cpad[pl.ds(base + off, CONV_ROWS), :]))
            return tuple(new)

        init = tuple(jnp.zeros((8, cl), F32) for _ in range(CONF_KERNEL + SHORT_KERNEL))
        accs = lax.fori_loop(0, nchunk, chunk, init)
        for k in range(CONF_KERNEL):
            dwdw_ref[k:k + 1, :] = jnp.sum(accs[k], axis=0, keepdims=True)
        for k in range(SHORT_KERNEL):
            dw3_ref[k:k + 1, :] = jnp.sum(accs[CONF_KERNEL + k], axis=0, keepdims=True)

    def col(blk0):
        return pl.BlockSpec((t, cl), lambda j: (0, blk0 + j))

    own = pl.BlockSpec((t, cl), lambda j: (0, j))
    return pl.pallas_call(
        body,
        name="conv_bwd",
        grid=(nb,),
        in_specs=[col(0), col(nb), col(3 * nb), col(4 * nb), own, own,
                  pl.BlockSpec((CONF_KERNEL, cl), lambda j: (0, j)),
                  pl.BlockSpec((SHORT_KERNEL, cl), lambda j: (0, j))],
        out_specs=[own] * 4 + [pl.BlockSpec((CONF_KERNEL, cl), lambda j: (0, j)),
                               pl.BlockSpec((SHORT_KERNEL, cl), lambda j: (0, j))],
        out_shape=[jax.ShapeDtypeStruct((t, d_conf), MXU_DTYPE)] * 4
        + [jax.ShapeDtypeStruct((CONF_KERNEL, d_conf), F32), jax.ShapeDtypeStruct((SHORT_KERNEL, d_conf), F32)],
        scratch_shapes=[pltpu.VMEM((t + CONV_PAD, cl), F32)] * 4,
        compiler_params=pltpu.CompilerParams(dimension_semantics=("parallel",)),
    )(proj, proj, proj, proj, d_ac, d_c3, wdw, w3)


def _elementwise(name, fn, ins, out_dtypes, tr, deps=()):
    ins = [(a, ()) if not isinstance(a, tuple) else a for a in ins]
    r, c = ins[0][0].shape[-2:]
    assert r % tr == 0, (name, r, tr)
    n_in = len(ins)

    n_out = len(out_dtypes)

    def step(*refs):
        tiles = fn(*[x[...] for x in refs[:n_in]])
        for o_ref, tile in zip(refs[n_in:], tiles):
            o_ref[...] = tile.astype(o_ref.dtype)

    if r == tr:
        return pl.pallas_call(
            lambda *refs: step(*refs[:n_in], *refs[n_in + len(deps):]),
            name=name,
            in_specs=[pl.BlockSpec((None,) * len(lead) + (tr, c), lambda: (*lead, 0, 0)) for _, lead in ins]
            + [ANY_SPEC] * len(deps),
            out_specs=[pl.BlockSpec((tr, c), lambda: (0, 0))] * n_out,
            out_shape=[jax.ShapeDtypeStruct((r, c), d) for d in out_dtypes],
        )(*[a for a, _ in ins], *deps)

    def spec(lead):
        return pl.BlockSpec((None,) * len(lead) + (tr, c), lambda i: (*lead, i, 0),
                            pipeline_mode=pl.Buffered(STREAM_BUFFERS))

    def body(*refs):
        pltpu.emit_pipeline(
            step, grid=(r // tr,), in_specs=[spec(lead) for _, lead in ins],
            out_specs=[pl.BlockSpec((tr, c), lambda i: (i, 0))] * n_out,
        )(*refs[:n_in], *refs[n_in + len(deps):])

    res = pl.pallas_call(
        body,
        name=name,
        in_specs=[ANY_SPEC] * (n_in + len(deps)),
        out_specs=[ANY_SPEC] * n_out,
        out_shape=[jax.ShapeDtypeStruct((r, c), d) for d in out_dtypes],
    )(*[a for a, _ in ins], *deps)
    return res


def _adamw_tiles(w, g, m, v):
    m = ADAM_B1 * m + (1.0 - ADAM_B1) * g
    v = ADAM_B2 * v + (1.0 - ADAM_B2) * jnp.square(g)
    m_hat = m / (1.0 - ADAM_B1 ** ADAM_STEP)
    v_hat = v / (1.0 - ADAM_B2 ** ADAM_STEP)
    delta = -ADAM_LR * (m_hat / (jnp.sqrt(v_hat) + ADAM_EPS) + ADAM_WD * w)
    return g, delta, m, v


def _adamw(name, w, g, m, v, tr, deps=()):
    shape = w.shape
    flat = [a.reshape(shape[-2:]) if a.ndim > 2 else a for a in (w, g, m, v)]
    res = _elementwise(name, _adamw_tiles, flat, [F32] * 4, tr, deps)
    return [a.reshape(shape) for a in res]


def _small_adamw(name, total, entries):
    sw = total.shape[1]
    n = len(entries)

    def body(total_ref, *refs):
        ins, outs = refs[:3 * n], refs[3 * n:]
        for e, (row0, w, _, _) in enumerate(entries):
            for q in range(w.shape[1] // sw):
                cols = slice(q * sw, (q + 1) * sw)
                tiles = _adamw_tiles(ins[3 * e][:, cols], total_ref[row0 + q:row0 + q + 1, :],
                                     ins[3 * e + 1][:, cols], ins[3 * e + 2][:, cols])
                for o_ref, tile in zip(outs[4 * e:4 * e + 4], tiles):
                    o_ref[:, cols] = tile

    res = pl.pallas_call(
        body,
        name=name,
        out_shape=[jax.ShapeDtypeStruct(w.shape, F32) for _, w, _, _ in entries for _ in range(4)],
    )(total, *[a for _, w, m, v in entries for a in (w, m, v)])
    return [list(res[4 * e:4 * e + 4]) for e in range(n)]


def _pair_sum(name, p, q, core, tr):
    n_p, _, hr, c = p.shape
    assert hr % tr == 0

    def step(p_tile, q_tile, o_tile):
        o_tile[...] = (p_tile[...] + q_tile[...].astype(F32)).astype(o_tile.dtype)

    def body(core_ref, p_ref, q_ref, o_ref):
        half = core_ref[0]
        deep = pl.Buffered(STREAM_BUFFERS)
        pltpu.emit_pipeline(
            step, grid=(n_p, hr // tr),
            in_specs=[pl.BlockSpec((None, None, tr, c), lambda a, i: (a, half, i, 0), pipeline_mode=deep),
                      pl.BlockSpec((None, tr, c), lambda a, i: (a, i, 0), pipeline_mode=deep)],
            out_specs=[pl.BlockSpec((None, tr, c), lambda a, i: (a, i, 0))],
        )(p_ref, q_ref, o_ref)

    return pl.pallas_call(
        body,
        name=name,
        in_specs=[pl.BlockSpec(memory_space=pltpu.SMEM), ANY_SPEC, ANY_SPEC],
        out_specs=ANY_SPEC,
        out_shape=jax.ShapeDtypeStruct((n_p, hr, c), WIRE_DTYPE),
    )(core, p, q)


def _into_slot(name, w, slots, slot, dtype, tr, deps=()):
    r, c = w.shape
    assert r % tr == 0

    def body(slot_ref, w_ref, *rest):
        o_ref = rest[len(deps)]
        o_ref[...] = w_ref[...].astype(o_ref.dtype)

    return pl.pallas_call(
        body,
        name=name,
        grid_spec=pltpu.PrefetchScalarGridSpec(
            num_scalar_prefetch=1,
            grid=(r // tr,),
            in_specs=[pl.BlockSpec((tr, c), lambda i, slot_ref: (i, 0))] + [ANY_SPEC] * len(deps),
            out_specs=pl.BlockSpec((None, tr, c), lambda i, slot_ref: (slot_ref[0], i, 0)),
        ),
        out_shape=jax.ShapeDtypeStruct((slots, r, c), dtype),
        compiler_params=pltpu.CompilerParams(dimension_semantics=("parallel",)),
    )(slot, w, *deps)


def _sum_pieces(name, own, rb, place, tr):
    n_p, hr, c = rb.shape
    assert hr % tr == 0

    def body(place_ref, own_ref, *refs):
        chip = place_ref[0]
        acc = None
        for k in range(n_p):
            tile = jnp.where(chip == k, own_ref[...], refs[k][...]).astype(F32)
            acc = tile if acc is None else acc + tile
        refs[n_p][...] = acc

    def landed(k):
        return pl.BlockSpec((None, tr, c), lambda i, place_ref: (jnp.where(place_ref[0] == k, (k + 1) % n_p, k), i, 0))

    return pl.pallas_call(
        body,
        name=name,
        grid_spec=pltpu.PrefetchScalarGridSpec(
            num_scalar_prefetch=1,
            grid=(hr // tr,),
            in_specs=[pl.BlockSpec((None, tr, c), lambda i, place_ref: (place_ref[0], i, 0))]
            + [landed(k) for k in range(n_p)],
            out_specs=pl.BlockSpec((None, tr, c), lambda i, place_ref: (place_ref[1], i, 0)),
        ),
        out_shape=jax.ShapeDtypeStruct((2, hr, c), F32),
        compiler_params=pltpu.CompilerParams(dimension_semantics=("parallel",)),
    )(place, own, *([rb] * n_p))


def _pair_sum_group(name, ps, qs, core):
    n = len(ps)

    def body(core_ref, *refs):
        for s in range(n):
            refs[2 * n + s][...] = (refs[s][...] + refs[n + s][...].astype(F32)).astype(WIRE_DTYPE)

    def piece(shape):
        return pl.BlockSpec((None, *shape), lambda a, core_ref: (a, 0, 0))

    return pl.pallas_call(
        body,
        name=name,
        grid_spec=pltpu.PrefetchScalarGridSpec(
            num_scalar_prefetch=1,
            grid=(ps[0].shape[0],),
            in_specs=[pl.BlockSpec((None, None, *p.shape[2:]), lambda a, core_ref: (a, core_ref[0], 0, 0)) for p in ps]
            + [piece(q.shape[1:]) for q in qs],
            out_specs=[piece(q.shape[1:]) for q in qs],
        ),
        out_shape=[jax.ShapeDtypeStruct(q.shape, WIRE_DTYPE) for q in qs],
        compiler_params=pltpu.CompilerParams(dimension_semantics=("parallel",)),
    )(core, *ps, *qs)


def _sum_pieces_group(name, owns, rbs, place):
    n = len(owns)
    n_p = rbs[0].shape[0]

    def body(place_ref, *refs):
        chip = place_ref[0]
        for s in range(n):
            lands = refs[n + s * n_p:n + (s + 1) * n_p]
            acc = None
            for k in range(n_p):
                tile = jnp.where(chip == k, refs[s][...], lands[k][...]).astype(F32)
                acc = tile if acc is None else acc + tile
            refs[n + n * n_p + s][...] = acc

    def landed(k, shape):
        return pl.BlockSpec((None, *shape), lambda i, place_ref: (jnp.where(place_ref[0] == k, (k + 1) % n_p, k), 0, 0))

    return pl.pallas_call(
        body,
        name=name,
        grid_spec=pltpu.PrefetchScalarGridSpec(
            num_scalar_prefetch=1,
            grid=(1,),
            in_specs=[pl.BlockSpec((None, *o.shape[1:]), lambda i, place_ref: (place_ref[0], 0, 0)) for o in owns]
            + [landed(k, rb.shape[1:]) for rb in rbs for k in range(n_p)],
            out_specs=[pl.BlockSpec((None, *rb.shape[1:]), lambda i, place_ref: (place_ref[1], 0, 0)) for rb in rbs],
        ),
        out_shape=[jax.ShapeDtypeStruct((2, *rb.shape[1:]), F32) for rb in rbs],
        compiler_params=pltpu.CompilerParams(dimension_semantics=("arbitrary",)),
    )(place, *owns, *[rb for rb in rbs for _ in range(n_p)])


HBM_SPEC = pl.BlockSpec(memory_space=pl.ANY)


def _place():
    x, y, c = lax.axis_index("x"), lax.axis_index("y"), lax.axis_index("c")
    chips = [(1 - x, y), (x, 1 - y), (1 - x, 1 - y)]
    return x, y, c, chips


HBM_ONLY = pl.BlockSpec(memory_space=pltpu.HBM)
SEM_SPEC = pl.BlockSpec(memory_space=pltpu.SEMAPHORE)
DATAFLOW = pltpu.SideEffectType.DATAFLOW_SIDE_EFFECTING


def _in_hbm(a):
    return pltpu.with_memory_space_constraint(a, pltpu.HBM)


def _shard_part(ref, is_split, slot, h):
    if not is_split:
        return ref.at[slot]
    hr = ref.shape[1] // 2
    return ref.at[slot, pl.ds(h * hr, hr), :]


TOKEN = jax.ShapeDtypeStruct((8, LANES), F32)
VMEM_SPEC = pl.BlockSpec(memory_space=pltpu.VMEM)


def _gather_start(name, bufs, split, groups, deps=()):
    n, ng = len(bufs), len(groups)

    def body(*refs):
        ins, sems = refs[:n], refs[n + len(deps):n + len(deps) + 2 * ng]
        refs[-1][...] = jnp.zeros(TOKEN.shape, TOKEN.dtype)
        x, y, c, chips = _place()
        me = 2 * x + y
        for g, members in enumerate(groups):
            for s, a in enumerate(members):
                mine = _shard_part(ins[a], split[a], me, c)
                for j, chip in enumerate(chips):
                    pltpu.make_async_remote_copy(
                        src_ref=mine, dst_ref=mine, send_sem=sems[2 * g].at[3 * s + j], recv_sem=sems[2 * g + 1].at[3 * s + j],
                        device_id=(*chip, c), device_id_type=MESH).start()

    res = pl.pallas_call(
        body,
        name=name,
        in_specs=[HBM_ONLY] * n + [ANY_SPEC] * len(deps),
        out_specs=[SEM_SPEC] * (2 * ng) + [HBM_ONLY] * n + [VMEM_SPEC],
        out_shape=[pltpu.SemaphoreType.DMA((3 * len(members),)) for members in groups for _ in range(2)]
        + [pltpu.HBM(b.shape, b.dtype) for b in bufs] + [TOKEN],
        input_output_aliases={a: 2 * ng + a for a in range(n)},
        compiler_params=pltpu.CompilerParams(has_side_effects=DATAFLOW),
    )(*[_in_hbm(b) for b in bufs], *deps)
    return [(res[2 * g], res[2 * g + 1]) for g in range(ng)], list(res[2 * ng:2 * ng + n]), res[-1]


def _gather_wait(name, bufs, split, sems, after):
    n = len(bufs)

    def body(*refs):
        ins, send_sems, recv_sems = refs[:n], refs[n], refs[n + 1]
        x, y, c, chips = _place()
        me = 2 * x + y
        for s in range(n):
            for j, chip in enumerate(chips):
                copy = pltpu.make_async_remote_copy(
                    src_ref=_shard_part(ins[s], split[s], me, c),
                    dst_ref=_shard_part(ins[s], split[s], 2 * chip[0] + chip[1], c),
                    send_sem=send_sems.at[3 * s + j], recv_sem=recv_sems.at[3 * s + j],
                    device_id=(*chip, c), device_id_type=MESH)
                copy.wait_send()
                copy.wait_recv()

    res = pl.pallas_call(
        body,
        name=name,
        in_specs=[HBM_ONLY] * n + [SEM_SPEC, SEM_SPEC] + [ANY_SPEC] * len(after),
        out_specs=[HBM_ONLY] * n,
        out_shape=[pltpu.HBM(b.shape, b.dtype) for b in bufs],
        input_output_aliases={a: a for a in range(n)},
        compiler_params=pltpu.CompilerParams(has_side_effects=DATAFLOW),
    )(*bufs, *sems, *after)
    return list(res)


def _pass_halves(name, bufs):
    n = len(bufs)

    def body(*refs):
        outs = refs[n:2 * n]
        send_sems, recv_sems = refs[2 * n:]
        x, y, c, chips = _place()
        sibling = (x, y, 1 - c)

        def copy(a, j, h):
            blk = _shard_part(outs[a], True, 2 * chips[j][0] + chips[j][1], h)
            return pltpu.make_async_remote_copy(
                src_ref=blk, dst_ref=blk, send_sem=send_sems.at[3 * a + j], recv_sem=recv_sems.at[3 * a + j],
                device_id=sibling, device_id_type=MESH)

        sends = [copy(a, j, c) for a in range(n) for j in range(3)]
        for cp in sends:
            cp.start()
        for a in range(n):
            for j in range(3):
                copy(a, j, 1 - c).wait_recv()
        for cp in sends:
            cp.wait_send()

    res = pl.pallas_call(
        body,
        name=name,
        in_specs=[HBM_SPEC] * n,
        out_specs=[HBM_SPEC] * n,
        out_shape=[jax.ShapeDtypeStruct(b.shape, b.dtype) for b in bufs],
        input_output_aliases={a: a for a in range(n)},
        scratch_shapes=[pltpu.SemaphoreType.DMA((3 * n,)), pltpu.SemaphoreType.DMA((3 * n,))],
    )(*bufs)
    return list(res)


def _exchange_start(name, pairs):
    n = len(pairs)

    def body(*refs):
        pair_refs, land_refs, send_sems, recv_sems = refs[:n], refs[n:2 * n], refs[2 * n], refs[2 * n + 1]
        x, y, c, chips = _place()
        me = 2 * x + y
        for a in range(n):
            for j, chip in enumerate(chips):
                pltpu.make_async_remote_copy(
                    src_ref=pair_refs[a].at[2 * chip[0] + chip[1]], dst_ref=land_refs[a].at[me],
                    send_sem=send_sems.at[3 * a + j], recv_sem=recv_sems.at[3 * a + j],
                    device_id=(*chip, c), device_id_type=MESH).start()
        refs[-1][...] = jnp.zeros(TOKEN.shape, TOKEN.dtype)

    res = pl.pallas_call(
        body,
        name=name,
        in_specs=[HBM_ONLY] * (2 * n),
        out_specs=[SEM_SPEC, SEM_SPEC] + [HBM_ONLY] * (2 * n) + [VMEM_SPEC],
        out_shape=[pltpu.SemaphoreType.DMA((3 * n,)), pltpu.SemaphoreType.DMA((3 * n,))]
        + [pltpu.HBM(p.shape, p.dtype) for p in pairs] * 2 + [TOKEN],
        input_output_aliases={a: 2 + a for a in range(2 * n)},
        compiler_params=pltpu.CompilerParams(has_side_effects=DATAFLOW),
    )(*[_in_hbm(p) for p in pairs], *[_in_hbm(lax.empty(p.shape, p.dtype)) for p in pairs])
    return (res[0], res[1]), list(res[2:2 + n]), list(res[2 + n:2 + 2 * n]), res[-1]


def _exchange_wait(name, groups, after):
    sizes = [len(pairs) for _, pairs, _ in groups]
    n_buf = 2 * sum(sizes)

    def body(*refs):
        x, y, c, chips = _place()
        at_buf, at_sem = 0, n_buf
        for n in sizes:
            pair_refs, land_refs = refs[at_buf:at_buf + n], refs[at_buf + n:at_buf + 2 * n]
            send_sems, recv_sems = refs[at_sem], refs[at_sem + 1]
            at_buf, at_sem = at_buf + 2 * n, at_sem + 2
            for a in range(n):
                for j, chip in enumerate(chips):
                    k = 2 * chip[0] + chip[1]
                    copy = pltpu.make_async_remote_copy(
                        src_ref=pair_refs[a].at[k], dst_ref=land_refs[a].at[k],
                        send_sem=send_sems.at[3 * a + j], recv_sem=recv_sems.at[3 * a + j],
                        device_id=(*chip, c), device_id_type=MESH)
                    copy.wait_send()
                    copy.wait_recv()

    bufs = [b for _, pairs, lands in groups for b in (*pairs, *lands)]
    sems = [s for group_sems, _, _ in groups for s in group_sems]
    res = pl.pallas_call(
        body,
        name=name,
        in_specs=[HBM_ONLY] * n_buf + [SEM_SPEC] * len(sems) + [ANY_SPEC] * len(after),
        out_specs=[HBM_ONLY] * n_buf,
        out_shape=[pltpu.HBM(b.shape, b.dtype) for b in bufs],
        input_output_aliases={a: a for a in range(n_buf)},
        compiler_params=pltpu.CompilerParams(has_side_effects=DATAFLOW),
    )(*bufs, *sems, *after)
    out, at = [], 0
    for n in sizes:
        out.append((list(res[at:at + n]), list(res[at + n:at + 2 * n])))
        at += 2 * n
    return out


def _swap_start(name, halves):
    n = len(halves)
    n_p = halves[0].shape[0]
    land_shapes = [(n_p, *h.shape[2:]) for h in halves]

    def body(*refs):
        half_refs, land_refs, send_sems, recv_sems = refs[:n], refs[n:2 * n], refs[2 * n], refs[2 * n + 1]
        x, y, c, _ = _place()
        for a in range(n):
            for p in range(n_p):
                pltpu.make_async_remote_copy(
                    src_ref=half_refs[a].at[p, 1 - c], dst_ref=land_refs[a].at[p],
                    send_sem=send_sems.at[n_p * a + p], recv_sem=recv_sems.at[n_p * a + p],
                    device_id=(x, y, 1 - c), device_id_type=MESH).start()
        refs[-1][...] = jnp.zeros(TOKEN.shape, TOKEN.dtype)

    res = pl.pallas_call(
        body,
        name=name,
        in_specs=[HBM_ONLY] * (2 * n),
        out_specs=[SEM_SPEC, SEM_SPEC] + [HBM_ONLY] * (2 * n) + [VMEM_SPEC],
        out_shape=[pltpu.SemaphoreType.DMA((n_p * n,)), pltpu.SemaphoreType.DMA((n_p * n,))]
        + [pltpu.HBM(h.shape, h.dtype) for h in halves]
        + [pltpu.HBM(s, h.dtype) for s, h in zip(land_shapes, halves)] + [TOKEN],
        input_output_aliases={a: 2 + a for a in range(2 * n)},
        compiler_params=pltpu.CompilerParams(has_side_effects=DATAFLOW),
    )(*[_in_hbm(h) for h in halves], *[_in_hbm(lax.empty(s, h.dtype)) for s, h in zip(land_shapes, halves)])
    return (res[0], res[1]), list(res[2:2 + n]), list(res[2 + n:2 + 2 * n]), res[-1]


def _swap_wait(name, halves, lands, sems, after):
    n = len(halves)
    n_p = halves[0].shape[0]

    def body(*refs):
        half_refs, land_refs, send_sems, recv_sems = refs[:n], refs[n:2 * n], refs[2 * n], refs[2 * n + 1]
        x, y, c, _ = _place()
        for a in range(n):
            for p in range(n_p):
                copy = pltpu.make_async_remote_copy(
                    src_ref=half_refs[a].at[p, 1 - c], dst_ref=land_refs[a].at[p],
                    send_sem=send_sems.at[n_p * a + p], recv_sem=recv_sems.at[n_p * a + p],
                    device_id=(x, y, 1 - c), device_id_type=MESH)
                copy.wait_send()
                copy.wait_recv()

    res = pl.pallas_call(
        body,
        name=name,
        in_specs=[HBM_ONLY] * (2 * n) + [SEM_SPEC, SEM_SPEC] + [ANY_SPEC] * len(after),
        out_specs=[HBM_ONLY] * (2 * n),
        out_shape=[pltpu.HBM(b.shape, b.dtype) for b in (*halves, *lands)],
        input_output_aliases={a: a for a in range(2 * n)},
        compiler_params=pltpu.CompilerParams(has_side_effects=DATAFLOW),
    )(*halves, *lands, *sems, *after)
    return list(res[:n]), list(res[n:])


def _join_start(name, bufs):
    n = len(bufs)

    def body(*refs):
        ins, send_sems, recv_sems = refs[:n], refs[n], refs[n + 1]
        x, y, c, _ = _place()
        for a in range(n):
            pltpu.make_async_remote_copy(
                src_ref=ins[a].at[c], dst_ref=ins[a].at[c], send_sem=send_sems.at[a], recv_sem=recv_sems.at[a],
                device_id=(x, y, 1 - c), device_id_type=MESH).start()
        refs[-1][...] = jnp.zeros(TOKEN.shape, TOKEN.dtype)

    res = pl.pallas_call(
        body,
        name=name,
        in_specs=[HBM_ONLY] * n,
        out_specs=[SEM_SPEC, SEM_SPEC] + [HBM_ONLY] * n + [VMEM_SPEC],
        out_shape=[pltpu.SemaphoreType.DMA((n,)), pltpu.SemaphoreType.DMA((n,))]
        + [pltpu.HBM(b.shape, b.dtype) for b in bufs] + [TOKEN],
        input_output_aliases={a: 2 + a for a in range(n)},
        compiler_params=pltpu.CompilerParams(has_side_effects=DATAFLOW),
    )(*[_in_hbm(b) for b in bufs])
    return (res[0], res[1]), list(res[2:2 + n]), res[-1]


def _join_wait(name, bufs, sems, after):
    n = len(bufs)

    def body(*refs):
        ins, send_sems, recv_sems = refs[:n], refs[n], refs[n + 1]
        x, y, c, _ = _place()
        for a in range(n):
            copy = pltpu.make_async_remote_copy(
                src_ref=ins[a].at[c], dst_ref=ins[a].at[1 - c], send_sem=send_sems.at[a], recv_sem=recv_sems.at[a],
                device_id=(x, y, 1 - c), device_id_type=MESH)
            copy.wait_send()
            copy.wait_recv()

    res = pl.pallas_call(
        body,
        name=name,
        in_specs=[HBM_ONLY] * n + [SEM_SPEC, SEM_SPEC] + [ANY_SPEC] * len(after),
        out_specs=[HBM_ONLY] * n,
        out_shape=[pltpu.HBM(b.shape, b.dtype) for b in bufs],
        input_output_aliases={a: a for a in range(n)},
        compiler_params=pltpu.CompilerParams(has_side_effects=DATAFLOW),
    )(*bufs, *sems, *after)
    return list(res)


def _allgather_small(name, block, deps=()):
    m_per, n = block.shape

    def body(x_ref, *rest):
        out_ref, send_sems, recv_sems, local_sem = rest[len(deps):]
        x, y, c, chips = _place()
        me, sibling = (x, y, c), (x, y, 1 - c)

        def rows(px, py, pc):
            return out_ref.at[pl.ds((4 * px + 2 * py + pc) * m_per, m_per), :]

        def copy(k, blk, to, src=None):
            return pltpu.make_async_remote_copy(
                src_ref=rows(*blk) if src is None else src, dst_ref=rows(*blk),
                send_sem=send_sems.at[k], recv_sem=recv_sems.at[k], device_id=to, device_id_type=MESH)

        mine = pltpu.make_async_copy(x_ref, rows(*me), local_sem)
        mine.start()
        first = [copy(0, me, sibling, src=x_ref)]
        first += [copy(1 + j, me, (*chip, c), src=x_ref) for j, chip in enumerate(chips)]
        for cp in first:
            cp.start()
        passed = [copy(4 + j, (*chip, c), sibling) for j, chip in enumerate(chips)]
        for j, chip in enumerate(chips):
            copy(1 + j, (*chip, c), me).wait_recv()
            passed[j].start()
        copy(0, sibling, me).wait_recv()
        for j, chip in enumerate(chips):
            copy(4 + j, (*chip, 1 - c), me).wait_recv()
        for cp in first + passed:
            cp.wait_send()
        mine.wait()

    return pl.pallas_call(
        body,
        name=name,
        out_shape=jax.ShapeDtypeStruct((8 * m_per, n), block.dtype),
        in_specs=[pl.BlockSpec(memory_space=pltpu.VMEM)] + [ANY_SPEC] * len(deps),
        out_specs=pl.BlockSpec(memory_space=pltpu.VMEM),
        scratch_shapes=[pltpu.SemaphoreType.DMA((7,)), pltpu.SemaphoreType.DMA((7,)), pltpu.SemaphoreType.DMA],
    )(block, *deps)


def _sum_blocks(name, gathered, n_blocks):
    r = gathered.shape[0] // n_blocks
    c = gathered.shape[1]

    def body(g_ref, o_ref):
        acc = g_ref[0:r, :]
        for b in range(1, n_blocks):
            acc = acc + g_ref[b * r:(b + 1) * r, :]
        o_ref[...] = acc

    return pl.pallas_call(body, name=name, out_shape=jax.ShapeDtypeStruct((r, c), F32))(gathered)


def _largest_tile(n, cap, mult):
    best = None
    for d in range(mult, min(n, cap) + 1, mult):
        if n % d == 0:
            best = d
    assert best is not None, (n, cap, mult)
    return best


def kernel(x, meta, g_pre_mix, w_in, b_gates, conf_dw_w, conf_dw_b, conf_ln_g, conf_ln_b, conf_w_pw, short_dw_w, short_w_out, w_o, g_post_mix, g_pre_mlp, w_up, w_down, g_post_mlp, loss_target, m_meta, m_g_pre_mix, m_w_in, m_b_gates, m_conf_dw_w, m_conf_dw_b, m_conf_ln_g, m_conf_ln_b, m_conf_w_pw, m_short_dw_w, m_short_w_out, m_w_o, m_g_post_mix, m_g_pre_mlp, m_w_up, m_w_down, m_g_post_mlp, v_meta, v_g_pre_mix, v_w_in, v_b_gates, v_conf_dw_w, v_conf_dw_b, v_conf_ln_g, v_conf_ln_b, v_conf_w_pw, v_short_dw_w, v_short_w_out, v_w_o, v_g_post_mix, v_g_pre_mlp, v_w_up, v_w_down, v_g_post_mlp):
    seq, d = x.shape[1], x.shape[2]
    t_real = seq + N_META
    t = -(-t_real // LANES) * LANES
    d_conf = conf_dw_b.shape[1]
    d_ff = w_up.shape[2] * N_CHIPS
    in_cols = w_in.shape[2] * N_CHIPS
    assert in_cols == 5 * d_conf + 2 * d and d == 2 * d_conf
    cw = d_conf
    core = lax.axis_index("c")
    chip = 2 * lax.axis_index("x") + lax.axis_index("y")

    tr = _largest_tile(t, 272, ROW_CHUNK)
    tm = t
    tn_in = _largest_tile(in_cols // N_CHIPS, 768, LANES)
    tn_d = _largest_tile(d, 1024, LANES)
    tn_h = _largest_tile(d, 512, LANES)
    tn_ff = _largest_tile(d_ff // N_CHIPS, 1024, LANES)
    tn_pw = _largest_tile(d // N_CHIPS, 512, LANES)

    big = [w_in[0], conf_w_pw[0], short_w_out[0], w_o[0], w_up[0], w_down[0]]
    chip_arr = chip.astype(jnp.int32).reshape(1)

    def cast(a, deps):
        return _into_slot(f"cast_w{a}", big[a], N_CHIPS, chip_arr, MXU_DTYPE, _largest_tile(big[a].shape[0], 256, 16), deps)

    small = [_into_slot(f"place_w{a}", w, N_CHIPS, chip_arr, F32, w.shape[0])
             for a, w in enumerate([meta, conf_dw_w[0], short_dw_w[0]])]
    sems_small, fly_small, tok_small = _gather_start("gather_start_small", small, [False] * 3, [[0, 1, 2]])
    sems_in, fly_in, tok_in = _gather_start("gather_start_in", [cast(0, [tok_small])], [True], [[0]])
    rest_groups = [[0, 1], [2], [3], [4]]
    sems_rest, fly_rest, tok_rest = _gather_start(
        "gather_start_rest", [cast(a, [tok_in]) for a in range(1, 6)], [True] * 5, rest_groups)

    def arrive(g, after):
        members = rest_groups[g]
        got = _gather_wait(f"gather_wait_rest{g}", [fly_rest[a] for a in members], [True] * len(members),
                           sems_rest[g], after)
        return _pass_halves(f"gather_pass_rest{g}", got)

    meta_g, wdw_g, w3_g = _gather_wait("gather_wait_small", fly_small, [False] * 3, sems_small[0], [tok_in])
    meta_full = jnp.transpose(meta_g, (1, 0, 2)).reshape(N_META, d)
    wdw = jnp.transpose(wdw_g, (1, 0, 2)).reshape(CONF_KERNEL, d_conf)
    w3 = jnp.transpose(w3_g, (1, 0, 2)).reshape(SHORT_KERNEL, d_conf)

    tail = jnp.zeros((t - t_real, d), F32)
    h0 = jnp.concatenate([meta_full, x[0], tail], axis=0)
    target = jnp.concatenate([jnp.zeros((N_META, d), F32), loss_target[0], tail], axis=0)

    def norm_in(i, rows, vecs):
        return [_rms_fwd(rows[0], vecs[0])], []

    (n_lp,) = _rowwise("norm_in", norm_in, [(h0, d, 0)], [g_pre_mix], [(d, MXU_DTYPE)], [], tr, deps=[tok_rest])
    (wg_in,) = _pass_halves("gather_pass_in", _gather_wait("gather_wait_in", fly_in, [True], sems_in[0], [n_lp]))
    proj =_matmul("proj", n_lp, wg_in, kind="nn", tm=tm, tn=tn_in, tk=d, out_dtypes=[F32])
    ac, c3 = _conv_fwd(proj, wdw, conf_dw_b, w3, d_conf)

    def ln_parts(ac_t, ln_g, ln_b):
        mu = jnp.mean(ac_t, axis=-1, keepdims=True)
        xc = ac_t - mu
        rstd = lax.rsqrt(jnp.mean(xc * xc, axis=-1, keepdims=True) + LN_EPS)
        xh = xc * rstd
        return xh, rstd, xh * ln_g + ln_b

    def branch_act(i, rows, vecs):
        ac_t, c3_t, bg_t = rows
        _, _, al = ln_parts(ac_t, vecs[0], vecs[1])
        return [al * _sigmoid(al), bg_t * c3_t], []

    a_act, s_lp = _rowwise("branch_act", branch_act, [(ac, cw, 0), (c3, cw, 0), (proj, cw, 2)],
                           [conf_ln_g, conf_ln_b], [(d_conf, MXU_DTYPE), (d_conf, MXU_DTYPE)], [], tr)
    wg_pw, wg_sout = arrive(0, [a_act])
    y_a = _matmul("y_a", a_act, wg_pw, kind="nn", tm=tm, tn=tn_pw, tk=d_conf, out_dtypes=[F32])
    y_b = _matmul("y_b", s_lp, wg_sout, kind="nn", tm=tm, tn=tn_pw, tk=d_conf, out_dtypes=[F32])

    gate_rows = [(proj, cw, 5), (proj, cw, 6), (proj, cw, 7), (proj, cw, 8)]

    def gates_of(rows, b):
        ga = _sigmoid(jnp.concatenate([rows[0], rows[1]], axis=1) + b[:, :d])
        gb = _sigmoid(jnp.concatenate([rows[2], rows[3]], axis=1) + b[:, d:])
        return ga, gb

    def gate(i, rows, vecs):
        ga, gb = gates_of(rows[2:], vecs[0])
        return [ga * rows[0] + gb * rows[1]], []

    (m_lp,) = _rowwise("gate", gate, [(y_a, d, 0), (y_b, d, 0)] + gate_rows, [b_gates], [(d, MXU_DTYPE)], [], tr)
    wg_o = arrive(1, [m_lp])[0].reshape(d, d)
    mix = _matmul("mix", m_lp, wg_o, kind="nn", tm=tm, tn=tn_d, tk=d, out_dtypes=[F32])

    def post_mix(i, rows, vecs):
        h1_t = rows[0] + _rms_fwd(rows[1], vecs[0])
        return [h1_t, _rms_fwd(h1_t, vecs[1])], []

    h1, n2_lp = _rowwise("post_mix", post_mix, [(h0, d, 0), (mix, d, 0)], [g_post_mix, g_pre_mlp],
                         [(d, F32), (d, MXU_DTYPE)], [], tr)
    (wg_up,) = arrive(2, [n2_lp])
    up, f_lp = _matmul("up", n2_lp, wg_up, kind="nn", tm=tm, tn=tn_h, tk=d, out_dtypes=[F32, MXU_DTYPE],
                       epilogue=lambda acc: (acc, jnp.square(jnp.maximum(acc, 0.0))))
    wg_down = arrive(3, [f_lp])[0].reshape(d_ff, d)
    dn = _matmul("down", f_lp, wg_down, kind="nn", tm=tm, tn=tn_h, tk=_largest_tile(d_ff, 2048, LANES),
                 out_dtypes=[F32])

    def head(row0, rows, vecs):
        h1_t, dn_t, tgt = rows
        y = h1_t + _rms_fwd(dn_t, vecs[0])
        row = row0 + lax.broadcasted_iota(jnp.int32, (ROW_CHUNK, 1), 0)
        err = jnp.where(jnp.logical_and(row >= N_META, row < t_real), y - tgt, 0.0)
        dy = err / d
        d_dn, dg = _rms_bwd(dn_t, vecs[0], dy)
        loss_rows = 0.5 * jnp.mean(err * err, axis=-1, keepdims=True)
        return [dy, d_dn], [dg, jnp.broadcast_to(loss_rows, (ROW_CHUNK, LANES))]

    dy, d_dn, dg_post_mlp, loss_vec = _rowwise(
        "head", head, [(h1, d, 0), (dn, d, 0), (target, d, 0)], [g_post_mlp], [(d, F32), (d, MXU_DTYPE)], [d, LANES], tr)

    core_arr = core.astype(jnp.int32).reshape(1)
    place = jnp.stack([chip, core]).astype(jnp.int32)
    in_flight = {}


    def tag(members):
        return "".join(str(a) for a in members)

    own_halves = {}
    two_types = dict(out_dtypes=[F32, WIRE_DTYPE], epilogue=lambda acc: (acc, acc))

    def swap_start(members, gws):
        def in_halves(gw):
            return gw.reshape(N_CHIPS, 2, gw.shape[-2] // 2, gw.shape[-1])

        own_halves[members] = [in_halves(gw.reshape(N_CHIPS, -1, gw.shape[-1])) for gw, _ in gws]
        for_sibling = [in_halves(lp.reshape(N_CHIPS, -1, lp.shape[-1])) for _, lp in gws]
        *in_flight[members], token = _swap_start(f"swap_start{tag(members)}", for_sibling)
        return token

    def exchange_start(members, after):
        sems, halves, lands = in_flight[members]
        _, lands = _swap_wait(f"swap_wait{tag(members)}", halves, lands, sems, after)
        if len(members) > 1:
            pairs = _pair_sum_group(f"pair_sum{tag(members)}", own_halves[members], lands, core_arr)
        else:
            pairs = [_pair_sum(f"pair_sum{a}", h, q, core_arr, _largest_tile(q.shape[1], 256, 16))
                     for a, h, q in zip(members, own_halves[members], lands)]
        *in_flight[members], token = _exchange_start(f"exchange_start{tag(members)}", pairs)
        return token

    def reduce_finish(groups, after):
        waited = _exchange_wait("exchange_wait" + "_".join(tag(g) for g in groups), [in_flight[g] for g in groups], after)
        out = {}
        for g, (pairs, lands) in zip(groups, waited):
            if len(g) > 1:
                out.update(zip(g, _sum_pieces_group(f"sum_pieces{tag(g)}", pairs, lands, place)))
            else:
                out[g[0]] = _sum_pieces(f"sum_pieces{g[0]}", pairs[0], lands[0], place,
                                        _largest_tile(lands[0].shape[1], 256, 16))
        return out

    d_up = _matmul("d_up", d_dn, wg_down, kind="nt", tm=tm, tn=tn_h, tk=d, out_dtypes=[MXU_DTYPE], extras=[up],
                   epilogue=lambda acc, up_t: (acc * (2.0 * jnp.maximum(up_t, 0.0)),))
    tk_t = t
    gw_down = _matmul("gw_down", f_lp, d_dn, kind="tn", tm=_largest_tile(d_ff, 2048, LANES), tn=tn_h, tk=tk_t,
                      **two_types)
    tok = swap_start((5,), [gw_down])
    d_n2 = _matmul("d_n2", d_up, wg_up, kind="nt", tm=tm, tn=tn_h, tk=d_ff // N_CHIPS, out_dtypes=[F32], deps=[tok])
    tok = exchange_start((5,), [d_n2])
    gw_up = _matmul("gw_up", n2_lp, d_up, kind="tn", tm=_largest_tile(d, 2048, LANES), tn=tn_h, tk=tk_t,
                    out_pieces=N_CHIPS, deps=[tok], **two_types)
    tok = swap_start((4,), [gw_up])

    def bwd_mid(i, rows, vecs):
        dy_t, dn2_t, h1_t, mix_t = rows
        d_h1a, dg_pre_mlp = _rms_bwd(h1_t, vecs[1], dn2_t)
        d_h1 = dy_t + d_h1a
        d_mix, dg_post_mix = _rms_bwd(mix_t, vecs[0], d_h1)
        return [d_h1, d_mix], [dg_pre_mlp, dg_post_mix]

    d_h1, d_mix, dg_pre_mlp, dg_post_mix = _rowwise(
        "bwd_mid", bwd_mid, [(dy, d, 0), (d_n2, d, 0), (h1, d, 0), (mix, d, 0)], [g_post_mix, g_pre_mlp],
        [(d, F32), (d, MXU_DTYPE)], [d, d], tr, deps=[tok])
    d_m = _matmul("d_m", d_mix, wg_o, kind="nt", tm=tm, tn=tn_h, tk=d, out_dtypes=[F32])
    tok = exchange_start((4,), [d_m])
    gw_o = _matmul("gw_o", m_lp, d_mix, kind="tn", tm=_largest_tile(d, 2048, LANES), tn=tn_h, tk=tk_t, deps=[tok],
                   **two_types)

    def gate_bwd(i, rows, vecs):
        dm_t, ya_t, yb_t = rows[:3]
        ga, gb = gates_of(rows[3:], vecs[0])
        d_gpre = jnp.concatenate([dm_t * ya_t * ga * (1.0 - ga), dm_t * yb_t * gb * (1.0 - gb)], axis=1)
        return [dm_t * ga, dm_t * gb, d_gpre], [d_gpre]

    d_ya, d_yb, d_gpre, dg_b_gates = _rowwise(
        "gate_bwd", gate_bwd, [(d_m, d, 0), (y_a, d, 0), (y_b, d, 0)] + gate_rows, [b_gates],
        [(d, MXU_DTYPE), (d, MXU_DTYPE), (2 * d, MXU_DTYPE)], [2 * d], tr, deps=[tok])
    d_aact = _matmul("d_aact", d_ya, wg_pw, kind="nt", tm=tm, tn=d_conf // 2, tk=tn_pw, out_dtypes=[F32], deps=[tok])
    gw_pw = _matmul("gw_pw", a_act, d_ya, kind="tn", tm=_largest_tile(d_conf, 2048, LANES), tn=tn_pw, tk=tk_t,
                    out_pieces=N_CHIPS, **two_types)
    d_s = _matmul("d_s", d_yb, wg_sout, kind="nt", tm=tm, tn=d_conf // 2, tk=tn_pw, out_dtypes=[F32], deps=[tok])
    gw_sout = _matmul("gw_sout", s_lp, d_yb, kind="tn", tm=_largest_tile(d_conf, 2048, LANES), tn=tn_pw, tk=tk_t,
                      out_pieces=N_CHIPS, deps=[tok], **two_types)
    small_big = (3, 1, 2)
    tok = swap_start(small_big, [gw_o, gw_pw, gw_sout])

    def branch_bwd(i, rows, vecs):
        daact_t, ds_t, ac_t, c3_t, bg_t = rows
        xh, rstd, al = ln_parts(ac_t, vecs[0], vecs[1])
        sg = _sigmoid(al)
        d_al = daact_t * (sg * (1.0 + al * (1.0 - sg)))
        dxh = d_al * vecs[0]
        d_ac = rstd * (dxh - jnp.mean(dxh, axis=-1, keepdims=True) - xh * jnp.mean(dxh * xh, axis=-1, keepdims=True))
        return [d_ac, ds_t * bg_t, ds_t * c3_t], [d_al * xh, d_al, d_ac]

    d_ac, d_c3, d_bg, dg_ln_g, dg_ln_b, dg_dw_b = _rowwise(
        "branch_bwd", branch_bwd, [(d_aact, cw, 0), (d_s, cw, 0), (ac, cw, 0), (c3, cw, 0), (proj, cw, 2)],
        [conf_ln_g, conf_ln_b], [(d_conf, F32), (d_conf, F32), (d_conf, MXU_DTYPE)], [d_conf] * 3, tr, deps=[tok])
    d_av, d_ag, d_cg, d_v, dg_wdw, dg_w3 = _conv_bwd(proj, d_ac, d_c3, wdw, w3, d_conf)
    d_proj = jnp.concatenate([d_av, d_ag, d_bg, d_cg, d_v, d_gpre], axis=1)
    small_w = d_conf

    def pack(arrs):
        flat = jnp.concatenate([a.reshape(-1, small_w) for a in arrs], axis=0)
        return jnp.pad(flat, ((0, -flat.shape[0] % 8), (0, 0)))

    def reduce_small(tag, arrs, deps):
        assert all(a.size % small_w == 0 for a in arrs)
        gathered = _allgather_small(f"allgather_small_{tag}", pack(arrs), deps)
        first, r0 = [], 0
        for a in arrs:
            first.append(r0)
            r0 += a.size // small_w
        return _sum_blocks(f"sum_small_{tag}", gathered, 8), first

    def rows_of(total, r0, like):
        return total[r0:r0 + like.size // small_w].reshape(like.shape)

    tok = exchange_start(small_big, [d_proj])

    gw_in = _matmul("gw_in", n_lp, d_proj, kind="tn", tm=_largest_tile(d, 2048, LANES), tn=tn_in, tk=tk_t,
                    out_pieces=N_CHIPS, deps=[tok], **two_types)
    tok = swap_start((0,), [gw_in])
    d_n = _matmul("d_n", d_proj, wg_in, kind="nt", tm=tm, tn=tn_h, tk=in_cols // N_CHIPS, out_dtypes=[F32], deps=[tok])
    tok = exchange_start((0,), [d_n])

    early = [5, 4, 3, 1, 2]
    reduced = reduce_finish([(5,), (4,), small_big], [tok])
    join_sems, joining, tok = _join_start("join_start_early", [reduced[a] for a in early])

    def bwd_in(i, rows, vecs):
        d_h0a, dg = _rms_bwd(rows[2], vecs[0], rows[1])
        return [rows[0] + d_h0a], [dg]

    d_h0, dg_pre_mix = _rowwise("bwd_in", bwd_in, [(d_h1, d, 0), (d_n, d, 0), (h0, d, 0)], [g_pre_mix],
                                [(d, F32)], [d], tr, deps=[tok])
    grad_x = d_h0[N_META:t_real][None]

    big_m = [m_w_in, m_conf_w_pw, m_short_w_out, m_w_o, m_w_up, m_w_down]
    big_v = [v_w_in, v_conf_w_pw, v_short_w_out, v_w_o, v_w_up, v_w_down]
    big_res = {}

    def adam_group(members, joined):
        for a, j in zip(members, joined):
            big_res[a] = _adamw(f"adamw_big{a}", big[a], j.reshape(big[a].shape), big_m[a][0], big_v[a][0],
                                _largest_tile(big[a].shape[0], 256, 8))

    adam_group(early, _join_wait("join_wait_early", joining, join_sems, [d_h0]))
    reduced = reduce_finish([(0,)], [big_res[a][1] for a in early])
    join_sems, joining, tok = _join_start("join_start_late", [reduced[0]])
    rep_g = [dg_pre_mix, dg_b_gates, dg_dw_b, dg_ln_g, dg_ln_b, dg_post_mix, dg_pre_mlp, dg_post_mlp]
    rep_w = [g_pre_mix, b_gates, conf_dw_b, conf_ln_g, conf_ln_b, g_post_mix, g_pre_mlp, g_post_mlp]
    rep_m = [m_g_pre_mix, m_b_gates, m_conf_dw_b, m_conf_ln_g, m_conf_ln_b, m_g_post_mix, m_g_pre_mlp, m_g_post_mlp]
    rep_v = [v_g_pre_mix, v_b_gates, v_conf_dw_b, v_conf_ln_g, v_conf_ln_b, v_g_post_mix, v_g_pre_mlp, v_g_post_mlp]
    col_g = [d_h0[:N_META], dg_wdw, dg_w3]
    loss_row = jnp.tile(loss_vec, (1, small_w // LANES))
    total, first = reduce_small("all", rep_g + col_g + [loss_row], [tok])
    loss = total[first[-1], 0]
    rep_res = _small_adamw("adamw_rep", total, [(first[e], w, m, v) for e, (w, m, v) in enumerate(zip(rep_w, rep_m, rep_v))])
    col_res = {}
    for a, (w, g_part, m, v) in enumerate([(meta, col_g[0], m_meta, v_meta), (conf_dw_w[0], col_g[1], m_conf_dw_w[0], v_conf_dw_w[0]),
                                           (short_dw_w[0], col_g[2], m_short_dw_w[0], v_short_dw_w[0])]):
        g_full = rows_of(total, first[len(rep_g) + a], g_part)
        g_own = lax.dynamic_slice_in_dim(g_full, chip * w.shape[1], w.shape[1], axis=1)
        col_res[a] = _adamw(f"adamw_col{a}", w, g_own, m, v, w.shape[0])
    adam_group([0], _join_wait("join_wait_late", joining, join_sems, [rep_res[0][1]] + [col_res[a][1] for a in range(3)]))

    def leaf(q):
        r = lambda a: rep_res[a][q]
        b = lambda a: big_res[a][q][None]
        return [col_res[0][q], r(0), b(0), r(1), col_res[1][q][None], r(2), r(3), r(4), b(1), col_res[2][q][None], b(2),
                b(3), r(5), r(6), b(4), b(5), r(7)]

    return (loss, grad_x, *leaf(0), *leaf(1), *leaf(2), *leaf(3))
```
